```python
import math, functools
import jax, jax.numpy as jnp
from jax import lax
import numpy as np

D_MODEL = 2048
BATCH = 4
SEQ = 4096
DEPTH = 2

CTX_LEN = 256
GRID_W = 64
N_BRANCH = 4
BR_WIDTH = D_MODEL // N_BRANCH
HEAD_DIM = 128
N_HEADS = BR_WIDTH // HEAD_DIM
S5_GROUP = 16
S5_GROUPS = BR_WIDTH // S5_GROUP
S5_STATE = 64
S5_DT_MIN = 1e-3
S5_DT_MAX = 1e-1
D_FF = 4 * D_MODEL
CHUNK = 64
CONV_K = 3
NORM_EPS = 1e-6
NEG_BIG = -1e30
F_TINY = 1e-30
RET_DECAY_EXP0 = 5.0
MLSTM_F_BIAS_LO = 3.0
MLSTM_F_BIAS_HI = 6.0
N_MOD = 6

IN_SPLITS = (
    BR_WIDTH, BR_WIDTH, BR_WIDTH, BR_WIDTH, BR_WIDTH,
    BR_WIDTH, BR_WIDTH, BR_WIDTH, BR_WIDTH,
    BR_WIDTH,
    2 * BR_WIDTH, BR_WIDTH, BR_WIDTH, 4 * N_HEADS,
    N_BRANCH * D_MODEL,
)
IN_WIDTH = sum(IN_SPLITS)

kernel_name = "hybrid_gated_recurrent_diffusion_block"


def _split_cols(p):
    idx = np.cumsum(np.array(IN_SPLITS))[:-1].tolist()
    return jnp.split(p, idx, axis=-1)


def rmsnorm(x, g):
    xf = x.astype(jnp.float32)
    y = xf * lax.rsqrt(jnp.mean(jnp.square(xf), axis=-1, keepdims=True) + NORM_EPS)
    return (y * g.astype(jnp.float32)).astype(x.dtype)


def head_rmsnorm(o, g):
    b, l, h, d = o.shape
    return rmsnorm(o, g.reshape(h, d)).reshape(b, l, h * d)


def heads(t):
    return t.reshape(t.shape[:2] + (N_HEADS, HEAD_DIM))


def modulate(h, shift, scale):
    return h * (1.0 + scale) + shift


def _chunks(t):
    b, l, h = t.shape[:3]
    t = t.astype(jnp.float32).reshape((b, l // CHUNK, CHUNK, h) + t.shape[3:])
    return jnp.moveaxis(t, (1, 2), (0, 3))


def _unchunk(t, dtype):
    n, b, h, c = t.shape[:4]
    t = jnp.moveaxis(t, (0, 3), (1, 2))
    return t.reshape((b, n * c, h) + t.shape[4:]).astype(dtype)


def hgrn2_forget(z, lb):
    zf = z.astype(jnp.float32)
    lbf = lb.astype(jnp.float32)
    f = lbf + (1.0 - lbf) * jax.nn.sigmoid(zf)
    log_f = jnp.log(jnp.maximum(f, F_TINY))
    k = (1.0 - lbf) * jax.nn.sigmoid(-zf)
    return log_f, k


def hgrn2_scan(q, k, v, log_f, state):
    dtype = q.dtype
    tril = jnp.tril(jnp.ones((CHUNK, CHUNK), dtype=bool))[:, :, None]

    def step(s, blk):
        qb, kb, vb, fb = blk
        cum = jnp.cumsum(fb, axis=2)
        diff = cum[:, :, :, None, :] - cum[:, :, None, :, :]
        decay = jnp.where(tril, jnp.exp(jnp.where(tril, diff, 0.0)), 0.0)
        scores = jnp.einsum("bhtd,bhsd,bhtsd->bhts", qb, kb, decay)
        out = (jnp.einsum("bhts,bhsv->bhtv", scores, vb)
               + jnp.einsum("bhtd,bhdv->bhtv", qb * jnp.exp(cum), s))
        tail = cum[:, :, -1:, :]
        s = (jnp.exp(tail[:, :, 0, :, None]) * s
             + jnp.einsum("bhsd,bhsv->bhdv", kb * jnp.exp(tail - cum), vb))
        return s, out

    state, out = lax.scan(step, state, (_chunks(q), _chunks(k), _chunks(v), _chunks(log_f)))
    return _unchunk(out, dtype), state


def retention_scan(q, k, v, state, *, log_gamma):
    dtype = q.dtype
    lg = log_gamma.astype(jnp.float32)[:, None]
    t = jnp.arange(CHUNK, dtype=jnp.float32)
    rel = t[:, None] - t[None, :]
    intra = jnp.where(rel >= 0, jnp.exp(lg[:, :, None] * jnp.maximum(rel, 0.0)), 0.0)
    from_state = jnp.exp(lg * (t + 1.0))
    into_state = jnp.exp(lg * (CHUNK - 1.0 - t))
    chunk_decay = jnp.exp(lg * CHUNK)

    def step(s, blk):
        qb, kb, vb = blk
        scores = jnp.einsum("bhtd,bhsd->bhts", qb, kb) * intra
        out = (jnp.einsum("bhts,bhsv->bhtv", scores, vb)
               + from_state[:, :, None] * jnp.einsum("bhtd,bhdv->bhtv", qb, s))
        s = (chunk_decay[:, :, None] * s
             + jnp.einsum("bhsd,bhsv->bhdv", kb * into_state[:, :, None], vb))
        return s, out

    state, out = lax.scan(step, state, (_chunks(q), _chunks(k), _chunks(v)))
    return _unchunk(out, dtype), state


def mlstm_scan(q, k, v, log_i, log_f, state):
    dtype = q.dtype
    tril = jnp.tril(jnp.ones((CHUNK, CHUNK), dtype=bool))

    def step(carry, blk):
        cmat, nvec, m = carry
        qb, kb, vb, ib, fb = blk
        cum = jnp.cumsum(fb, axis=-1)
        logw = jnp.where(tril, cum[..., :, None] - cum[..., None, :] + ib[..., None, :], NEG_BIG)
        from_state = cum + m[..., None]
        m_t = jnp.maximum(from_state, jnp.max(logw, axis=-1))
        w = jnp.exp(logw - m_t[..., None])
        w_state = jnp.exp(from_state - m_t)
        scores = jnp.einsum("bhtd,bhsd->bhts", qb, kb) * w
        num = (jnp.einsum("bhts,bhsv->bhtv", scores, vb)
               + w_state[..., None] * jnp.einsum("bhtd,bhdv->bhtv", qb, cmat))
        den = jnp.sum(scores, axis=-1) + w_state * jnp.einsum("bhtd,bhd->bht", qb, nvec)
        h = num / jnp.maximum(jnp.abs(den), jnp.exp(-m_t))[..., None]
        total = cum[..., -1]
        logw_end = total[..., None] - cum + ib
        m_new = jnp.maximum(total + m, jnp.max(logw_end, axis=-1))
        w_end = jnp.exp(logw_end - m_new[..., None])
        keep = jnp.exp(total + m - m_new)
        cmat = keep[..., None, None] * cmat + jnp.einsum("bhs,bhsd,bhsv->bhdv", w_end, kb, vb)
        nvec = keep[..., None] * nvec + jnp.einsum("bhs,bhsd->bhd", w_end, kb)
        return (cmat, nvec, m_new), h

    state, out = lax.scan(step, state, (_chunks(q), _chunks(k), _chunks(v), _chunks(log_i), _chunks(log_f)))
    return _unchunk(out, dtype), state


def s5_discretize(a_re, a_im, log_dt, b_re, b_im, c_re, c_im):
    a_re = a_re.astype(jnp.float32)
    a_im = a_im.astype(jnp.float32)
    dt = jnp.exp(log_dt.astype(jnp.float32))[:, None]
    mag = jnp.exp(a_re * dt)
    lam_re = mag * jnp.cos(a_im * dt)
    lam_im = mag * jnp.sin(a_im * dt)
    den = a_re * a_re + a_im * a_im
    num_re, num_im = lam_re - 1.0, lam_im
    fr = (num_re * a_re + num_im * a_im) / den
    fi = (num_im * a_re - num_re * a_im) / den
    b_re = b_re.astype(jnp.float32)
    b_im = b_im.astype(jnp.float32)
    bb_re = fr[..., None] * b_re - fi[..., None] * b_im
    bb_im = fr[..., None] * b_im + fi[..., None] * b_re
    return dict(lam_re=lam_re, lam_im=lam_im, bb_re=bb_re, bb_im=bb_im,
                c_re=c_re.astype(jnp.float32), c_im=c_im.astype(jnp.float32))


def _complex_affine_combine(e1, e2):
    a1r, a1i, b1r, b1i = e1
    a2r, a2i, b2r, b2i = e2
    return (a1r * a2r - a1i * a2i, a1r * a2i + a1i * a2r,
            a2r * b1r - a2i * b1i + b2r, a2r * b1i + a2i * b1r + b2i)


def s5_scan(u, state, *, lam_re, lam_im, bb_re, bb_im, c_re, c_im):
    dtype = u.dtype
    b, l, _ = u.shape
    ug = jnp.swapaxes(u.astype(jnp.float32).reshape(b, l, S5_GROUPS, S5_GROUP), 0, 1)
    bu_re = jnp.einsum("lbgh,gph->lbgp", ug, bb_re)
    bu_im = jnp.einsum("lbgh,gph->lbgp", ug, bb_im)
    x0_re, x0_im = state
    bu_re = bu_re.at[0].add(lam_re * x0_re - lam_im * x0_im)
    bu_im = bu_im.at[0].add(lam_re * x0_im + lam_im * x0_re)
    a_re = jnp.broadcast_to(lam_re, (l, 1) + lam_re.shape)
    a_im = jnp.broadcast_to(lam_im, (l, 1) + lam_im.shape)
    _, _, x_re, x_im = lax.associative_scan(_complex_affine_combine, (a_re, a_im, bu_re, bu_im), axis=0)
    y = jnp.einsum("lbgp,ghp->blgh", x_re, c_re) - jnp.einsum("lbgp,ghp->blgh", x_im, c_im)
    return y.reshape(b, l, BR_WIDTH).astype(dtype), (x_re[-1], x_im[-1])


def bidirectional(scan_fwd, scan_bwd, ctx_fwd, lat_fwd, ctx_bwd, lat_bwd, state0):
    oc_f, st = scan_fwd(*ctx_fwd, state0)
    ox_f, _ = scan_fwd(*lat_fwd, st)
    rev = lambda ts: [jnp.flip(t, axis=1) for t in ts]
    oc_b, st = scan_bwd(*rev(ctx_bwd), state0)
    ox_b, _ = scan_bwd(*rev(lat_bwd), st)
    return oc_f + jnp.flip(oc_b, axis=1), ox_f + jnp.flip(ox_b, axis=1)


def grid_dwconv(t, w, bias, rows, cols):
    b, l, ch = t.shape
    img = t.reshape(b, rows, cols, ch)
    out = lax.conv_general_dilated(img, w[:, :, None, :].astype(t.dtype), (1, 1), "SAME",
                                   dimension_numbers=("NHWC", "HWIO", "NHWC"),
                                   feature_group_count=ch)
    return (out + bias).reshape(b, l, ch)


def gated_merge(branch_outs, gate_pre, w_branch, w_out):
    g = gate_pre.reshape(gate_pre.shape[:2] + (N_BRANCH, D_MODEL))
    y = jax.nn.sigmoid(g[:, :, 0]) * (branch_outs[0] @ w_branch[0])
    for j in range(1, N_BRANCH):
        y = y + jax.nn.sigmoid(g[:, :, j]) * (branch_outs[j] @ w_branch[j])
    return y @ w_out


def token_mixer(hc, hx, lp, lb, with_ctx):
    bsz = hx.shape[0]
    rows = hx.shape[1] // GRID_W
    cc = _split_cols(hc @ lp["w_in"])
    cx = _split_cols(hx @ lp["w_in"])
    zeros_s = jnp.zeros((bsz, N_HEADS, HEAD_DIM, HEAD_DIM), jnp.float32)

    def hgrn_in(cols, d):
        log_f, k = hgrn2_forget(cols[1 + d], lb[d])
        return (heads(jax.nn.silu(cols[0])), heads(k), heads(cols[3]), heads(log_f))
    raw_a = bidirectional(hgrn2_scan, hgrn2_scan, hgrn_in(cc, 0), hgrn_in(cx, 0),
                          hgrn_in(cc, 1), hgrn_in(cx, 1), zeros_s)

    log_gamma = jnp.log1p(-jnp.exp(lp["ret_decay"].astype(jnp.float32)))
    ret_in = lambda cols: (heads(cols[5]), heads(cols[6]) * HEAD_DIM ** -0.5, heads(cols[7]))
    raw_b = bidirectional(functools.partial(retention_scan, log_gamma=log_gamma[0]),
                          functools.partial(retention_scan, log_gamma=log_gamma[1]),
                          ret_in(cc), ret_in(cx), ret_in(cc), ret_in(cx), zeros_s)

    s5_fns = [functools.partial(s5_scan, **s5_discretize(
        lp["s5_a_re"][d], lp["s5_a_im"][d], lp["s5_log_dt"][d], lp["s5_b_re"][d],
        lp["s5_b_im"][d], lp["s5_c_re"][d], lp["s5_c_im"][d])) for d in (0, 1)]
    x0 = (jnp.zeros((bsz, S5_GROUPS, S5_STATE), jnp.float32),
          jnp.zeros((bsz, S5_GROUPS, S5_STATE), jnp.float32))
    raw_c = bidirectional(s5_fns[0], s5_fns[1], (cc[9],), (cx[9],), (cc[9],), (cx[9],), x0)

    def mlstm_in(cols, n_rows, n_cols):
        qk = jax.nn.silu(grid_dwconv(cols[10], lp["mlstm_conv_w"], lp["mlstm_conv_b"], n_rows, n_cols))
        q, k = jnp.split(qk, 2, axis=-1)
        g = (cols[13] + lp["mlstm_gate_b"].reshape(-1)).astype(jnp.float32)
        g = g.reshape(g.shape[:2] + (2, 2, N_HEADS))
        base = (heads(q), heads(k) * HEAD_DIM ** -0.5, heads(cols[11]))
        return [base + (g[:, :, d, 0], jax.nn.log_sigmoid(g[:, :, d, 1])) for d in (0, 1)]
    mc = mlstm_in(cc, 1, hc.shape[1])
    mx = mlstm_in(cx, rows, GRID_W)
    st0 = (zeros_s, jnp.zeros((bsz, N_HEADS, HEAD_DIM), jnp.float32),
           jnp.zeros((bsz, N_HEADS), jnp.float32))
    raw_d = bidirectional(mlstm_scan, mlstm_scan, mc[0], mx[0], mc[1], mx[1], st0)

    def finish(cols, s):
        oa = head_rmsnorm(raw_a[s], lp["hgrn_norm"]) * jax.nn.silu(cols[4])
        ob = head_rmsnorm(raw_b[s], lp["ret_norm"]) * jax.nn.silu(cols[8])
        yc = jax.nn.gelu(raw_c[s] + lp["s5_d"] * cols[9], approximate=False)
        oc = yc * jax.nn.sigmoid(yc @ lp["s5_glu_w"] + lp["s5_glu_b"])
        od = head_rmsnorm(raw_d[s], lp["mlstm_norm"]) * jax.nn.silu(cols[12])
        return gated_merge((oa, ob, oc, od), cols[14], lp["w_branch"], lp["w_out"])

    out_x = finish(cx, 1)
    out_c = finish(cc, 0) if with_ctx else None
    return out_c, out_x


def sq_relu_mlp(h, w1, w2):
    return jnp.square(jax.nn.relu(h @ w1)) @ w2


def trunk_layer(xc, xx, c, c_ctx, lp, lb, with_ctx):
    mod_x = jnp.split((jax.nn.silu(c) @ lp["w_mod"] + lp["b_mod"])[:, None, :], N_MOD, axis=-1)
    mod_c = jnp.split((jax.nn.silu(c_ctx)[None] @ lp["w_mod"] + lp["b_mod"])[:, None, :], N_MOD, axis=-1)
    hx = modulate(rmsnorm(xx, lp["norm_mix"]), mod_x[0], mod_x[1])
    hc = modulate(rmsnorm(xc, lp["norm_mix"]), mod_c[0], mod_c[1])
    mix_c, mix_x = token_mixer(hc, hx, lp, lb, with_ctx)
    xx = xx + mod_x[2] * mix_x
    xx = xx + mod_x[5] * sq_relu_mlp(modulate(rmsnorm(xx, lp["norm_mlp"]), mod_x[3], mod_x[4]),
                                     lp["w_ff1"], lp["w_ff2"])
    if with_ctx:
        xc = xc + mod_c[2] * mix_c
        xc = xc + mod_c[5] * sq_relu_mlp(modulate(rmsnorm(xc, lp["norm_mlp"]), mod_c[3], mod_c[4]),
                                         lp["w_ff1"], lp["w_ff2"])
    return xc, xx


def setup_inputs(seed: int = 0) -> dict:
    key = jax.random.key(seed)
    keys = iter(jax.random.split(key, 48))
    f32 = jnp.float32
    nrm = lambda shape, scale: scale * jax.random.normal(next(keys), shape, f32)
    L = DEPTH
    G, P, Hg = S5_GROUPS, S5_STATE, S5_GROUP
    x = nrm((BATCH, SEQ, D_MODEL), 1.0)
    c = nrm((BATCH, D_MODEL), 1.0)
    ctx = nrm((BATCH, CTX_LEN, D_MODEL), 1.0)
    c_ctx = nrm((D_MODEL,), 1.0)
    w_mod = nrm((L, D_MODEL, N_MOD * D_MODEL), 0.5 * D_MODEL ** -0.5)
    b_mod = nrm((L, N_MOD * D_MODEL), 0.02)
    norm_mix = 1.0 + nrm((L, D_MODEL), 0.02)
    norm_mlp = 1.0 + nrm((L, D_MODEL), 0.02)
    w_in = nrm((L, D_MODEL, IN_WIDTH), D_MODEL ** -0.5)
    hgrn_lb_logits = nrm((L, 2, BR_WIDTH), 0.5)
    hgrn_norm = 1.0 + nrm((L, BR_WIDTH), 0.02)
    ret_base = -(RET_DECAY_EXP0 + jnp.arange(N_HEADS, dtype=f32)) * math.log(2.0)
    ret_decay = ret_base + nrm((L, 2, N_HEADS), 0.05)
    ret_norm = 1.0 + nrm((L, BR_WIDTH), 0.02)
    s5_a_re = -0.5 * jnp.exp(nrm((L, 2, G, P), 0.05))
    s5_a_im = math.pi * jnp.arange(P, dtype=f32) + nrm((L, 2, G, P), 0.01)
    u = jax.random.uniform(next(keys), (L, 2, G), f32)
    s5_log_dt = math.log(S5_DT_MIN) + u * (math.log(S5_DT_MAX) - math.log(S5_DT_MIN))
    s5_b_re = nrm((L, 2, G, P, Hg), (2 * Hg) ** -0.5)
    s5_b_im = nrm((L, 2, G, P, Hg), (2 * Hg) ** -0.5)
    s5_c_re = nrm((L, 2, G, Hg, P), (2 * P) ** -0.5)
    s5_c_im = nrm((L, 2, G, Hg, P), (2 * P) ** -0.5)
    s5_d = nrm((L, BR_WIDTH), 0.5)
    s5_glu_w = nrm((L, BR_WIDTH, BR_WIDTH), BR_WIDTH ** -0.5)
    s5_glu_b = nrm((L, BR_WIDTH), 0.02)
    mlstm_conv_w = nrm((L, CONV_K, CONV_K, 2 * BR_WIDTH), 1.0 / CONV_K)
    mlstm_conv_b = nrm((L, 2 * BR_WIDTH), 0.02)
    i_bias = nrm((L, 2, N_HEADS), 0.1)
    f_bias = jnp.linspace(MLSTM_F_BIAS_LO, MLSTM_F_BIAS_HI, N_HEADS, dtype=f32) + nrm((L, 2, N_HEADS), 0.1)
    mlstm_gate_b = jnp.stack([i_bias, f_bias], axis=2)
    mlstm_norm = 1.0 + nrm((L, BR_WIDTH), 0.02)
    w_branch = nrm((L, N_BRANCH, BR_WIDTH, D_MODEL), BR_WIDTH ** -0.5)
    w_out = nrm((L, D_MODEL, D_MODEL), D_MODEL ** -0.5)
    w_ff1 = nrm((L, D_MODEL, D_FF), D_MODEL ** -0.5)
    w_ff2 = nrm((L, D_FF, D_MODEL), D_FF ** -0.5)
    final_norm = 1.0 + nrm((D_MODEL,), 0.02)
    return {"x": x, "c": c, "ctx": ctx, "c_ctx": c_ctx, "w_mod": w_mod, "b_mod": b_mod,
            "norm_mix": norm_mix, "norm_mlp": norm_mlp, "w_in": w_in,
            "hgrn_lb_logits": hgrn_lb_logits, "hgrn_norm": hgrn_norm,
            "ret_decay": ret_decay, "ret_norm": ret_norm,
            "s5_a_re": s5_a_re, "s5_a_im": s5_a_im, "s5_log_dt": s5_log_dt,
            "s5_b_re": s5_b_re, "s5_b_im": s5_b_im, "s5_c_re": s5_c_re, "s5_c_im": s5_c_im,
            "s5_d": s5_d, "s5_glu_w": s5_glu_w, "s5_glu_b": s5_glu_b,
            "mlstm_conv_w": mlstm_conv_w, "mlstm_conv_b": mlstm_conv_b,
            "mlstm_gate_b": mlstm_gate_b, "mlstm_norm": mlstm_norm,
            "w_branch": w_branch, "w_out": w_out, "w_ff1": w_ff1, "w_ff2": w_ff2,
            "final_norm": final_norm}


def reference(x, c, ctx, c_ctx, w_mod, b_mod, norm_mix, norm_mlp, w_in, hgrn_lb_logits, hgrn_norm,
              ret_decay, ret_norm, s5_a_re, s5_a_im, s5_log_dt, s5_b_re, s5_b_im, s5_c_re, s5_c_im,
              s5_d, s5_glu_w, s5_glu_b, mlstm_conv_w, mlstm_conv_b, mlstm_gate_b, mlstm_norm,
              w_branch, w_out, w_ff1, w_ff2, final_norm):
    p_lb = jax.nn.softmax(hgrn_lb_logits.astype(jnp.float32), axis=0)
    lower_bounds = jnp.cumsum(p_lb, axis=0) - p_lb[0]
    hc, hx = ctx, x
    for l in range(DEPTH):
        lp = dict(w_mod=w_mod[l], b_mod=b_mod[l], norm_mix=norm_mix[l], norm_mlp=norm_mlp[l],
                  w_in=w_in[l], hgrn_norm=hgrn_norm[l], ret_decay=ret_decay[l], ret_norm=ret_norm[l],
                  s5_a_re=s5_a_re[l], s5_a_im=s5_a_im[l], s5_log_dt=s5_log_dt[l],
                  s5_b_re=s5_b_re[l], s5_b_im=s5_b_im[l], s5_c_re=s5_c_re[l], s5_c_im=s5_c_im[l],
                  s5_d=s5_d[l], s5_glu_w=s5_glu_w[l], s5_glu_b=s5_glu_b[l],
                  mlstm_conv_w=mlstm_conv_w[l], mlstm_conv_b=mlstm_conv_b[l],
                  mlstm_gate_b=mlstm_gate_b[l], mlstm_norm=mlstm_norm[l],
                  w_branch=w_branch[l], w_out=w_out[l], w_ff1=w_ff1[l], w_ff2=w_ff2[l])
        hc, hx = trunk_layer(hc, hx, c, c_ctx, lp, lower_bounds[l], l < DEPTH - 1)
    return rmsnorm(hx, final_norm)
```

```python
import functools
import math

import jax
import jax.numpy as jnp
from jax import lax
from jax.experimental import pallas as pl
from jax.experimental.pallas import tpu as pltpu

F32 = jnp.float32
BF16 = jnp.bfloat16

D_MODEL = 2048
N_BRANCH = 4
BR = D_MODEL // N_BRANCH
HD = 128
NH = BR // HD
S5_HG = 16
S5_G = BR // S5_HG
S5_P = 64
D_FF = 4 * D_MODEL
N_MOD = 6
GRID_W = 64
NORM_EPS = 1e-6
NEG_BIG = -1e30
F_TINY = 1e-30
S5_DT_MIN = 1e-3

COL_MERGE = 0
COL_HQ, COL_HF, COL_HV, COL_HG = 16, 17, 19, 20
COL_RQ, COL_RK, COL_RV, COL_RG = 21, 22, 23, 24
COL_SU = 25
COL_MQK, COL_MV, COL_MZ = 26, 28, 29
P_WIDTH = 30 * BR
GATE_OFF = 14 * BR
GATE_PAD = 128

T_SCAN = 64
HGRN_SUB = 16
S5_T = 64
S5_FLAT = S5_T * S5_HG
S5_BPAD = 8
VMEM_LIMIT = 56 * 1024 * 1024

_HI = lax.Precision.HIGHEST


def _cparams(sem):
    return pltpu.CompilerParams(dimension_semantics=sem, vmem_limit_bytes=VMEM_LIMIT)


def _dot(a, b):
    return jnp.dot(a, b, preferred_element_type=F32)


def _dot_nt(a, b):
    return lax.dot_general(a, b, (((1,), (1,)), ((), ())), preferred_element_type=F32)


def _dot_tn(a, b):
    return lax.dot_general(a, b, (((0,), (0,)), ((), ())), preferred_element_type=F32)


def _sigmoid(x):
    return 1.0 / (1.0 + jnp.exp(-x))


def _silu(x):
    return x * _sigmoid(x)


def _mod_kernel(c_ref, w_ref, b_ref, o_ref):
    a = _silu(c_ref[...]).astype(BF16)
    o_ref[...] = _dot(a, w_ref[...].astype(BF16)) + b_ref[...]


def _modulation(c_pad, w_mod, b_mod):
    n = w_mod.shape[1]
    tn = 1024
    return pl.pallas_call(
        _mod_kernel,
        out_shape=jax.ShapeDtypeStruct((8, n), F32),
        grid=(n // tn,),
        in_specs=[pl.BlockSpec((8, D_MODEL), lambda j: (0, 0)),
                  pl.BlockSpec((D_MODEL, tn), lambda j: (0, j)),
                  pl.BlockSpec((1, tn), lambda j: (0, j))],
        out_specs=pl.BlockSpec((8, tn), lambda j: (0, j)),
        compiler_params=_cparams(("parallel",)),
        name="modulation",
    )(c_pad, w_mod, b_mod.reshape(1, n))


def _norm_mod(x, g, shift, scale):
    y = x * lax.rsqrt(jnp.mean(jnp.square(x), axis=-1, keepdims=True) + NORM_EPS)
    return (y * g) * (1.0 + scale) + shift


def _inproj_kernel(x_ref, mod_ref, g_ref, w_ref, wg_ref, p_ref, gate_ref, xn_ref):
    @pl.when(pl.program_id(1) == 0)
    def _():
        h = _norm_mod(x_ref[...], g_ref[...], mod_ref[0:1, :], mod_ref[1:2, :])
        xn_ref[...] = h.astype(BF16)
        gate_ref[...] = _dot(xn_ref[...], wg_ref[...])

    p_ref[...] = _dot(xn_ref[...], w_ref[...]).astype(p_ref.dtype)


def _mod_row_map(n_lat_tiles, tiles_per_batch, n_batch):
    def row(i):
        return jnp.where(i < n_lat_tiles, i // tiles_per_batch, n_batch)
    return row


def _in_proj(x_all, mod, norm_w, w_main, w_gate, *, n_batch, seq, tm=512, tn=512):
    n = x_all.shape[0]
    mrow = _mod_row_map(n_batch * seq // tm, seq // tm, n_batch)
    return pl.pallas_call(
        _inproj_kernel,
        out_shape=(jax.ShapeDtypeStruct((n, P_WIDTH), BF16),
                   jax.ShapeDtypeStruct((n, GATE_PAD), F32)),
        grid=(n // tm, P_WIDTH // tn),
        in_specs=[pl.BlockSpec((tm, D_MODEL), lambda i, j: (i, 0)),
                  pl.BlockSpec((None, 8, D_MODEL), lambda i, j: (mrow(i), 0, 0)),
                  pl.BlockSpec((1, D_MODEL), lambda i, j: (0, 0)),
                  pl.BlockSpec((D_MODEL, tn), lambda i, j: (0, j)),
                  pl.BlockSpec((D_MODEL, GATE_PAD), lambda i, j: (0, 0))],
        out_specs=(pl.BlockSpec((tm, tn), lambda i, j: (i, j)),
                   pl.BlockSpec((tm, GATE_PAD), lambda i, j: (i, 0))),
        scratch_shapes=[pltpu.VMEM((tm, D_MODEL), BF16)],
        compiler_params=_cparams(("parallel", "arbitrary")),
        name="in_proj",
    )(x_all, mod, norm_w.reshape(1, D_MODEL), w_main, w_gate)


CONV_ROWS = 256


def _conv_kernel(up_ref, x_ref, dn_ref, w_ref, b_ref, o_ref, *, n_lat_blocks, blocks_per_img):
    rb = pl.program_id(0)
    x = x_ref[...].astype(F32)
    w = w_ref[...]
    n = CONV_ROWS

    @pl.when(rb < n_lat_blocks)
    def _():
        rr = rb % blocks_per_img
        up = jnp.where(rr > 0, up_ref[...].astype(F32), 0.0)
        dn = jnp.where(rr < blocks_per_img - 1, dn_ref[...].astype(F32), 0.0)
        ext = jnp.concatenate([up, x, dn], axis=0)
        ne = n + 2 * GRID_W
        col = lax.broadcasted_iota(jnp.int32, (ne, 1), 0) % GRID_W
        xl = jnp.where(col >= 1, pltpu.roll(ext, 1, 0), 0.0)
        xr = jnp.where(col <= GRID_W - 2, pltpu.roll(ext, ne - 1, 0), 0.0)
        acc = None
        for i in range(3):
            y = w[3 * i:3 * i + 1] * xl + w[3 * i + 1:3 * i + 2] * ext + w[3 * i + 2:3 * i + 3] * xr
            part = y[i * GRID_W:i * GRID_W + n]
            acc = part if acc is None else acc + part
        o_ref[...] = _silu(acc + b_ref[...]).astype(o_ref.dtype)

    @pl.when(rb >= n_lat_blocks)
    def _():
        t = lax.broadcasted_iota(jnp.int32, (n, 1), 0)
        xl = jnp.where(t >= 1, pltpu.roll(x, 1, 0), 0.0)
        xr = jnp.where(t <= n - 2, pltpu.roll(x, n - 1, 0), 0.0)
        acc = w[3:4] * xl + w[4:5] * x + w[5:6] * xr
        o_ref[...] = _silu(acc + b_ref[...]).astype(o_ref.dtype)


def _mlstm_conv(p, conv_w, conv_b, *, n_batch, seq, ctx_len, cb=256):
    assert ctx_len == CONV_ROWS and seq % CONV_ROWS == 0
    n = p.shape[0]
    n_blocks = n // CONV_ROWS
    hb = CONV_ROWS // GRID_W
    n_halo = n // GRID_W
    c0 = COL_MQK * BR // cb
    kern = functools.partial(_conv_kernel, n_lat_blocks=n_batch * seq // CONV_ROWS,
                             blocks_per_img=seq // CONV_ROWS)
    return pl.pallas_call(
        kern,
        out_shape=jax.ShapeDtypeStruct((n, 2 * BR), BF16),
        grid=(n_blocks, 2 * BR // cb),
        in_specs=[pl.BlockSpec((GRID_W, cb), lambda r, c: (jnp.maximum(r * hb - 1, 0), c0 + c)),
                  pl.BlockSpec((CONV_ROWS, cb), lambda r, c: (r, c0 + c)),
                  pl.BlockSpec((GRID_W, cb), lambda r, c: (jnp.minimum((r + 1) * hb, n_halo - 1), c0 + c)),
                  pl.BlockSpec((9, cb), lambda r, c: (0, c)),
                  pl.BlockSpec((1, cb), lambda r, c: (0, c))],
        out_specs=pl.BlockSpec((CONV_ROWS, cb), lambda r, c: (r, c)),
        compiler_params=_cparams(("parallel", "parallel")),
        name="mlstm_conv",
    )(p, p, p, conv_w.reshape(9, 2 * BR), conv_b.reshape(1, 2 * BR))


def _chunk_block(b, d, i, *, n_ctx, n_lat, n_batch):
    ctx_j = jnp.where(d == 0, i, n_ctx - 1 - i)
    lat_j = jnp.where(d == 0, i - n_ctx, n_ctx + n_lat - 1 - i)
    return jnp.where(i < n_ctx, n_batch * n_lat + b * n_ctx + ctx_j, b * n_lat + lat_j)


def _order_mask(d, t):
    r = lax.broadcasted_iota(jnp.int32, (t, t), 0)
    c = lax.broadcasted_iota(jnp.int32, (t, t), 1)
    return jnp.where(d == 0, r - c, c - r) >= 0


def _ret_kernel(lg_ref, q_ref, k_ref, v_ref, o_ref, s_ref):
    d = pl.program_id(1)
    t = T_SCAN

    @pl.when(pl.program_id(2) == 0)
    def _():
        s_ref[...] = jnp.zeros_like(s_ref)

    r = lax.broadcasted_iota(jnp.int32, (t, t), 0)
    c = lax.broadcasted_iota(jnp.int32, (t, t), 1)
    rel = jnp.where(d == 0, r - c, c - r)
    relf = jnp.maximum(rel, 0).astype(F32)
    tt = lax.broadcasted_iota(jnp.int32, (t, 1), 0)
    pos = jnp.where(d == 0, tt, t - 1 - tt).astype(F32)
    scale = HD ** -0.5
    for h in range(NH):
        sl = slice(h * HD, (h + 1) * HD)
        lg = lg_ref[d, h]
        q = q_ref[:, sl]
        k = k_ref[:, sl]
        v = v_ref[:, sl]
        intra = jnp.where(rel >= 0, jnp.exp(lg * relf), 0.0)
        scores = (_dot_nt(q, k) * scale) * intra
        s_old = s_ref[h]
        out = _dot(scores.astype(BF16), v) + jnp.exp(lg * (pos + 1.0)) * _dot(q, s_old.astype(BF16))
        kd = (k.astype(F32) * (scale * jnp.exp(lg * (t - 1.0 - pos)))).astype(BF16)
        s_ref[h] = jnp.exp(lg * t) * s_old + _dot_tn(kd, v)
        o_ref[:, sl] = out


def _hgrn_kernel(lb_ref, q_ref, z_ref, v_ref, o_ref, st_ref):
    d = pl.program_id(1)
    t = T_SCAN
    cs = HGRN_SUB
    nsub = t // cs

    @pl.when(pl.program_id(2) == 0)
    def _():
        st_ref[...] = jnp.zeros_like(st_ref)

    lb = lb_ref[...]
    z = z_ref[...].astype(F32)
    f = lb + (1.0 - lb) * _sigmoid(z)
    logf = jnp.log(jnp.maximum(f, F_TINY))
    kk = (1.0 - lb) * _sigmoid(-z)
    qq = _silu(q_ref[...].astype(F32))
    tri = _order_mask(d, t).astype(F32)
    cum = jnp.dot(tri, logf, precision=_HI, preferred_element_type=F32)
    srow = lax.broadcasted_iota(jnp.int32, (cs, 1), 0)
    scol = lax.broadcasted_iota(jnp.int32, (cs, t), 1)
    for h in range(NH):
        sl = slice(h * HD, (h + 1) * HD)
        cum_h = cum[:, sl]
        q_h = qq[:, sl]
        k_h = kk[:, sl]
        v_h = v_ref[:, sl]
        v_f = v_h.astype(F32)
        st = st_ref[h]
        tail = jnp.where(d == 0, cum_h[t - 1:t], cum_h[0:1])
        out = _dot_nt((q_h * jnp.exp(cum_h)).astype(BF16), st.astype(BF16))
        rows_out = []
        for si in range(nsub):
            lo, hi = si * cs, (si + 1) * cs
            zero = jnp.zeros((1, HD), F32)
            b_f = cum_h[lo - 1:lo] if si > 0 else zero
            b_b = cum_h[hi:hi + 1] if si < nsub - 1 else zero
            b_i = jnp.where(d == 0, b_f, b_b)
            c_s, q_s, k_s, v_s = cum_h[lo:hi], q_h[lo:hi], k_h[lo:hi], v_f[lo:hi]
            qt = (q_s * jnp.exp(c_s - b_i)).astype(BF16)
            kt = (k_h * jnp.exp(jnp.minimum(b_i - cum_h, 0.0))).astype(BF16)
            valid = jnp.where(d == 0, lo - 1 - scol, scol - hi) >= 0
            s_off = jnp.where(valid, _dot_nt(qt, kt), 0.0)
            o_s = _dot(s_off.astype(BF16), v_h)
            diag = []
            for ti in range(cs):
                m = jnp.where(d == 0, ti - srow, srow - ti) >= 0
                e = jnp.exp(jnp.where(m, c_s[ti:ti + 1] - c_s, 0.0))
                term = jnp.where(m, e, 0.0) * (q_s[ti:ti + 1] * k_s)
                w = jnp.sum(term, axis=1, keepdims=True)
                diag.append(jnp.sum(w * v_s, axis=0, keepdims=True))
            rows_out.append(o_s + jnp.concatenate(diag, axis=0))
        o_ref[:, sl] = out + jnp.concatenate(rows_out, axis=0)
        ke = (k_h * jnp.exp(tail - cum_h)).astype(BF16)
        st_ref[h] = st * jnp.exp(tail) + _dot_tn(v_h, ke)


def _log_sigmoid(x):
    return jnp.minimum(x, 0.0) - jnp.log(1.0 + jnp.exp(-jnp.abs(x)))


def _mlstm_kernel(qk_ref, v_ref, g_ref, gb_ref, o_ref, c_ref, n_ref, m_ref):
    d = pl.program_id(1)
    t = T_SCAN

    @pl.when(pl.program_id(2) == 0)
    def _():
        c_ref[...] = jnp.zeros_like(c_ref)
        n_ref[...] = jnp.zeros_like(n_ref)
        m_ref[...] = jnp.zeros_like(m_ref)

    g = g_ref[...] + gb_ref[...]
    gt = g.T
    mask = _order_mask(d, t)
    tri = mask.astype(F32)
    lf = _log_sigmoid(g)
    cum_c = jnp.dot(tri, lf, precision=_HI, preferred_element_type=F32)
    cum_r = lax.dot_general(lf.T, tri, (((1,), (1,)), ((), ())), precision=_HI,
                            preferred_element_type=F32)
    scale = HD ** -0.5
    for h in range(NH):
        sl = slice(h * HD, (h + 1) * HD)
        li, lfw = h, NH + h
        i_col = jnp.where(d == 0, g[:, li:li + 1], g[:, 8 + li:8 + li + 1])
        i_row = jnp.where(d == 0, gt[li:li + 1], gt[8 + li:8 + li + 1])
        cc = jnp.where(d == 0, cum_c[:, lfw:lfw + 1], cum_c[:, 8 + lfw:8 + lfw + 1])
        cr = jnp.where(d == 0, cum_r[lfw:lfw + 1], cum_r[8 + lfw:8 + lfw + 1])
        total = jnp.where(d == 0, cr[:, t - 1:t], cr[:, 0:1])
        m_old = m_ref[h][:, 0:1]
        q = qk_ref[:, sl]
        k = qk_ref[:, BR + h * HD:BR + (h + 1) * HD]
        v = v_ref[:, sl]
        logw = jnp.where(mask, cc - cr + i_row, NEG_BIG)
        from_state = cc + m_old
        m_t = jnp.maximum(from_state, jnp.max(logw, axis=1, keepdims=True))
        w = jnp.exp(logw - m_t)
        w_state = jnp.exp(from_state - m_t)
        scores = (_dot_nt(q, k) * scale) * w
        c_old = c_ref[h]
        n_old = n_ref[h]
        num = _dot(scores.astype(BF16), v) + w_state * _dot(q, c_old.astype(BF16))
        den = (jnp.sum(scores, axis=1, keepdims=True)
               + w_state * jnp.sum(q.astype(F32) * n_old, axis=1, keepdims=True))
        o_ref[:, sl] = num / jnp.maximum(jnp.abs(den), jnp.exp(-m_t))
        m_new = jnp.maximum(total + m_old, jnp.max(total - cr + i_row, axis=1, keepdims=True))
        keep = jnp.exp(total + m_old - m_new)
        w_end = jnp.exp(total - cc + i_col - m_new)
        kw = k.astype(F32) * (scale * w_end)
        c_ref[h] = keep * c_old + _dot_tn(kw.astype(BF16), v)
        n_ref[h] = keep * n_old + jnp.sum(kw, axis=0, keepdims=True)
        m_ref[h] = jnp.broadcast_to(m_new, (1, HD))


def _scans(p, gates, qk, lg, lb, gate_b, *, n_batch, seq, ctx_len):
    n = p.shape[0]
    t = T_SCAN
    n_ctx, n_lat = ctx_len // t, seq // t
    blk = functools.partial(_chunk_block, n_ctx=n_ctx, n_lat=n_lat, n_batch=n_batch)
    n_steps = n_ctx + n_lat

    def col(cblock):
        return pl.BlockSpec((t, BR), lambda b, d, i: (blk(b, d, i), cblock))

    out_spec = pl.BlockSpec((None, t, BR), lambda b, d, i: (d, blk(b, d, i), 0))
    state = pltpu.VMEM((NH, HD, HD), F32)

    def call(kernel, inputs, in_specs, scratch, name):
        return pl.pallas_call(
            kernel,
            out_shape=jax.ShapeDtypeStruct((2, n, BR), F32),
            grid=(n_batch, 2, n_steps),
            in_specs=in_specs,
            out_specs=out_spec,
            scratch_shapes=scratch,
            compiler_params=_cparams(("parallel", "arbitrary", "arbitrary")),
            name=name,
        )(*inputs)

    raw_b = call(_ret_kernel, (lg, p, p, p),
                 [pl.BlockSpec(memory_space=pltpu.SMEM), col(COL_RQ), col(COL_RK), col(COL_RV)],
                 [state], "retention_scan")
    raw_a = call(_hgrn_kernel, (lb, p, p, p),
                 [pl.BlockSpec((None, 1, BR), lambda b, d, i: (d, 0, 0)),
                  col(COL_HQ),
                  pl.BlockSpec((t, BR), lambda b, d, i: (blk(b, d, i), COL_HF + d)),
                  col(COL_HV)],
                 [state], "hgrn2_scan")
    raw_d = call(_mlstm_kernel, (qk, p, gates, gate_b),
                 [pl.BlockSpec((t, 2 * BR), lambda b, d, i: (blk(b, d, i), 0)),
                  col(COL_MV),
                  pl.BlockSpec((t, GATE_PAD), lambda b, d, i: (blk(b, d, i), 0)),
                  pl.BlockSpec((1, GATE_PAD), lambda b, d, i: (0, 0))],
                 [state, pltpu.VMEM((NH, 1, HD), F32), pltpu.VMEM((NH, 1, HD), F32)], "mlstm_scan")
    return raw_a, raw_b, raw_d


def _s5_kernel(u_ref, kr_ref, w_ref, ys_ref, lam_ref, o_ref, m_ref, wv_ref, xp_ref, *, n_ctx, n_steps):
    d = pl.program_id(1)
    t = S5_T
    lane = lax.broadcasted_iota(jnp.int32, (S5_HG, S5_FLAT), 1)
    kr = kr_ref[...]

    @pl.when(d == 0)
    def _():
        for s in range(t):
            sh = s * S5_HG
            blk = kr if sh == 0 else jnp.where(lane >= sh, pltpu.roll(kr, sh, 1), 0.0)
            m_ref[s * S5_HG:(s + 1) * S5_HG, :] = blk.astype(BF16)

    @pl.when(d == 1)
    def _():
        for s in range(t):
            sh = (t - 1 - s) * S5_HG
            blk = kr if sh == 0 else jnp.where(lane < S5_FLAT - sh, pltpu.roll(kr, S5_FLAT - sh, 1), 0.0)
            m_ref[s * S5_HG:(s + 1) * S5_HG, :] = blk.astype(BF16)

    u = u_ref[...]
    wv_ref[...] = _dot(u, w_ref[...])
    lr = lam_ref[0:1, :]
    li = lam_ref[1:2, :]

    def step(i, carry):
        xr, xi = carry
        c = jnp.where(d == 0, i, jnp.where(i < n_ctx, n_ctx - 1 - i, n_steps + n_ctx - 1 - i))
        row = pl.multiple_of(c * S5_BPAD, S5_BPAD)
        xp_ref[pl.ds(row, S5_BPAD), 0:128] = xr
        xp_ref[pl.ds(row, S5_BPAD), 128:256] = xi
        wr = wv_ref[pl.ds(row, S5_BPAD), 0:128]
        wi = wv_ref[pl.ds(row, S5_BPAD), 128:256]
        return lr * xr - li * xi + wr, lr * xi + li * xr + wi

    zero = jnp.zeros((S5_BPAD, 128), F32)
    lax.fori_loop(0, n_steps, step, (zero, zero))
    y = _dot(u, m_ref[...]) + _dot(xp_ref[...].astype(BF16), ys_ref[...])

    @pl.when(d == 0)
    def _():
        o_ref[...] = y

    @pl.when(d == 1)
    def _():
        o_ref[...] += y


def _s5_operators(a_re, a_im, log_dt, b_re, b_im, c_re, c_im):
    t = S5_T
    a_re, a_im = a_re.astype(F32), a_im.astype(F32)
    dt = jnp.exp(log_dt.astype(F32))[..., None]
    mag = jnp.exp(a_re * dt)
    lam_re, lam_im = mag * jnp.cos(a_im * dt), mag * jnp.sin(a_im * dt)
    den = a_re * a_re + a_im * a_im
    num_re, num_im = lam_re - 1.0, lam_im
    fr = (num_re * a_re + num_im * a_im) / den
    fi = (num_im * a_re - num_re * a_im) / den
    b_re, b_im = b_re.astype(F32), b_im.astype(F32)
    bb_re = fr[..., None] * b_re - fi[..., None] * b_im
    bb_im = fr[..., None] * b_im + fi[..., None] * b_re
    c_re, c_im = c_re.astype(F32), c_im.astype(F32)
    n = jnp.arange(t + 1, dtype=F32)[:, None, None, None]
    pmag = jnp.exp(n * (a_re * dt)[None])
    l_re = pmag * jnp.cos(n * (a_im * dt)[None])
    l_im = pmag * jnp.sin(n * (a_im * dt)[None])
    lb_re = l_re[..., None] * bb_re[None] - l_im[..., None] * bb_im[None]
    lb_im = l_re[..., None] * bb_im[None] + l_im[..., None] * bb_re[None]
    kk = (jnp.einsum("dghp,ndgpk->ndghk", c_re, lb_re[:t], precision=_HI)
          - jnp.einsum("dghp,ndgpk->ndghk", c_im, lb_im[:t], precision=_HI))
    kr_f = jnp.transpose(kk[:, 0], (1, 3, 0, 2)).reshape(S5_G, S5_HG, S5_FLAT)
    kr_b = jnp.transpose(kk[::-1, 1], (1, 3, 0, 2)).reshape(S5_G, S5_HG, S5_FLAT)
    kr = jnp.stack([kr_f, kr_b])

    def w_of(di, order):
        wr = jnp.transpose(lb_re[order, di], (1, 0, 3, 2)).reshape(S5_G, S5_FLAT, S5_P)
        wi = jnp.transpose(lb_im[order, di], (1, 0, 3, 2)).reshape(S5_G, S5_FLAT, S5_P)
        z = jnp.zeros_like(wr)
        return jnp.concatenate([wr, z, wi, z], axis=-1)

    asc = jnp.arange(t)
    w = jnp.stack([w_of(0, t - 1 - asc), w_of(1, asc)]).astype(BF16)

    def ys_of(di, order):
        lr, li = l_re[order, di], l_im[order, di]
        cr, ci = c_re[di], c_im[di]
        yr = cr[None] * lr[:, :, None, :] - ci[None] * li[:, :, None, :]
        yi = -(cr[None] * li[:, :, None, :] + ci[None] * lr[:, :, None, :])
        yr = jnp.transpose(yr, (1, 3, 0, 2)).reshape(S5_G, S5_P, S5_FLAT)
        yi = jnp.transpose(yi, (1, 3, 0, 2)).reshape(S5_G, S5_P, S5_FLAT)
        z = jnp.zeros_like(yr)
        return jnp.concatenate([yr, z, yi, z], axis=1)

    ys = jnp.stack([ys_of(0, asc + 1), ys_of(1, t - asc)]).astype(BF16)
    pad = jnp.zeros((2, S5_G, 128 - S5_P), F32)
    lam_t = jnp.stack([jnp.concatenate([l_re[t], pad], -1), jnp.concatenate([l_im[t], pad], -1)], axis=2)
    return kr, w, ys, lam_t


def _s5_scan(p, ops, *, n_batch, seq, ctx_len):
    kr, w, ys, lam_t = ops
    t = S5_T
    n_ctx, n_lat = ctx_len // t, seq // t
    n_steps = n_ctx + n_lat
    nl = n_batch * seq
    u = p[:, COL_SU * BR:(COL_SU + 1) * BR]

    def to_steps(x, nchunk):
        x = x.reshape(n_batch, nchunk, t, S5_G, S5_HG)
        return jnp.transpose(x, (3, 1, 0, 2, 4)).reshape(S5_G, nchunk, n_batch, S5_FLAT)

    uf = jnp.concatenate([to_steps(u[nl:], n_ctx), to_steps(u[:nl], n_lat)], axis=1)
    uf = jnp.pad(uf, ((0, 0), (0, 0), (0, S5_BPAD - n_batch), (0, 0))).reshape(S5_G, n_steps * S5_BPAD, S5_FLAT)
    rows = n_steps * S5_BPAD
    kern = functools.partial(_s5_kernel, n_ctx=n_ctx, n_steps=n_steps)
    yf = pl.pallas_call(
        kern,
        out_shape=jax.ShapeDtypeStruct((S5_G, rows, S5_FLAT), F32),
        grid=(S5_G, 2),
        in_specs=[pl.BlockSpec((None, rows, S5_FLAT), lambda g, d: (g, 0, 0)),
                  pl.BlockSpec((None, None, S5_HG, S5_FLAT), lambda g, d: (d, g, 0, 0)),
                  pl.BlockSpec((None, None, S5_FLAT, 256), lambda g, d: (d, g, 0, 0)),
                  pl.BlockSpec((None, None, 256, S5_FLAT), lambda g, d: (d, g, 0, 0)),
                  pl.BlockSpec((None, None, 2, 128), lambda g, d: (d, g, 0, 0))],
        out_specs=pl.BlockSpec((None, rows, S5_FLAT), lambda g, d: (g, 0, 0)),
        scratch_shapes=[pltpu.VMEM((S5_FLAT, S5_FLAT), BF16),
                        pltpu.VMEM((rows, 256), F32),
                        pltpu.VMEM((rows, 256), F32)],
        compiler_params=_cparams(("parallel", "arbitrary")),
        name="s5_scan",
    )(uf, kr, w, ys, lam_t)
    yf = yf.reshape(S5_G, n_steps, S5_BPAD, t, S5_HG)[:, :, :n_batch]

    def from_steps(x, nchunk):
        return jnp.transpose(x, (2, 1, 3, 0, 4)).reshape(n_batch * nchunk * t, BR)

    return jnp.concatenate([from_steps(yf[:, n_ctx:], n_lat), from_steps(yf[:, :n_ctx], n_ctx)], axis=0)


def _head_rms(x, g):
    parts = []
    for h in range(NH):
        xh = x[:, h * HD:(h + 1) * HD]
        parts.append(xh * lax.rsqrt(jnp.mean(jnp.square(xh), axis=-1, keepdims=True) + NORM_EPS))
    return jnp.concatenate(parts, axis=-1) * g


def _finish_kernel(ra_ref, rb_ref, rc_ref, rd_ref, ga_ref, gb_ref, u_ref, gd_ref, mg_ref, x_ref, mod_ref,
                   vec_ref, glu_w_ref, wb_ref, wo_ref, o_ref):
    vec = vec_ref[...]
    oa = _head_rms(ra_ref[0] + ra_ref[1], vec[0:1]) * _silu(ga_ref[...].astype(F32))
    ob = _head_rms(rb_ref[0] + rb_ref[1], vec[1:2]) * _silu(gb_ref[...].astype(F32))
    od = _head_rms(rd_ref[0] + rd_ref[1], vec[2:3]) * _silu(gd_ref[...].astype(F32))
    yc = rc_ref[...] + vec[3:4] * u_ref[...].astype(F32)
    yc = 0.5 * yc * (1.0 + lax.erf(yc * (2.0 ** -0.5)))
    oc = yc * _sigmoid(_dot(yc.astype(BF16), glu_w_ref[...]) + vec[4:5])
    y = None
    for j, o in enumerate((oa, ob, oc, od)):
        gate = _sigmoid(mg_ref[:, j * D_MODEL:(j + 1) * D_MODEL].astype(F32))
        term = gate * _dot(o.astype(BF16), wb_ref[j])
        y = term if y is None else y + term
    mix = _dot(y.astype(BF16), wo_ref[...])
    o_ref[...] = x_ref[...] + mod_ref[2:3, :] * mix


def _finish(raw_a, raw_b, raw_c, raw_d, p, x_all, mod, vec, glu_w, w_branch, w_out, *, n_rows, n_batch, seq,
            tm=256):
    mrow = _mod_row_map(n_batch * seq // tm, seq // tm, n_batch)
    const2 = lambda i: (0, 0)
    const3 = lambda i: (0, 0, 0)
    one = pl.Buffered(1)

    def raw(i):
        return (0, i, 0)

    def pcol(cblock, width=BR):
        return pl.BlockSpec((tm, width), lambda i: (i, cblock * BR // width))

    return pl.pallas_call(
        _finish_kernel,
        out_shape=jax.ShapeDtypeStruct((n_rows, D_MODEL), F32),
        grid=(n_rows // tm,),
        in_specs=[pl.BlockSpec((2, tm, BR), raw),
                  pl.BlockSpec((2, tm, BR), raw),
                  pl.BlockSpec((tm, BR), lambda i: (i, 0)),
                  pl.BlockSpec((2, tm, BR), raw),
                  pcol(COL_HG), pcol(COL_RG), pcol(COL_SU), pcol(COL_MZ),
                  pl.BlockSpec((tm, N_BRANCH * D_MODEL), lambda i: (i, COL_MERGE * BR // (N_BRANCH * D_MODEL))),
                  pl.BlockSpec((tm, D_MODEL), lambda i: (i, 0)),
                  pl.BlockSpec((None, 8, D_MODEL), lambda i: (mrow(i), 0, 0)),
                  pl.BlockSpec((8, BR), const2, pipeline_mode=one),
                  pl.BlockSpec((BR, BR), const2, pipeline_mode=one),
                  pl.BlockSpec((N_BRANCH, BR, D_MODEL), const3, pipeline_mode=one),
                  pl.BlockSpec((D_MODEL, D_MODEL), const2, pipeline_mode=one)],
        out_specs=pl.BlockSpec((tm, D_MODEL), lambda i: (i, 0)),
        compiler_params=_cparams(("parallel",)),
        name="finish",
    )(raw_a, raw_b, raw_c, raw_d, p, p, p, p, p, x_all, mod, vec, glu_w, w_branch, w_out)


def _mlp_kernel(x_ref, mod_ref, g_ref, w1_ref, w2_ref, fin_ref, o_ref, xn_ref, acc_ref, *, final_norm):
    j = pl.program_id(1)

    @pl.when(j == 0)
    def _():
        h = _norm_mod(x_ref[...], g_ref[...], mod_ref[3:4, :], mod_ref[4:5, :])
        xn_ref[...] = h.astype(BF16)
        acc_ref[...] = jnp.zeros_like(acc_ref)

    a = jnp.square(jnp.maximum(_dot(xn_ref[...], w1_ref[...]), 0.0))
    acc_ref[...] += _dot(a.astype(BF16), w2_ref[...])

    @pl.when(j == pl.num_programs(1) - 1)
    def _():
        y = x_ref[...] + mod_ref[5:6, :] * acc_ref[...]
        if final_norm:
            y = (y * lax.rsqrt(jnp.mean(jnp.square(y), axis=-1, keepdims=True) + NORM_EPS)) * fin_ref[...]
        o_ref[...] = y


def _mlp(x_all, mod, norm_w, w1, w2, fin_w, *, n_rows, n_batch, seq, final_norm, tm=512, tf=512):
    mrow = _mod_row_map(n_batch * seq // tm, seq // tm, n_batch)
    kern = functools.partial(_mlp_kernel, final_norm=final_norm)
    return pl.pallas_call(
        kern,
        out_shape=jax.ShapeDtypeStruct((n_rows, D_MODEL), F32),
        grid=(n_rows // tm, D_FF // tf),
        in_specs=[pl.BlockSpec((tm, D_MODEL), lambda i, j: (i, 0)),
                  pl.BlockSpec((None, 8, D_MODEL), lambda i, j: (mrow(i), 0, 0)),
                  pl.BlockSpec((1, D_MODEL), lambda i, j: (0, 0)),
                  pl.BlockSpec((D_MODEL, tf), lambda i, j: (0, j)),
                  pl.BlockSpec((tf, D_MODEL), lambda i, j: (j, 0)),
                  pl.BlockSpec((1, D_MODEL), lambda i, j: (0, 0))],
        out_specs=pl.BlockSpec((tm, D_MODEL), lambda i, j: (i, 0)),
        scratch_shapes=[pltpu.VMEM((tm, D_MODEL), BF16), pltpu.VMEM((tm, D_MODEL), F32)],
        compiler_params=_cparams(("parallel", "arbitrary")),
        name="mlp",
    )(x_all, mod, norm_w.reshape(1, D_MODEL), w1, w2, fin_w.reshape(1, D_MODEL))


def _split_w_in(w):
    main = jnp.concatenate([w[:, GATE_OFF + 4 * NH:], w[:, :GATE_OFF]], axis=1).astype(BF16)
    gate = jnp.pad(w[:, GATE_OFF:GATE_OFF + 4 * NH], ((0, 0), (0, GATE_PAD - 4 * NH))).astype(BF16)
    return main, gate


def kernel(x, c, ctx, c_ctx, w_mod, b_mod, norm_mix, norm_mlp, w_in, hgrn_lb_logits, hgrn_norm, ret_decay, ret_norm, s5_a_re, s5_a_im, s5_log_dt, s5_b_re, s5_b_im, s5_c_re, s5_c_im, s5_d, s5_glu_w, s5_glu_b, mlstm_conv_w, mlstm_conv_b, mlstm_gate_b, mlstm_norm, w_branch, w_out, w_ff1, w_ff2, final_norm):
    n_batch, seq, _ = x.shape
    ctx_len = ctx.shape[1]
    depth = w_in.shape[0]
    nl = n_batch * seq
    dims = dict(n_batch=n_batch, seq=seq)

    p_lb = jax.nn.softmax(hgrn_lb_logits.astype(F32), axis=0)
    lower_bounds = jnp.cumsum(p_lb, axis=0) - p_lb[0]
    c_pad = jnp.zeros((8, D_MODEL), F32).at[:n_batch].set(c).at[n_batch].set(c_ctx)
    x_all = jnp.concatenate([x.reshape(nl, D_MODEL), ctx.reshape(n_batch * ctx_len, D_MODEL)], axis=0)

    for l in range(depth):
        last = l == depth - 1
        mod = _modulation(c_pad, w_mod[l], b_mod[l])
        mod = jnp.pad(mod[:n_batch + 1].reshape(n_batch + 1, N_MOD, D_MODEL), ((0, 0), (0, 8 - N_MOD), (0, 0)))
        w_main, w_gate = _split_w_in(w_in[l])
        p, gates = _in_proj(x_all, mod, norm_mix[l], w_main, w_gate, **dims)
        qk = _mlstm_conv(p, mlstm_conv_w[l], mlstm_conv_b[l], ctx_len=ctx_len, **dims)
        lg = jnp.log1p(-jnp.exp(ret_decay[l].astype(F32)))
        gate_b = jnp.pad(mlstm_gate_b[l].reshape(1, 4 * NH).astype(F32), ((0, 0), (0, GATE_PAD - 4 * NH)))
        raw_a, raw_b, raw_d = _scans(p, gates, qk, lg, lower_bounds[l].reshape(2, 1, BR), gate_b,
                                     ctx_len=ctx_len, **dims)
        ops = _s5_operators(s5_a_re[l], s5_a_im[l], s5_log_dt[l], s5_b_re[l], s5_b_im[l], s5_c_re[l], s5_c_im[l])
        raw_c = _s5_scan(p, ops, ctx_len=ctx_len, **dims)
        vec = jnp.zeros((8, BR), F32).at[0].set(hgrn_norm[l]).at[1].set(ret_norm[l]).at[2].set(mlstm_norm[l])
        vec = vec.at[3].set(s5_d[l]).at[4].set(s5_glu_b[l])
        n_rows = nl if last else x_all.shape[0]
        x_mid = _finish(raw_a, raw_b, raw_c, raw_d, p, x_all, mod, vec, s5_glu_w[l].astype(BF16),
                        w_branch[l].astype(BF16), w_out[l].astype(BF16), n_rows=n_rows, **dims)
        x_all = _mlp(x_mid, mod, norm_mlp[l], w_ff1[l].astype(BF16), w_ff2[l].astype(BF16), final_norm,
                     n_rows=n_rows, final_norm=last, **dims)
    return x_all.reshape(n_batch, seq, D_MODEL)
```

```python
import functools
import math

import jax
import jax.numpy as jnp
from jax import lax
from jax.experimental import pallas as pl
from jax.experimental.pallas import tpu as pltpu

F32 = jnp.float32
BF16 = jnp.bfloat16

D_MODEL = 2048
N_BRANCH = 4
BR = D_MODEL // N_BRANCH
HD = 128
NH = BR // HD
S5_HG = 16
S5_G = BR // S5_HG
S5_P = 64
D_FF = 4 * D_MODEL
N_MOD = 6
GRID_W = 64
NORM_EPS = 1e-6
NEG_BIG = -1e30
F_TINY = 1e-30
S5_DT_MIN = 1e-3

COL_MERGE = 0
COL_HQ, COL_HF, COL_HV, COL_HG = 16, 17, 19, 20
COL_RQ, COL_RK, COL_RV, COL_RG = 21, 22, 23, 24
COL_SU = 25
COL_MQK, COL_MV, COL_MZ = 26, 28, 29
P_WIDTH = 30 * BR
GATE_OFF = 14 * BR
GATE_PAD = 128

T_HGRN = 64
T_ATTN = 256
HGRN_SUB = 16
S5_T = 64
S5_FLAT = S5_T * S5_HG
S5_BPAD = 8
VMEM_LIMIT = 56 * 1024 * 1024

_HI = lax.Precision.HIGHEST


def _cparams(sem):
    return pltpu.CompilerParams(dimension_semantics=sem, vmem_limit_bytes=VMEM_LIMIT)


def _dot(a, b):
    return jnp.dot(a, b, preferred_element_type=F32)


def _dot_nt(a, b):
    return lax.dot_general(a, b, (((1,), (1,)), ((), ())), preferred_element_type=F32)


def _dot_tn(a, b):
    return lax.dot_general(a, b, (((0,), (0,)), ((), ())), preferred_element_type=F32)


def _sigmoid(x):
    return 1.0 / (1.0 + jnp.exp(-x))


def _silu(x):
    return x * _sigmoid(x)


def _mod_kernel(c_ref, w_ref, b_ref, o_ref):
    a = _silu(c_ref[...]).astype(BF16)
    o_ref[...] = _dot(a, w_ref[...].astype(BF16)) + b_ref[...]


def _modulation(c_pad, w_mod, b_mod, layer):
    n = w_mod.shape[2]
    tn = 1024
    return pl.pallas_call(
        _mod_kernel,
        out_shape=jax.ShapeDtypeStruct((8, n), F32),
        grid=(n // tn,),
        in_specs=[pl.BlockSpec((8, D_MODEL), lambda j: (0, 0)),
                  pl.BlockSpec((None, D_MODEL, tn), lambda j: (layer, 0, j)),
                  pl.BlockSpec((None, 1, tn), lambda j: (layer, 0, j))],
        out_specs=pl.BlockSpec((8, tn), lambda j: (0, j)),
        compiler_params=_cparams(("parallel",)),
        name="modulation",
    )(c_pad, w_mod, b_mod.reshape(b_mod.shape[0], 1, n))


def _norm_mod(x, g, shift, scale):
    y = x * lax.rsqrt(jnp.mean(jnp.square(x), axis=-1, keepdims=True) + NORM_EPS)
    return (y * g) * (1.0 + scale) + shift


def _inproj_kernel(x_ref, mod_ref, g_ref, w_ref, wg_ref, p_ref, gate_ref, xn_ref):
    @pl.when(pl.program_id(1) == 0)
    def _():
        h = _norm_mod(x_ref[...], g_ref[...], mod_ref[0:1, :], mod_ref[1:2, :])
        xn_ref[...] = h.astype(BF16)
        gate_ref[...] = _dot(xn_ref[...], wg_ref[...])

    p_ref[...] = _dot(xn_ref[...], w_ref[...]).astype(p_ref.dtype)


def _mod_row_map(n_rows, tm, n_batch, seq):
    assert n_rows % tm == 0 and seq % tm == 0, (n_rows, seq, tm)
    n_lat_tiles, tiles_per_batch = n_batch * seq // tm, seq // tm

    def row(i):
        return jnp.where(i < n_lat_tiles, i // tiles_per_batch, n_batch)
    return row


def _in_proj(x_all, mod, norm_w, w_main, w_gate, layer, *, n_batch, seq, tm=1024, tn=1024):
    n = x_all.shape[0]
    mrow = _mod_row_map(n, tm, n_batch, seq)
    return pl.pallas_call(
        _inproj_kernel,
        out_shape=(jax.ShapeDtypeStruct((n, P_WIDTH), BF16),
                   jax.ShapeDtypeStruct((n, GATE_PAD), F32)),
        grid=(n // tm, P_WIDTH // tn),
        in_specs=[pl.BlockSpec((tm, D_MODEL), lambda i, j: (i, 0)),
                  pl.BlockSpec((None, 8, D_MODEL), lambda i, j: (mrow(i), 0, 0)),
                  pl.BlockSpec((None, 1, D_MODEL), lambda i, j: (layer, 0, 0)),
                  pl.BlockSpec((None, D_MODEL, tn), lambda i, j: (layer, 0, j)),
                  pl.BlockSpec((None, D_MODEL, GATE_PAD), lambda i, j: (layer, 0, 0))],
        out_specs=(pl.BlockSpec((tm, tn), lambda i, j: (i, j)),
                   pl.BlockSpec((tm, GATE_PAD), lambda i, j: (i, 0))),
        scratch_shapes=[pltpu.VMEM((tm, D_MODEL), BF16)],
        compiler_params=_cparams(("parallel", "arbitrary")),
        name="in_proj",
    )(x_all, mod, norm_w, w_main, w_gate)


CONV_ROWS = 256


def _conv_kernel(up_ref, x_ref, dn_ref, w_ref, b_ref, o_ref, *, n_lat_blocks, blocks_per_img):
    rb = pl.program_id(0)
    x = x_ref[...].astype(F32)
    w = w_ref[...]
    n = CONV_ROWS

    @pl.when(rb < n_lat_blocks)
    def _():
        rr = rb % blocks_per_img
        up = jnp.where(rr > 0, up_ref[...].astype(F32), 0.0)
        dn = jnp.where(rr < blocks_per_img - 1, dn_ref[...].astype(F32), 0.0)
        ext = jnp.concatenate([up, x, dn], axis=0)
        ne = n + 2 * GRID_W
        col = lax.broadcasted_iota(jnp.int32, (ne, 1), 0) % GRID_W
        xl = jnp.where(col >= 1, pltpu.roll(ext, 1, 0), 0.0)
        xr = jnp.where(col <= GRID_W - 2, pltpu.roll(ext, ne - 1, 0), 0.0)
        acc = None
        for i in range(3):
            y = w[3 * i:3 * i + 1] * xl + w[3 * i + 1:3 * i + 2] * ext + w[3 * i + 2:3 * i + 3] * xr
            part = y[i * GRID_W:i * GRID_W + n]
            acc = part if acc is None else acc + part
        o_ref[...] = _silu(acc + b_ref[...]).astype(o_ref.dtype)

    @pl.when(rb >= n_lat_blocks)
    def _():
        t = lax.broadcasted_iota(jnp.int32, (n, 1), 0)
        xl = jnp.where(t >= 1, pltpu.roll(x, 1, 0), 0.0)
        xr = jnp.where(t <= n - 2, pltpu.roll(x, n - 1, 0), 0.0)
        acc = w[3:4] * xl + w[4:5] * x + w[5:6] * xr
        o_ref[...] = _silu(acc + b_ref[...]).astype(o_ref.dtype)


def _mlstm_conv(p, conv_w, conv_b, layer, *, n_batch, seq, ctx_len, cb=256):
    assert ctx_len == CONV_ROWS and seq % CONV_ROWS == 0
    n = p.shape[0]
    n_blocks = n // CONV_ROWS
    hb = CONV_ROWS // GRID_W
    n_halo = n // GRID_W
    c0 = COL_MQK * BR // cb
    kern = functools.partial(_conv_kernel, n_lat_blocks=n_batch * seq // CONV_ROWS,
                             blocks_per_img=seq // CONV_ROWS)
    return pl.pallas_call(
        kern,
        out_shape=jax.ShapeDtypeStruct((n, 2 * BR), BF16),
        grid=(n_blocks, 2 * BR // cb),
        in_specs=[pl.BlockSpec((GRID_W, cb), lambda r, c: (jnp.maximum(r * hb - 1, 0), c0 + c)),
                  pl.BlockSpec((CONV_ROWS, cb), lambda r, c: (r, c0 + c)),
                  pl.BlockSpec((GRID_W, cb), lambda r, c: (jnp.minimum((r + 1) * hb, n_halo - 1), c0 + c)),
                  pl.BlockSpec((None, 9, cb), lambda r, c: (layer, 0, c)),
                  pl.BlockSpec((None, 1, cb), lambda r, c: (layer, 0, c))],
        out_specs=pl.BlockSpec((CONV_ROWS, cb), lambda r, c: (r, c)),
        compiler_params=_cparams(("parallel", "parallel")),
        name="mlstm_conv",
    )(p, p, p, conv_w, conv_b)


def _chunk_block(b, d, i, *, n_ctx, n_lat, n_batch):
    ctx_j = jnp.where(d == 0, i, n_ctx - 1 - i)
    lat_j = jnp.where(d == 0, i - n_ctx, n_ctx + n_lat - 1 - i)
    return jnp.where(i < n_ctx, n_batch * n_lat + b * n_ctx + ctx_j, b * n_lat + lat_j)


def _order_mask(d, t):
    r = lax.broadcasted_iota(jnp.int32, (t, t), 0)
    c = lax.broadcasted_iota(jnp.int32, (t, t), 1)
    return jnp.where(d == 0, r - c, c - r) >= 0


def _ret_kernel(lg_ref, q_ref, k_ref, v_ref, o_ref, s_ref):
    d = pl.program_id(1)
    t = T_ATTN

    @pl.when(pl.program_id(2) == 0)
    def _():
        s_ref[...] = jnp.zeros_like(s_ref)

    r = lax.broadcasted_iota(jnp.int32, (t, t), 0)
    c = lax.broadcasted_iota(jnp.int32, (t, t), 1)
    rel = jnp.where(d == 0, r - c, c - r)
    relf = jnp.maximum(rel, 0).astype(F32)
    tt = lax.broadcasted_iota(jnp.int32, (t, 1), 0)
    pos = jnp.where(d == 0, tt, t - 1 - tt).astype(F32)
    scale = HD ** -0.5
    for h in range(NH):
        sl = slice(h * HD, (h + 1) * HD)
        lg = lg_ref[d, h]
        q = q_ref[:, sl]
        k = k_ref[:, sl]
        v = v_ref[:, sl]
        intra = jnp.where(rel >= 0, jnp.exp(lg * relf), 0.0)
        scores = (_dot_nt(q, k) * scale) * intra
        s_old = s_ref[h]
        out = _dot(scores.astype(BF16), v) + jnp.exp(lg * (pos + 1.0)) * _dot(q, s_old.astype(BF16))
        kd = (k.astype(F32) * (scale * jnp.exp(lg * (t - 1.0 - pos)))).astype(BF16)
        s_ref[h] = jnp.exp(lg * t) * s_old + _dot_tn(kd, v)
        o_ref[:, sl] = out


def _hgrn_kernel(lb_ref, q_ref, z_ref, v_ref, o_ref, st_ref):
    d = pl.program_id(1)
    t = T_HGRN
    cs = HGRN_SUB
    nsub = t // cs

    @pl.when(pl.program_id(2) == 0)
    def _():
        st_ref[...] = jnp.zeros_like(st_ref)

    lb = lb_ref[...]
    z = z_ref[...].astype(F32)
    f = lb + (1.0 - lb) * _sigmoid(z)
    logf = jnp.log(jnp.maximum(f, F_TINY))
    kk = (1.0 - lb) * _sigmoid(-z)
    qq = _silu(q_ref[...].astype(F32))
    tri = _order_mask(d, t).astype(F32)
    cum = jnp.dot(tri, logf, precision=_HI, preferred_element_type=F32)
    srow = lax.broadcasted_iota(jnp.int32, (cs, 1), 0)
    scol = lax.broadcasted_iota(jnp.int32, (cs, t), 1)
    for h in range(NH):
        sl = slice(h * HD, (h + 1) * HD)
        cum_h = cum[:, sl]
        q_h = qq[:, sl]
        k_h = kk[:, sl]
        v_h = v_ref[:, sl]
        v_f = v_h.astype(F32)
        st = st_ref[h]
        tail = jnp.where(d == 0, cum_h[t - 1:t], cum_h[0:1])
        out = _dot_nt((q_h * jnp.exp(cum_h)).astype(BF16), st.astype(BF16))
        rows_out = []
        for si in range(nsub):
            lo, hi = si * cs, (si + 1) * cs
            zero = jnp.zeros((1, HD), F32)
            b_f = cum_h[lo - 1:lo] if si > 0 else zero
            b_b = cum_h[hi:hi + 1] if si < nsub - 1 else zero
            b_i = jnp.where(d == 0, b_f, b_b)
            c_s, q_s, k_s, v_s = cum_h[lo:hi], q_h[lo:hi], k_h[lo:hi], v_f[lo:hi]
            qt = (q_s * jnp.exp(c_s - b_i)).astype(BF16)
            kt = (k_h * jnp.exp(jnp.minimum(b_i - cum_h, 0.0))).astype(BF16)
            valid = jnp.where(d == 0, lo - 1 - scol, scol - hi) >= 0
            s_off = jnp.where(valid, _dot_nt(qt, kt), 0.0)
            o_s = _dot(s_off.astype(BF16), v_h)
            diag = []
            for ti in range(cs):
                m = jnp.where(d == 0, ti - srow, srow - ti) >= 0
                e = jnp.exp(jnp.where(m, c_s[ti:ti + 1] - c_s, 0.0))
                term = jnp.where(m, e, 0.0) * (q_s[ti:ti + 1] * k_s)
                w = jnp.sum(term, axis=1, keepdims=True)
                diag.append(jnp.sum(w * v_s, axis=0, keepdims=True))
            rows_out.append(o_s + jnp.concatenate(diag, axis=0))
        o_ref[:, sl] = out + jnp.concatenate(rows_out, axis=0)
        ke = (k_h * jnp.exp(tail - cum_h)).astype(BF16)
        st_ref[h] = st * jnp.exp(tail) + _dot_tn(v_h, ke)


def _log_sigmoid(x):
    return jnp.minimum(x, 0.0) - jnp.log(1.0 + jnp.exp(-jnp.abs(x)))


def _mlstm_kernel(qk_ref, v_ref, g_ref, gb_ref, o_ref, c_ref, n_ref, m_ref):
    d = pl.program_id(1)
    t = T_ATTN

    @pl.when(pl.program_id(2) == 0)
    def _():
        c_ref[...] = jnp.zeros_like(c_ref)
        n_ref[...] = jnp.zeros_like(n_ref)
        m_ref[...] = jnp.zeros_like(m_ref)

    g = g_ref[...] + gb_ref[...]
    gt = g.T
    mask = _order_mask(d, t)
    tri = mask.astype(F32)
    lf = _log_sigmoid(g)
    cum_c = jnp.dot(tri, lf, precision=_HI, preferred_element_type=F32)
    cum_r = lax.dot_general(lf.T, tri, (((1,), (1,)), ((), ())), precision=_HI,
                            preferred_element_type=F32)
    scale = HD ** -0.5
    for h in range(NH):
        sl = slice(h * HD, (h + 1) * HD)
        li, lfw = h, NH + h
        i_col = jnp.where(d == 0, g[:, li:li + 1], g[:, 8 + li:8 + li + 1])
        i_row = jnp.where(d == 0, gt[li:li + 1], gt[8 + li:8 + li + 1])
        cc = jnp.where(d == 0, cum_c[:, lfw:lfw + 1], cum_c[:, 8 + lfw:8 + lfw + 1])
        cr = jnp.where(d == 0, cum_r[lfw:lfw + 1], cum_r[8 + lfw:8 + lfw + 1])
        total = jnp.where(d == 0, cr[:, t - 1:t], cr[:, 0:1])
        m_old = m_ref[h][:, 0:1]
        q = qk_ref[:, sl]
        k = qk_ref[:, BR + h * HD:BR + (h + 1) * HD]
        v = v_ref[:, sl]
        logw = jnp.where(mask, cc - cr + i_row, NEG_BIG)
        from_state = cc + m_old
        m_t = jnp.maximum(from_state, jnp.max(logw, axis=1, keepdims=True))
        w = jnp.exp(logw - m_t)
        w_state = jnp.exp(from_state - m_t)
        scores = (_dot_nt(q, k) * scale) * w
        c_old = c_ref[h]
        n_old = n_ref[h]
        num = _dot(scores.astype(BF16), v) + w_state * _dot(q, c_old.astype(BF16))
        den = (jnp.sum(scores, axis=1, keepdims=True)
               + w_state * jnp.sum(q.astype(F32) * n_old, axis=1, keepdims=True))
        o_ref[:, sl] = num / jnp.maximum(jnp.abs(den), jnp.exp(-m_t))
        m_new = jnp.maximum(total + m_old, jnp.max(total - cr + i_row, axis=1, keepdims=True))
        keep = jnp.exp(total + m_old - m_new)
        w_end = jnp.exp(total - cc + i_col - m_new)
        kw = k.astype(F32) * (scale * w_end)
        c_ref[h] = keep * c_old + _dot_tn(kw.astype(BF16), v)
        n_ref[h] = keep * n_old + jnp.sum(kw, axis=0, keepdims=True)
        m_ref[h] = jnp.broadcast_to(m_new, (1, HD))


def _scans(p, gates, qk, lg, lb, gate_b, layer, *, n_batch, seq, ctx_len):
    n = p.shape[0]
    state = pltpu.VMEM((NH, HD, HD), F32)

    def call(kernel, t, inputs, make_specs, scratch, name):
        n_ctx, n_lat = ctx_len // t, seq // t
        blk = functools.partial(_chunk_block, n_ctx=n_ctx, n_lat=n_lat, n_batch=n_batch)

        def rows(width, cblock=0):
            return pl.BlockSpec((t, width), lambda b, d, i: (blk(b, d, i), cblock))

        return pl.pallas_call(
            kernel,
            out_shape=jax.ShapeDtypeStruct((2, n, BR), F32),
            grid=(n_batch, 2, n_ctx + n_lat),
            in_specs=make_specs(rows, blk),
            out_specs=pl.BlockSpec((None, t, BR), lambda b, d, i: (d, blk(b, d, i), 0)),
            scratch_shapes=scratch,
            compiler_params=_cparams(("parallel", "arbitrary", "arbitrary")),
            name=name,
        )(*inputs)

    raw_b = call(_ret_kernel, T_ATTN, (lg, p, p, p),
                 lambda rows, blk: [pl.BlockSpec(memory_space=pltpu.SMEM),
                                    rows(BR, COL_RQ), rows(BR, COL_RK), rows(BR, COL_RV)],
                 [state], "retention_scan")
    raw_a = call(_hgrn_kernel, T_HGRN, (lb, p, p, p),
                 lambda rows, blk: [pl.BlockSpec((None, None, 1, BR), lambda b, d, i: (layer, d, 0, 0)),
                                    rows(BR, COL_HQ),
                                    pl.BlockSpec((T_HGRN, BR), lambda b, d, i: (blk(b, d, i), COL_HF + d)),
                                    rows(BR, COL_HV)],
                 [state], "hgrn2_scan")
    raw_d = call(_mlstm_kernel, T_ATTN, (qk, p, gates, gate_b),
                 lambda rows, blk: [rows(2 * BR), rows(BR, COL_MV), rows(GATE_PAD),
                                    pl.BlockSpec((None, 1, GATE_PAD), lambda b, d, i: (layer, 0, 0))],
                 [state, pltpu.VMEM((NH, 1, HD), F32), pltpu.VMEM((NH, 1, HD), F32)], "mlstm_scan")
    return raw_a, raw_b, raw_d


def _s5_kernel(u_ref, kr_ref, w_ref, ys_ref, lam_ref, o_ref, m_ref, wv_ref, xp_ref, *, n_ctx, n_steps):
    d = pl.program_id(1)
    t = S5_T
    lane = lax.broadcasted_iota(jnp.int32, (S5_HG, S5_FLAT), 1)
    kr = kr_ref[...]

    @pl.when(d == 0)
    def _():
        for s in range(t):
            sh = s * S5_HG
            blk = kr if sh == 0 else jnp.where(lane >= sh, pltpu.roll(kr, sh, 1), 0.0)
            m_ref[s * S5_HG:(s + 1) * S5_HG, :] = blk.astype(BF16)

    @pl.when(d == 1)
    def _():
        for s in range(t):
            sh = (t - 1 - s) * S5_HG
            blk = kr if sh == 0 else jnp.where(lane < S5_FLAT - sh, pltpu.roll(kr, S5_FLAT - sh, 1), 0.0)
            m_ref[s * S5_HG:(s + 1) * S5_HG, :] = blk.astype(BF16)

    u = u_ref[...]
    wv_ref[...] = _dot(u, w_ref[...])
    lr = lam_ref[0:1, :]
    li = lam_ref[1:2, :]

    def step(i, carry):
        xr, xi = carry
        c = jnp.where(d == 0, i, jnp.where(i < n_ctx, n_ctx - 1 - i, n_steps + n_ctx - 1 - i))
        row = pl.multiple_of(c * S5_BPAD, S5_BPAD)
        xp_ref[pl.ds(row, S5_BPAD), 0:128] = xr
        xp_ref[pl.ds(row, S5_BPAD), 128:256] = xi
        wr = wv_ref[pl.ds(row, S5_BPAD), 0:128]
        wi = wv_ref[pl.ds(row, S5_BPAD), 128:256]
        return lr * xr - li * xi + wr, lr * xi + li * xr + wi

    zero = jnp.zeros((S5_BPAD, 128), F32)
    lax.fori_loop(0, n_steps, step, (zero, zero))
    y = _dot(u, m_ref[...]) + _dot(xp_ref[...].astype(BF16), ys_ref[...])

    @pl.when(d == 0)
    def _():
        o_ref[...] = y

    @pl.when(d == 1)
    def _():
        o_ref[...] += y


def _s5_operators(a_re, a_im, log_dt, b_re, b_im, c_re, c_im):
    t = S5_T
    a_re, a_im = a_re.astype(F32), a_im.astype(F32)
    dt = jnp.exp(log_dt.astype(F32))[..., None]
    mag = jnp.exp(a_re * dt)
    lam_re, lam_im = mag * jnp.cos(a_im * dt), mag * jnp.sin(a_im * dt)
    den = a_re * a_re + a_im * a_im
    num_re, num_im = lam_re - 1.0, lam_im
    fr = (num_re * a_re + num_im * a_im) / den
    fi = (num_im * a_re - num_re * a_im) / den
    b_re, b_im = b_re.astype(F32), b_im.astype(F32)
    bb_re = fr[..., None] * b_re - fi[..., None] * b_im
    bb_im = fr[..., None] * b_im + fi[..., None] * b_re
    c_re, c_im = c_re.astype(F32), c_im.astype(F32)
    n = jnp.arange(t + 1, dtype=F32)[:, None, None, None]
    pmag = jnp.exp(n * (a_re * dt)[None])
    l_re = pmag * jnp.cos(n * (a_im * dt)[None])
    l_im = pmag * jnp.sin(n * (a_im * dt)[None])
    lb_re = l_re[..., None] * bb_re[None] - l_im[..., None] * bb_im[None]
    lb_im = l_re[..., None] * bb_im[None] + l_im[..., None] * bb_re[None]
    kk = (jnp.einsum("dghp,ndgpk->ndghk", c_re, lb_re[:t], precision=_HI)
          - jnp.einsum("dghp,ndgpk->ndghk", c_im, lb_im[:t], precision=_HI))
    kr_f = jnp.transpose(kk[:, 0], (1, 3, 0, 2)).reshape(S5_G, S5_HG, S5_FLAT)
    kr_b = jnp.transpose(kk[::-1, 1], (1, 3, 0, 2)).reshape(S5_G, S5_HG, S5_FLAT)
    kr = jnp.stack([kr_f, kr_b])

    def w_of(di, order):
        wr = jnp.transpose(lb_re[order, di], (1, 0, 3, 2)).reshape(S5_G, S5_FLAT, S5_P)
        wi = jnp.transpose(lb_im[order, di], (1, 0, 3, 2)).reshape(S5_G, S5_FLAT, S5_P)
        z = jnp.zeros_like(wr)
        return jnp.concatenate([wr, z, wi, z], axis=-1)

    asc = jnp.arange(t)
    w = jnp.stack([w_of(0, t - 1 - asc), w_of(1, asc)]).astype(BF16)

    def ys_of(di, order):
        lr, li = l_re[order, di], l_im[order, di]
        cr, ci = c_re[di], c_im[di]
        yr = cr[None] * lr[:, :, None, :] - ci[None] * li[:, :, None, :]
        yi = -(cr[None] * li[:, :, None, :] + ci[None] * lr[:, :, None, :])
        yr = jnp.transpose(yr, (1, 3, 0, 2)).reshape(S5_G, S5_P, S5_FLAT)
        yi = jnp.transpose(yi, (1, 3, 0, 2)).reshape(S5_G, S5_P, S5_FLAT)
        z = jnp.zeros_like(yr)
        return jnp.concatenate([yr, z, yi, z], axis=1)

    ys = jnp.stack([ys_of(0, asc + 1), ys_of(1, t - asc)]).astype(BF16)
    pad = jnp.zeros((2, S5_G, 128 - S5_P), F32)
    lam_t = jnp.stack([jnp.concatenate([l_re[t], pad], -1), jnp.concatenate([l_im[t], pad], -1)], axis=2)
    return kr, w, ys, lam_t


def _s5_scan(p, ops, *, n_batch, seq, ctx_len):
    kr, w, ys, lam_t = ops
    t = S5_T
    n_ctx, n_lat = ctx_len // t, seq // t
    n_steps = n_ctx + n_lat
    nl = n_batch * seq
    u = p[:, COL_SU * BR:(COL_SU + 1) * BR]

    def to_steps(x, nchunk):
        x = x.reshape(n_batch, nchunk, t, S5_G, S5_HG)
        return jnp.transpose(x, (3, 1, 0, 2, 4)).reshape(S5_G, nchunk, n_batch, S5_FLAT)

    uf = jnp.concatenate([to_steps(u[nl:], n_ctx), to_steps(u[:nl], n_lat)], axis=1)
    uf = jnp.pad(uf, ((0, 0), (0, 0), (0, S5_BPAD - n_batch), (0, 0))).reshape(S5_G, n_steps * S5_BPAD, S5_FLAT)
    rows = n_steps * S5_BPAD
    kern = functools.partial(_s5_kernel, n_ctx=n_ctx, n_steps=n_steps)
    yf = pl.pallas_call(
        kern,
        out_shape=jax.ShapeDtypeStruct((S5_G, rows, S5_FLAT), F32),
        grid=(S5_G, 2),
        in_specs=[pl.BlockSpec((None, rows, S5_FLAT), lambda g, d: (g, 0, 0)),
                  pl.BlockSpec((None, None, S5_HG, S5_FLAT), lambda g, d: (d, g, 0, 0)),
                  pl.BlockSpec((None, None, S5_FLAT, 256), lambda g, d: (d, g, 0, 0)),
                  pl.BlockSpec((None, None, 256, S5_FLAT), lambda g, d: (d, g, 0, 0)),
                  pl.BlockSpec((None, None, 2, 128), lambda g, d: (d, g, 0, 0))],
        out_specs=pl.BlockSpec((None, rows, S5_FLAT), lambda g, d: (g, 0, 0)),
        scratch_shapes=[pltpu.VMEM((S5_FLAT, S5_FLAT), BF16),
                        pltpu.VMEM((rows, 256), F32),
                        pltpu.VMEM((rows, 256), F32)],
        compiler_params=_cparams(("parallel", "arbitrary")),
        name="s5_scan",
    )(uf, kr, w, ys, lam_t)
    yf = yf.reshape(S5_G, n_steps, S5_BPAD, t, S5_HG)[:, :, :n_batch]

    def from_steps(x, nchunk):
        return jnp.transpose(x, (2, 1, 3, 0, 4)).reshape(n_batch * nchunk * t, BR)

    return jnp.concatenate([from_steps(yf[:, n_ctx:], n_lat), from_steps(yf[:, :n_ctx], n_ctx)], axis=0)


def _head_rms(x, g):
    parts = []
    for h in range(NH):
        xh = x[:, h * HD:(h + 1) * HD]
        parts.append(xh * lax.rsqrt(jnp.mean(jnp.square(xh), axis=-1, keepdims=True) + NORM_EPS))
    return jnp.concatenate(parts, axis=-1) * g


def _finish_kernel(ra_ref, rb_ref, rc_ref, rd_ref, ga_ref, gb_ref, u_ref, gd_ref, mg_ref, x_ref, mod_ref,
                   vec_ref, glu_w_ref, wb_ref, wo_ref, o_ref):
    vec = vec_ref[...]
    oa = _head_rms(ra_ref[0] + ra_ref[1], vec[0:1]) * _silu(ga_ref[...].astype(F32))
    ob = _head_rms(rb_ref[0] + rb_ref[1], vec[1:2]) * _silu(gb_ref[...].astype(F32))
    od = _head_rms(rd_ref[0] + rd_ref[1], vec[2:3]) * _silu(gd_ref[...].astype(F32))
    yc = rc_ref[...] + vec[3:4] * u_ref[...].astype(F32)
    yc = 0.5 * yc * (1.0 + lax.erf(yc * (2.0 ** -0.5)))
    oc = yc * _sigmoid(_dot(yc.astype(BF16), glu_w_ref[...]) + vec[4:5])
    y = None
    for j, o in enumerate((oa, ob, oc, od)):
        gate = _sigmoid(mg_ref[:, j * D_MODEL:(j + 1) * D_MODEL].astype(F32))
        term = gate * _dot(o.astype(BF16), wb_ref[j])
        y = term if y is None else y + term
    mix = _dot(y.astype(BF16), wo_ref[...])
    o_ref[...] = x_ref[...] + mod_ref[2:3, :] * mix


def _finish(raw_a, raw_b, raw_c, raw_d, p, x_all, mod, vec, glu_w, w_branch, w_out, layer, *, n_rows, n_batch,
            seq, tm=256):
    mrow = _mod_row_map(n_rows, tm, n_batch, seq)
    one = pl.Buffered(1)

    def raw(i):
        return (0, i, 0)

    def pcol(cblock):
        return pl.BlockSpec((tm, BR), lambda i: (i, cblock))

    return pl.pallas_call(
        _finish_kernel,
        out_shape=jax.ShapeDtypeStruct((n_rows, D_MODEL), F32),
        grid=(n_rows // tm,),
        in_specs=[pl.BlockSpec((2, tm, BR), raw),
                  pl.BlockSpec((2, tm, BR), raw),
                  pl.BlockSpec((tm, BR), lambda i: (i, 0)),
                  pl.BlockSpec((2, tm, BR), raw),
                  pcol(COL_HG), pcol(COL_RG), pcol(COL_SU), pcol(COL_MZ),
                  pl.BlockSpec((tm, N_BRANCH * D_MODEL), lambda i: (i, COL_MERGE)),
                  pl.BlockSpec((tm, D_MODEL), lambda i: (i, 0)),
                  pl.BlockSpec((None, 8, D_MODEL), lambda i: (mrow(i), 0, 0)),
                  pl.BlockSpec((None, 8, BR), lambda i: (layer, 0, 0), pipeline_mode=one),
                  pl.BlockSpec((None, BR, BR), lambda i: (layer, 0, 0), pipeline_mode=one),
                  pl.BlockSpec((None, N_BRANCH, BR, D_MODEL), lambda i: (layer, 0, 0, 0), pipeline_mode=one),
                  pl.BlockSpec((None, D_MODEL, D_MODEL), lambda i: (layer, 0, 0), pipeline_mode=one)],
        out_specs=pl.BlockSpec((tm, D_MODEL), lambda i: (i, 0)),
        compiler_params=_cparams(("parallel",)),
        name="finish",
    )(raw_a, raw_b, raw_c, raw_d, p, p, p, p, p, x_all, mod, vec, glu_w, w_branch, w_out)


def _mlp_kernel(x_ref, mod_ref, g_ref, w1_ref, w2_ref, fin_ref, o_ref, xn_ref, acc_ref, *, final_norm):
    j = pl.program_id(1)

    @pl.when(j == 0)
    def _():
        h = _norm_mod(x_ref[...], g_ref[...], mod_ref[3:4, :], mod_ref[4:5, :])
        xn_ref[...] = h.astype(BF16)
        acc_ref[...] = jnp.zeros_like(acc_ref)

    a = jnp.square(jnp.maximum(_dot(xn_ref[...], w1_ref[...]), 0.0))
    acc_ref[...] += _dot(a.astype(BF16), w2_ref[...])

    @pl.when(j == pl.num_programs(1) - 1)
    def _():
        y = x_ref[...] + mod_ref[5:6, :] * acc_ref[...]
        if final_norm:
            y = (y * lax.rsqrt(jnp.mean(jnp.square(y), axis=-1, keepdims=True) + NORM_EPS)) * fin_ref[...]
        o_ref[...] = y


def _mlp(x_all, mod, norm_w, w1, w2, fin_w, layer, *, n_rows, n_batch, seq, final_norm, tm=512, tf=1024):
    mrow = _mod_row_map(n_rows, tm, n_batch, seq)
    kern = functools.partial(_mlp_kernel, final_norm=final_norm)
    return pl.pallas_call(
        kern,
        out_shape=jax.ShapeDtypeStruct((n_rows, D_MODEL), F32),
        grid=(n_rows // tm, D_FF // tf),
        in_specs=[pl.BlockSpec((tm, D_MODEL), lambda i, j: (i, 0)),
                  pl.BlockSpec((None, 8, D_MODEL), lambda i, j: (mrow(i), 0, 0)),
                  pl.BlockSpec((None, 1, D_MODEL), lambda i, j: (layer, 0, 0)),
                  pl.BlockSpec((None, D_MODEL, tf), lambda i, j: (layer, 0, j)),
                  pl.BlockSpec((None, tf, D_MODEL), lambda i, j: (layer, j, 0)),
                  pl.BlockSpec((1, D_MODEL), lambda i, j: (0, 0))],
        out_specs=pl.BlockSpec((tm, D_MODEL), lambda i, j: (i, 0)),
        scratch_shapes=[pltpu.VMEM((tm, D_MODEL), BF16), pltpu.VMEM((tm, D_MODEL), F32)],
        compiler_params=_cparams(("parallel", "arbitrary")),
        name="mlp",
    )(x_all, mod, norm_w, w1, w2, fin_w.reshape(1, D_MODEL))


def _split_w_in(w):
    main = jnp.concatenate([w[..., GATE_OFF + 4 * NH:], w[..., :GATE_OFF]], axis=-1).astype(BF16)
    gate = jnp.pad(w[..., GATE_OFF:GATE_OFF + 4 * NH], ((0, 0), (0, 0), (0, GATE_PAD - 4 * NH))).astype(BF16)
    return main, gate


def kernel(x, c, ctx, c_ctx, w_mod, b_mod, norm_mix, norm_mlp, w_in, hgrn_lb_logits, hgrn_norm, ret_decay, ret_norm, s5_a_re, s5_a_im, s5_log_dt, s5_b_re, s5_b_im, s5_c_re, s5_c_im, s5_d, s5_glu_w, s5_glu_b, mlstm_conv_w, mlstm_conv_b, mlstm_gate_b, mlstm_norm, w_branch, w_out, w_ff1, w_ff2, final_norm):
    n_batch, seq, _ = x.shape
    ctx_len = ctx.shape[1]
    depth = w_in.shape[0]
    nl = n_batch * seq
    dims = dict(n_batch=n_batch, seq=seq)

    p_lb = jax.nn.softmax(hgrn_lb_logits.astype(F32), axis=0)
    lower_bounds = (jnp.cumsum(p_lb, axis=0) - p_lb[0]).reshape(depth, 2, 1, BR)
    lg = jnp.log1p(-jnp.exp(ret_decay.astype(F32)))
    gate_b = jnp.pad(mlstm_gate_b.reshape(depth, 1, 4 * NH).astype(F32), ((0, 0), (0, 0), (0, GATE_PAD - 4 * NH)))
    vec = jnp.stack([hgrn_norm, ret_norm, mlstm_norm, s5_d, s5_glu_b], axis=1).astype(F32)
    vec = jnp.pad(vec, ((0, 0), (0, 8 - vec.shape[1]), (0, 0)))
    w_main, w_gate = _split_w_in(w_in)
    glu_w, wb, wo = s5_glu_w.astype(BF16), w_branch.astype(BF16), w_out.astype(BF16)
    w1, w2 = w_ff1.astype(BF16), w_ff2.astype(BF16)
    norm_mix3 = norm_mix.reshape(depth, 1, D_MODEL)
    norm_mlp3 = norm_mlp.reshape(depth, 1, D_MODEL)
    conv_w = mlstm_conv_w.reshape(depth, 9, 2 * BR)
    conv_b = mlstm_conv_b.reshape(depth, 1, 2 * BR)

    c_pad = jnp.zeros((8, D_MODEL), F32).at[:n_batch].set(c).at[n_batch].set(c_ctx)
    x_all = jnp.concatenate([x.reshape(nl, D_MODEL), ctx.reshape(n_batch * ctx_len, D_MODEL)], axis=0)

    for l in range(depth):
        last = l == depth - 1
        mod = _modulation(c_pad, w_mod, b_mod, l)
        mod = jnp.pad(mod[:n_batch + 1].reshape(n_batch + 1, N_MOD, D_MODEL), ((0, 0), (0, 8 - N_MOD), (0, 0)))
        p, gates = _in_proj(x_all, mod, norm_mix3, w_main, w_gate, l, **dims)
        qk = _mlstm_conv(p, conv_w, conv_b, l, ctx_len=ctx_len, **dims)
        raw_a, raw_b, raw_d = _scans(p, gates, qk, lg[l], lower_bounds, gate_b, l, ctx_len=ctx_len, **dims)
        ops = _s5_operators(s5_a_re[l], s5_a_im[l], s5_log_dt[l], s5_b_re[l], s5_b_im[l], s5_c_re[l], s5_c_im[l])
        raw_c = _s5_scan(p, ops, ctx_len=ctx_len, **dims)
        n_rows = nl if last else x_all.shape[0]
        x_mid = _finish(raw_a, raw_b, raw_c, raw_d, p, x_all, mod, vec, glu_w, wb, wo, l, n_rows=n_rows, **dims)
        x_all = _mlp(x_mid, mod, norm_mlp3, w1, w2, final_norm, l, n_rows=n_rows, final_norm=last, **dims)
    return x_all.reshape(n_batch, seq, D_MODEL)
```

```python
import functools
import math

import jax
import jax.numpy as jnp
from jax import lax
from jax.experimental import pallas as pl
from jax.experimental.pallas import tpu as pltpu

F32 = jnp.float32
BF16 = jnp.bfloat16

D_MODEL = 2048
N_BRANCH = 4
BR = D_MODEL // N_BRANCH
HD = 128
NH = BR // HD
S5_HG = 16
S5_G = BR // S5_HG
S5_P = 64
D_FF = 4 * D_MODEL
N_MOD = 6
GRID_W = 64
NORM_EPS = 1e-6
NEG_BIG = -1e30
F_TINY = 1e-30
S5_DT_MIN = 1e-3

COL_MERGE = 0
COL_HQ, COL_HF, COL_HV, COL_HG = 16, 17, 19, 20
COL_RQ, COL_RK, COL_RV, COL_RG = 21, 22, 23, 24
COL_SU = 25
COL_MQK, COL_MV, COL_MZ = 26, 28, 29
P_WIDTH = 30 * BR
GATE_OFF = 14 * BR
GATE_PAD = 128

T_HGRN = 128
T_ATTN = 256
S5_T = 128
S5_PP = 128
S5_ROWS = 40
VMEM_LIMIT = 56 * 1024 * 1024

_HI = lax.Precision.HIGHEST


def _cparams(sem):
    return pltpu.CompilerParams(dimension_semantics=sem, vmem_limit_bytes=VMEM_LIMIT)


def _dot(a, b):
    return jnp.dot(a, b, preferred_element_type=F32)


def _dot_nt(a, b):
    return lax.dot_general(a, b, (((1,), (1,)), ((), ())), preferred_element_type=F32)


def _dot_tn(a, b):
    return lax.dot_general(a, b, (((0,), (0,)), ((), ())), preferred_element_type=F32)


def _split3(x):
    hi = x.astype(BF16)
    r1 = x - hi.astype(F32)
    mid = r1.astype(BF16)
    return hi, mid, (r1 - mid.astype(F32)).astype(BF16)


def _cumsum_rows(tri, x):
    hi, mid, lo = _split3(x)
    return _dot(tri, hi) + (_dot(tri, mid) + _dot(tri, lo))


def _sigmoid(x):
    return 1.0 / (1.0 + jnp.exp(-x))


def _silu(x):
    return x * _sigmoid(x)


def _mod_kernel(c_ref, w_ref, b_ref, o_ref):
    a = _silu(c_ref[...]).astype(BF16)
    o_ref[...] = _dot(a, w_ref[...].astype(BF16)) + b_ref[...]


def _modulation(c_pad, w_mod, b_mod, layer):
    n = w_mod.shape[2]
    tn = 1024
    return pl.pallas_call(
        _mod_kernel,
        out_shape=jax.ShapeDtypeStruct((8, n), F32),
        grid=(n // tn,),
        in_specs=[pl.BlockSpec((8, D_MODEL), lambda j: (0, 0)),
                  pl.BlockSpec((None, D_MODEL, tn), lambda j: (layer, 0, j)),
                  pl.BlockSpec((None, 1, tn), lambda j: (layer, 0, j))],
        out_specs=pl.BlockSpec((8, tn), lambda j: (0, j)),
        compiler_params=_cparams(("parallel",)),
        name="modulation",
    )(c_pad, w_mod, b_mod.reshape(b_mod.shape[0], 1, n))


def _norm_mod(x, g, shift, scale):
    y = x * lax.rsqrt(jnp.mean(jnp.square(x), axis=-1, keepdims=True) + NORM_EPS)
    return (y * g) * (1.0 + scale) + shift


def _inproj_kernel(x_ref, mod_ref, g_ref, w_ref, wg_ref, p_ref, gate_ref, xn_ref):
    @pl.when(pl.program_id(1) == 0)
    def _():
        h = _norm_mod(x_ref[...], g_ref[...], mod_ref[0:1, :], mod_ref[1:2, :])
        xn_ref[...] = h.astype(BF16)
        gate_ref[...] = _dot(xn_ref[...], wg_ref[...])

    p_ref[...] = _dot(xn_ref[...], w_ref[...]).astype(p_ref.dtype)


def _mod_row_map(n_rows, tm, n_batch, seq):
    assert n_rows % tm == 0 and seq % tm == 0, (n_rows, seq, tm)
    n_lat_tiles, tiles_per_batch = n_batch * seq // tm, seq // tm

    def row(i):
        return jnp.where(i < n_lat_tiles, i // tiles_per_batch, n_batch)
    return row


def _in_proj(x_all, mod, norm_w, w_main, w_gate, layer, *, n_batch, seq, tm=1024, tn=1024):
    n = x_all.shape[0]
    mrow = _mod_row_map(n, tm, n_batch, seq)
    return pl.pallas_call(
        _inproj_kernel,
        out_shape=(jax.ShapeDtypeStruct((n, P_WIDTH), BF16),
                   jax.ShapeDtypeStruct((n, GATE_PAD), F32)),
        grid=(n // tm, P_WIDTH // tn),
        in_specs=[pl.BlockSpec((tm, D_MODEL), lambda i, j: (i, 0)),
                  pl.BlockSpec((None, 8, D_MODEL), lambda i, j: (mrow(i), 0, 0)),
                  pl.BlockSpec((None, 1, D_MODEL), lambda i, j: (layer, 0, 0)),
                  pl.BlockSpec((None, D_MODEL, tn), lambda i, j: (layer, 0, j)),
                  pl.BlockSpec((None, D_MODEL, GATE_PAD), lambda i, j: (layer, 0, 0))],
        out_specs=(pl.BlockSpec((tm, tn), lambda i, j: (i, j)),
                   pl.BlockSpec((tm, GATE_PAD), lambda i, j: (i, 0))),
        scratch_shapes=[pltpu.VMEM((tm, D_MODEL), BF16)],
        compiler_params=_cparams(("parallel", "arbitrary")),
        name="in_proj",
    )(x_all, mod, norm_w, w_main, w_gate)


CONV_ROWS = 256


def _conv_kernel(up_ref, x_ref, dn_ref, w_ref, b_ref, o_ref, *, n_lat_blocks, blocks_per_img):
    rb = pl.program_id(0)
    x = x_ref[...].astype(F32)
    w = w_ref[...]
    n = CONV_ROWS

    @pl.when(rb < n_lat_blocks)
    def _():
        rr = rb % blocks_per_img
        up = jnp.where(rr > 0, up_ref[...].astype(F32), 0.0)
        dn = jnp.where(rr < blocks_per_img - 1, dn_ref[...].astype(F32), 0.0)
        ext = jnp.concatenate([up, x, dn], axis=0)
        ne = n + 2 * GRID_W
        col = lax.broadcasted_iota(jnp.int32, (ne, 1), 0) % GRID_W
        xl = jnp.where(col >= 1, pltpu.roll(ext, 1, 0), 0.0)
        xr = jnp.where(col <= GRID_W - 2, pltpu.roll(ext, ne - 1, 0), 0.0)
        acc = None
        for i in range(3):
            y = w[3 * i:3 * i + 1] * xl + w[3 * i + 1:3 * i + 2] * ext + w[3 * i + 2:3 * i + 3] * xr
            part = y[i * GRID_W:i * GRID_W + n]
            acc = part if acc is None else acc + part
        o_ref[...] = _silu(acc + b_ref[...]).astype(o_ref.dtype)

    @pl.when(rb >= n_lat_blocks)
    def _():
        t = lax.broadcasted_iota(jnp.int32, (n, 1), 0)
        xl = jnp.where(t >= 1, pltpu.roll(x, 1, 0), 0.0)
        xr = jnp.where(t <= n - 2, pltpu.roll(x, n - 1, 0), 0.0)
        acc = w[3:4] * xl + w[4:5] * x + w[5:6] * xr
        o_ref[...] = _silu(acc + b_ref[...]).astype(o_ref.dtype)


def _mlstm_conv(p, conv_w, conv_b, layer, *, n_batch, seq, ctx_len, cb=256):
    assert ctx_len == CONV_ROWS and seq % CONV_ROWS == 0
    n = p.shape[0]
    n_blocks = n // CONV_ROWS
    hb = CONV_ROWS // GRID_W
    n_halo = n // GRID_W
    c0 = COL_MQK * BR // cb
    kern = functools.partial(_conv_kernel, n_lat_blocks=n_batch * seq // CONV_ROWS,
                             blocks_per_img=seq // CONV_ROWS)
    return pl.pallas_call(
        kern,
        out_shape=jax.ShapeDtypeStruct((n, 2 * BR), BF16),
        grid=(n_blocks, 2 * BR // cb),
        in_specs=[pl.BlockSpec((GRID_W, cb), lambda r, c: (jnp.maximum(r * hb - 1, 0), c0 + c)),
                  pl.BlockSpec((CONV_ROWS, cb), lambda r, c: (r, c0 + c)),
                  pl.BlockSpec((GRID_W, cb), lambda r, c: (jnp.minimum((r + 1) * hb, n_halo - 1), c0 + c)),
                  pl.BlockSpec((None, 9, cb), lambda r, c: (layer, 0, c)),
                  pl.BlockSpec((None, 1, cb), lambda r, c: (layer, 0, c))],
        out_specs=pl.BlockSpec((CONV_ROWS, cb), lambda r, c: (r, c)),
        compiler_params=_cparams(("parallel", "parallel")),
        name="mlstm_conv",
    )(p, p, p, conv_w, conv_b)


def _chunk_block(b, d, i, *, n_ctx, n_lat, n_batch):
    ctx_j = jnp.where(d == 0, i, n_ctx - 1 - i)
    lat_j = jnp.where(d == 0, i - n_ctx, n_ctx + n_lat - 1 - i)
    return jnp.where(i < n_ctx, n_batch * n_lat + b * n_ctx + ctx_j, b * n_lat + lat_j)


def _order_mask(d, t):
    r = lax.broadcasted_iota(jnp.int32, (t, t), 0)
    c = lax.broadcasted_iota(jnp.int32, (t, t), 1)
    return jnp.where(d == 0, r - c, c - r) >= 0


def _ret_kernel(lg_ref, q_ref, k_ref, v_ref, o_ref, s_ref):
    d = pl.program_id(1)
    t = T_ATTN

    @pl.when(pl.program_id(2) == 0)
    def _():
        s_ref[...] = jnp.zeros_like(s_ref)

    r = lax.broadcasted_iota(jnp.int32, (t, t), 0)
    c = lax.broadcasted_iota(jnp.int32, (t, t), 1)
    rel = jnp.where(d == 0, r - c, c - r)
    relf = jnp.maximum(rel, 0).astype(F32)
    tt = lax.broadcasted_iota(jnp.int32, (t, 1), 0)
    pos = jnp.where(d == 0, tt, t - 1 - tt).astype(F32)
    scale = HD ** -0.5
    for h in range(NH):
        sl = slice(h * HD, (h + 1) * HD)
        lg = lg_ref[d, h]
        q = q_ref[:, sl]
        k = k_ref[:, sl]
        v = v_ref[:, sl]
        intra = jnp.where(rel >= 0, jnp.exp(lg * relf), 0.0)
        scores = (_dot_nt(q, k) * scale) * intra
        s_old = s_ref[h]
        out = _dot(scores.astype(BF16), v) + jnp.exp(lg * (pos + 1.0)) * _dot(q, s_old.astype(BF16))
        kd = (k.astype(F32) * (scale * jnp.exp(lg * (t - 1.0 - pos)))).astype(BF16)
        s_ref[h] = jnp.exp(lg * t) * s_old + _dot_tn(kd, v)
        o_ref[:, sl] = out


def _hgrn_direction(fwd, lb_ref, q_ref, z_ref, v_ref, o_ref, st_ref):
    t = T_HGRN
    lb = lb_ref[...]
    z = z_ref[...].astype(F32)
    f = lb + (1.0 - lb) * _sigmoid(z)
    logf = jnp.log(jnp.maximum(f, F_TINY))
    kk = (1.0 - lb) * _sigmoid(-z)
    qq = _silu(q_ref[...].astype(F32))
    r2 = lax.broadcasted_iota(jnp.int32, (t, t), 0)
    c2 = lax.broadcasted_iota(jnp.int32, (t, t), 1)
    tri = jnp.where((c2 <= r2) if fwd else (c2 >= r2), 1.0, 0.0).astype(BF16)
    cum = _cumsum_rows(tri, logf)
    row = lax.broadcasted_iota(jnp.int32, (t, 1), 0)
    if fwd:
        e_ref = jnp.where(row == 0, 0.0, pltpu.roll(cum, 1, 0))
    else:
        e_ref = jnp.where(row == t - 1, 0.0, pltpu.roll(cum, t - 1, 0))
    f_ref = cum
    levels = []
    w = 1
    while w < t:
        q_l = (qq * jnp.exp(cum - e_ref)).astype(BF16)
        k_l = (kk * jnp.exp(f_ref - cum)).astype(BF16)
        late, early = (w, 0) if fwd else (0, w)
        key = ((r2 ^ c2) & ~(2 * w - 1)) | ((r2 & w) ^ late) | ((c2 & w) ^ early)
        levels.append((q_l, k_l, key == 0))
        upper = (row & w) != 0
        if fwd:
            e_ref = jnp.where(upper, pltpu.roll(e_ref, w, 0), e_ref)
            f_ref = jnp.where(upper, f_ref, pltpu.roll(f_ref, t - w, 0))
        else:
            e_ref = jnp.where(upper, e_ref, pltpu.roll(e_ref, t - w, 0))
            f_ref = jnp.where(upper, pltpu.roll(f_ref, w, 0), f_ref)
        w *= 2
    q_b = qq.astype(BF16)
    k_b = kk.astype(BF16)
    tail = cum[t - 1:t] if fwd else cum[0:1]
    q_state = (qq * jnp.exp(cum)).astype(BF16)
    k_state = (kk * jnp.exp(tail - cum)).astype(BF16)
    decay = jnp.exp(tail)
    for h in range(NH):
        sl = slice(h * HD, (h + 1) * HD)
        v_h = v_ref[:, sl]
        scores = jnp.where(r2 == c2, _dot_nt(q_b[:, sl], k_b[:, sl]), 0.0)
        for q_l, k_l, m in levels:
            scores = jnp.where(m, _dot_nt(q_l[:, sl], k_l[:, sl]), scores)
        st = st_ref[h]
        o_ref[:, sl] = _dot(scores.astype(BF16), v_h) + _dot_nt(q_state[:, sl], st.astype(BF16))
        st_ref[h] = st * decay[:, sl] + _dot_tn(v_h, k_state[:, sl])


def _hgrn_kernel(lb_ref, q_ref, z_ref, v_ref, o_ref, st_ref):
    d = pl.program_id(1)

    @pl.when(pl.program_id(2) == 0)
    def _():
        st_ref[...] = jnp.zeros_like(st_ref)

    @pl.when(d == 0)
    def _():
        _hgrn_direction(True, lb_ref, q_ref, z_ref, v_ref, o_ref, st_ref)

    @pl.when(d == 1)
    def _():
        _hgrn_direction(False, lb_ref, q_ref, z_ref, v_ref, o_ref, st_ref)


def _log_sigmoid(x):
    return jnp.minimum(x, 0.0) - jnp.log(1.0 + jnp.exp(-jnp.abs(x)))


def _mlstm_kernel(qk_ref, v_ref, g_ref, gb_ref, o_ref, c_ref, n_ref, m_ref):
    d = pl.program_id(1)
    t = T_ATTN

    @pl.when(pl.program_id(2) == 0)
    def _():
        c_ref[...] = jnp.zeros_like(c_ref)
        n_ref[...] = jnp.zeros_like(n_ref)
        m_ref[...] = jnp.zeros_like(m_ref)

    g = g_ref[...] + gb_ref[...]
    gt = g.T
    mask = _order_mask(d, t)
    tri = jnp.where(mask, 1.0, 0.0).astype(BF16)
    lf = _log_sigmoid(g)
    cum_c = _cumsum_rows(tri, lf)
    lt_hi, lt_mid, lt_lo = _split3(lf.T)
    cum_r = _dot_nt(lt_hi, tri) + (_dot_nt(lt_mid, tri) + _dot_nt(lt_lo, tri))
    scale = HD ** -0.5
    for h in range(NH):
        sl = slice(h * HD, (h + 1) * HD)
        li, lfw = h, NH + h
        i_col = jnp.where(d == 0, g[:, li:li + 1], g[:, 8 + li:8 + li + 1])
        i_row = jnp.where(d == 0, gt[li:li + 1], gt[8 + li:8 + li + 1])
        cc = jnp.where(d == 0, cum_c[:, lfw:lfw + 1], cum_c[:, 8 + lfw:8 + lfw + 1])
        cr = jnp.where(d == 0, cum_r[lfw:lfw + 1], cum_r[8 + lfw:8 + lfw + 1])
        total = jnp.where(d == 0, cr[:, t - 1:t], cr[:, 0:1])
        m_old = m_ref[h][:, 0:1]
        q = qk_ref[:, sl]
        k = qk_ref[:, BR + h * HD:BR + (h + 1) * HD]
        v = v_ref[:, sl]
        logw = jnp.where(mask, cc - cr + i_row, NEG_BIG)
        from_state = cc + m_old
        m_t = jnp.maximum(from_state, jnp.max(logw, axis=1, keepdims=True))
        w = jnp.exp(logw - m_t)
        w_state = jnp.exp(from_state - m_t)
        scores = (_dot_nt(q, k) * scale) * w
        c_old = c_ref[h]
        n_old = n_ref[h]
        num = _dot(scores.astype(BF16), v) + w_state * _dot(q, c_old.astype(BF16))
        den = (jnp.sum(scores, axis=1, keepdims=True)
               + w_state * jnp.sum(q.astype(F32) * n_old, axis=1, keepdims=True))
        o_ref[:, sl] = num / jnp.maximum(jnp.abs(den), jnp.exp(-m_t))
        m_new = jnp.maximum(total + m_old, jnp.max(total - cr + i_row, axis=1, keepdims=True))
        keep = jnp.exp(total + m_old - m_new)
        w_end = jnp.exp(total - cc + i_col - m_new)
        kw = k.astype(F32) * (scale * w_end)
        c_ref[h] = keep * c_old + _dot_tn(kw.astype(BF16), v)
        n_ref[h] = keep * n_old + jnp.sum(kw, axis=0, keepdims=True)
        m_ref[h] = jnp.broadcast_to(m_new, (1, HD))


def _scans(p, gates, qk, lg, lb, gate_b, layer, *, n_batch, seq, ctx_len):
    n = p.shape[0]
    state = pltpu.VMEM((NH, HD, HD), F32)

    def call(kernel, t, inputs, make_specs, scratch, name):
        n_ctx, n_lat = ctx_len // t, seq // t
        blk = functools.partial(_chunk_block, n_ctx=n_ctx, n_lat=n_lat, n_batch=n_batch)

        def rows(width, cblock=0):
            return pl.BlockSpec((t, width), lambda b, d, i: (blk(b, d, i), cblock))

        return pl.pallas_call(
            kernel,
            out_shape=jax.ShapeDtypeStruct((2, n, BR), F32),
            grid=(n_batch, 2, n_ctx + n_lat),
            in_specs=make_specs(rows, blk),
            out_specs=pl.BlockSpec((None, t, BR), lambda b, d, i: (d, blk(b, d, i), 0)),
            scratch_shapes=scratch,
            compiler_params=_cparams(("parallel", "arbitrary", "arbitrary")),
            name=name,
        )(*inputs)

    raw_b = call(_ret_kernel, T_ATTN, (lg, p, p, p),
                 lambda rows, blk: [pl.BlockSpec(memory_space=pltpu.SMEM),
                                    rows(BR, COL_RQ), rows(BR, COL_RK), rows(BR, COL_RV)],
                 [state], "retention_scan")
    raw_a = call(_hgrn_kernel, T_HGRN, (lb, p, p, p),
                 lambda rows, blk: [pl.BlockSpec((None, None, 1, BR), lambda b, d, i: (layer, d, 0, 0)),
                                    rows(BR, COL_HQ),
                                    pl.BlockSpec((T_HGRN, BR), lambda b, d, i: (blk(b, d, i), COL_HF + d)),
                                    rows(BR, COL_HV)],
                 [state], "hgrn2_scan")
    raw_d = call(_mlstm_kernel, T_ATTN, (qk, p, gates, gate_b),
                 lambda rows, blk: [rows(2 * BR), rows(BR, COL_MV), rows(GATE_PAD),
                                    pl.BlockSpec((None, 1, GATE_PAD), lambda b, d, i: (layer, 0, 0))],
                 [state, pltpu.VMEM((NH, 1, HD), F32), pltpu.VMEM((NH, 1, HD), F32)], "mlstm_scan")
    return raw_a, raw_b, raw_d


def _cmul(ar, ai, br, bi):
    return ar * br - ai * bi, ar * bi + ai * br


def _pow_table(lam_r, lam_i, expo):
    pr = jnp.ones(expo.shape, F32)
    pi = jnp.zeros(expo.shape, F32)
    ar = jnp.broadcast_to(lam_r, expo.shape)
    ai = jnp.broadcast_to(lam_i, expo.shape)
    for k in range(S5_T.bit_length() - 1):
        nr, ni = _cmul(pr, pi, ar, ai)
        bit = (expo & (1 << k)) != 0
        pr, pi = jnp.where(bit, nr, pr), jnp.where(bit, ni, pi)
        ar, ai = _cmul(ar, ai, ar, ai)
    return pr, pi


def _s5_kernel(u_ref, col_ref, row_ref, o_ref, m_ref, w_ref, ys_ref, z_ref, wr_ref, wi_ref, xr_ref, xi_ref, *,
               n_batch, n_ctx, n_lat):
    d = pl.program_id(1)
    t = S5_T
    col = col_ref[...]
    rowp = row_ref[...]
    lam_rc, lam_ic = col[:, 0:1], col[:, 1:2]
    bbr, bbi = rowp[0:S5_HG], rowp[S5_HG:2 * S5_HG]
    lam_rr, lam_ir = rowp[2 * S5_HG:2 * S5_HG + 1], rowp[2 * S5_HG + 1:2 * S5_HG + 2]
    lane = lax.broadcasted_iota(jnp.int32, (t, t), 1)
    sub = lax.broadcasted_iota(jnp.int32, (t, t), 0)

    pr, pi = _pow_table(lam_rc, lam_ic, jnp.where(d == 0, lane, t - 1 - lane))
    qr, qi = _cmul(pr, pi, lam_rc, lam_ic)
    for h in range(S5_HG):
        cr, ci = col[:, 2 + h:3 + h], col[:, 2 + S5_HG + h:3 + S5_HG + h]
        hs = slice(h * t, (h + 1) * t)
        z_ref[0:S5_PP, hs] = cr * pr - ci * pi
        z_ref[S5_PP:2 * S5_PP, hs] = -(cr * pi + ci * pr)
        ys_ref[0:S5_PP, hs] = (cr * qr - ci * qi).astype(BF16)
        ys_ref[S5_PP:2 * S5_PP, hs] = (-(cr * qi + ci * qr)).astype(BF16)
    b_hi, b_mid, _ = _split3(jnp.concatenate([bbr, bbi], axis=1))
    z_hi, z_mid, _ = _split3(z_ref[...])
    krow = _dot(b_hi, z_hi) + (_dot(b_hi, z_mid) + _dot(b_mid, z_hi))

    def build_m(shift, keep):
        for hp in range(S5_HG):
            for h in range(S5_HG):
                tile = jnp.broadcast_to(krow[hp:hp + 1, h * t:(h + 1) * t], (t, t))
                tile = pltpu.roll(tile, shift, 1, stride=1, stride_axis=0)
                m_ref[hp * t:(hp + 1) * t, h * t:(h + 1) * t] = jnp.where(keep, tile, 0.0).astype(BF16)

    @pl.when(d == 0)
    def _():
        build_m(0, lane >= sub)

    @pl.when(d == 1)
    def _():
        build_m(1, lane <= sub)

    tr, ti = _pow_table(lam_rr, lam_ir, jnp.where(d == 0, t - 1 - sub, sub))
    for hp in range(S5_HG):
        br, bi = bbr[hp:hp + 1], bbi[hp:hp + 1]
        w_ref[hp * t:(hp + 1) * t, 0:S5_PP] = (tr * br - ti * bi).astype(BF16)
        w_ref[hp * t:(hp + 1) * t, S5_PP:2 * S5_PP] = (tr * bi + ti * br).astype(BF16)

    u = jnp.concatenate([u_ref[hp] for hp in range(S5_HG)], axis=1)
    wv = _dot(u, w_ref[...])
    wr_ref[...] = wv[:, 0:S5_PP]
    wi_ref[...] = wv[:, S5_PP:2 * S5_PP]
    lr, li = lam_rr, lam_ir
    for _ in range(t.bit_length() - 1):
        lr, li = _cmul(lr, li, lr, li)

    def make_step(base, stride, count):
        def step(i, carry):
            xr, xi = carry
            j = jnp.where(d == 0, i, count - 1 - i)
            idx = pl.ds(base + j, n_batch, stride=stride)
            xr_ref[idx, :] = xr
            xi_ref[idx, :] = xi
            nr, ni = _cmul(xr, xi, lr, li)
            return nr + wr_ref[idx, :], ni + wi_ref[idx, :]
        return step

    zero = jnp.zeros((n_batch, S5_PP), F32)
    carry = lax.fori_loop(0, n_ctx, make_step(n_batch * n_lat, n_ctx, n_ctx), (zero, zero))
    lax.fori_loop(0, n_lat, make_step(0, n_lat, n_lat), carry)
    x_prev = jnp.concatenate([xr_ref[...], xi_ref[...]], axis=1).astype(BF16)
    y = _dot(u, m_ref[...]) + _dot(x_prev, ys_ref[...])

    @pl.when(d == 0)
    def _():
        for h in range(S5_HG):
            o_ref[h] = y[:, h * t:(h + 1) * t]

    @pl.when(d == 1)
    def _():
        for h in range(S5_HG):
            o_ref[h] += y[:, h * t:(h + 1) * t]


def _s5_params(a_re, a_im, log_dt, b_re, b_im, c_re, c_im):
    a_re, a_im = a_re.astype(F32), a_im.astype(F32)
    dt = jnp.exp(log_dt.astype(F32))[..., None]
    mag = jnp.exp(a_re * dt)
    lam_re, lam_im = mag * jnp.cos(a_im * dt), mag * jnp.sin(a_im * dt)
    den = a_re * a_re + a_im * a_im
    num_re, num_im = lam_re - 1.0, lam_im
    fr = (num_re * a_re + num_im * a_im) / den
    fi = (num_im * a_re - num_re * a_im) / den
    b_re, b_im = b_re.astype(F32), b_im.astype(F32)
    bb_re = fr[..., None] * b_re - fi[..., None] * b_im
    bb_im = fr[..., None] * b_im + fi[..., None] * b_re
    c_re_t = jnp.swapaxes(c_re.astype(F32), -1, -2)
    c_im_t = jnp.swapaxes(c_im.astype(F32), -1, -2)
    col = jnp.concatenate([lam_re[..., None], lam_im[..., None], c_re_t, c_im_t], axis=-1)
    col = jnp.pad(col, ((0, 0),) * 3 + ((0, S5_PP - S5_P), (0, 128 - col.shape[-1])))
    row = jnp.concatenate([jnp.swapaxes(bb_re, -1, -2), jnp.swapaxes(bb_im, -1, -2),
                           lam_re[..., None, :], lam_im[..., None, :]], axis=-2)
    row = jnp.pad(row, ((0, 0),) * 3 + ((0, S5_ROWS - row.shape[-2]), (0, S5_PP - S5_P)))
    return col, row


def _s5_scan(p, col, row, layer, *, n_batch, seq, ctx_len):
    t = S5_T
    n = p.shape[0]
    nc = n // t
    n_ctx, n_lat = ctx_len // t, seq // t
    ut = p[:, COL_SU * BR:(COL_SU + 1) * BR].T.reshape(S5_G, S5_HG, nc, t)
    kern = functools.partial(_s5_kernel, n_batch=n_batch, n_ctx=n_ctx, n_lat=n_lat)
    yt = pl.pallas_call(
        kern,
        out_shape=jax.ShapeDtypeStruct((S5_G, S5_HG, nc, t), F32),
        grid=(S5_G, 2),
        in_specs=[pl.BlockSpec((None, S5_HG, nc, t), lambda g, d: (g, 0, 0, 0)),
                  pl.BlockSpec((None, None, None, S5_PP, 128), lambda g, d: (layer, d, g, 0, 0)),
                  pl.BlockSpec((None, None, None, S5_ROWS, S5_PP), lambda g, d: (layer, d, g, 0, 0))],
        out_specs=pl.BlockSpec((None, S5_HG, nc, t), lambda g, d: (g, 0, 0, 0)),
        scratch_shapes=[pltpu.VMEM((S5_HG * t, S5_HG * t), BF16),
                        pltpu.VMEM((S5_HG * t, 2 * S5_PP), BF16),
                        pltpu.VMEM((2 * S5_PP, S5_HG * t), BF16),
                        pltpu.VMEM((2 * S5_PP, S5_HG * t), F32),
                        pltpu.VMEM((nc, S5_PP), F32), pltpu.VMEM((nc, S5_PP), F32),
                        pltpu.VMEM((nc, S5_PP), F32), pltpu.VMEM((nc, S5_PP), F32)],
        compiler_params=_cparams(("parallel", "arbitrary")),
        name="s5_scan",
    )(ut, col, row)
    return yt.reshape(BR, n).T


def _head_rms(x, g):
    parts = []
    for h in range(NH):
        xh = x[:, h * HD:(h + 1) * HD]
        parts.append(xh * lax.rsqrt(jnp.mean(jnp.square(xh), axis=-1, keepdims=True) + NORM_EPS))
    return jnp.concatenate(parts, axis=-1) * g


def _finish_kernel(ra_ref, rb_ref, rc_ref, rd_ref, ga_ref, gb_ref, u_ref, gd_ref, mg_ref, x_ref, mod_ref,
                   vec_ref, glu_w_ref, wb_ref, wo_ref, o_ref):
    vec = vec_ref[...]
    oa = _head_rms(ra_ref[0] + ra_ref[1], vec[0:1]) * _silu(ga_ref[...].astype(F32))
    ob = _head_rms(rb_ref[0] + rb_ref[1], vec[1:2]) * _silu(gb_ref[...].astype(F32))
    od = _head_rms(rd_ref[0] + rd_ref[1], vec[2:3]) * _silu(gd_ref[...].astype(F32))
    yc = rc_ref[...] + vec[3:4] * u_ref[...].astype(F32)
    yc = 0.5 * yc * (1.0 + lax.erf(yc * (2.0 ** -0.5)))
    oc = yc * _sigmoid(_dot(yc.astype(BF16), glu_w_ref[...]) + vec[4:5])
    y = None
    for j, o in enumerate((oa, ob, oc, od)):
        gate = _sigmoid(mg_ref[:, j * D_MODEL:(j + 1) * D_MODEL].astype(F32))
        term = gate * _dot(o.astype(BF16), wb_ref[j])
        y = term if y is None else y + term
    mix = _dot(y.astype(BF16), wo_ref[...])
    o_ref[...] = x_ref[...] + mod_ref[2:3, :] * mix


def _finish(raw_a, raw_b, raw_c, raw_d, p, x_all, mod, vec, glu_w, w_branch, w_out, layer, *, n_rows, n_batch,
            seq, tm=256):
    mrow = _mod_row_map(n_rows, tm, n_batch, seq)
    one = pl.Buffered(1)

    def raw(i):
        return (0, i, 0)

    def pcol(cblock):
        return pl.BlockSpec((tm, BR), lambda i: (i, cblock))

    return pl.pallas_call(
        _finish_kernel,
        out_shape=jax.ShapeDtypeStruct((n_rows, D_MODEL), F32),
        grid=(n_rows // tm,),
        in_specs=[pl.BlockSpec((2, tm, BR), raw),
                  pl.BlockSpec((2, tm, BR), raw),
                  pl.BlockSpec((tm, BR), lambda i: (i, 0)),
                  pl.BlockSpec((2, tm, BR), raw),
                  pcol(COL_HG), pcol(COL_RG), pcol(COL_SU), pcol(COL_MZ),
                  pl.BlockSpec((tm, N_BRANCH * D_MODEL), lambda i: (i, COL_MERGE)),
                  pl.BlockSpec((tm, D_MODEL), lambda i: (i, 0)),
                  pl.BlockSpec((None, 8, D_MODEL), lambda i: (mrow(i), 0, 0)),
                  pl.BlockSpec((None, 8, BR), lambda i: (layer, 0, 0), pipeline_mode=one),
                  pl.BlockSpec((None, BR, BR), lambda i: (layer, 0, 0), pipeline_mode=one),
                  pl.BlockSpec((None, N_BRANCH, BR, D_MODEL), lambda i: (layer, 0, 0, 0), pipeline_mode=one),
                  pl.BlockSpec((None, D_MODEL, D_MODEL), lambda i: (layer, 0, 0), pipeline_mode=one)],
        out_specs=pl.BlockSpec((tm, D_MODEL), lambda i: (i, 0)),
        compiler_params=_cparams(("parallel",)),
        name="finish",
    )(raw_a, raw_b, raw_c, raw_d, p, p, p, p, p, x_all, mod, vec, glu_w, w_branch, w_out)


def _mlp_kernel(x_ref, mod_ref, g_ref, w1_ref, w2_ref, fin_ref, o_ref, xn_ref, acc_ref, *, final_norm):
    j = pl.program_id(1)

    @pl.when(j == 0)
    def _():
        h = _norm_mod(x_ref[...], g_ref[...], mod_ref[3:4, :], mod_ref[4:5, :])
        xn_ref[...] = h.astype(BF16)
        acc_ref[...] = jnp.zeros_like(acc_ref)

    a = jnp.square(jnp.maximum(_dot(xn_ref[...], w1_ref[...]), 0.0))
    acc_ref[...] += _dot(a.astype(BF16), w2_ref[...])

    @pl.when(j == pl.num_programs(1) - 1)
    def _():
        y = x_ref[...] + mod_ref[5:6, :] * acc_ref[...]
        if final_norm:
            y = (y * lax.rsqrt(jnp.mean(jnp.square(y), axis=-1, keepdims=True) + NORM_EPS)) * fin_ref[...]
        o_ref[...] = y


def _mlp(x_all, mod, norm_w, w1, w2, fin_w, layer, *, n_rows, n_batch, seq, final_norm, tm=512, tf=1024):
    mrow = _mod_row_map(n_rows, tm, n_batch, seq)
    kern = functools.partial(_mlp_kernel, final_norm=final_norm)
    return pl.pallas_call(
        kern,
        out_shape=jax.ShapeDtypeStruct((n_rows, D_MODEL), F32),
        grid=(n_rows // tm, D_FF // tf),
        in_specs=[pl.BlockSpec((tm, D_MODEL), lambda i, j: (i, 0)),
                  pl.BlockSpec((None, 8, D_MODEL), lambda i, j: (mrow(i), 0, 0)),
                  pl.BlockSpec((None, 1, D_MODEL), lambda i, j: (layer, 0, 0)),
                  pl.BlockSpec((None, D_MODEL, tf), lambda i, j: (layer, 0, j)),
                  pl.BlockSpec((None, tf, D_MODEL), lambda i, j: (layer, j, 0)),
                  pl.BlockSpec((1, D_MODEL), lambda i, j: (0, 0))],
        out_specs=pl.BlockSpec((tm, D_MODEL), lambda i, j: (i, 0)),
        scratch_shapes=[pltpu.VMEM((tm, D_MODEL), BF16), pltpu.VMEM((tm, D_MODEL), F32)],
        compiler_params=_cparams(("parallel", "arbitrary")),
        name="mlp",
    )(x_all, mod, norm_w, w1, w2, fin_w.reshape(1, D_MODEL))


def _split_w_in(w):
    main = jnp.concatenate([w[..., GATE_OFF + 4 * NH:], w[..., :GATE_OFF]], axis=-1).astype(BF16)
    gate = jnp.pad(w[..., GATE_OFF:GATE_OFF + 4 * NH], ((0, 0), (0, 0), (0, GATE_PAD - 4 * NH))).astype(BF16)
    return main, gate


def kernel(x, c, ctx, c_ctx, w_mod, b_mod, norm_mix, norm_mlp, w_in, hgrn_lb_logits, hgrn_norm, ret_decay, ret_norm, s5_a_re, s5_a_im, s5_log_dt, s5_b_re, s5_b_im, s5_c_re, s5_c_im, s5_d, s5_glu_w, s5_glu_b, mlstm_conv_w, mlstm_conv_b, mlstm_gate_b, mlstm_norm, w_branch, w_out, w_ff1, w_ff2, final_norm):
    n_batch, seq, _ = x.shape
    ctx_len = ctx.shape[1]
    depth = w_in.shape[0]
    nl = n_batch * seq
    dims = dict(n_batch=n_batch, seq=seq)

    p_lb = jax.nn.softmax(hgrn_lb_logits.astype(F32), axis=0)
    lower_bounds = (jnp.cumsum(p_lb, axis=0) - p_lb[0]).reshape(depth, 2, 1, BR)
    lg = jnp.log1p(-jnp.exp(ret_decay.astype(F32)))
    gate_b = jnp.pad(mlstm_gate_b.reshape(depth, 1, 4 * NH).astype(F32), ((0, 0), (0, 0), (0, GATE_PAD - 4 * NH)))
    vec = jnp.stack([hgrn_norm, ret_norm, mlstm_norm, s5_d, s5_glu_b], axis=1).astype(F32)
    vec = jnp.pad(vec, ((0, 0), (0, 8 - vec.shape[1]), (0, 0)))
    w_main, w_gate = _split_w_in(w_in)
    glu_w, wb, wo = s5_glu_w.astype(BF16), w_branch.astype(BF16), w_out.astype(BF16)
    w1, w2 = w_ff1.astype(BF16), w_ff2.astype(BF16)
    norm_mix3 = norm_mix.reshape(depth, 1, D_MODEL)
    norm_mlp3 = norm_mlp.reshape(depth, 1, D_MODEL)
    conv_w = mlstm_conv_w.reshape(depth, 9, 2 * BR)
    conv_b = mlstm_conv_b.reshape(depth, 1, 2 * BR)
    s5_col, s5_row = _s5_params(s5_a_re, s5_a_im, s5_log_dt, s5_b_re, s5_b_im, s5_c_re, s5_c_im)

    c_pad = jnp.zeros((8, D_MODEL), F32).at[:n_batch].set(c).at[n_batch].set(c_ctx)
    x_all = jnp.concatenate([x.reshape(nl, D_MODEL), ctx.reshape(n_batch * ctx_len, D_MODEL)], axis=0)

    for l in range(depth):
        last = l == depth - 1
        mod = _modulation(c_pad, w_mod, b_mod, l)
        mod = jnp.pad(mod[:n_batch + 1].reshape(n_batch + 1, N_MOD, D_MODEL), ((0, 0), (0, 8 - N_MOD), (0, 0)))
        p, gates = _in_proj(x_all, mod, norm_mix3, w_main, w_gate, l, **dims)
        qk = _mlstm_conv(p, conv_w, conv_b, l, ctx_len=ctx_len, **dims)
        raw_a, raw_b, raw_d = _scans(p, gates, qk, lg[l], lower_bounds, gate_b, l, ctx_len=ctx_len, **dims)
        raw_c = _s5_scan(p, s5_col, s5_row, l, ctx_len=ctx_len, **dims)
        n_rows = nl if last else x_all.shape[0]
        x_mid = _finish(raw_a, raw_b, raw_c, raw_d, p, x_all, mod, vec, glu_w, wb, wo, l, n_rows=n_rows, **dims)
        x_all = _mlp(x_mid, mod, norm_mlp3, w1, w2, final_norm, l, n_rows=n_rows, final_norm=last, **dims)
    return x_all.reshape(n_batch, seq, D_MODEL)
```

```python
import functools
import math

import jax
import jax.numpy as jnp
from jax import lax
from jax.experimental import pallas as pl
from jax.experimental.pallas import tpu as pltpu

F32 = jnp.float32
BF16 = jnp.bfloat16

D_MODEL = 2048
N_BRANCH = 4
BR = D_MODEL // N_BRANCH
HD = 128
NH = BR // HD
S5_HG = 16
S5_G = BR // S5_HG
S5_P = 64
D_FF = 4 * D_MODEL
N_MOD = 6
GRID_W = 64
NORM_EPS = 1e-6
NEG_BIG = -1e30
F_TINY = 1e-30
S5_DT_MIN = 1e-3

COL_MERGE = 0
COL_HQ, COL_HF, COL_HV, COL_HG = 16, 17, 19, 20
COL_RQ, COL_RK, COL_RV, COL_RG = 21, 22, 23, 24
COL_SU = 25
COL_MQK, COL_MV, COL_MZ = 26, 28, 29
P_WIDTH = 30 * BR
GATE_OFF = 14 * BR
GATE_PAD = 128

T_HGRN = 128
T_ATTN = 256
S5_T = 128
S5_PP = 128
S5_ROWS = 40
VMEM_LIMIT = 56 * 1024 * 1024

_HI = lax.Precision.HIGHEST


def _cparams(sem):
    return pltpu.CompilerParams(dimension_semantics=sem, vmem_limit_bytes=VMEM_LIMIT)


def _dot(a, b):
    return jnp.dot(a, b, preferred_element_type=F32)


def _dot_nt(a, b):
    return lax.dot_general(a, b, (((1,), (1,)), ((), ())), preferred_element_type=F32)


def _dot_tn(a, b):
    return lax.dot_general(a, b, (((0,), (0,)), ((), ())), preferred_element_type=F32)


def _split3(x):
    hi = x.astype(BF16)
    r1 = x - hi.astype(F32)
    mid = r1.astype(BF16)
    return hi, mid, (r1 - mid.astype(F32)).astype(BF16)


def _cumsum_rows(tri, x):
    hi, mid, lo = _split3(x)
    return _dot(tri, hi) + (_dot(tri, mid) + _dot(tri, lo))


def _sigmoid(x):
    return 1.0 / (1.0 + jnp.exp(-x))


def _silu(x):
    return x * _sigmoid(x)


def _mod_kernel(c_ref, w_ref, b_ref, o_ref):
    a = _silu(c_ref[...]).astype(BF16)
    o_ref[...] = _dot(a, w_ref[...].astype(BF16)) + b_ref[...]


def _modulation(c_pad, w_mod, b_mod, layer):
    n = w_mod.shape[2]
    tn = 1024
    return pl.pallas_call(
        _mod_kernel,
        out_shape=jax.ShapeDtypeStruct((8, n), F32),
        grid=(n // tn,),
        in_specs=[pl.BlockSpec((8, D_MODEL), lambda j: (0, 0)),
                  pl.BlockSpec((None, D_MODEL, tn), lambda j: (layer, 0, j)),
                  pl.BlockSpec((None, 1, tn), lambda j: (layer, 0, j))],
        out_specs=pl.BlockSpec((8, tn), lambda j: (0, j)),
        compiler_params=_cparams(("parallel",)),
        name="modulation",
    )(c_pad, w_mod, b_mod.reshape(b_mod.shape[0], 1, n))


def _norm_mod(x, g, shift, scale):
    y = x * lax.rsqrt(jnp.mean(jnp.square(x), axis=-1, keepdims=True) + NORM_EPS)
    return (y * g) * (1.0 + scale) + shift


def _inproj_kernel(x_ref, mod_ref, g_ref, w_ref, wg_ref, p_ref, gate_ref, xn_ref):
    @pl.when(pl.program_id(1) == 0)
    def _():
        h = _norm_mod(x_ref[...], g_ref[...], mod_ref[0:1, :], mod_ref[1:2, :])
        xn_ref[...] = h.astype(BF16)
        gate_ref[...] = _dot(xn_ref[...], wg_ref[...])

    p_ref[...] = _dot(xn_ref[...], w_ref[...]).astype(p_ref.dtype)


def _mod_row_map(n_rows, tm, n_batch, seq):
    assert n_rows % tm == 0 and seq % tm == 0, (n_rows, seq, tm)
    n_lat_tiles, tiles_per_batch = n_batch * seq // tm, seq // tm

    def row(i):
        return jnp.where(i < n_lat_tiles, i // tiles_per_batch, n_batch)
    return row


def _in_proj(x_all, mod, norm_w, w_main, w_gate, layer, *, n_batch, seq, tm=1024, tn=1536):
    n = x_all.shape[0]
    mrow = _mod_row_map(n, tm, n_batch, seq)
    return pl.pallas_call(
        _inproj_kernel,
        out_shape=(jax.ShapeDtypeStruct((n, P_WIDTH), BF16),
                   jax.ShapeDtypeStruct((n, GATE_PAD), F32)),
        grid=(n // tm, P_WIDTH // tn),
        in_specs=[pl.BlockSpec((tm, D_MODEL), lambda i, j: (i, 0), pipeline_mode=pl.Buffered(1)),
                  pl.BlockSpec((None, 8, D_MODEL), lambda i, j: (mrow(i), 0, 0)),
                  pl.BlockSpec((None, 1, D_MODEL), lambda i, j: (layer, 0, 0)),
                  pl.BlockSpec((None, D_MODEL, tn), lambda i, j: (layer, 0, j)),
                  pl.BlockSpec((None, D_MODEL, GATE_PAD), lambda i, j: (layer, 0, 0))],
        out_specs=(pl.BlockSpec((tm, tn), lambda i, j: (i, j)),
                   pl.BlockSpec((tm, GATE_PAD), lambda i, j: (i, 0))),
        scratch_shapes=[pltpu.VMEM((tm, D_MODEL), BF16)],
        compiler_params=_cparams(("parallel", "arbitrary")),
        name="in_proj",
    )(x_all, mod, norm_w, w_main, w_gate)


CONV_ROWS = 256


def _conv_kernel(up_ref, x_ref, dn_ref, w_ref, b_ref, o_ref, *, n_lat_blocks, blocks_per_img):
    rb = pl.program_id(0)
    x = x_ref[...].astype(F32)
    w = w_ref[...]
    n = CONV_ROWS

    @pl.when(rb < n_lat_blocks)
    def _():
        rr = rb % blocks_per_img
        up = jnp.where(rr > 0, up_ref[...].astype(F32), 0.0)
        dn = jnp.where(rr < blocks_per_img - 1, dn_ref[...].astype(F32), 0.0)
        ext = jnp.concatenate([up, x, dn], axis=0)
        ne = n + 2 * GRID_W
        col = lax.broadcasted_iota(jnp.int32, (ne, 1), 0) % GRID_W
        xl = jnp.where(col >= 1, pltpu.roll(ext, 1, 0), 0.0)
        xr = jnp.where(col <= GRID_W - 2, pltpu.roll(ext, ne - 1, 0), 0.0)
        acc = None
        for i in range(3):
            y = w[3 * i:3 * i + 1] * xl + w[3 * i + 1:3 * i + 2] * ext + w[3 * i + 2:3 * i + 3] * xr
            part = y[i * GRID_W:i * GRID_W + n]
            acc = part if acc is None else acc + part
        o_ref[...] = _silu(acc + b_ref[...]).astype(o_ref.dtype)

    @pl.when(rb >= n_lat_blocks)
    def _():
        t = lax.broadcasted_iota(jnp.int32, (n, 1), 0)
        xl = jnp.where(t >= 1, pltpu.roll(x, 1, 0), 0.0)
        xr = jnp.where(t <= n - 2, pltpu.roll(x, n - 1, 0), 0.0)
        acc = w[3:4] * xl + w[4:5] * x + w[5:6] * xr
        o_ref[...] = _silu(acc + b_ref[...]).astype(o_ref.dtype)


def _mlstm_conv(p, conv_w, conv_b, layer, *, n_batch, seq, ctx_len, cb=2 * BR):
    assert ctx_len == CONV_ROWS and seq % CONV_ROWS == 0
    n = p.shape[0]
    n_blocks = n // CONV_ROWS
    hb = CONV_ROWS // GRID_W
    n_halo = n // GRID_W
    c0 = COL_MQK * BR // cb
    kern = functools.partial(_conv_kernel, n_lat_blocks=n_batch * seq // CONV_ROWS,
                             blocks_per_img=seq // CONV_ROWS)
    return pl.pallas_call(
        kern,
        out_shape=jax.ShapeDtypeStruct((n, 2 * BR), BF16),
        grid=(n_blocks, 2 * BR // cb),
        in_specs=[pl.BlockSpec((GRID_W, cb), lambda r, c: (jnp.maximum(r * hb - 1, 0), c0 + c)),
                  pl.BlockSpec((CONV_ROWS, cb), lambda r, c: (r, c0 + c)),
                  pl.BlockSpec((GRID_W, cb), lambda r, c: (jnp.minimum((r + 1) * hb, n_halo - 1), c0 + c)),
                  pl.BlockSpec((None, 9, cb), lambda r, c: (layer, 0, c)),
                  pl.BlockSpec((None, 1, cb), lambda r, c: (layer, 0, c))],
        out_specs=pl.BlockSpec((CONV_ROWS, cb), lambda r, c: (r, c)),
        compiler_params=_cparams(("parallel", "parallel")),
        name="mlstm_conv",
    )(p, p, p, conv_w, conv_b)


def _chunk_block(b, d, i, *, n_ctx, n_lat, n_batch):
    ctx_j = jnp.where(d == 0, i, n_ctx - 1 - i)
    lat_j = jnp.where(d == 0, i - n_ctx, n_ctx + n_lat - 1 - i)
    return jnp.where(i < n_ctx, n_batch * n_lat + b * n_ctx + ctx_j, b * n_lat + lat_j)


def _order_mask(d, t):
    r = lax.broadcasted_iota(jnp.int32, (t, t), 0)
    c = lax.broadcasted_iota(jnp.int32, (t, t), 1)
    return jnp.where(d == 0, r - c, c - r) >= 0


def _ret_kernel(lg_ref, q_ref, k_ref, v_ref, o_ref, s_ref):
    d = pl.program_id(1)
    t = T_ATTN

    @pl.when(pl.program_id(2) == 0)
    def _():
        s_ref[...] = jnp.zeros_like(s_ref)

    r = lax.broadcasted_iota(jnp.int32, (t, t), 0)
    c = lax.broadcasted_iota(jnp.int32, (t, t), 1)
    rel = jnp.where(d == 0, r - c, c - r)
    relf = jnp.maximum(rel, 0).astype(F32)
    tt = lax.broadcasted_iota(jnp.int32, (t, 1), 0)
    pos = jnp.where(d == 0, tt, t - 1 - tt).astype(F32)
    scale = HD ** -0.5
    for h in range(NH):
        sl = slice(h * HD, (h + 1) * HD)
        lg = lg_ref[d, h]
        q = q_ref[:, sl]
        k = k_ref[:, sl]
        v = v_ref[:, sl]
        intra = jnp.where(rel >= 0, jnp.exp(lg * relf), 0.0)
        scores = (_dot_nt(q, k) * scale) * intra
        s_old = s_ref[h]
        out = _dot(scores.astype(BF16), v) + jnp.exp(lg * (pos + 1.0)) * _dot(q, s_old.astype(BF16))
        kd = (k.astype(F32) * (scale * jnp.exp(lg * (t - 1.0 - pos)))).astype(BF16)
        s_ref[h] = jnp.exp(lg * t) * s_old + _dot_tn(kd, v)
        o_ref[:, sl] = out


def _hgrn_direction(fwd, lb_ref, q_ref, z_ref, v_ref, o_ref, st_ref):
    t = T_HGRN
    lb = lb_ref[...]
    z = z_ref[...].astype(F32)
    f = lb + (1.0 - lb) * _sigmoid(z)
    logf = jnp.log(jnp.maximum(f, F_TINY))
    kk = (1.0 - lb) * _sigmoid(-z)
    qq = _silu(q_ref[...].astype(F32))
    r2 = lax.broadcasted_iota(jnp.int32, (t, t), 0)
    c2 = lax.broadcasted_iota(jnp.int32, (t, t), 1)
    tri = jnp.where((c2 <= r2) if fwd else (c2 >= r2), 1.0, 0.0).astype(BF16)
    cum = _cumsum_rows(tri, logf)
    row = lax.broadcasted_iota(jnp.int32, (t, 1), 0)
    if fwd:
        e_ref = jnp.where(row == 0, 0.0, pltpu.roll(cum, 1, 0))
    else:
        e_ref = jnp.where(row == t - 1, 0.0, pltpu.roll(cum, t - 1, 0))
    f_ref = cum
    levels = []
    w = 1
    while w < t:
        q_l = (qq * jnp.exp(cum - e_ref)).astype(BF16)
        k_l = (kk * jnp.exp(f_ref - cum)).astype(BF16)
        late, early = (w, 0) if fwd else (0, w)
        key = ((r2 ^ c2) & ~(2 * w - 1)) | ((r2 & w) ^ late) | ((c2 & w) ^ early)
        levels.append((q_l, k_l, key == 0))
        upper = (row & w) != 0
        if fwd:
            e_ref = jnp.where(upper, pltpu.roll(e_ref, w, 0), e_ref)
            f_ref = jnp.where(upper, f_ref, pltpu.roll(f_ref, t - w, 0))
        else:
            e_ref = jnp.where(upper, e_ref, pltpu.roll(e_ref, t - w, 0))
            f_ref = jnp.where(upper, pltpu.roll(f_ref, w, 0), f_ref)
        w *= 2
    q_b = qq.astype(BF16)
    k_b = kk.astype(BF16)
    tail = cum[t - 1:t] if fwd else cum[0:1]
    q_state = (qq * jnp.exp(cum)).astype(BF16)
    k_state = (kk * jnp.exp(tail - cum)).astype(BF16)
    decay = jnp.exp(tail)
    for h in range(NH):
        sl = slice(h * HD, (h + 1) * HD)
        v_h = v_ref[:, sl]
        scores = jnp.where(r2 == c2, _dot_nt(q_b[:, sl], k_b[:, sl]), 0.0)
        for q_l, k_l, m in levels:
            scores = jnp.where(m, _dot_nt(q_l[:, sl], k_l[:, sl]), scores)
        st = st_ref[h]
        o_ref[:, sl] = _dot(scores.astype(BF16), v_h) + _dot_nt(q_state[:, sl], st.astype(BF16))
        st_ref[h] = st * decay[:, sl] + _dot_tn(v_h, k_state[:, sl])


def _hgrn_kernel(lb_ref, q_ref, z_ref, v_ref, o_ref, st_ref):
    d = pl.program_id(1)

    @pl.when(pl.program_id(2) == 0)
    def _():
        st_ref[...] = jnp.zeros_like(st_ref)

    @pl.when(d == 0)
    def _():
        _hgrn_direction(True, lb_ref, q_ref, z_ref, v_ref, o_ref, st_ref)

    @pl.when(d == 1)
    def _():
        _hgrn_direction(False, lb_ref, q_ref, z_ref, v_ref, o_ref, st_ref)


def _log_sigmoid(x):
    return jnp.minimum(x, 0.0) - jnp.log(1.0 + jnp.exp(-jnp.abs(x)))


def _mlstm_kernel(qk_ref, v_ref, g_ref, gb_ref, o_ref, c_ref, n_ref, m_ref):
    d = pl.program_id(1)
    t = T_ATTN

    @pl.when(pl.program_id(2) == 0)
    def _():
        c_ref[...] = jnp.zeros_like(c_ref)
        n_ref[...] = jnp.zeros_like(n_ref)
        m_ref[...] = jnp.zeros_like(m_ref)

    g = g_ref[...] + gb_ref[...]
    gt = g.T
    mask = _order_mask(d, t)
    tri = jnp.where(mask, 1.0, 0.0).astype(BF16)
    lf = _log_sigmoid(g)
    cum_c = _cumsum_rows(tri, lf)
    lt_hi, lt_mid, lt_lo = _split3(lf.T)
    cum_r = _dot_nt(lt_hi, tri) + (_dot_nt(lt_mid, tri) + _dot_nt(lt_lo, tri))
    scale = HD ** -0.5
    for h in range(NH):
        sl = slice(h * HD, (h + 1) * HD)
        li, lfw = h, NH + h
        i_col = jnp.where(d == 0, g[:, li:li + 1], g[:, 8 + li:8 + li + 1])
        i_row = jnp.where(d == 0, gt[li:li + 1], gt[8 + li:8 + li + 1])
        cc = jnp.where(d == 0, cum_c[:, lfw:lfw + 1], cum_c[:, 8 + lfw:8 + lfw + 1])
        cr = jnp.where(d == 0, cum_r[lfw:lfw + 1], cum_r[8 + lfw:8 + lfw + 1])
        total = jnp.where(d == 0, cr[:, t - 1:t], cr[:, 0:1])
        m_old = m_ref[h][:, 0:1]
        q = qk_ref[:, sl]
        k = qk_ref[:, BR + h * HD:BR + (h + 1) * HD]
        v = v_ref[:, sl]
        logw = jnp.where(mask, cc - cr + i_row, NEG_BIG)
        from_state = cc + m_old
        m_t = jnp.maximum(from_state, jnp.max(logw, axis=1, keepdims=True))
        w = jnp.exp(logw - m_t)
        w_state = jnp.exp(from_state - m_t)
        scores = (_dot_nt(q, k) * scale) * w
        c_old = c_ref[h]
        n_old = n_ref[h]
        num = _dot(scores.astype(BF16), v) + w_state * _dot(q, c_old.astype(BF16))
        den = (jnp.sum(scores, axis=1, keepdims=True)
               + w_state * jnp.sum(q.astype(F32) * n_old, axis=1, keepdims=True))
        o_ref[:, sl] = num / jnp.maximum(jnp.abs(den), jnp.exp(-m_t))
        m_new = jnp.maximum(total + m_old, jnp.max(total - cr + i_row, axis=1, keepdims=True))
        keep = jnp.exp(total + m_old - m_new)
        w_end = jnp.exp(total - cc + i_col - m_new)
        kw = k.astype(F32) * (scale * w_end)
        c_ref[h] = keep * c_old + _dot_tn(kw.astype(BF16), v)
        n_ref[h] = keep * n_old + jnp.sum(kw, axis=0, keepdims=True)
        m_ref[h] = jnp.broadcast_to(m_new, (1, HD))


def _scans(p, gates, qk, lg, lb, gate_b, layer, *, n_batch, seq, ctx_len):
    n = p.shape[0]
    state = pltpu.VMEM((NH, HD, HD), F32)

    def call(kernel, t, inputs, make_specs, scratch, name):
        n_ctx, n_lat = ctx_len // t, seq // t
        blk = functools.partial(_chunk_block, n_ctx=n_ctx, n_lat=n_lat, n_batch=n_batch)

        def rows(width, cblock=0):
            return pl.BlockSpec((t, width), lambda b, d, i: (blk(b, d, i), cblock))

        return pl.pallas_call(
            kernel,
            out_shape=jax.ShapeDtypeStruct((2, n, BR), F32),
            grid=(n_batch, 2, n_ctx + n_lat),
            in_specs=make_specs(rows, blk),
            out_specs=pl.BlockSpec((None, t, BR), lambda b, d, i: (d, blk(b, d, i), 0)),
            scratch_shapes=scratch,
            compiler_params=_cparams(("parallel", "arbitrary", "arbitrary")),
            name=name,
        )(*inputs)

    raw_b = call(_ret_kernel, T_ATTN, (lg, p, p, p),
                 lambda rows, blk: [pl.BlockSpec(memory_space=pltpu.SMEM),
                                    rows(BR, COL_RQ), rows(BR, COL_RK), rows(BR, COL_RV)],
                 [state], "retention_scan")
    raw_a = call(_hgrn_kernel, T_HGRN, (lb, p, p, p),
                 lambda rows, blk: [pl.BlockSpec((None, None, 1, BR), lambda b, d, i: (layer, d, 0, 0)),
                                    rows(BR, COL_HQ),
                                    pl.BlockSpec((T_HGRN, BR), lambda b, d, i: (blk(b, d, i), COL_HF + d)),
                                    rows(BR, COL_HV)],
                 [state], "hgrn2_scan")
    raw_d = call(_mlstm_kernel, T_ATTN, (qk, p, gates, gate_b),
                 lambda rows, blk: [rows(2 * BR), rows(BR, COL_MV), rows(GATE_PAD),
                                    pl.BlockSpec((None, 1, GATE_PAD), lambda b, d, i: (layer, 0, 0))],
                 [state, pltpu.VMEM((NH, 1, HD), F32), pltpu.VMEM((NH, 1, HD), F32)], "mlstm_scan")
    return raw_a, raw_b, raw_d


def _cmul(ar, ai, br, bi):
    return ar * br - ai * bi, ar * bi + ai * br


def _pow_table(lam_r, lam_i, expo):
    pr = jnp.ones(expo.shape, F32)
    pi = jnp.zeros(expo.shape, F32)
    ar = jnp.broadcast_to(lam_r, expo.shape)
    ai = jnp.broadcast_to(lam_i, expo.shape)
    for k in range(S5_T.bit_length() - 1):
        nr, ni = _cmul(pr, pi, ar, ai)
        bit = (expo & (1 << k)) != 0
        pr, pi = jnp.where(bit, nr, pr), jnp.where(bit, ni, pi)
        ar, ai = _cmul(ar, ai, ar, ai)
    return pr, pi


def _s5_kernel(u_ref, col_ref, row_ref, o_ref, m_ref, w_ref, ys_ref, z_ref, wr_ref, wi_ref, xr_ref, xi_ref, *,
               n_batch, n_ctx, n_lat):
    d = pl.program_id(1)
    t = S5_T
    col = col_ref[...]
    rowp = row_ref[...]
    lam_rc, lam_ic = col[:, 0:1], col[:, 1:2]
    bbr, bbi = rowp[0:S5_HG], rowp[S5_HG:2 * S5_HG]
    lam_rr, lam_ir = rowp[2 * S5_HG:2 * S5_HG + 1], rowp[2 * S5_HG + 1:2 * S5_HG + 2]
    lane = lax.broadcasted_iota(jnp.int32, (t, t), 1)
    sub = lax.broadcasted_iota(jnp.int32, (t, t), 0)

    pr, pi = _pow_table(lam_rc, lam_ic, jnp.where(d == 0, lane, t - 1 - lane))
    qr, qi = _cmul(pr, pi, lam_rc, lam_ic)
    for h in range(S5_HG):
        cr, ci = col[:, 2 + h:3 + h], col[:, 2 + S5_HG + h:3 + S5_HG + h]
        hs = slice(h * t, (h + 1) * t)
        z_ref[0:S5_PP, hs] = cr * pr - ci * pi
        z_ref[S5_PP:2 * S5_PP, hs] = -(cr * pi + ci * pr)
        ys_ref[0:S5_PP, hs] = (cr * qr - ci * qi).astype(BF16)
        ys_ref[S5_PP:2 * S5_PP, hs] = (-(cr * qi + ci * qr)).astype(BF16)
    b_hi, b_mid, _ = _split3(jnp.concatenate([bbr, bbi], axis=1))
    z_hi, z_mid, _ = _split3(z_ref[...])
    krow = _dot(b_hi, z_hi) + (_dot(b_hi, z_mid) + _dot(b_mid, z_hi))

    def build_m(shift, keep):
        keep = jnp.where(keep, 1.0, 0.0)
        for hp in range(S5_HG):
            for h in range(S5_HG):
                tile = jnp.broadcast_to(krow[hp:hp + 1, h * t:(h + 1) * t], (t, t))
                tile = pltpu.roll(tile, shift, 1, stride=1, stride_axis=0)
                m_ref[hp * t:(hp + 1) * t, h * t:(h + 1) * t] = (tile * keep).astype(BF16)

    @pl.when(d == 0)
    def _():
        build_m(0, lane >= sub)

    @pl.when(d == 1)
    def _():
        build_m(1, lane <= sub)

    tr, ti = _pow_table(lam_rr, lam_ir, jnp.where(d == 0, t - 1 - sub, sub))
    for hp in range(S5_HG):
        br, bi = bbr[hp:hp + 1], bbi[hp:hp + 1]
        w_ref[hp * t:(hp + 1) * t, 0:S5_PP] = (tr * br - ti * bi).astype(BF16)
        w_ref[hp * t:(hp + 1) * t, S5_PP:2 * S5_PP] = (tr * bi + ti * br).astype(BF16)

    u = jnp.concatenate([u_ref[hp] for hp in range(S5_HG)], axis=1)
    wv = _dot(u, w_ref[...])
    wr_ref[...] = wv[:, 0:S5_PP]
    wi_ref[...] = wv[:, S5_PP:2 * S5_PP]
    lr, li = lam_rr, lam_ir
    for _ in range(t.bit_length() - 1):
        lr, li = _cmul(lr, li, lr, li)

    def make_step(base, stride, count):
        def step(i, carry):
            xr, xi = carry
            j = jnp.where(d == 0, i, count - 1 - i)
            idx = pl.ds(base + j, n_batch, stride=stride)
            xr_ref[idx, :] = xr
            xi_ref[idx, :] = xi
            nr, ni = _cmul(xr, xi, lr, li)
            return nr + wr_ref[idx, :], ni + wi_ref[idx, :]
        return step

    zero = jnp.zeros((n_batch, S5_PP), F32)
    carry = lax.fori_loop(0, n_ctx, make_step(n_batch * n_lat, n_ctx, n_ctx), (zero, zero))
    lax.fori_loop(0, n_lat, make_step(0, n_lat, n_lat), carry)
    x_prev = jnp.concatenate([xr_ref[...], xi_ref[...]], axis=1).astype(BF16)
    y = _dot(u, m_ref[...]) + _dot(x_prev, ys_ref[...])

    @pl.when(d == 0)
    def _():
        for h in range(S5_HG):
            o_ref[h] = y[:, h * t:(h + 1) * t]

    @pl.when(d == 1)
    def _():
        for h in range(S5_HG):
            o_ref[h] += y[:, h * t:(h + 1) * t]


def _s5_params(a_re, a_im, log_dt, b_re, b_im, c_re, c_im):
    a_re, a_im = a_re.astype(F32), a_im.astype(F32)
    dt = jnp.exp(log_dt.astype(F32))[..., None]
    mag = jnp.exp(a_re * dt)
    lam_re, lam_im = mag * jnp.cos(a_im * dt), mag * jnp.sin(a_im * dt)
    den = a_re * a_re + a_im * a_im
    num_re, num_im = lam_re - 1.0, lam_im
    fr = (num_re * a_re + num_im * a_im) / den
    fi = (num_im * a_re - num_re * a_im) / den
    b_re, b_im = b_re.astype(F32), b_im.astype(F32)
    bb_re = fr[..., None] * b_re - fi[..., None] * b_im
    bb_im = fr[..., None] * b_im + fi[..., None] * b_re
    c_re_t = jnp.swapaxes(c_re.astype(F32), -1, -2)
    c_im_t = jnp.swapaxes(c_im.astype(F32), -1, -2)
    col = jnp.concatenate([lam_re[..., None], lam_im[..., None], c_re_t, c_im_t], axis=-1)
    col = jnp.pad(col, ((0, 0),) * 3 + ((0, S5_PP - S5_P), (0, 128 - col.shape[-1])))
    row = jnp.concatenate([jnp.swapaxes(bb_re, -1, -2), jnp.swapaxes(bb_im, -1, -2),
                           lam_re[..., None, :], lam_im[..., None, :]], axis=-2)
    row = jnp.pad(row, ((0, 0),) * 3 + ((0, S5_ROWS - row.shape[-2]), (0, S5_PP - S5_P)))
    return col, row


def _s5_scan(p, col, row, layer, *, n_batch, seq, ctx_len):
    t = S5_T
    n = p.shape[0]
    nc = n // t
    n_ctx, n_lat = ctx_len // t, seq // t
    ut = p[:, COL_SU * BR:(COL_SU + 1) * BR].T.reshape(S5_G, S5_HG, nc, t)
    kern = functools.partial(_s5_kernel, n_batch=n_batch, n_ctx=n_ctx, n_lat=n_lat)
    yt = pl.pallas_call(
        kern,
        out_shape=jax.ShapeDtypeStruct((S5_G, S5_HG, nc, t), F32),
        grid=(S5_G, 2),
        in_specs=[pl.BlockSpec((None, S5_HG, nc, t), lambda g, d: (g, 0, 0, 0)),
                  pl.BlockSpec((None, None, None, S5_PP, 128), lambda g, d: (layer, d, g, 0, 0)),
                  pl.BlockSpec((None, None, None, S5_ROWS, S5_PP), lambda g, d: (layer, d, g, 0, 0))],
        out_specs=pl.BlockSpec((None, S5_HG, nc, t), lambda g, d: (g, 0, 0, 0)),
        scratch_shapes=[pltpu.VMEM((S5_HG * t, S5_HG * t), BF16),
                        pltpu.VMEM((S5_HG * t, 2 * S5_PP), BF16),
                        pltpu.VMEM((2 * S5_PP, S5_HG * t), BF16),
                        pltpu.VMEM((2 * S5_PP, S5_HG * t), F32),
                        pltpu.VMEM((nc, S5_PP), F32), pltpu.VMEM((nc, S5_PP), F32),
                        pltpu.VMEM((nc, S5_PP), F32), pltpu.VMEM((nc, S5_PP), F32)],
        compiler_params=_cparams(("parallel", "arbitrary")),
        name="s5_scan",
    )(ut, col, row)
    return yt.reshape(BR, n).T


def _head_rms(x, g):
    parts = []
    for h in range(NH):
        xh = x[:, h * HD:(h + 1) * HD]
        parts.append(xh * lax.rsqrt(jnp.mean(jnp.square(xh), axis=-1, keepdims=True) + NORM_EPS))
    return jnp.concatenate(parts, axis=-1) * g


def _finish_kernel(ra_ref, rb_ref, rc_ref, rd_ref, ga_ref, gb_ref, u_ref, gd_ref, mg_ref, x_ref, mod_ref,
                   vec_ref, glu_w_ref, wb_ref, wo_ref, o_ref):
    vec = vec_ref[...]
    oa = _head_rms(ra_ref[0] + ra_ref[1], vec[0:1]) * _silu(ga_ref[...].astype(F32))
    ob = _head_rms(rb_ref[0] + rb_ref[1], vec[1:2]) * _silu(gb_ref[...].astype(F32))
    od = _head_rms(rd_ref[0] + rd_ref[1], vec[2:3]) * _silu(gd_ref[...].astype(F32))
    yc = rc_ref[...] + vec[3:4] * u_ref[...].astype(F32)
    yc = 0.5 * yc * (1.0 + lax.erf(yc * (2.0 ** -0.5)))
    oc = yc * _sigmoid(_dot(yc.astype(BF16), glu_w_ref[...]) + vec[4:5])
    y = None
    for j, o in enumerate((oa, ob, oc, od)):
        gate = _sigmoid(mg_ref[:, j * D_MODEL:(j + 1) * D_MODEL].astype(F32))
        term = gate * _dot(o.astype(BF16), wb_ref[j])
        y = term if y is None else y + term
    mix = _dot(y.astype(BF16), wo_ref[...])
    o_ref[...] = x_ref[...] + mod_ref[2:3, :] * mix


def _finish(raw_a, raw_b, raw_c, raw_d, p, x_all, mod, vec, glu_w, w_branch, w_out, layer, *, n_rows, n_batch,
            seq, tm=256):
    mrow = _mod_row_map(n_rows, tm, n_batch, seq)
    one = pl.Buffered(1)

    def raw(i):
        return (0, i, 0)

    def pcol(cblock):
        return pl.BlockSpec((tm, BR), lambda i: (i, cblock))

    return pl.pallas_call(
        _finish_kernel,
        out_shape=jax.ShapeDtypeStruct((n_rows, D_MODEL), F32),
        grid=(n_rows // tm,),
        in_specs=[pl.BlockSpec((2, tm, BR), raw),
                  pl.BlockSpec((2, tm, BR), raw),
                  pl.BlockSpec((tm, BR), lambda i: (i, 0)),
                  pl.BlockSpec((2, tm, BR), raw),
                  pcol(COL_HG), pcol(COL_RG), pcol(COL_SU), pcol(COL_MZ),
                  pl.BlockSpec((tm, N_BRANCH * D_MODEL), lambda i: (i, COL_MERGE)),
                  pl.BlockSpec((tm, D_MODEL), lambda i: (i, 0)),
                  pl.BlockSpec((None, 8, D_MODEL), lambda i: (mrow(i), 0, 0)),
                  pl.BlockSpec((None, 8, BR), lambda i: (layer, 0, 0), pipeline_mode=one),
                  pl.BlockSpec((None, BR, BR), lambda i: (layer, 0, 0), pipeline_mode=one),
                  pl.BlockSpec((None, N_BRANCH, BR, D_MODEL), lambda i: (layer, 0, 0, 0), pipeline_mode=one),
                  pl.BlockSpec((None, D_MODEL, D_MODEL), lambda i: (layer, 0, 0), pipeline_mode=one)],
        out_specs=pl.BlockSpec((tm, D_MODEL), lambda i: (i, 0)),
        compiler_params=_cparams(("parallel",)),
        name="finish",
    )(raw_a, raw_b, raw_c, raw_d, p, p, p, p, p, x_all, mod, vec, glu_w, w_branch, w_out)


def _mlp_kernel(x_ref, mod_ref, g_ref, w1_ref, w2_ref, fin_ref, o_ref, xn_ref, *, final_norm):
    j = pl.program_id(1)

    @pl.when(j == 0)
    def _():
        h = _norm_mod(x_ref[...], g_ref[...], mod_ref[3:4, :], mod_ref[4:5, :])
        xn_ref[...] = h.astype(BF16)
        o_ref[...] = jnp.zeros_like(o_ref)

    a = jnp.square(jnp.maximum(_dot(xn_ref[...], w1_ref[...]), 0.0))
    o_ref[...] += _dot(a.astype(BF16), w2_ref[...])

    @pl.when(j == pl.num_programs(1) - 1)
    def _():
        y = x_ref[...] + mod_ref[5:6, :] * o_ref[...]
        if final_norm:
            y = (y * lax.rsqrt(jnp.mean(jnp.square(y), axis=-1, keepdims=True) + NORM_EPS)) * fin_ref[...]
        o_ref[...] = y


def _mlp(x_all, mod, norm_w, w1, w2, fin_w, layer, *, n_rows, n_batch, seq, final_norm, tm=1024, tf=512):
    mrow = _mod_row_map(n_rows, tm, n_batch, seq)
    kern = functools.partial(_mlp_kernel, final_norm=final_norm)
    return pl.pallas_call(
        kern,
        out_shape=jax.ShapeDtypeStruct((n_rows, D_MODEL), F32),
        grid=(n_rows // tm, D_FF // tf),
        in_specs=[pl.BlockSpec((tm, D_MODEL), lambda i, j: (i, 0), pipeline_mode=pl.Buffered(1)),
                  pl.BlockSpec((None, 8, D_MODEL), lambda i, j: (mrow(i), 0, 0)),
                  pl.BlockSpec((None, 1, D_MODEL), lambda i, j: (layer, 0, 0)),
                  pl.BlockSpec((None, D_MODEL, tf), lambda i, j: (layer, 0, j)),
                  pl.BlockSpec((None, tf, D_MODEL), lambda i, j: (layer, j, 0)),
                  pl.BlockSpec((1, D_MODEL), lambda i, j: (0, 0))],
        out_specs=pl.BlockSpec((tm, D_MODEL), lambda i, j: (i, 0)),
        scratch_shapes=[pltpu.VMEM((tm, D_MODEL), BF16)],
        compiler_params=_cparams(("parallel", "arbitrary")),
        name="mlp",
    )(x_all, mod, norm_w, w1, w2, fin_w.reshape(1, D_MODEL))


def _split_w_in(w):
    main = jnp.concatenate([w[..., GATE_OFF + 4 * NH:], w[..., :GATE_OFF]], axis=-1).astype(BF16)
    gate = jnp.pad(w[..., GATE_OFF:GATE_OFF + 4 * NH], ((0, 0), (0, 0), (0, GATE_PAD - 4 * NH))).astype(BF16)
    return main, gate


def kernel(x, c, ctx, c_ctx, w_mod, b_mod, norm_mix, norm_mlp, w_in, hgrn_lb_logits, hgrn_norm, ret_decay, ret_norm, s5_a_re, s5_a_im, s5_log_dt, s5_b_re, s5_b_im, s5_c_re, s5_c_im, s5_d, s5_glu_w, s5_glu_b, mlstm_conv_w, mlstm_conv_b, mlstm_gate_b, mlstm_norm, w_branch, w_out, w_ff1, w_ff2, final_norm):
    n_batch, seq, _ = x.shape
    ctx_len = ctx.shape[1]
    depth = w_in.shape[0]
    nl = n_batch * seq
    dims = dict(n_batch=n_batch, seq=seq)

    p_lb = jax.nn.softmax(hgrn_lb_logits.astype(F32), axis=0)
    lower_bounds = (jnp.cumsum(p_lb, axis=0) - p_lb[0]).reshape(depth, 2, 1, BR)
    lg = jnp.log1p(-jnp.exp(ret_decay.astype(F32)))
    gate_b = jnp.pad(mlstm_gate_b.reshape(depth, 1, 4 * NH).astype(F32), ((0, 0), (0, 0), (0, GATE_PAD - 4 * NH)))
    vec = jnp.stack([hgrn_norm, ret_norm, mlstm_norm, s5_d, s5_glu_b], axis=1).astype(F32)
    vec = jnp.pad(vec, ((0, 0), (0, 8 - vec.shape[1]), (0, 0)))
    w_main, w_gate = _split_w_in(w_in)
    glu_w, wb, wo = s5_glu_w.astype(BF16), w_branch.astype(BF16), w_out.astype(BF16)
    w1, w2 = w_ff1.astype(BF16), w_ff2.astype(BF16)
    norm_mix3 = norm_mix.reshape(depth, 1, D_MODEL)
    norm_mlp3 = norm_mlp.reshape(depth, 1, D_MODEL)
    conv_w = mlstm_conv_w.reshape(depth, 9, 2 * BR)
    conv_b = mlstm_conv_b.reshape(depth, 1, 2 * BR)
    s5_col, s5_row = _s5_params(s5_a_re, s5_a_im, s5_log_dt, s5_b_re, s5_b_im, s5_c_re, s5_c_im)

    c_pad = jnp.zeros((8, D_MODEL), F32).at[:n_batch].set(c).at[n_batch].set(c_ctx)
    x_all = jnp.concatenate([x.reshape(nl, D_MODEL), ctx.reshape(n_batch * ctx_len, D_MODEL)], axis=0)

    for l in range(depth):
        last = l == depth - 1
        mod = _modulation(c_pad, w_mod, b_mod, l)
        mod = jnp.pad(mod[:n_batch + 1].reshape(n_batch + 1, N_MOD, D_MODEL), ((0, 0), (0, 8 - N_MOD), (0, 0)))
        p, gates = _in_proj(x_all, mod, norm_mix3, w_main, w_gate, l, **dims)
        qk = _mlstm_conv(p, conv_w, conv_b, l, ctx_len=ctx_len, **dims)
        raw_a, raw_b, raw_d = _scans(p, gates, qk, lg[l], lower_bounds, gate_b, l, ctx_len=ctx_len, **dims)
        raw_c = _s5_scan(p, s5_col, s5_row, l, ctx_len=ctx_len, **dims)
        n_rows = nl if last else x_all.shape[0]
        x_mid = _finish(raw_a, raw_b, raw_c, raw_d, p, x_all, mod, vec, glu_w, wb, wo, l, n_rows=n_rows, **dims)
        x_all = _mlp(x_mid, mod, norm_mlp3, w1, w2, final_norm, l, n_rows=n_rows, final_norm=last, **dims)
    return x_all.reshape(n_batch, seq, D_MODEL)
```

```python
import functools
import math

import jax
import jax.numpy as jnp
from jax import lax
from jax.experimental import pallas as pl
from jax.experimental.pallas import tpu as pltpu

F32 = jnp.float32
BF16 = jnp.bfloat16

D_MODEL = 2048
N_BRANCH = 4
BR = D_MODEL // N_BRANCH
HD = 128
NH = BR // HD
S5_HG = 16
S5_G = BR // S5_HG
S5_P = 64
D_FF = 4 * D_MODEL
N_MOD = 6
GRID_W = 64
NORM_EPS = 1e-6
NEG_BIG = -1e30
F_TINY = 1e-30
S5_DT_MIN = 1e-3

COL_MERGE = 0
COL_HQ, COL_HF, COL_HV, COL_HG = 16, 17, 19, 20
COL_RQ, COL_RK, COL_RV, COL_RG = 21, 22, 23, 24
COL_SU = 25
COL_MQK, COL_MV, COL_MZ = 26, 28, 29
P_WIDTH = 30 * BR
GATE_OFF = 14 * BR
GATE_PAD = 128

T_HGRN = 128
T_ATTN = 256
S5_T = 128
S5_PP = 128
S5_ROWS = 40
VMEM_LIMIT = 56 * 1024 * 1024

_HI = lax.Precision.HIGHEST


def _cparams(sem):
    return pltpu.CompilerParams(dimension_semantics=sem, vmem_limit_bytes=VMEM_LIMIT)


def _dot(a, b):
    return jnp.dot(a, b, preferred_element_type=F32)


def _dot_nt(a, b):
    return lax.dot_general(a, b, (((1,), (1,)), ((), ())), preferred_element_type=F32)


def _dot_tn(a, b):
    return lax.dot_general(a, b, (((0,), (0,)), ((), ())), preferred_element_type=F32)


def _split3(x):
    hi = x.astype(BF16)
    r1 = x - hi.astype(F32)
    mid = r1.astype(BF16)
    return hi, mid, (r1 - mid.astype(F32)).astype(BF16)


def _cumsum_rows(tri, x):
    hi, mid, lo = _split3(x)
    return _dot(tri, hi) + (_dot(tri, mid) + _dot(tri, lo))


def _sigmoid(x):
    return 1.0 / (1.0 + jnp.exp(-x))


def _silu(x):
    return x * _sigmoid(x)


def _mod_kernel(c_ref, w_ref, b_ref, o_ref):
    @pl.when(pl.program_id(0) == 0)
    def _():
        o_ref[...] = jnp.broadcast_to(b_ref[...], o_ref.shape)

    a = _silu(c_ref[...]).astype(BF16)
    o_ref[...] += _dot(a, w_ref[...].astype(BF16))


def _modulation(c_pad, w_mod, b_mod, layer, tk=256):
    n = w_mod.shape[2]
    return pl.pallas_call(
        _mod_kernel,
        out_shape=jax.ShapeDtypeStruct((8, n), F32),
        grid=(D_MODEL // tk,),
        in_specs=[pl.BlockSpec((8, tk), lambda k: (0, k)),
                  pl.BlockSpec((None, tk, n), lambda k: (layer, k, 0)),
                  pl.BlockSpec((None, 1, n), lambda k: (layer, 0, 0))],
        out_specs=pl.BlockSpec((8, n), lambda k: (0, 0)),
        compiler_params=_cparams(("arbitrary",)),
        name="modulation",
    )(c_pad, w_mod, b_mod.reshape(b_mod.shape[0], 1, n))


def _norm_mod(x, g, shift, scale):
    y = x * lax.rsqrt(jnp.mean(jnp.square(x), axis=-1, keepdims=True) + NORM_EPS)
    return (y * g) * (1.0 + scale) + shift


def _inproj_kernel(x_ref, mod_ref, g_ref, w_ref, wg_ref, p_ref, gate_ref, xn_ref):
    @pl.when(pl.program_id(1) == 0)
    def _():
        h = _norm_mod(x_ref[...], g_ref[...], mod_ref[0:1, :], mod_ref[1:2, :])
        xn_ref[...] = h.astype(BF16)
        gate_ref[...] = _dot(xn_ref[...], wg_ref[...])

    p_ref[...] = _dot(xn_ref[...], w_ref[...]).astype(p_ref.dtype)


def _mod_row_map(n_rows, tm, n_batch, seq):
    assert n_rows % tm == 0 and seq % tm == 0, (n_rows, seq, tm)
    n_lat_tiles, tiles_per_batch = n_batch * seq // tm, seq // tm

    def row(i):
        return jnp.where(i < n_lat_tiles, i // tiles_per_batch, n_batch)
    return row


def _in_proj(x_all, mod, norm_w, w_main, w_gate, layer, *, n_batch, seq, tm=1024, tn=1024):
    n = x_all.shape[0]
    mrow = _mod_row_map(n, tm, n_batch, seq)
    return pl.pallas_call(
        _inproj_kernel,
        out_shape=(jax.ShapeDtypeStruct((n, P_WIDTH), BF16),
                   jax.ShapeDtypeStruct((n, GATE_PAD), F32)),
        grid=(n // tm, P_WIDTH // tn),
        in_specs=[pl.BlockSpec((tm, D_MODEL), lambda i, j: (i, 0)),
                  pl.BlockSpec((None, 8, D_MODEL), lambda i, j: (mrow(i), 0, 0)),
                  pl.BlockSpec((None, 1, D_MODEL), lambda i, j: (layer, 0, 0)),
                  pl.BlockSpec((None, D_MODEL, tn), lambda i, j: (layer, 0, j)),
                  pl.BlockSpec((None, D_MODEL, GATE_PAD), lambda i, j: (layer, 0, 0))],
        out_specs=(pl.BlockSpec((tm, tn), lambda i, j: (i, j)),
                   pl.BlockSpec((tm, GATE_PAD), lambda i, j: (i, 0))),
        scratch_shapes=[pltpu.VMEM((tm, D_MODEL), BF16)],
        compiler_params=_cparams(("parallel", "arbitrary")),
        name="in_proj",
    )(x_all, mod, norm_w, w_main, w_gate)


CONV_ROWS = 256


def _conv_kernel(up_ref, x_ref, dn_ref, w_ref, b_ref, o_ref, *, n_lat_blocks, blocks_per_img):
    rb = pl.program_id(0)
    x = x_ref[...].astype(F32)
    w = w_ref[...]
    n = CONV_ROWS

    @pl.when(rb < n_lat_blocks)
    def _():
        rr = rb % blocks_per_img
        up = jnp.where(rr > 0, up_ref[...].astype(F32), 0.0)
        dn = jnp.where(rr < blocks_per_img - 1, dn_ref[...].astype(F32), 0.0)
        ext = jnp.concatenate([up, x, dn], axis=0)
        ne = n + 2 * GRID_W
        col = lax.broadcasted_iota(jnp.int32, (ne, 1), 0) % GRID_W
        xl = jnp.where(col >= 1, pltpu.roll(ext, 1, 0), 0.0)
        xr = jnp.where(col <= GRID_W - 2, pltpu.roll(ext, ne - 1, 0), 0.0)
        acc = None
        for i in range(3):
            y = w[3 * i:3 * i + 1] * xl + w[3 * i + 1:3 * i + 2] * ext + w[3 * i + 2:3 * i + 3] * xr
            part = y[i * GRID_W:i * GRID_W + n]
            acc = part if acc is None else acc + part
        o_ref[...] = _silu(acc + b_ref[...]).astype(o_ref.dtype)

    @pl.when(rb >= n_lat_blocks)
    def _():
        t = lax.broadcasted_iota(jnp.int32, (n, 1), 0)
        xl = jnp.where(t >= 1, pltpu.roll(x, 1, 0), 0.0)
        xr = jnp.where(t <= n - 2, pltpu.roll(x, n - 1, 0), 0.0)
        acc = w[3:4] * xl + w[4:5] * x + w[5:6] * xr
        o_ref[...] = _silu(acc + b_ref[...]).astype(o_ref.dtype)


def _mlstm_conv(p, conv_w, conv_b, layer, *, n_batch, seq, ctx_len, cb=2 * BR):
    assert ctx_len == CONV_ROWS and seq % CONV_ROWS == 0
    n = p.shape[0]
    n_blocks = n // CONV_ROWS
    hb = CONV_ROWS // GRID_W
    n_halo = n // GRID_W
    c0 = COL_MQK * BR // cb
    kern = functools.partial(_conv_kernel, n_lat_blocks=n_batch * seq // CONV_ROWS,
                             blocks_per_img=seq // CONV_ROWS)
    return pl.pallas_call(
        kern,
        out_shape=jax.ShapeDtypeStruct((n, 2 * BR), BF16),
        grid=(n_blocks, 2 * BR // cb),
        in_specs=[pl.BlockSpec((GRID_W, cb), lambda r, c: (jnp.maximum(r * hb - 1, 0), c0 + c)),
                  pl.BlockSpec((CONV_ROWS, cb), lambda r, c: (r, c0 + c)),
                  pl.BlockSpec((GRID_W, cb), lambda r, c: (jnp.minimum((r + 1) * hb, n_halo - 1), c0 + c)),
                  pl.BlockSpec((None, 9, cb), lambda r, c: (layer, 0, c)),
                  pl.BlockSpec((None, 1, cb), lambda r, c: (layer, 0, c))],
        out_specs=pl.BlockSpec((CONV_ROWS, cb), lambda r, c: (r, c)),
        compiler_params=_cparams(("parallel", "parallel")),
        name="mlstm_conv",
    )(p, p, p, conv_w, conv_b)


def _chunk_block(b, d, i, *, n_ctx, n_lat, n_batch):
    ctx_j = jnp.where(d == 0, i, n_ctx - 1 - i)
    lat_j = jnp.where(d == 0, i - n_ctx, n_ctx + n_lat - 1 - i)
    return jnp.where(i < n_ctx, n_batch * n_lat + b * n_ctx + ctx_j, b * n_lat + lat_j)


def _order_mask(d, t):
    r = lax.broadcasted_iota(jnp.int32, (t, t), 0)
    c = lax.broadcasted_iota(jnp.int32, (t, t), 1)
    return jnp.where(d == 0, r - c, c - r) >= 0


def _ret_kernel(lg_ref, q_ref, k_ref, v_ref, o_ref, s_ref):
    d = pl.program_id(1)
    t = T_ATTN

    @pl.when(pl.program_id(2) == 0)
    def _():
        s_ref[...] = jnp.zeros_like(s_ref)

    r = lax.broadcasted_iota(jnp.int32, (t, t), 0)
    c = lax.broadcasted_iota(jnp.int32, (t, t), 1)
    rel = jnp.where(d == 0, r - c, c - r)
    relf = jnp.maximum(rel, 0).astype(F32)
    tt = lax.broadcasted_iota(jnp.int32, (t, 1), 0)
    pos = jnp.where(d == 0, tt, t - 1 - tt).astype(F32)
    scale = HD ** -0.5
    for h in range(NH):
        sl = slice(h * HD, (h + 1) * HD)
        lg = lg_ref[d, h]
        q = q_ref[:, sl]
        k = k_ref[:, sl]
        v = v_ref[:, sl]
        intra = jnp.where(rel >= 0, jnp.exp(lg * relf), 0.0)
        scores = (_dot_nt(q, k) * scale) * intra
        s_old = s_ref[h]
        out = _dot(scores.astype(BF16), v) + jnp.exp(lg * (pos + 1.0)) * _dot(q, s_old.astype(BF16))
        kd = (k.astype(F32) * (scale * jnp.exp(lg * (t - 1.0 - pos)))).astype(BF16)
        s_ref[h] = jnp.exp(lg * t) * s_old + _dot_tn(kd, v)
        o_ref[:, sl] = out


def _hgrn_direction(fwd, lb_ref, q_ref, z_ref, v_ref, o_ref, st_ref):
    t = T_HGRN
    lb = lb_ref[...]
    z = z_ref[...].astype(F32)
    f = lb + (1.0 - lb) * _sigmoid(z)
    logf = jnp.log(jnp.maximum(f, F_TINY))
    kk = (1.0 - lb) * _sigmoid(-z)
    qq = _silu(q_ref[...].astype(F32))
    r2 = lax.broadcasted_iota(jnp.int32, (t, t), 0)
    c2 = lax.broadcasted_iota(jnp.int32, (t, t), 1)
    tri = jnp.where((c2 <= r2) if fwd else (c2 >= r2), 1.0, 0.0).astype(BF16)
    cum = _cumsum_rows(tri, logf)
    row = lax.broadcasted_iota(jnp.int32, (t, 1), 0)
    if fwd:
        e_ref = jnp.where(row == 0, 0.0, pltpu.roll(cum, 1, 0))
    else:
        e_ref = jnp.where(row == t - 1, 0.0, pltpu.roll(cum, t - 1, 0))
    f_ref = cum
    levels = []
    w = 1
    while w < t:
        q_l = (qq * jnp.exp(cum - e_ref)).astype(BF16)
        k_l = (kk * jnp.exp(f_ref - cum)).astype(BF16)
        late, early = (w, 0) if fwd else (0, w)
        key = ((r2 ^ c2) & ~(2 * w - 1)) | ((r2 & w) ^ late) | ((c2 & w) ^ early)
        levels.append((q_l, k_l, key == 0))
        upper = (row & w) != 0
        if fwd:
            e_ref = jnp.where(upper, pltpu.roll(e_ref, w, 0), e_ref)
            f_ref = jnp.where(upper, f_ref, pltpu.roll(f_ref, t - w, 0))
        else:
            e_ref = jnp.where(upper, e_ref, pltpu.roll(e_ref, t - w, 0))
            f_ref = jnp.where(upper, pltpu.roll(f_ref, w, 0), f_ref)
        w *= 2
    q_b = qq.astype(BF16)
    k_b = kk.astype(BF16)
    tail = cum[t - 1:t] if fwd else cum[0:1]
    q_state = (qq * jnp.exp(cum)).astype(BF16)
    k_state = (kk * jnp.exp(tail - cum)).astype(BF16)
    decay = jnp.exp(tail)
    for h in range(NH):
        sl = slice(h * HD, (h + 1) * HD)
        v_h = v_ref[:, sl]
        scores = jnp.where(r2 == c2, _dot_nt(q_b[:, sl], k_b[:, sl]), 0.0)
        for q_l, k_l, m in levels:
            scores = jnp.where(m, _dot_nt(q_l[:, sl], k_l[:, sl]), scores)
        st = st_ref[h]
        o_ref[:, sl] = _dot(scores.astype(BF16), v_h) + _dot_nt(q_state[:, sl], st.astype(BF16))
        st_ref[h] = st * decay[:, sl] + _dot_tn(v_h, k_state[:, sl])


def _hgrn_kernel(lb_ref, q_ref, z_ref, v_ref, o_ref, st_ref):
    d = pl.program_id(1)

    @pl.when(pl.program_id(2) == 0)
    def _():
        st_ref[...] = jnp.zeros_like(st_ref)

    @pl.when(d == 0)
    def _():
        _hgrn_direction(True, lb_ref, q_ref, z_ref, v_ref, o_ref, st_ref)

    @pl.when(d == 1)
    def _():
        _hgrn_direction(False, lb_ref, q_ref, z_ref, v_ref, o_ref, st_ref)


def _log_sigmoid(x):
    return jnp.minimum(x, 0.0) - jnp.log(1.0 + jnp.exp(-jnp.abs(x)))


def _mlstm_kernel(qk_ref, v_ref, g_ref, gb_ref, o_ref, c_ref, n_ref, m_ref):
    d = pl.program_id(1)
    t = T_ATTN

    @pl.when(pl.program_id(2) == 0)
    def _():
        c_ref[...] = jnp.zeros_like(c_ref)
        n_ref[...] = jnp.zeros_like(n_ref)
        m_ref[...] = jnp.zeros_like(m_ref)

    g = g_ref[...] + gb_ref[...]
    gt = g.T
    mask = _order_mask(d, t)
    tri = jnp.where(mask, 1.0, 0.0).astype(BF16)
    lf = _log_sigmoid(g)
    cum_c = _cumsum_rows(tri, lf)
    lt_hi, lt_mid, lt_lo = _split3(lf.T)
    cum_r = _dot_nt(lt_hi, tri) + (_dot_nt(lt_mid, tri) + _dot_nt(lt_lo, tri))
    scale = HD ** -0.5
    for h in range(NH):
        sl = slice(h * HD, (h + 1) * HD)
        li, lfw = h, NH + h
        i_col = jnp.where(d == 0, g[:, li:li + 1], g[:, 8 + li:8 + li + 1])
        i_row = jnp.where(d == 0, gt[li:li + 1], gt[8 + li:8 + li + 1])
        cc = jnp.where(d == 0, cum_c[:, lfw:lfw + 1], cum_c[:, 8 + lfw:8 + lfw + 1])
        cr = jnp.where(d == 0, cum_r[lfw:lfw + 1], cum_r[8 + lfw:8 + lfw + 1])
        total = jnp.where(d == 0, cr[:, t - 1:t], cr[:, 0:1])
        m_old = m_ref[h][:, 0:1]
        q = qk_ref[:, sl]
        k = qk_ref[:, BR + h * HD:BR + (h + 1) * HD]
        v = v_ref[:, sl]
        logw = jnp.where(mask, cc - cr + i_row, NEG_BIG)
        from_state = cc + m_old
        m_t = jnp.maximum(from_state, jnp.max(logw, axis=1, keepdims=True))
        w = jnp.exp(logw - m_t)
        w_state = jnp.exp(from_state - m_t)
        scores = (_dot_nt(q, k) * scale) * w
        c_old = c_ref[h]
        n_old = n_ref[h]
        num = _dot(scores.astype(BF16), v) + w_state * _dot(q, c_old.astype(BF16))
        den = (jnp.sum(scores, axis=1, keepdims=True)
               + w_state * jnp.sum(q.astype(F32) * n_old, axis=1, keepdims=True))
        o_ref[:, sl] = num / jnp.maximum(jnp.abs(den), jnp.exp(-m_t))
        m_new = jnp.maximum(total + m_old, jnp.max(total - cr + i_row, axis=1, keepdims=True))
        keep = jnp.exp(total + m_old - m_new)
        w_end = jnp.exp(total - cc + i_col - m_new)
        kw = k.astype(F32) * (scale * w_end)
        c_ref[h] = keep * c_old + _dot_tn(kw.astype(BF16), v)
        n_ref[h] = keep * n_old + jnp.sum(kw, axis=0, keepdims=True)
        m_ref[h] = jnp.broadcast_to(m_new, (1, HD))


def _scans(p, gates, qk, lg, lb, gate_b, layer, *, n_batch, seq, ctx_len):
    n = p.shape[0]
    state = pltpu.VMEM((NH, HD, HD), F32)

    def call(kernel, t, inputs, make_specs, scratch, name):
        n_ctx, n_lat = ctx_len // t, seq // t
        blk = functools.partial(_chunk_block, n_ctx=n_ctx, n_lat=n_lat, n_batch=n_batch)

        def rows(width, cblock=0):
            return pl.BlockSpec((t, width), lambda b, d, i: (blk(b, d, i), cblock))

        return pl.pallas_call(
            kernel,
            out_shape=jax.ShapeDtypeStruct((2, n, BR), F32),
            grid=(n_batch, 2, n_ctx + n_lat),
            in_specs=make_specs(rows, blk),
            out_specs=pl.BlockSpec((None, t, BR), lambda b, d, i: (d, blk(b, d, i), 0)),
            scratch_shapes=scratch,
            compiler_params=_cparams(("parallel", "arbitrary", "arbitrary")),
            name=name,
        )(*inputs)

    raw_b = call(_ret_kernel, T_ATTN, (lg, p, p, p),
                 lambda rows, blk: [pl.BlockSpec(memory_space=pltpu.SMEM),
                                    rows(BR, COL_RQ), rows(BR, COL_RK), rows(BR, COL_RV)],
                 [state], "retention_scan")
    raw_a = call(_hgrn_kernel, T_HGRN, (lb, p, p, p),
                 lambda rows, blk: [pl.BlockSpec((None, None, 1, BR), lambda b, d, i: (layer, d, 0, 0)),
                                    rows(BR, COL_HQ),
                                    pl.BlockSpec((T_HGRN, BR), lambda b, d, i: (blk(b, d, i), COL_HF + d)),
                                    rows(BR, COL_HV)],
                 [state], "hgrn2_scan")
    raw_d = call(_mlstm_kernel, T_ATTN, (qk, p, gates, gate_b),
                 lambda rows, blk: [rows(2 * BR), rows(BR, COL_MV), rows(GATE_PAD),
                                    pl.BlockSpec((None, 1, GATE_PAD), lambda b, d, i: (layer, 0, 0))],
                 [state, pltpu.VMEM((NH, 1, HD), F32), pltpu.VMEM((NH, 1, HD), F32)], "mlstm_scan")
    return raw_a, raw_b, raw_d


def _cmul(ar, ai, br, bi):
    return ar * br - ai * bi, ar * bi + ai * br


def _pow_table(lam_r, lam_i, expo):
    pr = jnp.ones(expo.shape, F32)
    pi = jnp.zeros(expo.shape, F32)
    ar = jnp.broadcast_to(lam_r, expo.shape)
    ai = jnp.broadcast_to(lam_i, expo.shape)
    for k in range(S5_T.bit_length() - 1):
        nr, ni = _cmul(pr, pi, ar, ai)
        bit = (expo & (1 << k)) != 0
        pr, pi = jnp.where(bit, nr, pr), jnp.where(bit, ni, pi)
        ar, ai = _cmul(ar, ai, ar, ai)
    return pr, pi


def _s5_kernel(u_ref, col_ref, row_ref, o_ref, m_ref, w_ref, ys_ref, z_ref, wr_ref, wi_ref, xr_ref, xi_ref, *,
               n_batch, n_ctx, n_lat):
    d = pl.program_id(1)
    t = S5_T
    col = col_ref[...]
    rowp = row_ref[...]
    lam_rc, lam_ic = col[:, 0:1], col[:, 1:2]
    bbr, bbi = rowp[0:S5_HG], rowp[S5_HG:2 * S5_HG]
    lam_rr, lam_ir = rowp[2 * S5_HG:2 * S5_HG + 1], rowp[2 * S5_HG + 1:2 * S5_HG + 2]
    lane = lax.broadcasted_iota(jnp.int32, (t, t), 1)
    sub = lax.broadcasted_iota(jnp.int32, (t, t), 0)

    pr, pi = _pow_table(lam_rc, lam_ic, jnp.where(d == 0, lane, t - 1 - lane))
    qr, qi = _cmul(pr, pi, lam_rc, lam_ic)
    for h in range(S5_HG):
        cr, ci = col[:, 2 + h:3 + h], col[:, 2 + S5_HG + h:3 + S5_HG + h]
        hs = slice(h * t, (h + 1) * t)
        z_ref[0:S5_PP, hs] = cr * pr - ci * pi
        z_ref[S5_PP:2 * S5_PP, hs] = -(cr * pi + ci * pr)
        ys_ref[0:S5_PP, hs] = (cr * qr - ci * qi).astype(BF16)
        ys_ref[S5_PP:2 * S5_PP, hs] = (-(cr * qi + ci * qr)).astype(BF16)
    b_hi, b_mid, _ = _split3(jnp.concatenate([bbr, bbi], axis=1))
    z_hi, z_mid, _ = _split3(z_ref[...])
    krow = _dot(b_hi, z_hi) + (_dot(b_hi, z_mid) + _dot(b_mid, z_hi))

    def build_m(shift, keep):
        for hp in range(S5_HG):
            for h in range(S5_HG):
                tile = jnp.broadcast_to(krow[hp:hp + 1, h * t:(h + 1) * t], (t, t))
                tile = pltpu.roll(tile, shift, 1, stride=1, stride_axis=0)
                m_ref[hp * t:(hp + 1) * t, h * t:(h + 1) * t] = jnp.where(keep, tile, 0.0).astype(BF16)

    @pl.when(d == 0)
    def _():
        build_m(0, lane >= sub)

    @pl.when(d == 1)
    def _():
        build_m(1, lane <= sub)

    tr, ti = _pow_table(lam_rr, lam_ir, jnp.where(d == 0, t - 1 - sub, sub))
    for hp in range(S5_HG):
        br, bi = bbr[hp:hp + 1], bbi[hp:hp + 1]
        w_ref[hp * t:(hp + 1) * t, 0:S5_PP] = (tr * br - ti * bi).astype(BF16)
        w_ref[hp * t:(hp + 1) * t, S5_PP:2 * S5_PP] = (tr * bi + ti * br).astype(BF16)

    u = jnp.concatenate([u_ref[hp] for hp in range(S5_HG)], axis=1)
    wv = _dot(u, w_ref[...])
    wr_ref[...] = wv[:, 0:S5_PP]
    wi_ref[...] = wv[:, S5_PP:2 * S5_PP]
    lr, li = lam_rr, lam_ir
    for _ in range(t.bit_length() - 1):
        lr, li = _cmul(lr, li, lr, li)

    def make_step(base, stride, count):
        def step(i, carry):
            xr, xi = carry
            j = jnp.where(d == 0, i, count - 1 - i)
            idx = pl.ds(base + j, n_batch, stride=stride)
            xr_ref[idx, :] = xr
            xi_ref[idx, :] = xi
            nr, ni = _cmul(xr, xi, lr, li)
            return nr + wr_ref[idx, :], ni + wi_ref[idx, :]
        return step

    zero = jnp.zeros((n_batch, S5_PP), F32)
    carry = lax.fori_loop(0, n_ctx, make_step(n_batch * n_lat, n_ctx, n_ctx), (zero, zero))
    lax.fori_loop(0, n_lat, make_step(0, n_lat, n_lat), carry)
    x_prev = jnp.concatenate([xr_ref[...], xi_ref[...]], axis=1).astype(BF16)
    y = _dot(u, m_ref[...]) + _dot(x_prev, ys_ref[...])

    @pl.when(d == 0)
    def _():
        for h in range(S5_HG):
            o_ref[h] = y[:, h * t:(h + 1) * t]

    @pl.when(d == 1)
    def _():
        for h in range(S5_HG):
            o_ref[h] += y[:, h * t:(h + 1) * t]


def _s5_params(a_re, a_im, log_dt, b_re, b_im, c_re, c_im):
    a_re, a_im = a_re.astype(F32), a_im.astype(F32)
    dt = jnp.exp(log_dt.astype(F32))[..., None]
    mag = jnp.exp(a_re * dt)
    lam_re, lam_im = mag * jnp.cos(a_im * dt), mag * jnp.sin(a_im * dt)
    den = a_re * a_re + a_im * a_im
    num_re, num_im = lam_re - 1.0, lam_im
    fr = (num_re * a_re + num_im * a_im) / den
    fi = (num_im * a_re - num_re * a_im) / den
    b_re, b_im = b_re.astype(F32), b_im.astype(F32)
    bb_re = fr[..., None] * b_re - fi[..., None] * b_im
    bb_im = fr[..., None] * b_im + fi[..., None] * b_re
    c_re_t = jnp.swapaxes(c_re.astype(F32), -1, -2)
    c_im_t = jnp.swapaxes(c_im.astype(F32), -1, -2)
    col = jnp.concatenate([lam_re[..., None], lam_im[..., None], c_re_t, c_im_t], axis=-1)
    col = jnp.pad(col, ((0, 0),) * 3 + ((0, S5_PP - S5_P), (0, 128 - col.shape[-1])))
    row = jnp.concatenate([jnp.swapaxes(bb_re, -1, -2), jnp.swapaxes(bb_im, -1, -2),
                           lam_re[..., None, :], lam_im[..., None, :]], axis=-2)
    row = jnp.pad(row, ((0, 0),) * 3 + ((0, S5_ROWS - row.shape[-2]), (0, S5_PP - S5_P)))
    return col, row


def _s5_scan(p, col, row, layer, *, n_batch, seq, ctx_len):
    t = S5_T
    n = p.shape[0]
    nc = n // t
    n_ctx, n_lat = ctx_len // t, seq // t
    ut = p[:, COL_SU * BR:(COL_SU + 1) * BR].T.reshape(S5_G, S5_HG, nc, t)
    kern = functools.partial(_s5_kernel, n_batch=n_batch, n_ctx=n_ctx, n_lat=n_lat)
    yt = pl.pallas_call(
        kern,
        out_shape=jax.ShapeDtypeStruct((S5_G, S5_HG, nc, t), F32),
        grid=(S5_G, 2),
        in_specs=[pl.BlockSpec((None, S5_HG, nc, t), lambda g, d: (g, 0, 0, 0)),
                  pl.BlockSpec((None, None, None, S5_PP, 128), lambda g, d: (layer, d, g, 0, 0)),
                  pl.BlockSpec((None, None, None, S5_ROWS, S5_PP), lambda g, d: (layer, d, g, 0, 0))],
        out_specs=pl.BlockSpec((None, S5_HG, nc, t), lambda g, d: (g, 0, 0, 0)),
        scratch_shapes=[pltpu.VMEM((S5_HG * t, S5_HG * t), BF16),
                        pltpu.VMEM((S5_HG * t, 2 * S5_PP), BF16),
                        pltpu.VMEM((2 * S5_PP, S5_HG * t), BF16),
                        pltpu.VMEM((2 * S5_PP, S5_HG * t), F32),
                        pltpu.VMEM((nc, S5_PP), F32), pltpu.VMEM((nc, S5_PP), F32),
                        pltpu.VMEM((nc, S5_PP), F32), pltpu.VMEM((nc, S5_PP), F32)],
        compiler_params=_cparams(("parallel", "arbitrary")),
        name="s5_scan",
    )(ut, col, row)
    return yt.reshape(BR, n).T


def _head_rms(x, g):
    parts = []
    for h in range(NH):
        xh = x[:, h * HD:(h + 1) * HD]
        parts.append(xh * lax.rsqrt(jnp.mean(jnp.square(xh), axis=-1, keepdims=True) + NORM_EPS))
    return jnp.concatenate(parts, axis=-1) * g


def _finish_kernel(ra_ref, rb_ref, rc_ref, rd_ref, ga_ref, gb_ref, u_ref, gd_ref, mg_ref, x_ref, mod_ref,
                   vec_ref, glu_w_ref, wb_ref, wo_ref, o_ref):
    vec = vec_ref[...]
    oa = _head_rms(ra_ref[0] + ra_ref[1], vec[0:1]) * _silu(ga_ref[...].astype(F32))
    ob = _head_rms(rb_ref[0] + rb_ref[1], vec[1:2]) * _silu(gb_ref[...].astype(F32))
    od = _head_rms(rd_ref[0] + rd_ref[1], vec[2:3]) * _silu(gd_ref[...].astype(F32))
    yc = rc_ref[...] + vec[3:4] * u_ref[...].astype(F32)
    yc = 0.5 * yc * (1.0 + lax.erf(yc * (2.0 ** -0.5)))
    oc = yc * _sigmoid(_dot(yc.astype(BF16), glu_w_ref[...]) + vec[4:5])
    y = None
    for j, o in enumerate((oa, ob, oc, od)):
        gate = _sigmoid(mg_ref[:, j * D_MODEL:(j + 1) * D_MODEL].astype(F32))
        term = gate * _dot(o.astype(BF16), wb_ref[j])
        y = term if y is None else y + term
    mix = _dot(y.astype(BF16), wo_ref[...])
    o_ref[...] = x_ref[...] + mod_ref[2:3, :] * mix


def _finish(raw_a, raw_b, raw_c, raw_d, p, x_all, mod, vec, glu_w, w_branch, w_out, layer, *, n_rows, n_batch,
            seq, tm=256):
    mrow = _mod_row_map(n_rows, tm, n_batch, seq)
    one = pl.Buffered(1)

    def raw(i):
        return (0, i, 0)

    def pcol(cblock):
        return pl.BlockSpec((tm, BR), lambda i: (i, cblock))

    return pl.pallas_call(
        _finish_kernel,
        out_shape=jax.ShapeDtypeStruct((n_rows, D_MODEL), F32),
        grid=(n_rows // tm,),
        in_specs=[pl.BlockSpec((2, tm, BR), raw),
                  pl.BlockSpec((2, tm, BR), raw),
                  pl.BlockSpec((tm, BR), lambda i: (i, 0)),
                  pl.BlockSpec((2, tm, BR), raw),
                  pcol(COL_HG), pcol(COL_RG), pcol(COL_SU), pcol(COL_MZ),
                  pl.BlockSpec((tm, N_BRANCH * D_MODEL), lambda i: (i, COL_MERGE)),
                  pl.BlockSpec((tm, D_MODEL), lambda i: (i, 0)),
                  pl.BlockSpec((None, 8, D_MODEL), lambda i: (mrow(i), 0, 0)),
                  pl.BlockSpec((None, 8, BR), lambda i: (layer, 0, 0), pipeline_mode=one),
                  pl.BlockSpec((None, BR, BR), lambda i: (layer, 0, 0), pipeline_mode=one),
                  pl.BlockSpec((None, N_BRANCH, BR, D_MODEL), lambda i: (layer, 0, 0, 0), pipeline_mode=one),
                  pl.BlockSpec((None, D_MODEL, D_MODEL), lambda i: (layer, 0, 0), pipeline_mode=one)],
        out_specs=pl.BlockSpec((tm, D_MODEL), lambda i: (i, 0)),
        compiler_params=_cparams(("parallel",)),
        name="finish",
    )(raw_a, raw_b, raw_c, raw_d, p, p, p, p, p, x_all, mod, vec, glu_w, w_branch, w_out)


def _mlp_kernel(x_ref, mod_ref, g_ref, w1_ref, w2_ref, fin_ref, o_ref, xn_ref, acc_ref, *, final_norm):
    j = pl.program_id(1)

    @pl.when(j == 0)
    def _():
        h = _norm_mod(x_ref[...], g_ref[...], mod_ref[3:4, :], mod_ref[4:5, :])
        xn_ref[...] = h.astype(BF16)
        acc_ref[...] = jnp.zeros_like(acc_ref)

    a = jnp.square(jnp.maximum(_dot(xn_ref[...], w1_ref[...]), 0.0))
    acc_ref[...] += _dot(a.astype(BF16), w2_ref[...])

    @pl.when(j == pl.num_programs(1) - 1)
    def _():
        y = x_ref[...] + mod_ref[5:6, :] * acc_ref[...]
        if final_norm:
            y = (y * lax.rsqrt(jnp.mean(jnp.square(y), axis=-1, keepdims=True) + NORM_EPS)) * fin_ref[...]
        o_ref[...] = y


def _mlp(x_all, mod, norm_w, w1, w2, fin_w, layer, *, n_rows, n_batch, seq, final_norm, tm=512, tf=1024):
    mrow = _mod_row_map(n_rows, tm, n_batch, seq)
    kern = functools.partial(_mlp_kernel, final_norm=final_norm)
    return pl.pallas_call(
        kern,
        out_shape=jax.ShapeDtypeStruct((n_rows, D_MODEL), F32),
        grid=(n_rows // tm, D_FF // tf),
        in_specs=[pl.BlockSpec((tm, D_MODEL), lambda i, j: (i, 0)),
                  pl.BlockSpec((None, 8, D_MODEL), lambda i, j: (mrow(i), 0, 0)),
                  pl.BlockSpec((None, 1, D_MODEL), lambda i, j: (layer, 0, 0)),
                  pl.BlockSpec((None, D_MODEL, tf), lambda i, j: (layer, 0, j)),
                  pl.BlockSpec((None, tf, D_MODEL), lambda i, j: (layer, j, 0)),
                  pl.BlockSpec((1, D_MODEL), lambda i, j: (0, 0))],
        out_specs=pl.BlockSpec((tm, D_MODEL), lambda i, j: (i, 0)),
        scratch_shapes=[pltpu.VMEM((tm, D_MODEL), BF16), pltpu.VMEM((tm, D_MODEL), F32)],
        compiler_params=_cparams(("parallel", "arbitrary")),
        name="mlp",
    )(x_all, mod, norm_w, w1, w2, fin_w.reshape(1, D_MODEL))


W_PREP_TN = 1024
N_ALIGNED = GATE_OFF // W_PREP_TN
N_MERGE = N_BRANCH * D_MODEL // W_PREP_TN


def _w_prep_kernel(a_ref, b_ref, g_ref, main_ref, gate_ref):
    j = pl.program_id(2)
    ng = 4 * NH

    @pl.when(j == 0)
    def _():
        lane = lax.broadcasted_iota(jnp.int32, gate_ref.shape, 1)
        gate_ref[...] = jnp.where(lane < ng, g_ref[...], 0.0).astype(BF16)

    @pl.when(j < N_MERGE)
    def _():
        main_ref[...] = jnp.concatenate([a_ref[:, ng:], b_ref[:, :ng]], axis=1).astype(BF16)

    @pl.when(j >= N_MERGE)
    def _():
        main_ref[...] = a_ref[...].astype(BF16)


def _split_w_in(w, tr=512):
    depth = w.shape[0]
    tn = W_PREP_TN
    sub = tn // GATE_PAD
    a_map = lambda l, r, j: (l, r, jnp.where(j < N_MERGE, N_ALIGNED + j, j - N_MERGE))
    b_map = lambda l, r, j: (l, r, jnp.where(j < N_MERGE, (N_ALIGNED + j + 1) * sub, 0))
    return pl.pallas_call(
        _w_prep_kernel,
        out_shape=(jax.ShapeDtypeStruct((depth, D_MODEL, P_WIDTH), BF16),
                   jax.ShapeDtypeStruct((depth, D_MODEL, GATE_PAD), BF16)),
        grid=(depth, D_MODEL // tr, N_MERGE + N_ALIGNED),
        in_specs=[pl.BlockSpec((None, tr, tn), a_map),
                  pl.BlockSpec((None, tr, GATE_PAD), b_map),
                  pl.BlockSpec((None, tr, GATE_PAD), lambda l, r, j: (l, r, N_ALIGNED * sub))],
        out_specs=(pl.BlockSpec((None, tr, tn), lambda l, r, j: (l, r, j)),
                   pl.BlockSpec((None, tr, GATE_PAD), lambda l, r, j: (l, r, 0))),
        compiler_params=_cparams(("parallel", "parallel", "arbitrary")),
        name="w_in_prep",
    )(w, w, w)


def kernel(x, c, ctx, c_ctx, w_mod, b_mod, norm_mix, norm_mlp, w_in, hgrn_lb_logits, hgrn_norm, ret_decay, ret_norm, s5_a_re, s5_a_im, s5_log_dt, s5_b_re, s5_b_im, s5_c_re, s5_c_im, s5_d, s5_glu_w, s5_glu_b, mlstm_conv_w, mlstm_conv_b, mlstm_gate_b, mlstm_norm, w_branch, w_out, w_ff1, w_ff2, final_norm):
    n_batch, seq, _ = x.shape
    ctx_len = ctx.shape[1]
    depth = w_in.shape[0]
    nl = n_batch * seq
    dims = dict(n_batch=n_batch, seq=seq)

    p_lb = jax.nn.softmax(hgrn_lb_logits.astype(F32), axis=0)
    lower_bounds = (jnp.cumsum(p_lb, axis=0) - p_lb[0]).reshape(depth, 2, 1, BR)
    lg = jnp.log1p(-jnp.exp(ret_decay.astype(F32)))
    gate_b = jnp.pad(mlstm_gate_b.reshape(depth, 1, 4 * NH).astype(F32), ((0, 0), (0, 0), (0, GATE_PAD - 4 * NH)))
    vec = jnp.stack([hgrn_norm, ret_norm, mlstm_norm, s5_d, s5_glu_b], axis=1).astype(F32)
    vec = jnp.pad(vec, ((0, 0), (0, 8 - vec.shape[1]), (0, 0)))
    w_main, w_gate = _split_w_in(w_in)
    glu_w, wb, wo = s5_glu_w.astype(BF16), w_branch.astype(BF16), w_out.astype(BF16)
    w1, w2 = w_ff1.astype(BF16), w_ff2.astype(BF16)
    norm_mix3 = norm_mix.reshape(depth, 1, D_MODEL)
    norm_mlp3 = norm_mlp.reshape(depth, 1, D_MODEL)
    conv_w = mlstm_conv_w.reshape(depth, 9, 2 * BR)
    conv_b = mlstm_conv_b.reshape(depth, 1, 2 * BR)
    s5_col, s5_row = _s5_params(s5_a_re, s5_a_im, s5_log_dt, s5_b_re, s5_b_im, s5_c_re, s5_c_im)

    c_pad = jnp.zeros((8, D_MODEL), F32).at[:n_batch].set(c).at[n_batch].set(c_ctx)
    x_all = jnp.concatenate([x.reshape(nl, D_MODEL), ctx.reshape(n_batch * ctx_len, D_MODEL)], axis=0)

    for l in range(depth):
        last = l == depth - 1
        mod = _modulation(c_pad, w_mod, b_mod, l)
        mod = jnp.pad(mod[:n_batch + 1].reshape(n_batch + 1, N_MOD, D_MODEL), ((0, 0), (0, 8 - N_MOD), (0, 0)))
        p, gates = _in_proj(x_all, mod, norm_mix3, w_main, w_gate, l, **dims)
        qk = _mlstm_conv(p, conv_w, conv_b, l, ctx_len=ctx_len, **dims)
        raw_a, raw_b, raw_d = _scans(p, gates, qk, lg[l], lower_bounds, gate_b, l, ctx_len=ctx_len, **dims)
        raw_c = _s5_scan(p, s5_col, s5_row, l, ctx_len=ctx_len, **dims)
        n_rows = nl if last else x_all.shape[0]
        x_mid = _finish(raw_a, raw_b, raw_c, raw_d, p, x_all, mod, vec, glu_w, wb, wo, l, n_rows=n_rows, **dims)
        x_all = _mlp(x_mid, mod, norm_mlp3, w1, w2, final_norm, l, n_rows=n_rows, final_norm=last, **dims)
    return x_all.reshape(n_batch, seq, D_MODEL)
```

```python
import functools
import math

import jax
import jax.numpy as jnp
from jax import lax
from jax.experimental import pallas as pl
from jax.experimental.pallas import tpu as pltpu

F32 = jnp.float32
BF16 = jnp.bfloat16

D_MODEL = 2048
N_BRANCH = 4
BR = D_MODEL // N_BRANCH
HD = 128
NH = BR // HD
S5_HG = 16
S5_G = BR // S5_HG
S5_P = 64
D_FF = 4 * D_MODEL
N_MOD = 6
GRID_W = 64
NORM_EPS = 1e-6
NEG_BIG = -1e30
F_TINY = 1e-30
S5_DT_MIN = 1e-3

COL_MERGE = 0
COL_HQ, COL_HF, COL_HV, COL_HG = 16, 17, 19, 20
COL_RQ, COL_RK, COL_RV, COL_RG = 21, 22, 23, 24
COL_SU = 25
COL_MQK, COL_MV, COL_MZ = 26, 28, 29
P_WIDTH = 30 * BR
GATE_OFF = 14 * BR
GATE_PAD = 128

T_HGRN = 128
T_ATTN = 256
S5_T = 128
S5_PP = 128
S5_ROWS = 40
VMEM_LIMIT = 56 * 1024 * 1024

_HI = lax.Precision.HIGHEST


def _cparams(sem):
    return pltpu.CompilerParams(dimension_semantics=sem, vmem_limit_bytes=VMEM_LIMIT)


def _dot(a, b):
    return jnp.dot(a, b, preferred_element_type=F32)


def _dot_nt(a, b):
    return lax.dot_general(a, b, (((1,), (1,)), ((), ())), preferred_element_type=F32)


def _dot_tn(a, b):
    return lax.dot_general(a, b, (((0,), (0,)), ((), ())), preferred_element_type=F32)


def _split3(x):
    hi = x.astype(BF16)
    r1 = x - hi.astype(F32)
    mid = r1.astype(BF16)
    return hi, mid, (r1 - mid.astype(F32)).astype(BF16)


def _cumsum_rows(tri, x):
    hi, mid, lo = _split3(x)
    return _dot(tri, hi) + (_dot(tri, mid) + _dot(tri, lo))


def _sigmoid(x):
    return 1.0 / (1.0 + jnp.exp(-x))


def _silu(x):
    return x * _sigmoid(x)


def _mod_kernel(c_ref, w_ref, b_ref, o_ref):
    @pl.when(pl.program_id(0) == 0)
    def _():
        o_ref[...] = jnp.broadcast_to(b_ref[...], o_ref.shape)

    a = _silu(c_ref[...]).astype(BF16)
    o_ref[...] += _dot(a, w_ref[...].astype(BF16))


def _modulation(c_pad, w_mod, b_mod, layer, tk=256):
    n = w_mod.shape[2]
    return pl.pallas_call(
        _mod_kernel,
        out_shape=jax.ShapeDtypeStruct((8, n), F32),
        grid=(D_MODEL // tk,),
        in_specs=[pl.BlockSpec((8, tk), lambda k: (0, k)),
                  pl.BlockSpec((None, tk, n), lambda k: (layer, k, 0)),
                  pl.BlockSpec((None, 1, n), lambda k: (layer, 0, 0))],
        out_specs=pl.BlockSpec((8, n), lambda k: (0, 0)),
        compiler_params=_cparams(("arbitrary",)),
        name="modulation",
    )(c_pad, w_mod, b_mod.reshape(b_mod.shape[0], 1, n))


def _norm_mod(x, g, shift, scale):
    y = x * lax.rsqrt(jnp.mean(jnp.square(x), axis=-1, keepdims=True) + NORM_EPS)
    return (y * g) * (1.0 + scale) + shift


def _inproj_kernel(x_ref, mod_ref, g_ref, w_ref, wg_ref, p_ref, gate_ref, xn_ref):
    @pl.when(pl.program_id(1) == 0)
    def _():
        h = _norm_mod(x_ref[...], g_ref[...], mod_ref[0:1, :], mod_ref[1:2, :])
        xn_ref[...] = h.astype(BF16)
        gate_ref[...] = _dot_nt(xn_ref[...], wg_ref[...])

    p_ref[...] = _dot_nt(xn_ref[...], w_ref[...]).astype(p_ref.dtype)


def _mod_row_map(n_rows, tm, n_batch, seq):
    assert n_rows % tm == 0 and seq % tm == 0, (n_rows, seq, tm)
    n_lat_tiles, tiles_per_batch = n_batch * seq // tm, seq // tm

    def row(i):
        return jnp.where(i < n_lat_tiles, i // tiles_per_batch, n_batch)
    return row


def _in_proj(x_all, mod, norm_w, w_main, w_gate, layer, *, n_batch, seq, tm=1024, tn=1024):
    n = x_all.shape[0]
    mrow = _mod_row_map(n, tm, n_batch, seq)
    return pl.pallas_call(
        _inproj_kernel,
        out_shape=(jax.ShapeDtypeStruct((n, P_WIDTH), BF16),
                   jax.ShapeDtypeStruct((n, GATE_PAD), F32)),
        grid=(n // tm, P_WIDTH // tn),
        in_specs=[pl.BlockSpec((tm, D_MODEL), lambda i, j: (i, 0)),
                  pl.BlockSpec((None, 8, D_MODEL), lambda i, j: (mrow(i), 0, 0)),
                  pl.BlockSpec((None, 1, D_MODEL), lambda i, j: (layer, 0, 0)),
                  pl.BlockSpec((None, tn, D_MODEL), lambda i, j: (layer, j, 0)),
                  pl.BlockSpec((None, GATE_PAD, D_MODEL), lambda i, j: (layer, 0, 0))],
        out_specs=(pl.BlockSpec((tm, tn), lambda i, j: (i, j)),
                   pl.BlockSpec((tm, GATE_PAD), lambda i, j: (i, 0))),
        scratch_shapes=[pltpu.VMEM((tm, D_MODEL), BF16)],
        compiler_params=_cparams(("parallel", "arbitrary")),
        name="in_proj",
    )(x_all, mod, norm_w, w_main, w_gate)


CONV_ROWS = 256


def _conv_kernel(up_ref, x_ref, dn_ref, w_ref, b_ref, o_ref, *, n_lat_blocks, blocks_per_img):
    rb = pl.program_id(0)
    x = x_ref[...].astype(F32)
    w = w_ref[...]
    n = CONV_ROWS

    @pl.when(rb < n_lat_blocks)
    def _():
        rr = rb % blocks_per_img
        up = jnp.where(rr > 0, up_ref[...].astype(F32), 0.0)
        dn = jnp.where(rr < blocks_per_img - 1, dn_ref[...].astype(F32), 0.0)
        ext = jnp.concatenate([up, x, dn], axis=0)
        ne = n + 2 * GRID_W
        col = lax.broadcasted_iota(jnp.int32, (ne, 1), 0) % GRID_W
        xl = jnp.where(col >= 1, pltpu.roll(ext, 1, 0), 0.0)
        xr = jnp.where(col <= GRID_W - 2, pltpu.roll(ext, ne - 1, 0), 0.0)
        acc = None
        for i in range(3):
            y = w[3 * i:3 * i + 1] * xl + w[3 * i + 1:3 * i + 2] * ext + w[3 * i + 2:3 * i + 3] * xr
            part = y[i * GRID_W:i * GRID_W + n]
            acc = part if acc is None else acc + part
        o_ref[...] = _silu(acc + b_ref[...]).astype(o_ref.dtype)

    @pl.when(rb >= n_lat_blocks)
    def _():
        t = lax.broadcasted_iota(jnp.int32, (n, 1), 0)
        xl = jnp.where(t >= 1, pltpu.roll(x, 1, 0), 0.0)
        xr = jnp.where(t <= n - 2, pltpu.roll(x, n - 1, 0), 0.0)
        acc = w[3:4] * xl + w[4:5] * x + w[5:6] * xr
        o_ref[...] = _silu(acc + b_ref[...]).astype(o_ref.dtype)


def _mlstm_conv(p, conv_w, conv_b, layer, *, n_batch, seq, ctx_len, cb=2 * BR):
    assert ctx_len == CONV_ROWS and seq % CONV_ROWS == 0
    n = p.shape[0]
    n_blocks = n // CONV_ROWS
    hb = CONV_ROWS // GRID_W
    n_halo = n // GRID_W
    c0 = COL_MQK * BR // cb
    kern = functools.partial(_conv_kernel, n_lat_blocks=n_batch * seq // CONV_ROWS,
                             blocks_per_img=seq // CONV_ROWS)
    return pl.pallas_call(
        kern,
        out_shape=jax.ShapeDtypeStruct((n, 2 * BR), BF16),
        grid=(n_blocks, 2 * BR // cb),
        in_specs=[pl.BlockSpec((GRID_W, cb), lambda r, c: (jnp.maximum(r * hb - 1, 0), c0 + c)),
                  pl.BlockSpec((CONV_ROWS, cb), lambda r, c: (r, c0 + c)),
                  pl.BlockSpec((GRID_W, cb), lambda r, c: (jnp.minimum((r + 1) * hb, n_halo - 1), c0 + c)),
                  pl.BlockSpec((None, 9, cb), lambda r, c: (layer, 0, c)),
                  pl.BlockSpec((None, 1, cb), lambda r, c: (layer, 0, c))],
        out_specs=pl.BlockSpec((CONV_ROWS, cb), lambda r, c: (r, c)),
        compiler_params=_cparams(("parallel", "parallel")),
        name="mlstm_conv",
    )(p, p, p, conv_w, conv_b)


def _chunk_block(b, d, i, *, n_ctx, n_lat, n_batch):
    ctx_j = jnp.where(d == 0, i, n_ctx - 1 - i)
    lat_j = jnp.where(d == 0, i - n_ctx, n_ctx + n_lat - 1 - i)
    return jnp.where(i < n_ctx, n_batch * n_lat + b * n_ctx + ctx_j, b * n_lat + lat_j)


def _order_mask(d, t):
    r = lax.broadcasted_iota(jnp.int32, (t, t), 0)
    c = lax.broadcasted_iota(jnp.int32, (t, t), 1)
    return jnp.where(d == 0, r - c, c - r) >= 0


def _ret_kernel(lg_ref, q_ref, k_ref, v_ref, o_ref, s_ref):
    d = pl.program_id(1)
    t = T_ATTN

    @pl.when(pl.program_id(2) == 0)
    def _():
        s_ref[...] = jnp.zeros_like(s_ref)

    r = lax.broadcasted_iota(jnp.int32, (t, t), 0)
    c = lax.broadcasted_iota(jnp.int32, (t, t), 1)
    rel = jnp.where(d == 0, r - c, c - r)
    relf = jnp.maximum(rel, 0).astype(F32)
    tt = lax.broadcasted_iota(jnp.int32, (t, 1), 0)
    pos = jnp.where(d == 0, tt, t - 1 - tt).astype(F32)
    scale = HD ** -0.5
    for h in range(NH):
        sl = slice(h * HD, (h + 1) * HD)
        lg = lg_ref[d, h]
        q = q_ref[:, sl]
        k = k_ref[:, sl]
        v = v_ref[:, sl]
        intra = jnp.where(rel >= 0, jnp.exp(lg * relf), 0.0)
        scores = (_dot_nt(q, k) * scale) * intra
        s_old = s_ref[h]
        out = _dot(scores.astype(BF16), v) + jnp.exp(lg * (pos + 1.0)) * _dot(q, s_old.astype(BF16))
        kd = (k.astype(F32) * (scale * jnp.exp(lg * (t - 1.0 - pos)))).astype(BF16)
        s_ref[h] = jnp.exp(lg * t) * s_old + _dot_tn(kd, v)
        o_ref[:, sl] = out


def _hgrn_direction(fwd, lb_ref, q_ref, z_ref, v_ref, o_ref, st_ref):
    t = T_HGRN
    lb = lb_ref[...]
    z = z_ref[...].astype(F32)
    f = lb + (1.0 - lb) * _sigmoid(z)
    logf = jnp.log(jnp.maximum(f, F_TINY))
    kk = (1.0 - lb) * _sigmoid(-z)
    qq = _silu(q_ref[...].astype(F32))
    r2 = lax.broadcasted_iota(jnp.int32, (t, t), 0)
    c2 = lax.broadcasted_iota(jnp.int32, (t, t), 1)
    tri = jnp.where((c2 <= r2) if fwd else (c2 >= r2), 1.0, 0.0).astype(BF16)
    cum = _cumsum_rows(tri, logf)
    row = lax.broadcasted_iota(jnp.int32, (t, 1), 0)
    if fwd:
        e_ref = jnp.where(row == 0, 0.0, pltpu.roll(cum, 1, 0))
    else:
        e_ref = jnp.where(row == t - 1, 0.0, pltpu.roll(cum, t - 1, 0))
    f_ref = cum
    levels = []
    w = 1
    while w < t:
        q_l = (qq * jnp.exp(cum - e_ref)).astype(BF16)
        k_l = (kk * jnp.exp(f_ref - cum)).astype(BF16)
        late, early = (w, 0) if fwd else (0, w)
        key = ((r2 ^ c2) & ~(2 * w - 1)) | ((r2 & w) ^ late) | ((c2 & w) ^ early)
        levels.append((q_l, k_l, key == 0))
        upper = (row & w) != 0
        if fwd:
            e_ref = jnp.where(upper, pltpu.roll(e_ref, w, 0), e_ref)
            f_ref = jnp.where(upper, f_ref, pltpu.roll(f_ref, t - w, 0))
        else:
            e_ref = jnp.where(upper, e_ref, pltpu.roll(e_ref, t - w, 0))
            f_ref = jnp.where(upper, pltpu.roll(f_ref, w, 0), f_ref)
        w *= 2
    q_b = qq.astype(BF16)
    k_b = kk.astype(BF16)
    tail = cum[t - 1:t] if fwd else cum[0:1]
    q_state = (qq * jnp.exp(cum)).astype(BF16)
    k_state = (kk * jnp.exp(tail - cum)).astype(BF16)
    decay = jnp.exp(tail)
    for h in range(NH):
        sl = slice(h * HD, (h + 1) * HD)
        v_h = v_ref[:, sl]
        scores = jnp.where(r2 == c2, _dot_nt(q_b[:, sl], k_b[:, sl]), 0.0)
        for q_l, k_l, m in levels:
            scores = jnp.where(m, _dot_nt(q_l[:, sl], k_l[:, sl]), scores)
        st = st_ref[h]
        o_ref[:, sl] = _dot(scores.astype(BF16), v_h) + _dot_nt(q_state[:, sl], st.astype(BF16))
        st_ref[h] = st * decay[:, sl] + _dot_tn(v_h, k_state[:, sl])


def _hgrn_kernel(lb_ref, q_ref, z_ref, v_ref, o_ref, st_ref):
    d = pl.program_id(1)

    @pl.when(pl.program_id(2) == 0)
    def _():
        st_ref[...] = jnp.zeros_like(st_ref)

    @pl.when(d == 0)
    def _():
        _hgrn_direction(True, lb_ref, q_ref, z_ref, v_ref, o_ref, st_ref)

    @pl.when(d == 1)
    def _():
        _hgrn_direction(False, lb_ref, q_ref, z_ref, v_ref, o_ref, st_ref)


def _log_sigmoid(x):
    return jnp.minimum(x, 0.0) - jnp.log(1.0 + jnp.exp(-jnp.abs(x)))


def _mlstm_kernel(qk_ref, v_ref, g_ref, gb_ref, o_ref, c_ref, n_ref, m_ref):
    d = pl.program_id(1)
    t = T_ATTN

    @pl.when(pl.program_id(2) == 0)
    def _():
        c_ref[...] = jnp.zeros_like(c_ref)
        n_ref[...] = jnp.zeros_like(n_ref)
        m_ref[...] = jnp.zeros_like(m_ref)

    g = g_ref[...] + gb_ref[...]
    gt = g.T
    mask = _order_mask(d, t)
    tri = jnp.where(mask, 1.0, 0.0).astype(BF16)
    lf = _log_sigmoid(g)
    cum_c = _cumsum_rows(tri, lf)
    lt_hi, lt_mid, lt_lo = _split3(lf.T)
    cum_r = _dot_nt(lt_hi, tri) + (_dot_nt(lt_mid, tri) + _dot_nt(lt_lo, tri))
    scale = HD ** -0.5
    for h in range(NH):
        sl = slice(h * HD, (h + 1) * HD)
        li, lfw = h, NH + h
        i_col = jnp.where(d == 0, g[:, li:li + 1], g[:, 8 + li:8 + li + 1])
        i_row = jnp.where(d == 0, gt[li:li + 1], gt[8 + li:8 + li + 1])
        cc = jnp.where(d == 0, cum_c[:, lfw:lfw + 1], cum_c[:, 8 + lfw:8 + lfw + 1])
        cr = jnp.where(d == 0, cum_r[lfw:lfw + 1], cum_r[8 + lfw:8 + lfw + 1])
        total = jnp.where(d == 0, cr[:, t - 1:t], cr[:, 0:1])
        m_old = m_ref[h][:, 0:1]
        q = qk_ref[:, sl]
        k = qk_ref[:, BR + h * HD:BR + (h + 1) * HD]
        v = v_ref[:, sl]
        logw = jnp.where(mask, cc - cr + i_row, NEG_BIG)
        from_state = cc + m_old
        m_t = jnp.maximum(from_state, jnp.max(logw, axis=1, keepdims=True))
        w = jnp.exp(logw - m_t)
        w_state = jnp.exp(from_state - m_t)
        scores = (_dot_nt(q, k) * scale) * w
        c_old = c_ref[h]
        n_old = n_ref[h]
        num = _dot(scores.astype(BF16), v) + w_state * _dot(q, c_old.astype(BF16))
        den = (jnp.sum(scores, axis=1, keepdims=True)
               + w_state * jnp.sum(q.astype(F32) * n_old, axis=1, keepdims=True))
        o_ref[:, sl] = num / jnp.maximum(jnp.abs(den), jnp.exp(-m_t))
        m_new = jnp.maximum(total + m_old, jnp.max(total - cr + i_row, axis=1, keepdims=True))
        keep = jnp.exp(total + m_old - m_new)
        w_end = jnp.exp(total - cc + i_col - m_new)
        kw = k.astype(F32) * (scale * w_end)
        c_ref[h] = keep * c_old + _dot_tn(kw.astype(BF16), v)
        n_ref[h] = keep * n_old + jnp.sum(kw, axis=0, keepdims=True)
        m_ref[h] = jnp.broadcast_to(m_new, (1, HD))


def _scans(p, gates, qk, lg, lb, gate_b, layer, *, n_batch, seq, ctx_len):
    n = p.shape[0]
    state = pltpu.VMEM((NH, HD, HD), F32)

    def call(kernel, t, inputs, make_specs, scratch, name):
        n_ctx, n_lat = ctx_len // t, seq // t
        blk = functools.partial(_chunk_block, n_ctx=n_ctx, n_lat=n_lat, n_batch=n_batch)

        def rows(width, cblock=0):
            return pl.BlockSpec((t, width), lambda b, d, i: (blk(b, d, i), cblock))

        return pl.pallas_call(
            kernel,
            out_shape=jax.ShapeDtypeStruct((2, n, BR), F32),
            grid=(n_batch, 2, n_ctx + n_lat),
            in_specs=make_specs(rows, blk),
            out_specs=pl.BlockSpec((None, t, BR), lambda b, d, i: (d, blk(b, d, i), 0)),
            scratch_shapes=scratch,
            compiler_params=_cparams(("parallel", "arbitrary", "arbitrary")),
            name=name,
        )(*inputs)

    raw_b = call(_ret_kernel, T_ATTN, (lg, p, p, p),
                 lambda rows, blk: [pl.BlockSpec(memory_space=pltpu.SMEM),
                                    rows(BR, COL_RQ), rows(BR, COL_RK), rows(BR, COL_RV)],
                 [state], "retention_scan")
    raw_a = call(_hgrn_kernel, T_HGRN, (lb, p, p, p),
                 lambda rows, blk: [pl.BlockSpec((None, None, 1, BR), lambda b, d, i: (layer, d, 0, 0)),
                                    rows(BR, COL_HQ),
                                    pl.BlockSpec((T_HGRN, BR), lambda b, d, i: (blk(b, d, i), COL_HF + d)),
                                    rows(BR, COL_HV)],
                 [state], "hgrn2_scan")
    raw_d = call(_mlstm_kernel, T_ATTN, (qk, p, gates, gate_b),
                 lambda rows, blk: [rows(2 * BR), rows(BR, COL_MV), rows(GATE_PAD),
                                    pl.BlockSpec((None, 1, GATE_PAD), lambda b, d, i: (layer, 0, 0))],
                 [state, pltpu.VMEM((NH, 1, HD), F32), pltpu.VMEM((NH, 1, HD), F32)], "mlstm_scan")
    return raw_a, raw_b, raw_d


def _cmul(ar, ai, br, bi):
    return ar * br - ai * bi, ar * bi + ai * br


def _pow_table(lam_r, lam_i, expo):
    pr = jnp.ones(expo.shape, F32)
    pi = jnp.zeros(expo.shape, F32)
    ar = jnp.broadcast_to(lam_r, expo.shape)
    ai = jnp.broadcast_to(lam_i, expo.shape)
    for k in range(S5_T.bit_length() - 1):
        nr, ni = _cmul(pr, pi, ar, ai)
        bit = (expo & (1 << k)) != 0
        pr, pi = jnp.where(bit, nr, pr), jnp.where(bit, ni, pi)
        ar, ai = _cmul(ar, ai, ar, ai)
    return pr, pi


def _s5_kernel(u_ref, col_ref, row_ref, o_ref, m_ref, w_ref, ys_ref, z_ref, wr_ref, wi_ref, xr_ref, xi_ref, *,
               n_batch, n_ctx, n_lat):
    d = pl.program_id(1)
    t = S5_T
    col = col_ref[...]
    rowp = row_ref[...]
    lam_rc, lam_ic = col[:, 0:1], col[:, 1:2]
    bbr, bbi = rowp[0:S5_HG], rowp[S5_HG:2 * S5_HG]
    lam_rr, lam_ir = rowp[2 * S5_HG:2 * S5_HG + 1], rowp[2 * S5_HG + 1:2 * S5_HG + 2]
    lane = lax.broadcasted_iota(jnp.int32, (t, t), 1)
    sub = lax.broadcasted_iota(jnp.int32, (t, t), 0)

    pr, pi = _pow_table(lam_rc, lam_ic, jnp.where(d == 0, lane, t - 1 - lane))
    qr, qi = _cmul(pr, pi, lam_rc, lam_ic)
    for h in range(S5_HG):
        cr, ci = col[:, 2 + h:3 + h], col[:, 2 + S5_HG + h:3 + S5_HG + h]
        hs = slice(h * t, (h + 1) * t)
        z_ref[0:S5_PP, hs] = cr * pr - ci * pi
        z_ref[S5_PP:2 * S5_PP, hs] = -(cr * pi + ci * pr)
        ys_ref[0:S5_PP, hs] = (cr * qr - ci * qi).astype(BF16)
        ys_ref[S5_PP:2 * S5_PP, hs] = (-(cr * qi + ci * qr)).astype(BF16)
    b_hi, b_mid, _ = _split3(jnp.concatenate([bbr, bbi], axis=1))
    z_hi, z_mid, _ = _split3(z_ref[...])
    krow = _dot(b_hi, z_hi) + (_dot(b_hi, z_mid) + _dot(b_mid, z_hi))

    def build_m(shift, keep):
        for hp in range(S5_HG):
            for h in range(S5_HG):
                tile = jnp.broadcast_to(krow[hp:hp + 1, h * t:(h + 1) * t], (t, t))
                tile = pltpu.roll(tile, shift, 1, stride=1, stride_axis=0)
                m_ref[hp * t:(hp + 1) * t, h * t:(h + 1) * t] = jnp.where(keep, tile, 0.0).astype(BF16)

    @pl.when(d == 0)
    def _():
        build_m(0, lane >= sub)

    @pl.when(d == 1)
    def _():
        build_m(1, lane <= sub)

    tr, ti = _pow_table(lam_rr, lam_ir, jnp.where(d == 0, t - 1 - sub, sub))
    for hp in range(S5_HG):
        br, bi = bbr[hp:hp + 1], bbi[hp:hp + 1]
        w_ref[hp * t:(hp + 1) * t, 0:S5_PP] = (tr * br - ti * bi).astype(BF16)
        w_ref[hp * t:(hp + 1) * t, S5_PP:2 * S5_PP] = (tr * bi + ti * br).astype(BF16)

    u = jnp.concatenate([u_ref[hp] for hp in range(S5_HG)], axis=1)
    wv = _dot(u, w_ref[...])
    wr_ref[...] = wv[:, 0:S5_PP]
    wi_ref[...] = wv[:, S5_PP:2 * S5_PP]
    lr, li = lam_rr, lam_ir
    for _ in range(t.bit_length() - 1):
        lr, li = _cmul(lr, li, lr, li)

    def make_step(base, stride, count):
        def step(i, carry):
            xr, xi = carry
            j = jnp.where(d == 0, i, count - 1 - i)
            idx = pl.ds(base + j, n_batch, stride=stride)
            xr_ref[idx, :] = xr
            xi_ref[idx, :] = xi
            nr, ni = _cmul(xr, xi, lr, li)
            return nr + wr_ref[idx, :], ni + wi_ref[idx, :]
        return step

    zero = jnp.zeros((n_batch, S5_PP), F32)
    carry = lax.fori_loop(0, n_ctx, make_step(n_batch * n_lat, n_ctx, n_ctx), (zero, zero))
    lax.fori_loop(0, n_lat, make_step(0, n_lat, n_lat), carry)
    x_prev = jnp.concatenate([xr_ref[...], xi_ref[...]], axis=1).astype(BF16)
    y = _dot(u, m_ref[...]) + _dot(x_prev, ys_ref[...])

    @pl.when(d == 0)
    def _():
        for h in range(S5_HG):
            o_ref[h] = y[:, h * t:(h + 1) * t]

    @pl.when(d == 1)
    def _():
        for h in range(S5_HG):
            o_ref[h] += y[:, h * t:(h + 1) * t]


def _s5_params(a_re, a_im, log_dt, b_re, b_im, c_re, c_im):
    a_re, a_im = a_re.astype(F32), a_im.astype(F32)
    dt = jnp.exp(log_dt.astype(F32))[..., None]
    mag = jnp.exp(a_re * dt)
    lam_re, lam_im = mag * jnp.cos(a_im * dt), mag * jnp.sin(a_im * dt)
    den = a_re * a_re + a_im * a_im
    num_re, num_im = lam_re - 1.0, lam_im
    fr = (num_re * a_re + num_im * a_im) / den
    fi = (num_im * a_re - num_re * a_im) / den
    b_re, b_im = b_re.astype(F32), b_im.astype(F32)
    bb_re = fr[..., None] * b_re - fi[..., None] * b_im
    bb_im = fr[..., None] * b_im + fi[..., None] * b_re
    c_re_t = jnp.swapaxes(c_re.astype(F32), -1, -2)
    c_im_t = jnp.swapaxes(c_im.astype(F32), -1, -2)
    col = jnp.concatenate([lam_re[..., None], lam_im[..., None], c_re_t, c_im_t], axis=-1)
    col = jnp.pad(col, ((0, 0),) * 3 + ((0, S5_PP - S5_P), (0, 128 - col.shape[-1])))
    row = jnp.concatenate([jnp.swapaxes(bb_re, -1, -2), jnp.swapaxes(bb_im, -1, -2),
                           lam_re[..., None, :], lam_im[..., None, :]], axis=-2)
    row = jnp.pad(row, ((0, 0),) * 3 + ((0, S5_ROWS - row.shape[-2]), (0, S5_PP - S5_P)))
    return col, row


def _s5_scan(p, col, row, layer, *, n_batch, seq, ctx_len):
    t = S5_T
    n = p.shape[0]
    nc = n // t
    n_ctx, n_lat = ctx_len // t, seq // t
    ut = p[:, COL_SU * BR:(COL_SU + 1) * BR].T.reshape(S5_G, S5_HG, nc, t)
    kern = functools.partial(_s5_kernel, n_batch=n_batch, n_ctx=n_ctx, n_lat=n_lat)
    yt = pl.pallas_call(
        kern,
        out_shape=jax.ShapeDtypeStruct((S5_G, S5_HG, nc, t), F32),
        grid=(S5_G, 2),
        in_specs=[pl.BlockSpec((None, S5_HG, nc, t), lambda g, d: (g, 0, 0, 0)),
                  pl.BlockSpec((None, None, None, S5_PP, 128), lambda g, d: (layer, d, g, 0, 0)),
                  pl.BlockSpec((None, None, None, S5_ROWS, S5_PP), lambda g, d: (layer, d, g, 0, 0))],
        out_specs=pl.BlockSpec((None, S5_HG, nc, t), lambda g, d: (g, 0, 0, 0)),
        scratch_shapes=[pltpu.VMEM((S5_HG * t, S5_HG * t), BF16),
                        pltpu.VMEM((S5_HG * t, 2 * S5_PP), BF16),
                        pltpu.VMEM((2 * S5_PP, S5_HG * t), BF16),
                        pltpu.VMEM((2 * S5_PP, S5_HG * t), F32),
                        pltpu.VMEM((nc, S5_PP), F32), pltpu.VMEM((nc, S5_PP), F32),
                        pltpu.VMEM((nc, S5_PP), F32), pltpu.VMEM((nc, S5_PP), F32)],
        compiler_params=_cparams(("parallel", "arbitrary")),
        name="s5_scan",
    )(ut, col, row)
    return yt.reshape(BR, n).T


def _head_rms(x, g):
    parts = []
    for h in range(NH):
        xh = x[:, h * HD:(h + 1) * HD]
        parts.append(xh * lax.rsqrt(jnp.mean(jnp.square(xh), axis=-1, keepdims=True) + NORM_EPS))
    return jnp.concatenate(parts, axis=-1) * g


def _finish_kernel(ra_ref, rb_ref, rc_ref, rd_ref, ga_ref, gb_ref, u_ref, gd_ref, mg_ref, x_ref, mod_ref,
                   vec_ref, glu_w_ref, wb_ref, wo_ref, o_ref):
    vec = vec_ref[...]
    oa = _head_rms(ra_ref[0] + ra_ref[1], vec[0:1]) * _silu(ga_ref[...].astype(F32))
    ob = _head_rms(rb_ref[0] + rb_ref[1], vec[1:2]) * _silu(gb_ref[...].astype(F32))
    od = _head_rms(rd_ref[0] + rd_ref[1], vec[2:3]) * _silu(gd_ref[...].astype(F32))
    yc = rc_ref[...] + vec[3:4] * u_ref[...].astype(F32)
    yc = 0.5 * yc * (1.0 + lax.erf(yc * (2.0 ** -0.5)))
    oc = yc * _sigmoid(_dot(yc.astype(BF16), glu_w_ref[...]) + vec[4:5])
    y = None
    for j, o in enumerate((oa, ob, oc, od)):
        gate = _sigmoid(mg_ref[:, j * D_MODEL:(j + 1) * D_MODEL].astype(F32))
        term = gate * _dot(o.astype(BF16), wb_ref[j])
        y = term if y is None else y + term
    mix = _dot(y.astype(BF16), wo_ref[...])
    o_ref[...] = x_ref[...] + mod_ref[2:3, :] * mix


def _finish(raw_a, raw_b, raw_c, raw_d, p, x_all, mod, vec, glu_w, w_branch, w_out, layer, *, n_rows, n_batch,
            seq, tm=256):
    mrow = _mod_row_map(n_rows, tm, n_batch, seq)
    one = pl.Buffered(1)

    def raw(i):
        return (0, i, 0)

    def pcol(cblock):
        return pl.BlockSpec((tm, BR), lambda i: (i, cblock))

    return pl.pallas_call(
        _finish_kernel,
        out_shape=jax.ShapeDtypeStruct((n_rows, D_MODEL), F32),
        grid=(n_rows // tm,),
        in_specs=[pl.BlockSpec((2, tm, BR), raw),
                  pl.BlockSpec((2, tm, BR), raw),
                  pl.BlockSpec((tm, BR), lambda i: (i, 0)),
                  pl.BlockSpec((2, tm, BR), raw),
                  pcol(COL_HG), pcol(COL_RG), pcol(COL_SU), pcol(COL_MZ),
                  pl.BlockSpec((tm, N_BRANCH * D_MODEL), lambda i: (i, COL_MERGE)),
                  pl.BlockSpec((tm, D_MODEL), lambda i: (i, 0)),
                  pl.BlockSpec((None, 8, D_MODEL), lambda i: (mrow(i), 0, 0)),
                  pl.BlockSpec((None, 8, BR), lambda i: (layer, 0, 0), pipeline_mode=one),
                  pl.BlockSpec((None, BR, BR), lambda i: (layer, 0, 0), pipeline_mode=one),
                  pl.BlockSpec((None, N_BRANCH, BR, D_MODEL), lambda i: (layer, 0, 0, 0), pipeline_mode=one),
                  pl.BlockSpec((None, D_MODEL, D_MODEL), lambda i: (layer, 0, 0), pipeline_mode=one)],
        out_specs=pl.BlockSpec((tm, D_MODEL), lambda i: (i, 0)),
        compiler_params=_cparams(("parallel",)),
        name="finish",
    )(raw_a, raw_b, raw_c, raw_d, p, p, p, p, p, x_all, mod, vec, glu_w, w_branch, w_out)


def _mlp_kernel(x_ref, mod_ref, g_ref, w1_ref, w2_ref, fin_ref, o_ref, xn_ref, acc_ref, *, final_norm):
    j = pl.program_id(1)

    @pl.when(j == 0)
    def _():
        h = _norm_mod(x_ref[...], g_ref[...], mod_ref[3:4, :], mod_ref[4:5, :])
        xn_ref[...] = h.astype(BF16)
        acc_ref[...] = jnp.zeros_like(acc_ref)

    a = jnp.square(jnp.maximum(_dot(xn_ref[...], w1_ref[...]), 0.0))
    acc_ref[...] += _dot(a.astype(BF16), w2_ref[...])

    @pl.when(j == pl.num_programs(1) - 1)
    def _():
        y = x_ref[...] + mod_ref[5:6, :] * acc_ref[...]
        if final_norm:
            y = (y * lax.rsqrt(jnp.mean(jnp.square(y), axis=-1, keepdims=True) + NORM_EPS)) * fin_ref[...]
        o_ref[...] = y


def _mlp(x_all, mod, norm_w, w1, w2, fin_w, layer, *, n_rows, n_batch, seq, final_norm, tm=512, tf=1024):
    mrow = _mod_row_map(n_rows, tm, n_batch, seq)
    kern = functools.partial(_mlp_kernel, final_norm=final_norm)
    return pl.pallas_call(
        kern,
        out_shape=jax.ShapeDtypeStruct((n_rows, D_MODEL), F32),
        grid=(n_rows // tm, D_FF // tf),
        in_specs=[pl.BlockSpec((tm, D_MODEL), lambda i, j: (i, 0)),
                  pl.BlockSpec((None, 8, D_MODEL), lambda i, j: (mrow(i), 0, 0)),
                  pl.BlockSpec((None, 1, D_MODEL), lambda i, j: (layer, 0, 0)),
                  pl.BlockSpec((None, D_MODEL, tf), lambda i, j: (layer, 0, j)),
                  pl.BlockSpec((None, tf, D_MODEL), lambda i, j: (layer, j, 0)),
                  pl.BlockSpec((1, D_MODEL), lambda i, j: (0, 0))],
        out_specs=pl.BlockSpec((tm, D_MODEL), lambda i, j: (i, 0)),
        scratch_shapes=[pltpu.VMEM((tm, D_MODEL), BF16), pltpu.VMEM((tm, D_MODEL), F32)],
        compiler_params=_cparams(("parallel", "arbitrary")),
        name="mlp",
    )(x_all, mod, norm_w, w1, w2, fin_w.reshape(1, D_MODEL))


W_PREP_ROWS = 1024
N_MERGE_BLK = N_BRANCH * D_MODEL // W_PREP_ROWS
MERGE_OFF = GATE_OFF + 4 * NH


def _w_prep_kernel(a_ref, g_ref, main_ref, gate_ref):
    @pl.when(pl.program_id(1) == 0)
    def _():
        gate_ref[...] = jnp.zeros_like(gate_ref)
        gate_ref[0:4 * NH, :] = g_ref[0].astype(BF16)

    main_ref[...] = a_ref[0].astype(BF16)


def _split_w_in(w):
    depth = w.shape[0]
    wt = jnp.swapaxes(w, 1, 2)
    tr = W_PREP_ROWS

    def a_map(l, j):
        row = jnp.where(j < N_MERGE_BLK, MERGE_OFF + j * tr, (j - N_MERGE_BLK) * tr)
        return l, pl.multiple_of(row, 16), 0

    return pl.pallas_call(
        _w_prep_kernel,
        out_shape=(jax.ShapeDtypeStruct((depth, P_WIDTH, D_MODEL), BF16),
                   jax.ShapeDtypeStruct((depth, GATE_PAD, D_MODEL), BF16)),
        grid=(depth, P_WIDTH // tr),
        in_specs=[pl.BlockSpec((pl.Element(1), pl.Element(tr), pl.Element(D_MODEL)), a_map),
                  pl.BlockSpec((pl.Element(1), pl.Element(4 * NH), pl.Element(D_MODEL)),
                               lambda l, j: (l, GATE_OFF, 0))],
        out_specs=(pl.BlockSpec((None, tr, D_MODEL), lambda l, j: (l, j, 0)),
                   pl.BlockSpec((None, GATE_PAD, D_MODEL), lambda l, j: (l, 0, 0))),
        compiler_params=_cparams(("parallel", "arbitrary")),
        name="w_in_prep",
    )(wt, wt)


def kernel(x, c, ctx, c_ctx, w_mod, b_mod, norm_mix, norm_mlp, w_in, hgrn_lb_logits, hgrn_norm, ret_decay, ret_norm, s5_a_re, s5_a_im, s5_log_dt, s5_b_re, s5_b_im, s5_c_re, s5_c_im, s5_d, s5_glu_w, s5_glu_b, mlstm_conv_w, mlstm_conv_b, mlstm_gate_b, mlstm_norm, w_branch, w_out, w_ff1, w_ff2, final_norm):
    n_batch, seq, _ = x.shape
    ctx_len = ctx.shape[1]
    depth = w_in.shape[0]
    nl = n_batch * seq
    dims = dict(n_batch=n_batch, seq=seq)

    p_lb = jax.nn.softmax(hgrn_lb_logits.astype(F32), axis=0)
    lower_bounds = (jnp.cumsum(p_lb, axis=0) - p_lb[0]).reshape(depth, 2, 1, BR)
    lg = jnp.log1p(-jnp.exp(ret_decay.astype(F32)))
    gate_b = jnp.pad(mlstm_gate_b.reshape(depth, 1, 4 * NH).astype(F32), ((0, 0), (0, 0), (0, GATE_PAD - 4 * NH)))
    vec = jnp.stack([hgrn_norm, ret_norm, mlstm_norm, s5_d, s5_glu_b], axis=1).astype(F32)
    vec = jnp.pad(vec, ((0, 0), (0, 8 - vec.shape[1]), (0, 0)))
    w_main, w_gate = _split_w_in(w_in)
    glu_w, wb, wo = s5_glu_w.astype(BF16), w_branch.astype(BF16), w_out.astype(BF16)
    w1, w2 = w_ff1.astype(BF16), w_ff2.astype(BF16)
    norm_mix3 = norm_mix.reshape(depth, 1, D_MODEL)
    norm_mlp3 = norm_mlp.reshape(depth, 1, D_MODEL)
    conv_w = mlstm_conv_w.reshape(depth, 9, 2 * BR)
    conv_b = mlstm_conv_b.reshape(depth, 1, 2 * BR)
    s5_col, s5_row = _s5_params(s5_a_re, s5_a_im, s5_log_dt, s5_b_re, s5_b_im, s5_c_re, s5_c_im)

    c_pad = jnp.zeros((8, D_MODEL), F32).at[:n_batch].set(c).at[n_batch].set(c_ctx)
    x_all = jnp.concatenate([x.reshape(nl, D_MODEL), ctx.reshape(n_batch * ctx_len, D_MODEL)], axis=0)

    for l in range(depth):
        last = l == depth - 1
        mod = _modulation(c_pad, w_mod, b_mod, l)
        mod = jnp.pad(mod[:n_batch + 1].reshape(n_batch + 1, N_MOD, D_MODEL), ((0, 0), (0, 8 - N_MOD), (0, 0)))
        p, gates = _in_proj(x_all, mod, norm_mix3, w_main, w_gate, l, **dims)
        qk = _mlstm_conv(p, conv_w, conv_b, l, ctx_len=ctx_len, **dims)
        raw_a, raw_b, raw_d = _scans(p, gates, qk, lg[l], lower_bounds, gate_b, l, ctx_len=ctx_len, **dims)
        raw_c = _s5_scan(p, s5_col, s5_row, l, ctx_len=ctx_len, **dims)
        n_rows = nl if last else x_all.shape[0]
        x_mid = _finish(raw_a, raw_b, raw_c, raw_d, p, x_all, mod, vec, glu_w, wb, wo, l, n_rows=n_rows, **dims)
        x_all = _mlp(x_mid, mod, norm_mlp3, w1, w2, final_norm, l, n_rows=n_rows, final_norm=last, **dims)
    return x_all.reshape(n_batch, seq, D_MODEL)
```

```python
import functools
import math

import jax
import jax.numpy as jnp
from jax import lax
from jax.experimental import pallas as pl
from jax.experimental.pallas import tpu as pltpu

F32 = jnp.float32
BF16 = jnp.bfloat16

D_MODEL = 2048
N_BRANCH = 4
BR = D_MODEL // N_BRANCH
HD = 128
NH = BR // HD
S5_HG = 16
S5_G = BR // S5_HG
S5_P = 64
D_FF = 4 * D_MODEL
N_MOD = 6
GRID_W = 64
NORM_EPS = 1e-6
NEG_BIG = -1e30
F_TINY = 1e-30
S5_DT_MIN = 1e-3

COL_MERGE = 0
COL_HQ, COL_HF, COL_HV, COL_HG = 16, 17, 19, 20
COL_RQ, COL_RK, COL_RV, COL_RG = 21, 22, 23, 24
COL_SU = 25
COL_MQK, COL_MV, COL_MZ = 26, 28, 29
P_WIDTH = 30 * BR
GATE_OFF = 14 * BR
GATE_PAD = 128

T_HGRN = 128
T_ATTN = 256
S5_T = 128
S5_PP = 128
S5_ROWS = 40
VMEM_LIMIT = 56 * 1024 * 1024

_HI = lax.Precision.HIGHEST


def _cparams(sem):
    return pltpu.CompilerParams(dimension_semantics=sem, vmem_limit_bytes=VMEM_LIMIT)


def _dot(a, b):
    return jnp.dot(a, b, preferred_element_type=F32)


def _dot_nt(a, b):
    return lax.dot_general(a, b, (((1,), (1,)), ((), ())), preferred_element_type=F32)


def _dot_tn(a, b):
    return lax.dot_general(a, b, (((0,), (0,)), ((), ())), preferred_element_type=F32)


def _split3(x):
    hi = x.astype(BF16)
    r1 = x - hi.astype(F32)
    mid = r1.astype(BF16)
    return hi, mid, (r1 - mid.astype(F32)).astype(BF16)


def _cumsum_rows(tri, x):
    hi, mid, lo = _split3(x)
    return _dot(tri, hi) + (_dot(tri, mid) + _dot(tri, lo))


def _sigmoid(x):
    return 1.0 / (1.0 + jnp.exp(-x))


def _sigmoid_tanh(x):
    return 0.5 * jnp.tanh(0.5 * x) + 0.5


def _silu(x):
    return x * _sigmoid(x)


def _mod_kernel(c_ref, w_ref, b_ref, o_ref):
    @pl.when(pl.program_id(0) == 0)
    def _():
        o_ref[...] = jnp.broadcast_to(b_ref[...], o_ref.shape)

    a = _silu(c_ref[...]).astype(BF16)
    o_ref[...] += _dot(a, w_ref[...].astype(BF16))


def _modulation(c_pad, w_mod, b_mod, layer, tk=256):
    n = w_mod.shape[2]
    return pl.pallas_call(
        _mod_kernel,
        out_shape=jax.ShapeDtypeStruct((8, n), F32),
        grid=(D_MODEL // tk,),
        in_specs=[pl.BlockSpec((8, tk), lambda k: (0, k)),
                  pl.BlockSpec((None, tk, n), lambda k: (layer, k, 0)),
                  pl.BlockSpec((None, 1, n), lambda k: (layer, 0, 0))],
        out_specs=pl.BlockSpec((8, n), lambda k: (0, 0)),
        compiler_params=_cparams(("arbitrary",)),
        name="modulation",
    )(c_pad, w_mod, b_mod.reshape(b_mod.shape[0], 1, n))


def _norm_mod(x, g, shift, scale):
    y = x * lax.rsqrt(jnp.mean(jnp.square(x), axis=-1, keepdims=True) + NORM_EPS)
    return (y * g) * (1.0 + scale) + shift


def _inproj_kernel(x_ref, mod_ref, g_ref, w_ref, wg_ref, p_ref, gate_ref, xn_ref):
    @pl.when(pl.program_id(1) == 0)
    def _():
        h = _norm_mod(x_ref[...], g_ref[...], mod_ref[0:1, :], mod_ref[1:2, :])
        xn_ref[...] = h.astype(BF16)
        gate_ref[...] = _dot_nt(xn_ref[...], wg_ref[...])

    p_ref[...] = _dot_nt(xn_ref[...], w_ref[...]).astype(p_ref.dtype)


def _mod_row_map(n_rows, tm, n_batch, seq):
    assert n_rows % tm == 0 and seq % tm == 0, (n_rows, seq, tm)
    n_lat_tiles, tiles_per_batch = n_batch * seq // tm, seq // tm

    def row(i):
        return jnp.where(i < n_lat_tiles, i // tiles_per_batch, n_batch)
    return row


def _in_proj(x_all, mod, norm_w, w_main, w_gate, layer, *, n_batch, seq, tm=1024, tn=1024):
    n = x_all.shape[0]
    mrow = _mod_row_map(n, tm, n_batch, seq)
    return pl.pallas_call(
        _inproj_kernel,
        out_shape=(jax.ShapeDtypeStruct((n, P_WIDTH), BF16),
                   jax.ShapeDtypeStruct((n, GATE_PAD), F32)),
        grid=(n // tm, P_WIDTH // tn),
        in_specs=[pl.BlockSpec((tm, D_MODEL), lambda i, j: (i, 0)),
                  pl.BlockSpec((None, 8, D_MODEL), lambda i, j: (mrow(i), 0, 0)),
                  pl.BlockSpec((None, 1, D_MODEL), lambda i, j: (layer, 0, 0)),
                  pl.BlockSpec((None, tn, D_MODEL), lambda i, j: (layer, j, 0)),
                  pl.BlockSpec((None, GATE_PAD, D_MODEL), lambda i, j: (layer, 0, 0))],
        out_specs=(pl.BlockSpec((tm, tn), lambda i, j: (i, j)),
                   pl.BlockSpec((tm, GATE_PAD), lambda i, j: (i, 0))),
        scratch_shapes=[pltpu.VMEM((tm, D_MODEL), BF16)],
        compiler_params=_cparams(("parallel", "arbitrary")),
        name="in_proj",
    )(x_all, mod, norm_w, w_main, w_gate)


CONV_ROWS = 256


def _conv_kernel(up_ref, x_ref, dn_ref, w_ref, b_ref, o_ref, *, n_lat_blocks, blocks_per_img):
    rb = pl.program_id(0)
    x = x_ref[...].astype(F32)
    w = w_ref[...]
    n = CONV_ROWS

    @pl.when(rb < n_lat_blocks)
    def _():
        rr = rb % blocks_per_img
        up = jnp.where(rr > 0, up_ref[...].astype(F32), 0.0)
        dn = jnp.where(rr < blocks_per_img - 1, dn_ref[...].astype(F32), 0.0)
        ext = jnp.concatenate([up, x, dn], axis=0)
        ne = n + 2 * GRID_W
        col = lax.broadcasted_iota(jnp.int32, (ne, 1), 0) % GRID_W
        xl = jnp.where(col >= 1, pltpu.roll(ext, 1, 0), 0.0)
        xr = jnp.where(col <= GRID_W - 2, pltpu.roll(ext, ne - 1, 0), 0.0)
        acc = None
        for i in range(3):
            y = w[3 * i:3 * i + 1] * xl + w[3 * i + 1:3 * i + 2] * ext + w[3 * i + 2:3 * i + 3] * xr
            part = y[i * GRID_W:i * GRID_W + n]
            acc = part if acc is None else acc + part
        o_ref[...] = _silu(acc + b_ref[...]).astype(o_ref.dtype)

    @pl.when(rb >= n_lat_blocks)
    def _():
        t = lax.broadcasted_iota(jnp.int32, (n, 1), 0)
        xl = jnp.where(t >= 1, pltpu.roll(x, 1, 0), 0.0)
        xr = jnp.where(t <= n - 2, pltpu.roll(x, n - 1, 0), 0.0)
        acc = w[3:4] * xl + w[4:5] * x + w[5:6] * xr
        o_ref[...] = _silu(acc + b_ref[...]).astype(o_ref.dtype)


def _mlstm_conv(p, conv_w, conv_b, layer, *, n_batch, seq, ctx_len, cb=2 * BR):
    assert ctx_len == CONV_ROWS and seq % CONV_ROWS == 0
    n = p.shape[0]
    n_blocks = n // CONV_ROWS
    hb = CONV_ROWS // GRID_W
    n_halo = n // GRID_W
    c0 = COL_MQK * BR // cb
    kern = functools.partial(_conv_kernel, n_lat_blocks=n_batch * seq // CONV_ROWS,
                             blocks_per_img=seq // CONV_ROWS)
    return pl.pallas_call(
        kern,
        out_shape=jax.ShapeDtypeStruct((n, 2 * BR), BF16),
        grid=(n_blocks, 2 * BR // cb),
        in_specs=[pl.BlockSpec((GRID_W, cb), lambda r, c: (jnp.maximum(r * hb - 1, 0), c0 + c)),
                  pl.BlockSpec((CONV_ROWS, cb), lambda r, c: (r, c0 + c)),
                  pl.BlockSpec((GRID_W, cb), lambda r, c: (jnp.minimum((r + 1) * hb, n_halo - 1), c0 + c)),
                  pl.BlockSpec((None, 9, cb), lambda r, c: (layer, 0, c)),
                  pl.BlockSpec((None, 1, cb), lambda r, c: (layer, 0, c))],
        out_specs=pl.BlockSpec((CONV_ROWS, cb), lambda r, c: (r, c)),
        compiler_params=_cparams(("parallel", "parallel")),
        name="mlstm_conv",
    )(p, p, p, conv_w, conv_b)


def _chunk_block(b, d, i, *, n_ctx, n_lat, n_batch):
    ctx_j = i if d == 0 else n_ctx - 1 - i
    lat_j = i - n_ctx if d == 0 else n_ctx + n_lat - 1 - i
    return jnp.where(i < n_ctx, n_batch * n_lat + b * n_ctx + ctx_j, b * n_lat + lat_j)


def _order_mask(d, t):
    r = lax.broadcasted_iota(jnp.int32, (t, t), 0)
    c = lax.broadcasted_iota(jnp.int32, (t, t), 1)
    return (c <= r) if d == 0 else (c >= r)


def _init_at_first_step(*refs):
    @pl.when(pl.program_id(1) == 0)
    def _():
        for ref in refs:
            ref[...] = jnp.zeros_like(ref)


def _ret_direction(d, lg_ref, q_ref, k_ref, v_ref, o_ref, s_ref):
    t = T_ATTN
    r = lax.broadcasted_iota(jnp.int32, (t, t), 0)
    c = lax.broadcasted_iota(jnp.int32, (t, t), 1)
    rel = (r - c) if d == 0 else (c - r)
    relf = jnp.maximum(rel, 0).astype(F32)
    tt = lax.broadcasted_iota(jnp.int32, (t, 1), 0)
    pos = (tt if d == 0 else t - 1 - tt).astype(F32)
    scale = HD ** -0.5
    for h in range(NH):
        sl = slice(h * HD, (h + 1) * HD)
        lg = lg_ref[d, h]
        q = q_ref[:, sl]
        k = k_ref[:, sl]
        v = v_ref[:, sl]
        intra = jnp.where(rel >= 0, jnp.exp(lg * relf), 0.0)
        scores = (_dot_nt(q, k) * scale) * intra
        s_old = s_ref[d, h]
        out = _dot(scores.astype(BF16), v) + jnp.exp(lg * (pos + 1.0)) * _dot(q, s_old.astype(BF16))
        kd = (k.astype(F32) * (scale * jnp.exp(lg * (t - 1.0 - pos)))).astype(BF16)
        s_ref[d, h] = jnp.exp(lg * t) * s_old + _dot_tn(kd, v)
        o_ref[:, sl] = out


def _ret_kernel(lg_ref, qf_ref, kf_ref, vf_ref, qb_ref, kb_ref, vb_ref, of_ref, ob_ref, s_ref):
    _init_at_first_step(s_ref)
    _ret_direction(0, lg_ref, qf_ref, kf_ref, vf_ref, of_ref, s_ref)
    _ret_direction(1, lg_ref, qb_ref, kb_ref, vb_ref, ob_ref, s_ref)


def _hgrn_direction(d, lb_ref, q_ref, z_ref, v_ref, o_ref, st_ref):
    t = T_HGRN
    fwd = d == 0
    lb = lb_ref[...]
    z = z_ref[...].astype(F32)
    f = lb + (1.0 - lb) * _sigmoid(z)
    logf = jnp.log(jnp.maximum(f, F_TINY))
    kk = (1.0 - lb) * _sigmoid(-z)
    qq = _silu(q_ref[...].astype(F32))
    r2 = lax.broadcasted_iota(jnp.int32, (t, t), 0)
    c2 = lax.broadcasted_iota(jnp.int32, (t, t), 1)
    tri = jnp.where((c2 <= r2) if fwd else (c2 >= r2), 1.0, 0.0).astype(BF16)
    cum = _cumsum_rows(tri, logf)
    row = lax.broadcasted_iota(jnp.int32, (t, 1), 0)
    if fwd:
        e_ref = jnp.where(row == 0, 0.0, pltpu.roll(cum, 1, 0))
    else:
        e_ref = jnp.where(row == t - 1, 0.0, pltpu.roll(cum, t - 1, 0))
    f_ref = cum
    levels = []
    w = 1
    while w < t:
        q_l = (qq * jnp.exp(cum - e_ref)).astype(BF16)
        k_l = (kk * jnp.exp(f_ref - cum)).astype(BF16)
        late, early = (w, 0) if fwd else (0, w)
        key = ((r2 ^ c2) & ~(2 * w - 1)) | ((r2 & w) ^ late) | ((c2 & w) ^ early)
        levels.append((q_l, k_l, key == 0))
        upper = (row & w) != 0
        if fwd:
            e_ref = jnp.where(upper, pltpu.roll(e_ref, w, 0), e_ref)
            f_ref = jnp.where(upper, f_ref, pltpu.roll(f_ref, t - w, 0))
        else:
            e_ref = jnp.where(upper, e_ref, pltpu.roll(e_ref, t - w, 0))
            f_ref = jnp.where(upper, pltpu.roll(f_ref, w, 0), f_ref)
        w *= 2
    q_b = qq.astype(BF16)
    k_b = kk.astype(BF16)
    tail = cum[t - 1:t] if fwd else cum[0:1]
    q_state = (qq * jnp.exp(cum)).astype(BF16)
    k_state = (kk * jnp.exp(tail - cum)).astype(BF16)
    decay = jnp.exp(tail)
    for h in range(NH):
        sl = slice(h * HD, (h + 1) * HD)
        v_h = v_ref[:, sl]
        scores = jnp.where(r2 == c2, _dot_nt(q_b[:, sl], k_b[:, sl]), 0.0)
        for q_l, k_l, m in levels:
            scores = jnp.where(m, _dot_nt(q_l[:, sl], k_l[:, sl]), scores)
        st = st_ref[d, h]
        o_ref[:, sl] = _dot(scores.astype(BF16), v_h) + _dot_nt(q_state[:, sl], st.astype(BF16))
        st_ref[d, h] = st * decay[:, sl] + _dot_tn(v_h, k_state[:, sl])


def _hgrn_kernel(lbf_ref, qf_ref, zf_ref, vf_ref, lbb_ref, qb_ref, zb_ref, vb_ref, of_ref, ob_ref, st_ref):
    _init_at_first_step(st_ref)
    _hgrn_direction(0, lbf_ref, qf_ref, zf_ref, vf_ref, of_ref, st_ref)
    _hgrn_direction(1, lbb_ref, qb_ref, zb_ref, vb_ref, ob_ref, st_ref)


def _log_sigmoid(x):
    return jnp.minimum(x, 0.0) - jnp.log(1.0 + jnp.exp(-jnp.abs(x)))


def _mlstm_direction(d, qk_ref, v_ref, g_ref, gb_ref, o_ref, c_ref, n_ref, m_ref):
    t = T_ATTN
    g = g_ref[...] + gb_ref[...]
    gt = g.T
    mask = _order_mask(d, t)
    tri = jnp.where(mask, 1.0, 0.0).astype(BF16)
    lf = _log_sigmoid(g)
    cum_c = _cumsum_rows(tri, lf)
    lt_hi, lt_mid, lt_lo = _split3(lf.T)
    cum_r = _dot_nt(lt_hi, tri) + (_dot_nt(lt_mid, tri) + _dot_nt(lt_lo, tri))
    scale = HD ** -0.5
    last = t - 1 if d == 0 else 0
    for h in range(NH):
        sl = slice(h * HD, (h + 1) * HD)
        li, lfw = 2 * NH * d + h, 2 * NH * d + NH + h
        i_col = g[:, li:li + 1]
        i_row = gt[li:li + 1]
        cc = cum_c[:, lfw:lfw + 1]
        cr = cum_r[lfw:lfw + 1]
        total = cr[:, last:last + 1]
        m_old = m_ref[d, h][:, 0:1]
        q = qk_ref[:, sl]
        k = qk_ref[:, BR + h * HD:BR + (h + 1) * HD]
        v = v_ref[:, sl]
        logw = jnp.where(mask, cc - cr + i_row, NEG_BIG)
        from_state = cc + m_old
        m_t = jnp.maximum(from_state, jnp.max(logw, axis=1, keepdims=True))
        w = jnp.exp(logw - m_t)
        w_state = jnp.exp(from_state - m_t)
        scores = (_dot_nt(q, k) * scale) * w
        c_old = c_ref[d, h]
        n_old = n_ref[d, h]
        num = _dot(scores.astype(BF16), v) + w_state * _dot(q, c_old.astype(BF16))
        den = (jnp.sum(scores, axis=1, keepdims=True)
               + w_state * jnp.sum(q.astype(F32) * n_old, axis=1, keepdims=True))
        o_ref[:, sl] = num / jnp.maximum(jnp.abs(den), jnp.exp(-m_t))
        m_new = jnp.maximum(total + m_old, jnp.max(total - cr + i_row, axis=1, keepdims=True))
        keep = jnp.exp(total + m_old - m_new)
        w_end = jnp.exp(total - cc + i_col - m_new)
        kw = k.astype(F32) * (scale * w_end)
        c_ref[d, h] = keep * c_old + _dot_tn(kw.astype(BF16), v)
        n_ref[d, h] = keep * n_old + jnp.sum(kw, axis=0, keepdims=True)
        m_ref[d, h] = jnp.broadcast_to(m_new, (1, HD))


def _mlstm_kernel(qkf_ref, vf_ref, gf_ref, qkb_ref, vb_ref, gb_ref, bias_ref, of_ref, ob_ref, c_ref, n_ref, m_ref):
    _init_at_first_step(c_ref, n_ref, m_ref)
    _mlstm_direction(0, qkf_ref, vf_ref, gf_ref, bias_ref, of_ref, c_ref, n_ref, m_ref)
    _mlstm_direction(1, qkb_ref, vb_ref, gb_ref, bias_ref, ob_ref, c_ref, n_ref, m_ref)


def _scans(p, gates, qk, lg, lb, gate_b, layer, *, n_batch, seq, ctx_len):
    n = p.shape[0]
    state = pltpu.VMEM((2, NH, HD, HD), F32)
    vec = pltpu.VMEM((2, NH, 1, HD), F32)

    def call(kernel, t, inputs, make_specs, scratch, name):
        n_ctx, n_lat = ctx_len // t, seq // t

        def rows(d, width, cblock=0):
            return pl.BlockSpec((t, width), lambda b, i: (_chunk_block(b, d, i, n_ctx=n_ctx, n_lat=n_lat,
                                                                        n_batch=n_batch), cblock))

        out = jax.ShapeDtypeStruct((n, BR), F32)
        return pl.pallas_call(
            kernel,
            out_shape=(out, out),
            grid=(n_batch, n_ctx + n_lat),
            in_specs=make_specs(rows),
            out_specs=(rows(0, BR), rows(1, BR)),
            scratch_shapes=scratch,
            compiler_params=_cparams(("parallel", "arbitrary")),
            name=name,
        )(*inputs)

    raw_b = call(_ret_kernel, T_ATTN, (lg, p, p, p, p, p, p),
                 lambda rows: [pl.BlockSpec(memory_space=pltpu.SMEM)]
                 + [rows(d, BR, c) for d in (0, 1) for c in (COL_RQ, COL_RK, COL_RV)],
                 [state], "retention_scan")
    lb_spec = lambda d: pl.BlockSpec((None, None, 1, BR), lambda b, i: (layer, d, 0, 0))
    raw_a = call(_hgrn_kernel, T_HGRN, (lb, p, p, p, lb, p, p, p),
                 lambda rows: [spec for d in (0, 1)
                               for spec in (lb_spec(d), rows(d, BR, COL_HQ), rows(d, BR, COL_HF + d),
                                            rows(d, BR, COL_HV))],
                 [state], "hgrn2_scan")
    raw_d = call(_mlstm_kernel, T_ATTN, (qk, p, gates, qk, p, gates, gate_b),
                 lambda rows: [spec for d in (0, 1)
                               for spec in (rows(d, 2 * BR), rows(d, BR, COL_MV), rows(d, GATE_PAD))]
                 + [pl.BlockSpec((None, 1, GATE_PAD), lambda b, i: (layer, 0, 0))],
                 [state, vec, vec], "mlstm_scan")
    return raw_a, raw_b, raw_d


def _cmul(ar, ai, br, bi):
    return ar * br - ai * bi, ar * bi + ai * br


def _pow_table(lam_r, lam_i, expo):
    pr = jnp.ones(expo.shape, F32)
    pi = jnp.zeros(expo.shape, F32)
    ar = jnp.broadcast_to(lam_r, expo.shape)
    ai = jnp.broadcast_to(lam_i, expo.shape)
    for k in range(S5_T.bit_length() - 1):
        nr, ni = _cmul(pr, pi, ar, ai)
        bit = (expo & (1 << k)) != 0
        pr, pi = jnp.where(bit, nr, pr), jnp.where(bit, ni, pi)
        ar, ai = _cmul(ar, ai, ar, ai)
    return pr, pi


def _s5_kernel(u_ref, col_ref, row_ref, o_ref, m_ref, w_ref, ys_ref, z_ref, wr_ref, wi_ref, xr_ref, xi_ref, *,
               n_batch, n_ctx, n_lat):
    d = pl.program_id(1)
    t = S5_T
    col = col_ref[...]
    rowp = row_ref[...]
    lam_rc, lam_ic = col[:, 0:1], col[:, 1:2]
    bbr, bbi = rowp[0:S5_HG], rowp[S5_HG:2 * S5_HG]
    lam_rr, lam_ir = rowp[2 * S5_HG:2 * S5_HG + 1], rowp[2 * S5_HG + 1:2 * S5_HG + 2]
    lane = lax.broadcasted_iota(jnp.int32, (t, t), 1)
    sub = lax.broadcasted_iota(jnp.int32, (t, t), 0)

    pr, pi = _pow_table(lam_rc, lam_ic, jnp.where(d == 0, lane, t - 1 - lane))
    qr, qi = _cmul(pr, pi, lam_rc, lam_ic)
    for h in range(S5_HG):
        cr, ci = col[:, 2 + h:3 + h], col[:, 2 + S5_HG + h:3 + S5_HG + h]
        hs = slice(h * t, (h + 1) * t)
        z_ref[0:S5_PP, hs] = cr * pr - ci * pi
        z_ref[S5_PP:2 * S5_PP, hs] = -(cr * pi + ci * pr)
        ys_ref[0:S5_PP, hs] = (cr * qr - ci * qi).astype(BF16)
        ys_ref[S5_PP:2 * S5_PP, hs] = (-(cr * qi + ci * qr)).astype(BF16)
    b_hi, b_mid, _ = _split3(jnp.concatenate([bbr, bbi], axis=1))
    z_hi, z_mid, _ = _split3(z_ref[...])
    krow = _dot(b_hi, z_hi) + (_dot(b_hi, z_mid) + _dot(b_mid, z_hi))

    def build_m(shift, keep):
        for hp in range(S5_HG):
            for h in range(S5_HG):
                tile = jnp.broadcast_to(krow[hp:hp + 1, h * t:(h + 1) * t], (t, t))
                tile = pltpu.roll(tile, shift, 1, stride=1, stride_axis=0)
                m_ref[hp * t:(hp + 1) * t, h * t:(h + 1) * t] = jnp.where(keep, tile, 0.0).astype(BF16)

    @pl.when(d == 0)
    def _():
        build_m(0, lane >= sub)

    @pl.when(d == 1)
    def _():
        build_m(1, lane <= sub)

    tr, ti = _pow_table(lam_rr, lam_ir, jnp.where(d == 0, t - 1 - sub, sub))
    for hp in range(S5_HG):
        br, bi = bbr[hp:hp + 1], bbi[hp:hp + 1]
        w_ref[hp * t:(hp + 1) * t, 0:S5_PP] = (tr * br - ti * bi).astype(BF16)
        w_ref[hp * t:(hp + 1) * t, S5_PP:2 * S5_PP] = (tr * bi + ti * br).astype(BF16)

    u = jnp.concatenate([u_ref[hp] for hp in range(S5_HG)], axis=1)
    wv = _dot(u, w_ref[...])
    wr_ref[...] = wv[:, 0:S5_PP]
    wi_ref[...] = wv[:, S5_PP:2 * S5_PP]
    lr, li = lam_rr, lam_ir
    for _ in range(t.bit_length() - 1):
        lr, li = _cmul(lr, li, lr, li)

    def make_step(base, stride, count):
        def step(i, carry):
            xr, xi = carry
            j = jnp.where(d == 0, i, count - 1 - i)
            idx = pl.ds(base + j, n_batch, stride=stride)
            xr_ref[idx, :] = xr
            xi_ref[idx, :] = xi
            nr, ni = _cmul(xr, xi, lr, li)
            return nr + wr_ref[idx, :], ni + wi_ref[idx, :]
        return step

    zero = jnp.zeros((n_batch, S5_PP), F32)
    carry = lax.fori_loop(0, n_ctx, make_step(n_batch * n_lat, n_ctx, n_ctx), (zero, zero))
    lax.fori_loop(0, n_lat, make_step(0, n_lat, n_lat), carry)
    x_prev = jnp.concatenate([xr_ref[...], xi_ref[...]], axis=1).astype(BF16)
    y = _dot(u, m_ref[...]) + _dot(x_prev, ys_ref[...])

    @pl.when(d == 0)
    def _():
        for h in range(S5_HG):
            o_ref[h] = y[:, h * t:(h + 1) * t]

    @pl.when(d == 1)
    def _():
        for h in range(S5_HG):
            o_ref[h] += y[:, h * t:(h + 1) * t]


def _s5_params(a_re, a_im, log_dt, b_re, b_im, c_re, c_im):
    a_re, a_im = a_re.astype(F32), a_im.astype(F32)
    dt = jnp.exp(log_dt.astype(F32))[..., None]
    mag = jnp.exp(a_re * dt)
    lam_re, lam_im = mag * jnp.cos(a_im * dt), mag * jnp.sin(a_im * dt)
    den = a_re * a_re + a_im * a_im
    num_re, num_im = lam_re - 1.0, lam_im
    fr = (num_re * a_re + num_im * a_im) / den
    fi = (num_im * a_re - num_re * a_im) / den
    b_re, b_im = b_re.astype(F32), b_im.astype(F32)
    bb_re = fr[..., None] * b_re - fi[..., None] * b_im
    bb_im = fr[..., None] * b_im + fi[..., None] * b_re
    c_re_t = jnp.swapaxes(c_re.astype(F32), -1, -2)
    c_im_t = jnp.swapaxes(c_im.astype(F32), -1, -2)
    col = jnp.concatenate([lam_re[..., None], lam_im[..., None], c_re_t, c_im_t], axis=-1)
    col = jnp.pad(col, ((0, 0),) * 3 + ((0, S5_PP - S5_P), (0, 128 - col.shape[-1])))
    row = jnp.concatenate([jnp.swapaxes(bb_re, -1, -2), jnp.swapaxes(bb_im, -1, -2),
                           lam_re[..., None, :], lam_im[..., None, :]], axis=-2)
    row = jnp.pad(row, ((0, 0),) * 3 + ((0, S5_ROWS - row.shape[-2]), (0, S5_PP - S5_P)))
    return col, row


def _s5_scan(p, col, row, layer, *, n_batch, seq, ctx_len):
    t = S5_T
    n = p.shape[0]
    nc = n // t
    n_ctx, n_lat = ctx_len // t, seq // t
    ut = p[:, COL_SU * BR:(COL_SU + 1) * BR].T.reshape(S5_G, S5_HG, nc, t)
    kern = functools.partial(_s5_kernel, n_batch=n_batch, n_ctx=n_ctx, n_lat=n_lat)
    yt = pl.pallas_call(
        kern,
        out_shape=jax.ShapeDtypeStruct((S5_G, S5_HG, nc, t), F32),
        grid=(S5_G, 2),
        in_specs=[pl.BlockSpec((None, S5_HG, nc, t), lambda g, d: (g, 0, 0, 0)),
                  pl.BlockSpec((None, None, None, S5_PP, 128), lambda g, d: (layer, d, g, 0, 0)),
                  pl.BlockSpec((None, None, None, S5_ROWS, S5_PP), lambda g, d: (layer, d, g, 0, 0))],
        out_specs=pl.BlockSpec((None, S5_HG, nc, t), lambda g, d: (g, 0, 0, 0)),
        scratch_shapes=[pltpu.VMEM((S5_HG * t, S5_HG * t), BF16),
                        pltpu.VMEM((S5_HG * t, 2 * S5_PP), BF16),
                        pltpu.VMEM((2 * S5_PP, S5_HG * t), BF16),
                        pltpu.VMEM((2 * S5_PP, S5_HG * t), F32),
                        pltpu.VMEM((nc, S5_PP), F32), pltpu.VMEM((nc, S5_PP), F32),
                        pltpu.VMEM((nc, S5_PP), F32), pltpu.VMEM((nc, S5_PP), F32)],
        compiler_params=_cparams(("parallel", "arbitrary")),
        name="s5_scan",
    )(ut, col, row)
    return yt.reshape(BR, n).T


def _head_rms(x, g):
    parts = []
    for h in range(NH):
        xh = x[:, h * HD:(h + 1) * HD]
        parts.append(xh * lax.rsqrt(jnp.mean(jnp.square(xh), axis=-1, keepdims=True) + NORM_EPS))
    return jnp.concatenate(parts, axis=-1) * g


def _finish_kernel(raf_ref, rab_ref, rbf_ref, rbb_ref, rc_ref, rdf_ref, rdb_ref, ga_ref, gb_ref, u_ref, gd_ref,
                   mg_ref, x_ref, mod_ref, vec_ref, glu_w_ref, wb_ref, wo_ref, o_ref):
    vec = vec_ref[...]
    oa = _head_rms(raf_ref[...] + rab_ref[...], vec[0:1]) * _silu(ga_ref[...].astype(F32))
    ob = _head_rms(rbf_ref[...] + rbb_ref[...], vec[1:2]) * _silu(gb_ref[...].astype(F32))
    od = _head_rms(rdf_ref[...] + rdb_ref[...], vec[2:3]) * _silu(gd_ref[...].astype(F32))
    yc = rc_ref[...] + vec[3:4] * u_ref[...].astype(F32)
    yc = 0.5 * yc * (1.0 + lax.erf(yc * (2.0 ** -0.5)))
    oc = yc * _sigmoid(_dot(yc.astype(BF16), glu_w_ref[...]) + vec[4:5])
    y = None
    for j, o in enumerate((oa, ob, oc, od)):
        gate = _sigmoid_tanh(mg_ref[:, j * D_MODEL:(j + 1) * D_MODEL].astype(F32))
        term = gate * _dot(o.astype(BF16), wb_ref[j])
        y = term if y is None else y + term
    mix = _dot(y.astype(BF16), wo_ref[...])
    o_ref[...] = x_ref[...] + mod_ref[2:3, :] * mix


def _finish(raw_a, raw_b, raw_c, raw_d, p, x_all, mod, vec, glu_w, w_branch, w_out, layer, *, n_rows, n_batch,
            seq, tm=256):
    mrow = _mod_row_map(n_rows, tm, n_batch, seq)
    one = pl.Buffered(1)

    def pcol(cblock):
        return pl.BlockSpec((tm, BR), lambda i: (i, cblock))

    raw = pl.BlockSpec((tm, BR), lambda i: (i, 0))
    return pl.pallas_call(
        _finish_kernel,
        out_shape=jax.ShapeDtypeStruct((n_rows, D_MODEL), F32),
        grid=(n_rows // tm,),
        in_specs=[raw, raw, raw, raw, raw, raw, raw,
                  pcol(COL_HG), pcol(COL_RG), pcol(COL_SU), pcol(COL_MZ),
                  pl.BlockSpec((tm, N_BRANCH * D_MODEL), lambda i: (i, COL_MERGE)),
                  pl.BlockSpec((tm, D_MODEL), lambda i: (i, 0)),
                  pl.BlockSpec((None, 8, D_MODEL), lambda i: (mrow(i), 0, 0)),
                  pl.BlockSpec((None, 8, BR), lambda i: (layer, 0, 0), pipeline_mode=one),
                  pl.BlockSpec((None, BR, BR), lambda i: (layer, 0, 0), pipeline_mode=one),
                  pl.BlockSpec((None, N_BRANCH, BR, D_MODEL), lambda i: (layer, 0, 0, 0), pipeline_mode=one),
                  pl.BlockSpec((None, D_MODEL, D_MODEL), lambda i: (layer, 0, 0), pipeline_mode=one)],
        out_specs=pl.BlockSpec((tm, D_MODEL), lambda i: (i, 0)),
        compiler_params=_cparams(("parallel",)),
        name="finish",
    )(*raw_a, *raw_b, raw_c, *raw_d, p, p, p, p, p, x_all, mod, vec, glu_w, w_branch, w_out)


def _mlp_kernel(x_ref, mod_ref, g_ref, w1_ref, w2_ref, fin_ref, o_ref, xn_ref, acc_ref, *, final_norm):
    j = pl.program_id(1)

    @pl.when(j == 0)
    def _():
        h = _norm_mod(x_ref[...], g_ref[...], mod_ref[3:4, :], mod_ref[4:5, :])
        xn_ref[...] = h.astype(BF16)
        acc_ref[...] = jnp.zeros_like(acc_ref)

    a = jnp.square(jnp.maximum(_dot(xn_ref[...], w1_ref[...]), 0.0))
    acc_ref[...] += _dot(a.astype(BF16), w2_ref[...])

    @pl.when(j == pl.num_programs(1) - 1)
    def _():
        y = x_ref[...] + mod_ref[5:6, :] * acc_ref[...]
        if final_norm:
            y = (y * lax.rsqrt(jnp.mean(jnp.square(y), axis=-1, keepdims=True) + NORM_EPS)) * fin_ref[...]
        o_ref[...] = y


def _mlp(x_all, mod, norm_w, w1, w2, fin_w, layer, *, n_rows, n_batch, seq, final_norm, tm=512, tf=1024):
    mrow = _mod_row_map(n_rows, tm, n_batch, seq)
    kern = functools.partial(_mlp_kernel, final_norm=final_norm)
    return pl.pallas_call(
        kern,
        out_shape=jax.ShapeDtypeStruct((n_rows, D_MODEL), F32),
        grid=(n_rows // tm, D_FF // tf),
        in_specs=[pl.BlockSpec((tm, D_MODEL), lambda i, j: (i, 0)),
                  pl.BlockSpec((None, 8, D_MODEL), lambda i, j: (mrow(i), 0, 0)),
                  pl.BlockSpec((None, 1, D_MODEL), lambda i, j: (layer, 0, 0)),
                  pl.BlockSpec((None, D_MODEL, tf), lambda i, j: (layer, 0, j)),
                  pl.BlockSpec((None, tf, D_MODEL), lambda i, j: (layer, j, 0)),
                  pl.BlockSpec((1, D_MODEL), lambda i, j: (0, 0))],
        out_specs=pl.BlockSpec((tm, D_MODEL), lambda i, j: (i, 0)),
        scratch_shapes=[pltpu.VMEM((tm, D_MODEL), BF16), pltpu.VMEM((tm, D_MODEL), F32)],
        compiler_params=_cparams(("parallel", "arbitrary")),
        name="mlp",
    )(x_all, mod, norm_w, w1, w2, fin_w.reshape(1, D_MODEL))


W_PREP_ROWS = 1024
N_MERGE_BLK = N_BRANCH * D_MODEL // W_PREP_ROWS
MERGE_OFF = GATE_OFF + 4 * NH


def _w_prep_kernel(a_ref, g_ref, main_ref, gate_ref):
    @pl.when(pl.program_id(1) == 0)
    def _():
        gate_ref[...] = jnp.zeros_like(gate_ref)
        gate_ref[0:4 * NH, :] = g_ref[0].astype(BF16)

    main_ref[...] = a_ref[0].astype(BF16)


def _split_w_in(w):
    depth = w.shape[0]
    wt = jnp.swapaxes(w, 1, 2)
    tr = W_PREP_ROWS

    def a_map(l, j):
        row = jnp.where(j < N_MERGE_BLK, MERGE_OFF + j * tr, (j - N_MERGE_BLK) * tr)
        return l, pl.multiple_of(row, 16), 0

    return pl.pallas_call(
        _w_prep_kernel,
        out_shape=(jax.ShapeDtypeStruct((depth, P_WIDTH, D_MODEL), BF16),
                   jax.ShapeDtypeStruct((depth, GATE_PAD, D_MODEL), BF16)),
        grid=(depth, P_WIDTH // tr),
        in_specs=[pl.BlockSpec((pl.Element(1), pl.Element(tr), pl.Element(D_MODEL)), a_map),
                  pl.BlockSpec((pl.Element(1), pl.Element(4 * NH), pl.Element(D_MODEL)),
                               lambda l, j: (l, GATE_OFF, 0))],
        out_specs=(pl.BlockSpec((None, tr, D_MODEL), lambda l, j: (l, j, 0)),
                   pl.BlockSpec((None, GATE_PAD, D_MODEL), lambda l, j: (l, 0, 0))),
        compiler_params=_cparams(("parallel", "arbitrary")),
        name="w_in_prep",
    )(wt, wt)


def kernel(x, c, ctx, c_ctx, w_mod, b_mod, norm_mix, norm_mlp, w_in, hgrn_lb_logits, hgrn_norm, ret_decay, ret_norm, s5_a_re, s5_a_im, s5_log_dt, s5_b_re, s5_b_im, s5_c_re, s5_c_im, s5_d, s5_glu_w, s5_glu_b, mlstm_conv_w, mlstm_conv_b, mlstm_gate_b, mlstm_norm, w_branch, w_out, w_ff1, w_ff2, final_norm):
    n_batch, seq, _ = x.shape
    ctx_len = ctx.shape[1]
    depth = w_in.shape[0]
    nl = n_batch * seq
    dims = dict(n_batch=n_batch, seq=seq)

    p_lb = jax.nn.softmax(hgrn_lb_logits.astype(F32), axis=0)
    lower_bounds = (jnp.cumsum(p_lb, axis=0) - p_lb[0]).reshape(depth, 2, 1, BR)
    lg = jnp.log1p(-jnp.exp(ret_decay.astype(F32)))
    gate_b = jnp.pad(mlstm_gate_b.reshape(depth, 1, 4 * NH).astype(F32), ((0, 0), (0, 0), (0, GATE_PAD - 4 * NH)))
    vec = jnp.stack([hgrn_norm, ret_norm, mlstm_norm, s5_d, s5_glu_b], axis=1).astype(F32)
    vec = jnp.pad(vec, ((0, 0), (0, 8 - vec.shape[1]), (0, 0)))
    w_main, w_gate = _split_w_in(w_in)
    glu_w, wb, wo = s5_glu_w.astype(BF16), w_branch.astype(BF16), w_out.astype(BF16)
    w1, w2 = w_ff1.astype(BF16), w_ff2.astype(BF16)
    norm_mix3 = norm_mix.reshape(depth, 1, D_MODEL)
    norm_mlp3 = norm_mlp.reshape(depth, 1, D_MODEL)
    conv_w = mlstm_conv_w.reshape(depth, 9, 2 * BR)
    conv_b = mlstm_conv_b.reshape(depth, 1, 2 * BR)
    s5_col, s5_row = _s5_params(s5_a_re, s5_a_im, s5_log_dt, s5_b_re, s5_b_im, s5_c_re, s5_c_im)

    c_pad = jnp.zeros((8, D_MODEL), F32).at[:n_batch].set(c).at[n_batch].set(c_ctx)
    x_all = jnp.concatenate([x.reshape(nl, D_MODEL), ctx.reshape(n_batch * ctx_len, D_MODEL)], axis=0)

    for l in range(depth):
        last = l == depth - 1
        mod = _modulation(c_pad, w_mod, b_mod, l)
        mod = jnp.pad(mod[:n_batch + 1].reshape(n_batch + 1, N_MOD, D_MODEL), ((0, 0), (0, 8 - N_MOD), (0, 0)))
        p, gates = _in_proj(x_all, mod, norm_mix3, w_main, w_gate, l, **dims)
        qk = _mlstm_conv(p, conv_w, conv_b, l, ctx_len=ctx_len, **dims)
        raw_a, raw_b, raw_d = _scans(p, gates, qk, lg[l], lower_bounds, gate_b, l, ctx_len=ctx_len, **dims)
        raw_c = _s5_scan(p, s5_col, s5_row, l, ctx_len=ctx_len, **dims)
        n_rows = nl if last else x_all.shape[0]
        x_mid = _finish(raw_a, raw_b, raw_c, raw_d, p, x_all, mod, vec, glu_w, wb, wo, l, n_rows=n_rows, **dims)
        x_all = _mlp(x_mid, mod, norm_mlp3, w1, w2, final_norm, l, n_rows=n_rows, final_norm=last, **dims)
    return x_all.reshape(n_batch, seq, D_MODEL)
```

```python
import functools
import math

import jax
import jax.numpy as jnp
from jax import lax
from jax.experimental import pallas as pl
from jax.experimental.pallas import tpu as pltpu

F32 = jnp.float32
BF16 = jnp.bfloat16

D_MODEL = 2048
N_BRANCH = 4
BR = D_MODEL // N_BRANCH
HD = 128
NH = BR // HD
S5_HG = 16
S5_G = BR // S5_HG
S5_P = 64
D_FF = 4 * D_MODEL
N_MOD = 6
GRID_W = 64
NORM_EPS = 1e-6
NEG_BIG = -1e30
F_TINY = 1e-30
S5_DT_MIN = 1e-3

COL_MERGE = 0
COL_HQ, COL_HF, COL_HV, COL_HG = 16, 17, 19, 20
COL_RQ, COL_RK, COL_RV, COL_RG = 21, 22, 23, 24
COL_SU = 25
COL_MQK, COL_MV, COL_MZ = 26, 28, 29
P_WIDTH = 30 * BR
GATE_OFF = 14 * BR
GATE_PAD = 128

T_HGRN = 128
T_ATTN = 256
S5_T = 128
S5_PP = 128
S5_ROWS = 40
VMEM_LIMIT = 56 * 1024 * 1024

_HI = lax.Precision.HIGHEST


def _cparams(sem):
    return pltpu.CompilerParams(dimension_semantics=sem, vmem_limit_bytes=VMEM_LIMIT)


def _dot(a, b):
    return jnp.dot(a, b, preferred_element_type=F32)


def _dot_nt(a, b):
    return lax.dot_general(a, b, (((1,), (1,)), ((), ())), preferred_element_type=F32)


def _dot_tn(a, b):
    return lax.dot_general(a, b, (((0,), (0,)), ((), ())), preferred_element_type=F32)


def _split3(x):
    hi = x.astype(BF16)
    r1 = x - hi.astype(F32)
    mid = r1.astype(BF16)
    return hi, mid, (r1 - mid.astype(F32)).astype(BF16)


def _cumsum_rows(tri, x):
    hi, mid, lo = _split3(x)
    return _dot(tri, hi) + (_dot(tri, mid) + _dot(tri, lo))


def _sigmoid(x):
    return 1.0 / (1.0 + jnp.exp(-x))


def _sigmoid_tanh(x):
    return 0.5 * jnp.tanh(0.5 * x) + 0.5


def _silu(x):
    return x * _sigmoid(x)


def _mod_kernel(c_ref, w_ref, b_ref, o_ref):
    @pl.when(pl.program_id(0) == 0)
    def _():
        o_ref[...] = jnp.broadcast_to(b_ref[...], o_ref.shape)

    a = _silu(c_ref[...]).astype(BF16)
    o_ref[...] += _dot(a, w_ref[...].astype(BF16))


def _modulation(c_pad, w_mod, b_mod, layer, tk=256):
    n = w_mod.shape[2]
    return pl.pallas_call(
        _mod_kernel,
        out_shape=jax.ShapeDtypeStruct((8, n), F32),
        grid=(D_MODEL // tk,),
        in_specs=[pl.BlockSpec((8, tk), lambda k: (0, k)),
                  pl.BlockSpec((None, tk, n), lambda k: (layer, k, 0)),
                  pl.BlockSpec((None, 1, n), lambda k: (layer, 0, 0))],
        out_specs=pl.BlockSpec((8, n), lambda k: (0, 0)),
        compiler_params=_cparams(("arbitrary",)),
        name="modulation",
    )(c_pad, w_mod, b_mod.reshape(b_mod.shape[0], 1, n))


def _norm_mod(x, g, shift, scale):
    y = x * lax.rsqrt(jnp.mean(jnp.square(x), axis=-1, keepdims=True) + NORM_EPS)
    return (y * g) * (1.0 + scale) + shift


def _inproj_kernel(x_ref, mod_ref, g_ref, w_ref, wg_ref, p_ref, gate_ref, xn_ref):
    @pl.when(pl.program_id(1) == 0)
    def _():
        h = _norm_mod(x_ref[...], g_ref[...], mod_ref[0:1, :], mod_ref[1:2, :])
        xn_ref[...] = h.astype(BF16)
        gate_ref[...] = _dot_nt(xn_ref[...], wg_ref[...])

    p_ref[...] = _dot_nt(xn_ref[...], w_ref[...]).astype(p_ref.dtype)


def _mod_row_map(n_rows, tm, n_batch, seq):
    assert n_rows % tm == 0 and seq % tm == 0, (n_rows, seq, tm)
    n_lat_tiles, tiles_per_batch = n_batch * seq // tm, seq // tm

    def row(i):
        return jnp.where(i < n_lat_tiles, i // tiles_per_batch, n_batch)
    return row


def _in_proj(x_all, mod, norm_w, w_main, w_gate, layer, *, n_batch, seq, tm=1024, tn=1024):
    n = x_all.shape[0]
    mrow = _mod_row_map(n, tm, n_batch, seq)
    return pl.pallas_call(
        _inproj_kernel,
        out_shape=(jax.ShapeDtypeStruct((n, P_WIDTH), BF16),
                   jax.ShapeDtypeStruct((n, GATE_PAD), F32)),
        grid=(n // tm, P_WIDTH // tn),
        in_specs=[pl.BlockSpec((tm, D_MODEL), lambda i, j: (i, 0)),
                  pl.BlockSpec((None, 8, D_MODEL), lambda i, j: (mrow(i), 0, 0)),
                  pl.BlockSpec((None, 1, D_MODEL), lambda i, j: (layer, 0, 0)),
                  pl.BlockSpec((None, tn, D_MODEL), lambda i, j: (layer, j, 0)),
                  pl.BlockSpec((None, GATE_PAD, D_MODEL), lambda i, j: (layer, 0, 0))],
        out_specs=(pl.BlockSpec((tm, tn), lambda i, j: (i, j)),
                   pl.BlockSpec((tm, GATE_PAD), lambda i, j: (i, 0))),
        scratch_shapes=[pltpu.VMEM((tm, D_MODEL), BF16)],
        compiler_params=_cparams(("parallel", "arbitrary")),
        name="in_proj",
    )(x_all, mod, norm_w, w_main, w_gate)


CONV_ROWS = 256


def _conv_kernel(up_ref, x_ref, dn_ref, w_ref, b_ref, o_ref, *, n_lat_blocks, blocks_per_img):
    rb = pl.program_id(0)
    x = x_ref[...].astype(F32)
    w = w_ref[...]
    n = CONV_ROWS

    @pl.when(rb < n_lat_blocks)
    def _():
        rr = rb % blocks_per_img
        up = jnp.where(rr > 0, up_ref[...].astype(F32), 0.0)
        dn = jnp.where(rr < blocks_per_img - 1, dn_ref[...].astype(F32), 0.0)
        ext = jnp.concatenate([up, x, dn], axis=0)
        ne = n + 2 * GRID_W
        col = lax.broadcasted_iota(jnp.int32, (ne, 1), 0) % GRID_W
        xl = jnp.where(col >= 1, pltpu.roll(ext, 1, 0), 0.0)
        xr = jnp.where(col <= GRID_W - 2, pltpu.roll(ext, ne - 1, 0), 0.0)
        acc = None
        for i in range(3):
            y = w[3 * i:3 * i + 1] * xl + w[3 * i + 1:3 * i + 2] * ext + w[3 * i + 2:3 * i + 3] * xr
            part = y[i * GRID_W:i * GRID_W + n]
            acc = part if acc is None else acc + part
        o_ref[...] = _silu(acc + b_ref[...]).astype(o_ref.dtype)

    @pl.when(rb >= n_lat_blocks)
    def _():
        t = lax.broadcasted_iota(jnp.int32, (n, 1), 0)
        xl = jnp.where(t >= 1, pltpu.roll(x, 1, 0), 0.0)
        xr = jnp.where(t <= n - 2, pltpu.roll(x, n - 1, 0), 0.0)
        acc = w[3:4] * xl + w[4:5] * x + w[5:6] * xr
        o_ref[...] = _silu(acc + b_ref[...]).astype(o_ref.dtype)


def _mlstm_conv(p, conv_w, conv_b, layer, *, n_batch, seq, ctx_len, cb=2 * BR):
    assert ctx_len == CONV_ROWS and seq % CONV_ROWS == 0
    n = p.shape[0]
    n_blocks = n // CONV_ROWS
    hb = CONV_ROWS // GRID_W
    n_halo = n // GRID_W
    c0 = COL_MQK * BR // cb
    kern = functools.partial(_conv_kernel, n_lat_blocks=n_batch * seq // CONV_ROWS,
                             blocks_per_img=seq // CONV_ROWS)
    return pl.pallas_call(
        kern,
        out_shape=jax.ShapeDtypeStruct((n, 2 * BR), BF16),
        grid=(n_blocks, 2 * BR // cb),
        in_specs=[pl.BlockSpec((GRID_W, cb), lambda r, c: (jnp.maximum(r * hb - 1, 0), c0 + c)),
                  pl.BlockSpec((CONV_ROWS, cb), lambda r, c: (r, c0 + c)),
                  pl.BlockSpec((GRID_W, cb), lambda r, c: (jnp.minimum((r + 1) * hb, n_halo - 1), c0 + c)),
                  pl.BlockSpec((None, 9, cb), lambda r, c: (layer, 0, c)),
                  pl.BlockSpec((None, 1, cb), lambda r, c: (layer, 0, c))],
        out_specs=pl.BlockSpec((CONV_ROWS, cb), lambda r, c: (r, c)),
        compiler_params=_cparams(("parallel", "parallel")),
        name="mlstm_conv",
    )(p, p, p, conv_w, conv_b)


def _chunk_block(b, d, i, *, n_ctx, n_lat, n_batch):
    ctx_j = i if d == 0 else n_ctx - 1 - i
    lat_j = i - n_ctx if d == 0 else n_ctx + n_lat - 1 - i
    return jnp.where(i < n_ctx, n_batch * n_lat + b * n_ctx + ctx_j, b * n_lat + lat_j)


def _order_mask(d, t):
    r = lax.broadcasted_iota(jnp.int32, (t, t), 0)
    c = lax.broadcasted_iota(jnp.int32, (t, t), 1)
    return (c <= r) if d == 0 else (c >= r)


def _init_at_first_step(*refs):
    @pl.when(pl.program_id(1) == 0)
    def _():
        for ref in refs:
            ref[...] = jnp.zeros_like(ref)


def _ret_direction(d, lg_ref, q_ref, k_ref, v_ref, o_ref, s_ref):
    t = T_ATTN
    r = lax.broadcasted_iota(jnp.int32, (t, t), 0)
    c = lax.broadcasted_iota(jnp.int32, (t, t), 1)
    rel = (r - c) if d == 0 else (c - r)
    relf = jnp.maximum(rel, 0).astype(F32)
    tt = lax.broadcasted_iota(jnp.int32, (t, 1), 0)
    pos = (tt if d == 0 else t - 1 - tt).astype(F32)
    scale = HD ** -0.5
    for h in range(NH):
        sl = slice(h * HD, (h + 1) * HD)
        lg = lg_ref[d, h]
        q = q_ref[:, sl]
        k = k_ref[:, sl]
        v = v_ref[:, sl]
        intra = jnp.where(rel >= 0, jnp.exp(lg * relf), 0.0)
        scores = (_dot_nt(q, k) * scale) * intra
        s_old = s_ref[d, h]
        out = _dot(scores.astype(BF16), v) + jnp.exp(lg * (pos + 1.0)) * _dot(q, s_old.astype(BF16))
        kd = (k.astype(F32) * (scale * jnp.exp(lg * (t - 1.0 - pos)))).astype(BF16)
        s_ref[d, h] = jnp.exp(lg * t) * s_old + _dot_tn(kd, v)
        o_ref[:, sl] = out


def _ret_kernel(lg_ref, qf_ref, kf_ref, vf_ref, qb_ref, kb_ref, vb_ref, of_ref, ob_ref, s_ref):
    _init_at_first_step(s_ref)
    _ret_direction(0, lg_ref, qf_ref, kf_ref, vf_ref, of_ref, s_ref)
    _ret_direction(1, lg_ref, qb_ref, kb_ref, vb_ref, ob_ref, s_ref)


def _hgrn_direction(d, lb_ref, q_ref, z_ref, v_ref, o_ref, st_ref):
    t = T_HGRN
    fwd = d == 0
    lb = lb_ref[...]
    z = z_ref[...].astype(F32)
    f = lb + (1.0 - lb) * _sigmoid(z)
    logf = jnp.log(jnp.maximum(f, F_TINY))
    kk = (1.0 - lb) * _sigmoid(-z)
    qq = _silu(q_ref[...].astype(F32))
    r2 = lax.broadcasted_iota(jnp.int32, (t, t), 0)
    c2 = lax.broadcasted_iota(jnp.int32, (t, t), 1)
    tri = jnp.where((c2 <= r2) if fwd else (c2 >= r2), 1.0, 0.0).astype(BF16)
    cum = _cumsum_rows(tri, logf)
    row = lax.broadcasted_iota(jnp.int32, (t, 1), 0)
    if fwd:
        e_ref = jnp.where(row == 0, 0.0, pltpu.roll(cum, 1, 0))
    else:
        e_ref = jnp.where(row == t - 1, 0.0, pltpu.roll(cum, t - 1, 0))
    f_ref = cum
    levels = []
    w = 1
    while w < t:
        q_l = (qq * jnp.exp(cum - e_ref)).astype(BF16)
        k_l = (kk * jnp.exp(f_ref - cum)).astype(BF16)
        late, early = (w, 0) if fwd else (0, w)
        key = ((r2 ^ c2) & ~(2 * w - 1)) | ((r2 & w) ^ late) | ((c2 & w) ^ early)
        levels.append((q_l, k_l, key == 0))
        upper = (row & w) != 0
        if fwd:
            e_ref = jnp.where(upper, pltpu.roll(e_ref, w, 0), e_ref)
            f_ref = jnp.where(upper, f_ref, pltpu.roll(f_ref, t - w, 0))
        else:
            e_ref = jnp.where(upper, e_ref, pltpu.roll(e_ref, t - w, 0))
            f_ref = jnp.where(upper, pltpu.roll(f_ref, w, 0), f_ref)
        w *= 2
    q_b = qq.astype(BF16)
    k_b = kk.astype(BF16)
    tail = cum[t - 1:t] if fwd else cum[0:1]
    q_state = (qq * jnp.exp(cum)).astype(BF16)
    k_state = (kk * jnp.exp(tail - cum)).astype(BF16)
    decay = jnp.exp(tail)
    for h in range(NH):
        sl = slice(h * HD, (h + 1) * HD)
        v_h = v_ref[:, sl]
        scores = jnp.where(r2 == c2, _dot_nt(q_b[:, sl], k_b[:, sl]), 0.0)
        for q_l, k_l, m in levels:
            scores = jnp.where(m, _dot_nt(q_l[:, sl], k_l[:, sl]), scores)
        st = st_ref[d, h]
        o_ref[:, sl] = _dot(scores.astype(BF16), v_h) + _dot_nt(q_state[:, sl], st.astype(BF16))
        st_ref[d, h] = st * decay[:, sl] + _dot_tn(v_h, k_state[:, sl])


def _hgrn_kernel(lbf_ref, qf_ref, zf_ref, vf_ref, lbb_ref, qb_ref, zb_ref, vb_ref, of_ref, ob_ref, st_ref):
    _init_at_first_step(st_ref)
    _hgrn_direction(0, lbf_ref, qf_ref, zf_ref, vf_ref, of_ref, st_ref)
    _hgrn_direction(1, lbb_ref, qb_ref, zb_ref, vb_ref, ob_ref, st_ref)


def _log_sigmoid(x):
    return jnp.minimum(x, 0.0) - jnp.log(1.0 + jnp.exp(-jnp.abs(x)))


def _rep_lane(parts, lane, width):
    sel = jnp.where(lax.broadcasted_iota(jnp.int32, (GATE_PAD, width), 0) == lane, 1.0, 0.0).astype(BF16)
    hi, mid, lo = parts
    return _dot(hi, sel) + (_dot(mid, sel) + _dot(lo, sel))


def _mlstm_direction(d, qk_ref, v_ref, g_ref, gb_ref, o_ref, c_ref, m_ref):
    t = T_ATTN
    assert t == 2 * HD
    fwd = d == 0
    g = g_ref[...] + gb_ref[...]
    r2 = lax.broadcasted_iota(jnp.int32, (t, t), 0)
    c2 = lax.broadcasted_iota(jnp.int32, (t, t), 1)
    mask = (c2 <= r2) if fwd else (c2 >= r2)
    tri = jnp.where(mask, 1.0, 0.0).astype(BF16)
    eye = jnp.where(r2 == c2, 1.0, 0.0).astype(BF16)
    lf = _log_sigmoid(g)
    cum_c = _cumsum_rows(tri, lf)
    a_col = g - pltpu.roll(cum_c, GATE_PAD - NH, 1)
    row = lax.broadcasted_iota(jnp.int32, (t, 1), 0)
    pm = a_col
    k = 1
    while k < t:
        if fwd:
            pm = jnp.maximum(pm, jnp.where(row >= k, pltpu.roll(pm, k, 0), NEG_BIG))
        else:
            pm = jnp.maximum(pm, jnp.where(row < t - k, pltpu.roll(pm, t - k, 0), NEG_BIG))
        k *= 2
    a_parts, pm_parts, cum_parts = _split3(a_col), _split3(pm), _split3(cum_c)
    a_hi, a_mid, a_lo = a_parts
    a_rows = _dot_tn(a_hi, eye) + (_dot_tn(a_mid, eye) + _dot_tn(a_lo, eye))
    scale = HD ** -0.5
    last = t - 1 if fwd else 0
    ones = jnp.ones((t, HD), BF16)
    for h in range(NH):
        sl = slice(h * HD, (h + 1) * HD)
        li, lfw = 2 * NH * d + h, 2 * NH * d + NH + h
        m_old = m_ref[d, h]
        mx2 = jnp.maximum(_rep_lane(pm_parts, li, 2 * HD), jnp.concatenate([m_old, m_old], axis=1))
        mx = mx2[:, 0:HD]
        cc = _rep_lane(cum_parts, lfw, HD)
        a_rep = _rep_lane(a_parts, li, HD)
        a_row = a_rows[li:li + 1]
        q = qk_ref[:, sl]
        k_h = qk_ref[:, BR + h * HD:BR + (h + 1) * HD]
        v_aug = jnp.concatenate([v_ref[:, sl], ones], axis=1)
        w = jnp.where(mask, jnp.exp(a_row - mx2), 0.0)
        scores = (_dot_nt(q, k_h) * scale) * w
        w_state = jnp.exp(m_old - mx)
        c_old = c_ref[d, h]
        s_hi = scores.astype(BF16)
        s_mid = (scores - s_hi.astype(F32)).astype(BF16)
        intra = _dot(s_hi, v_aug)
        inter = _dot(q, c_old.astype(BF16))
        num = intra[:, 0:HD] + w_state * inter[:, 0:HD]
        den = (intra[:, HD:2 * HD] + _dot(s_mid, ones)) + w_state * inter[:, HD:2 * HD]
        o_ref[:, sl] = num / jnp.maximum(jnp.abs(den), jnp.exp(-(cc + mx)))
        total = cc[last:last + 1]
        m_new_rel = mx[last:last + 1]
        keep = jnp.exp(m_old - m_new_rel)
        w_end = jnp.exp(a_rep - m_new_rel)
        kw = (k_h.astype(F32) * (scale * w_end)).astype(BF16)
        c_ref[d, h] = jnp.concatenate([keep, keep], axis=1) * c_old + _dot_tn(kw, v_aug)
        m_ref[d, h] = total + m_new_rel


def _mlstm_kernel(qkf_ref, vf_ref, gf_ref, qkb_ref, vb_ref, gb_ref, bias_ref, of_ref, ob_ref, c_ref, m_ref):
    _init_at_first_step(c_ref, m_ref)
    _mlstm_direction(0, qkf_ref, vf_ref, gf_ref, bias_ref, of_ref, c_ref, m_ref)
    _mlstm_direction(1, qkb_ref, vb_ref, gb_ref, bias_ref, ob_ref, c_ref, m_ref)


def _scans(p, gates, qk, lg, lb, gate_b, layer, *, n_batch, seq, ctx_len):
    n = p.shape[0]
    state = pltpu.VMEM((2, NH, HD, HD), F32)
    vec = pltpu.VMEM((2, NH, 1, HD), F32)

    def call(kernel, t, inputs, make_specs, scratch, name):
        n_ctx, n_lat = ctx_len // t, seq // t

        def rows(d, width, cblock=0):
            return pl.BlockSpec((t, width), lambda b, i: (_chunk_block(b, d, i, n_ctx=n_ctx, n_lat=n_lat,
                                                                        n_batch=n_batch), cblock))

        out = jax.ShapeDtypeStruct((n, BR), F32)
        return pl.pallas_call(
            kernel,
            out_shape=(out, out),
            grid=(n_batch, n_ctx + n_lat),
            in_specs=make_specs(rows),
            out_specs=(rows(0, BR), rows(1, BR)),
            scratch_shapes=scratch,
            compiler_params=_cparams(("parallel", "arbitrary")),
            name=name,
        )(*inputs)

    raw_b = call(_ret_kernel, T_ATTN, (lg, p, p, p, p, p, p),
                 lambda rows: [pl.BlockSpec(memory_space=pltpu.SMEM)]
                 + [rows(d, BR, c) for d in (0, 1) for c in (COL_RQ, COL_RK, COL_RV)],
                 [state], "retention_scan")
    lb_spec = lambda d: pl.BlockSpec((None, None, 1, BR), lambda b, i: (layer, d, 0, 0))
    raw_a = call(_hgrn_kernel, T_HGRN, (lb, p, p, p, lb, p, p, p),
                 lambda rows: [spec for d in (0, 1)
                               for spec in (lb_spec(d), rows(d, BR, COL_HQ), rows(d, BR, COL_HF + d),
                                            rows(d, BR, COL_HV))],
                 [state], "hgrn2_scan")
    raw_d = call(_mlstm_kernel, T_ATTN, (qk, p, gates, qk, p, gates, gate_b),
                 lambda rows: [spec for d in (0, 1)
                               for spec in (rows(d, 2 * BR), rows(d, BR, COL_MV), rows(d, GATE_PAD))]
                 + [pl.BlockSpec((None, 1, GATE_PAD), lambda b, i: (layer, 0, 0))],
                 [pltpu.VMEM((2, NH, HD, 2 * HD), F32), vec], "mlstm_scan")
    return raw_a, raw_b, raw_d


def _cmul(ar, ai, br, bi):
    return ar * br - ai * bi, ar * bi + ai * br


def _pow_table(lam_r, lam_i, expo):
    pr = jnp.ones(expo.shape, F32)
    pi = jnp.zeros(expo.shape, F32)
    ar = jnp.broadcast_to(lam_r, expo.shape)
    ai = jnp.broadcast_to(lam_i, expo.shape)
    for k in range(S5_T.bit_length() - 1):
        nr, ni = _cmul(pr, pi, ar, ai)
        bit = (expo & (1 << k)) != 0
        pr, pi = jnp.where(bit, nr, pr), jnp.where(bit, ni, pi)
        ar, ai = _cmul(ar, ai, ar, ai)
    return pr, pi


def _s5_kernel(u_ref, col_ref, row_ref, o_ref, m_ref, w_ref, ys_ref, z_ref, wr_ref, wi_ref, xr_ref, xi_ref, *,
               n_batch, n_ctx, n_lat):
    d = pl.program_id(1)
    t = S5_T
    col = col_ref[...]
    rowp = row_ref[...]
    lam_rc, lam_ic = col[:, 0:1], col[:, 1:2]
    bbr, bbi = rowp[0:S5_HG], rowp[S5_HG:2 * S5_HG]
    lam_rr, lam_ir = rowp[2 * S5_HG:2 * S5_HG + 1], rowp[2 * S5_HG + 1:2 * S5_HG + 2]
    lane = lax.broadcasted_iota(jnp.int32, (t, t), 1)
    sub = lax.broadcasted_iota(jnp.int32, (t, t), 0)

    pr, pi = _pow_table(lam_rc, lam_ic, jnp.where(d == 0, lane, t - 1 - lane))
    qr, qi = _cmul(pr, pi, lam_rc, lam_ic)
    for h in range(S5_HG):
        cr, ci = col[:, 2 + h:3 + h], col[:, 2 + S5_HG + h:3 + S5_HG + h]
        hs = slice(h * t, (h + 1) * t)
        z_ref[0:S5_PP, hs] = cr * pr - ci * pi
        z_ref[S5_PP:2 * S5_PP, hs] = -(cr * pi + ci * pr)
        ys_ref[0:S5_PP, hs] = (cr * qr - ci * qi).astype(BF16)
        ys_ref[S5_PP:2 * S5_PP, hs] = (-(cr * qi + ci * qr)).astype(BF16)
    b_hi, b_mid, _ = _split3(jnp.concatenate([bbr, bbi], axis=1))
    z_hi, z_mid, _ = _split3(z_ref[...])
    krow = _dot(b_hi, z_hi) + (_dot(b_hi, z_mid) + _dot(b_mid, z_hi))

    def build_m(shift, keep):
        for hp in range(S5_HG):
            for h in range(S5_HG):
                tile = jnp.broadcast_to(krow[hp:hp + 1, h * t:(h + 1) * t], (t, t))
                tile = pltpu.roll(tile, shift, 1, stride=1, stride_axis=0)
                m_ref[hp * t:(hp + 1) * t, h * t:(h + 1) * t] = jnp.where(keep, tile, 0.0).astype(BF16)

    @pl.when(d == 0)
    def _():
        build_m(0, lane >= sub)

    @pl.when(d == 1)
    def _():
        build_m(1, lane <= sub)

    tr, ti = _pow_table(lam_rr, lam_ir, jnp.where(d == 0, t - 1 - sub, sub))
    for hp in range(S5_HG):
        br, bi = bbr[hp:hp + 1], bbi[hp:hp + 1]
        w_ref[hp * t:(hp + 1) * t, 0:S5_PP] = (tr * br - ti * bi).astype(BF16)
        w_ref[hp * t:(hp + 1) * t, S5_PP:2 * S5_PP] = (tr * bi + ti * br).astype(BF16)

    u = jnp.concatenate([u_ref[hp] for hp in range(S5_HG)], axis=1)
    wv = _dot(u, w_ref[...])
    wr_ref[...] = wv[:, 0:S5_PP]
    wi_ref[...] = wv[:, S5_PP:2 * S5_PP]
    lr, li = lam_rr, lam_ir
    for _ in range(t.bit_length() - 1):
        lr, li = _cmul(lr, li, lr, li)

    def make_step(base, stride, count):
        def step(i, carry):
            xr, xi = carry
            j = jnp.where(d == 0, i, count - 1 - i)
            idx = pl.ds(base + j, n_batch, stride=stride)
            xr_ref[idx, :] = xr
            xi_ref[idx, :] = xi
            nr, ni = _cmul(xr, xi, lr, li)
            return nr + wr_ref[idx, :], ni + wi_ref[idx, :]
        return step

    zero = jnp.zeros((n_batch, S5_PP), F32)
    carry = lax.fori_loop(0, n_ctx, make_step(n_batch * n_lat, n_ctx, n_ctx), (zero, zero))
    lax.fori_loop(0, n_lat, make_step(0, n_lat, n_lat), carry)
    x_prev = jnp.concatenate([xr_ref[...], xi_ref[...]], axis=1).astype(BF16)
    y = _dot(u, m_ref[...]) + _dot(x_prev, ys_ref[...])

    @pl.when(d == 0)
    def _():
        for h in range(S5_HG):
            o_ref[h] = y[:, h * t:(h + 1) * t]

    @pl.when(d == 1)
    def _():
        for h in range(S5_HG):
            o_ref[h] += y[:, h * t:(h + 1) * t]


def _s5_params(a_re, a_im, log_dt, b_re, b_im, c_re, c_im):
    a_re, a_im = a_re.astype(F32), a_im.astype(F32)
    dt = jnp.exp(log_dt.astype(F32))[..., None]
    mag = jnp.exp(a_re * dt)
    lam_re, lam_im = mag * jnp.cos(a_im * dt), mag * jnp.sin(a_im * dt)
    den = a_re * a_re + a_im * a_im
    num_re, num_im = lam_re - 1.0, lam_im
    fr = (num_re * a_re + num_im * a_im) / den
    fi = (num_im * a_re - num_re * a_im) / den
    b_re, b_im = b_re.astype(F32), b_im.astype(F32)
    bb_re = fr[..., None] * b_re - fi[..., None] * b_im
    bb_im = fr[..., None] * b_im + fi[..., None] * b_re
    c_re_t = jnp.swapaxes(c_re.astype(F32), -1, -2)
    c_im_t = jnp.swapaxes(c_im.astype(F32), -1, -2)
    col = jnp.concatenate([lam_re[..., None], lam_im[..., None], c_re_t, c_im_t], axis=-1)
    col = jnp.pad(col, ((0, 0),) * 3 + ((0, S5_PP - S5_P), (0, 128 - col.shape[-1])))
    row = jnp.concatenate([jnp.swapaxes(bb_re, -1, -2), jnp.swapaxes(bb_im, -1, -2),
                           lam_re[..., None, :], lam_im[..., None, :]], axis=-2)
    row = jnp.pad(row, ((0, 0),) * 3 + ((0, S5_ROWS - row.shape[-2]), (0, S5_PP - S5_P)))
    return col, row


def _s5_scan(p, col, row, layer, *, n_batch, seq, ctx_len):
    t = S5_T
    n = p.shape[0]
    nc = n // t
    n_ctx, n_lat = ctx_len // t, seq // t
    ut = p[:, COL_SU * BR:(COL_SU + 1) * BR].T.reshape(S5_G, S5_HG, nc, t)
    kern = functools.partial(_s5_kernel, n_batch=n_batch, n_ctx=n_ctx, n_lat=n_lat)
    yt = pl.pallas_call(
        kern,
        out_shape=jax.ShapeDtypeStruct((S5_G, S5_HG, nc, t), F32),
        grid=(S5_G, 2),
        in_specs=[pl.BlockSpec((None, S5_HG, nc, t), lambda g, d: (g, 0, 0, 0)),
                  pl.BlockSpec((None, None, None, S5_PP, 128), lambda g, d: (layer, d, g, 0, 0)),
                  pl.BlockSpec((None, None, None, S5_ROWS, S5_PP), lambda g, d: (layer, d, g, 0, 0))],
        out_specs=pl.BlockSpec((None, S5_HG, nc, t), lambda g, d: (g, 0, 0, 0)),
        scratch_shapes=[pltpu.VMEM((S5_HG * t, S5_HG * t), BF16),
                        pltpu.VMEM((S5_HG * t, 2 * S5_PP), BF16),
                        pltpu.VMEM((2 * S5_PP, S5_HG * t), BF16),
                        pltpu.VMEM((2 * S5_PP, S5_HG * t), F32),
                        pltpu.VMEM((nc, S5_PP), F32), pltpu.VMEM((nc, S5_PP), F32),
                        pltpu.VMEM((nc, S5_PP), F32), pltpu.VMEM((nc, S5_PP), F32)],
        compiler_params=_cparams(("parallel", "arbitrary")),
        name="s5_scan",
    )(ut, col, row)
    return yt.reshape(BR, n).T


def _head_rms(x, g):
    parts = []
    for h in range(NH):
        xh = x[:, h * HD:(h + 1) * HD]
        parts.append(xh * lax.rsqrt(jnp.mean(jnp.square(xh), axis=-1, keepdims=True) + NORM_EPS))
    return jnp.concatenate(parts, axis=-1) * g


def _finish_kernel(raf_ref, rab_ref, rbf_ref, rbb_ref, rc_ref, rdf_ref, rdb_ref, ga_ref, gb_ref, u_ref, gd_ref,
                   mg_ref, x_ref, mod_ref, vec_ref, glu_w_ref, wb_ref, wo_ref, o_ref):
    vec = vec_ref[...]
    oa = _head_rms(raf_ref[...] + rab_ref[...], vec[0:1]) * _silu(ga_ref[...].astype(F32))
    ob = _head_rms(rbf_ref[...] + rbb_ref[...], vec[1:2]) * _silu(gb_ref[...].astype(F32))
    od = _head_rms(rdf_ref[...] + rdb_ref[...], vec[2:3]) * _silu(gd_ref[...].astype(F32))
    yc = rc_ref[...] + vec[3:4] * u_ref[...].astype(F32)
    yc = 0.5 * yc * (1.0 + lax.erf(yc * (2.0 ** -0.5)))
    oc = yc * _sigmoid(_dot(yc.astype(BF16), glu_w_ref[...]) + vec[4:5])
    y = None
    for j, o in enumerate((oa, ob, oc, od)):
        gate = _sigmoid_tanh(mg_ref[:, j * D_MODEL:(j + 1) * D_MODEL].astype(F32))
        term = gate * _dot(o.astype(BF16), wb_ref[j])
        y = term if y is None else y + term
    mix = _dot(y.astype(BF16), wo_ref[...])
    o_ref[...] = x_ref[...] + mod_ref[2:3, :] * mix


def _finish(raw_a, raw_b, raw_c, raw_d, p, x_all, mod, vec, glu_w, w_branch, w_out, layer, *, n_rows, n_batch,
            seq, tm=256):
    mrow = _mod_row_map(n_rows, tm, n_batch, seq)
    one = pl.Buffered(1)

    def pcol(cblock):
        return pl.BlockSpec((tm, BR), lambda i: (i, cblock))

    raw = pl.BlockSpec((tm, BR), lambda i: (i, 0))
    return pl.pallas_call(
        _finish_kernel,
        out_shape=jax.ShapeDtypeStruct((n_rows, D_MODEL), F32),
        grid=(n_rows // tm,),
        in_specs=[raw, raw, raw, raw, raw, raw, raw,
                  pcol(COL_HG), pcol(COL_RG), pcol(COL_SU), pcol(COL_MZ),
                  pl.BlockSpec((tm, N_BRANCH * D_MODEL), lambda i: (i, COL_MERGE)),
                  pl.BlockSpec((tm, D_MODEL), lambda i: (i, 0)),
                  pl.BlockSpec((None, 8, D_MODEL), lambda i: (mrow(i), 0, 0)),
                  pl.BlockSpec((None, 8, BR), lambda i: (layer, 0, 0), pipeline_mode=one),
                  pl.BlockSpec((None, BR, BR), lambda i: (layer, 0, 0), pipeline_mode=one),
                  pl.BlockSpec((None, N_BRANCH, BR, D_MODEL), lambda i: (layer, 0, 0, 0), pipeline_mode=one),
                  pl.BlockSpec((None, D_MODEL, D_MODEL), lambda i: (layer, 0, 0), pipeline_mode=one)],
        out_specs=pl.BlockSpec((tm, D_MODEL), lambda i: (i, 0)),
        compiler_params=_cparams(("parallel",)),
        name="finish",
    )(*raw_a, *raw_b, raw_c, *raw_d, p, p, p, p, p, x_all, mod, vec, glu_w, w_branch, w_out)


def _mlp_kernel(x_ref, mod_ref, g_ref, w1_ref, w2_ref, fin_ref, o_ref, xn_ref, acc_ref, *, final_norm):
    j = pl.program_id(1)

    @pl.when(j == 0)
    def _():
        h = _norm_mod(x_ref[...], g_ref[...], mod_ref[3:4, :], mod_ref[4:5, :])
        xn_ref[...] = h.astype(BF16)
        acc_ref[...] = jnp.zeros_like(acc_ref)

    a = jnp.square(jnp.maximum(_dot(xn_ref[...], w1_ref[...]), 0.0))
    acc_ref[...] += _dot(a.astype(BF16), w2_ref[...])

    @pl.when(j == pl.num_programs(1) - 1)
    def _():
        y = x_ref[...] + mod_ref[5:6, :] * acc_ref[...]
        if final_norm:
            y = (y * lax.rsqrt(jnp.mean(jnp.square(y), axis=-1, keepdims=True) + NORM_EPS)) * fin_ref[...]
        o_ref[...] = y


def _mlp(x_all, mod, norm_w, w1, w2, fin_w, layer, *, n_rows, n_batch, seq, final_norm, tm=512, tf=1024):
    mrow = _mod_row_map(n_rows, tm, n_batch, seq)
    kern = functools.partial(_mlp_kernel, final_norm=final_norm)
    return pl.pallas_call(
        kern,
        out_shape=jax.ShapeDtypeStruct((n_rows, D_MODEL), F32),
        grid=(n_rows // tm, D_FF // tf),
        in_specs=[pl.BlockSpec((tm, D_MODEL), lambda i, j: (i, 0)),
                  pl.BlockSpec((None, 8, D_MODEL), lambda i, j: (mrow(i), 0, 0)),
                  pl.BlockSpec((None, 1, D_MODEL), lambda i, j: (layer, 0, 0)),
                  pl.BlockSpec((None, D_MODEL, tf), lambda i, j: (layer, 0, j)),
                  pl.BlockSpec((None, tf, D_MODEL), lambda i, j: (layer, j, 0)),
                  pl.BlockSpec((1, D_MODEL), lambda i, j: (0, 0))],
        out_specs=pl.BlockSpec((tm, D_MODEL), lambda i, j: (i, 0)),
        scratch_shapes=[pltpu.VMEM((tm, D_MODEL), BF16), pltpu.VMEM((tm, D_MODEL), F32)],
        compiler_params=_cparams(("parallel", "arbitrary")),
        name="mlp",
    )(x_all, mod, norm_w, w1, w2, fin_w.reshape(1, D_MODEL))


W_PREP_ROWS = 1024
N_MERGE_BLK = N_BRANCH * D_MODEL // W_PREP_ROWS
MERGE_OFF = GATE_OFF + 4 * NH


def _w_prep_kernel(a_ref, g_ref, main_ref, gate_ref):
    @pl.when(pl.program_id(1) == 0)
    def _():
        gate_ref[...] = jnp.zeros_like(gate_ref)
        gate_ref[0:4 * NH, :] = g_ref[0].astype(BF16)

    main_ref[...] = a_ref[0].astype(BF16)


def _split_w_in(w):
    depth = w.shape[0]
    wt = jnp.swapaxes(w, 1, 2)
    tr = W_PREP_ROWS

    def a_map(l, j):
        row = jnp.where(j < N_MERGE_BLK, MERGE_OFF + j * tr, (j - N_MERGE_BLK) * tr)
        return l, pl.multiple_of(row, 16), 0

    return pl.pallas_call(
        _w_prep_kernel,
        out_shape=(jax.ShapeDtypeStruct((depth, P_WIDTH, D_MODEL), BF16),
                   jax.ShapeDtypeStruct((depth, GATE_PAD, D_MODEL), BF16)),
        grid=(depth, P_WIDTH // tr),
        in_specs=[pl.BlockSpec((pl.Element(1), pl.Element(tr), pl.Element(D_MODEL)), a_map),
                  pl.BlockSpec((pl.Element(1), pl.Element(4 * NH), pl.Element(D_MODEL)),
                               lambda l, j: (l, GATE_OFF, 0))],
        out_specs=(pl.BlockSpec((None, tr, D_MODEL), lambda l, j: (l, j, 0)),
                   pl.BlockSpec((None, GATE_PAD, D_MODEL), lambda l, j: (l, 0, 0))),
        compiler_params=_cparams(("parallel", "arbitrary")),
        name="w_in_prep",
    )(wt, wt)


def kernel(x, c, ctx, c_ctx, w_mod, b_mod, norm_mix, norm_mlp, w_in, hgrn_lb_logits, hgrn_norm, ret_decay, ret_norm, s5_a_re, s5_a_im, s5_log_dt, s5_b_re, s5_b_im, s5_c_re, s5_c_im, s5_d, s5_glu_w, s5_glu_b, mlstm_conv_w, mlstm_conv_b, mlstm_gate_b, mlstm_norm, w_branch, w_out, w_ff1, w_ff2, final_norm):
    n_batch, seq, _ = x.shape
    ctx_len = ctx.shape[1]
    depth = w_in.shape[0]
    nl = n_batch * seq
    dims = dict(n_batch=n_batch, seq=seq)

    p_lb = jax.nn.softmax(hgrn_lb_logits.astype(F32), axis=0)
    lower_bounds = (jnp.cumsum(p_lb, axis=0) - p_lb[0]).reshape(depth, 2, 1, BR)
    lg = jnp.log1p(-jnp.exp(ret_decay.astype(F32)))
    gate_b = jnp.pad(mlstm_gate_b.reshape(depth, 1, 4 * NH).astype(F32), ((0, 0), (0, 0), (0, GATE_PAD - 4 * NH)))
    vec = jnp.stack([hgrn_norm, ret_norm, mlstm_norm, s5_d, s5_glu_b], axis=1).astype(F32)
    vec = jnp.pad(vec, ((0, 0), (0, 8 - vec.shape[1]), (0, 0)))
    w_main, w_gate = _split_w_in(w_in)
    glu_w, wb, wo = s5_glu_w.astype(BF16), w_branch.astype(BF16), w_out.astype(BF16)
    w1, w2 = w_ff1.astype(BF16), w_ff2.astype(BF16)
    norm_mix3 = norm_mix.reshape(depth, 1, D_MODEL)
    norm_mlp3 = norm_mlp.reshape(depth, 1, D_MODEL)
    conv_w = mlstm_conv_w.reshape(depth, 9, 2 * BR)
    conv_b = mlstm_conv_b.reshape(depth, 1, 2 * BR)
    s5_col, s5_row = _s5_params(s5_a_re, s5_a_im, s5_log_dt, s5_b_re, s5_b_im, s5_c_re, s5_c_im)

    c_pad = jnp.zeros((8, D_MODEL), F32).at[:n_batch].set(c).at[n_batch].set(c_ctx)
    x_all = jnp.concatenate([x.reshape(nl, D_MODEL), ctx.reshape(n_batch * ctx_len, D_MODEL)], axis=0)

    for l in range(depth):
        last = l == depth - 1
        mod = _modulation(c_pad, w_mod, b_mod, l)
        mod = jnp.pad(mod[:n_batch + 1].reshape(n_batch + 1, N_MOD, D_MODEL), ((0, 0), (0, 8 - N_MOD), (0, 0)))
        p, gates = _in_proj(x_all, mod, norm_mix3, w_main, w_gate, l, **dims)
        qk = _mlstm_conv(p, conv_w, conv_b, l, ctx_len=ctx_len, **dims)
        raw_a, raw_b, raw_d = _scans(p, gates, qk, lg[l], lower_bounds, gate_b, l, ctx_len=ctx_len, **dims)
        raw_c = _s5_scan(p, s5_col, s5_row, l, ctx_len=ctx_len, **dims)
        n_rows = nl if last else x_all.shape[0]
        x_mid = _finish(raw_a, raw_b, raw_c, raw_d, p, x_all, mod, vec, glu_w, wb, wo, l, n_rows=n_rows, **dims)
        x_all = _mlp(x_mid, mod, norm_mlp3, w1, w2, final_norm, l, n_rows=n_rows, final_norm=last, **dims)
    return x_all.reshape(n_batch, seq, D_MODEL)
```

```python
import functools
import math

import jax
import jax.numpy as jnp
from jax import lax
from jax.experimental import pallas as pl
from jax.experimental.pallas import tpu as pltpu

F32 = jnp.float32
BF16 = jnp.bfloat16

D_MODEL = 2048
N_BRANCH = 4
BR = D_MODEL // N_BRANCH
HD = 128
NH = BR // HD
S5_HG = 16
S5_G = BR // S5_HG
S5_P = 64
D_FF = 4 * D_MODEL
N_MOD = 6
GRID_W = 64
NORM_EPS = 1e-6
NEG_BIG = -1e30
F_TINY = 1e-30
S5_DT_MIN = 1e-3

COL_MERGE = 0
COL_HQ, COL_HF, COL_HV, COL_HG = 16, 17, 19, 20
COL_RQ, COL_RK, COL_RV, COL_RG = 21, 22, 23, 24
COL_SU = 25
COL_MQK, COL_MV, COL_MZ = 26, 28, 29
P_WIDTH = 30 * BR
GATE_OFF = 14 * BR
GATE_PAD = 128

T_HGRN = 128
T_ATTN = 256
S5_T = 128
S5_PP = 128
S5_ROWS = 40
VMEM_LIMIT = 56 * 1024 * 1024

_HI = lax.Precision.HIGHEST


def _cparams(sem):
    return pltpu.CompilerParams(dimension_semantics=sem, vmem_limit_bytes=VMEM_LIMIT)


def _dot(a, b):
    return jnp.dot(a, b, preferred_element_type=F32)


def _dot_nt(a, b):
    return lax.dot_general(a, b, (((1,), (1,)), ((), ())), preferred_element_type=F32)


def _dot_tn(a, b):
    return lax.dot_general(a, b, (((0,), (0,)), ((), ())), preferred_element_type=F32)


def _split3(x):
    hi = x.astype(BF16)
    r1 = x - hi.astype(F32)
    mid = r1.astype(BF16)
    return hi, mid, (r1 - mid.astype(F32)).astype(BF16)


def _cumsum_rows(tri, x):
    hi, mid, lo = _split3(x)
    return _dot(tri, hi) + (_dot(tri, mid) + _dot(tri, lo))


def _sigmoid(x):
    return 1.0 / (1.0 + jnp.exp(-x))


def _sigmoid_tanh(x):
    return 0.5 * jnp.tanh(0.5 * x) + 0.5


def _silu(x):
    return x * _sigmoid(x)


def _mod_kernel(c_ref, w_ref, b_ref, o_ref):
    @pl.when(pl.program_id(0) == 0)
    def _():
        o_ref[...] = jnp.broadcast_to(b_ref[...], o_ref.shape)

    a = _silu(c_ref[...]).astype(BF16)
    o_ref[...] += _dot(a, w_ref[...].astype(BF16))


def _modulation(c_pad, w_mod, b_mod, layer, tk=256):
    n = w_mod.shape[2]
    return pl.pallas_call(
        _mod_kernel,
        out_shape=jax.ShapeDtypeStruct((8, n), F32),
        grid=(D_MODEL // tk,),
        in_specs=[pl.BlockSpec((8, tk), lambda k: (0, k)),
                  pl.BlockSpec((None, tk, n), lambda k: (layer, k, 0)),
                  pl.BlockSpec((None, 1, n), lambda k: (layer, 0, 0))],
        out_specs=pl.BlockSpec((8, n), lambda k: (0, 0)),
        compiler_params=_cparams(("arbitrary",)),
        name="modulation",
    )(c_pad, w_mod, b_mod.reshape(b_mod.shape[0], 1, n))


def _norm_mod(x, g, shift, scale):
    y = x * lax.rsqrt(jnp.mean(jnp.square(x), axis=-1, keepdims=True) + NORM_EPS)
    return (y * g) * (1.0 + scale) + shift


def _inproj_kernel(x_ref, mod_ref, g_ref, w_ref, wg_ref, p_ref, gate_ref, xn_ref):
    @pl.when(pl.program_id(1) == 0)
    def _():
        h = _norm_mod(x_ref[...], g_ref[...], mod_ref[0:1, :], mod_ref[1:2, :])
        xn_ref[...] = h.astype(BF16)
        gate_ref[...] = _dot_nt(xn_ref[...], wg_ref[...])

    p_ref[...] = _dot_nt(xn_ref[...], w_ref[...]).astype(p_ref.dtype)


def _mod_row_map(n_rows, tm, n_batch, seq):
    assert n_rows % tm == 0 and seq % tm == 0, (n_rows, seq, tm)
    n_lat_tiles, tiles_per_batch = n_batch * seq // tm, seq // tm

    def row(i):
        return jnp.where(i < n_lat_tiles, i // tiles_per_batch, n_batch)
    return row


def _in_proj(x_all, mod, norm_w, w_main, w_gate, layer, *, n_batch, seq, tm=1024, tn=1024):
    n = x_all.shape[0]
    mrow = _mod_row_map(n, tm, n_batch, seq)
    return pl.pallas_call(
        _inproj_kernel,
        out_shape=(jax.ShapeDtypeStruct((n, P_WIDTH), BF16),
                   jax.ShapeDtypeStruct((n, GATE_PAD), F32)),
        grid=(n // tm, P_WIDTH // tn),
        in_specs=[pl.BlockSpec((tm, D_MODEL), lambda i, j: (i, 0)),
                  pl.BlockSpec((None, 8, D_MODEL), lambda i, j: (mrow(i), 0, 0)),
                  pl.BlockSpec((None, 1, D_MODEL), lambda i, j: (layer, 0, 0)),
                  pl.BlockSpec((None, tn, D_MODEL), lambda i, j: (layer, j, 0)),
                  pl.BlockSpec((None, GATE_PAD, D_MODEL), lambda i, j: (layer, 0, 0))],
        out_specs=(pl.BlockSpec((tm, tn), lambda i, j: (i, j)),
                   pl.BlockSpec((tm, GATE_PAD), lambda i, j: (i, 0))),
        scratch_shapes=[pltpu.VMEM((tm, D_MODEL), BF16)],
        compiler_params=_cparams(("parallel", "arbitrary")),
        name="in_proj",
    )(x_all, mod, norm_w, w_main, w_gate)


CONV_ROWS = 256


def _conv_kernel(up_ref, x_ref, dn_ref, w_ref, b_ref, o_ref, *, n_lat_blocks, blocks_per_img):
    rb = pl.program_id(0)
    x = x_ref[...].astype(F32)
    w = w_ref[...]
    n = CONV_ROWS

    @pl.when(rb < n_lat_blocks)
    def _():
        rr = rb % blocks_per_img
        up = jnp.where(rr > 0, up_ref[...].astype(F32), 0.0)
        dn = jnp.where(rr < blocks_per_img - 1, dn_ref[...].astype(F32), 0.0)
        ext = jnp.concatenate([up, x, dn], axis=0)
        ne = n + 2 * GRID_W
        col = lax.broadcasted_iota(jnp.int32, (ne, 1), 0) % GRID_W
        xl = jnp.where(col >= 1, pltpu.roll(ext, 1, 0), 0.0)
        xr = jnp.where(col <= GRID_W - 2, pltpu.roll(ext, ne - 1, 0), 0.0)
        acc = None
        for i in range(3):
            y = w[3 * i:3 * i + 1] * xl + w[3 * i + 1:3 * i + 2] * ext + w[3 * i + 2:3 * i + 3] * xr
            part = y[i * GRID_W:i * GRID_W + n]
            acc = part if acc is None else acc + part
        o_ref[...] = _silu(acc + b_ref[...]).astype(o_ref.dtype)

    @pl.when(rb >= n_lat_blocks)
    def _():
        t = lax.broadcasted_iota(jnp.int32, (n, 1), 0)
        xl = jnp.where(t >= 1, pltpu.roll(x, 1, 0), 0.0)
        xr = jnp.where(t <= n - 2, pltpu.roll(x, n - 1, 0), 0.0)
        acc = w[3:4] * xl + w[4:5] * x + w[5:6] * xr
        o_ref[...] = _silu(acc + b_ref[...]).astype(o_ref.dtype)


def _mlstm_conv(p, conv_w, conv_b, layer, *, n_batch, seq, ctx_len, cb=2 * BR):
    assert ctx_len == CONV_ROWS and seq % CONV_ROWS == 0
    n = p.shape[0]
    n_blocks = n // CONV_ROWS
    hb = CONV_ROWS // GRID_W
    n_halo = n // GRID_W
    c0 = COL_MQK * BR // cb
    kern = functools.partial(_conv_kernel, n_lat_blocks=n_batch * seq // CONV_ROWS,
                             blocks_per_img=seq // CONV_ROWS)
    return pl.pallas_call(
        kern,
        out_shape=jax.ShapeDtypeStruct((n, 2 * BR), BF16),
        grid=(n_blocks, 2 * BR // cb),
        in_specs=[pl.BlockSpec((GRID_W, cb), lambda r, c: (jnp.maximum(r * hb - 1, 0), c0 + c)),
                  pl.BlockSpec((CONV_ROWS, cb), lambda r, c: (r, c0 + c)),
                  pl.BlockSpec((GRID_W, cb), lambda r, c: (jnp.minimum((r + 1) * hb, n_halo - 1), c0 + c)),
                  pl.BlockSpec((None, 9, cb), lambda r, c: (layer, 0, c)),
                  pl.BlockSpec((None, 1, cb), lambda r, c: (layer, 0, c))],
        out_specs=pl.BlockSpec((CONV_ROWS, cb), lambda r, c: (r, c)),
        compiler_params=_cparams(("parallel", "parallel")),
        name="mlstm_conv",
    )(p, p, p, conv_w, conv_b)


def _chunk_block(b, d, i, *, n_ctx, n_lat, n_batch):
    ctx_j = i if d == 0 else n_ctx - 1 - i
    lat_j = i - n_ctx if d == 0 else n_ctx + n_lat - 1 - i
    return jnp.where(i < n_ctx, n_batch * n_lat + b * n_ctx + ctx_j, b * n_lat + lat_j)


def _order_mask(d, t):
    r = lax.broadcasted_iota(jnp.int32, (t, t), 0)
    c = lax.broadcasted_iota(jnp.int32, (t, t), 1)
    return (c <= r) if d == 0 else (c >= r)


def _init_at_first_step(*refs):
    @pl.when(pl.program_id(1) == 0)
    def _():
        for ref in refs:
            ref[...] = jnp.zeros_like(ref)


def _ret_direction(d, lg_ref, q_ref, k_ref, v_ref, o_ref, s_ref):
    t = T_ATTN
    r = lax.broadcasted_iota(jnp.int32, (t, t), 0)
    c = lax.broadcasted_iota(jnp.int32, (t, t), 1)
    rel = (r - c) if d == 0 else (c - r)
    relf = jnp.maximum(rel, 0).astype(F32)
    tt = lax.broadcasted_iota(jnp.int32, (t, 1), 0)
    pos = (tt if d == 0 else t - 1 - tt).astype(F32)
    scale = HD ** -0.5
    for h in range(NH):
        sl = slice(h * HD, (h + 1) * HD)
        lg = lg_ref[d, h]
        q = q_ref[:, sl]
        k = k_ref[:, sl]
        v = v_ref[:, sl]
        intra = jnp.where(rel >= 0, jnp.exp(lg * relf), 0.0)
        scores = (_dot_nt(q, k) * scale) * intra
        s_old = s_ref[d, h]
        out = _dot(scores.astype(BF16), v) + jnp.exp(lg * (pos + 1.0)) * _dot(q, s_old.astype(BF16))
        kd = (k.astype(F32) * (scale * jnp.exp(lg * (t - 1.0 - pos)))).astype(BF16)
        s_ref[d, h] = jnp.exp(lg * t) * s_old + _dot_tn(kd, v)
        o_ref[:, sl] = out


def _ret_kernel(lg_ref, qf_ref, kf_ref, vf_ref, qb_ref, kb_ref, vb_ref, of_ref, ob_ref, s_ref):
    _init_at_first_step(s_ref)
    _ret_direction(0, lg_ref, qf_ref, kf_ref, vf_ref, of_ref, s_ref)
    _ret_direction(1, lg_ref, qb_ref, kb_ref, vb_ref, ob_ref, s_ref)


def _hgrn_direction(d, lb_ref, q_ref, z_ref, v_ref, o_ref, st_ref):
    t = T_HGRN
    fwd = d == 0
    lb = lb_ref[...]
    z = z_ref[...].astype(F32)
    sig = _sigmoid(z)
    f = lb + (1.0 - lb) * sig
    logf = jnp.log(jnp.maximum(f, F_TINY))
    kk = (1.0 - lb) * (1.0 - sig)
    qq = _silu(q_ref[...].astype(F32))
    r2 = lax.broadcasted_iota(jnp.int32, (t, t), 0)
    c2 = lax.broadcasted_iota(jnp.int32, (t, t), 1)
    tri = jnp.where((c2 <= r2) if fwd else (c2 >= r2), 1.0, 0.0).astype(BF16)
    cum = _cumsum_rows(tri, logf)
    xdiff = jnp.where((c2 < r2) if fwd else (c2 > r2), r2 ^ c2, 0)
    row = lax.broadcasted_iota(jnp.int32, (t, 1), 0)
    if fwd:
        e_ref = jnp.where(row == 0, 0.0, pltpu.roll(cum, 1, 0))
    else:
        e_ref = jnp.where(row == t - 1, 0.0, pltpu.roll(cum, t - 1, 0))
    f_ref = cum
    levels = []
    w = 1
    while w < t:
        q_l = (qq * jnp.exp(cum - e_ref)).astype(BF16)
        k_l = (kk * jnp.exp(f_ref - cum)).astype(BF16)
        levels.append((q_l, k_l, (xdiff >> (w.bit_length() - 1)) == 1))
        upper = (row & w) != 0
        if fwd:
            e_ref = jnp.where(upper, pltpu.roll(e_ref, w, 0), e_ref)
            f_ref = jnp.where(upper, f_ref, pltpu.roll(f_ref, t - w, 0))
        else:
            e_ref = jnp.where(upper, e_ref, pltpu.roll(e_ref, t - w, 0))
            f_ref = jnp.where(upper, pltpu.roll(f_ref, w, 0), f_ref)
        w *= 2
    q_b = qq.astype(BF16)
    k_b = kk.astype(BF16)
    tail = cum[t - 1:t] if fwd else cum[0:1]
    q_state = (qq * jnp.exp(cum)).astype(BF16)
    k_state = (kk * jnp.exp(tail - cum)).astype(BF16)
    decay = jnp.exp(tail)
    for h in range(NH):
        sl = slice(h * HD, (h + 1) * HD)
        v_h = v_ref[:, sl]
        scores = jnp.where(r2 == c2, _dot_nt(q_b[:, sl], k_b[:, sl]), 0.0)
        for q_l, k_l, m in levels:
            scores = jnp.where(m, _dot_nt(q_l[:, sl], k_l[:, sl]), scores)
        st = st_ref[d, h]
        o_ref[:, sl] = _dot(scores.astype(BF16), v_h) + _dot_nt(q_state[:, sl], st.astype(BF16))
        st_ref[d, h] = st * decay[:, sl] + _dot_tn(v_h, k_state[:, sl])


def _hgrn_kernel(lbf_ref, qf_ref, zf_ref, vf_ref, lbb_ref, qb_ref, zb_ref, vb_ref, of_ref, ob_ref, st_ref):
    _init_at_first_step(st_ref)
    _hgrn_direction(0, lbf_ref, qf_ref, zf_ref, vf_ref, of_ref, st_ref)
    _hgrn_direction(1, lbb_ref, qb_ref, zb_ref, vb_ref, ob_ref, st_ref)


def _log_sigmoid(x):
    return jnp.minimum(x, 0.0) - jnp.log(1.0 + jnp.exp(-jnp.abs(x)))


def _rep_lane(parts, lane, width):
    sel = jnp.where(lax.broadcasted_iota(jnp.int32, (GATE_PAD, width), 0) == lane, 1.0, 0.0).astype(BF16)
    hi, mid, lo = parts
    return _dot(hi, sel) + (_dot(mid, sel) + _dot(lo, sel))


def _mlstm_direction(d, qk_ref, v_ref, g_ref, gb_ref, o_ref, c_ref, m_ref):
    t = T_ATTN
    assert t == 2 * HD
    fwd = d == 0
    g = g_ref[...] + gb_ref[...]
    r2 = lax.broadcasted_iota(jnp.int32, (t, t), 0)
    c2 = lax.broadcasted_iota(jnp.int32, (t, t), 1)
    mask = (c2 <= r2) if fwd else (c2 >= r2)
    tri = jnp.where(mask, 1.0, 0.0).astype(BF16)
    eye = jnp.where(r2 == c2, 1.0, 0.0).astype(BF16)
    lf = _log_sigmoid(g)
    cum_c = _cumsum_rows(tri, lf)
    a_col = g - pltpu.roll(cum_c, GATE_PAD - NH, 1)
    row = lax.broadcasted_iota(jnp.int32, (t, 1), 0)
    pm = a_col
    k = 1
    while k < t:
        if fwd:
            pm = jnp.maximum(pm, jnp.where(row >= k, pltpu.roll(pm, k, 0), NEG_BIG))
        else:
            pm = jnp.maximum(pm, jnp.where(row < t - k, pltpu.roll(pm, t - k, 0), NEG_BIG))
        k *= 2
    a_parts, pm_parts, cum_parts = _split3(a_col), _split3(pm), _split3(cum_c)
    a_hi, a_mid, a_lo = a_parts
    a_rows = _dot_tn(a_hi, eye) + (_dot_tn(a_mid, eye) + _dot_tn(a_lo, eye))
    scale = HD ** -0.5
    last = t - 1 if fwd else 0
    ones = jnp.ones((t, HD), BF16)
    for h in range(NH):
        sl = slice(h * HD, (h + 1) * HD)
        li, lfw = 2 * NH * d + h, 2 * NH * d + NH + h
        m_old = m_ref[d, h]
        mx2 = jnp.maximum(_rep_lane(pm_parts, li, 2 * HD), jnp.concatenate([m_old, m_old], axis=1))
        mx = mx2[:, 0:HD]
        cc = _rep_lane(cum_parts, lfw, HD)
        a_rep = _rep_lane(a_parts, li, HD)
        a_row = a_rows[li:li + 1]
        q = qk_ref[:, sl]
        k_h = qk_ref[:, BR + h * HD:BR + (h + 1) * HD]
        v_aug = jnp.concatenate([v_ref[:, sl], ones], axis=1)
        w = jnp.where(mask, jnp.exp(a_row - mx2), 0.0)
        scores = (_dot_nt(q, k_h) * scale) * w
        w_state = jnp.exp(m_old - mx)
        c_old = c_ref[d, h]
        s_hi = scores.astype(BF16)
        s_mid = (scores - s_hi.astype(F32)).astype(BF16)
        intra = _dot(s_hi, v_aug)
        inter = _dot(q, c_old.astype(BF16))
        num = intra[:, 0:HD] + w_state * inter[:, 0:HD]
        den = (intra[:, HD:2 * HD] + _dot(s_mid, ones)) + w_state * inter[:, HD:2 * HD]
        o_ref[:, sl] = num / jnp.maximum(jnp.abs(den), jnp.exp(-(cc + mx)))
        total = cc[last:last + 1]
        m_new_rel = mx[last:last + 1]
        keep = jnp.exp(m_old - m_new_rel)
        w_end = jnp.exp(a_rep - m_new_rel)
        kw = (k_h.astype(F32) * (scale * w_end)).astype(BF16)
        c_ref[d, h] = jnp.concatenate([keep, keep], axis=1) * c_old + _dot_tn(kw, v_aug)
        m_ref[d, h] = total + m_new_rel


def _mlstm_kernel(qkf_ref, vf_ref, gf_ref, qkb_ref, vb_ref, gb_ref, bias_ref, of_ref, ob_ref, c_ref, m_ref):
    _init_at_first_step(c_ref, m_ref)
    _mlstm_direction(0, qkf_ref, vf_ref, gf_ref, bias_ref, of_ref, c_ref, m_ref)
    _mlstm_direction(1, qkb_ref, vb_ref, gb_ref, bias_ref, ob_ref, c_ref, m_ref)


def _scans(p, gates, qk, lg, lb, gate_b, layer, *, n_batch, seq, ctx_len):
    n = p.shape[0]
    state = pltpu.VMEM((2, NH, HD, HD), F32)
    vec = pltpu.VMEM((2, NH, 1, HD), F32)

    def call(kernel, t, inputs, make_specs, scratch, name):
        n_ctx, n_lat = ctx_len // t, seq // t

        def rows(d, width, cblock=0):
            return pl.BlockSpec((t, width), lambda b, i: (_chunk_block(b, d, i, n_ctx=n_ctx, n_lat=n_lat,
                                                                        n_batch=n_batch), cblock))

        out = jax.ShapeDtypeStruct((n, BR), F32)
        return pl.pallas_call(
            kernel,
            out_shape=(out, out),
            grid=(n_batch, n_ctx + n_lat),
            in_specs=make_specs(rows),
            out_specs=(rows(0, BR), rows(1, BR)),
            scratch_shapes=scratch,
            compiler_params=_cparams(("parallel", "arbitrary")),
            name=name,
        )(*inputs)

    raw_b = call(_ret_kernel, T_ATTN, (lg, p, p, p, p, p, p),
                 lambda rows: [pl.BlockSpec(memory_space=pltpu.SMEM)]
                 + [rows(d, BR, c) for d in (0, 1) for c in (COL_RQ, COL_RK, COL_RV)],
                 [state], "retention_scan")
    lb_spec = lambda d: pl.BlockSpec((None, None, 1, BR), lambda b, i: (layer, d, 0, 0))
    raw_a = call(_hgrn_kernel, T_HGRN, (lb, p, p, p, lb, p, p, p),
                 lambda rows: [spec for d in (0, 1)
                               for spec in (lb_spec(d), rows(d, BR, COL_HQ), rows(d, BR, COL_HF + d),
                                            rows(d, BR, COL_HV))],
                 [state], "hgrn2_scan")
    raw_d = call(_mlstm_kernel, T_ATTN, (qk, p, gates, qk, p, gates, gate_b),
                 lambda rows: [spec for d in (0, 1)
                               for spec in (rows(d, 2 * BR), rows(d, BR, COL_MV), rows(d, GATE_PAD))]
                 + [pl.BlockSpec((None, 1, GATE_PAD), lambda b, i: (layer, 0, 0))],
                 [pltpu.VMEM((2, NH, HD, 2 * HD), F32), vec], "mlstm_scan")
    return raw_a, raw_b, raw_d


def _cmul(ar, ai, br, bi):
    return ar * br - ai * bi, ar * bi + ai * br


def _pow_table(lam_r, lam_i, expo):
    pr = jnp.ones(expo.shape, F32)
    pi = jnp.zeros(expo.shape, F32)
    ar = jnp.broadcast_to(lam_r, expo.shape)
    ai = jnp.broadcast_to(lam_i, expo.shape)
    for k in range(S5_T.bit_length() - 1):
        nr, ni = _cmul(pr, pi, ar, ai)
        bit = (expo & (1 << k)) != 0
        pr, pi = jnp.where(bit, nr, pr), jnp.where(bit, ni, pi)
        ar, ai = _cmul(ar, ai, ar, ai)
    return pr, pi


def _s5_kernel(u_ref, col_ref, row_ref, o_ref, m_ref, w_ref, ys_ref, z_ref, wr_ref, wi_ref, xr_ref, xi_ref, *,
               n_batch, n_ctx, n_lat):
    d = pl.program_id(1)
    t = S5_T
    col = col_ref[...]
    rowp = row_ref[...]
    lam_rc, lam_ic = col[:, 0:1], col[:, 1:2]
    bbr, bbi = rowp[0:S5_HG], rowp[S5_HG:2 * S5_HG]
    lam_rr, lam_ir = rowp[2 * S5_HG:2 * S5_HG + 1], rowp[2 * S5_HG + 1:2 * S5_HG + 2]
    lane = lax.broadcasted_iota(jnp.int32, (t, t), 1)
    sub = lax.broadcasted_iota(jnp.int32, (t, t), 0)

    pr, pi = _pow_table(lam_rc, lam_ic, jnp.where(d == 0, lane, t - 1 - lane))
    qr, qi = _cmul(pr, pi, lam_rc, lam_ic)
    for h in range(S5_HG):
        cr, ci = col[:, 2 + h:3 + h], col[:, 2 + S5_HG + h:3 + S5_HG + h]
        hs = slice(h * t, (h + 1) * t)
        z_ref[0:S5_PP, hs] = cr * pr - ci * pi
        z_ref[S5_PP:2 * S5_PP, hs] = -(cr * pi + ci * pr)
        ys_ref[0:S5_PP, hs] = (cr * qr - ci * qi).astype(BF16)
        ys_ref[S5_PP:2 * S5_PP, hs] = (-(cr * qi + ci * qr)).astype(BF16)
    b_hi, b_mid, _ = _split3(jnp.concatenate([bbr, bbi], axis=1))
    z_hi, z_mid, _ = _split3(z_ref[...])
    krow = _dot(b_hi, z_hi) + (_dot(b_hi, z_mid) + _dot(b_mid, z_hi))

    def build_m(shift, keep):
        for hp in range(S5_HG):
            for h in range(S5_HG):
                tile = jnp.broadcast_to(krow[hp:hp + 1, h * t:(h + 1) * t], (t, t))
                tile = pltpu.roll(tile, shift, 1, stride=1, stride_axis=0)
                m_ref[hp * t:(hp + 1) * t, h * t:(h + 1) * t] = jnp.where(keep, tile, 0.0).astype(BF16)

    @pl.when(d == 0)
    def _():
        build_m(0, lane >= sub)

    @pl.when(d == 1)
    def _():
        build_m(1, lane <= sub)

    tr, ti = _pow_table(lam_rr, lam_ir, jnp.where(d == 0, t - 1 - sub, sub))
    for hp in range(S5_HG):
        br, bi = bbr[hp:hp + 1], bbi[hp:hp + 1]
        w_ref[hp * t:(hp + 1) * t, 0:S5_PP] = (tr * br - ti * bi).astype(BF16)
        w_ref[hp * t:(hp + 1) * t, S5_PP:2 * S5_PP] = (tr * bi + ti * br).astype(BF16)

    u = jnp.concatenate([u_ref[hp] for hp in range(S5_HG)], axis=1)
    wv = _dot(u, w_ref[...])
    wr_ref[...] = wv[:, 0:S5_PP]
    wi_ref[...] = wv[:, S5_PP:2 * S5_PP]
    lr, li = lam_rr, lam_ir
    for _ in range(t.bit_length() - 1):
        lr, li = _cmul(lr, li, lr, li)

    def make_step(base, stride, count):
        def step(i, carry):
            xr, xi = carry
            j = jnp.where(d == 0, i, count - 1 - i)
            idx = pl.ds(base + j, n_batch, stride=stride)
            xr_ref[idx, :] = xr
            xi_ref[idx, :] = xi
            nr, ni = _cmul(xr, xi, lr, li)
            return nr + wr_ref[idx, :], ni + wi_ref[idx, :]
        return step

    zero = jnp.zeros((n_batch, S5_PP), F32)
    carry = lax.fori_loop(0, n_ctx, make_step(n_batch * n_lat, n_ctx, n_ctx), (zero, zero))
    lax.fori_loop(0, n_lat, make_step(0, n_lat, n_lat), carry)
    x_prev = jnp.concatenate([xr_ref[...], xi_ref[...]], axis=1).astype(BF16)
    y = _dot(u, m_ref[...]) + _dot(x_prev, ys_ref[...])

    @pl.when(d == 0)
    def _():
        for h in range(S5_HG):
            o_ref[h] = y[:, h * t:(h + 1) * t]

    @pl.when(d == 1)
    def _():
        for h in range(S5_HG):
            o_ref[h] += y[:, h * t:(h + 1) * t]


def _s5_params(a_re, a_im, log_dt, b_re, b_im, c_re, c_im):
    a_re, a_im = a_re.astype(F32), a_im.astype(F32)
    dt = jnp.exp(log_dt.astype(F32))[..., None]
    mag = jnp.exp(a_re * dt)
    lam_re, lam_im = mag * jnp.cos(a_im * dt), mag * jnp.sin(a_im * dt)
    den = a_re * a_re + a_im * a_im
    num_re, num_im = lam_re - 1.0, lam_im
    fr = (num_re * a_re + num_im * a_im) / den
    fi = (num_im * a_re - num_re * a_im) / den
    b_re, b_im = b_re.astype(F32), b_im.astype(F32)
    bb_re = fr[..., None] * b_re - fi[..., None] * b_im
    bb_im = fr[..., None] * b_im + fi[..., None] * b_re
    c_re_t = jnp.swapaxes(c_re.astype(F32), -1, -2)
    c_im_t = jnp.swapaxes(c_im.astype(F32), -1, -2)
    col = jnp.concatenate([lam_re[..., None], lam_im[..., None], c_re_t, c_im_t], axis=-1)
    col = jnp.pad(col, ((0, 0),) * 3 + ((0, S5_PP - S5_P), (0, 128 - col.shape[-1])))
    row = jnp.concatenate([jnp.swapaxes(bb_re, -1, -2), jnp.swapaxes(bb_im, -1, -2),
                           lam_re[..., None, :], lam_im[..., None, :]], axis=-2)
    row = jnp.pad(row, ((0, 0),) * 3 + ((0, S5_ROWS - row.shape[-2]), (0, S5_PP - S5_P)))
    return col, row


def _s5_scan(p, col, row, layer, *, n_batch, seq, ctx_len):
    t = S5_T
    n = p.shape[0]
    nc = n // t
    n_ctx, n_lat = ctx_len // t, seq // t
    ut = p[:, COL_SU * BR:(COL_SU + 1) * BR].T.reshape(S5_G, S5_HG, nc, t)
    kern = functools.partial(_s5_kernel, n_batch=n_batch, n_ctx=n_ctx, n_lat=n_lat)
    yt = pl.pallas_call(
        kern,
        out_shape=jax.ShapeDtypeStruct((S5_G, S5_HG, nc, t), F32),
        grid=(S5_G, 2),
        in_specs=[pl.BlockSpec((None, S5_HG, nc, t), lambda g, d: (g, 0, 0, 0)),
                  pl.BlockSpec((None, None, None, S5_PP, 128), lambda g, d: (layer, d, g, 0, 0)),
                  pl.BlockSpec((None, None, None, S5_ROWS, S5_PP), lambda g, d: (layer, d, g, 0, 0))],
        out_specs=pl.BlockSpec((None, S5_HG, nc, t), lambda g, d: (g, 0, 0, 0)),
        scratch_shapes=[pltpu.VMEM((S5_HG * t, S5_HG * t), BF16),
                        pltpu.VMEM((S5_HG * t, 2 * S5_PP), BF16),
                        pltpu.VMEM((2 * S5_PP, S5_HG * t), BF16),
                        pltpu.VMEM((2 * S5_PP, S5_HG * t), F32),
                        pltpu.VMEM((nc, S5_PP), F32), pltpu.VMEM((nc, S5_PP), F32),
                        pltpu.VMEM((nc, S5_PP), F32), pltpu.VMEM((nc, S5_PP), F32)],
        compiler_params=_cparams(("parallel", "arbitrary")),
        name="s5_scan",
    )(ut, col, row)
    return yt.reshape(BR, n).T


def _head_rms(x, g):
    parts = []
    for h in range(NH):
        xh = x[:, h * HD:(h + 1) * HD]
        parts.append(xh * lax.rsqrt(jnp.mean(jnp.square(xh), axis=-1, keepdims=True) + NORM_EPS))
    return jnp.concatenate(parts, axis=-1) * g


def _finish_kernel(raf_ref, rab_ref, rbf_ref, rbb_ref, rc_ref, rdf_ref, rdb_ref, ga_ref, gb_ref, u_ref, gd_ref,
                   mg_ref, x_ref, mod_ref, vec_ref, glu_w_ref, wb_ref, wo_ref, o_ref):
    vec = vec_ref[...]
    oa = _head_rms(raf_ref[...] + rab_ref[...], vec[0:1]) * _silu(ga_ref[...].astype(F32))
    ob = _head_rms(rbf_ref[...] + rbb_ref[...], vec[1:2]) * _silu(gb_ref[...].astype(F32))
    od = _head_rms(rdf_ref[...] + rdb_ref[...], vec[2:3]) * _silu(gd_ref[...].astype(F32))
    yc = rc_ref[...] + vec[3:4] * u_ref[...].astype(F32)
    yc = 0.5 * yc * (1.0 + lax.erf(yc * (2.0 ** -0.5)))
    oc = yc * _sigmoid(_dot(yc.astype(BF16), glu_w_ref[...]) + vec[4:5])
    y = None
    for j, o in enumerate((oa, ob, oc, od)):
        gate = _sigmoid_tanh(mg_ref[:, j * D_MODEL:(j + 1) * D_MODEL].astype(F32))
        term = gate * _dot(o.astype(BF16), wb_ref[j])
        y = term if y is None else y + term
    mix = _dot(y.astype(BF16), wo_ref[...])
    o_ref[...] = x_ref[...] + mod_ref[2:3, :] * mix


def _finish(raw_a, raw_b, raw_c, raw_d, p, x_all, mod, vec, glu_w, w_branch, w_out, layer, *, n_rows, n_batch,
            seq, tm=256):
    mrow = _mod_row_map(n_rows, tm, n_batch, seq)
    one = pl.Buffered(1)

    def pcol(cblock):
        return pl.BlockSpec((tm, BR), lambda i: (i, cblock))

    raw = pl.BlockSpec((tm, BR), lambda i: (i, 0))
    return pl.pallas_call(
        _finish_kernel,
        out_shape=jax.ShapeDtypeStruct((n_rows, D_MODEL), F32),
        grid=(n_rows // tm,),
        in_specs=[raw, raw, raw, raw, raw, raw, raw,
                  pcol(COL_HG), pcol(COL_RG), pcol(COL_SU), pcol(COL_MZ),
                  pl.BlockSpec((tm, N_BRANCH * D_MODEL), lambda i: (i, COL_MERGE)),
                  pl.BlockSpec((tm, D_MODEL), lambda i: (i, 0)),
                  pl.BlockSpec((None, 8, D_MODEL), lambda i: (mrow(i), 0, 0)),
                  pl.BlockSpec((None, 8, BR), lambda i: (layer, 0, 0), pipeline_mode=one),
                  pl.BlockSpec((None, BR, BR), lambda i: (layer, 0, 0), pipeline_mode=one),
                  pl.BlockSpec((None, N_BRANCH, BR, D_MODEL), lambda i: (layer, 0, 0, 0), pipeline_mode=one),
                  pl.BlockSpec((None, D_MODEL, D_MODEL), lambda i: (layer, 0, 0), pipeline_mode=one)],
        out_specs=pl.BlockSpec((tm, D_MODEL), lambda i: (i, 0)),
        compiler_params=_cparams(("parallel",)),
        name="finish",
    )(*raw_a, *raw_b, raw_c, *raw_d, p, p, p, p, p, x_all, mod, vec, glu_w, w_branch, w_out)


def _mlp_kernel(x_ref, mod_ref, g_ref, w1_ref, w2_ref, fin_ref, o_ref, xn_ref, acc_ref, *, final_norm):
    j = pl.program_id(1)

    @pl.when(j == 0)
    def _():
        h = _norm_mod(x_ref[...], g_ref[...], mod_ref[3:4, :], mod_ref[4:5, :])
        xn_ref[...] = h.astype(BF16)
        acc_ref[...] = jnp.zeros_like(acc_ref)

    a = jnp.square(jnp.maximum(_dot(xn_ref[...], w1_ref[...]), 0.0))
    acc_ref[...] += _dot(a.astype(BF16), w2_ref[...])

    @pl.when(j == pl.num_programs(1) - 1)
    def _():
        y = x_ref[...] + mod_ref[5:6, :] * acc_ref[...]
        if final_norm:
            y = (y * lax.rsqrt(jnp.mean(jnp.square(y), axis=-1, keepdims=True) + NORM_EPS)) * fin_ref[...]
        o_ref[...] = y


def _mlp(x_all, mod, norm_w, w1, w2, fin_w, layer, *, n_rows, n_batch, seq, final_norm, tm=512, tf=1024):
    mrow = _mod_row_map(n_rows, tm, n_batch, seq)
    kern = functools.partial(_mlp_kernel, final_norm=final_norm)
    return pl.pallas_call(
        kern,
        out_shape=jax.ShapeDtypeStruct((n_rows, D_MODEL), F32),
        grid=(n_rows // tm, D_FF // tf),
        in_specs=[pl.BlockSpec((tm, D_MODEL), lambda i, j: (i, 0)),
                  pl.BlockSpec((None, 8, D_MODEL), lambda i, j: (mrow(i), 0, 0)),
                  pl.BlockSpec((None, 1, D_MODEL), lambda i, j: (layer, 0, 0)),
                  pl.BlockSpec((None, D_MODEL, tf), lambda i, j: (layer, 0, j)),
                  pl.BlockSpec((None, tf, D_MODEL), lambda i, j: (layer, j, 0)),
                  pl.BlockSpec((1, D_MODEL), lambda i, j: (0, 0))],
        out_specs=pl.BlockSpec((tm, D_MODEL), lambda i, j: (i, 0)),
        scratch_shapes=[pltpu.VMEM((tm, D_MODEL), BF16), pltpu.VMEM((tm, D_MODEL), F32)],
        compiler_params=_cparams(("parallel", "arbitrary")),
        name="mlp",
    )(x_all, mod, norm_w, w1, w2, fin_w.reshape(1, D_MODEL))


W_PREP_ROWS = 1024
N_MERGE_BLK = N_BRANCH * D_MODEL // W_PREP_ROWS
MERGE_OFF = GATE_OFF + 4 * NH


def _w_prep_kernel(a_ref, g_ref, main_ref, gate_ref):
    @pl.when(pl.program_id(1) == 0)
    def _():
        gate_ref[...] = jnp.zeros_like(gate_ref)
        gate_ref[0:4 * NH, :] = g_ref[0].astype(BF16)

    main_ref[...] = a_ref[0].astype(BF16)


def _split_w_in(w):
    depth = w.shape[0]
    wt = jnp.swapaxes(w, 1, 2)
    tr = W_PREP_ROWS

    def a_map(l, j):
        row = jnp.where(j < N_MERGE_BLK, MERGE_OFF + j * tr, (j - N_MERGE_BLK) * tr)
        return l, pl.multiple_of(row, 16), 0

    return pl.pallas_call(
        _w_prep_kernel,
        out_shape=(jax.ShapeDtypeStruct((depth, P_WIDTH, D_MODEL), BF16),
                   jax.ShapeDtypeStruct((depth, GATE_PAD, D_MODEL), BF16)),
        grid=(depth, P_WIDTH // tr),
        in_specs=[pl.BlockSpec((pl.Element(1), pl.Element(tr), pl.Element(D_MODEL)), a_map),
                  pl.BlockSpec((pl.Element(1), pl.Element(4 * NH), pl.Element(D_MODEL)),
                               lambda l, j: (l, GATE_OFF, 0))],
        out_specs=(pl.BlockSpec((None, tr, D_MODEL), lambda l, j: (l, j, 0)),
                   pl.BlockSpec((None, GATE_PAD, D_MODEL), lambda l, j: (l, 0, 0))),
        compiler_params=_cparams(("parallel", "arbitrary")),
        name="w_in_prep",
    )(wt, wt)


def kernel(x, c, ctx, c_ctx, w_mod, b_mod, norm_mix, norm_mlp, w_in, hgrn_lb_logits, hgrn_norm, ret_decay, ret_norm, s5_a_re, s5_a_im, s5_log_dt, s5_b_re, s5_b_im, s5_c_re, s5_c_im, s5_d, s5_glu_w, s5_glu_b, mlstm_conv_w, mlstm_conv_b, mlstm_gate_b, mlstm_norm, w_branch, w_out, w_ff1, w_ff2, final_norm):
    n_batch, seq, _ = x.shape
    ctx_len = ctx.shape[1]
    depth = w_in.shape[0]
    nl = n_batch * seq
    dims = dict(n_batch=n_batch, seq=seq)

    p_lb = jax.nn.softmax(hgrn_lb_logits.astype(F32), axis=0)
    lower_bounds = (jnp.cumsum(p_lb, axis=0) - p_lb[0]).reshape(depth, 2, 1, BR)
    lg = jnp.log1p(-jnp.exp(ret_decay.astype(F32)))
    gate_b = jnp.pad(mlstm_gate_b.reshape(depth, 1, 4 * NH).astype(F32), ((0, 0), (0, 0), (0, GATE_PAD - 4 * NH)))
    vec = jnp.stack([hgrn_norm, ret_norm, mlstm_norm, s5_d, s5_glu_b], axis=1).astype(F32)
    vec = jnp.pad(vec, ((0, 0), (0, 8 - vec.shape[1]), (0, 0)))
    w_main, w_gate = _split_w_in(w_in)
    glu_w, wb, wo = s5_glu_w.astype(BF16), w_branch.astype(BF16), w_out.astype(BF16)
    w1, w2 = w_ff1.astype(BF16), w_ff2.astype(BF16)
    norm_mix3 = norm_mix.reshape(depth, 1, D_MODEL)
    norm_mlp3 = norm_mlp.reshape(depth, 1, D_MODEL)
    conv_w = mlstm_conv_w.reshape(depth, 9, 2 * BR)
    conv_b = mlstm_conv_b.reshape(depth, 1, 2 * BR)
    s5_col, s5_row = _s5_params(s5_a_re, s5_a_im, s5_log_dt, s5_b_re, s5_b_im, s5_c_re, s5_c_im)

    c_pad = jnp.zeros((8, D_MODEL), F32).at[:n_batch].set(c).at[n_batch].set(c_ctx)
    x_all = jnp.concatenate([x.reshape(nl, D_MODEL), ctx.reshape(n_batch * ctx_len, D_MODEL)], axis=0)

    for l in range(depth):
        last = l == depth - 1
        mod = _modulation(c_pad, w_mod, b_mod, l)
        mod = jnp.pad(mod[:n_batch + 1].reshape(n_batch + 1, N_MOD, D_MODEL), ((0, 0), (0, 8 - N_MOD), (0, 0)))
        p, gates = _in_proj(x_all, mod, norm_mix3, w_main, w_gate, l, **dims)
        qk = _mlstm_conv(p, conv_w, conv_b, l, ctx_len=ctx_len, **dims)
        raw_a, raw_b, raw_d = _scans(p, gates, qk, lg[l], lower_bounds, gate_b, l, ctx_len=ctx_len, **dims)
        raw_c = _s5_scan(p, s5_col, s5_row, l, ctx_len=ctx_len, **dims)
        n_rows = nl if last else x_all.shape[0]
        x_mid = _finish(raw_a, raw_b, raw_c, raw_d, p, x_all, mod, vec, glu_w, wb, wo, l, n_rows=n_rows, **dims)
        x_all = _mlp(x_mid, mod, norm_mlp3, w1, w2, final_norm, l, n_rows=n_rows, final_norm=last, **dims)
    return x_all.reshape(n_batch, seq, D_MODEL)
```

```python
import functools
import math

import jax
import jax.numpy as jnp
from jax import lax
from jax.experimental import pallas as pl
from jax.experimental.pallas import tpu as pltpu

F32 = jnp.float32
BF16 = jnp.bfloat16

D_MODEL = 2048
N_BRANCH = 4
BR = D_MODEL // N_BRANCH
HD = 128
NH = BR // HD
S5_HG = 16
S5_G = BR // S5_HG
S5_P = 64
D_FF = 4 * D_MODEL
N_MOD = 6
GRID_W = 64
NORM_EPS = 1e-6
NEG_BIG = -1e30
F_TINY = 1e-30
S5_DT_MIN = 1e-3

COL_MERGE = 0
COL_HQ, COL_HF, COL_HV, COL_HG = 16, 17, 19, 20
COL_RQ, COL_RK, COL_RV, COL_RG = 21, 22, 23, 24
COL_SU = 25
COL_MQK, COL_MV, COL_MZ = 26, 28, 29
P_WIDTH = 30 * BR
GATE_OFF = 14 * BR
GATE_PAD = 128

T_HGRN = 128
T_ATTN = 256
S5_T = 128
S5_PP = 128
S5_ROWS = 40
VMEM_LIMIT = 56 * 1024 * 1024

_HI = lax.Precision.HIGHEST


def _cparams(sem):
    return pltpu.CompilerParams(dimension_semantics=sem, vmem_limit_bytes=VMEM_LIMIT)


def _dot(a, b):
    return jnp.dot(a, b, preferred_element_type=F32)


def _dot_nt(a, b):
    return lax.dot_general(a, b, (((1,), (1,)), ((), ())), preferred_element_type=F32)


def _dot_tn(a, b):
    return lax.dot_general(a, b, (((0,), (0,)), ((), ())), preferred_element_type=F32)


def _split3(x):
    hi = x.astype(BF16)
    r1 = x - hi.astype(F32)
    mid = r1.astype(BF16)
    return hi, mid, (r1 - mid.astype(F32)).astype(BF16)


def _cumsum_rows(tri, x):
    hi, mid, lo = _split3(x)
    return _dot(tri, hi) + (_dot(tri, mid) + _dot(tri, lo))


def _sigmoid(x):
    return 1.0 / (1.0 + jnp.exp(-x))


def _sigmoid_tanh(x):
    return 0.5 * jnp.tanh(0.5 * x) + 0.5


def _silu(x):
    return x * _sigmoid(x)


def _mod_kernel(c_ref, w_ref, b_ref, o_ref):
    @pl.when(pl.program_id(0) == 0)
    def _():
        o_ref[...] = jnp.broadcast_to(b_ref[...], o_ref.shape)

    a = _silu(c_ref[...]).astype(BF16)
    o_ref[...] += _dot(a, w_ref[...].astype(BF16))


def _modulation(c_pad, w_mod, b_mod, layer, tk=256):
    n = w_mod.shape[2]
    return pl.pallas_call(
        _mod_kernel,
        out_shape=jax.ShapeDtypeStruct((8, n), F32),
        grid=(D_MODEL // tk,),
        in_specs=[pl.BlockSpec((8, tk), lambda k: (0, k)),
                  pl.BlockSpec((None, tk, n), lambda k: (layer, k, 0)),
                  pl.BlockSpec((None, 1, n), lambda k: (layer, 0, 0))],
        out_specs=pl.BlockSpec((8, n), lambda k: (0, 0)),
        compiler_params=_cparams(("arbitrary",)),
        name="modulation",
    )(c_pad, w_mod, b_mod.reshape(b_mod.shape[0], 1, n))


def _norm_mod(x, g, shift, scale):
    y = x * lax.rsqrt(jnp.mean(jnp.square(x), axis=-1, keepdims=True) + NORM_EPS)
    return (y * g) * (1.0 + scale) + shift


def _inproj_kernel(x_ref, mod_ref, g_ref, w_ref, wg_ref, p_ref, gate_ref, xn_ref):
    @pl.when(pl.program_id(1) == 0)
    def _():
        h = _norm_mod(x_ref[...], g_ref[...], mod_ref[0:1, :], mod_ref[1:2, :])
        xn_ref[...] = h.astype(BF16)
        gate_ref[...] = _dot_nt(xn_ref[...], wg_ref[...])

    p_ref[...] = _dot_nt(xn_ref[...], w_ref[...]).astype(p_ref.dtype)


def _mod_row_map(n_rows, tm, n_batch, seq):
    assert n_rows % tm == 0 and seq % tm == 0, (n_rows, seq, tm)
    n_lat_tiles, tiles_per_batch = n_batch * seq // tm, seq // tm

    def row(i):
        return jnp.where(i < n_lat_tiles, i // tiles_per_batch, n_batch)
    return row


def _in_proj(x_all, mod, norm_w, w_main, w_gate, layer, *, n_batch, seq, tm=1024, tn=1536):
    n = x_all.shape[0]
    assert P_WIDTH % tn == 0, tn
    mrow = _mod_row_map(n, tm, n_batch, seq)
    return pl.pallas_call(
        _inproj_kernel,
        out_shape=(jax.ShapeDtypeStruct((n, P_WIDTH), BF16),
                   jax.ShapeDtypeStruct((n, GATE_PAD), F32)),
        grid=(n // tm, P_WIDTH // tn),
        in_specs=[pl.BlockSpec((tm, D_MODEL), lambda i, j: (i, 0)),
                  pl.BlockSpec((None, 8, D_MODEL), lambda i, j: (mrow(i), 0, 0)),
                  pl.BlockSpec((None, 1, D_MODEL), lambda i, j: (layer, 0, 0)),
                  pl.BlockSpec((None, tn, D_MODEL), lambda i, j: (layer, j, 0)),
                  pl.BlockSpec((None, GATE_PAD, D_MODEL), lambda i, j: (layer, 0, 0))],
        out_specs=(pl.BlockSpec((tm, tn), lambda i, j: (i, j)),
                   pl.BlockSpec((tm, GATE_PAD), lambda i, j: (i, 0))),
        scratch_shapes=[pltpu.VMEM((tm, D_MODEL), BF16)],
        compiler_params=_cparams(("parallel", "arbitrary")),
        name="in_proj",
    )(x_all, mod, norm_w, w_main, w_gate)


CONV_ROWS = 256


def _conv_kernel(up_ref, x_ref, dn_ref, w_ref, b_ref, o_ref, *, n_lat_blocks, blocks_per_img):
    rb = pl.program_id(0)
    x = x_ref[...].astype(F32)
    w = w_ref[...]
    n = CONV_ROWS

    @pl.when(rb < n_lat_blocks)
    def _():
        rr = rb % blocks_per_img
        up = jnp.where(rr > 0, up_ref[...].astype(F32), 0.0)
        dn = jnp.where(rr < blocks_per_img - 1, dn_ref[...].astype(F32), 0.0)
        ext = jnp.concatenate([up, x, dn], axis=0)
        ne = n + 2 * GRID_W
        col = lax.broadcasted_iota(jnp.int32, (ne, 1), 0) % GRID_W
        xl = jnp.where(col >= 1, pltpu.roll(ext, 1, 0), 0.0)
        xr = jnp.where(col <= GRID_W - 2, pltpu.roll(ext, ne - 1, 0), 0.0)
        acc = None
        for i in range(3):
            y = w[3 * i:3 * i + 1] * xl + w[3 * i + 1:3 * i + 2] * ext + w[3 * i + 2:3 * i + 3] * xr
            part = y[i * GRID_W:i * GRID_W + n]
            acc = part if acc is None else acc + part
        o_ref[...] = _silu(acc + b_ref[...]).astype(o_ref.dtype)

    @pl.when(rb >= n_lat_blocks)
    def _():
        t = lax.broadcasted_iota(jnp.int32, (n, 1), 0)
        xl = jnp.where(t >= 1, pltpu.roll(x, 1, 0), 0.0)
        xr = jnp.where(t <= n - 2, pltpu.roll(x, n - 1, 0), 0.0)
        acc = w[3:4] * xl + w[4:5] * x + w[5:6] * xr
        o_ref[...] = _silu(acc + b_ref[...]).astype(o_ref.dtype)


def _mlstm_conv(p, conv_w, conv_b, layer, *, n_batch, seq, ctx_len, cb=2 * BR):
    assert ctx_len == CONV_ROWS and seq % CONV_ROWS == 0
    n = p.shape[0]
    n_blocks = n // CONV_ROWS
    hb = CONV_ROWS // GRID_W
    n_halo = n // GRID_W
    c0 = COL_MQK * BR // cb
    kern = functools.partial(_conv_kernel, n_lat_blocks=n_batch * seq // CONV_ROWS,
                             blocks_per_img=seq // CONV_ROWS)
    return pl.pallas_call(
        kern,
        out_shape=jax.ShapeDtypeStruct((n, 2 * BR), BF16),
        grid=(n_blocks, 2 * BR // cb),
        in_specs=[pl.BlockSpec((GRID_W, cb), lambda r, c: (jnp.maximum(r * hb - 1, 0), c0 + c)),
                  pl.BlockSpec((CONV_ROWS, cb), lambda r, c: (r, c0 + c)),
                  pl.BlockSpec((GRID_W, cb), lambda r, c: (jnp.minimum((r + 1) * hb, n_halo - 1), c0 + c)),
                  pl.BlockSpec((None, 9, cb), lambda r, c: (layer, 0, c)),
                  pl.BlockSpec((None, 1, cb), lambda r, c: (layer, 0, c))],
        out_specs=pl.BlockSpec((CONV_ROWS, cb), lambda r, c: (r, c)),
        compiler_params=_cparams(("parallel", "parallel")),
        name="mlstm_conv",
    )(p, p, p, conv_w, conv_b)


def _chunk_block(b, d, i, *, n_ctx, n_lat, n_batch):
    ctx_j = i if d == 0 else n_ctx - 1 - i
    lat_j = i - n_ctx if d == 0 else n_ctx + n_lat - 1 - i
    return jnp.where(i < n_ctx, n_batch * n_lat + b * n_ctx + ctx_j, b * n_lat + lat_j)


def _order_mask(d, t):
    r = lax.broadcasted_iota(jnp.int32, (t, t), 0)
    c = lax.broadcasted_iota(jnp.int32, (t, t), 1)
    return (c <= r) if d == 0 else (c >= r)


def _init_at_first_step(*refs):
    @pl.when(pl.program_id(1) == 0)
    def _():
        for ref in refs:
            ref[...] = jnp.zeros_like(ref)


def _ret_direction(d, lg_ref, q_ref, k_ref, v_ref, o_ref, s_ref):
    t = T_ATTN
    r = lax.broadcasted_iota(jnp.int32, (t, t), 0)
    c = lax.broadcasted_iota(jnp.int32, (t, t), 1)
    rel = (r - c) if d == 0 else (c - r)
    relf = jnp.maximum(rel, 0).astype(F32)
    tt = lax.broadcasted_iota(jnp.int32, (t, 1), 0)
    pos = (tt if d == 0 else t - 1 - tt).astype(F32)
    scale = HD ** -0.5
    for h in range(NH):
        sl = slice(h * HD, (h + 1) * HD)
        lg = lg_ref[d, h]
        q = q_ref[:, sl]
        k = k_ref[:, sl]
        v = v_ref[:, sl]
        intra = jnp.where(rel >= 0, jnp.exp(lg * relf), 0.0)
        scores = (_dot_nt(q, k) * scale) * intra
        s_old = s_ref[d, h]
        out = _dot(scores.astype(BF16), v) + jnp.exp(lg * (pos + 1.0)) * _dot(q, s_old.astype(BF16))
        kd = (k.astype(F32) * (scale * jnp.exp(lg * (t - 1.0 - pos)))).astype(BF16)
        s_ref[d, h] = jnp.exp(lg * t) * s_old + _dot_tn(kd, v)
        o_ref[:, sl] = out


def _ret_kernel(lg_ref, qf_ref, kf_ref, vf_ref, qb_ref, kb_ref, vb_ref, of_ref, ob_ref, s_ref):
    _init_at_first_step(s_ref)
    _ret_direction(0, lg_ref, qf_ref, kf_ref, vf_ref, of_ref, s_ref)
    _ret_direction(1, lg_ref, qb_ref, kb_ref, vb_ref, ob_ref, s_ref)


def _hgrn_direction(d, lb_ref, q_ref, z_ref, v_ref, o_ref, st_ref):
    t = T_HGRN
    fwd = d == 0
    lb = lb_ref[...]
    z = z_ref[...].astype(F32)
    sig = _sigmoid(z)
    f = lb + (1.0 - lb) * sig
    logf = jnp.log(jnp.maximum(f, F_TINY))
    kk = (1.0 - lb) * (1.0 - sig)
    qq = _silu(q_ref[...].astype(F32))
    r2 = lax.broadcasted_iota(jnp.int32, (t, t), 0)
    c2 = lax.broadcasted_iota(jnp.int32, (t, t), 1)
    tri = jnp.where((c2 <= r2) if fwd else (c2 >= r2), 1.0, 0.0).astype(BF16)
    cum = _cumsum_rows(tri, logf)
    xdiff = jnp.where((c2 < r2) if fwd else (c2 > r2), r2 ^ c2, 0)
    row = lax.broadcasted_iota(jnp.int32, (t, 1), 0)
    if fwd:
        e_ref = jnp.where(row == 0, 0.0, pltpu.roll(cum, 1, 0))
    else:
        e_ref = jnp.where(row == t - 1, 0.0, pltpu.roll(cum, t - 1, 0))
    f_ref = cum
    levels = []
    w = 1
    while w < t:
        q_l = (qq * jnp.exp(cum - e_ref)).astype(BF16)
        k_l = (kk * jnp.exp(f_ref - cum)).astype(BF16)
        levels.append((q_l, k_l, (xdiff >> (w.bit_length() - 1)) == 1))
        upper = (row & w) != 0
        if fwd:
            e_ref = jnp.where(upper, pltpu.roll(e_ref, w, 0), e_ref)
            f_ref = jnp.where(upper, f_ref, pltpu.roll(f_ref, t - w, 0))
        else:
            e_ref = jnp.where(upper, e_ref, pltpu.roll(e_ref, t - w, 0))
            f_ref = jnp.where(upper, pltpu.roll(f_ref, w, 0), f_ref)
        w *= 2
    q_b = qq.astype(BF16)
    k_b = kk.astype(BF16)
    tail = cum[t - 1:t] if fwd else cum[0:1]
    q_state = (qq * jnp.exp(cum)).astype(BF16)
    k_state = (kk * jnp.exp(tail - cum)).astype(BF16)
    decay = jnp.exp(tail)
    for h in range(NH):
        sl = slice(h * HD, (h + 1) * HD)
        v_h = v_ref[:, sl]
        scores = jnp.where(r2 == c2, _dot_nt(q_b[:, sl], k_b[:, sl]), 0.0)
        for q_l, k_l, m in levels:
            scores = jnp.where(m, _dot_nt(q_l[:, sl], k_l[:, sl]), scores)
        st = st_ref[d, h]
        o_ref[:, sl] = _dot(scores.astype(BF16), v_h) + _dot_nt(q_state[:, sl], st.astype(BF16))
        st_ref[d, h] = st * decay[:, sl] + _dot_tn(v_h, k_state[:, sl])


def _hgrn_kernel(lbf_ref, qf_ref, zf_ref, vf_ref, lbb_ref, qb_ref, zb_ref, vb_ref, of_ref, ob_ref, st_ref):
    _init_at_first_step(st_ref)
    _hgrn_direction(0, lbf_ref, qf_ref, zf_ref, vf_ref, of_ref, st_ref)
    _hgrn_direction(1, lbb_ref, qb_ref, zb_ref, vb_ref, ob_ref, st_ref)


def _log_sigmoid(x):
    return jnp.minimum(x, 0.0) - jnp.log(1.0 + jnp.exp(-jnp.abs(x)))


def _rep_lane(parts, lane, width):
    sel = jnp.where(lax.broadcasted_iota(jnp.int32, (GATE_PAD, width), 0) == lane, 1.0, 0.0).astype(BF16)
    hi, mid, lo = parts
    return _dot(hi, sel) + (_dot(mid, sel) + _dot(lo, sel))


def _mlstm_direction(d, qk_ref, v_ref, g_ref, gb_ref, o_ref, c_ref, m_ref):
    t = T_ATTN
    assert t == 2 * HD
    fwd = d == 0
    g = g_ref[...] + gb_ref[...]
    r2 = lax.broadcasted_iota(jnp.int32, (t, t), 0)
    c2 = lax.broadcasted_iota(jnp.int32, (t, t), 1)
    mask = (c2 <= r2) if fwd else (c2 >= r2)
    tri = jnp.where(mask, 1.0, 0.0).astype(BF16)
    eye = jnp.where(r2 == c2, 1.0, 0.0).astype(BF16)
    lf = _log_sigmoid(g)
    cum_c = _cumsum_rows(tri, lf)
    a_col = g - pltpu.roll(cum_c, GATE_PAD - NH, 1)
    row = lax.broadcasted_iota(jnp.int32, (t, 1), 0)
    pm = a_col
    k = 1
    while k < t:
        if fwd:
            pm = jnp.maximum(pm, jnp.where(row >= k, pltpu.roll(pm, k, 0), NEG_BIG))
        else:
            pm = jnp.maximum(pm, jnp.where(row < t - k, pltpu.roll(pm, t - k, 0), NEG_BIG))
        k *= 2
    a_parts, pm_parts, cum_parts = _split3(a_col), _split3(pm), _split3(cum_c)
    a_hi, a_mid, a_lo = a_parts
    a_rows = _dot_tn(a_hi, eye) + (_dot_tn(a_mid, eye) + _dot_tn(a_lo, eye))
    scale = HD ** -0.5
    last = t - 1 if fwd else 0
    ones = jnp.ones((t, HD), BF16)
    for h in range(NH):
        sl = slice(h * HD, (h + 1) * HD)
        li, lfw = 2 * NH * d + h, 2 * NH * d + NH + h
        m_old = m_ref[d, h]
        mx2 = jnp.maximum(_rep_lane(pm_parts, li, 2 * HD), jnp.concatenate([m_old, m_old], axis=1))
        mx = mx2[:, 0:HD]
        cc = _rep_lane(cum_parts, lfw, HD)
        a_rep = _rep_lane(a_parts, li, HD)
        a_row = a_rows[li:li + 1]
        q = qk_ref[:, sl]
        k_h = qk_ref[:, BR + h * HD:BR + (h + 1) * HD]
        v_aug = jnp.concatenate([v_ref[:, sl], ones], axis=1)
        w = jnp.where(mask, jnp.exp(a_row - mx2), 0.0)
        scores = (_dot_nt(q, k_h) * scale) * w
        w_state = jnp.exp(m_old - mx)
        c_old = c_ref[d, h]
        s_hi = scores.astype(BF16)
        s_mid = (scores - s_hi.astype(F32)).astype(BF16)
        intra = _dot(s_hi, v_aug)
        inter = _dot(q, c_old.astype(BF16))
        num = intra[:, 0:HD] + w_state * inter[:, 0:HD]
        den = (intra[:, HD:2 * HD] + _dot(s_mid, ones)) + w_state * inter[:, HD:2 * HD]
        o_ref[:, sl] = num / jnp.maximum(jnp.abs(den), jnp.exp(-(cc + mx)))
        total = cc[last:last + 1]
        m_new_rel = mx[last:last + 1]
        keep = jnp.exp(m_old - m_new_rel)
        w_end = jnp.exp(a_rep - m_new_rel)
        kw = (k_h.astype(F32) * (scale * w_end)).astype(BF16)
        c_ref[d, h] = jnp.concatenate([keep, keep], axis=1) * c_old + _dot_tn(kw, v_aug)
        m_ref[d, h] = total + m_new_rel


def _mlstm_kernel(qkf_ref, vf_ref, gf_ref, qkb_ref, vb_ref, gb_ref, bias_ref, of_ref, ob_ref, c_ref, m_ref):
    _init_at_first_step(c_ref, m_ref)
    _mlstm_direction(0, qkf_ref, vf_ref, gf_ref, bias_ref, of_ref, c_ref, m_ref)
    _mlstm_direction(1, qkb_ref, vb_ref, gb_ref, bias_ref, ob_ref, c_ref, m_ref)


def _scans(p, gates, qk, lg, lb, gate_b, layer, *, n_batch, seq, ctx_len):
    n = p.shape[0]
    state = pltpu.VMEM((2, NH, HD, HD), F32)
    vec = pltpu.VMEM((2, NH, 1, HD), F32)

    def call(kernel, t, inputs, make_specs, scratch, name):
        assert ctx_len % t == 0 and seq % t == 0, (ctx_len, seq, t)
        n_ctx, n_lat = ctx_len // t, seq // t

        def rows(d, width, cblock=0):
            return pl.BlockSpec((t, width), lambda b, i: (_chunk_block(b, d, i, n_ctx=n_ctx, n_lat=n_lat,
                                                                        n_batch=n_batch), cblock))

        out = jax.ShapeDtypeStruct((n, BR), F32)
        return pl.pallas_call(
            kernel,
            out_shape=(out, out),
            grid=(n_batch, n_ctx + n_lat),
            in_specs=make_specs(rows),
            out_specs=(rows(0, BR), rows(1, BR)),
            scratch_shapes=scratch,
            compiler_params=_cparams(("parallel", "arbitrary")),
            name=name,
        )(*inputs)

    raw_b = call(_ret_kernel, T_ATTN, (lg, p, p, p, p, p, p),
                 lambda rows: [pl.BlockSpec(memory_space=pltpu.SMEM)]
                 + [rows(d, BR, c) for d in (0, 1) for c in (COL_RQ, COL_RK, COL_RV)],
                 [state], "retention_scan")
    lb_spec = lambda d: pl.BlockSpec((None, None, 1, BR), lambda b, i: (layer, d, 0, 0))
    raw_a = call(_hgrn_kernel, T_HGRN, (lb, p, p, p, lb, p, p, p),
                 lambda rows: [spec for d in (0, 1)
                               for spec in (lb_spec(d), rows(d, BR, COL_HQ), rows(d, BR, COL_HF + d),
                                            rows(d, BR, COL_HV))],
                 [state], "hgrn2_scan")
    raw_d = call(_mlstm_kernel, T_ATTN, (qk, p, gates, qk, p, gates, gate_b),
                 lambda rows: [spec for d in (0, 1)
                               for spec in (rows(d, 2 * BR), rows(d, BR, COL_MV), rows(d, GATE_PAD))]
                 + [pl.BlockSpec((None, 1, GATE_PAD), lambda b, i: (layer, 0, 0))],
                 [pltpu.VMEM((2, NH, HD, 2 * HD), F32), vec], "mlstm_scan")
    return raw_a, raw_b, raw_d


def _cmul(ar, ai, br, bi):
    return ar * br - ai * bi, ar * bi + ai * br


def _pow_table(lam_r, lam_i, expo):
    pr = jnp.ones(expo.shape, F32)
    pi = jnp.zeros(expo.shape, F32)
    ar = jnp.broadcast_to(lam_r, expo.shape)
    ai = jnp.broadcast_to(lam_i, expo.shape)
    for k in range(S5_T.bit_length() - 1):
        nr, ni = _cmul(pr, pi, ar, ai)
        bit = (expo & (1 << k)) != 0
        pr, pi = jnp.where(bit, nr, pr), jnp.where(bit, ni, pi)
        ar, ai = _cmul(ar, ai, ar, ai)
    return pr, pi


def _s5_kernel(u_ref, col_ref, row_ref, o_ref, m_ref, w_ref, ys_ref, z_ref, wr_ref, wi_ref, xr_ref, xi_ref, *,
               n_batch, n_ctx, n_lat):
    d = pl.program_id(1)
    t = S5_T
    col = col_ref[...]
    rowp = row_ref[...]
    lam_rc, lam_ic = col[:, 0:1], col[:, 1:2]
    bbr, bbi = rowp[0:S5_HG], rowp[S5_HG:2 * S5_HG]
    lam_rr, lam_ir = rowp[2 * S5_HG:2 * S5_HG + 1], rowp[2 * S5_HG + 1:2 * S5_HG + 2]
    lane = lax.broadcasted_iota(jnp.int32, (t, t), 1)
    sub = lax.broadcasted_iota(jnp.int32, (t, t), 0)

    pr, pi = _pow_table(lam_rc, lam_ic, jnp.where(d == 0, lane, t - 1 - lane))
    qr, qi = _cmul(pr, pi, lam_rc, lam_ic)
    for h in range(S5_HG):
        cr, ci = col[:, 2 + h:3 + h], col[:, 2 + S5_HG + h:3 + S5_HG + h]
        hs = slice(h * t, (h + 1) * t)
        z_ref[0:S5_PP, hs] = cr * pr - ci * pi
        z_ref[S5_PP:2 * S5_PP, hs] = -(cr * pi + ci * pr)
        ys_ref[0:S5_PP, hs] = (cr * qr - ci * qi).astype(BF16)
        ys_ref[S5_PP:2 * S5_PP, hs] = (-(cr * qi + ci * qr)).astype(BF16)
    b_hi, b_mid, _ = _split3(jnp.concatenate([bbr, bbi], axis=1))
    z_hi, z_mid, _ = _split3(z_ref[...])
    krow = _dot(b_hi, z_hi) + (_dot(b_hi, z_mid) + _dot(b_mid, z_hi))

    def build_m(shift, keep):
        for hp in range(S5_HG):
            for h in range(S5_HG):
                tile = jnp.broadcast_to(krow[hp:hp + 1, h * t:(h + 1) * t], (t, t))
                tile = pltpu.roll(tile, shift, 1, stride=1, stride_axis=0)
                m_ref[hp * t:(hp + 1) * t, h * t:(h + 1) * t] = jnp.where(keep, tile, 0.0).astype(BF16)

    @pl.when(d == 0)
    def _():
        build_m(0, lane >= sub)

    @pl.when(d == 1)
    def _():
        build_m(1, lane <= sub)

    tr, ti = _pow_table(lam_rr, lam_ir, jnp.where(d == 0, t - 1 - sub, sub))
    for hp in range(S5_HG):
        br, bi = bbr[hp:hp + 1], bbi[hp:hp + 1]
        w_ref[hp * t:(hp + 1) * t, 0:S5_PP] = (tr * br - ti * bi).astype(BF16)
        w_ref[hp * t:(hp + 1) * t, S5_PP:2 * S5_PP] = (tr * bi + ti * br).astype(BF16)

    u = jnp.concatenate([u_ref[hp] for hp in range(S5_HG)], axis=1)
    wv = _dot(u, w_ref[...])
    wr_ref[...] = wv[:, 0:S5_PP]
    wi_ref[...] = wv[:, S5_PP:2 * S5_PP]
    lr, li = lam_rr, lam_ir
    for _ in range(t.bit_length() - 1):
        lr, li = _cmul(lr, li, lr, li)

    def make_step(base, stride, count):
        def step(i, carry):
            xr, xi = carry
            j = jnp.where(d == 0, i, count - 1 - i)
            idx = pl.ds(base + j, n_batch, stride=stride)
            xr_ref[idx, :] = xr
            xi_ref[idx, :] = xi
            nr, ni = _cmul(xr, xi, lr, li)
            return nr + wr_ref[idx, :], ni + wi_ref[idx, :]
        return step

    zero = jnp.zeros((n_batch, S5_PP), F32)
    carry = lax.fori_loop(0, n_ctx, make_step(n_batch * n_lat, n_ctx, n_ctx), (zero, zero))
    lax.fori_loop(0, n_lat, make_step(0, n_lat, n_lat), carry)
    x_prev = jnp.concatenate([xr_ref[...], xi_ref[...]], axis=1).astype(BF16)
    y = _dot(u, m_ref[...]) + _dot(x_prev, ys_ref[...])

    @pl.when(d == 0)
    def _():
        for h in range(S5_HG):
            o_ref[h] = y[:, h * t:(h + 1) * t]

    @pl.when(d == 1)
    def _():
        for h in range(S5_HG):
            o_ref[h] += y[:, h * t:(h + 1) * t]


def _s5_params(a_re, a_im, log_dt, b_re, b_im, c_re, c_im):
    a_re, a_im = a_re.astype(F32), a_im.astype(F32)
    dt = jnp.exp(log_dt.astype(F32))[..., None]
    mag = jnp.exp(a_re * dt)
    lam_re, lam_im = mag * jnp.cos(a_im * dt), mag * jnp.sin(a_im * dt)
    den = a_re * a_re + a_im * a_im
    num_re, num_im = lam_re - 1.0, lam_im
    fr = (num_re * a_re + num_im * a_im) / den
    fi = (num_im * a_re - num_re * a_im) / den
    b_re, b_im = b_re.astype(F32), b_im.astype(F32)
    bb_re = fr[..., None] * b_re - fi[..., None] * b_im
    bb_im = fr[..., None] * b_im + fi[..., None] * b_re
    c_re_t = jnp.swapaxes(c_re.astype(F32), -1, -2)
    c_im_t = jnp.swapaxes(c_im.astype(F32), -1, -2)
    col = jnp.concatenate([lam_re[..., None], lam_im[..., None], c_re_t, c_im_t], axis=-1)
    col = jnp.pad(col, ((0, 0),) * 3 + ((0, S5_PP - S5_P), (0, 128 - col.shape[-1])))
    row = jnp.concatenate([jnp.swapaxes(bb_re, -1, -2), jnp.swapaxes(bb_im, -1, -2),
                           lam_re[..., None, :], lam_im[..., None, :]], axis=-2)
    row = jnp.pad(row, ((0, 0),) * 3 + ((0, S5_ROWS - row.shape[-2]), (0, S5_PP - S5_P)))
    return col, row


def _s5_scan(p, col, row, layer, *, n_batch, seq, ctx_len):
    t = S5_T
    n = p.shape[0]
    nc = n // t
    assert ctx_len % t == 0 and seq % t == 0, (ctx_len, seq, t)
    n_ctx, n_lat = ctx_len // t, seq // t
    ut = p[:, COL_SU * BR:(COL_SU + 1) * BR].T.reshape(S5_G, S5_HG, nc, t)
    kern = functools.partial(_s5_kernel, n_batch=n_batch, n_ctx=n_ctx, n_lat=n_lat)
    yt = pl.pallas_call(
        kern,
        out_shape=jax.ShapeDtypeStruct((S5_G, S5_HG, nc, t), F32),
        grid=(S5_G, 2),
        in_specs=[pl.BlockSpec((None, S5_HG, nc, t), lambda g, d: (g, 0, 0, 0)),
                  pl.BlockSpec((None, None, None, S5_PP, 128), lambda g, d: (layer, d, g, 0, 0)),
                  pl.BlockSpec((None, None, None, S5_ROWS, S5_PP), lambda g, d: (layer, d, g, 0, 0))],
        out_specs=pl.BlockSpec((None, S5_HG, nc, t), lambda g, d: (g, 0, 0, 0)),
        scratch_shapes=[pltpu.VMEM((S5_HG * t, S5_HG * t), BF16),
                        pltpu.VMEM((S5_HG * t, 2 * S5_PP), BF16),
                        pltpu.VMEM((2 * S5_PP, S5_HG * t), BF16),
                        pltpu.VMEM((2 * S5_PP, S5_HG * t), F32),
                        pltpu.VMEM((nc, S5_PP), F32), pltpu.VMEM((nc, S5_PP), F32),
                        pltpu.VMEM((nc, S5_PP), F32), pltpu.VMEM((nc, S5_PP), F32)],
        compiler_params=_cparams(("parallel", "arbitrary")),
        name="s5_scan",
    )(ut, col, row)
    return yt.reshape(BR, n).T


def _head_rms(x, g):
    parts = []
    for h in range(NH):
        xh = x[:, h * HD:(h + 1) * HD]
        parts.append(xh * lax.rsqrt(jnp.mean(jnp.square(xh), axis=-1, keepdims=True) + NORM_EPS))
    return jnp.concatenate(parts, axis=-1) * g


def _finish_kernel(raf_ref, rab_ref, rbf_ref, rbb_ref, rc_ref, rdf_ref, rdb_ref, ga_ref, gb_ref, u_ref, gd_ref,
                   mg_ref, x_ref, mod_ref, vec_ref, glu_w_ref, wb_ref, wo_ref, o_ref):
    vec = vec_ref[...]
    oa = _head_rms(raf_ref[...] + rab_ref[...], vec[0:1]) * _silu(ga_ref[...].astype(F32))
    ob = _head_rms(rbf_ref[...] + rbb_ref[...], vec[1:2]) * _silu(gb_ref[...].astype(F32))
    od = _head_rms(rdf_ref[...] + rdb_ref[...], vec[2:3]) * _silu(gd_ref[...].astype(F32))
    yc = rc_ref[...] + vec[3:4] * u_ref[...].astype(F32)
    yc = 0.5 * yc * (1.0 + lax.erf(yc * (2.0 ** -0.5)))
    oc = yc * _sigmoid(_dot(yc.astype(BF16), glu_w_ref[...]) + vec[4:5])
    y = None
    for j, o in enumerate((oa, ob, oc, od)):
        gate = _sigmoid_tanh(mg_ref[:, j * D_MODEL:(j + 1) * D_MODEL].astype(F32))
        term = gate * _dot(o.astype(BF16), wb_ref[j])
        y = term if y is None else y + term
    mix = _dot(y.astype(BF16), wo_ref[...])
    o_ref[...] = x_ref[...] + mod_ref[2:3, :] * mix


def _finish(raw_a, raw_b, raw_c, raw_d, p, x_all, mod, vec, glu_w, w_branch, w_out, layer, *, n_rows, n_batch,
            seq, tm=256):
    mrow = _mod_row_map(n_rows, tm, n_batch, seq)
    one = pl.Buffered(1)

    def pcol(cblock):
        return pl.BlockSpec((tm, BR), lambda i: (i, cblock))

    raw = pl.BlockSpec((tm, BR), lambda i: (i, 0))
    return pl.pallas_call(
        _finish_kernel,
        out_shape=jax.ShapeDtypeStruct((n_rows, D_MODEL), F32),
        grid=(n_rows // tm,),
        in_specs=[raw, raw, raw, raw, raw, raw, raw,
                  pcol(COL_HG), pcol(COL_RG), pcol(COL_SU), pcol(COL_MZ),
                  pl.BlockSpec((tm, N_BRANCH * D_MODEL), lambda i: (i, COL_MERGE)),
                  pl.BlockSpec((tm, D_MODEL), lambda i: (i, 0)),
                  pl.BlockSpec((None, 8, D_MODEL), lambda i: (mrow(i), 0, 0)),
                  pl.BlockSpec((None, 8, BR), lambda i: (layer, 0, 0), pipeline_mode=one),
                  pl.BlockSpec((None, BR, BR), lambda i: (layer, 0, 0), pipeline_mode=one),
                  pl.BlockSpec((None, N_BRANCH, BR, D_MODEL), lambda i: (layer, 0, 0, 0), pipeline_mode=one),
                  pl.BlockSpec((None, D_MODEL, D_MODEL), lambda i: (layer, 0, 0), pipeline_mode=one)],
        out_specs=pl.BlockSpec((tm, D_MODEL), lambda i: (i, 0)),
        compiler_params=_cparams(("parallel",)),
        name="finish",
    )(*raw_a, *raw_b, raw_c, *raw_d, p, p, p, p, p, x_all, mod, vec, glu_w, w_branch, w_out)


def _mlp_kernel(x_ref, mod_ref, g_ref, w1_ref, w2_ref, fin_ref, o_ref, xn_ref, acc_ref, *, final_norm):
    j = pl.program_id(1)

    @pl.when(j == 0)
    def _():
        h = _norm_mod(x_ref[...], g_ref[...], mod_ref[3:4, :], mod_ref[4:5, :])
        xn_ref[...] = h.astype(BF16)
        acc_ref[...] = jnp.zeros_like(acc_ref)

    a = jnp.square(jnp.maximum(_dot(xn_ref[...], w1_ref[...]), 0.0))
    acc_ref[...] += _dot(a.astype(BF16), w2_ref[...])

    @pl.when(j == pl.num_programs(1) - 1)
    def _():
        y = x_ref[...] + mod_ref[5:6, :] * acc_ref[...]
        if final_norm:
            y = (y * lax.rsqrt(jnp.mean(jnp.square(y), axis=-1, keepdims=True) + NORM_EPS)) * fin_ref[...]
        o_ref[...] = y


def _mlp(x_all, mod, norm_w, w1, w2, fin_w, layer, *, n_rows, n_batch, seq, final_norm, tm=512, tf=1024):
    mrow = _mod_row_map(n_rows, tm, n_batch, seq)
    assert D_FF % tf == 0, tf
    kern = functools.partial(_mlp_kernel, final_norm=final_norm)
    return pl.pallas_call(
        kern,
        out_shape=jax.ShapeDtypeStruct((n_rows, D_MODEL), F32),
        grid=(n_rows // tm, D_FF // tf),
        in_specs=[pl.BlockSpec((tm, D_MODEL), lambda i, j: (i, 0)),
                  pl.BlockSpec((None, 8, D_MODEL), lambda i, j: (mrow(i), 0, 0)),
                  pl.BlockSpec((None, 1, D_MODEL), lambda i, j: (layer, 0, 0)),
                  pl.BlockSpec((None, D_MODEL, tf), lambda i, j: (layer, 0, j)),
                  pl.BlockSpec((None, tf, D_MODEL), lambda i, j: (layer, j, 0)),
                  pl.BlockSpec((1, D_MODEL), lambda i, j: (0, 0))],
        out_specs=pl.BlockSpec((tm, D_MODEL), lambda i, j: (i, 0)),
        scratch_shapes=[pltpu.VMEM((tm, D_MODEL), BF16), pltpu.VMEM((tm, D_MODEL), F32)],
        compiler_params=_cparams(("parallel", "arbitrary")),
        name="mlp",
    )(x_all, mod, norm_w, w1, w2, fin_w.reshape(1, D_MODEL))


W_PREP_ROWS = 1024
N_MERGE_BLK = N_BRANCH * D_MODEL // W_PREP_ROWS
MERGE_OFF = GATE_OFF + 4 * NH


def _w_prep_kernel(a_ref, g_ref, main_ref, gate_ref):
    @pl.when(pl.program_id(1) == 0)
    def _():
        gate_ref[...] = jnp.zeros_like(gate_ref)
        gate_ref[0:4 * NH, :] = g_ref[0].astype(BF16)

    main_ref[...] = a_ref[0].astype(BF16)


def _split_w_in(w):
    depth = w.shape[0]
    wt = jnp.swapaxes(w, 1, 2)
    tr = W_PREP_ROWS

    def a_map(l, j):
        row = jnp.where(j < N_MERGE_BLK, MERGE_OFF + j * tr, (j - N_MERGE_BLK) * tr)
        return l, pl.multiple_of(row, 16), 0

    return pl.pallas_call(
        _w_prep_kernel,
        out_shape=(jax.ShapeDtypeStruct((depth, P_WIDTH, D_MODEL), BF16),
                   jax.ShapeDtypeStruct((depth, GATE_PAD, D_MODEL), BF16)),
        grid=(depth, P_WIDTH // tr),
        in_specs=[pl.BlockSpec((pl.Element(1), pl.Element(tr), pl.Element(D_MODEL)), a_map),
                  pl.BlockSpec((pl.Element(1), pl.Element(4 * NH), pl.Element(D_MODEL)),
                               lambda l, j: (l, GATE_OFF, 0))],
        out_specs=(pl.BlockSpec((None, tr, D_MODEL), lambda l, j: (l, j, 0)),
                   pl.BlockSpec((None, GATE_PAD, D_MODEL), lambda l, j: (l, 0, 0))),
        compiler_params=_cparams(("parallel", "arbitrary")),
        name="w_in_prep",
    )(wt, wt)


def kernel(x, c, ctx, c_ctx, w_mod, b_mod, norm_mix, norm_mlp, w_in, hgrn_lb_logits, hgrn_norm, ret_decay, ret_norm, s5_a_re, s5_a_im, s5_log_dt, s5_b_re, s5_b_im, s5_c_re, s5_c_im, s5_d, s5_glu_w, s5_glu_b, mlstm_conv_w, mlstm_conv_b, mlstm_gate_b, mlstm_norm, w_branch, w_out, w_ff1, w_ff2, final_norm):
    n_batch, seq, _ = x.shape
    ctx_len = ctx.shape[1]
    depth = w_in.shape[0]
    nl = n_batch * seq
    dims = dict(n_batch=n_batch, seq=seq)

    p_lb = jax.nn.softmax(hgrn_lb_logits.astype(F32), axis=0)
    lower_bounds = (jnp.cumsum(p_lb, axis=0) - p_lb[0]).reshape(depth, 2, 1, BR)
    lg = jnp.log1p(-jnp.exp(ret_decay.astype(F32)))
    gate_b = jnp.pad(mlstm_gate_b.reshape(depth, 1, 4 * NH).astype(F32), ((0, 0), (0, 0), (0, GATE_PAD - 4 * NH)))
    vec = jnp.stack([hgrn_norm, ret_norm, mlstm_norm, s5_d, s5_glu_b], axis=1).astype(F32)
    vec = jnp.pad(vec, ((0, 0), (0, 8 - vec.shape[1]), (0, 0)))
    w_main, w_gate = _split_w_in(w_in)
    glu_w, wb, wo = s5_glu_w.astype(BF16), w_branch.astype(BF16), w_out.astype(BF16)
    w1, w2 = w_ff1.astype(BF16), w_ff2.astype(BF16)
    norm_mix3 = norm_mix.reshape(depth, 1, D_MODEL)
    norm_mlp3 = norm_mlp.reshape(depth, 1, D_MODEL)
    conv_w = mlstm_conv_w.reshape(depth, 9, 2 * BR)
    conv_b = mlstm_conv_b.reshape(depth, 1, 2 * BR)
    s5_col, s5_row = _s5_params(s5_a_re, s5_a_im, s5_log_dt, s5_b_re, s5_b_im, s5_c_re, s5_c_im)

    c_pad = jnp.zeros((8, D_MODEL), F32).at[:n_batch].set(c).at[n_batch].set(c_ctx)
    x_all = jnp.concatenate([x.reshape(nl, D_MODEL), ctx.reshape(n_batch * ctx_len, D_MODEL)], axis=0)

    for l in range(depth):
        last = l == depth - 1
        mod = _modulation(c_pad, w_mod, b_mod, l)
        mod = jnp.pad(mod[:n_batch + 1].reshape(n_batch + 1, N_MOD, D_MODEL), ((0, 0), (0, 8 - N_MOD), (0, 0)))
        p, gates = _in_proj(x_all, mod, norm_mix3, w_main, w_gate, l, **dims)
        qk = _mlstm_conv(p, conv_w, conv_b, l, ctx_len=ctx_len, **dims)
        raw_a, raw_b, raw_d = _scans(p, gates, qk, lg[l], lower_bounds, gate_b, l, ctx_len=ctx_len, **dims)
        raw_c = _s5_scan(p, s5_col, s5_row, l, ctx_len=ctx_len, **dims)
        n_rows = nl if last else x_all.shape[0]
        x_mid = _finish(raw_a, raw_b, raw_c, raw_d, p, x_all, mod, vec, glu_w, wb, wo, l, n_rows=n_rows, **dims)
        x_all = _mlp(x_mid, mod, norm_mlp3, w1, w2, final_norm, l, n_rows=n_rows, final_norm=last, **dims)
    return x_all.reshape(n_batch, seq, D_MODEL)
```

```python
import functools
import math

import jax
import jax.numpy as jnp
from jax import lax
from jax.experimental import pallas as pl
from jax.experimental.pallas import tpu as pltpu

F32 = jnp.float32
BF16 = jnp.bfloat16

D_MODEL = 2048
N_BRANCH = 4
BR = D_MODEL // N_BRANCH
HD = 128
NH = BR // HD
S5_HG = 16
S5_G = BR // S5_HG
S5_P = 64
D_FF = 4 * D_MODEL
N_MOD = 6
GRID_W = 64
NORM_EPS = 1e-6
NEG_BIG = -1e30
F_TINY = 1e-30
S5_DT_MIN = 1e-3

COL_MERGE = 0
COL_HQ, COL_HF, COL_HV, COL_HG = 16, 17, 19, 20
COL_RQ, COL_RK, COL_RV, COL_RG = 21, 22, 23, 24
COL_SU = 25
COL_MQK, COL_MV, COL_MZ = 26, 28, 29
P_WIDTH = 30 * BR
GATE_OFF = 14 * BR
GATE_PAD = 128

T_HGRN = 128
T_ATTN = 256
S5_T = 128
S5_PP = 128
S5_ROWS = 40
VMEM_LIMIT = 56 * 1024 * 1024

_HI = lax.Precision.HIGHEST


def _cparams(sem):
    return pltpu.CompilerParams(dimension_semantics=sem, vmem_limit_bytes=VMEM_LIMIT)


def _dot(a, b):
    return jnp.dot(a, b, preferred_element_type=F32)


def _dot_nt(a, b):
    return lax.dot_general(a, b, (((1,), (1,)), ((), ())), preferred_element_type=F32)


def _dot_tn(a, b):
    return lax.dot_general(a, b, (((0,), (0,)), ((), ())), preferred_element_type=F32)


def _split3(x):
    hi = x.astype(BF16)
    r1 = x - hi.astype(F32)
    mid = r1.astype(BF16)
    return hi, mid, (r1 - mid.astype(F32)).astype(BF16)


def _split2(x):
    hi = x.astype(BF16)
    return hi, (x - hi.astype(F32)).astype(BF16)


def _cumsum_rows(tri, x):
    hi, mid = _split2(x)
    return _dot(tri, hi) + _dot(tri, mid)


def _sigmoid(x):
    return 1.0 / (1.0 + jnp.exp(-x))


def _sigmoid_tanh(x):
    return 0.5 * jnp.tanh(0.5 * x) + 0.5


def _silu(x):
    return x * _sigmoid(x)


def _mod_kernel(c_ref, w_ref, b_ref, o_ref):
    @pl.when(pl.program_id(0) == 0)
    def _():
        o_ref[...] = jnp.broadcast_to(b_ref[...], o_ref.shape)

    a = _silu(c_ref[...]).astype(BF16)
    o_ref[...] += _dot(a, w_ref[...].astype(BF16))


def _modulation(c_pad, w_mod, b_mod, layer, tk=256):
    n = w_mod.shape[2]
    return pl.pallas_call(
        _mod_kernel,
        out_shape=jax.ShapeDtypeStruct((8, n), F32),
        grid=(D_MODEL // tk,),
        in_specs=[pl.BlockSpec((8, tk), lambda k: (0, k)),
                  pl.BlockSpec((None, tk, n), lambda k: (layer, k, 0)),
                  pl.BlockSpec((None, 1, n), lambda k: (layer, 0, 0))],
        out_specs=pl.BlockSpec((8, n), lambda k: (0, 0)),
        compiler_params=_cparams(("arbitrary",)),
        name="modulation",
    )(c_pad, w_mod, b_mod.reshape(b_mod.shape[0], 1, n))


def _norm_mod(x, g, shift, scale):
    y = x * lax.rsqrt(jnp.mean(jnp.square(x), axis=-1, keepdims=True) + NORM_EPS)
    return (y * g) * (1.0 + scale) + shift


def _inproj_kernel(x_ref, mod_ref, g_ref, w_ref, wg_ref, p_ref, gate_ref, xn_ref):
    @pl.when(pl.program_id(1) == 0)
    def _():
        h = _norm_mod(x_ref[...], g_ref[...], mod_ref[0:1, :], mod_ref[1:2, :])
        xn_ref[...] = h.astype(BF16)
        gate_ref[...] = _dot_nt(xn_ref[...], wg_ref[...])

    p_ref[...] = _dot_nt(xn_ref[...], w_ref[...]).astype(p_ref.dtype)


def _mod_row_map(n_rows, tm, n_batch, seq):
    assert n_rows % tm == 0 and seq % tm == 0, (n_rows, seq, tm)
    n_lat_tiles, tiles_per_batch = n_batch * seq // tm, seq // tm

    def row(i):
        return jnp.where(i < n_lat_tiles, i // tiles_per_batch, n_batch)
    return row


def _in_proj(x_all, mod, norm_w, w_main, w_gate, layer, *, n_batch, seq, tm=1024, tn=1536):
    n = x_all.shape[0]
    assert P_WIDTH % tn == 0, tn
    mrow = _mod_row_map(n, tm, n_batch, seq)
    return pl.pallas_call(
        _inproj_kernel,
        out_shape=(jax.ShapeDtypeStruct((n, P_WIDTH), BF16),
                   jax.ShapeDtypeStruct((n, GATE_PAD), F32)),
        grid=(n // tm, P_WIDTH // tn),
        in_specs=[pl.BlockSpec((tm, D_MODEL), lambda i, j: (i, 0)),
                  pl.BlockSpec((None, 8, D_MODEL), lambda i, j: (mrow(i), 0, 0)),
                  pl.BlockSpec((None, 1, D_MODEL), lambda i, j: (layer, 0, 0)),
                  pl.BlockSpec((None, tn, D_MODEL), lambda i, j: (layer, j, 0)),
                  pl.BlockSpec((None, GATE_PAD, D_MODEL), lambda i, j: (layer, 0, 0))],
        out_specs=(pl.BlockSpec((tm, tn), lambda i, j: (i, j)),
                   pl.BlockSpec((tm, GATE_PAD), lambda i, j: (i, 0))),
        scratch_shapes=[pltpu.VMEM((tm, D_MODEL), BF16)],
        compiler_params=_cparams(("parallel", "arbitrary")),
        name="in_proj",
    )(x_all, mod, norm_w, w_main, w_gate)


CONV_ROWS = 256


def _conv_kernel(up_ref, x_ref, dn_ref, w_ref, b_ref, o_ref, *, n_lat_blocks, blocks_per_img):
    rb = pl.program_id(0)
    x = x_ref[...].astype(F32)
    w = w_ref[...]
    n = CONV_ROWS

    @pl.when(rb < n_lat_blocks)
    def _():
        rr = rb % blocks_per_img
        up = jnp.where(rr > 0, up_ref[...].astype(F32), 0.0)
        dn = jnp.where(rr < blocks_per_img - 1, dn_ref[...].astype(F32), 0.0)
        ext = jnp.concatenate([up, x, dn], axis=0)
        ne = n + 2 * GRID_W
        col = lax.broadcasted_iota(jnp.int32, (ne, 1), 0) % GRID_W
        xl = jnp.where(col >= 1, pltpu.roll(ext, 1, 0), 0.0)
        xr = jnp.where(col <= GRID_W - 2, pltpu.roll(ext, ne - 1, 0), 0.0)
        acc = None
        for i in range(3):
            y = w[3 * i:3 * i + 1] * xl + w[3 * i + 1:3 * i + 2] * ext + w[3 * i + 2:3 * i + 3] * xr
            part = y[i * GRID_W:i * GRID_W + n]
            acc = part if acc is None else acc + part
        o_ref[...] = _silu(acc + b_ref[...]).astype(o_ref.dtype)

    @pl.when(rb >= n_lat_blocks)
    def _():
        t = lax.broadcasted_iota(jnp.int32, (n, 1), 0)
        xl = jnp.where(t >= 1, pltpu.roll(x, 1, 0), 0.0)
        xr = jnp.where(t <= n - 2, pltpu.roll(x, n - 1, 0), 0.0)
        acc = w[3:4] * xl + w[4:5] * x + w[5:6] * xr
        o_ref[...] = _silu(acc + b_ref[...]).astype(o_ref.dtype)


def _mlstm_conv(p, conv_w, conv_b, layer, *, n_batch, seq, ctx_len, cb=2 * BR):
    assert ctx_len == CONV_ROWS and seq % CONV_ROWS == 0
    n = p.shape[0]
    n_blocks = n // CONV_ROWS
    hb = CONV_ROWS // GRID_W
    n_halo = n // GRID_W
    c0 = COL_MQK * BR // cb
    kern = functools.partial(_conv_kernel, n_lat_blocks=n_batch * seq // CONV_ROWS,
                             blocks_per_img=seq // CONV_ROWS)
    return pl.pallas_call(
        kern,
        out_shape=jax.ShapeDtypeStruct((n, 2 * BR), BF16),
        grid=(n_blocks, 2 * BR // cb),
        in_specs=[pl.BlockSpec((GRID_W, cb), lambda r, c: (jnp.maximum(r * hb - 1, 0), c0 + c)),
                  pl.BlockSpec((CONV_ROWS, cb), lambda r, c: (r, c0 + c)),
                  pl.BlockSpec((GRID_W, cb), lambda r, c: (jnp.minimum((r + 1) * hb, n_halo - 1), c0 + c)),
                  pl.BlockSpec((None, 9, cb), lambda r, c: (layer, 0, c)),
                  pl.BlockSpec((None, 1, cb), lambda r, c: (layer, 0, c))],
        out_specs=pl.BlockSpec((CONV_ROWS, cb), lambda r, c: (r, c)),
        compiler_params=_cparams(("parallel", "parallel")),
        name="mlstm_conv",
    )(p, p, p, conv_w, conv_b)


def _chunk_block(b, d, i, *, n_ctx, n_lat, n_batch):
    ctx_j = i if d == 0 else n_ctx - 1 - i
    lat_j = i - n_ctx if d == 0 else n_ctx + n_lat - 1 - i
    return jnp.where(i < n_ctx, n_batch * n_lat + b * n_ctx + ctx_j, b * n_lat + lat_j)


def _order_mask(d, t):
    r = lax.broadcasted_iota(jnp.int32, (t, t), 0)
    c = lax.broadcasted_iota(jnp.int32, (t, t), 1)
    return (c <= r) if d == 0 else (c >= r)


def _init_at_first_step(*refs):
    @pl.when(pl.program_id(1) == 0)
    def _():
        for ref in refs:
            ref[...] = jnp.zeros_like(ref)


def _ret_direction(d, lg_ref, q_ref, k_ref, v_ref, o_ref, s_ref):
    t = T_ATTN
    r = lax.broadcasted_iota(jnp.int32, (t, t), 0)
    c = lax.broadcasted_iota(jnp.int32, (t, t), 1)
    rel = (r - c) if d == 0 else (c - r)
    relf = jnp.maximum(rel, 0).astype(F32)
    tt = lax.broadcasted_iota(jnp.int32, (t, 1), 0)
    pos = (tt if d == 0 else t - 1 - tt).astype(F32)
    scale = HD ** -0.5
    for h in range(NH):
        sl = slice(h * HD, (h + 1) * HD)
        lg = lg_ref[d, h]
        q = q_ref[:, sl]
        k = k_ref[:, sl]
        v = v_ref[:, sl]
        intra = jnp.where(rel >= 0, jnp.exp(lg * relf), 0.0)
        scores = (_dot_nt(q, k) * scale) * intra
        s_old = s_ref[d, h]
        out = _dot(scores.astype(BF16), v) + jnp.exp(lg * (pos + 1.0)) * _dot(q, s_old.astype(BF16))
        kd = (k.astype(F32) * (scale * jnp.exp(lg * (t - 1.0 - pos)))).astype(BF16)
        s_ref[d, h] = jnp.exp(lg * t) * s_old + _dot_tn(kd, v)
        o_ref[:, sl] = out


def _ret_kernel(lg_ref, qf_ref, kf_ref, vf_ref, qb_ref, kb_ref, vb_ref, of_ref, ob_ref, s_ref):
    _init_at_first_step(s_ref)
    _ret_direction(0, lg_ref, qf_ref, kf_ref, vf_ref, of_ref, s_ref)
    _ret_direction(1, lg_ref, qb_ref, kb_ref, vb_ref, ob_ref, s_ref)


def _hgrn_direction(d, lb_ref, q_ref, z_ref, v_ref, o_ref, st_ref):
    t = T_HGRN
    fwd = d == 0
    lb = lb_ref[...]
    z = z_ref[...].astype(F32)
    sig = _sigmoid(z)
    f = lb + (1.0 - lb) * sig
    logf = jnp.log(jnp.maximum(f, F_TINY))
    kk = (1.0 - lb) * (1.0 - sig)
    qq = _silu(q_ref[...].astype(F32))
    r2 = lax.broadcasted_iota(jnp.int32, (t, t), 0)
    c2 = lax.broadcasted_iota(jnp.int32, (t, t), 1)
    tri = jnp.where((c2 <= r2) if fwd else (c2 >= r2), 1.0, 0.0).astype(BF16)
    cum = _cumsum_rows(tri, logf)
    xdiff = jnp.where((c2 < r2) if fwd else (c2 > r2), r2 ^ c2, 0)
    row = lax.broadcasted_iota(jnp.int32, (t, 1), 0)
    if fwd:
        e_ref = jnp.where(row == 0, 0.0, pltpu.roll(cum, 1, 0))
    else:
        e_ref = jnp.where(row == t - 1, 0.0, pltpu.roll(cum, t - 1, 0))
    f_ref = cum
    levels = []
    w = 1
    while w < t:
        q_l = (qq * jnp.exp(cum - e_ref)).astype(BF16)
        k_l = (kk * jnp.exp(f_ref - cum)).astype(BF16)
        levels.append((q_l, k_l, (xdiff >> (w.bit_length() - 1)) == 1))
        upper = (row & w) != 0
        if fwd:
            e_ref = jnp.where(upper, pltpu.roll(e_ref, w, 0), e_ref)
            f_ref = jnp.where(upper, f_ref, pltpu.roll(f_ref, t - w, 0))
        else:
            e_ref = jnp.where(upper, e_ref, pltpu.roll(e_ref, t - w, 0))
            f_ref = jnp.where(upper, pltpu.roll(f_ref, w, 0), f_ref)
        w *= 2
    q_b = qq.astype(BF16)
    k_b = kk.astype(BF16)
    tail = cum[t - 1:t] if fwd else cum[0:1]
    q_state = (qq * jnp.exp(cum)).astype(BF16)
    k_state = (kk * jnp.exp(tail - cum)).astype(BF16)
    decay = jnp.exp(tail)
    for h in range(NH):
        sl = slice(h * HD, (h + 1) * HD)
        v_h = v_ref[:, sl]
        scores = jnp.where(r2 == c2, _dot_nt(q_b[:, sl], k_b[:, sl]), 0.0)
        for q_l, k_l, m in levels:
            scores = jnp.where(m, _dot_nt(q_l[:, sl], k_l[:, sl]), scores)
        st = st_ref[d, h]
        o_ref[:, sl] = _dot(scores.astype(BF16), v_h) + _dot_nt(q_state[:, sl], st.astype(BF16))
        st_ref[d, h] = st * decay[:, sl] + _dot_tn(v_h, k_state[:, sl])


def _hgrn_kernel(lbf_ref, qf_ref, zf_ref, vf_ref, lbb_ref, qb_ref, zb_ref, vb_ref, of_ref, ob_ref, st_ref):
    _init_at_first_step(st_ref)
    _hgrn_direction(0, lbf_ref, qf_ref, zf_ref, vf_ref, of_ref, st_ref)
    _hgrn_direction(1, lbb_ref, qb_ref, zb_ref, vb_ref, ob_ref, st_ref)


def _log_sigmoid(x):
    return jnp.minimum(x, 0.0) - jnp.log(1.0 + jnp.exp(-jnp.abs(x)))


def _rep_lane(parts, lane, width):
    sel = jnp.where(lax.broadcasted_iota(jnp.int32, (GATE_PAD, width), 0) == lane, 1.0, 0.0).astype(BF16)
    hi, mid = parts
    return _dot(hi, sel) + _dot(mid, sel)


def _mlstm_direction(d, qk_ref, v_ref, g_ref, gb_ref, o_ref, c_ref, m_ref):
    t = T_ATTN
    assert t == 2 * HD
    fwd = d == 0
    g = g_ref[...] + gb_ref[...]
    r2 = lax.broadcasted_iota(jnp.int32, (t, t), 0)
    c2 = lax.broadcasted_iota(jnp.int32, (t, t), 1)
    mask = (c2 <= r2) if fwd else (c2 >= r2)
    tri = jnp.where(mask, 1.0, 0.0).astype(BF16)
    eye = jnp.where(r2 == c2, 1.0, 0.0).astype(BF16)
    lf = _log_sigmoid(g)
    cum_c = _cumsum_rows(tri, lf)
    a_col = g - pltpu.roll(cum_c, GATE_PAD - NH, 1)
    row = lax.broadcasted_iota(jnp.int32, (t, 1), 0)
    pm = a_col
    k = 1
    while k < t:
        if fwd:
            pm = jnp.maximum(pm, jnp.where(row >= k, pltpu.roll(pm, k, 0), NEG_BIG))
        else:
            pm = jnp.maximum(pm, jnp.where(row < t - k, pltpu.roll(pm, t - k, 0), NEG_BIG))
        k *= 2
    a_parts, pm_parts, cum_parts = _split2(a_col), _split2(pm), _split2(cum_c)
    a_rows = _dot_tn(a_parts[0], eye) + _dot_tn(a_parts[1], eye)
    scale = HD ** -0.5
    last = t - 1 if fwd else 0
    ones = jnp.ones((t, HD), BF16)
    for h in range(NH):
        sl = slice(h * HD, (h + 1) * HD)
        li, lfw = 2 * NH * d + h, 2 * NH * d + NH + h
        m_old = m_ref[d, h]
        mx2 = jnp.maximum(_rep_lane(pm_parts, li, 2 * HD), jnp.concatenate([m_old, m_old], axis=1))
        mx = mx2[:, 0:HD]
        cc = _rep_lane(cum_parts, lfw, HD)
        a_rep = _rep_lane(a_parts, li, HD)
        a_row = a_rows[li:li + 1]
        q = qk_ref[:, sl]
        k_h = qk_ref[:, BR + h * HD:BR + (h + 1) * HD]
        v_aug = jnp.concatenate([v_ref[:, sl], ones], axis=1)
        w = jnp.where(mask, jnp.exp(a_row - mx2), 0.0)
        scores = (_dot_nt(q, k_h) * scale) * w
        w_state = jnp.exp(m_old - mx)
        c_old = c_ref[d, h]
        s_hi = scores.astype(BF16)
        s_mid = (scores - s_hi.astype(F32)).astype(BF16)
        intra = _dot(s_hi, v_aug)
        inter = _dot(q, c_old.astype(BF16))
        num = intra[:, 0:HD] + w_state * inter[:, 0:HD]
        den = (intra[:, HD:2 * HD] + _dot(s_mid, ones)) + w_state * inter[:, HD:2 * HD]
        o_ref[:, sl] = num / jnp.maximum(jnp.abs(den), jnp.exp(-(cc + mx)))
        total = cc[last:last + 1]
        m_new_rel = mx[last:last + 1]
        keep = jnp.exp(m_old - m_new_rel)
        w_end = jnp.exp(a_rep - m_new_rel)
        kw = (k_h.astype(F32) * (scale * w_end)).astype(BF16)
        c_ref[d, h] = jnp.concatenate([keep, keep], axis=1) * c_old + _dot_tn(kw, v_aug)
        m_ref[d, h] = total + m_new_rel


def _mlstm_kernel(qkf_ref, vf_ref, gf_ref, qkb_ref, vb_ref, gb_ref, bias_ref, of_ref, ob_ref, c_ref, m_ref):
    _init_at_first_step(c_ref, m_ref)
    _mlstm_direction(0, qkf_ref, vf_ref, gf_ref, bias_ref, of_ref, c_ref, m_ref)
    _mlstm_direction(1, qkb_ref, vb_ref, gb_ref, bias_ref, ob_ref, c_ref, m_ref)


def _scans(p, gates, qk, lg, lb, gate_b, layer, *, n_batch, seq, ctx_len):
    n = p.shape[0]
    state = pltpu.VMEM((2, NH, HD, HD), F32)
    vec = pltpu.VMEM((2, NH, 1, HD), F32)

    def call(kernel, t, inputs, make_specs, scratch, name):
        assert ctx_len % t == 0 and seq % t == 0, (ctx_len, seq, t)
        n_ctx, n_lat = ctx_len // t, seq // t

        def rows(d, width, cblock=0):
            return pl.BlockSpec((t, width), lambda b, i: (_chunk_block(b, d, i, n_ctx=n_ctx, n_lat=n_lat,
                                                                        n_batch=n_batch), cblock))

        out = jax.ShapeDtypeStruct((n, BR), F32)
        return pl.pallas_call(
            kernel,
            out_shape=(out, out),
            grid=(n_batch, n_ctx + n_lat),
            in_specs=make_specs(rows),
            out_specs=(rows(0, BR), rows(1, BR)),
            scratch_shapes=scratch,
            compiler_params=_cparams(("parallel", "arbitrary")),
            name=name,
        )(*inputs)

    raw_b = call(_ret_kernel, T_ATTN, (lg, p, p, p, p, p, p),
                 lambda rows: [pl.BlockSpec(memory_space=pltpu.SMEM)]
                 + [rows(d, BR, c) for d in (0, 1) for c in (COL_RQ, COL_RK, COL_RV)],
                 [state], "retention_scan")
    lb_spec = lambda d: pl.BlockSpec((None, None, 1, BR), lambda b, i: (layer, d, 0, 0))
    raw_a = call(_hgrn_kernel, T_HGRN, (lb, p, p, p, lb, p, p, p),
                 lambda rows: [spec for d in (0, 1)
                               for spec in (lb_spec(d), rows(d, BR, COL_HQ), rows(d, BR, COL_HF + d),
                                            rows(d, BR, COL_HV))],
                 [state], "hgrn2_scan")
    raw_d = call(_mlstm_kernel, T_ATTN, (qk, p, gates, qk, p, gates, gate_b),
                 lambda rows: [spec for d in (0, 1)
                               for spec in (rows(d, 2 * BR), rows(d, BR, COL_MV), rows(d, GATE_PAD))]
                 + [pl.BlockSpec((None, 1, GATE_PAD), lambda b, i: (layer, 0, 0))],
                 [pltpu.VMEM((2, NH, HD, 2 * HD), F32), vec], "mlstm_scan")
    return raw_a, raw_b, raw_d


def _cmul(ar, ai, br, bi):
    return ar * br - ai * bi, ar * bi + ai * br


def _pow_table(lam_r, lam_i, expo):
    pr = jnp.ones(expo.shape, F32)
    pi = jnp.zeros(expo.shape, F32)
    ar = jnp.broadcast_to(lam_r, expo.shape)
    ai = jnp.broadcast_to(lam_i, expo.shape)
    for k in range(S5_T.bit_length() - 1):
        nr, ni = _cmul(pr, pi, ar, ai)
        bit = (expo & (1 << k)) != 0
        pr, pi = jnp.where(bit, nr, pr), jnp.where(bit, ni, pi)
        ar, ai = _cmul(ar, ai, ar, ai)
    return pr, pi


def _s5_kernel(u_ref, col_ref, row_ref, o_ref, m_ref, w_ref, ys_ref, z_ref, wr_ref, wi_ref, xr_ref, xi_ref, *,
               n_batch, n_ctx, n_lat):
    d = pl.program_id(1)
    t = S5_T
    col = col_ref[...]
    rowp = row_ref[...]
    lam_rc, lam_ic = col[:, 0:1], col[:, 1:2]
    bbr, bbi = rowp[0:S5_HG], rowp[S5_HG:2 * S5_HG]
    lam_rr, lam_ir = rowp[2 * S5_HG:2 * S5_HG + 1], rowp[2 * S5_HG + 1:2 * S5_HG + 2]
    lane = lax.broadcasted_iota(jnp.int32, (t, t), 1)
    sub = lax.broadcasted_iota(jnp.int32, (t, t), 0)

    pr, pi = _pow_table(lam_rc, lam_ic, jnp.where(d == 0, lane, t - 1 - lane))
    qr, qi = _cmul(pr, pi, lam_rc, lam_ic)
    for h in range(S5_HG):
        cr, ci = col[:, 2 + h:3 + h], col[:, 2 + S5_HG + h:3 + S5_HG + h]
        hs = slice(h * t, (h + 1) * t)
        z_ref[0:S5_PP, hs] = cr * pr - ci * pi
        z_ref[S5_PP:2 * S5_PP, hs] = -(cr * pi + ci * pr)
        ys_ref[0:S5_PP, hs] = (cr * qr - ci * qi).astype(BF16)
        ys_ref[S5_PP:2 * S5_PP, hs] = (-(cr * qi + ci * qr)).astype(BF16)
    b_hi, b_mid, _ = _split3(jnp.concatenate([bbr, bbi], axis=1))
    z_hi, z_mid, _ = _split3(z_ref[...])
    krow = _dot(b_hi, z_hi) + (_dot(b_hi, z_mid) + _dot(b_mid, z_hi))

    def build_m(shift, keep):
        for hp in range(S5_HG):
            for h in range(S5_HG):
                tile = jnp.broadcast_to(krow[hp:hp + 1, h * t:(h + 1) * t], (t, t))
                tile = pltpu.roll(tile, shift, 1, stride=1, stride_axis=0)
                m_ref[hp * t:(hp + 1) * t, h * t:(h + 1) * t] = jnp.where(keep, tile, 0.0).astype(BF16)

    @pl.when(d == 0)
    def _():
        build_m(0, lane >= sub)

    @pl.when(d == 1)
    def _():
        build_m(1, lane <= sub)

    tr, ti = _pow_table(lam_rr, lam_ir, jnp.where(d == 0, t - 1 - sub, sub))
    for hp in range(S5_HG):
        br, bi = bbr[hp:hp + 1], bbi[hp:hp + 1]
        w_ref[hp * t:(hp + 1) * t, 0:S5_PP] = (tr * br - ti * bi).astype(BF16)
        w_ref[hp * t:(hp + 1) * t, S5_PP:2 * S5_PP] = (tr * bi + ti * br).astype(BF16)

    u = jnp.concatenate([u_ref[hp] for hp in range(S5_HG)], axis=1)
    wv = _dot(u, w_ref[...])
    wr_ref[...] = wv[:, 0:S5_PP]
    wi_ref[...] = wv[:, S5_PP:2 * S5_PP]
    lr, li = lam_rr, lam_ir
    for _ in range(t.bit_length() - 1):
        lr, li = _cmul(lr, li, lr, li)

    def make_step(base, stride, count):
        def step(i, carry):
            xr, xi = carry
            j = jnp.where(d == 0, i, count - 1 - i)
            idx = pl.ds(base + j, n_batch, stride=stride)
            xr_ref[idx, :] = xr
            xi_ref[idx, :] = xi
            nr, ni = _cmul(xr, xi, lr, li)
            return nr + wr_ref[idx, :], ni + wi_ref[idx, :]
        return step

    zero = jnp.zeros((n_batch, S5_PP), F32)
    carry = lax.fori_loop(0, n_ctx, make_step(n_batch * n_lat, n_ctx, n_ctx), (zero, zero))
    lax.fori_loop(0, n_lat, make_step(0, n_lat, n_lat), carry)
    x_prev = jnp.concatenate([xr_ref[...], xi_ref[...]], axis=1).astype(BF16)
    y = _dot(u, m_ref[...]) + _dot(x_prev, ys_ref[...])

    @pl.when(d == 0)
    def _():
        for h in range(S5_HG):
            o_ref[h] = y[:, h * t:(h + 1) * t]

    @pl.when(d == 1)
    def _():
        for h in range(S5_HG):
            o_ref[h] += y[:, h * t:(h + 1) * t]


def _s5_params(a_re, a_im, log_dt, b_re, b_im, c_re, c_im):
    a_re, a_im = a_re.astype(F32), a_im.astype(F32)
    dt = jnp.exp(log_dt.astype(F32))[..., None]
    mag = jnp.exp(a_re * dt)
    lam_re, lam_im = mag * jnp.cos(a_im * dt), mag * jnp.sin(a_im * dt)
    den = a_re * a_re + a_im * a_im
    num_re, num_im = lam_re - 1.0, lam_im
    fr = (num_re * a_re + num_im * a_im) / den
    fi = (num_im * a_re - num_re * a_im) / den
    b_re, b_im = b_re.astype(F32), b_im.astype(F32)
    bb_re = fr[..., None] * b_re - fi[..., None] * b_im
    bb_im = fr[..., None] * b_im + fi[..., None] * b_re
    c_re_t = jnp.swapaxes(c_re.astype(F32), -1, -2)
    c_im_t = jnp.swapaxes(c_im.astype(F32), -1, -2)
    col = jnp.concatenate([lam_re[..., None], lam_im[..., None], c_re_t, c_im_t], axis=-1)
    col = jnp.pad(col, ((0, 0),) * 3 + ((0, S5_PP - S5_P), (0, 128 - col.shape[-1])))
    row = jnp.concatenate([jnp.swapaxes(bb_re, -1, -2), jnp.swapaxes(bb_im, -1, -2),
                           lam_re[..., None, :], lam_im[..., None, :]], axis=-2)
    row = jnp.pad(row, ((0, 0),) * 3 + ((0, S5_ROWS - row.shape[-2]), (0, S5_PP - S5_P)))
    return col, row


def _s5_scan(p, col, row, layer, *, n_batch, seq, ctx_len):
    t = S5_T
    n = p.shape[0]
    nc = n // t
    assert ctx_len % t == 0 and seq % t == 0, (ctx_len, seq, t)
    n_ctx, n_lat = ctx_len // t, seq // t
    ut = p[:, COL_SU * BR:(COL_SU + 1) * BR].T.reshape(S5_G, S5_HG, nc, t)
    kern = functools.partial(_s5_kernel, n_batch=n_batch, n_ctx=n_ctx, n_lat=n_lat)
    yt = pl.pallas_call(
        kern,
        out_shape=jax.ShapeDtypeStruct((S5_G, S5_HG, nc, t), F32),
        grid=(S5_G, 2),
        in_specs=[pl.BlockSpec((None, S5_HG, nc, t), lambda g, d: (g, 0, 0, 0)),
                  pl.BlockSpec((None, None, None, S5_PP, 128), lambda g, d: (layer, d, g, 0, 0)),
                  pl.BlockSpec((None, None, None, S5_ROWS, S5_PP), lambda g, d: (layer, d, g, 0, 0))],
        out_specs=pl.BlockSpec((None, S5_HG, nc, t), lambda g, d: (g, 0, 0, 0)),
        scratch_shapes=[pltpu.VMEM((S5_HG * t, S5_HG * t), BF16),
                        pltpu.VMEM((S5_HG * t, 2 * S5_PP), BF16),
                        pltpu.VMEM((2 * S5_PP, S5_HG * t), BF16),
                        pltpu.VMEM((2 * S5_PP, S5_HG * t), F32),
                        pltpu.VMEM((nc, S5_PP), F32), pltpu.VMEM((nc, S5_PP), F32),
                        pltpu.VMEM((nc, S5_PP), F32), pltpu.VMEM((nc, S5_PP), F32)],
        compiler_params=_cparams(("parallel", "arbitrary")),
        name="s5_scan",
    )(ut, col, row)
    return yt.reshape(BR, n).T


def _head_rms(x, g):
    parts = []
    for h in range(NH):
        xh = x[:, h * HD:(h + 1) * HD]
        parts.append(xh * lax.rsqrt(jnp.mean(jnp.square(xh), axis=-1, keepdims=True) + NORM_EPS))
    return jnp.concatenate(parts, axis=-1) * g


def _finish_kernel(raf_ref, rab_ref, rbf_ref, rbb_ref, rc_ref, rdf_ref, rdb_ref, ga_ref, gb_ref, u_ref, gd_ref,
                   mg_ref, x_ref, mod_ref, vec_ref, glu_w_ref, wb_ref, wo_ref, o_ref):
    vec = vec_ref[...]
    oa = _head_rms(raf_ref[...] + rab_ref[...], vec[0:1]) * _silu(ga_ref[...].astype(F32))
    ob = _head_rms(rbf_ref[...] + rbb_ref[...], vec[1:2]) * _silu(gb_ref[...].astype(F32))
    od = _head_rms(rdf_ref[...] + rdb_ref[...], vec[2:3]) * _silu(gd_ref[...].astype(F32))
    yc = rc_ref[...] + vec[3:4] * u_ref[...].astype(F32)
    yc = 0.5 * yc * (1.0 + lax.erf(yc * (2.0 ** -0.5)))
    oc = yc * _sigmoid(_dot(yc.astype(BF16), glu_w_ref[...]) + vec[4:5])
    y = None
    for j, o in enumerate((oa, ob, oc, od)):
        gate = _sigmoid_tanh(mg_ref[:, j * D_MODEL:(j + 1) * D_MODEL].astype(F32))
        term = gate * _dot(o.astype(BF16), wb_ref[j])
        y = term if y is None else y + term
    mix = _dot(y.astype(BF16), wo_ref[...])
    o_ref[...] = x_ref[...] + mod_ref[2:3, :] * mix


def _finish(raw_a, raw_b, raw_c, raw_d, p, x_all, mod, vec, glu_w, w_branch, w_out, layer, *, n_rows, n_batch,
            seq, tm=256):
    mrow = _mod_row_map(n_rows, tm, n_batch, seq)
    one = pl.Buffered(1)

    def pcol(cblock):
        return pl.BlockSpec((tm, BR), lambda i: (i, cblock))

    raw = pl.BlockSpec((tm, BR), lambda i: (i, 0))
    return pl.pallas_call(
        _finish_kernel,
        out_shape=jax.ShapeDtypeStruct((n_rows, D_MODEL), F32),
        grid=(n_rows // tm,),
        in_specs=[raw, raw, raw, raw, raw, raw, raw,
                  pcol(COL_HG), pcol(COL_RG), pcol(COL_SU), pcol(COL_MZ),
                  pl.BlockSpec((tm, N_BRANCH * D_MODEL), lambda i: (i, COL_MERGE)),
                  pl.BlockSpec((tm, D_MODEL), lambda i: (i, 0)),
                  pl.BlockSpec((None, 8, D_MODEL), lambda i: (mrow(i), 0, 0)),
                  pl.BlockSpec((None, 8, BR), lambda i: (layer, 0, 0), pipeline_mode=one),
                  pl.BlockSpec((None, BR, BR), lambda i: (layer, 0, 0), pipeline_mode=one),
                  pl.BlockSpec((None, N_BRANCH, BR, D_MODEL), lambda i: (layer, 0, 0, 0), pipeline_mode=one),
                  pl.BlockSpec((None, D_MODEL, D_MODEL), lambda i: (layer, 0, 0), pipeline_mode=one)],
        out_specs=pl.BlockSpec((tm, D_MODEL), lambda i: (i, 0)),
        compiler_params=_cparams(("parallel",)),
        name="finish",
    )(*raw_a, *raw_b, raw_c, *raw_d, p, p, p, p, p, x_all, mod, vec, glu_w, w_branch, w_out)


def _mlp_kernel(x_ref, mod_ref, g_ref, w1_ref, w2_ref, fin_ref, o_ref, xn_ref, acc_ref, *, final_norm):
    j = pl.program_id(1)

    @pl.when(j == 0)
    def _():
        h = _norm_mod(x_ref[...], g_ref[...], mod_ref[3:4, :], mod_ref[4:5, :])
        xn_ref[...] = h.astype(BF16)
        acc_ref[...] = jnp.zeros_like(acc_ref)

    a = jnp.square(jnp.maximum(_dot(xn_ref[...], w1_ref[...]), 0.0))
    acc_ref[...] += _dot(a.astype(BF16), w2_ref[...])

    @pl.when(j == pl.num_programs(1) - 1)
    def _():
        y = x_ref[...] + mod_ref[5:6, :] * acc_ref[...]
        if final_norm:
            y = (y * lax.rsqrt(jnp.mean(jnp.square(y), axis=-1, keepdims=True) + NORM_EPS)) * fin_ref[...]
        o_ref[...] = y


def _mlp(x_all, mod, norm_w, w1, w2, fin_w, layer, *, n_rows, n_batch, seq, final_norm, tm=512, tf=1024):
    mrow = _mod_row_map(n_rows, tm, n_batch, seq)
    assert D_FF % tf == 0, tf
    kern = functools.partial(_mlp_kernel, final_norm=final_norm)
    return pl.pallas_call(
        kern,
        out_shape=jax.ShapeDtypeStruct((n_rows, D_MODEL), F32),
        grid=(n_rows // tm, D_FF // tf),
        in_specs=[pl.BlockSpec((tm, D_MODEL), lambda i, j: (i, 0)),
                  pl.BlockSpec((None, 8, D_MODEL), lambda i, j: (mrow(i), 0, 0)),
                  pl.BlockSpec((None, 1, D_MODEL), lambda i, j: (layer, 0, 0)),
                  pl.BlockSpec((None, D_MODEL, tf), lambda i, j: (layer, 0, j)),
                  pl.BlockSpec((None, tf, D_MODEL), lambda i, j: (layer, j, 0)),
                  pl.BlockSpec((1, D_MODEL), lambda i, j: (0, 0))],
        out_specs=pl.BlockSpec((tm, D_MODEL), lambda i, j: (i, 0)),
        scratch_shapes=[pltpu.VMEM((tm, D_MODEL), BF16), pltpu.VMEM((tm, D_MODEL), F32)],
        compiler_params=_cparams(("parallel", "arbitrary")),
        name="mlp",
    )(x_all, mod, norm_w, w1, w2, fin_w.reshape(1, D_MODEL))


W_PREP_ROWS = 1024
N_MERGE_BLK = N_BRANCH * D_MODEL // W_PREP_ROWS
MERGE_OFF = GATE_OFF + 4 * NH


def _w_prep_kernel(a_ref, g_ref, main_ref, gate_ref):
    @pl.when(pl.program_id(1) == 0)
    def _():
        gate_ref[...] = jnp.zeros_like(gate_ref)
        gate_ref[0:4 * NH, :] = g_ref[0].astype(BF16)

    main_ref[...] = a_ref[0].astype(BF16)


def _split_w_in(w):
    depth = w.shape[0]
    wt = jnp.swapaxes(w, 1, 2)
    tr = W_PREP_ROWS

    def a_map(l, j):
        row = jnp.where(j < N_MERGE_BLK, MERGE_OFF + j * tr, (j - N_MERGE_BLK) * tr)
        return l, pl.multiple_of(row, 16), 0

    return pl.pallas_call(
        _w_prep_kernel,
        out_shape=(jax.ShapeDtypeStruct((depth, P_WIDTH, D_MODEL), BF16),
                   jax.ShapeDtypeStruct((depth, GATE_PAD, D_MODEL), BF16)),
        grid=(depth, P_WIDTH // tr),
        in_specs=[pl.BlockSpec((pl.Element(1), pl.Element(tr), pl.Element(D_MODEL)), a_map),
                  pl.BlockSpec((pl.Element(1), pl.Element(4 * NH), pl.Element(D_MODEL)),
                               lambda l, j: (l, GATE_OFF, 0))],
        out_specs=(pl.BlockSpec((None, tr, D_MODEL), lambda l, j: (l, j, 0)),
                   pl.BlockSpec((None, GATE_PAD, D_MODEL), lambda l, j: (l, 0, 0))),
        compiler_params=_cparams(("parallel", "arbitrary")),
        name="w_in_prep",
    )(wt, wt)


def kernel(x, c, ctx, c_ctx, w_mod, b_mod, norm_mix, norm_mlp, w_in, hgrn_lb_logits, hgrn_norm, ret_decay, ret_norm, s5_a_re, s5_a_im, s5_log_dt, s5_b_re, s5_b_im, s5_c_re, s5_c_im, s5_d, s5_glu_w, s5_glu_b, mlstm_conv_w, mlstm_conv_b, mlstm_gate_b, mlstm_norm, w_branch, w_out, w_ff1, w_ff2, final_norm):
    n_batch, seq, _ = x.shape
    ctx_len = ctx.shape[1]
    depth = w_in.shape[0]
    nl = n_batch * seq
    dims = dict(n_batch=n_batch, seq=seq)

    p_lb = jax.nn.softmax(hgrn_lb_logits.astype(F32), axis=0)
    lower_bounds = (jnp.cumsum(p_lb, axis=0) - p_lb[0]).reshape(depth, 2, 1, BR)
    lg = jnp.log1p(-jnp.exp(ret_decay.astype(F32)))
    gate_b = jnp.pad(mlstm_gate_b.reshape(depth, 1, 4 * NH).astype(F32), ((0, 0), (0, 0), (0, GATE_PAD - 4 * NH)))
    vec = jnp.stack([hgrn_norm, ret_norm, mlstm_norm, s5_d, s5_glu_b], axis=1).astype(F32)
    vec = jnp.pad(vec, ((0, 0), (0, 8 - vec.shape[1]), (0, 0)))
    w_main, w_gate = _split_w_in(w_in)
    glu_w, wb, wo = s5_glu_w.astype(BF16), w_branch.astype(BF16), w_out.astype(BF16)
    w1, w2 = w_ff1.astype(BF16), w_ff2.astype(BF16)
    norm_mix3 = norm_mix.reshape(depth, 1, D_MODEL)
    norm_mlp3 = norm_mlp.reshape(depth, 1, D_MODEL)
    conv_w = mlstm_conv_w.reshape(depth, 9, 2 * BR)
    conv_b = mlstm_conv_b.reshape(depth, 1, 2 * BR)
    s5_col, s5_row = _s5_params(s5_a_re, s5_a_im, s5_log_dt, s5_b_re, s5_b_im, s5_c_re, s5_c_im)

    c_pad = jnp.zeros((8, D_MODEL), F32).at[:n_batch].set(c).at[n_batch].set(c_ctx)
    x_all = jnp.concatenate([x.reshape(nl, D_MODEL), ctx.reshape(n_batch * ctx_len, D_MODEL)], axis=0)

    for l in range(depth):
        last = l == depth - 1
        mod = _modulation(c_pad, w_mod, b_mod, l)
        mod = jnp.pad(mod[:n_batch + 1].reshape(n_batch + 1, N_MOD, D_MODEL), ((0, 0), (0, 8 - N_MOD), (0, 0)))
        p, gates = _in_proj(x_all, mod, norm_mix3, w_main, w_gate, l, **dims)
        qk = _mlstm_conv(p, conv_w, conv_b, l, ctx_len=ctx_len, **dims)
        raw_a, raw_b, raw_d = _scans(p, gates, qk, lg[l], lower_bounds, gate_b, l, ctx_len=ctx_len, **dims)
        raw_c = _s5_scan(p, s5_col, s5_row, l, ctx_len=ctx_len, **dims)
        n_rows = nl if last else x_all.shape[0]
        x_mid = _finish(raw_a, raw_b, raw_c, raw_d, p, x_all, mod, vec, glu_w, wb, wo, l, n_rows=n_rows, **dims)
        x_all = _mlp(x_mid, mod, norm_mlp3, w1, w2, final_norm, l, n_rows=n_rows, final_norm=last, **dims)
    return x_all.reshape(n_batch, seq, D_MODEL)
```

```python
import functools
import math

import jax
import jax.numpy as jnp
from jax import lax
from jax.experimental import pallas as pl
from jax.experimental.pallas import tpu as pltpu

F32 = jnp.float32
BF16 = jnp.bfloat16

D_MODEL = 2048
N_BRANCH = 4
BR = D_MODEL // N_BRANCH
HD = 128
NH = BR // HD
S5_HG = 16
S5_G = BR // S5_HG
S5_P = 64
D_FF = 4 * D_MODEL
N_MOD = 6
GRID_W = 64
NORM_EPS = 1e-6
NEG_BIG = -1e30
F_TINY = 1e-30
S5_DT_MIN = 1e-3

COL_MERGE = 0
COL_HQ, COL_HF, COL_HV, COL_HG = 16, 17, 19, 20
COL_RQ, COL_RK, COL_RV, COL_RG = 21, 22, 23, 24
COL_SU = 25
COL_MQK, COL_MV, COL_MZ = 26, 28, 29
P_WIDTH = 30 * BR
GATE_OFF = 14 * BR
GATE_PAD = 128

T_HGRN = 128
T_ATTN = 256
S5_T = 128
S5_PP = 128
S5_ROWS = 40
VMEM_LIMIT = 56 * 1024 * 1024

_HI = lax.Precision.HIGHEST


def _cparams(sem):
    return pltpu.CompilerParams(dimension_semantics=sem, vmem_limit_bytes=VMEM_LIMIT)


def _dot(a, b):
    return jnp.dot(a, b, preferred_element_type=F32)


def _dot_nt(a, b):
    return lax.dot_general(a, b, (((1,), (1,)), ((), ())), preferred_element_type=F32)


def _dot_tn(a, b):
    return lax.dot_general(a, b, (((0,), (0,)), ((), ())), preferred_element_type=F32)


def _split3(x):
    hi = x.astype(BF16)
    r1 = x - hi.astype(F32)
    mid = r1.astype(BF16)
    return hi, mid, (r1 - mid.astype(F32)).astype(BF16)


def _split2(x):
    hi = x.astype(BF16)
    return hi, (x - hi.astype(F32)).astype(BF16)


def _cumsum_rows(tri, x):
    hi, mid = _split2(x)
    return _dot(tri, hi) + _dot(tri, mid)


def _sigmoid(x):
    return 1.0 / (1.0 + jnp.exp(-x))


def _silu(x):
    return x * _sigmoid(x)


def _mod_kernel(c_ref, w_ref, b_ref, o_ref):
    @pl.when(pl.program_id(0) == 0)
    def _():
        o_ref[...] = jnp.broadcast_to(b_ref[...], o_ref.shape)

    a = _silu(c_ref[...]).astype(BF16)
    o_ref[...] += _dot(a, w_ref[...].astype(BF16))


def _modulation(c_pad, w_mod, b_mod, layer, tk=256):
    n = w_mod.shape[2]
    return pl.pallas_call(
        _mod_kernel,
        out_shape=jax.ShapeDtypeStruct((8, n), F32),
        grid=(D_MODEL // tk,),
        in_specs=[pl.BlockSpec((8, tk), lambda k: (0, k)),
                  pl.BlockSpec((None, tk, n), lambda k: (layer, k, 0)),
                  pl.BlockSpec((None, 1, n), lambda k: (layer, 0, 0))],
        out_specs=pl.BlockSpec((8, n), lambda k: (0, 0)),
        compiler_params=_cparams(("arbitrary",)),
        name="modulation",
    )(c_pad, w_mod, b_mod.reshape(b_mod.shape[0], 1, n))


def _norm_mod(x, g, shift, scale):
    y = x * lax.rsqrt(jnp.mean(jnp.square(x), axis=-1, keepdims=True) + NORM_EPS)
    return (y * g) * (1.0 + scale) + shift


def _inproj_kernel(x_ref, mod_ref, g_ref, w_ref, wg_ref, p_ref, gate_ref, xn_ref):
    @pl.when(pl.program_id(1) == 0)
    def _():
        h = _norm_mod(x_ref[...], g_ref[...], mod_ref[0:1, :], mod_ref[1:2, :])
        xn_ref[...] = h.astype(BF16)
        gate_ref[...] = _dot_nt(xn_ref[...], wg_ref[...])

    p_ref[...] = _dot_nt(xn_ref[...], w_ref[...]).astype(p_ref.dtype)


def _mod_row_map(n_rows, tm, n_batch, seq):
    assert n_rows % tm == 0 and seq % tm == 0, (n_rows, seq, tm)
    n_lat_tiles, tiles_per_batch = n_batch * seq // tm, seq // tm

    def row(i):
        return jnp.where(i < n_lat_tiles, i // tiles_per_batch, n_batch)
    return row


def _in_proj(x_all, mod, norm_w, w_main, w_gate, layer, *, n_batch, seq, tm=1024, tn=1536):
    n = x_all.shape[0]
    assert P_WIDTH % tn == 0, tn
    mrow = _mod_row_map(n, tm, n_batch, seq)
    return pl.pallas_call(
        _inproj_kernel,
        out_shape=(jax.ShapeDtypeStruct((n, P_WIDTH), BF16),
                   jax.ShapeDtypeStruct((n, GATE_PAD), F32)),
        grid=(n // tm, P_WIDTH // tn),
        in_specs=[pl.BlockSpec((tm, D_MODEL), lambda i, j: (i, 0)),
                  pl.BlockSpec((None, 8, D_MODEL), lambda i, j: (mrow(i), 0, 0)),
                  pl.BlockSpec((None, 1, D_MODEL), lambda i, j: (layer, 0, 0)),
                  pl.BlockSpec((None, tn, D_MODEL), lambda i, j: (layer, j, 0)),
                  pl.BlockSpec((None, GATE_PAD, D_MODEL), lambda i, j: (layer, 0, 0))],
        out_specs=(pl.BlockSpec((tm, tn), lambda i, j: (i, j)),
                   pl.BlockSpec((tm, GATE_PAD), lambda i, j: (i, 0))),
        scratch_shapes=[pltpu.VMEM((tm, D_MODEL), BF16)],
        compiler_params=_cparams(("parallel", "arbitrary")),
        name="in_proj",
    )(x_all, mod, norm_w, w_main, w_gate)


CONV_ROWS = 256


def _conv_kernel(up_ref, x_ref, dn_ref, w_ref, b_ref, o_ref, *, n_lat_blocks, blocks_per_img):
    rb = pl.program_id(0)
    x = x_ref[...].astype(F32)
    w = w_ref[...]
    n = CONV_ROWS

    @pl.when(rb < n_lat_blocks)
    def _():
        rr = rb % blocks_per_img
        up = jnp.where(rr > 0, up_ref[...].astype(F32), 0.0)
        dn = jnp.where(rr < blocks_per_img - 1, dn_ref[...].astype(F32), 0.0)
        ext = jnp.concatenate([up, x, dn], axis=0)
        ne = n + 2 * GRID_W
        col = lax.broadcasted_iota(jnp.int32, (ne, 1), 0) % GRID_W
        xl = jnp.where(col >= 1, pltpu.roll(ext, 1, 0), 0.0)
        xr = jnp.where(col <= GRID_W - 2, pltpu.roll(ext, ne - 1, 0), 0.0)
        acc = None
        for i in range(3):
            y = w[3 * i:3 * i + 1] * xl + w[3 * i + 1:3 * i + 2] * ext + w[3 * i + 2:3 * i + 3] * xr
            part = y[i * GRID_W:i * GRID_W + n]
            acc = part if acc is None else acc + part
        o_ref[...] = _silu(acc + b_ref[...]).astype(o_ref.dtype)

    @pl.when(rb >= n_lat_blocks)
    def _():
        t = lax.broadcasted_iota(jnp.int32, (n, 1), 0)
        xl = jnp.where(t >= 1, pltpu.roll(x, 1, 0), 0.0)
        xr = jnp.where(t <= n - 2, pltpu.roll(x, n - 1, 0), 0.0)
        acc = w[3:4] * xl + w[4:5] * x + w[5:6] * xr
        o_ref[...] = _silu(acc + b_ref[...]).astype(o_ref.dtype)


def _mlstm_conv(p, conv_w, conv_b, layer, *, n_batch, seq, ctx_len, cb=2 * BR):
    assert ctx_len == CONV_ROWS and seq % CONV_ROWS == 0
    n = p.shape[0]
    n_blocks = n // CONV_ROWS
    hb = CONV_ROWS // GRID_W
    n_halo = n // GRID_W
    c0 = COL_MQK * BR // cb
    kern = functools.partial(_conv_kernel, n_lat_blocks=n_batch * seq // CONV_ROWS,
                             blocks_per_img=seq // CONV_ROWS)
    return pl.pallas_call(
        kern,
        out_shape=jax.ShapeDtypeStruct((n, 2 * BR), BF16),
        grid=(n_blocks, 2 * BR // cb),
        in_specs=[pl.BlockSpec((GRID_W, cb), lambda r, c: (jnp.maximum(r * hb - 1, 0), c0 + c)),
                  pl.BlockSpec((CONV_ROWS, cb), lambda r, c: (r, c0 + c)),
                  pl.BlockSpec((GRID_W, cb), lambda r, c: (jnp.minimum((r + 1) * hb, n_halo - 1), c0 + c)),
                  pl.BlockSpec((None, 9, cb), lambda r, c: (layer, 0, c)),
                  pl.BlockSpec((None, 1, cb), lambda r, c: (layer, 0, c))],
        out_specs=pl.BlockSpec((CONV_ROWS, cb), lambda r, c: (r, c)),
        compiler_params=_cparams(("parallel", "parallel")),
        name="mlstm_conv",
    )(p, p, p, conv_w, conv_b)


def _chunk_block(b, d, i, *, n_ctx, n_lat, n_batch):
    ctx_j = i if d == 0 else n_ctx - 1 - i
    lat_j = i - n_ctx if d == 0 else n_ctx + n_lat - 1 - i
    return jnp.where(i < n_ctx, n_batch * n_lat + b * n_ctx + ctx_j, b * n_lat + lat_j)


def _order_mask(d, t):
    r = lax.broadcasted_iota(jnp.int32, (t, t), 0)
    c = lax.broadcasted_iota(jnp.int32, (t, t), 1)
    return (c <= r) if d == 0 else (c >= r)


def _init_at_first_step(*refs):
    @pl.when(pl.program_id(1) == 0)
    def _():
        for ref in refs:
            ref[...] = jnp.zeros_like(ref)


def _ret_direction(d, lg_ref, q_ref, k_ref, v_ref, o_ref, s_ref):
    t = T_ATTN
    r = lax.broadcasted_iota(jnp.int32, (t, t), 0)
    c = lax.broadcasted_iota(jnp.int32, (t, t), 1)
    rel = (r - c) if d == 0 else (c - r)
    relf = jnp.maximum(rel, 0).astype(F32)
    tt = lax.broadcasted_iota(jnp.int32, (t, 1), 0)
    pos = (tt if d == 0 else t - 1 - tt).astype(F32)
    scale = HD ** -0.5
    for h in range(NH):
        sl = slice(h * HD, (h + 1) * HD)
        lg = lg_ref[d, h]
        q = q_ref[:, sl]
        k = k_ref[:, sl]
        v = v_ref[:, sl]
        intra = jnp.where(rel >= 0, jnp.exp(lg * relf), 0.0)
        scores = (_dot_nt(q, k) * scale) * intra
        s_old = s_ref[d, h]
        out = _dot(scores.astype(BF16), v) + jnp.exp(lg * (pos + 1.0)) * _dot(q, s_old.astype(BF16))
        kd = (k.astype(F32) * (scale * jnp.exp(lg * (t - 1.0 - pos)))).astype(BF16)
        s_ref[d, h] = jnp.exp(lg * t) * s_old + _dot_tn(kd, v)
        o_ref[:, sl] = out


def _ret_kernel(lg_ref, qf_ref, kf_ref, vf_ref, qb_ref, kb_ref, vb_ref, of_ref, ob_ref, s_ref):
    _init_at_first_step(s_ref)
    _ret_direction(0, lg_ref, qf_ref, kf_ref, vf_ref, of_ref, s_ref)
    _ret_direction(1, lg_ref, qb_ref, kb_ref, vb_ref, ob_ref, s_ref)


def _hgrn_direction(d, lb_ref, q_ref, z_ref, v_ref, o_ref, st_ref):
    t = T_HGRN
    fwd = d == 0
    lb = lb_ref[...]
    z = z_ref[...].astype(F32)
    sig = _sigmoid(z)
    f = lb + (1.0 - lb) * sig
    logf = jnp.log(jnp.maximum(f, F_TINY))
    kk = (1.0 - lb) * (1.0 - sig)
    qq = _silu(q_ref[...].astype(F32))
    r2 = lax.broadcasted_iota(jnp.int32, (t, t), 0)
    c2 = lax.broadcasted_iota(jnp.int32, (t, t), 1)
    tri = jnp.where((c2 <= r2) if fwd else (c2 >= r2), 1.0, 0.0).astype(BF16)
    cum = _cumsum_rows(tri, logf)
    xdiff = jnp.where((c2 < r2) if fwd else (c2 > r2), r2 ^ c2, 0)
    row = lax.broadcasted_iota(jnp.int32, (t, 1), 0)
    if fwd:
        e_ref = jnp.where(row == 0, 0.0, pltpu.roll(cum, 1, 0))
    else:
        e_ref = jnp.where(row == t - 1, 0.0, pltpu.roll(cum, t - 1, 0))
    f_ref = cum
    levels = []
    w = 1
    while w < t:
        if w == 1:
            q_l, k_l = (qq * jnp.maximum(f, F_TINY)).astype(BF16), kk.astype(BF16)
        else:
            q_l = (qq * jnp.exp(cum - e_ref)).astype(BF16)
            k_l = (kk * jnp.exp(f_ref - cum)).astype(BF16)
        levels.append((q_l, k_l, (xdiff >> (w.bit_length() - 1)) == 1))
        upper = (row & w) != 0
        if fwd:
            e_ref = jnp.where(upper, pltpu.roll(e_ref, w, 0), e_ref)
            f_ref = jnp.where(upper, f_ref, pltpu.roll(f_ref, t - w, 0))
        else:
            e_ref = jnp.where(upper, e_ref, pltpu.roll(e_ref, t - w, 0))
            f_ref = jnp.where(upper, pltpu.roll(f_ref, w, 0), f_ref)
        w *= 2
    q_b = qq.astype(BF16)
    k_b = kk.astype(BF16)
    tail = cum[t - 1:t] if fwd else cum[0:1]
    q_state = (qq * jnp.exp(cum)).astype(BF16)
    k_state = (kk * jnp.exp(tail - cum)).astype(BF16)
    decay = jnp.exp(tail)
    for h in range(NH):
        sl = slice(h * HD, (h + 1) * HD)
        v_h = v_ref[:, sl]
        scores = jnp.where(r2 == c2, _dot_nt(q_b[:, sl], k_b[:, sl]), 0.0)
        for q_l, k_l, m in levels:
            scores = jnp.where(m, _dot_nt(q_l[:, sl], k_l[:, sl]), scores)
        st = st_ref[d, h]
        o_ref[:, sl] = _dot(scores.astype(BF16), v_h) + _dot_nt(q_state[:, sl], st.astype(BF16))
        st_ref[d, h] = st * decay[:, sl] + _dot_tn(v_h, k_state[:, sl])


def _hgrn_kernel(lbf_ref, qf_ref, zf_ref, vf_ref, lbb_ref, qb_ref, zb_ref, vb_ref, of_ref, ob_ref, st_ref):
    _init_at_first_step(st_ref)
    _hgrn_direction(0, lbf_ref, qf_ref, zf_ref, vf_ref, of_ref, st_ref)
    _hgrn_direction(1, lbb_ref, qb_ref, zb_ref, vb_ref, ob_ref, st_ref)


def _log_sigmoid(x):
    return jnp.minimum(x, 0.0) - jnp.log(1.0 + jnp.exp(-jnp.abs(x)))


def _rep_lane(parts, lane, width):
    sel = jnp.where(lax.broadcasted_iota(jnp.int32, (GATE_PAD, width), 0) == lane, 1.0, 0.0).astype(BF16)
    hi, mid = parts
    return _dot(hi, sel) + _dot(mid, sel)


def _mlstm_direction(d, qk_ref, v_ref, g_ref, gb_ref, o_ref, c_ref, m_ref):
    t = T_ATTN
    assert t == 2 * HD
    fwd = d == 0
    g = g_ref[...] + gb_ref[...]
    r2 = lax.broadcasted_iota(jnp.int32, (t, t), 0)
    c2 = lax.broadcasted_iota(jnp.int32, (t, t), 1)
    mask = (c2 <= r2) if fwd else (c2 >= r2)
    tri = jnp.where(mask, 1.0, 0.0).astype(BF16)
    eye = jnp.where(r2 == c2, 1.0, 0.0).astype(BF16)
    lf = _log_sigmoid(g)
    cum_c = _cumsum_rows(tri, lf)
    a_col = g - pltpu.roll(cum_c, GATE_PAD - NH, 1)
    row = lax.broadcasted_iota(jnp.int32, (t, 1), 0)
    pm = a_col
    k = 1
    while k < t:
        if fwd:
            pm = jnp.maximum(pm, jnp.where(row >= k, pltpu.roll(pm, k, 0), NEG_BIG))
        else:
            pm = jnp.maximum(pm, jnp.where(row < t - k, pltpu.roll(pm, t - k, 0), NEG_BIG))
        k *= 2
    a_parts, pm_parts, cum_parts = _split2(a_col), _split2(pm), _split2(cum_c)
    a_rows = _dot_tn(a_parts[0], eye) + _dot_tn(a_parts[1], eye)
    scale = HD ** -0.5
    last = t - 1 if fwd else 0
    ones = jnp.ones((t, HD), BF16)
    for h in range(NH):
        sl = slice(h * HD, (h + 1) * HD)
        li, lfw = 2 * NH * d + h, 2 * NH * d + NH + h
        m_old = m_ref[d, h]
        mx2 = jnp.maximum(_rep_lane(pm_parts, li, 2 * HD), jnp.concatenate([m_old, m_old], axis=1))
        mx = mx2[:, 0:HD]
        cc = _rep_lane(cum_parts, lfw, HD)
        a_rep = _rep_lane(a_parts, li, HD)
        a_row = a_rows[li:li + 1]
        q = qk_ref[:, sl]
        k_h = qk_ref[:, BR + h * HD:BR + (h + 1) * HD]
        v_aug = jnp.concatenate([v_ref[:, sl], ones], axis=1)
        w = jnp.where(mask, jnp.exp(a_row - mx2), 0.0)
        scores = (_dot_nt(q, k_h) * scale) * w
        w_state = jnp.exp(m_old - mx)
        c_old = c_ref[d, h]
        s_hi = scores.astype(BF16)
        s_mid = (scores - s_hi.astype(F32)).astype(BF16)
        intra = _dot(s_hi, v_aug)
        inter = _dot(q, c_old.astype(BF16))
        num = intra[:, 0:HD] + w_state * inter[:, 0:HD]
        den = (intra[:, HD:2 * HD] + _dot(s_mid, ones)) + w_state * inter[:, HD:2 * HD]
        o_ref[:, sl] = num / jnp.maximum(jnp.abs(den), jnp.exp(-(cc + mx)))
        total = cc[last:last + 1]
        m_new_rel = mx[last:last + 1]
        keep = jnp.exp(m_old - m_new_rel)
        w_end = jnp.exp(a_rep - m_new_rel)
        kw = (k_h.astype(F32) * (scale * w_end)).astype(BF16)
        c_ref[d, h] = jnp.concatenate([keep, keep], axis=1) * c_old + _dot_tn(kw, v_aug)
        m_ref[d, h] = total + m_new_rel


def _mlstm_kernel(qkf_ref, vf_ref, gf_ref, qkb_ref, vb_ref, gb_ref, bias_ref, of_ref, ob_ref, c_ref, m_ref):
    _init_at_first_step(c_ref, m_ref)
    _mlstm_direction(0, qkf_ref, vf_ref, gf_ref, bias_ref, of_ref, c_ref, m_ref)
    _mlstm_direction(1, qkb_ref, vb_ref, gb_ref, bias_ref, ob_ref, c_ref, m_ref)


def _scans(p, gates, qk, lg, lb, gate_b, layer, *, n_batch, seq, ctx_len):
    n = p.shape[0]
    state = pltpu.VMEM((2, NH, HD, HD), F32)
    vec = pltpu.VMEM((2, NH, 1, HD), F32)

    def call(kernel, t, inputs, make_specs, scratch, name):
        assert ctx_len % t == 0 and seq % t == 0, (ctx_len, seq, t)
        n_ctx, n_lat = ctx_len // t, seq // t

        def rows(d, width, cblock=0):
            return pl.BlockSpec((t, width), lambda b, i: (_chunk_block(b, d, i, n_ctx=n_ctx, n_lat=n_lat,
                                                                        n_batch=n_batch), cblock))

        out = jax.ShapeDtypeStruct((n, BR), F32)
        return pl.pallas_call(
            kernel,
            out_shape=(out, out),
            grid=(n_batch, n_ctx + n_lat),
            in_specs=make_specs(rows),
            out_specs=(rows(0, BR), rows(1, BR)),
            scratch_shapes=scratch,
            compiler_params=_cparams(("parallel", "arbitrary")),
            name=name,
        )(*inputs)

    raw_b = call(_ret_kernel, T_ATTN, (lg, p, p, p, p, p, p),
                 lambda rows: [pl.BlockSpec(memory_space=pltpu.SMEM)]
                 + [rows(d, BR, c) for d in (0, 1) for c in (COL_RQ, COL_RK, COL_RV)],
                 [state], "retention_scan")
    lb_spec = lambda d: pl.BlockSpec((None, None, 1, BR), lambda b, i: (layer, d, 0, 0))
    raw_a = call(_hgrn_kernel, T_HGRN, (lb, p, p, p, lb, p, p, p),
                 lambda rows: [spec for d in (0, 1)
                               for spec in (lb_spec(d), rows(d, BR, COL_HQ), rows(d, BR, COL_HF + d),
                                            rows(d, BR, COL_HV))],
                 [state], "hgrn2_scan")
    raw_d = call(_mlstm_kernel, T_ATTN, (qk, p, gates, qk, p, gates, gate_b),
                 lambda rows: [spec for d in (0, 1)
                               for spec in (rows(d, 2 * BR), rows(d, BR, COL_MV), rows(d, GATE_PAD))]
                 + [pl.BlockSpec((None, 1, GATE_PAD), lambda b, i: (layer, 0, 0))],
                 [pltpu.VMEM((2, NH, HD, 2 * HD), F32), vec], "mlstm_scan")
    return raw_a, raw_b, raw_d


def _cmul(ar, ai, br, bi):
    return ar * br - ai * bi, ar * bi + ai * br


def _pow_table(lam_r, lam_i, expo):
    pr = jnp.ones(expo.shape, F32)
    pi = jnp.zeros(expo.shape, F32)
    ar = jnp.broadcast_to(lam_r, expo.shape)
    ai = jnp.broadcast_to(lam_i, expo.shape)
    for k in range(S5_T.bit_length() - 1):
        nr, ni = _cmul(pr, pi, ar, ai)
        bit = (expo & (1 << k)) != 0
        pr, pi = jnp.where(bit, nr, pr), jnp.where(bit, ni, pi)
        ar, ai = _cmul(ar, ai, ar, ai)
    return pr, pi


def _s5_kernel(u_ref, col_ref, row_ref, o_ref, m_ref, w_ref, ys_ref, z_ref, wr_ref, wi_ref, xr_ref, xi_ref, *,
               n_batch, n_ctx, n_lat):
    d = pl.program_id(1)
    t = S5_T
    col = col_ref[...]
    rowp = row_ref[...]
    lam_rc, lam_ic = col[:, 0:1], col[:, 1:2]
    bbr, bbi = rowp[0:S5_HG], rowp[S5_HG:2 * S5_HG]
    lam_rr, lam_ir = rowp[2 * S5_HG:2 * S5_HG + 1], rowp[2 * S5_HG + 1:2 * S5_HG + 2]
    lane = lax.broadcasted_iota(jnp.int32, (t, t), 1)
    sub = lax.broadcasted_iota(jnp.int32, (t, t), 0)

    pr, pi = _pow_table(lam_rc, lam_ic, jnp.where(d == 0, lane, t - 1 - lane))
    qr, qi = _cmul(pr, pi, lam_rc, lam_ic)
    for h in range(S5_HG):
        cr, ci = col[:, 2 + h:3 + h], col[:, 2 + S5_HG + h:3 + S5_HG + h]
        hs = slice(h * t, (h + 1) * t)
        z_ref[0:S5_PP, hs] = cr * pr - ci * pi
        z_ref[S5_PP:2 * S5_PP, hs] = -(cr * pi + ci * pr)
        ys_ref[0:S5_PP, hs] = (cr * qr - ci * qi).astype(BF16)
        ys_ref[S5_PP:2 * S5_PP, hs] = (-(cr * qi + ci * qr)).astype(BF16)
    b_hi, b_mid, _ = _split3(jnp.concatenate([bbr, bbi], axis=1))
    z_hi, z_mid, _ = _split3(z_ref[...])
    krow = _dot(b_hi, z_hi) + (_dot(b_hi, z_mid) + _dot(b_mid, z_hi))

    def build_m(shift, keep):
        for hp in range(S5_HG):
            for h in range(S5_HG):
                tile = jnp.broadcast_to(krow[hp:hp + 1, h * t:(h + 1) * t], (t, t))
                tile = pltpu.roll(tile, shift, 1, stride=1, stride_axis=0)
                m_ref[hp * t:(hp + 1) * t, h * t:(h + 1) * t] = jnp.where(keep, tile, 0.0).astype(BF16)

    @pl.when(d == 0)
    def _():
        build_m(0, lane >= sub)

    @pl.when(d == 1)
    def _():
        build_m(1, lane <= sub)

    tr, ti = _pow_table(lam_rr, lam_ir, jnp.where(d == 0, t - 1 - sub, sub))
    for hp in range(S5_HG):
        br, bi = bbr[hp:hp + 1], bbi[hp:hp + 1]
        w_ref[hp * t:(hp + 1) * t, 0:S5_PP] = (tr * br - ti * bi).astype(BF16)
        w_ref[hp * t:(hp + 1) * t, S5_PP:2 * S5_PP] = (tr * bi + ti * br).astype(BF16)

    u = jnp.concatenate([u_ref[hp] for hp in range(S5_HG)], axis=1)
    wv = _dot(u, w_ref[...])
    wr_ref[...] = wv[:, 0:S5_PP]
    wi_ref[...] = wv[:, S5_PP:2 * S5_PP]
    lr, li = lam_rr, lam_ir
    for _ in range(t.bit_length() - 1):
        lr, li = _cmul(lr, li, lr, li)

    def make_step(base, stride, count):
        def step(i, carry):
            xr, xi = carry
            j = jnp.where(d == 0, i, count - 1 - i)
            idx = pl.ds(base + j, n_batch, stride=stride)
            xr_ref[idx, :] = xr
            xi_ref[idx, :] = xi
            nr, ni = _cmul(xr, xi, lr, li)
            return nr + wr_ref[idx, :], ni + wi_ref[idx, :]
        return step

    zero = jnp.zeros((n_batch, S5_PP), F32)
    carry = lax.fori_loop(0, n_ctx, make_step(n_batch * n_lat, n_ctx, n_ctx), (zero, zero))
    lax.fori_loop(0, n_lat, make_step(0, n_lat, n_lat), carry)
    x_prev = jnp.concatenate([xr_ref[...], xi_ref[...]], axis=1).astype(BF16)
    y = _dot(u, m_ref[...]) + _dot(x_prev, ys_ref[...])

    @pl.when(d == 0)
    def _():
        for h in range(S5_HG):
            o_ref[h] = y[:, h * t:(h + 1) * t]

    @pl.when(d == 1)
    def _():
        for h in range(S5_HG):
            o_ref[h] += y[:, h * t:(h + 1) * t]


def _s5_params(a_re, a_im, log_dt, b_re, b_im, c_re, c_im):
    a_re, a_im = a_re.astype(F32), a_im.astype(F32)
    dt = jnp.exp(log_dt.astype(F32))[..., None]
    mag = jnp.exp(a_re * dt)
    lam_re, lam_im = mag * jnp.cos(a_im * dt), mag * jnp.sin(a_im * dt)
    den = a_re * a_re + a_im * a_im
    num_re, num_im = lam_re - 1.0, lam_im
    fr = (num_re * a_re + num_im * a_im) / den
    fi = (num_im * a_re - num_re * a_im) / den
    b_re, b_im = b_re.astype(F32), b_im.astype(F32)
    bb_re = fr[..., None] * b_re - fi[..., None] * b_im
    bb_im = fr[..., None] * b_im + fi[..., None] * b_re
    c_re_t = jnp.swapaxes(c_re.astype(F32), -1, -2)
    c_im_t = jnp.swapaxes(c_im.astype(F32), -1, -2)
    col = jnp.concatenate([lam_re[..., None], lam_im[..., None], c_re_t, c_im_t], axis=-1)
    col = jnp.pad(col, ((0, 0),) * 3 + ((0, S5_PP - S5_P), (0, 128 - col.shape[-1])))
    row = jnp.concatenate([jnp.swapaxes(bb_re, -1, -2), jnp.swapaxes(bb_im, -1, -2),
                           lam_re[..., None, :], lam_im[..., None, :]], axis=-2)
    row = jnp.pad(row, ((0, 0),) * 3 + ((0, S5_ROWS - row.shape[-2]), (0, S5_PP - S5_P)))
    return col, row


def _s5_scan(p, col, row, layer, *, n_batch, seq, ctx_len):
    t = S5_T
    n = p.shape[0]
    nc = n // t
    assert ctx_len % t == 0 and seq % t == 0, (ctx_len, seq, t)
    n_ctx, n_lat = ctx_len // t, seq // t
    ut = p[:, COL_SU * BR:(COL_SU + 1) * BR].T.reshape(S5_G, S5_HG, nc, t)
    kern = functools.partial(_s5_kernel, n_batch=n_batch, n_ctx=n_ctx, n_lat=n_lat)
    yt = pl.pallas_call(
        kern,
        out_shape=jax.ShapeDtypeStruct((S5_G, S5_HG, nc, t), F32),
        grid=(S5_G, 2),
        in_specs=[pl.BlockSpec((None, S5_HG, nc, t), lambda g, d: (g, 0, 0, 0)),
                  pl.BlockSpec((None, None, None, S5_PP, 128), lambda g, d: (layer, d, g, 0, 0)),
                  pl.BlockSpec((None, None, None, S5_ROWS, S5_PP), lambda g, d: (layer, d, g, 0, 0))],
        out_specs=pl.BlockSpec((None, S5_HG, nc, t), lambda g, d: (g, 0, 0, 0)),
        scratch_shapes=[pltpu.VMEM((S5_HG * t, S5_HG * t), BF16),
                        pltpu.VMEM((S5_HG * t, 2 * S5_PP), BF16),
                        pltpu.VMEM((2 * S5_PP, S5_HG * t), BF16),
                        pltpu.VMEM((2 * S5_PP, S5_HG * t), F32),
                        pltpu.VMEM((nc, S5_PP), F32), pltpu.VMEM((nc, S5_PP), F32),
                        pltpu.VMEM((nc, S5_PP), F32), pltpu.VMEM((nc, S5_PP), F32)],
        compiler_params=_cparams(("parallel", "arbitrary")),
        name="s5_scan",
    )(ut, col, row)
    return yt.reshape(BR, n).T


def _head_rms(x, g):
    parts = []
    for h in range(NH):
        xh = x[:, h * HD:(h + 1) * HD]
        parts.append(xh * lax.rsqrt(jnp.mean(jnp.square(xh), axis=-1, keepdims=True) + NORM_EPS))
    return jnp.concatenate(parts, axis=-1) * g


def _finish_kernel(raf_ref, rab_ref, rbf_ref, rbb_ref, rc_ref, rdf_ref, rdb_ref, ga_ref, gb_ref, u_ref, gd_ref,
                   mg_ref, x_ref, mod_ref, vec_ref, glu_w_ref, wb_ref, wo_ref, o_ref):
    vec = vec_ref[...]
    oa = _head_rms(raf_ref[...] + rab_ref[...], vec[0:1]) * _silu(ga_ref[...].astype(F32))
    ob = _head_rms(rbf_ref[...] + rbb_ref[...], vec[1:2]) * _silu(gb_ref[...].astype(F32))
    od = _head_rms(rdf_ref[...] + rdb_ref[...], vec[2:3]) * _silu(gd_ref[...].astype(F32))
    yc = rc_ref[...] + vec[3:4] * u_ref[...].astype(F32)
    yc = 0.5 * yc * (1.0 + lax.erf(yc * (2.0 ** -0.5)))
    oc = yc * _sigmoid(_dot(yc.astype(BF16), glu_w_ref[...]) + vec[4:5])
    y2 = None
    for j, o in enumerate((oa, ob, oc, od)):
        gate2 = jnp.tanh(0.5 * mg_ref[:, j * D_MODEL:(j + 1) * D_MODEL].astype(F32)) + 1.0
        term = gate2 * _dot(o.astype(BF16), wb_ref[j])
        y2 = term if y2 is None else y2 + term
    mix2 = _dot(y2.astype(BF16), wo_ref[...])
    o_ref[...] = x_ref[...] + (0.5 * mod_ref[2:3, :]) * mix2


def _finish(raw_a, raw_b, raw_c, raw_d, p, x_all, mod, vec, glu_w, w_branch, w_out, layer, *, n_rows, n_batch,
            seq, tm=256):
    mrow = _mod_row_map(n_rows, tm, n_batch, seq)
    one = pl.Buffered(1)

    def pcol(cblock):
        return pl.BlockSpec((tm, BR), lambda i: (i, cblock))

    raw = pl.BlockSpec((tm, BR), lambda i: (i, 0))
    return pl.pallas_call(
        _finish_kernel,
        out_shape=jax.ShapeDtypeStruct((n_rows, D_MODEL), F32),
        grid=(n_rows // tm,),
        in_specs=[raw, raw, raw, raw, raw, raw, raw,
                  pcol(COL_HG), pcol(COL_RG), pcol(COL_SU), pcol(COL_MZ),
                  pl.BlockSpec((tm, N_BRANCH * D_MODEL), lambda i: (i, COL_MERGE)),
                  pl.BlockSpec((tm, D_MODEL), lambda i: (i, 0)),
                  pl.BlockSpec((None, 8, D_MODEL), lambda i: (mrow(i), 0, 0)),
                  pl.BlockSpec((None, 8, BR), lambda i: (layer, 0, 0), pipeline_mode=one),
                  pl.BlockSpec((None, BR, BR), lambda i: (layer, 0, 0), pipeline_mode=one),
                  pl.BlockSpec((None, N_BRANCH, BR, D_MODEL), lambda i: (layer, 0, 0, 0), pipeline_mode=one),
                  pl.BlockSpec((None, D_MODEL, D_MODEL), lambda i: (layer, 0, 0), pipeline_mode=one)],
        out_specs=pl.BlockSpec((tm, D_MODEL), lambda i: (i, 0)),
        compiler_params=_cparams(("parallel",)),
        name="finish",
    )(*raw_a, *raw_b, raw_c, *raw_d, p, p, p, p, p, x_all, mod, vec, glu_w, w_branch, w_out)


def _mlp_kernel(x_ref, mod_ref, g_ref, w1_ref, w2_ref, fin_ref, o_ref, xn_ref, acc_ref, *, final_norm):
    j = pl.program_id(1)

    @pl.when(j == 0)
    def _():
        h = _norm_mod(x_ref[...], g_ref[...], mod_ref[3:4, :], mod_ref[4:5, :])
        xn_ref[...] = h.astype(BF16)
        acc_ref[...] = jnp.zeros_like(acc_ref)

    a = jnp.square(jnp.maximum(_dot(xn_ref[...], w1_ref[...]), 0.0))
    acc_ref[...] += _dot(a.astype(BF16), w2_ref[...])

    @pl.when(j == pl.num_programs(1) - 1)
    def _():
        y = x_ref[...] + mod_ref[5:6, :] * acc_ref[...]
        if final_norm:
            y = (y * lax.rsqrt(jnp.mean(jnp.square(y), axis=-1, keepdims=True) + NORM_EPS)) * fin_ref[...]
        o_ref[...] = y


def _mlp(x_all, mod, norm_w, w1, w2, fin_w, layer, *, n_rows, n_batch, seq, final_norm, tm=512, tf=1024):
    mrow = _mod_row_map(n_rows, tm, n_batch, seq)
    assert D_FF % tf == 0, tf
    kern = functools.partial(_mlp_kernel, final_norm=final_norm)
    return pl.pallas_call(
        kern,
        out_shape=jax.ShapeDtypeStruct((n_rows, D_MODEL), F32),
        grid=(n_rows // tm, D_FF // tf),
        in_specs=[pl.BlockSpec((tm, D_MODEL), lambda i, j: (i, 0)),
                  pl.BlockSpec((None, 8, D_MODEL), lambda i, j: (mrow(i), 0, 0)),
                  pl.BlockSpec((None, 1, D_MODEL), lambda i, j: (layer, 0, 0)),
                  pl.BlockSpec((None, D_MODEL, tf), lambda i, j: (layer, 0, j)),
                  pl.BlockSpec((None, tf, D_MODEL), lambda i, j: (layer, j, 0)),
                  pl.BlockSpec((1, D_MODEL), lambda i, j: (0, 0))],
        out_specs=pl.BlockSpec((tm, D_MODEL), lambda i, j: (i, 0)),
        scratch_shapes=[pltpu.VMEM((tm, D_MODEL), BF16), pltpu.VMEM((tm, D_MODEL), F32)],
        compiler_params=_cparams(("parallel", "arbitrary")),
        name="mlp",
    )(x_all, mod, norm_w, w1, w2, fin_w.reshape(1, D_MODEL))


W_PREP_ROWS = 1024
N_MERGE_BLK = N_BRANCH * D_MODEL // W_PREP_ROWS
MERGE_OFF = GATE_OFF + 4 * NH


def _w_prep_kernel(a_ref, g_ref, main_ref, gate_ref):
    @pl.when(pl.program_id(1) == 0)
    def _():
        gate_ref[...] = jnp.zeros_like(gate_ref)
        gate_ref[0:4 * NH, :] = g_ref[0].astype(BF16)

    main_ref[...] = a_ref[0].astype(BF16)


def _split_w_in(w):
    depth = w.shape[0]
    wt = jnp.swapaxes(w, 1, 2)
    tr = W_PREP_ROWS

    def a_map(l, j):
        row = jnp.where(j < N_MERGE_BLK, MERGE_OFF + j * tr, (j - N_MERGE_BLK) * tr)
        return l, pl.multiple_of(row, 16), 0

    return pl.pallas_call(
        _w_prep_kernel,
        out_shape=(jax.ShapeDtypeStruct((depth, P_WIDTH, D_MODEL), BF16),
                   jax.ShapeDtypeStruct((depth, GATE_PAD, D_MODEL), BF16)),
        grid=(depth, P_WIDTH // tr),
        in_specs=[pl.BlockSpec((pl.Element(1), pl.Element(tr), pl.Element(D_MODEL)), a_map),
                  pl.BlockSpec((pl.Element(1), pl.Element(4 * NH), pl.Element(D_MODEL)),
                               lambda l, j: (l, GATE_OFF, 0))],
        out_specs=(pl.BlockSpec((None, tr, D_MODEL), lambda l, j: (l, j, 0)),
                   pl.BlockSpec((None, GATE_PAD, D_MODEL), lambda l, j: (l, 0, 0))),
        compiler_params=_cparams(("parallel", "arbitrary")),
        name="w_in_prep",
    )(wt, wt)


def kernel(x, c, ctx, c_ctx, w_mod, b_mod, norm_mix, norm_mlp, w_in, hgrn_lb_logits, hgrn_norm, ret_decay, ret_norm, s5_a_re, s5_a_im, s5_log_dt, s5_b_re, s5_b_im, s5_c_re, s5_c_im, s5_d, s5_glu_w, s5_glu_b, mlstm_conv_w, mlstm_conv_b, mlstm_gate_b, mlstm_norm, w_branch, w_out, w_ff1, w_ff2, final_norm):
    n_batch, seq, _ = x.shape
    ctx_len = ctx.shape[1]
    depth = w_in.shape[0]
    nl = n_batch * seq
    dims = dict(n_batch=n_batch, seq=seq)

    p_lb = jax.nn.softmax(hgrn_lb_logits.astype(F32), axis=0)
    lower_bounds = (jnp.cumsum(p_lb, axis=0) - p_lb[0]).reshape(depth, 2, 1, BR)
    lg = jnp.log1p(-jnp.exp(ret_decay.astype(F32)))
    gate_b = jnp.pad(mlstm_gate_b.reshape(depth, 1, 4 * NH).astype(F32), ((0, 0), (0, 0), (0, GATE_PAD - 4 * NH)))
    vec = jnp.stack([hgrn_norm, ret_norm, mlstm_norm, s5_d, s5_glu_b], axis=1).astype(F32)
    vec = jnp.pad(vec, ((0, 0), (0, 8 - vec.shape[1]), (0, 0)))
    w_main, w_gate = _split_w_in(w_in)
    glu_w, wb, wo = s5_glu_w.astype(BF16), w_branch.astype(BF16), w_out.astype(BF16)
    w1, w2 = w_ff1.astype(BF16), w_ff2.astype(BF16)
    norm_mix3 = norm_mix.reshape(depth, 1, D_MODEL)
    norm_mlp3 = norm_mlp.reshape(depth, 1, D_MODEL)
    conv_w = mlstm_conv_w.reshape(depth, 9, 2 * BR)
    conv_b = mlstm_conv_b.reshape(depth, 1, 2 * BR)
    s5_col, s5_row = _s5_params(s5_a_re, s5_a_im, s5_log_dt, s5_b_re, s5_b_im, s5_c_re, s5_c_im)

    c_pad = jnp.zeros((8, D_MODEL), F32).at[:n_batch].set(c).at[n_batch].set(c_ctx)
    x_all = jnp.concatenate([x.reshape(nl, D_MODEL), ctx.reshape(n_batch * ctx_len, D_MODEL)], axis=0)

    for l in range(depth):
        last = l == depth - 1
        mod = _modulation(c_pad, w_mod, b_mod, l)
        mod = jnp.pad(mod[:n_batch + 1].reshape(n_batch + 1, N_MOD, D_MODEL), ((0, 0), (0, 8 - N_MOD), (0, 0)))
        p, gates = _in_proj(x_all, mod, norm_mix3, w_main, w_gate, l, **dims)
        qk = _mlstm_conv(p, conv_w, conv_b, l, ctx_len=ctx_len, **dims)
        raw_a, raw_b, raw_d = _scans(p, gates, qk, lg[l], lower_bounds, gate_b, l, ctx_len=ctx_len, **dims)
        raw_c = _s5_scan(p, s5_col, s5_row, l, ctx_len=ctx_len, **dims)
        n_rows = nl if last else x_all.shape[0]
        x_mid = _finish(raw_a, raw_b, raw_c, raw_d, p, x_all, mod, vec, glu_w, wb, wo, l, n_rows=n_rows, **dims)
        x_all = _mlp(x_mid, mod, norm_mlp3, w1, w2, final_norm, l, n_rows=n_rows, final_norm=last, **dims)
    return x_all.reshape(n_batch, seq, D_MODEL)
```

```python
import functools
import math

import jax
import jax.numpy as jnp
from jax import lax
from jax.experimental import pallas as pl
from jax.experimental.pallas import tpu as pltpu

F32 = jnp.float32
BF16 = jnp.bfloat16

D_MODEL = 2048
N_BRANCH = 4
BR = D_MODEL // N_BRANCH
HD = 128
NH = BR // HD
S5_HG = 16
S5_G = BR // S5_HG
S5_P = 64
D_FF = 4 * D_MODEL
N_MOD = 6
GRID_W = 64
NORM_EPS = 1e-6
NEG_BIG = -1e30
F_TINY = 1e-30
S5_DT_MIN = 1e-3

COL_MERGE = 0
COL_HQ, COL_HF, COL_HV, COL_HG = 16, 17, 19, 20
COL_RQ, COL_RK, COL_RV, COL_RG = 21, 22, 23, 24
COL_SU = 25
COL_MQK, COL_MV, COL_MZ = 26, 28, 29
P_WIDTH = 30 * BR
GATE_OFF = 14 * BR
GATE_PAD = 128

T_HGRN = 128
T_ATTN = 256
S5_T = 128
S5_PP = 128
S5_ROWS = 40
VMEM_LIMIT = 56 * 1024 * 1024

_HI = lax.Precision.HIGHEST


def _cparams(sem):
    return pltpu.CompilerParams(dimension_semantics=sem, vmem_limit_bytes=VMEM_LIMIT)


def _dot(a, b):
    return jnp.dot(a, b, preferred_element_type=F32)


def _dot_nt(a, b):
    return lax.dot_general(a, b, (((1,), (1,)), ((), ())), preferred_element_type=F32)


def _dot_tn(a, b):
    return lax.dot_general(a, b, (((0,), (0,)), ((), ())), preferred_element_type=F32)


def _split3(x):
    hi = x.astype(BF16)
    r1 = x - hi.astype(F32)
    mid = r1.astype(BF16)
    return hi, mid, (r1 - mid.astype(F32)).astype(BF16)


def _split2(x):
    hi = x.astype(BF16)
    return hi, (x - hi.astype(F32)).astype(BF16)


def _cumsum_rows(tri, x):
    hi, mid = _split2(x)
    return _dot(tri, hi) + _dot(tri, mid)


def _sigmoid(x):
    return 1.0 / (1.0 + jnp.exp(-x))


def _silu(x):
    return x * _sigmoid(x)


def _mod_kernel(c_ref, w_ref, b_ref, o_ref):
    @pl.when(pl.program_id(0) == 0)
    def _():
        o_ref[...] = jnp.broadcast_to(b_ref[...], o_ref.shape)

    a = _silu(c_ref[...]).astype(BF16)
    o_ref[...] += _dot(a, w_ref[...].astype(BF16))


def _modulation(c_pad, w_mod, b_mod, layer, tk=256):
    n = w_mod.shape[2]
    return pl.pallas_call(
        _mod_kernel,
        out_shape=jax.ShapeDtypeStruct((8, n), F32),
        grid=(D_MODEL // tk,),
        in_specs=[pl.BlockSpec((8, tk), lambda k: (0, k)),
                  pl.BlockSpec((None, tk, n), lambda k: (layer, k, 0)),
                  pl.BlockSpec((None, 1, n), lambda k: (layer, 0, 0))],
        out_specs=pl.BlockSpec((8, n), lambda k: (0, 0)),
        compiler_params=_cparams(("arbitrary",)),
        name="modulation",
    )(c_pad, w_mod, b_mod.reshape(b_mod.shape[0], 1, n))


def _norm_mod(x, g, shift, scale):
    y = x * lax.rsqrt(jnp.mean(jnp.square(x), axis=-1, keepdims=True) + NORM_EPS)
    return (y * g) * (1.0 + scale) + shift


def _inproj_kernel(x_ref, mod_ref, g_ref, w_ref, wg_ref, p_ref, gate_ref, xn_ref):
    @pl.when(pl.program_id(1) == 0)
    def _():
        h = _norm_mod(x_ref[...], g_ref[...], mod_ref[0:1, :], mod_ref[1:2, :])
        xn_ref[...] = h.astype(BF16)
        gate_ref[...] = _dot_nt(xn_ref[...], wg_ref[...])

    p_ref[...] = _dot_nt(xn_ref[...], w_ref[...]).astype(p_ref.dtype)


def _mod_row_map(n_rows, tm, n_batch, seq):
    assert n_rows % tm == 0 and seq % tm == 0, (n_rows, seq, tm)
    n_lat_tiles, tiles_per_batch = n_batch * seq // tm, seq // tm

    def row(i):
        return jnp.where(i < n_lat_tiles, i // tiles_per_batch, n_batch)
    return row


def _in_proj(x_all, mod, norm_w, w_main, w_gate, layer, *, n_batch, seq, tm=1024, tn=1536):
    n = x_all.shape[0]
    assert P_WIDTH % tn == 0, tn
    mrow = _mod_row_map(n, tm, n_batch, seq)
    return pl.pallas_call(
        _inproj_kernel,
        out_shape=(jax.ShapeDtypeStruct((n, P_WIDTH), BF16),
                   jax.ShapeDtypeStruct((n, GATE_PAD), F32)),
        grid=(n // tm, P_WIDTH // tn),
        in_specs=[pl.BlockSpec((tm, D_MODEL), lambda i, j: (i, 0)),
                  pl.BlockSpec((None, 8, D_MODEL), lambda i, j: (mrow(i), 0, 0)),
                  pl.BlockSpec((None, 1, D_MODEL), lambda i, j: (layer, 0, 0)),
                  pl.BlockSpec((None, tn, D_MODEL), lambda i, j: (layer, j, 0)),
                  pl.BlockSpec((None, GATE_PAD, D_MODEL), lambda i, j: (layer, 0, 0))],
        out_specs=(pl.BlockSpec((tm, tn), lambda i, j: (i, j)),
                   pl.BlockSpec((tm, GATE_PAD), lambda i, j: (i, 0))),
        scratch_shapes=[pltpu.VMEM((tm, D_MODEL), BF16)],
        compiler_params=_cparams(("parallel", "arbitrary")),
        name="in_proj",
    )(x_all, mod, norm_w, w_main, w_gate)


CONV_ROWS = 256


def _conv_kernel(up_ref, x_ref, dn_ref, w_ref, b_ref, o_ref, *, n_lat_blocks, blocks_per_img):
    rb = pl.program_id(0)
    x = x_ref[...].astype(F32)
    w = w_ref[...]
    n = CONV_ROWS

    @pl.when(rb < n_lat_blocks)
    def _():
        rr = rb % blocks_per_img
        up = jnp.where(rr > 0, up_ref[...].astype(F32), 0.0)
        dn = jnp.where(rr < blocks_per_img - 1, dn_ref[...].astype(F32), 0.0)
        ext = jnp.concatenate([up, x, dn], axis=0)
        ne = n + 2 * GRID_W
        col = lax.broadcasted_iota(jnp.int32, (ne, 1), 0) % GRID_W
        xl = jnp.where(col >= 1, pltpu.roll(ext, 1, 0), 0.0)
        xr = jnp.where(col <= GRID_W - 2, pltpu.roll(ext, ne - 1, 0), 0.0)
        acc = None
        for i in range(3):
            y = w[3 * i:3 * i + 1] * xl + w[3 * i + 1:3 * i + 2] * ext + w[3 * i + 2:3 * i + 3] * xr
            part = y[i * GRID_W:i * GRID_W + n]
            acc = part if acc is None else acc + part
        o_ref[...] = _silu(acc + b_ref[...]).astype(o_ref.dtype)

    @pl.when(rb >= n_lat_blocks)
    def _():
        t = lax.broadcasted_iota(jnp.int32, (n, 1), 0)
        xl = jnp.where(t >= 1, pltpu.roll(x, 1, 0), 0.0)
        xr = jnp.where(t <= n - 2, pltpu.roll(x, n - 1, 0), 0.0)
        acc = w[3:4] * xl + w[4:5] * x + w[5:6] * xr
        o_ref[...] = _silu(acc + b_ref[...]).astype(o_ref.dtype)


def _mlstm_conv(p, conv_w, conv_b, layer, *, n_batch, seq, ctx_len, cb=2 * BR):
    assert ctx_len == CONV_ROWS and seq % CONV_ROWS == 0
    n = p.shape[0]
    n_blocks = n // CONV_ROWS
    hb = CONV_ROWS // GRID_W
    n_halo = n // GRID_W
    c0 = COL_MQK * BR // cb
    kern = functools.partial(_conv_kernel, n_lat_blocks=n_batch * seq // CONV_ROWS,
                             blocks_per_img=seq // CONV_ROWS)
    return pl.pallas_call(
        kern,
        out_shape=jax.ShapeDtypeStruct((n, 2 * BR), BF16),
        grid=(n_blocks, 2 * BR // cb),
        in_specs=[pl.BlockSpec((GRID_W, cb), lambda r, c: (jnp.maximum(r * hb - 1, 0), c0 + c)),
                  pl.BlockSpec((CONV_ROWS, cb), lambda r, c: (r, c0 + c)),
                  pl.BlockSpec((GRID_W, cb), lambda r, c: (jnp.minimum((r + 1) * hb, n_halo - 1), c0 + c)),
                  pl.BlockSpec((None, 9, cb), lambda r, c: (layer, 0, c)),
                  pl.BlockSpec((None, 1, cb), lambda r, c: (layer, 0, c))],
        out_specs=pl.BlockSpec((CONV_ROWS, cb), lambda r, c: (r, c)),
        compiler_params=_cparams(("parallel", "parallel")),
        name="mlstm_conv",
    )(p, p, p, conv_w, conv_b)


def _chunk_block(b, d, i, *, n_ctx, n_lat, n_batch):
    ctx_j = i if d == 0 else n_ctx - 1 - i
    lat_j = i - n_ctx if d == 0 else n_ctx + n_lat - 1 - i
    return jnp.where(i < n_ctx, n_batch * n_lat + b * n_ctx + ctx_j, b * n_lat + lat_j)


def _order_mask(d, t):
    r = lax.broadcasted_iota(jnp.int32, (t, t), 0)
    c = lax.broadcasted_iota(jnp.int32, (t, t), 1)
    return (c <= r) if d == 0 else (c >= r)


def _init_at_first_step(*refs):
    @pl.when(pl.program_id(1) == 0)
    def _():
        for ref in refs:
            ref[...] = jnp.zeros_like(ref)


def _ret_direction(d, lg_ref, q_ref, k_ref, v_ref, o_ref, s_ref):
    t = T_ATTN
    r = lax.broadcasted_iota(jnp.int32, (t, t), 0)
    c = lax.broadcasted_iota(jnp.int32, (t, t), 1)
    rel = (r - c) if d == 0 else (c - r)
    relf = jnp.maximum(rel, 0).astype(F32)
    tt = lax.broadcasted_iota(jnp.int32, (t, 1), 0)
    pos = (tt if d == 0 else t - 1 - tt).astype(F32)
    scale = HD ** -0.5
    for h in range(NH):
        sl = slice(h * HD, (h + 1) * HD)
        lg = lg_ref[d, h]
        q = q_ref[:, sl]
        k = k_ref[:, sl]
        v = v_ref[:, sl]
        intra = jnp.where(rel >= 0, jnp.exp(lg * relf), 0.0)
        scores = (_dot_nt(q, k) * scale) * intra
        s_old = s_ref[d, h]
        out = _dot(scores.astype(BF16), v) + jnp.exp(lg * (pos + 1.0)) * _dot(q, s_old.astype(BF16))
        kd = (k.astype(F32) * (scale * jnp.exp(lg * (t - 1.0 - pos)))).astype(BF16)
        s_ref[d, h] = jnp.exp(lg * t) * s_old + _dot_tn(kd, v)
        o_ref[:, sl] = out


def _hgrn_direction(d, lb_ref, q_ref, z_ref, v_ref, o_ref, st_ref):
    t = T_HGRN
    fwd = d == 0
    lb = lb_ref[...]
    z = z_ref[...].astype(F32)
    sig = _sigmoid(z)
    f = lb + (1.0 - lb) * sig
    logf = jnp.log(jnp.maximum(f, F_TINY))
    kk = (1.0 - lb) * (1.0 - sig)
    qq = _silu(q_ref[...].astype(F32))
    r2 = lax.broadcasted_iota(jnp.int32, (t, t), 0)
    c2 = lax.broadcasted_iota(jnp.int32, (t, t), 1)
    tri = jnp.where((c2 <= r2) if fwd else (c2 >= r2), 1.0, 0.0).astype(BF16)
    cum = _cumsum_rows(tri, logf)
    xdiff = jnp.where((c2 < r2) if fwd else (c2 > r2), r2 ^ c2, 0)
    row = lax.broadcasted_iota(jnp.int32, (t, 1), 0)
    if fwd:
        e_ref = jnp.where(row == 0, 0.0, pltpu.roll(cum, 1, 0))
    else:
        e_ref = jnp.where(row == t - 1, 0.0, pltpu.roll(cum, t - 1, 0))
    f_ref = cum
    levels = []
    w = 1
    while w < t:
        if w == 1:
            q_l, k_l = (qq * jnp.maximum(f, F_TINY)).astype(BF16), kk.astype(BF16)
        else:
            q_l = (qq * jnp.exp(cum - e_ref)).astype(BF16)
            k_l = (kk * jnp.exp(f_ref - cum)).astype(BF16)
        levels.append((q_l, k_l, (xdiff >> (w.bit_length() - 1)) == 1))
        upper = (row & w) != 0
        if fwd:
            e_ref = jnp.where(upper, pltpu.roll(e_ref, w, 0), e_ref)
            f_ref = jnp.where(upper, f_ref, pltpu.roll(f_ref, t - w, 0))
        else:
            e_ref = jnp.where(upper, e_ref, pltpu.roll(e_ref, t - w, 0))
            f_ref = jnp.where(upper, pltpu.roll(f_ref, w, 0), f_ref)
        w *= 2
    q_b = qq.astype(BF16)
    k_b = kk.astype(BF16)
    tail = cum[t - 1:t] if fwd else cum[0:1]
    q_state = (qq * jnp.exp(cum)).astype(BF16)
    k_state = (kk * jnp.exp(tail - cum)).astype(BF16)
    decay = jnp.exp(tail)
    for h in range(NH):
        sl = slice(h * HD, (h + 1) * HD)
        v_h = v_ref[:, sl]
        scores = jnp.where(r2 == c2, _dot_nt(q_b[:, sl], k_b[:, sl]), 0.0)
        for q_l, k_l, m in levels:
            scores = jnp.where(m, _dot_nt(q_l[:, sl], k_l[:, sl]), scores)
        st = st_ref[d, h]
        o_ref[:, sl] = _dot(scores.astype(BF16), v_h) + _dot_nt(q_state[:, sl], st.astype(BF16))
        st_ref[d, h] = st * decay[:, sl] + _dot_tn(v_h, k_state[:, sl])


def _hgrn_kernel(lbf_ref, qf_ref, zf_ref, vf_ref, lbb_ref, qb_ref, zb_ref, vb_ref, of_ref, ob_ref, st_ref):
    _init_at_first_step(st_ref)
    _hgrn_direction(0, lbf_ref, qf_ref, zf_ref, vf_ref, of_ref, st_ref)
    _hgrn_direction(1, lbb_ref, qb_ref, zb_ref, vb_ref, ob_ref, st_ref)


def _log_sigmoid(x):
    return jnp.minimum(x, 0.0) - jnp.log(1.0 + jnp.exp(-jnp.abs(x)))


def _rep_lane(parts, lane, width):
    sel = jnp.where(lax.broadcasted_iota(jnp.int32, (GATE_PAD, width), 0) == lane, 1.0, 0.0).astype(BF16)
    hi, mid = parts
    return _dot(hi, sel) + _dot(mid, sel)


def _mlstm_direction(d, qk_ref, v_ref, g_ref, gb_ref, o_ref, c_ref, m_ref):
    t = T_ATTN
    assert t == 2 * HD
    fwd = d == 0
    g = g_ref[...] + gb_ref[...]
    r2 = lax.broadcasted_iota(jnp.int32, (t, t), 0)
    c2 = lax.broadcasted_iota(jnp.int32, (t, t), 1)
    mask = (c2 <= r2) if fwd else (c2 >= r2)
    tri = jnp.where(mask, 1.0, 0.0).astype(BF16)
    eye = jnp.where(r2 == c2, 1.0, 0.0).astype(BF16)
    lf = _log_sigmoid(g)
    cum_c = _cumsum_rows(tri, lf)
    a_col = g - pltpu.roll(cum_c, GATE_PAD - NH, 1)
    row = lax.broadcasted_iota(jnp.int32, (t, 1), 0)
    pm = a_col
    k = 1
    while k < t:
        if fwd:
            pm = jnp.maximum(pm, jnp.where(row >= k, pltpu.roll(pm, k, 0), NEG_BIG))
        else:
            pm = jnp.maximum(pm, jnp.where(row < t - k, pltpu.roll(pm, t - k, 0), NEG_BIG))
        k *= 2
    a_parts, pm_parts, cum_parts = _split2(a_col), _split2(pm), _split2(cum_c)
    a_rows = _dot_tn(a_parts[0], eye) + _dot_tn(a_parts[1], eye)
    scale = HD ** -0.5
    last = t - 1 if fwd else 0
    ones = jnp.ones((t, HD), BF16)
    for h in range(NH):
        sl = slice(h * HD, (h + 1) * HD)
        li, lfw = 2 * NH * d + h, 2 * NH * d + NH + h
        m_old = m_ref[d, h]
        mx2 = jnp.maximum(_rep_lane(pm_parts, li, 2 * HD), jnp.concatenate([m_old, m_old], axis=1))
        mx = mx2[:, 0:HD]
        cc = _rep_lane(cum_parts, lfw, HD)
        a_rep = _rep_lane(a_parts, li, HD)
        a_row = a_rows[li:li + 1]
        q = qk_ref[:, sl]
        k_h = qk_ref[:, BR + h * HD:BR + (h + 1) * HD]
        v_aug = jnp.concatenate([v_ref[:, sl], ones], axis=1)
        w = jnp.where(mask, jnp.exp(a_row - mx2), 0.0)
        scores = (_dot_nt(q, k_h) * scale) * w
        w_state = jnp.exp(m_old - mx)
        c_old = c_ref[d, h]
        s_hi = scores.astype(BF16)
        s_mid = (scores - s_hi.astype(F32)).astype(BF16)
        intra = _dot(s_hi, v_aug)
        inter = _dot(q, c_old.astype(BF16))
        num = intra[:, 0:HD] + w_state * inter[:, 0:HD]
        den = (intra[:, HD:2 * HD] + _dot(s_mid, ones)) + w_state * inter[:, HD:2 * HD]
        o_ref[:, sl] = num / jnp.maximum(jnp.abs(den), jnp.exp(-(cc + mx)))
        total = cc[last:last + 1]
        m_new_rel = mx[last:last + 1]
        keep = jnp.exp(m_old - m_new_rel)
        w_end = jnp.exp(a_rep - m_new_rel)
        kw = (k_h.astype(F32) * (scale * w_end)).astype(BF16)
        c_ref[d, h] = jnp.concatenate([keep, keep], axis=1) * c_old + _dot_tn(kw, v_aug)
        m_ref[d, h] = total + m_new_rel


def _ret_mlstm_kernel(lg_ref, rqf_ref, rkf_ref, rvf_ref, rqb_ref, rkb_ref, rvb_ref,
                      qkf_ref, vf_ref, gf_ref, qkb_ref, vb_ref, gb_ref, bias_ref,
                      rof_ref, rob_ref, mof_ref, mob_ref, s_ref, c_ref, m_ref):
    _init_at_first_step(s_ref, c_ref, m_ref)
    _ret_direction(0, lg_ref, rqf_ref, rkf_ref, rvf_ref, rof_ref, s_ref)
    _mlstm_direction(0, qkf_ref, vf_ref, gf_ref, bias_ref, mof_ref, c_ref, m_ref)
    _ret_direction(1, lg_ref, rqb_ref, rkb_ref, rvb_ref, rob_ref, s_ref)
    _mlstm_direction(1, qkb_ref, vb_ref, gb_ref, bias_ref, mob_ref, c_ref, m_ref)


def _scans(p, gates, qk, lg, lb, gate_b, layer, *, n_batch, seq, ctx_len):
    n = p.shape[0]
    state = pltpu.VMEM((2, NH, HD, HD), F32)
    vec = pltpu.VMEM((2, NH, 1, HD), F32)

    def call(kernel, t, inputs, make_specs, scratch, name, n_mixers=1):
        assert ctx_len % t == 0 and seq % t == 0, (ctx_len, seq, t)
        n_ctx, n_lat = ctx_len // t, seq // t

        def rows(d, width, cblock=0):
            return pl.BlockSpec((t, width), lambda b, i: (_chunk_block(b, d, i, n_ctx=n_ctx, n_lat=n_lat,
                                                                        n_batch=n_batch), cblock))

        out = jax.ShapeDtypeStruct((n, BR), F32)
        return pl.pallas_call(
            kernel,
            out_shape=(out, out) * n_mixers,
            grid=(n_batch, n_ctx + n_lat),
            in_specs=make_specs(rows),
            out_specs=(rows(0, BR), rows(1, BR)) * n_mixers,
            scratch_shapes=scratch,
            compiler_params=_cparams(("parallel", "arbitrary")),
            name=name,
        )(*inputs)

    lb_spec = lambda d: pl.BlockSpec((None, None, 1, BR), lambda b, i: (layer, d, 0, 0))
    raw_a = call(_hgrn_kernel, T_HGRN, (lb, p, p, p, lb, p, p, p),
                 lambda rows: [spec for d in (0, 1)
                               for spec in (lb_spec(d), rows(d, BR, COL_HQ), rows(d, BR, COL_HF + d),
                                            rows(d, BR, COL_HV))],
                 [state], "hgrn2_scan")
    raw_bd = call(_ret_mlstm_kernel, T_ATTN, (lg, p, p, p, p, p, p, qk, p, gates, qk, p, gates, gate_b),
                  lambda rows: [pl.BlockSpec(memory_space=pltpu.SMEM)]
                  + [rows(d, BR, c) for d in (0, 1) for c in (COL_RQ, COL_RK, COL_RV)]
                  + [spec for d in (0, 1)
                     for spec in (rows(d, 2 * BR), rows(d, BR, COL_MV), rows(d, GATE_PAD))]
                  + [pl.BlockSpec((None, 1, GATE_PAD), lambda b, i: (layer, 0, 0))],
                  [state, pltpu.VMEM((2, NH, HD, 2 * HD), F32), vec], "retention_mlstm_scan", n_mixers=2)
    return raw_a, raw_bd[0:2], raw_bd[2:4]


def _cmul(ar, ai, br, bi):
    return ar * br - ai * bi, ar * bi + ai * br


def _pow_table(lam_r, lam_i, expo):
    pr = jnp.ones(expo.shape, F32)
    pi = jnp.zeros(expo.shape, F32)
    ar = jnp.broadcast_to(lam_r, expo.shape)
    ai = jnp.broadcast_to(lam_i, expo.shape)
    for k in range(S5_T.bit_length() - 1):
        nr, ni = _cmul(pr, pi, ar, ai)
        bit = (expo & (1 << k)) != 0
        pr, pi = jnp.where(bit, nr, pr), jnp.where(bit, ni, pi)
        ar, ai = _cmul(ar, ai, ar, ai)
    return pr, pi


def _s5_kernel(u_ref, col_ref, row_ref, o_ref, m_ref, w_ref, ys_ref, z_ref, wr_ref, wi_ref, xr_ref, xi_ref, *,
               n_batch, n_ctx, n_lat):
    d = pl.program_id(1)
    t = S5_T
    col = col_ref[...]
    rowp = row_ref[...]
    lam_rc, lam_ic = col[:, 0:1], col[:, 1:2]
    bbr, bbi = rowp[0:S5_HG], rowp[S5_HG:2 * S5_HG]
    lam_rr, lam_ir = rowp[2 * S5_HG:2 * S5_HG + 1], rowp[2 * S5_HG + 1:2 * S5_HG + 2]
    lane = lax.broadcasted_iota(jnp.int32, (t, t), 1)
    sub = lax.broadcasted_iota(jnp.int32, (t, t), 0)

    pr, pi = _pow_table(lam_rc, lam_ic, jnp.where(d == 0, lane, t - 1 - lane))
    qr, qi = _cmul(pr, pi, lam_rc, lam_ic)
    for h in range(S5_HG):
        cr, ci = col[:, 2 + h:3 + h], col[:, 2 + S5_HG + h:3 + S5_HG + h]
        hs = slice(h * t, (h + 1) * t)
        z_ref[0:S5_PP, hs] = cr * pr - ci * pi
        z_ref[S5_PP:2 * S5_PP, hs] = -(cr * pi + ci * pr)
        ys_ref[0:S5_PP, hs] = (cr * qr - ci * qi).astype(BF16)
        ys_ref[S5_PP:2 * S5_PP, hs] = (-(cr * qi + ci * qr)).astype(BF16)
    b_hi, b_mid, _ = _split3(jnp.concatenate([bbr, bbi], axis=1))
    z_hi, z_mid, _ = _split3(z_ref[...])
    krow = _dot(b_hi, z_hi) + (_dot(b_hi, z_mid) + _dot(b_mid, z_hi))

    def build_m(shift, keep):
        for hp in range(S5_HG):
            for h in range(S5_HG):
                tile = jnp.broadcast_to(krow[hp:hp + 1, h * t:(h + 1) * t], (t, t))
                tile = pltpu.roll(tile, shift, 1, stride=1, stride_axis=0)
                m_ref[hp * t:(hp + 1) * t, h * t:(h + 1) * t] = jnp.where(keep, tile, 0.0).astype(BF16)

    @pl.when(d == 0)
    def _():
        build_m(0, lane >= sub)

    @pl.when(d == 1)
    def _():
        build_m(1, lane <= sub)

    tr, ti = _pow_table(lam_rr, lam_ir, jnp.where(d == 0, t - 1 - sub, sub))
    for hp in range(S5_HG):
        br, bi = bbr[hp:hp + 1], bbi[hp:hp + 1]
        w_ref[hp * t:(hp + 1) * t, 0:S5_PP] = (tr * br - ti * bi).astype(BF16)
        w_ref[hp * t:(hp + 1) * t, S5_PP:2 * S5_PP] = (tr * bi + ti * br).astype(BF16)

    u = jnp.concatenate([u_ref[hp] for hp in range(S5_HG)], axis=1)
    wv = _dot(u, w_ref[...])
    wr_ref[...] = wv[:, 0:S5_PP]
    wi_ref[...] = wv[:, S5_PP:2 * S5_PP]
    lr, li = lam_rr, lam_ir
    for _ in range(t.bit_length() - 1):
        lr, li = _cmul(lr, li, lr, li)

    def make_step(base, stride, count):
        def step(i, carry):
            xr, xi = carry
            j = jnp.where(d == 0, i, count - 1 - i)
            idx = pl.ds(base + j, n_batch, stride=stride)
            xr_ref[idx, :] = xr
            xi_ref[idx, :] = xi
            nr, ni = _cmul(xr, xi, lr, li)
            return nr + wr_ref[idx, :], ni + wi_ref[idx, :]
        return step

    zero = jnp.zeros((n_batch, S5_PP), F32)
    carry = lax.fori_loop(0, n_ctx, make_step(n_batch * n_lat, n_ctx, n_ctx), (zero, zero))
    lax.fori_loop(0, n_lat, make_step(0, n_lat, n_lat), carry)
    x_prev = jnp.concatenate([xr_ref[...], xi_ref[...]], axis=1).astype(BF16)
    y = _dot(u, m_ref[...]) + _dot(x_prev, ys_ref[...])

    @pl.when(d == 0)
    def _():
        for h in range(S5_HG):
            o_ref[h] = y[:, h * t:(h + 1) * t]

    @pl.when(d == 1)
    def _():
        for h in range(S5_HG):
            o_ref[h] += y[:, h * t:(h + 1) * t]


def _s5_params(a_re, a_im, log_dt, b_re, b_im, c_re, c_im):
    a_re, a_im = a_re.astype(F32), a_im.astype(F32)
    dt = jnp.exp(log_dt.astype(F32))[..., None]
    mag = jnp.exp(a_re * dt)
    lam_re, lam_im = mag * jnp.cos(a_im * dt), mag * jnp.sin(a_im * dt)
    den = a_re * a_re + a_im * a_im
    num_re, num_im = lam_re - 1.0, lam_im
    fr = (num_re * a_re + num_im * a_im) / den
    fi = (num_im * a_re - num_re * a_im) / den
    b_re, b_im = b_re.astype(F32), b_im.astype(F32)
    bb_re = fr[..., None] * b_re - fi[..., None] * b_im
    bb_im = fr[..., None] * b_im + fi[..., None] * b_re
    c_re_t = jnp.swapaxes(c_re.astype(F32), -1, -2)
    c_im_t = jnp.swapaxes(c_im.astype(F32), -1, -2)
    col = jnp.concatenate([lam_re[..., None], lam_im[..., None], c_re_t, c_im_t], axis=-1)
    col = jnp.pad(col, ((0, 0),) * 3 + ((0, S5_PP - S5_P), (0, 128 - col.shape[-1])))
    row = jnp.concatenate([jnp.swapaxes(bb_re, -1, -2), jnp.swapaxes(bb_im, -1, -2),
                           lam_re[..., None, :], lam_im[..., None, :]], axis=-2)
    row = jnp.pad(row, ((0, 0),) * 3 + ((0, S5_ROWS - row.shape[-2]), (0, S5_PP - S5_P)))
    return col, row


def _s5_scan(p, col, row, layer, *, n_batch, seq, ctx_len):
    t = S5_T
    n = p.shape[0]
    nc = n // t
    assert ctx_len % t == 0 and seq % t == 0, (ctx_len, seq, t)
    n_ctx, n_lat = ctx_len // t, seq // t
    ut = p[:, COL_SU * BR:(COL_SU + 1) * BR].T.reshape(S5_G, S5_HG, nc, t)
    kern = functools.partial(_s5_kernel, n_batch=n_batch, n_ctx=n_ctx, n_lat=n_lat)
    yt = pl.pallas_call(
        kern,
        out_shape=jax.ShapeDtypeStruct((S5_G, S5_HG, nc, t), F32),
        grid=(S5_G, 2),
        in_specs=[pl.BlockSpec((None, S5_HG, nc, t), lambda g, d: (g, 0, 0, 0)),
                  pl.BlockSpec((None, None, None, S5_PP, 128), lambda g, d: (layer, d, g, 0, 0)),
                  pl.BlockSpec((None, None, None, S5_ROWS, S5_PP), lambda g, d: (layer, d, g, 0, 0))],
        out_specs=pl.BlockSpec((None, S5_HG, nc, t), lambda g, d: (g, 0, 0, 0)),
        scratch_shapes=[pltpu.VMEM((S5_HG * t, S5_HG * t), BF16),
                        pltpu.VMEM((S5_HG * t, 2 * S5_PP), BF16),
                        pltpu.VMEM((2 * S5_PP, S5_HG * t), BF16),
                        pltpu.VMEM((2 * S5_PP, S5_HG * t), F32),
                        pltpu.VMEM((nc, S5_PP), F32), pltpu.VMEM((nc, S5_PP), F32),
                        pltpu.VMEM((nc, S5_PP), F32), pltpu.VMEM((nc, S5_PP), F32)],
        compiler_params=_cparams(("parallel", "arbitrary")),
        name="s5_scan",
    )(ut, col, row)
    return yt.reshape(BR, n).T


def _head_rms(x, g):
    parts = []
    for h in range(NH):
        xh = x[:, h * HD:(h + 1) * HD]
        parts.append(xh * lax.rsqrt(jnp.mean(jnp.square(xh), axis=-1, keepdims=True) + NORM_EPS))
    return jnp.concatenate(parts, axis=-1) * g


def _finish_kernel(raf_ref, rab_ref, rbf_ref, rbb_ref, rc_ref, rdf_ref, rdb_ref, ga_ref, gb_ref, u_ref, gd_ref,
                   mg_ref, x_ref, mod_ref, vec_ref, glu_w_ref, wb_ref, wo_ref, o_ref):
    vec = vec_ref[...]
    oa = _head_rms(raf_ref[...] + rab_ref[...], vec[0:1]) * _silu(ga_ref[...].astype(F32))
    ob = _head_rms(rbf_ref[...] + rbb_ref[...], vec[1:2]) * _silu(gb_ref[...].astype(F32))
    od = _head_rms(rdf_ref[...] + rdb_ref[...], vec[2:3]) * _silu(gd_ref[...].astype(F32))
    yc = rc_ref[...] + vec[3:4] * u_ref[...].astype(F32)
    yc = 0.5 * yc * (1.0 + lax.erf(yc * (2.0 ** -0.5)))
    oc = yc * _sigmoid(_dot(yc.astype(BF16), glu_w_ref[...]) + vec[4:5])
    y2 = None
    for j, o in enumerate((oa, ob, oc, od)):
        gate2 = jnp.tanh(0.5 * mg_ref[:, j * D_MODEL:(j + 1) * D_MODEL].astype(F32)) + 1.0
        term = gate2 * _dot(o.astype(BF16), wb_ref[j])
        y2 = term if y2 is None else y2 + term
    mix2 = _dot(y2.astype(BF16), wo_ref[...])
    o_ref[...] = x_ref[...] + (0.5 * mod_ref[2:3, :]) * mix2


def _finish(raw_a, raw_b, raw_c, raw_d, p, x_all, mod, vec, glu_w, w_branch, w_out, layer, *, n_rows, n_batch,
            seq, tm=256):
    mrow = _mod_row_map(n_rows, tm, n_batch, seq)
    one = pl.Buffered(1)

    def pcol(cblock):
        return pl.BlockSpec((tm, BR), lambda i: (i, cblock))

    raw = pl.BlockSpec((tm, BR), lambda i: (i, 0))
    return pl.pallas_call(
        _finish_kernel,
        out_shape=jax.ShapeDtypeStruct((n_rows, D_MODEL), F32),
        grid=(n_rows // tm,),
        in_specs=[raw, raw, raw, raw, raw, raw, raw,
                  pcol(COL_HG), pcol(COL_RG), pcol(COL_SU), pcol(COL_MZ),
                  pl.BlockSpec((tm, N_BRANCH * D_MODEL), lambda i: (i, COL_MERGE)),
                  pl.BlockSpec((tm, D_MODEL), lambda i: (i, 0)),
                  pl.BlockSpec((None, 8, D_MODEL), lambda i: (mrow(i), 0, 0)),
                  pl.BlockSpec((None, 8, BR), lambda i: (layer, 0, 0), pipeline_mode=one),
                  pl.BlockSpec((None, BR, BR), lambda i: (layer, 0, 0), pipeline_mode=one),
                  pl.BlockSpec((None, N_BRANCH, BR, D_MODEL), lambda i: (layer, 0, 0, 0), pipeline_mode=one),
                  pl.BlockSpec((None, D_MODEL, D_MODEL), lambda i: (layer, 0, 0), pipeline_mode=one)],
        out_specs=pl.BlockSpec((tm, D_MODEL), lambda i: (i, 0)),
        compiler_params=_cparams(("parallel",)),
        name="finish",
    )(*raw_a, *raw_b, raw_c, *raw_d, p, p, p, p, p, x_all, mod, vec, glu_w, w_branch, w_out)


def _mlp_kernel(x_ref, mod_ref, g_ref, w1_ref, w2_ref, fin_ref, o_ref, xn_ref, acc_ref, *, final_norm):
    j = pl.program_id(1)

    @pl.when(j == 0)
    def _():
        h = _norm_mod(x_ref[...], g_ref[...], mod_ref[3:4, :], mod_ref[4:5, :])
        xn_ref[...] = h.astype(BF16)
        acc_ref[...] = jnp.zeros_like(acc_ref)

    a = jnp.square(jnp.maximum(_dot(xn_ref[...], w1_ref[...]), 0.0))
    acc_ref[...] += _dot(a.astype(BF16), w2_ref[...])

    @pl.when(j == pl.num_programs(1) - 1)
    def _():
        y = x_ref[...] + mod_ref[5:6, :] * acc_ref[...]
        if final_norm:
            y = (y * lax.rsqrt(jnp.mean(jnp.square(y), axis=-1, keepdims=True) + NORM_EPS)) * fin_ref[...]
        o_ref[...] = y


def _mlp(x_all, mod, norm_w, w1, w2, fin_w, layer, *, n_rows, n_batch, seq, final_norm, tm=512, tf=1024):
    mrow = _mod_row_map(n_rows, tm, n_batch, seq)
    assert D_FF % tf == 0, tf
    kern = functools.partial(_mlp_kernel, final_norm=final_norm)
    return pl.pallas_call(
        kern,
        out_shape=jax.ShapeDtypeStruct((n_rows, D_MODEL), F32),
        grid=(n_rows // tm, D_FF // tf),
        in_specs=[pl.BlockSpec((tm, D_MODEL), lambda i, j: (i, 0)),
                  pl.BlockSpec((None, 8, D_MODEL), lambda i, j: (mrow(i), 0, 0)),
                  pl.BlockSpec((None, 1, D_MODEL), lambda i, j: (layer, 0, 0)),
                  pl.BlockSpec((None, D_MODEL, tf), lambda i, j: (layer, 0, j)),
                  pl.BlockSpec((None, tf, D_MODEL), lambda i, j: (layer, j, 0)),
                  pl.BlockSpec((1, D_MODEL), lambda i, j: (0, 0))],
        out_specs=pl.BlockSpec((tm, D_MODEL), lambda i, j: (i, 0)),
        scratch_shapes=[pltpu.VMEM((tm, D_MODEL), BF16), pltpu.VMEM((tm, D_MODEL), F32)],
        compiler_params=_cparams(("parallel", "arbitrary")),
        name="mlp",
    )(x_all, mod, norm_w, w1, w2, fin_w.reshape(1, D_MODEL))


W_PREP_ROWS = 1024
N_MERGE_BLK = N_BRANCH * D_MODEL // W_PREP_ROWS
MERGE_OFF = GATE_OFF + 4 * NH


def _w_prep_kernel(a_ref, g_ref, main_ref, gate_ref):
    @pl.when(pl.program_id(1) == 0)
    def _():
        gate_ref[...] = jnp.zeros_like(gate_ref)
        gate_ref[0:4 * NH, :] = g_ref[0].astype(BF16)

    main_ref[...] = a_ref[0].astype(BF16)


def _split_w_in(w):
    depth = w.shape[0]
    wt = jnp.swapaxes(w, 1, 2)
    tr = W_PREP_ROWS

    def a_map(l, j):
        row = jnp.where(j < N_MERGE_BLK, MERGE_OFF + j * tr, (j - N_MERGE_BLK) * tr)
        return l, pl.multiple_of(row, 16), 0

    return pl.pallas_call(
        _w_prep_kernel,
        out_shape=(jax.ShapeDtypeStruct((depth, P_WIDTH, D_MODEL), BF16),
                   jax.ShapeDtypeStruct((depth, GATE_PAD, D_MODEL), BF16)),
        grid=(depth, P_WIDTH // tr),
        in_specs=[pl.BlockSpec((pl.Element(1), pl.Element(tr), pl.Element(D_MODEL)), a_map),
                  pl.BlockSpec((pl.Element(1), pl.Element(4 * NH), pl.Element(D_MODEL)),
                               lambda l, j: (l, GATE_OFF, 0))],
        out_specs=(pl.BlockSpec((None, tr, D_MODEL), lambda l, j: (l, j, 0)),
                   pl.BlockSpec((None, GATE_PAD, D_MODEL), lambda l, j: (l, 0, 0))),
        compiler_params=_cparams(("parallel", "arbitrary")),
        name="w_in_prep",
    )(wt, wt)


def kernel(x, c, ctx, c_ctx, w_mod, b_mod, norm_mix, norm_mlp, w_in, hgrn_lb_logits, hgrn_norm, ret_decay, ret_norm, s5_a_re, s5_a_im, s5_log_dt, s5_b_re, s5_b_im, s5_c_re, s5_c_im, s5_d, s5_glu_w, s5_glu_b, mlstm_conv_w, mlstm_conv_b, mlstm_gate_b, mlstm_norm, w_branch, w_out, w_ff1, w_ff2, final_norm):
    n_batch, seq, _ = x.shape
    ctx_len = ctx.shape[1]
    depth = w_in.shape[0]
    nl = n_batch * seq
    dims = dict(n_batch=n_batch, seq=seq)

    p_lb = jax.nn.softmax(hgrn_lb_logits.astype(F32), axis=0)
    lower_bounds = (jnp.cumsum(p_lb, axis=0) - p_lb[0]).reshape(depth, 2, 1, BR)
    lg = jnp.log1p(-jnp.exp(ret_decay.astype(F32)))
    gate_b = jnp.pad(mlstm_gate_b.reshape(depth, 1, 4 * NH).astype(F32), ((0, 0), (0, 0), (0, GATE_PAD - 4 * NH)))
    vec = jnp.stack([hgrn_norm, ret_norm, mlstm_norm, s5_d, s5_glu_b], axis=1).astype(F32)
    vec = jnp.pad(vec, ((0, 0), (0, 8 - vec.shape[1]), (0, 0)))
    w_main, w_gate = _split_w_in(w_in)
    glu_w, wb, wo = s5_glu_w.astype(BF16), w_branch.astype(BF16), w_out.astype(BF16)
    w1, w2 = w_ff1.astype(BF16), w_ff2.astype(BF16)
    norm_mix3 = norm_mix.reshape(depth, 1, D_MODEL)
    norm_mlp3 = norm_mlp.reshape(depth, 1, D_MODEL)
    conv_w = mlstm_conv_w.reshape(depth, 9, 2 * BR)
    conv_b = mlstm_conv_b.reshape(depth, 1, 2 * BR)
    s5_col, s5_row = _s5_params(s5_a_re, s5_a_im, s5_log_dt, s5_b_re, s5_b_im, s5_c_re, s5_c_im)

    c_pad = jnp.zeros((8, D_MODEL), F32).at[:n_batch].set(c).at[n_batch].set(c_ctx)
    x_all = jnp.concatenate([x.reshape(nl, D_MODEL), ctx.reshape(n_batch * ctx_len, D_MODEL)], axis=0)

    for l in range(depth):
        last = l == depth - 1
        mod = _modulation(c_pad, w_mod, b_mod, l)
        mod = jnp.pad(mod[:n_batch + 1].reshape(n_batch + 1, N_MOD, D_MODEL), ((0, 0), (0, 8 - N_MOD), (0, 0)))
        p, gates = _in_proj(x_all, mod, norm_mix3, w_main, w_gate, l, **dims)
        qk = _mlstm_conv(p, conv_w, conv_b, l, ctx_len=ctx_len, **dims)
        raw_a, raw_b, raw_d = _scans(p, gates, qk, lg[l], lower_bounds, gate_b, l, ctx_len=ctx_len, **dims)
        raw_c = _s5_scan(p, s5_col, s5_row, l, ctx_len=ctx_len, **dims)
        n_rows = nl if last else x_all.shape[0]
        x_mid = _finish(raw_a, raw_b, raw_c, raw_d, p, x_all, mod, vec, glu_w, wb, wo, l, n_rows=n_rows, **dims)
        x_all = _mlp(x_mid, mod, norm_mlp3, w1, w2, final_norm, l, n_rows=n_rows, final_norm=last, **dims)
    return x_all.reshape(n_batch, seq, D_MODEL)
```

```python
import functools
import math

import jax
import jax.numpy as jnp
from jax import lax
from jax.experimental import pallas as pl
from jax.experimental.pallas import tpu as pltpu

F32 = jnp.float32
BF16 = jnp.bfloat16

D_MODEL = 2048
N_BRANCH = 4
BR = D_MODEL // N_BRANCH
HD = 128
NH = BR // HD
S5_HG = 16
S5_G = BR // S5_HG
S5_P = 64
D_FF = 4 * D_MODEL
N_MOD = 6
GRID_W = 64
NORM_EPS = 1e-6
NEG_BIG = -1e30
F_TINY = 1e-30
S5_DT_MIN = 1e-3

COL_MERGE = 0
COL_HQ, COL_HF, COL_HV, COL_HG = 16, 17, 19, 20
COL_RQ, COL_RK, COL_RV, COL_RG = 21, 22, 23, 24
COL_SU = 25
COL_MQK, COL_MV, COL_MZ = 26, 28, 29
P_WIDTH = 30 * BR
GATE_OFF = 14 * BR
GATE_PAD = 128

T_HGRN = 128
T_ATTN = 256
S5_T = 128
S5_PP = 128
S5_ROWS = 40
VMEM_LIMIT = 56 * 1024 * 1024

_HI = lax.Precision.HIGHEST


def _cparams(sem):
    return pltpu.CompilerParams(dimension_semantics=sem, vmem_limit_bytes=VMEM_LIMIT)


def _dot(a, b):
    return jnp.dot(a, b, preferred_element_type=F32)


def _dot_nt(a, b):
    return lax.dot_general(a, b, (((1,), (1,)), ((), ())), preferred_element_type=F32)


def _dot_tn(a, b):
    return lax.dot_general(a, b, (((0,), (0,)), ((), ())), preferred_element_type=F32)


def _split3(x):
    hi = x.astype(BF16)
    r1 = x - hi.astype(F32)
    mid = r1.astype(BF16)
    return hi, mid, (r1 - mid.astype(F32)).astype(BF16)


def _split2(x):
    hi = x.astype(BF16)
    return hi, (x - hi.astype(F32)).astype(BF16)


def _cumsum_rows(tri, x):
    hi, mid = _split2(x)
    return _dot(tri, hi) + _dot(tri, mid)


def _sigmoid(x):
    return 1.0 / (1.0 + jnp.exp(-x))


def _silu(x):
    return x * _sigmoid(x)


def _mod_kernel(c_ref, w_ref, b_ref, o_ref):
    @pl.when(pl.program_id(0) == 0)
    def _():
        o_ref[...] = jnp.broadcast_to(b_ref[...], o_ref.shape)

    a = _silu(c_ref[...]).astype(BF16)
    o_ref[...] += _dot(a, w_ref[...].astype(BF16))


def _modulation(c_pad, w_mod, b_mod, layer, tk=256):
    n = w_mod.shape[2]
    return pl.pallas_call(
        _mod_kernel,
        out_shape=jax.ShapeDtypeStruct((8, n), F32),
        grid=(D_MODEL // tk,),
        in_specs=[pl.BlockSpec((8, tk), lambda k: (0, k)),
                  pl.BlockSpec((None, tk, n), lambda k: (layer, k, 0)),
                  pl.BlockSpec((None, 1, n), lambda k: (layer, 0, 0))],
        out_specs=pl.BlockSpec((8, n), lambda k: (0, 0)),
        compiler_params=_cparams(("arbitrary",)),
        name="modulation",
    )(c_pad, w_mod, b_mod.reshape(b_mod.shape[0], 1, n))


def _norm_mod(x, g, shift, scale):
    y = x * lax.rsqrt(jnp.mean(jnp.square(x), axis=-1, keepdims=True) + NORM_EPS)
    return (y * g) * (1.0 + scale) + shift


def _inproj_kernel(x_ref, mod_ref, g_ref, w_ref, wg_ref, p_ref, gate_ref, xn_ref):
    @pl.when(pl.program_id(1) == 0)
    def _():
        h = _norm_mod(x_ref[...], g_ref[...], mod_ref[0:1, :], mod_ref[1:2, :])
        xn_ref[...] = h.astype(BF16)
        gate_ref[...] = _dot_nt(xn_ref[...], wg_ref[...])

    p_ref[...] = _dot_nt(xn_ref[...], w_ref[...]).astype(p_ref.dtype)


def _mod_row_map(n_rows, tm, n_batch, seq):
    assert n_rows % tm == 0 and seq % tm == 0, (n_rows, seq, tm)
    n_lat_tiles, tiles_per_batch = n_batch * seq // tm, seq // tm

    def row(i):
        return jnp.where(i < n_lat_tiles, i // tiles_per_batch, n_batch)
    return row


def _in_proj(x_all, mod, norm_w, w_main, w_gate, layer, *, n_batch, seq, tm=1024, tn=1536):
    n = x_all.shape[0]
    assert P_WIDTH % tn == 0, tn
    mrow = _mod_row_map(n, tm, n_batch, seq)
    return pl.pallas_call(
        _inproj_kernel,
        out_shape=(jax.ShapeDtypeStruct((n, P_WIDTH), BF16),
                   jax.ShapeDtypeStruct((n, GATE_PAD), F32)),
        grid=(n // tm, P_WIDTH // tn),
        in_specs=[pl.BlockSpec((tm, D_MODEL), lambda i, j: (i, 0)),
                  pl.BlockSpec((None, 8, D_MODEL), lambda i, j: (mrow(i), 0, 0)),
                  pl.BlockSpec((None, 1, D_MODEL), lambda i, j: (layer, 0, 0)),
                  pl.BlockSpec((None, tn, D_MODEL), lambda i, j: (layer, j, 0)),
                  pl.BlockSpec((None, GATE_PAD, D_MODEL), lambda i, j: (layer, 0, 0))],
        out_specs=(pl.BlockSpec((tm, tn), lambda i, j: (i, j)),
                   pl.BlockSpec((tm, GATE_PAD), lambda i, j: (i, 0))),
        scratch_shapes=[pltpu.VMEM((tm, D_MODEL), BF16)],
        compiler_params=_cparams(("parallel", "arbitrary")),
        name="in_proj",
    )(x_all, mod, norm_w, w_main, w_gate)


CONV_ROWS = 256


def _conv_kernel(up_ref, x_ref, dn_ref, w_ref, b_ref, o_ref, *, n_lat_blocks, blocks_per_img):
    rb = pl.program_id(0)
    x = x_ref[...].astype(F32)
    w = w_ref[...]
    n = CONV_ROWS

    @pl.when(rb < n_lat_blocks)
    def _():
        rr = rb % blocks_per_img
        up = jnp.where(rr > 0, up_ref[...].astype(F32), 0.0)
        dn = jnp.where(rr < blocks_per_img - 1, dn_ref[...].astype(F32), 0.0)
        ext = jnp.concatenate([up, x, dn], axis=0)
        ne = n + 2 * GRID_W
        col = lax.broadcasted_iota(jnp.int32, (ne, 1), 0) % GRID_W
        xl = jnp.where(col >= 1, pltpu.roll(ext, 1, 0), 0.0)
        xr = jnp.where(col <= GRID_W - 2, pltpu.roll(ext, ne - 1, 0), 0.0)
        acc = None
        for i in range(3):
            y = w[3 * i:3 * i + 1] * xl + w[3 * i + 1:3 * i + 2] * ext + w[3 * i + 2:3 * i + 3] * xr
            part = y[i * GRID_W:i * GRID_W + n]
            acc = part if acc is None else acc + part
        o_ref[...] = _silu(acc + b_ref[...]).astype(o_ref.dtype)

    @pl.when(rb >= n_lat_blocks)
    def _():
        t = lax.broadcasted_iota(jnp.int32, (n, 1), 0)
        xl = jnp.where(t >= 1, pltpu.roll(x, 1, 0), 0.0)
        xr = jnp.where(t <= n - 2, pltpu.roll(x, n - 1, 0), 0.0)
        acc = w[3:4] * xl + w[4:5] * x + w[5:6] * xr
        o_ref[...] = _silu(acc + b_ref[...]).astype(o_ref.dtype)


def _mlstm_conv(p, conv_w, conv_b, layer, *, n_batch, seq, ctx_len, cb=2 * BR):
    assert ctx_len == CONV_ROWS and seq % CONV_ROWS == 0
    n = p.shape[0]
    n_blocks = n // CONV_ROWS
    hb = CONV_ROWS // GRID_W
    n_halo = n // GRID_W
    c0 = COL_MQK * BR // cb
    kern = functools.partial(_conv_kernel, n_lat_blocks=n_batch * seq // CONV_ROWS,
                             blocks_per_img=seq // CONV_ROWS)
    return pl.pallas_call(
        kern,
        out_shape=jax.ShapeDtypeStruct((n, 2 * BR), BF16),
        grid=(n_blocks, 2 * BR // cb),
        in_specs=[pl.BlockSpec((GRID_W, cb), lambda r, c: (jnp.maximum(r * hb - 1, 0), c0 + c)),
                  pl.BlockSpec((CONV_ROWS, cb), lambda r, c: (r, c0 + c)),
                  pl.BlockSpec((GRID_W, cb), lambda r, c: (jnp.minimum((r + 1) * hb, n_halo - 1), c0 + c)),
                  pl.BlockSpec((None, 9, cb), lambda r, c: (layer, 0, c)),
                  pl.BlockSpec((None, 1, cb), lambda r, c: (layer, 0, c))],
        out_specs=pl.BlockSpec((CONV_ROWS, cb), lambda r, c: (r, c)),
        compiler_params=_cparams(("parallel", "parallel")),
        name="mlstm_conv",
    )(p, p, p, conv_w, conv_b)


def _chunk_block(b, d, i, *, n_ctx, n_lat, n_batch):
    ctx_j = i if d == 0 else n_ctx - 1 - i
    lat_j = i - n_ctx if d == 0 else n_ctx + n_lat - 1 - i
    return jnp.where(i < n_ctx, n_batch * n_lat + b * n_ctx + ctx_j, b * n_lat + lat_j)


def _order_mask(d, t):
    r = lax.broadcasted_iota(jnp.int32, (t, t), 0)
    c = lax.broadcasted_iota(jnp.int32, (t, t), 1)
    return (c <= r) if d == 0 else (c >= r)


def _init_at_first_step(*refs):
    @pl.when(pl.program_id(1) == 0)
    def _():
        for ref in refs:
            ref[...] = jnp.zeros_like(ref)


def _ret_direction(d, lg_ref, q_ref, k_ref, v_ref, o_ref, s_ref):
    t = T_ATTN
    r = lax.broadcasted_iota(jnp.int32, (t, t), 0)
    c = lax.broadcasted_iota(jnp.int32, (t, t), 1)
    rel = (r - c) if d == 0 else (c - r)
    relf = jnp.maximum(rel, 0).astype(F32)
    tt = lax.broadcasted_iota(jnp.int32, (t, 1), 0)
    pos = (tt if d == 0 else t - 1 - tt).astype(F32)
    scale = HD ** -0.5
    for h in range(NH):
        sl = slice(h * HD, (h + 1) * HD)
        lg = lg_ref[d, h]
        q = q_ref[:, sl]
        k = k_ref[:, sl]
        v = v_ref[:, sl]
        intra = jnp.where(rel >= 0, jnp.exp(lg * relf), 0.0)
        scores = (_dot_nt(q, k) * scale) * intra
        s_old = s_ref[d, h]
        out = _dot(scores.astype(BF16), v) + jnp.exp(lg * (pos + 1.0)) * _dot(q, s_old.astype(BF16))
        kd = (k.astype(F32) * (scale * jnp.exp(lg * (t - 1.0 - pos)))).astype(BF16)
        s_ref[d, h] = jnp.exp(lg * t) * s_old + _dot_tn(kd, v)
        o_ref[:, sl] = out


def _hgrn_direction(d, lb_ref, q_ref, z_ref, v_ref, o_ref, st_ref):
    t = T_HGRN
    fwd = d == 0
    lb = lb_ref[...]
    z = z_ref[...].astype(F32)
    sig = _sigmoid(z)
    f = lb + (1.0 - lb) * sig
    logf = jnp.log(jnp.maximum(f, F_TINY))
    kk = (1.0 - lb) * (1.0 - sig)
    qq = _silu(q_ref[...].astype(F32))
    r2 = lax.broadcasted_iota(jnp.int32, (t, t), 0)
    c2 = lax.broadcasted_iota(jnp.int32, (t, t), 1)
    tri = jnp.where((c2 <= r2) if fwd else (c2 >= r2), 1.0, 0.0).astype(BF16)
    cum = _cumsum_rows(tri, logf)
    xdiff = jnp.where((c2 < r2) if fwd else (c2 > r2), r2 ^ c2, 0)
    row = lax.broadcasted_iota(jnp.int32, (t, 1), 0)
    if fwd:
        e_ref = jnp.where(row == 0, 0.0, pltpu.roll(cum, 1, 0))
    else:
        e_ref = jnp.where(row == t - 1, 0.0, pltpu.roll(cum, t - 1, 0))
    f_ref = cum
    levels = []
    w = 1
    while w < t:
        if w == 1:
            q_l, k_l = (qq * jnp.maximum(f, F_TINY)).astype(BF16), kk.astype(BF16)
        else:
            q_l = (qq * jnp.exp(cum - e_ref)).astype(BF16)
            k_l = (kk * jnp.exp(f_ref - cum)).astype(BF16)
        levels.append((q_l, k_l, (xdiff >> (w.bit_length() - 1)) == 1))
        upper = (row & w) != 0
        if fwd:
            e_ref = jnp.where(upper, pltpu.roll(e_ref, w, 0), e_ref)
            f_ref = jnp.where(upper, f_ref, pltpu.roll(f_ref, t - w, 0))
        else:
            e_ref = jnp.where(upper, e_ref, pltpu.roll(e_ref, t - w, 0))
            f_ref = jnp.where(upper, pltpu.roll(f_ref, w, 0), f_ref)
        w *= 2
    q_b = qq.astype(BF16)
    k_b = kk.astype(BF16)
    tail = cum[t - 1:t] if fwd else cum[0:1]
    q_state = (qq * jnp.exp(cum)).astype(BF16)
    k_state = (kk * jnp.exp(tail - cum)).astype(BF16)
    decay = jnp.exp(tail)
    for h in range(NH):
        sl = slice(h * HD, (h + 1) * HD)
        v_h = v_ref[:, sl]
        scores = jnp.where(r2 == c2, _dot_nt(q_b[:, sl], k_b[:, sl]), 0.0)
        for q_l, k_l, m in levels:
            scores = jnp.where(m, _dot_nt(q_l[:, sl], k_l[:, sl]), scores)
        st = st_ref[d, h]
        o_ref[:, sl] = _dot(scores.astype(BF16), v_h) + _dot_nt(q_state[:, sl], st.astype(BF16))
        st_ref[d, h] = st * decay[:, sl] + _dot_tn(v_h, k_state[:, sl])


def _hgrn_kernel(lbf_ref, qf_ref, zf_ref, vf_ref, lbb_ref, qb_ref, zb_ref, vb_ref, of_ref, ob_ref, st_ref):
    _init_at_first_step(st_ref)
    _hgrn_direction(0, lbf_ref, qf_ref, zf_ref, vf_ref, of_ref, st_ref)
    _hgrn_direction(1, lbb_ref, qb_ref, zb_ref, vb_ref, ob_ref, st_ref)


def _log_sigmoid(x):
    return jnp.minimum(x, 0.0) - jnp.log(1.0 + jnp.exp(-jnp.abs(x)))


def _rep_lane(parts, lane, width):
    sel = jnp.where(lax.broadcasted_iota(jnp.int32, (GATE_PAD, width), 0) == lane, 1.0, 0.0).astype(BF16)
    hi, mid = parts
    return _dot(hi, sel) + _dot(mid, sel)


def _mlstm_direction(d, qk_ref, v_ref, g_ref, gb_ref, o_ref, c_ref, m_ref):
    t = T_ATTN
    assert t == 2 * HD
    fwd = d == 0
    g = g_ref[...] + gb_ref[...]
    r2 = lax.broadcasted_iota(jnp.int32, (t, t), 0)
    c2 = lax.broadcasted_iota(jnp.int32, (t, t), 1)
    mask = (c2 <= r2) if fwd else (c2 >= r2)
    tri = jnp.where(mask, 1.0, 0.0).astype(BF16)
    eye = jnp.where(r2 == c2, 1.0, 0.0).astype(BF16)
    lf = _log_sigmoid(g)
    cum_c = _cumsum_rows(tri, lf)
    a_col = g - pltpu.roll(cum_c, GATE_PAD - NH, 1)
    row = lax.broadcasted_iota(jnp.int32, (t, 1), 0)
    pm = a_col
    k = 1
    while k < t:
        if fwd:
            pm = jnp.maximum(pm, jnp.where(row >= k, pltpu.roll(pm, k, 0), NEG_BIG))
        else:
            pm = jnp.maximum(pm, jnp.where(row < t - k, pltpu.roll(pm, t - k, 0), NEG_BIG))
        k *= 2
    a_parts, pm_parts, cum_parts = _split2(a_col), _split2(pm), _split2(cum_c)
    a_rows = _dot_tn(a_parts[0], eye) + _dot_tn(a_parts[1], eye)
    scale = HD ** -0.5
    last = t - 1 if fwd else 0
    ones = jnp.ones((t, HD), BF16)
    for h in range(NH):
        sl = slice(h * HD, (h + 1) * HD)
        li, lfw = 2 * NH * d + h, 2 * NH * d + NH + h
        m_old = m_ref[d, h]
        mx2 = jnp.maximum(_rep_lane(pm_parts, li, 2 * HD), jnp.concatenate([m_old, m_old], axis=1))
        mx = mx2[:, 0:HD]
        cc = _rep_lane(cum_parts, lfw, HD)
        a_rep = _rep_lane(a_parts, li, HD)
        a_row = a_rows[li:li + 1]
        q = qk_ref[:, sl]
        k_h = qk_ref[:, BR + h * HD:BR + (h + 1) * HD]
        v_aug = jnp.concatenate([v_ref[:, sl], ones], axis=1)
        w = jnp.where(mask, jnp.exp(a_row - mx2), 0.0)
        scores = (_dot_nt(q, k_h) * scale) * w
        w_state = jnp.exp(m_old - mx)
        c_old = c_ref[d, h]
        s_hi = scores.astype(BF16)
        s_mid = (scores - s_hi.astype(F32)).astype(BF16)
        intra = _dot(s_hi, v_aug)
        inter = _dot(q, c_old.astype(BF16))
        num = intra[:, 0:HD] + w_state * inter[:, 0:HD]
        den = (intra[:, HD:2 * HD] + _dot(s_mid, ones)) + w_state * inter[:, HD:2 * HD]
        o_ref[:, sl] = num / jnp.maximum(jnp.abs(den), jnp.exp(-(cc + mx)))
        total = cc[last:last + 1]
        m_new_rel = mx[last:last + 1]
        keep = jnp.exp(m_old - m_new_rel)
        w_end = jnp.exp(a_rep - m_new_rel)
        kw = (k_h.astype(F32) * (scale * w_end)).astype(BF16)
        c_ref[d, h] = jnp.concatenate([keep, keep], axis=1) * c_old + _dot_tn(kw, v_aug)
        m_ref[d, h] = total + m_new_rel


def _ret_mlstm_kernel(lg_ref, rqf_ref, rkf_ref, rvf_ref, rqb_ref, rkb_ref, rvb_ref,
                      qkf_ref, vf_ref, gf_ref, qkb_ref, vb_ref, gb_ref, bias_ref,
                      rof_ref, rob_ref, mof_ref, mob_ref, s_ref, c_ref, m_ref):
    _init_at_first_step(s_ref, c_ref, m_ref)
    _ret_direction(0, lg_ref, rqf_ref, rkf_ref, rvf_ref, rof_ref, s_ref)
    _mlstm_direction(0, qkf_ref, vf_ref, gf_ref, bias_ref, mof_ref, c_ref, m_ref)
    _ret_direction(1, lg_ref, rqb_ref, rkb_ref, rvb_ref, rob_ref, s_ref)
    _mlstm_direction(1, qkb_ref, vb_ref, gb_ref, bias_ref, mob_ref, c_ref, m_ref)


def _scans(p, gates, qk, lg, lb, gate_b, layer, *, n_batch, seq, ctx_len):
    n = p.shape[0]
    state = pltpu.VMEM((2, NH, HD, HD), F32)
    vec = pltpu.VMEM((2, NH, 1, HD), F32)

    def call(kernel, t, inputs, make_specs, scratch, name, n_mixers=1):
        assert ctx_len % t == 0 and seq % t == 0, (ctx_len, seq, t)
        n_ctx, n_lat = ctx_len // t, seq // t

        def rows(d, width, cblock=0):
            return pl.BlockSpec((t, width), lambda b, i: (_chunk_block(b, d, i, n_ctx=n_ctx, n_lat=n_lat,
                                                                        n_batch=n_batch), cblock))

        out = jax.ShapeDtypeStruct((n, BR), F32)
        return pl.pallas_call(
            kernel,
            out_shape=(out, out) * n_mixers,
            grid=(n_batch, n_ctx + n_lat),
            in_specs=make_specs(rows),
            out_specs=(rows(0, BR), rows(1, BR)) * n_mixers,
            scratch_shapes=scratch,
            compiler_params=_cparams(("parallel", "arbitrary")),
            name=name,
        )(*inputs)

    lb_spec = lambda d: pl.BlockSpec((None, None, 1, BR), lambda b, i: (layer, d, 0, 0))
    raw_a = call(_hgrn_kernel, T_HGRN, (lb, p, p, p, lb, p, p, p),
                 lambda rows: [spec for d in (0, 1)
                               for spec in (lb_spec(d), rows(d, BR, COL_HQ), rows(d, BR, COL_HF + d),
                                            rows(d, BR, COL_HV))],
                 [state], "hgrn2_scan")
    raw_bd = call(_ret_mlstm_kernel, T_ATTN, (lg, p, p, p, p, p, p, qk, p, gates, qk, p, gates, gate_b),
                  lambda rows: [pl.BlockSpec(memory_space=pltpu.SMEM)]
                  + [rows(d, BR, c) for d in (0, 1) for c in (COL_RQ, COL_RK, COL_RV)]
                  + [spec for d in (0, 1)
                     for spec in (rows(d, 2 * BR), rows(d, BR, COL_MV), rows(d, GATE_PAD))]
                  + [pl.BlockSpec((None, 1, GATE_PAD), lambda b, i: (layer, 0, 0))],
                  [state, pltpu.VMEM((2, NH, HD, 2 * HD), F32), vec], "retention_mlstm_scan", n_mixers=2)
    return raw_a, raw_bd[0:2], raw_bd[2:4]


def _cmul(ar, ai, br, bi):
    return ar * br - ai * bi, ar * bi + ai * br


def _pow_table(lam_r, lam_i, expo):
    pr = jnp.ones(expo.shape, F32)
    pi = jnp.zeros(expo.shape, F32)
    ar = jnp.broadcast_to(lam_r, expo.shape)
    ai = jnp.broadcast_to(lam_i, expo.shape)
    for k in range(S5_T.bit_length() - 1):
        nr, ni = _cmul(pr, pi, ar, ai)
        bit = (expo & (1 << k)) != 0
        pr, pi = jnp.where(bit, nr, pr), jnp.where(bit, ni, pi)
        ar, ai = _cmul(ar, ai, ar, ai)
    return pr, pi


def _s5_operators(d, u, col_ref, row_ref, m_ref, w_ref, ys_ref, z_ref, wr_ref, wi_ref):
    t = S5_T
    fwd = d == 0
    col = col_ref[d]
    rowp = row_ref[d]
    lam_rc, lam_ic = col[:, 0:1], col[:, 1:2]
    bbr, bbi = rowp[0:S5_HG], rowp[S5_HG:2 * S5_HG]
    lam_rr, lam_ir = rowp[2 * S5_HG:2 * S5_HG + 1], rowp[2 * S5_HG + 1:2 * S5_HG + 2]
    lane = lax.broadcasted_iota(jnp.int32, (t, t), 1)
    sub = lax.broadcasted_iota(jnp.int32, (t, t), 0)

    pr, pi = _pow_table(lam_rc, lam_ic, lane if fwd else t - 1 - lane)
    qr, qi = _cmul(pr, pi, lam_rc, lam_ic)
    for h in range(S5_HG):
        cr, ci = col[:, 2 + h:3 + h], col[:, 2 + S5_HG + h:3 + S5_HG + h]
        hs = slice(h * t, (h + 1) * t)
        z_ref[d, 0:S5_PP, hs] = cr * pr - ci * pi
        z_ref[d, S5_PP:2 * S5_PP, hs] = -(cr * pi + ci * pr)
        ys_ref[d, 0:S5_PP, hs] = (cr * qr - ci * qi).astype(BF16)
        ys_ref[d, S5_PP:2 * S5_PP, hs] = (-(cr * qi + ci * qr)).astype(BF16)
    b_hi, b_mid, _ = _split3(jnp.concatenate([bbr, bbi], axis=1))
    z_hi, z_mid, _ = _split3(z_ref[d])
    krow = _dot(b_hi, z_hi) + (_dot(b_hi, z_mid) + _dot(b_mid, z_hi))
    shift, keep = (0, lane >= sub) if fwd else (1, lane <= sub)
    for hp in range(S5_HG):
        for h in range(S5_HG):
            tile = jnp.broadcast_to(krow[hp:hp + 1, h * t:(h + 1) * t], (t, t))
            tile = pltpu.roll(tile, shift, 1, stride=1, stride_axis=0)
            m_ref[d, hp * t:(hp + 1) * t, h * t:(h + 1) * t] = jnp.where(keep, tile, 0.0).astype(BF16)
    tr, ti = _pow_table(lam_rr, lam_ir, t - 1 - sub if fwd else sub)
    for hp in range(S5_HG):
        br, bi = bbr[hp:hp + 1], bbi[hp:hp + 1]
        w_ref[d, hp * t:(hp + 1) * t, 0:S5_PP] = (tr * br - ti * bi).astype(BF16)
        w_ref[d, hp * t:(hp + 1) * t, S5_PP:2 * S5_PP] = (tr * bi + ti * br).astype(BF16)
    wv = _dot(u, w_ref[d])
    wr_ref[d] = wv[:, 0:S5_PP]
    wi_ref[d] = wv[:, S5_PP:2 * S5_PP]
    lr, li = lam_rr, lam_ir
    for _ in range(t.bit_length() - 1):
        lr, li = _cmul(lr, li, lr, li)
    return lr, li


def _s5_kernel(u_ref, col_ref, row_ref, o_ref, m_ref, w_ref, ys_ref, z_ref, wr_ref, wi_ref, xr_ref, xi_ref, *,
               n_batch, n_ctx, n_lat):
    t = S5_T
    u = jnp.concatenate([u_ref[hp] for hp in range(S5_HG)], axis=1)
    lam_t = [_s5_operators(d, u, col_ref, row_ref, m_ref, w_ref, ys_ref, z_ref, wr_ref, wi_ref) for d in (0, 1)]

    def make_step(base, stride, count):
        def step(i, carry):
            out = []
            for d in (0, 1):
                xr, xi = carry[2 * d], carry[2 * d + 1]
                j = i if d == 0 else count - 1 - i
                idx = pl.ds(base + j, n_batch, stride=stride)
                xr_ref[d, idx, :] = xr
                xi_ref[d, idx, :] = xi
                nr, ni = _cmul(xr, xi, *lam_t[d])
                out += [nr + wr_ref[d, idx, :], ni + wi_ref[d, idx, :]]
            return tuple(out)
        return step

    zero = jnp.zeros((n_batch, S5_PP), F32)
    carry = lax.fori_loop(0, n_ctx, make_step(n_batch * n_lat, n_ctx, n_ctx), (zero,) * 4)
    lax.fori_loop(0, n_lat, make_step(0, n_lat, n_lat), carry)
    y = None
    for d in (0, 1):
        x_prev = jnp.concatenate([xr_ref[d], xi_ref[d]], axis=1).astype(BF16)
        y_d = _dot(u, m_ref[d]) + _dot(x_prev, ys_ref[d])
        y = y_d if y is None else y + y_d
    for h in range(S5_HG):
        o_ref[h] = y[:, h * t:(h + 1) * t]


def _s5_params(a_re, a_im, log_dt, b_re, b_im, c_re, c_im):
    a_re, a_im = a_re.astype(F32), a_im.astype(F32)
    dt = jnp.exp(log_dt.astype(F32))[..., None]
    mag = jnp.exp(a_re * dt)
    lam_re, lam_im = mag * jnp.cos(a_im * dt), mag * jnp.sin(a_im * dt)
    den = a_re * a_re + a_im * a_im
    num_re, num_im = lam_re - 1.0, lam_im
    fr = (num_re * a_re + num_im * a_im) / den
    fi = (num_im * a_re - num_re * a_im) / den
    b_re, b_im = b_re.astype(F32), b_im.astype(F32)
    bb_re = fr[..., None] * b_re - fi[..., None] * b_im
    bb_im = fr[..., None] * b_im + fi[..., None] * b_re
    c_re_t = jnp.swapaxes(c_re.astype(F32), -1, -2)
    c_im_t = jnp.swapaxes(c_im.astype(F32), -1, -2)
    col = jnp.concatenate([lam_re[..., None], lam_im[..., None], c_re_t, c_im_t], axis=-1)
    col = jnp.pad(col, ((0, 0),) * 3 + ((0, S5_PP - S5_P), (0, 128 - col.shape[-1])))
    row = jnp.concatenate([jnp.swapaxes(bb_re, -1, -2), jnp.swapaxes(bb_im, -1, -2),
                           lam_re[..., None, :], lam_im[..., None, :]], axis=-2)
    row = jnp.pad(row, ((0, 0),) * 3 + ((0, S5_ROWS - row.shape[-2]), (0, S5_PP - S5_P)))
    return col, row


def _s5_scan(p, col, row, layer, *, n_batch, seq, ctx_len):
    t = S5_T
    n = p.shape[0]
    nc = n // t
    assert ctx_len % t == 0 and seq % t == 0, (ctx_len, seq, t)
    n_ctx, n_lat = ctx_len // t, seq // t
    ut = p[:, COL_SU * BR:(COL_SU + 1) * BR].T.reshape(S5_G, S5_HG, nc, t)
    kern = functools.partial(_s5_kernel, n_batch=n_batch, n_ctx=n_ctx, n_lat=n_lat)
    yt = pl.pallas_call(
        kern,
        out_shape=jax.ShapeDtypeStruct((S5_G, S5_HG, nc, t), F32),
        grid=(S5_G,),
        in_specs=[pl.BlockSpec((None, S5_HG, nc, t), lambda g: (g, 0, 0, 0)),
                  pl.BlockSpec((None, 2, None, S5_PP, 128), lambda g: (layer, 0, g, 0, 0)),
                  pl.BlockSpec((None, 2, None, S5_ROWS, S5_PP), lambda g: (layer, 0, g, 0, 0))],
        out_specs=pl.BlockSpec((None, S5_HG, nc, t), lambda g: (g, 0, 0, 0)),
        scratch_shapes=[pltpu.VMEM((2, S5_HG * t, S5_HG * t), BF16),
                        pltpu.VMEM((2, S5_HG * t, 2 * S5_PP), BF16),
                        pltpu.VMEM((2, 2 * S5_PP, S5_HG * t), BF16),
                        pltpu.VMEM((2, 2 * S5_PP, S5_HG * t), F32),
                        pltpu.VMEM((2, nc, S5_PP), F32), pltpu.VMEM((2, nc, S5_PP), F32),
                        pltpu.VMEM((2, nc, S5_PP), F32), pltpu.VMEM((2, nc, S5_PP), F32)],
        compiler_params=_cparams(("parallel",)),
        name="s5_scan",
    )(ut, col, row)
    return yt.reshape(BR, n).T


def _head_rms(x, g):
    parts = []
    for h in range(NH):
        xh = x[:, h * HD:(h + 1) * HD]
        parts.append(xh * lax.rsqrt(jnp.mean(jnp.square(xh), axis=-1, keepdims=True) + NORM_EPS))
    return jnp.concatenate(parts, axis=-1) * g


def _finish_kernel(raf_ref, rab_ref, rbf_ref, rbb_ref, rc_ref, rdf_ref, rdb_ref, ga_ref, gb_ref, u_ref, gd_ref,
                   mg_ref, x_ref, mod_ref, vec_ref, glu_w_ref, wb_ref, wo_ref, o_ref):
    vec = vec_ref[...]
    oa = _head_rms(raf_ref[...] + rab_ref[...], vec[0:1]) * _silu(ga_ref[...].astype(F32))
    ob = _head_rms(rbf_ref[...] + rbb_ref[...], vec[1:2]) * _silu(gb_ref[...].astype(F32))
    od = _head_rms(rdf_ref[...] + rdb_ref[...], vec[2:3]) * _silu(gd_ref[...].astype(F32))
    yc = rc_ref[...] + vec[3:4] * u_ref[...].astype(F32)
    yc = 0.5 * yc * (1.0 + lax.erf(yc * (2.0 ** -0.5)))
    oc = yc * _sigmoid(_dot(yc.astype(BF16), glu_w_ref[...]) + vec[4:5])
    y2 = None
    for j, o in enumerate((oa, ob, oc, od)):
        gate2 = jnp.tanh(0.5 * mg_ref[:, j * D_MODEL:(j + 1) * D_MODEL].astype(F32)) + 1.0
        term = gate2 * _dot(o.astype(BF16), wb_ref[j])
        y2 = term if y2 is None else y2 + term
    mix2 = _dot(y2.astype(BF16), wo_ref[...])
    o_ref[...] = x_ref[...] + (0.5 * mod_ref[2:3, :]) * mix2


def _finish(raw_a, raw_b, raw_c, raw_d, p, x_all, mod, vec, glu_w, w_branch, w_out, layer, *, n_rows, n_batch,
            seq, tm=256):
    mrow = _mod_row_map(n_rows, tm, n_batch, seq)
    one = pl.Buffered(1)

    def pcol(cblock):
        return pl.BlockSpec((tm, BR), lambda i: (i, cblock))

    raw = pl.BlockSpec((tm, BR), lambda i: (i, 0))
    return pl.pallas_call(
        _finish_kernel,
        out_shape=jax.ShapeDtypeStruct((n_rows, D_MODEL), F32),
        grid=(n_rows // tm,),
        in_specs=[raw, raw, raw, raw, raw, raw, raw,
                  pcol(COL_HG), pcol(COL_RG), pcol(COL_SU), pcol(COL_MZ),
                  pl.BlockSpec((tm, N_BRANCH * D_MODEL), lambda i: (i, COL_MERGE)),
                  pl.BlockSpec((tm, D_MODEL), lambda i: (i, 0)),
                  pl.BlockSpec((None, 8, D_MODEL), lambda i: (mrow(i), 0, 0)),
                  pl.BlockSpec((None, 8, BR), lambda i: (layer, 0, 0), pipeline_mode=one),
                  pl.BlockSpec((None, BR, BR), lambda i: (layer, 0, 0), pipeline_mode=one),
                  pl.BlockSpec((None, N_BRANCH, BR, D_MODEL), lambda i: (layer, 0, 0, 0), pipeline_mode=one),
                  pl.BlockSpec((None, D_MODEL, D_MODEL), lambda i: (layer, 0, 0), pipeline_mode=one)],
        out_specs=pl.BlockSpec((tm, D_MODEL), lambda i: (i, 0)),
        compiler_params=_cparams(("parallel",)),
        name="finish",
    )(*raw_a, *raw_b, raw_c, *raw_d, p, p, p, p, p, x_all, mod, vec, glu_w, w_branch, w_out)


def _mlp_kernel(x_ref, mod_ref, g_ref, w1_ref, w2_ref, fin_ref, o_ref, xn_ref, acc_ref, *, final_norm):
    j = pl.program_id(1)

    @pl.when(j == 0)
    def _():
        h = _norm_mod(x_ref[...], g_ref[...], mod_ref[3:4, :], mod_ref[4:5, :])
        xn_ref[...] = h.astype(BF16)
        acc_ref[...] = jnp.zeros_like(acc_ref)

    a = jnp.square(jnp.maximum(_dot(xn_ref[...], w1_ref[...]), 0.0))
    acc_ref[...] += _dot(a.astype(BF16), w2_ref[...])

    @pl.when(j == pl.num_programs(1) - 1)
    def _():
        y = x_ref[...] + mod_ref[5:6, :] * acc_ref[...]
        if final_norm:
            y = (y * lax.rsqrt(jnp.mean(jnp.square(y), axis=-1, keepdims=True) + NORM_EPS)) * fin_ref[...]
        o_ref[...] = y


def _mlp(x_all, mod, norm_w, w1, w2, fin_w, layer, *, n_rows, n_batch, seq, final_norm, tm=512, tf=1024):
    mrow = _mod_row_map(n_rows, tm, n_batch, seq)
    assert D_FF % tf == 0, tf
    kern = functools.partial(_mlp_kernel, final_norm=final_norm)
    return pl.pallas_call(
        kern,
        out_shape=jax.ShapeDtypeStruct((n_rows, D_MODEL), F32),
        grid=(n_rows // tm, D_FF // tf),
        in_specs=[pl.BlockSpec((tm, D_MODEL), lambda i, j: (i, 0)),
                  pl.BlockSpec((None, 8, D_MODEL), lambda i, j: (mrow(i), 0, 0)),
                  pl.BlockSpec((None, 1, D_MODEL), lambda i, j: (layer, 0, 0)),
                  pl.BlockSpec((None, D_MODEL, tf), lambda i, j: (layer, 0, j)),
                  pl.BlockSpec((None, tf, D_MODEL), lambda i, j: (layer, j, 0)),
                  pl.BlockSpec((1, D_MODEL), lambda i, j: (0, 0))],
        out_specs=pl.BlockSpec((tm, D_MODEL), lambda i, j: (i, 0)),
        scratch_shapes=[pltpu.VMEM((tm, D_MODEL), BF16), pltpu.VMEM((tm, D_MODEL), F32)],
        compiler_params=_cparams(("parallel", "arbitrary")),
        name="mlp",
    )(x_all, mod, norm_w, w1, w2, fin_w.reshape(1, D_MODEL))


W_PREP_ROWS = 1024
N_MERGE_BLK = N_BRANCH * D_MODEL // W_PREP_ROWS
MERGE_OFF = GATE_OFF + 4 * NH


def _w_prep_kernel(a_ref, g_ref, main_ref, gate_ref):
    @pl.when(pl.program_id(1) == 0)
    def _():
        gate_ref[...] = jnp.zeros_like(gate_ref)
        gate_ref[0:4 * NH, :] = g_ref[0].astype(BF16)

    main_ref[...] = a_ref[0].astype(BF16)


def _split_w_in(w):
    depth = w.shape[0]
    wt = jnp.swapaxes(w, 1, 2)
    tr = W_PREP_ROWS

    def a_map(l, j):
        row = jnp.where(j < N_MERGE_BLK, MERGE_OFF + j * tr, (j - N_MERGE_BLK) * tr)
        return l, pl.multiple_of(row, 16), 0

    return pl.pallas_call(
        _w_prep_kernel,
        out_shape=(jax.ShapeDtypeStruct((depth, P_WIDTH, D_MODEL), BF16),
                   jax.ShapeDtypeStruct((depth, GATE_PAD, D_MODEL), BF16)),
        grid=(depth, P_WIDTH // tr),
        in_specs=[pl.BlockSpec((pl.Element(1), pl.Element(tr), pl.Element(D_MODEL)), a_map),
                  pl.BlockSpec((pl.Element(1), pl.Element(4 * NH), pl.Element(D_MODEL)),
                               lambda l, j: (l, GATE_OFF, 0))],
        out_specs=(pl.BlockSpec((None, tr, D_MODEL), lambda l, j: (l, j, 0)),
                   pl.BlockSpec((None, GATE_PAD, D_MODEL), lambda l, j: (l, 0, 0))),
        compiler_params=_cparams(("parallel", "arbitrary")),
        name="w_in_prep",
    )(wt, wt)


def kernel(x, c, ctx, c_ctx, w_mod, b_mod, norm_mix, norm_mlp, w_in, hgrn_lb_logits, hgrn_norm, ret_decay, ret_norm, s5_a_re, s5_a_im, s5_log_dt, s5_b_re, s5_b_im, s5_c_re, s5_c_im, s5_d, s5_glu_w, s5_glu_b, mlstm_conv_w, mlstm_conv_b, mlstm_gate_b, mlstm_norm, w_branch, w_out, w_ff1, w_ff2, final_norm):
    n_batch, seq, _ = x.shape
    ctx_len = ctx.shape[1]
    depth = w_in.shape[0]
    nl = n_batch * seq
    dims = dict(n_batch=n_batch, seq=seq)

    p_lb = jax.nn.softmax(hgrn_lb_logits.astype(F32), axis=0)
    lower_bounds = (jnp.cumsum(p_lb, axis=0) - p_lb[0]).reshape(depth, 2, 1, BR)
    lg = jnp.log1p(-jnp.exp(ret_decay.astype(F32)))
    gate_b = jnp.pad(mlstm_gate_b.reshape(depth, 1, 4 * NH).astype(F32), ((0, 0), (0, 0), (0, GATE_PAD - 4 * NH)))
    vec = jnp.stack([hgrn_norm, ret_norm, mlstm_norm, s5_d, s5_glu_b], axis=1).astype(F32)
    vec = jnp.pad(vec, ((0, 0), (0, 8 - vec.shape[1]), (0, 0)))
    w_main, w_gate = _split_w_in(w_in)
    glu_w, wb, wo = s5_glu_w.astype(BF16), w_branch.astype(BF16), w_out.astype(BF16)
    w1, w2 = w_ff1.astype(BF16), w_ff2.astype(BF16)
    norm_mix3 = norm_mix.reshape(depth, 1, D_MODEL)
    norm_mlp3 = norm_mlp.reshape(depth, 1, D_MODEL)
    conv_w = mlstm_conv_w.reshape(depth, 9, 2 * BR)
    conv_b = mlstm_conv_b.reshape(depth, 1, 2 * BR)
    s5_col, s5_row = _s5_params(s5_a_re, s5_a_im, s5_log_dt, s5_b_re, s5_b_im, s5_c_re, s5_c_im)

    c_pad = jnp.zeros((8, D_MODEL), F32).at[:n_batch].set(c).at[n_batch].set(c_ctx)
    x_all = jnp.concatenate([x.reshape(nl, D_MODEL), ctx.reshape(n_batch * ctx_len, D_MODEL)], axis=0)

    for l in range(depth):
        last = l == depth - 1
        mod = _modulation(c_pad, w_mod, b_mod, l)
        mod = jnp.pad(mod[:n_batch + 1].reshape(n_batch + 1, N_MOD, D_MODEL), ((0, 0), (0, 8 - N_MOD), (0, 0)))
        p, gates = _in_proj(x_all, mod, norm_mix3, w_main, w_gate, l, **dims)
        qk = _mlstm_conv(p, conv_w, conv_b, l, ctx_len=ctx_len, **dims)
        raw_a, raw_b, raw_d = _scans(p, gates, qk, lg[l], lower_bounds, gate_b, l, ctx_len=ctx_len, **dims)
        raw_c = _s5_scan(p, s5_col, s5_row, l, ctx_len=ctx_len, **dims)
        n_rows = nl if last else x_all.shape[0]
        x_mid = _finish(raw_a, raw_b, raw_c, raw_d, p, x_all, mod, vec, glu_w, wb, wo, l, n_rows=n_rows, **dims)
        x_all = _mlp(x_mid, mod, norm_mlp3, w1, w2, final_norm, l, n_rows=n_rows, final_norm=last, **dims)
    return x_all.reshape(n_batch, seq, D_MODEL)
```

```python
import functools
import math

import jax
import jax.numpy as jnp
from jax import lax
from jax.experimental import pallas as pl
from jax.experimental.pallas import tpu as pltpu

F32 = jnp.float32
BF16 = jnp.bfloat16

D_MODEL = 2048
N_BRANCH = 4
BR = D_MODEL // N_BRANCH
HD = 128
NH = BR // HD
S5_HG = 16
S5_G = BR // S5_HG
S5_P = 64
D_FF = 4 * D_MODEL
N_MOD = 6
GRID_W = 64
NORM_EPS = 1e-6
NEG_BIG = -1e30
F_TINY = 1e-30
S5_DT_MIN = 1e-3

COL_MERGE = 0
COL_HQ, COL_HF, COL_HV, COL_HG = 16, 17, 19, 20
COL_RQ, COL_RK, COL_RV, COL_RG = 21, 22, 23, 24
COL_SU = 25
COL_MQK, COL_MV, COL_MZ = 26, 28, 29
P_WIDTH = 30 * BR
GATE_OFF = 14 * BR
GATE_PAD = 128

T_HGRN = 128
T_ATTN = 256
S5_T = 128
S5_PP = 128
S5_ROWS = 40
VMEM_LIMIT = 56 * 1024 * 1024

_HI = lax.Precision.HIGHEST


def _cparams(sem):
    return pltpu.CompilerParams(dimension_semantics=sem, vmem_limit_bytes=VMEM_LIMIT)


def _dot(a, b):
    return jnp.dot(a, b, preferred_element_type=F32)


def _dot_nt(a, b):
    return lax.dot_general(a, b, (((1,), (1,)), ((), ())), preferred_element_type=F32)


def _dot_tn(a, b):
    return lax.dot_general(a, b, (((0,), (0,)), ((), ())), preferred_element_type=F32)


def _split3(x):
    hi = x.astype(BF16)
    r1 = x - hi.astype(F32)
    mid = r1.astype(BF16)
    return hi, mid, (r1 - mid.astype(F32)).astype(BF16)


def _split2(x):
    hi = x.astype(BF16)
    return hi, (x - hi.astype(F32)).astype(BF16)


def _cumsum_rows(tri, x):
    hi, mid = _split2(x)
    return _dot(tri, hi) + _dot(tri, mid)


def _sigmoid(x):
    return 1.0 / (1.0 + jnp.exp(-x))


def _silu(x):
    return x * _sigmoid(x)


def _mod_kernel(c_ref, w_ref, b_ref, o_ref):
    @pl.when(pl.program_id(0) == 0)
    def _():
        o_ref[...] = jnp.broadcast_to(b_ref[...], o_ref.shape)

    a = _silu(c_ref[...]).astype(BF16)
    o_ref[...] += _dot(a, w_ref[...].astype(BF16))


def _modulation(c_pad, w_mod, b_mod, layer, tk=256):
    n = w_mod.shape[2]
    return pl.pallas_call(
        _mod_kernel,
        out_shape=jax.ShapeDtypeStruct((8, n), F32),
        grid=(D_MODEL // tk,),
        in_specs=[pl.BlockSpec((8, tk), lambda k: (0, k)),
                  pl.BlockSpec((None, tk, n), lambda k: (layer, k, 0)),
                  pl.BlockSpec((None, 1, n), lambda k: (layer, 0, 0))],
        out_specs=pl.BlockSpec((8, n), lambda k: (0, 0)),
        compiler_params=_cparams(("arbitrary",)),
        name="modulation",
    )(c_pad, w_mod, b_mod.reshape(b_mod.shape[0], 1, n))


def _norm_mod(x, g, shift, scale):
    y = x * lax.rsqrt(jnp.mean(jnp.square(x), axis=-1, keepdims=True) + NORM_EPS)
    return (y * g) * (1.0 + scale) + shift


def _inproj_kernel(x_ref, mod_ref, g_ref, w_ref, wg_ref, p_ref, gate_ref, ut_ref, xn_ref, *, su_block, su_off):
    @pl.when(pl.program_id(1) == 0)
    def _():
        h = _norm_mod(x_ref[...], g_ref[...], mod_ref[0:1, :], mod_ref[1:2, :])
        xn_ref[...] = h.astype(BF16)
        gate_ref[...] = _dot_nt(xn_ref[...], wg_ref[...])

    acc = _dot_nt(xn_ref[...], w_ref[...])
    p_ref[...] = acc.astype(p_ref.dtype)

    @pl.when(pl.program_id(1) == su_block)
    def _():
        ut_ref[...] = acc[:, su_off:su_off + BR].T.astype(ut_ref.dtype)


def _mod_row_map(n_rows, tm, n_batch, seq):
    assert n_rows % tm == 0 and seq % tm == 0, (n_rows, seq, tm)
    n_lat_tiles, tiles_per_batch = n_batch * seq // tm, seq // tm

    def row(i):
        return jnp.where(i < n_lat_tiles, i // tiles_per_batch, n_batch)
    return row


def _in_proj(x_all, mod, norm_w, w_main, w_gate, layer, *, n_batch, seq, tm=1024, tn=1536):
    n = x_all.shape[0]
    assert P_WIDTH % tn == 0, tn
    su_block, su_off = divmod(COL_SU * BR, tn)
    assert su_off + BR <= tn, (su_off, tn)
    mrow = _mod_row_map(n, tm, n_batch, seq)
    return pl.pallas_call(
        functools.partial(_inproj_kernel, su_block=su_block, su_off=su_off),
        out_shape=(jax.ShapeDtypeStruct((n, P_WIDTH), BF16),
                   jax.ShapeDtypeStruct((n, GATE_PAD), F32),
                   jax.ShapeDtypeStruct((BR, n), BF16)),
        grid=(n // tm, P_WIDTH // tn),
        in_specs=[pl.BlockSpec((tm, D_MODEL), lambda i, j: (i, 0)),
                  pl.BlockSpec((None, 8, D_MODEL), lambda i, j: (mrow(i), 0, 0)),
                  pl.BlockSpec((None, 1, D_MODEL), lambda i, j: (layer, 0, 0)),
                  pl.BlockSpec((None, tn, D_MODEL), lambda i, j: (layer, j, 0)),
                  pl.BlockSpec((None, GATE_PAD, D_MODEL), lambda i, j: (layer, 0, 0))],
        out_specs=(pl.BlockSpec((tm, tn), lambda i, j: (i, j)),
                   pl.BlockSpec((tm, GATE_PAD), lambda i, j: (i, 0)),
                   pl.BlockSpec((BR, tm), lambda i, j: (0, i))),
        scratch_shapes=[pltpu.VMEM((tm, D_MODEL), BF16)],
        compiler_params=_cparams(("parallel", "arbitrary")),
        name="in_proj",
    )(x_all, mod, norm_w, w_main, w_gate)


CONV_ROWS = 256


def _conv_kernel(up_ref, x_ref, dn_ref, w_ref, b_ref, o_ref, *, n_lat_blocks, blocks_per_img):
    rb = pl.program_id(0)
    x = x_ref[...].astype(F32)
    w = w_ref[...]
    n = CONV_ROWS

    @pl.when(rb < n_lat_blocks)
    def _():
        rr = rb % blocks_per_img
        up = jnp.where(rr > 0, up_ref[...].astype(F32), 0.0)
        dn = jnp.where(rr < blocks_per_img - 1, dn_ref[...].astype(F32), 0.0)
        ext = jnp.concatenate([up, x, dn], axis=0)
        ne = n + 2 * GRID_W
        col = lax.broadcasted_iota(jnp.int32, (ne, 1), 0) % GRID_W
        xl = jnp.where(col >= 1, pltpu.roll(ext, 1, 0), 0.0)
        xr = jnp.where(col <= GRID_W - 2, pltpu.roll(ext, ne - 1, 0), 0.0)
        acc = None
        for i in range(3):
            y = w[3 * i:3 * i + 1] * xl + w[3 * i + 1:3 * i + 2] * ext + w[3 * i + 2:3 * i + 3] * xr
            part = y[i * GRID_W:i * GRID_W + n]
            acc = part if acc is None else acc + part
        o_ref[...] = _silu(acc + b_ref[...]).astype(o_ref.dtype)

    @pl.when(rb >= n_lat_blocks)
    def _():
        t = lax.broadcasted_iota(jnp.int32, (n, 1), 0)
        xl = jnp.where(t >= 1, pltpu.roll(x, 1, 0), 0.0)
        xr = jnp.where(t <= n - 2, pltpu.roll(x, n - 1, 0), 0.0)
        acc = w[3:4] * xl + w[4:5] * x + w[5:6] * xr
        o_ref[...] = _silu(acc + b_ref[...]).astype(o_ref.dtype)


def _mlstm_conv(p, conv_w, conv_b, layer, *, n_batch, seq, ctx_len, cb=2 * BR):
    assert ctx_len == CONV_ROWS and seq % CONV_ROWS == 0
    n = p.shape[0]
    n_blocks = n // CONV_ROWS
    hb = CONV_ROWS // GRID_W
    n_halo = n // GRID_W
    c0 = COL_MQK * BR // cb
    kern = functools.partial(_conv_kernel, n_lat_blocks=n_batch * seq // CONV_ROWS,
                             blocks_per_img=seq // CONV_ROWS)
    return pl.pallas_call(
        kern,
        out_shape=jax.ShapeDtypeStruct((n, 2 * BR), BF16),
        grid=(n_blocks, 2 * BR // cb),
        in_specs=[pl.BlockSpec((GRID_W, cb), lambda r, c: (jnp.maximum(r * hb - 1, 0), c0 + c)),
                  pl.BlockSpec((CONV_ROWS, cb), lambda r, c: (r, c0 + c)),
                  pl.BlockSpec((GRID_W, cb), lambda r, c: (jnp.minimum((r + 1) * hb, n_halo - 1), c0 + c)),
                  pl.BlockSpec((None, 9, cb), lambda r, c: (layer, 0, c)),
                  pl.BlockSpec((None, 1, cb), lambda r, c: (layer, 0, c))],
        out_specs=pl.BlockSpec((CONV_ROWS, cb), lambda r, c: (r, c)),
        compiler_params=_cparams(("parallel", "parallel")),
        name="mlstm_conv",
    )(p, p, p, conv_w, conv_b)


def _chunk_block(b, d, i, *, n_ctx, n_lat, n_batch):
    ctx_j = i if d == 0 else n_ctx - 1 - i
    lat_j = i - n_ctx if d == 0 else n_ctx + n_lat - 1 - i
    return jnp.where(i < n_ctx, n_batch * n_lat + b * n_ctx + ctx_j, b * n_lat + lat_j)


def _order_mask(d, t):
    r = lax.broadcasted_iota(jnp.int32, (t, t), 0)
    c = lax.broadcasted_iota(jnp.int32, (t, t), 1)
    return (c <= r) if d == 0 else (c >= r)


def _init_at_first_step(*refs):
    @pl.when(pl.program_id(1) == 0)
    def _():
        for ref in refs:
            ref[...] = jnp.zeros_like(ref)


def _ret_direction(d, lg_ref, q_ref, k_ref, v_ref, o_ref, s_ref):
    t = T_ATTN
    r = lax.broadcasted_iota(jnp.int32, (t, t), 0)
    c = lax.broadcasted_iota(jnp.int32, (t, t), 1)
    rel = (r - c) if d == 0 else (c - r)
    relf = jnp.maximum(rel, 0).astype(F32)
    tt = lax.broadcasted_iota(jnp.int32, (t, 1), 0)
    pos = (tt if d == 0 else t - 1 - tt).astype(F32)
    scale = HD ** -0.5
    for h in range(NH):
        sl = slice(h * HD, (h + 1) * HD)
        lg = lg_ref[d, h]
        q = q_ref[:, sl]
        k = k_ref[:, sl]
        v = v_ref[:, sl]
        intra = jnp.where(rel >= 0, jnp.exp(lg * relf), 0.0)
        scores = (_dot_nt(q, k) * scale) * intra
        s_old = s_ref[d, h]
        out = _dot(scores.astype(BF16), v) + jnp.exp(lg * (pos + 1.0)) * _dot(q, s_old.astype(BF16))
        kd = (k.astype(F32) * (scale * jnp.exp(lg * (t - 1.0 - pos)))).astype(BF16)
        s_ref[d, h] = jnp.exp(lg * t) * s_old + _dot_tn(kd, v)
        o_ref[:, sl] = out


def _hgrn_direction(d, lb_ref, q_ref, z_ref, v_ref, o_ref, st_ref):
    t = T_HGRN
    fwd = d == 0
    lb = lb_ref[...]
    z = z_ref[...].astype(F32)
    sig = _sigmoid(z)
    f = lb + (1.0 - lb) * sig
    logf = jnp.log(jnp.maximum(f, F_TINY))
    kk = (1.0 - lb) * (1.0 - sig)
    qq = _silu(q_ref[...].astype(F32))
    r2 = lax.broadcasted_iota(jnp.int32, (t, t), 0)
    c2 = lax.broadcasted_iota(jnp.int32, (t, t), 1)
    tri = jnp.where((c2 <= r2) if fwd else (c2 >= r2), 1.0, 0.0).astype(BF16)
    cum = _cumsum_rows(tri, logf)
    xdiff = jnp.where((c2 < r2) if fwd else (c2 > r2), r2 ^ c2, 0)
    row = lax.broadcasted_iota(jnp.int32, (t, 1), 0)
    if fwd:
        e_ref = jnp.where(row == 0, 0.0, pltpu.roll(cum, 1, 0))
    else:
        e_ref = jnp.where(row == t - 1, 0.0, pltpu.roll(cum, t - 1, 0))
    f_ref = cum
    levels = []
    w = 1
    while w < t:
        if w == 1:
            q_l, k_l = (qq * jnp.maximum(f, F_TINY)).astype(BF16), kk.astype(BF16)
        else:
            q_l = (qq * jnp.exp(cum - e_ref)).astype(BF16)
            k_l = (kk * jnp.exp(f_ref - cum)).astype(BF16)
        levels.append((q_l, k_l, (xdiff >> (w.bit_length() - 1)) == 1))
        upper = (row & w) != 0
        if fwd:
            e_ref = jnp.where(upper, pltpu.roll(e_ref, w, 0), e_ref)
            f_ref = jnp.where(upper, f_ref, pltpu.roll(f_ref, t - w, 0))
        else:
            e_ref = jnp.where(upper, e_ref, pltpu.roll(e_ref, t - w, 0))
            f_ref = jnp.where(upper, pltpu.roll(f_ref, w, 0), f_ref)
        w *= 2
    q_b = qq.astype(BF16)
    k_b = kk.astype(BF16)
    tail = cum[t - 1:t] if fwd else cum[0:1]
    q_state = (qq * jnp.exp(cum)).astype(BF16)
    k_state = (kk * jnp.exp(tail - cum)).astype(BF16)
    decay = jnp.exp(tail)
    for h in range(NH):
        sl = slice(h * HD, (h + 1) * HD)
        v_h = v_ref[:, sl]
        scores = jnp.where(r2 == c2, _dot_nt(q_b[:, sl], k_b[:, sl]), 0.0)
        for q_l, k_l, m in levels:
            scores = jnp.where(m, _dot_nt(q_l[:, sl], k_l[:, sl]), scores)
        st = st_ref[d, h]
        o_ref[:, sl] = _dot(scores.astype(BF16), v_h) + _dot_nt(q_state[:, sl], st.astype(BF16))
        st_ref[d, h] = st * decay[:, sl] + _dot_tn(v_h, k_state[:, sl])


def _hgrn_kernel(lbf_ref, qf_ref, zf_ref, vf_ref, lbb_ref, qb_ref, zb_ref, vb_ref, of_ref, ob_ref, st_ref):
    _init_at_first_step(st_ref)
    _hgrn_direction(0, lbf_ref, qf_ref, zf_ref, vf_ref, of_ref, st_ref)
    _hgrn_direction(1, lbb_ref, qb_ref, zb_ref, vb_ref, ob_ref, st_ref)


def _log_sigmoid(x):
    return jnp.minimum(x, 0.0) - jnp.log(1.0 + jnp.exp(-jnp.abs(x)))


def _rep_lane(parts, lane, width):
    sel = jnp.where(lax.broadcasted_iota(jnp.int32, (GATE_PAD, width), 0) == lane, 1.0, 0.0).astype(BF16)
    hi, mid = parts
    return _dot(hi, sel) + _dot(mid, sel)


def _mlstm_direction(d, qk_ref, v_ref, g_ref, gb_ref, o_ref, c_ref, m_ref):
    t = T_ATTN
    assert t == 2 * HD
    fwd = d == 0
    g = g_ref[...] + gb_ref[...]
    r2 = lax.broadcasted_iota(jnp.int32, (t, t), 0)
    c2 = lax.broadcasted_iota(jnp.int32, (t, t), 1)
    mask = (c2 <= r2) if fwd else (c2 >= r2)
    tri = jnp.where(mask, 1.0, 0.0).astype(BF16)
    eye = jnp.where(r2 == c2, 1.0, 0.0).astype(BF16)
    lf = _log_sigmoid(g)
    cum_c = _cumsum_rows(tri, lf)
    a_col = g - pltpu.roll(cum_c, GATE_PAD - NH, 1)
    row = lax.broadcasted_iota(jnp.int32, (t, 1), 0)
    pm = a_col
    k = 1
    while k < t:
        if fwd:
            pm = jnp.maximum(pm, jnp.where(row >= k, pltpu.roll(pm, k, 0), NEG_BIG))
        else:
            pm = jnp.maximum(pm, jnp.where(row < t - k, pltpu.roll(pm, t - k, 0), NEG_BIG))
        k *= 2
    a_parts, pm_parts, cum_parts = _split2(a_col), _split2(pm), _split2(cum_c)
    a_rows = _dot_tn(a_parts[0], eye) + _dot_tn(a_parts[1], eye)
    scale = HD ** -0.5
    last = t - 1 if fwd else 0
    ones = jnp.ones((t, HD), BF16)
    for h in range(NH):
        sl = slice(h * HD, (h + 1) * HD)
        li, lfw = 2 * NH * d + h, 2 * NH * d + NH + h
        m_old = m_ref[d, h]
        mx2 = jnp.maximum(_rep_lane(pm_parts, li, 2 * HD), jnp.concatenate([m_old, m_old], axis=1))
        mx = mx2[:, 0:HD]
        cc = _rep_lane(cum_parts, lfw, HD)
        a_rep = _rep_lane(a_parts, li, HD)
        a_row = a_rows[li:li + 1]
        q = qk_ref[:, sl]
        k_h = qk_ref[:, BR + h * HD:BR + (h + 1) * HD]
        v_aug = jnp.concatenate([v_ref[:, sl], ones], axis=1)
        w = jnp.where(mask, jnp.exp(a_row - mx2), 0.0)
        scores = (_dot_nt(q, k_h) * scale) * w
        w_state = jnp.exp(m_old - mx)
        c_old = c_ref[d, h]
        s_hi = scores.astype(BF16)
        s_mid = (scores - s_hi.astype(F32)).astype(BF16)
        intra = _dot(s_hi, v_aug)
        inter = _dot(q, c_old.astype(BF16))
        num = intra[:, 0:HD] + w_state * inter[:, 0:HD]
        den = (intra[:, HD:2 * HD] + _dot(s_mid, ones)) + w_state * inter[:, HD:2 * HD]
        o_ref[:, sl] = num / jnp.maximum(jnp.abs(den), jnp.exp(-(cc + mx)))
        total = cc[last:last + 1]
        m_new_rel = mx[last:last + 1]
        keep = jnp.exp(m_old - m_new_rel)
        w_end = jnp.exp(a_rep - m_new_rel)
        kw = (k_h.astype(F32) * (scale * w_end)).astype(BF16)
        c_ref[d, h] = jnp.concatenate([keep, keep], axis=1) * c_old + _dot_tn(kw, v_aug)
        m_ref[d, h] = total + m_new_rel


def _ret_mlstm_kernel(lg_ref, rqf_ref, rkf_ref, rvf_ref, rqb_ref, rkb_ref, rvb_ref,
                      qkf_ref, vf_ref, gf_ref, qkb_ref, vb_ref, gb_ref, bias_ref,
                      rof_ref, rob_ref, mof_ref, mob_ref, s_ref, c_ref, m_ref):
    _init_at_first_step(s_ref, c_ref, m_ref)
    _ret_direction(0, lg_ref, rqf_ref, rkf_ref, rvf_ref, rof_ref, s_ref)
    _mlstm_direction(0, qkf_ref, vf_ref, gf_ref, bias_ref, mof_ref, c_ref, m_ref)
    _ret_direction(1, lg_ref, rqb_ref, rkb_ref, rvb_ref, rob_ref, s_ref)
    _mlstm_direction(1, qkb_ref, vb_ref, gb_ref, bias_ref, mob_ref, c_ref, m_ref)


def _scans(p, gates, qk, lg, lb, gate_b, layer, *, n_batch, seq, ctx_len):
    n = p.shape[0]
    state = pltpu.VMEM((2, NH, HD, HD), F32)
    vec = pltpu.VMEM((2, NH, 1, HD), F32)

    def call(kernel, t, inputs, make_specs, scratch, name, n_mixers=1):
        assert ctx_len % t == 0 and seq % t == 0, (ctx_len, seq, t)
        n_ctx, n_lat = ctx_len // t, seq // t

        def rows(d, width, cblock=0):
            return pl.BlockSpec((t, width), lambda b, i: (_chunk_block(b, d, i, n_ctx=n_ctx, n_lat=n_lat,
                                                                        n_batch=n_batch), cblock))

        out = jax.ShapeDtypeStruct((n, BR), F32)
        return pl.pallas_call(
            kernel,
            out_shape=(out, out) * n_mixers,
            grid=(n_batch, n_ctx + n_lat),
            in_specs=make_specs(rows),
            out_specs=(rows(0, BR), rows(1, BR)) * n_mixers,
            scratch_shapes=scratch,
            compiler_params=_cparams(("parallel", "arbitrary")),
            name=name,
        )(*inputs)

    lb_spec = lambda d: pl.BlockSpec((None, None, 1, BR), lambda b, i: (layer, d, 0, 0))
    raw_a = call(_hgrn_kernel, T_HGRN, (lb, p, p, p, lb, p, p, p),
                 lambda rows: [spec for d in (0, 1)
                               for spec in (lb_spec(d), rows(d, BR, COL_HQ), rows(d, BR, COL_HF + d),
                                            rows(d, BR, COL_HV))],
                 [state], "hgrn2_scan")
    raw_bd = call(_ret_mlstm_kernel, T_ATTN, (lg, p, p, p, p, p, p, qk, p, gates, qk, p, gates, gate_b),
                  lambda rows: [pl.BlockSpec(memory_space=pltpu.SMEM)]
                  + [rows(d, BR, c) for d in (0, 1) for c in (COL_RQ, COL_RK, COL_RV)]
                  + [spec for d in (0, 1)
                     for spec in (rows(d, 2 * BR), rows(d, BR, COL_MV), rows(d, GATE_PAD))]
                  + [pl.BlockSpec((None, 1, GATE_PAD), lambda b, i: (layer, 0, 0))],
                  [state, pltpu.VMEM((2, NH, HD, 2 * HD), F32), vec], "retention_mlstm_scan", n_mixers=2)
    return raw_a, raw_bd[0:2], raw_bd[2:4]


def _cmul(ar, ai, br, bi):
    return ar * br - ai * bi, ar * bi + ai * br


def _pow_table(lam_r, lam_i, expo):
    pr = jnp.ones(expo.shape, F32)
    pi = jnp.zeros(expo.shape, F32)
    ar = jnp.broadcast_to(lam_r, expo.shape)
    ai = jnp.broadcast_to(lam_i, expo.shape)
    for k in range(S5_T.bit_length() - 1):
        nr, ni = _cmul(pr, pi, ar, ai)
        bit = (expo & (1 << k)) != 0
        pr, pi = jnp.where(bit, nr, pr), jnp.where(bit, ni, pi)
        ar, ai = _cmul(ar, ai, ar, ai)
    return pr, pi


def _s5_operators(d, u, col_ref, row_ref, m_ref, w_ref, ys_ref, z_ref, wr_ref, wi_ref):
    t = S5_T
    fwd = d == 0
    col = col_ref[d]
    rowp = row_ref[d]
    lam_rc, lam_ic = col[:, 0:1], col[:, 1:2]
    bbr, bbi = rowp[0:S5_HG], rowp[S5_HG:2 * S5_HG]
    lam_rr, lam_ir = rowp[2 * S5_HG:2 * S5_HG + 1], rowp[2 * S5_HG + 1:2 * S5_HG + 2]
    lane = lax.broadcasted_iota(jnp.int32, (t, t), 1)
    sub = lax.broadcasted_iota(jnp.int32, (t, t), 0)

    pr, pi = _pow_table(lam_rc, lam_ic, lane if fwd else t - 1 - lane)
    qr, qi = _cmul(pr, pi, lam_rc, lam_ic)
    for h in range(S5_HG):
        cr, ci = col[:, 2 + h:3 + h], col[:, 2 + S5_HG + h:3 + S5_HG + h]
        hs = slice(h * t, (h + 1) * t)
        z_ref[d, 0:S5_PP, hs] = cr * pr - ci * pi
        z_ref[d, S5_PP:2 * S5_PP, hs] = -(cr * pi + ci * pr)
        ys_ref[d, 0:S5_PP, hs] = (cr * qr - ci * qi).astype(BF16)
        ys_ref[d, S5_PP:2 * S5_PP, hs] = (-(cr * qi + ci * qr)).astype(BF16)
    b_hi, b_mid, _ = _split3(jnp.concatenate([bbr, bbi], axis=1))
    z_hi, z_mid, _ = _split3(z_ref[d])
    krow = _dot(b_hi, z_hi) + (_dot(b_hi, z_mid) + _dot(b_mid, z_hi))
    shift, keep = (0, lane >= sub) if fwd else (1, lane <= sub)
    for hp in range(S5_HG):
        for h in range(S5_HG):
            tile = jnp.broadcast_to(krow[hp:hp + 1, h * t:(h + 1) * t], (t, t))
            tile = pltpu.roll(tile, shift, 1, stride=1, stride_axis=0)
            m_ref[d, hp * t:(hp + 1) * t, h * t:(h + 1) * t] = jnp.where(keep, tile, 0.0).astype(BF16)
    tr, ti = _pow_table(lam_rr, lam_ir, t - 1 - sub if fwd else sub)
    for hp in range(S5_HG):
        br, bi = bbr[hp:hp + 1], bbi[hp:hp + 1]
        w_ref[d, hp * t:(hp + 1) * t, 0:S5_PP] = (tr * br - ti * bi).astype(BF16)
        w_ref[d, hp * t:(hp + 1) * t, S5_PP:2 * S5_PP] = (tr * bi + ti * br).astype(BF16)
    wv = _dot(u, w_ref[d])
    wr_ref[d] = wv[:, 0:S5_PP]
    wi_ref[d] = wv[:, S5_PP:2 * S5_PP]
    lr, li = lam_rr, lam_ir
    for _ in range(t.bit_length() - 1):
        lr, li = _cmul(lr, li, lr, li)
    return lr, li


def _s5_kernel(u_ref, col_ref, row_ref, o_ref, m_ref, w_ref, ys_ref, z_ref, wr_ref, wi_ref, xr_ref, xi_ref, *,
               n_batch, n_ctx, n_lat):
    t = S5_T
    u = jnp.concatenate([u_ref[hp] for hp in range(S5_HG)], axis=1)
    lam_t = [_s5_operators(d, u, col_ref, row_ref, m_ref, w_ref, ys_ref, z_ref, wr_ref, wi_ref) for d in (0, 1)]

    def make_step(base, stride, count):
        def step(i, carry):
            out = []
            for d in (0, 1):
                xr, xi = carry[2 * d], carry[2 * d + 1]
                j = i if d == 0 else count - 1 - i
                idx = pl.ds(base + j, n_batch, stride=stride)
                xr_ref[d, idx, :] = xr
                xi_ref[d, idx, :] = xi
                nr, ni = _cmul(xr, xi, *lam_t[d])
                out += [nr + wr_ref[d, idx, :], ni + wi_ref[d, idx, :]]
            return tuple(out)
        return step

    zero = jnp.zeros((n_batch, S5_PP), F32)
    carry = lax.fori_loop(0, n_ctx, make_step(n_batch * n_lat, n_ctx, n_ctx), (zero,) * 4)
    lax.fori_loop(0, n_lat, make_step(0, n_lat, n_lat), carry)
    y = None
    for d in (0, 1):
        x_prev = jnp.concatenate([xr_ref[d], xi_ref[d]], axis=1).astype(BF16)
        y_d = _dot(u, m_ref[d]) + _dot(x_prev, ys_ref[d])
        y = y_d if y is None else y + y_d
    for h in range(S5_HG):
        o_ref[h] = y[:, h * t:(h + 1) * t]


def _s5_params(a_re, a_im, log_dt, b_re, b_im, c_re, c_im):
    a_re, a_im = a_re.astype(F32), a_im.astype(F32)
    dt = jnp.exp(log_dt.astype(F32))[..., None]
    mag = jnp.exp(a_re * dt)
    lam_re, lam_im = mag * jnp.cos(a_im * dt), mag * jnp.sin(a_im * dt)
    den = a_re * a_re + a_im * a_im
    num_re, num_im = lam_re - 1.0, lam_im
    fr = (num_re * a_re + num_im * a_im) / den
    fi = (num_im * a_re - num_re * a_im) / den
    b_re, b_im = b_re.astype(F32), b_im.astype(F32)
    bb_re = fr[..., None] * b_re - fi[..., None] * b_im
    bb_im = fr[..., None] * b_im + fi[..., None] * b_re
    c_re_t = jnp.swapaxes(c_re.astype(F32), -1, -2)
    c_im_t = jnp.swapaxes(c_im.astype(F32), -1, -2)
    col = jnp.concatenate([lam_re[..., None], lam_im[..., None], c_re_t, c_im_t], axis=-1)
    col = jnp.pad(col, ((0, 0),) * 3 + ((0, S5_PP - S5_P), (0, 128 - col.shape[-1])))
    row = jnp.concatenate([jnp.swapaxes(bb_re, -1, -2), jnp.swapaxes(bb_im, -1, -2),
                           lam_re[..., None, :], lam_im[..., None, :]], axis=-2)
    row = jnp.pad(row, ((0, 0),) * 3 + ((0, S5_ROWS - row.shape[-2]), (0, S5_PP - S5_P)))
    return col, row


def _s5_scan(ut, col, row, layer, *, n_batch, seq, ctx_len):
    t = S5_T
    n = ut.shape[1]
    nc = n // t
    assert ctx_len % t == 0 and seq % t == 0, (ctx_len, seq, t)
    n_ctx, n_lat = ctx_len // t, seq // t
    ut = ut.reshape(S5_G, S5_HG, nc, t)
    kern = functools.partial(_s5_kernel, n_batch=n_batch, n_ctx=n_ctx, n_lat=n_lat)
    yt = pl.pallas_call(
        kern,
        out_shape=jax.ShapeDtypeStruct((S5_G, S5_HG, nc, t), F32),
        grid=(S5_G,),
        in_specs=[pl.BlockSpec((None, S5_HG, nc, t), lambda g: (g, 0, 0, 0)),
                  pl.BlockSpec((None, 2, None, S5_PP, 128), lambda g: (layer, 0, g, 0, 0)),
                  pl.BlockSpec((None, 2, None, S5_ROWS, S5_PP), lambda g: (layer, 0, g, 0, 0))],
        out_specs=pl.BlockSpec((None, S5_HG, nc, t), lambda g: (g, 0, 0, 0)),
        scratch_shapes=[pltpu.VMEM((2, S5_HG * t, S5_HG * t), BF16),
                        pltpu.VMEM((2, S5_HG * t, 2 * S5_PP), BF16),
                        pltpu.VMEM((2, 2 * S5_PP, S5_HG * t), BF16),
                        pltpu.VMEM((2, 2 * S5_PP, S5_HG * t), F32),
                        pltpu.VMEM((2, nc, S5_PP), F32), pltpu.VMEM((2, nc, S5_PP), F32),
                        pltpu.VMEM((2, nc, S5_PP), F32), pltpu.VMEM((2, nc, S5_PP), F32)],
        compiler_params=_cparams(("parallel",)),
        name="s5_scan",
    )(ut, col, row)
    return yt.reshape(BR, n)


def _head_rms(x, g):
    parts = []
    for h in range(NH):
        xh = x[:, h * HD:(h + 1) * HD]
        parts.append(xh * lax.rsqrt(jnp.mean(jnp.square(xh), axis=-1, keepdims=True) + NORM_EPS))
    return jnp.concatenate(parts, axis=-1) * g


def _finish_kernel(raf_ref, rab_ref, rbf_ref, rbb_ref, rc_ref, rdf_ref, rdb_ref, ga_ref, gb_ref, u_ref, gd_ref,
                   mg_ref, x_ref, mod_ref, vec_ref, glu_w_ref, wb_ref, wo_ref, o_ref):
    vec = vec_ref[...]
    oa = _head_rms(raf_ref[...] + rab_ref[...], vec[0:1]) * _silu(ga_ref[...].astype(F32))
    ob = _head_rms(rbf_ref[...] + rbb_ref[...], vec[1:2]) * _silu(gb_ref[...].astype(F32))
    od = _head_rms(rdf_ref[...] + rdb_ref[...], vec[2:3]) * _silu(gd_ref[...].astype(F32))
    yc = rc_ref[...].T + vec[3:4] * u_ref[...].astype(F32)
    yc = 0.5 * yc * (1.0 + lax.erf(yc * (2.0 ** -0.5)))
    oc = yc * _sigmoid(_dot(yc.astype(BF16), glu_w_ref[...]) + vec[4:5])
    y2 = None
    for j, o in enumerate((oa, ob, oc, od)):
        gate2 = jnp.tanh(0.5 * mg_ref[:, j * D_MODEL:(j + 1) * D_MODEL].astype(F32)) + 1.0
        term = gate2 * _dot(o.astype(BF16), wb_ref[j])
        y2 = term if y2 is None else y2 + term
    mix2 = _dot(y2.astype(BF16), wo_ref[...])
    o_ref[...] = x_ref[...] + (0.5 * mod_ref[2:3, :]) * mix2


def _finish(raw_a, raw_b, raw_c, raw_d, p, x_all, mod, vec, glu_w, w_branch, w_out, layer, *, n_rows, n_batch,
            seq, tm=256):
    mrow = _mod_row_map(n_rows, tm, n_batch, seq)
    one = pl.Buffered(1)

    def pcol(cblock):
        return pl.BlockSpec((tm, BR), lambda i: (i, cblock))

    raw = pl.BlockSpec((tm, BR), lambda i: (i, 0))
    return pl.pallas_call(
        _finish_kernel,
        out_shape=jax.ShapeDtypeStruct((n_rows, D_MODEL), F32),
        grid=(n_rows // tm,),
        in_specs=[raw, raw, raw, raw, pl.BlockSpec((BR, tm), lambda i: (0, i)), raw, raw,
                  pcol(COL_HG), pcol(COL_RG), pcol(COL_SU), pcol(COL_MZ),
                  pl.BlockSpec((tm, N_BRANCH * D_MODEL), lambda i: (i, COL_MERGE)),
                  pl.BlockSpec((tm, D_MODEL), lambda i: (i, 0)),
                  pl.BlockSpec((None, 8, D_MODEL), lambda i: (mrow(i), 0, 0)),
                  pl.BlockSpec((None, 8, BR), lambda i: (layer, 0, 0), pipeline_mode=one),
                  pl.BlockSpec((None, BR, BR), lambda i: (layer, 0, 0), pipeline_mode=one),
                  pl.BlockSpec((None, N_BRANCH, BR, D_MODEL), lambda i: (layer, 0, 0, 0), pipeline_mode=one),
                  pl.BlockSpec((None, D_MODEL, D_MODEL), lambda i: (layer, 0, 0), pipeline_mode=one)],
        out_specs=pl.BlockSpec((tm, D_MODEL), lambda i: (i, 0)),
        compiler_params=_cparams(("parallel",)),
        name="finish",
    )(*raw_a, *raw_b, raw_c, *raw_d, p, p, p, p, p, x_all, mod, vec, glu_w, w_branch, w_out)


def _mlp_kernel(x_ref, mod_ref, g_ref, w1_ref, w2_ref, fin_ref, o_ref, xn_ref, acc_ref, *, final_norm):
    j = pl.program_id(1)

    @pl.when(j == 0)
    def _():
        h = _norm_mod(x_ref[...], g_ref[...], mod_ref[3:4, :], mod_ref[4:5, :])
        xn_ref[...] = h.astype(BF16)
        acc_ref[...] = jnp.zeros_like(acc_ref)

    a = jnp.square(jnp.maximum(_dot(xn_ref[...], w1_ref[...]), 0.0))
    acc_ref[...] += _dot(a.astype(BF16), w2_ref[...])

    @pl.when(j == pl.num_programs(1) - 1)
    def _():
        y = x_ref[...] + mod_ref[5:6, :] * acc_ref[...]
        if final_norm:
            y = (y * lax.rsqrt(jnp.mean(jnp.square(y), axis=-1, keepdims=True) + NORM_EPS)) * fin_ref[...]
        o_ref[...] = y


def _mlp(x_all, mod, norm_w, w1, w2, fin_w, layer, *, n_rows, n_batch, seq, final_norm, tm=512, tf=1024):
    mrow = _mod_row_map(n_rows, tm, n_batch, seq)
    assert D_FF % tf == 0, tf
    kern = functools.partial(_mlp_kernel, final_norm=final_norm)
    return pl.pallas_call(
        kern,
        out_shape=jax.ShapeDtypeStruct((n_rows, D_MODEL), F32),
        grid=(n_rows // tm, D_FF // tf),
        in_specs=[pl.BlockSpec((tm, D_MODEL), lambda i, j: (i, 0)),
                  pl.BlockSpec((None, 8, D_MODEL), lambda i, j: (mrow(i), 0, 0)),
                  pl.BlockSpec((None, 1, D_MODEL), lambda i, j: (layer, 0, 0)),
                  pl.BlockSpec((None, D_MODEL, tf), lambda i, j: (layer, 0, j)),
                  pl.BlockSpec((None, tf, D_MODEL), lambda i, j: (layer, j, 0)),
                  pl.BlockSpec((1, D_MODEL), lambda i, j: (0, 0))],
        out_specs=pl.BlockSpec((tm, D_MODEL), lambda i, j: (i, 0)),
        scratch_shapes=[pltpu.VMEM((tm, D_MODEL), BF16), pltpu.VMEM((tm, D_MODEL), F32)],
        compiler_params=_cparams(("parallel", "arbitrary")),
        name="mlp",
    )(x_all, mod, norm_w, w1, w2, fin_w.reshape(1, D_MODEL))


W_PREP_ROWS = 1024
N_MERGE_BLK = N_BRANCH * D_MODEL // W_PREP_ROWS
MERGE_OFF = GATE_OFF + 4 * NH


def _w_prep_kernel(a_ref, g_ref, main_ref, gate_ref):
    @pl.when(pl.program_id(1) == 0)
    def _():
        gate_ref[...] = jnp.zeros_like(gate_ref)
        gate_ref[0:4 * NH, :] = g_ref[0].astype(BF16)

    main_ref[...] = a_ref[0].astype(BF16)


def _split_w_in(w):
    depth = w.shape[0]
    wt = jnp.swapaxes(w, 1, 2)
    tr = W_PREP_ROWS

    def a_map(l, j):
        row = jnp.where(j < N_MERGE_BLK, MERGE_OFF + j * tr, (j - N_MERGE_BLK) * tr)
        return l, pl.multiple_of(row, 16), 0

    return pl.pallas_call(
        _w_prep_kernel,
        out_shape=(jax.ShapeDtypeStruct((depth, P_WIDTH, D_MODEL), BF16),
                   jax.ShapeDtypeStruct((depth, GATE_PAD, D_MODEL), BF16)),
        grid=(depth, P_WIDTH // tr),
        in_specs=[pl.BlockSpec((pl.Element(1), pl.Element(tr), pl.Element(D_MODEL)), a_map),
                  pl.BlockSpec((pl.Element(1), pl.Element(4 * NH), pl.Element(D_MODEL)),
                               lambda l, j: (l, GATE_OFF, 0))],
        out_specs=(pl.BlockSpec((None, tr, D_MODEL), lambda l, j: (l, j, 0)),
                   pl.BlockSpec((None, GATE_PAD, D_MODEL), lambda l, j: (l, 0, 0))),
        compiler_params=_cparams(("parallel", "arbitrary")),
        name="w_in_prep",
    )(wt, wt)


def kernel(x, c, ctx, c_ctx, w_mod, b_mod, norm_mix, norm_mlp, w_in, hgrn_lb_logits, hgrn_norm, ret_decay, ret_norm, s5_a_re, s5_a_im, s5_log_dt, s5_b_re, s5_b_im, s5_c_re, s5_c_im, s5_d, s5_glu_w, s5_glu_b, mlstm_conv_w, mlstm_conv_b, mlstm_gate_b, mlstm_norm, w_branch, w_out, w_ff1, w_ff2, final_norm):
    n_batch, seq, _ = x.shape
    ctx_len = ctx.shape[1]
    depth = w_in.shape[0]
    nl = n_batch * seq
    dims = dict(n_batch=n_batch, seq=seq)

    p_lb = jax.nn.softmax(hgrn_lb_logits.astype(F32), axis=0)
    lower_bounds = (jnp.cumsum(p_lb, axis=0) - p_lb[0]).reshape(depth, 2, 1, BR)
    lg = jnp.log1p(-jnp.exp(ret_decay.astype(F32)))
    gate_b = jnp.pad(mlstm_gate_b.reshape(depth, 1, 4 * NH).astype(F32), ((0, 0), (0, 0), (0, GATE_PAD - 4 * NH)))
    vec = jnp.stack([hgrn_norm, ret_norm, mlstm_norm, s5_d, s5_glu_b], axis=1).astype(F32)
    vec = jnp.pad(vec, ((0, 0), (0, 8 - vec.shape[1]), (0, 0)))
    w_main, w_gate = _split_w_in(w_in)
    glu_w, wb, wo = s5_glu_w.astype(BF16), w_branch.astype(BF16), w_out.astype(BF16)
    w1, w2 = w_ff1.astype(BF16), w_ff2.astype(BF16)
    norm_mix3 = norm_mix.reshape(depth, 1, D_MODEL)
    norm_mlp3 = norm_mlp.reshape(depth, 1, D_MODEL)
    conv_w = mlstm_conv_w.reshape(depth, 9, 2 * BR)
    conv_b = mlstm_conv_b.reshape(depth, 1, 2 * BR)
    s5_col, s5_row = _s5_params(s5_a_re, s5_a_im, s5_log_dt, s5_b_re, s5_b_im, s5_c_re, s5_c_im)

    c_pad = jnp.zeros((8, D_MODEL), F32).at[:n_batch].set(c).at[n_batch].set(c_ctx)
    x_all = jnp.concatenate([x.reshape(nl, D_MODEL), ctx.reshape(n_batch * ctx_len, D_MODEL)], axis=0)

    for l in range(depth):
        last = l == depth - 1
        mod = _modulation(c_pad, w_mod, b_mod, l)
        mod = jnp.pad(mod[:n_batch + 1].reshape(n_batch + 1, N_MOD, D_MODEL), ((0, 0), (0, 8 - N_MOD), (0, 0)))
        p, gates, ut = _in_proj(x_all, mod, norm_mix3, w_main, w_gate, l, **dims)
        qk = _mlstm_conv(p, conv_w, conv_b, l, ctx_len=ctx_len, **dims)
        raw_a, raw_b, raw_d = _scans(p, gates, qk, lg[l], lower_bounds, gate_b, l, ctx_len=ctx_len, **dims)
        raw_c = _s5_scan(ut, s5_col, s5_row, l, ctx_len=ctx_len, **dims)
        n_rows = nl if last else x_all.shape[0]
        x_mid = _finish(raw_a, raw_b, raw_c, raw_d, p, x_all, mod, vec, glu_w, wb, wo, l, n_rows=n_rows, **dims)
        x_all = _mlp(x_mid, mod, norm_mlp3, w1, w2, final_norm, l, n_rows=n_rows, final_norm=last, **dims)
    return x_all.reshape(n_batch, seq, D_MODEL)
```

```python
import functools
import math

import jax
import jax.numpy as jnp
from jax import lax
from jax.experimental import pallas as pl
from jax.experimental.pallas import tpu as pltpu

F32 = jnp.float32
BF16 = jnp.bfloat16

D_MODEL = 2048
N_BRANCH = 4
BR = D_MODEL // N_BRANCH
HD = 128
NH = BR // HD
S5_HG = 16
S5_G = BR // S5_HG
S5_P = 64
D_FF = 4 * D_MODEL
N_MOD = 6
GRID_W = 64
NORM_EPS = 1e-6
NEG_BIG = -1e30
F_TINY = 1e-30
S5_DT_MIN = 1e-3

COL_MERGE = 0
COL_HQ, COL_HF, COL_HV, COL_HG = 16, 17, 19, 20
COL_RQ, COL_RK, COL_RV, COL_RG = 21, 22, 23, 24
COL_SU = 25
COL_MQK, COL_MV, COL_MZ = 26, 28, 29
P_WIDTH = 30 * BR
GATE_OFF = 14 * BR
GATE_PAD = 128

T_HGRN = 128
T_ATTN = 256
S5_T = 128
S5_PP = 128
S5_ROWS = 40
VMEM_LIMIT = 56 * 1024 * 1024

_HI = lax.Precision.HIGHEST


def _cparams(sem):
    return pltpu.CompilerParams(dimension_semantics=sem, vmem_limit_bytes=VMEM_LIMIT)


def _dot(a, b):
    return jnp.dot(a, b, preferred_element_type=F32)


def _dot_nt(a, b):
    return lax.dot_general(a, b, (((1,), (1,)), ((), ())), preferred_element_type=F32)


def _dot_tn(a, b):
    return lax.dot_general(a, b, (((0,), (0,)), ((), ())), preferred_element_type=F32)


def _split3(x):
    hi = x.astype(BF16)
    r1 = x - hi.astype(F32)
    mid = r1.astype(BF16)
    return hi, mid, (r1 - mid.astype(F32)).astype(BF16)


def _split2(x):
    hi = x.astype(BF16)
    return hi, (x - hi.astype(F32)).astype(BF16)


def _cumsum_rows(tri, x):
    hi, mid = _split2(x)
    return _dot(tri, hi) + _dot(tri, mid)


def _sigmoid(x):
    return 1.0 / (1.0 + jnp.exp(-x))


def _silu(x):
    return x * _sigmoid(x)


def _mod_kernel(c_ref, w_ref, b_ref, o_ref):
    @pl.when(pl.program_id(1) == 0)
    def _():
        o_ref[...] = jnp.broadcast_to(b_ref[...], o_ref.shape)

    a = _silu(c_ref[...]).astype(BF16)
    o_ref[...] += _dot(a, w_ref[...].astype(BF16))


def _modulation(c_pad, w_mod, b_mod, tk=256):
    depth, _, n = w_mod.shape
    return pl.pallas_call(
        _mod_kernel,
        out_shape=jax.ShapeDtypeStruct((depth, 8, n), F32),
        grid=(depth, D_MODEL // tk),
        in_specs=[pl.BlockSpec((8, tk), lambda l, k: (0, k)),
                  pl.BlockSpec((None, tk, n), lambda l, k: (l, k, 0)),
                  pl.BlockSpec((None, 1, n), lambda l, k: (l, 0, 0))],
        out_specs=pl.BlockSpec((None, 8, n), lambda l, k: (l, 0, 0)),
        compiler_params=_cparams(("parallel", "arbitrary")),
        name="modulation",
    )(c_pad, w_mod, b_mod.reshape(depth, 1, n))


def _norm_mod(x, g, shift, scale):
    y = x * lax.rsqrt(jnp.mean(jnp.square(x), axis=-1, keepdims=True) + NORM_EPS)
    return (y * g) * (1.0 + scale) + shift


def _inproj_kernel(x_ref, mod_ref, g_ref, w_ref, wg_ref, p_ref, gate_ref, ut_ref, xn_ref, *, su_block, su_off):
    @pl.when(pl.program_id(1) == 0)
    def _():
        h = _norm_mod(x_ref[...], g_ref[...], mod_ref[0:1, :], mod_ref[1:2, :])
        xn_ref[...] = h.astype(BF16)
        gate_ref[...] = _dot_nt(xn_ref[...], wg_ref[...])

    acc = _dot_nt(xn_ref[...], w_ref[...])
    p_ref[...] = acc.astype(p_ref.dtype)

    @pl.when(pl.program_id(1) == su_block)
    def _():
        ut_ref[...] = acc[:, su_off:su_off + BR].T.astype(ut_ref.dtype)


def _mod_row_map(n_rows, tm, n_batch, seq):
    assert n_rows % tm == 0 and seq % tm == 0, (n_rows, seq, tm)
    n_lat_tiles, tiles_per_batch = n_batch * seq // tm, seq // tm

    def row(i):
        return jnp.where(i < n_lat_tiles, i // tiles_per_batch, n_batch)
    return row


def _in_proj(x_all, mod, norm_w, w_main, w_gate, layer, *, n_batch, seq, tm=1024, tn=1536):
    n = x_all.shape[0]
    assert P_WIDTH % tn == 0, tn
    su_block, su_off = divmod(COL_SU * BR, tn)
    assert su_off + BR <= tn, (su_off, tn)
    mrow = _mod_row_map(n, tm, n_batch, seq)
    return pl.pallas_call(
        functools.partial(_inproj_kernel, su_block=su_block, su_off=su_off),
        out_shape=(jax.ShapeDtypeStruct((n, P_WIDTH), BF16),
                   jax.ShapeDtypeStruct((n, GATE_PAD), F32),
                   jax.ShapeDtypeStruct((BR, n), BF16)),
        grid=(n // tm, P_WIDTH // tn),
        in_specs=[pl.BlockSpec((tm, D_MODEL), lambda i, j: (i, 0)),
                  pl.BlockSpec((None, 8, D_MODEL), lambda i, j: (mrow(i), 0, 0)),
                  pl.BlockSpec((None, 1, D_MODEL), lambda i, j: (layer, 0, 0)),
                  pl.BlockSpec((None, tn, D_MODEL), lambda i, j: (layer, j, 0)),
                  pl.BlockSpec((None, GATE_PAD, D_MODEL), lambda i, j: (layer, 0, 0))],
        out_specs=(pl.BlockSpec((tm, tn), lambda i, j: (i, j)),
                   pl.BlockSpec((tm, GATE_PAD), lambda i, j: (i, 0)),
                   pl.BlockSpec((BR, tm), lambda i, j: (0, i))),
        scratch_shapes=[pltpu.VMEM((tm, D_MODEL), BF16)],
        compiler_params=_cparams(("parallel", "arbitrary")),
        name="in_proj",
    )(x_all, mod, norm_w, w_main, w_gate)


CONV_ROWS = 512


def _conv_kernel(up_ref, x_ref, dn_ref, w_ref, b_ref, o_ref, *, n_lat_blocks, blocks_per_img, ctx_len):
    rb = pl.program_id(0)
    x = x_ref[...].astype(F32)
    w = w_ref[...]
    n = CONV_ROWS

    @pl.when(rb < n_lat_blocks)
    def _():
        rr = rb % blocks_per_img
        up = jnp.where(rr > 0, up_ref[...].astype(F32), 0.0)
        dn = jnp.where(rr < blocks_per_img - 1, dn_ref[...].astype(F32), 0.0)
        ext = jnp.concatenate([up, x, dn], axis=0)
        ne = n + 2 * GRID_W
        col = lax.broadcasted_iota(jnp.int32, (ne, 1), 0) % GRID_W
        xl = jnp.where(col >= 1, pltpu.roll(ext, 1, 0), 0.0)
        xr = jnp.where(col <= GRID_W - 2, pltpu.roll(ext, ne - 1, 0), 0.0)
        acc = None
        for i in range(3):
            y = w[3 * i:3 * i + 1] * xl + w[3 * i + 1:3 * i + 2] * ext + w[3 * i + 2:3 * i + 3] * xr
            part = y[i * GRID_W:i * GRID_W + n]
            acc = part if acc is None else acc + part
        o_ref[...] = _silu(acc + b_ref[...]).astype(o_ref.dtype)

    @pl.when(rb >= n_lat_blocks)
    def _():
        t = lax.broadcasted_iota(jnp.int32, (n, 1), 0) % ctx_len
        xl = jnp.where(t >= 1, pltpu.roll(x, 1, 0), 0.0)
        xr = jnp.where(t <= ctx_len - 2, pltpu.roll(x, n - 1, 0), 0.0)
        acc = w[3:4] * xl + w[4:5] * x + w[5:6] * xr
        o_ref[...] = _silu(acc + b_ref[...]).astype(o_ref.dtype)


def _mlstm_conv(p, conv_w, conv_b, layer, *, n_batch, seq, ctx_len, cb=2 * BR):
    assert CONV_ROWS % ctx_len == 0 and seq % CONV_ROWS == 0 and (n_batch * ctx_len) % CONV_ROWS == 0
    n = p.shape[0]
    n_blocks = n // CONV_ROWS
    hb = CONV_ROWS // GRID_W
    n_halo = n // GRID_W
    c0 = COL_MQK * BR // cb
    kern = functools.partial(_conv_kernel, n_lat_blocks=n_batch * seq // CONV_ROWS,
                             blocks_per_img=seq // CONV_ROWS, ctx_len=ctx_len)
    return pl.pallas_call(
        kern,
        out_shape=jax.ShapeDtypeStruct((n, 2 * BR), BF16),
        grid=(n_blocks, 2 * BR // cb),
        in_specs=[pl.BlockSpec((GRID_W, cb), lambda r, c: (jnp.maximum(r * hb - 1, 0), c0 + c)),
                  pl.BlockSpec((CONV_ROWS, cb), lambda r, c: (r, c0 + c)),
                  pl.BlockSpec((GRID_W, cb), lambda r, c: (jnp.minimum((r + 1) * hb, n_halo - 1), c0 + c)),
                  pl.BlockSpec((None, 9, cb), lambda r, c: (layer, 0, c)),
                  pl.BlockSpec((None, 1, cb), lambda r, c: (layer, 0, c))],
        out_specs=pl.BlockSpec((CONV_ROWS, cb), lambda r, c: (r, c)),
        compiler_params=_cparams(("parallel", "parallel")),
        name="mlstm_conv",
    )(p, p, p, conv_w, conv_b)


def _chunk_block(b, d, i, *, n_ctx, n_lat, n_batch):
    ctx_j = i if d == 0 else n_ctx - 1 - i
    lat_j = i - n_ctx if d == 0 else n_ctx + n_lat - 1 - i
    return jnp.where(i < n_ctx, n_batch * n_lat + b * n_ctx + ctx_j, b * n_lat + lat_j)


def _order_mask(d, t):
    r = lax.broadcasted_iota(jnp.int32, (t, t), 0)
    c = lax.broadcasted_iota(jnp.int32, (t, t), 1)
    return (c <= r) if d == 0 else (c >= r)


def _init_at_first_step(*refs):
    @pl.when(pl.program_id(1) == 0)
    def _():
        for ref in refs:
            ref[...] = jnp.zeros_like(ref)


def _ret_direction(d, lg_ref, q_ref, k_ref, v_ref, o_ref, s_ref):
    t = T_ATTN
    r = lax.broadcasted_iota(jnp.int32, (t, t), 0)
    c = lax.broadcasted_iota(jnp.int32, (t, t), 1)
    rel = (r - c) if d == 0 else (c - r)
    relf = jnp.maximum(rel, 0).astype(F32)
    tt = lax.broadcasted_iota(jnp.int32, (t, 1), 0)
    pos = (tt if d == 0 else t - 1 - tt).astype(F32)
    scale = HD ** -0.5
    for h in range(NH):
        sl = slice(h * HD, (h + 1) * HD)
        lg = lg_ref[d, h]
        q = q_ref[:, sl]
        k = k_ref[:, sl]
        v = v_ref[:, sl]
        intra = jnp.where(rel >= 0, jnp.exp(lg * relf), 0.0)
        scores = (_dot_nt(q, k) * scale) * intra
        s_old = s_ref[d, h]
        out = _dot(scores.astype(BF16), v) + jnp.exp(lg * (pos + 1.0)) * _dot(q, s_old.astype(BF16))
        kd = (k.astype(F32) * (scale * jnp.exp(lg * (t - 1.0 - pos)))).astype(BF16)
        s_ref[d, h] = jnp.exp(lg * t) * s_old + _dot_tn(kd, v)
        o_ref[:, sl] = out


def _hgrn_direction(d, lb_ref, q_ref, z_ref, v_ref, o_ref, st_ref):
    t = T_HGRN
    fwd = d == 0
    lb = lb_ref[...]
    z = z_ref[...].astype(F32)
    sig = _sigmoid(z)
    f = lb + (1.0 - lb) * sig
    logf = jnp.log(jnp.maximum(f, F_TINY))
    kk = (1.0 - lb) * (1.0 - sig)
    qq = _silu(q_ref[...].astype(F32))
    r2 = lax.broadcasted_iota(jnp.int32, (t, t), 0)
    c2 = lax.broadcasted_iota(jnp.int32, (t, t), 1)
    tri = jnp.where((c2 <= r2) if fwd else (c2 >= r2), 1.0, 0.0).astype(BF16)
    cum = _cumsum_rows(tri, logf)
    xdiff = jnp.where((c2 < r2) if fwd else (c2 > r2), r2 ^ c2, 0)
    row = lax.broadcasted_iota(jnp.int32, (t, 1), 0)
    if fwd:
        e_ref = jnp.where(row == 0, 0.0, pltpu.roll(cum, 1, 0))
    else:
        e_ref = jnp.where(row == t - 1, 0.0, pltpu.roll(cum, t - 1, 0))
    f_ref = cum
    levels = []
    w = 1
    while w < t:
        if w == 1:
            q_l, k_l = (qq * jnp.maximum(f, F_TINY)).astype(BF16), kk.astype(BF16)
        else:
            q_l = (qq * jnp.exp(cum - e_ref)).astype(BF16)
            k_l = (kk * jnp.exp(f_ref - cum)).astype(BF16)
        levels.append((q_l, k_l, (xdiff >> (w.bit_length() - 1)) == 1))
        upper = (row & w) != 0
        if fwd:
            e_ref = jnp.where(upper, pltpu.roll(e_ref, w, 0), e_ref)
            f_ref = jnp.where(upper, f_ref, pltpu.roll(f_ref, t - w, 0))
        else:
            e_ref = jnp.where(upper, e_ref, pltpu.roll(e_ref, t - w, 0))
            f_ref = jnp.where(upper, pltpu.roll(f_ref, w, 0), f_ref)
        w *= 2
    q_b = qq.astype(BF16)
    k_b = kk.astype(BF16)
    tail = cum[t - 1:t] if fwd else cum[0:1]
    q_state = (qq * jnp.exp(cum)).astype(BF16)
    k_state = (kk * jnp.exp(tail - cum)).astype(BF16)
    decay = jnp.exp(tail)
    for h in range(NH):
        sl = slice(h * HD, (h + 1) * HD)
        v_h = v_ref[:, sl]
        scores = jnp.where(r2 == c2, _dot_nt(q_b[:, sl], k_b[:, sl]), 0.0)
        for q_l, k_l, m in levels:
            scores = jnp.where(m, _dot_nt(q_l[:, sl], k_l[:, sl]), scores)
        st = st_ref[d, h]
        o_ref[:, sl] = _dot(scores.astype(BF16), v_h) + _dot_nt(q_state[:, sl], st.astype(BF16))
        st_ref[d, h] = st * decay[:, sl] + _dot_tn(v_h, k_state[:, sl])


def _hgrn_kernel(lbf_ref, qf_ref, zf_ref, vf_ref, lbb_ref, qb_ref, zb_ref, vb_ref, of_ref, ob_ref, st_ref):
    _init_at_first_step(st_ref)
    _hgrn_direction(0, lbf_ref, qf_ref, zf_ref, vf_ref, of_ref, st_ref)
    _hgrn_direction(1, lbb_ref, qb_ref, zb_ref, vb_ref, ob_ref, st_ref)


def _log_sigmoid(x):
    return jnp.minimum(x, 0.0) - jnp.log(1.0 + jnp.exp(-jnp.abs(x)))


def _rep_lane(parts, lane, width):
    sel = jnp.where(lax.broadcasted_iota(jnp.int32, (GATE_PAD, width), 0) == lane, 1.0, 0.0).astype(BF16)
    hi, mid = parts
    return _dot(hi, sel) + _dot(mid, sel)


def _mlstm_direction(d, qk_ref, v_ref, g_ref, gb_ref, o_ref, c_ref, m_ref):
    t = T_ATTN
    assert t == 2 * HD
    fwd = d == 0
    g = g_ref[...] + gb_ref[...]
    r2 = lax.broadcasted_iota(jnp.int32, (t, t), 0)
    c2 = lax.broadcasted_iota(jnp.int32, (t, t), 1)
    mask = (c2 <= r2) if fwd else (c2 >= r2)
    tri = jnp.where(mask, 1.0, 0.0).astype(BF16)
    eye = jnp.where(r2 == c2, 1.0, 0.0).astype(BF16)
    lf = _log_sigmoid(g)
    cum_c = _cumsum_rows(tri, lf)
    a_col = g - pltpu.roll(cum_c, GATE_PAD - NH, 1)
    row = lax.broadcasted_iota(jnp.int32, (t, 1), 0)
    pm = a_col
    k = 1
    while k < t:
        if fwd:
            pm = jnp.maximum(pm, jnp.where(row >= k, pltpu.roll(pm, k, 0), NEG_BIG))
        else:
            pm = jnp.maximum(pm, jnp.where(row < t - k, pltpu.roll(pm, t - k, 0), NEG_BIG))
        k *= 2
    a_parts, pm_parts, cum_parts = _split2(a_col), _split2(pm), _split2(cum_c)
    a_rows = _dot_tn(a_parts[0], eye) + _dot_tn(a_parts[1], eye)
    scale = HD ** -0.5
    last = t - 1 if fwd else 0
    ones = jnp.ones((t, HD), BF16)
    for h in range(NH):
        sl = slice(h * HD, (h + 1) * HD)
        li, lfw = 2 * NH * d + h, 2 * NH * d + NH + h
        m_old = m_ref[d, h]
        mx2 = jnp.maximum(_rep_lane(pm_parts, li, 2 * HD), jnp.concatenate([m_old, m_old], axis=1))
        mx = mx2[:, 0:HD]
        cc = _rep_lane(cum_parts, lfw, HD)
        a_rep = _rep_lane(a_parts, li, HD)
        a_row = a_rows[li:li + 1]
        q = qk_ref[:, sl]
        k_h = qk_ref[:, BR + h * HD:BR + (h + 1) * HD]
        v_aug = jnp.concatenate([v_ref[:, sl], ones], axis=1)
        w = jnp.where(mask, jnp.exp(a_row - mx2), 0.0)
        scores = (_dot_nt(q, k_h) * scale) * w
        w_state = jnp.exp(m_old - mx)
        c_old = c_ref[d, h]
        s_hi = scores.astype(BF16)
        s_mid = (scores - s_hi.astype(F32)).astype(BF16)
        intra = _dot(s_hi, v_aug)
        inter = _dot(q, c_old.astype(BF16))
        num = intra[:, 0:HD] + w_state * inter[:, 0:HD]
        den = (intra[:, HD:2 * HD] + _dot(s_mid, ones)) + w_state * inter[:, HD:2 * HD]
        o_ref[:, sl] = num / jnp.maximum(jnp.abs(den), jnp.exp(-(cc + mx)))
        total = cc[last:last + 1]
        m_new_rel = mx[last:last + 1]
        keep = jnp.exp(m_old - m_new_rel)
        w_end = jnp.exp(a_rep - m_new_rel)
        kw = (k_h.astype(F32) * (scale * w_end)).astype(BF16)
        c_ref[d, h] = jnp.concatenate([keep, keep], axis=1) * c_old + _dot_tn(kw, v_aug)
        m_ref[d, h] = total + m_new_rel


def _ret_mlstm_kernel(lg_ref, rqf_ref, rkf_ref, rvf_ref, rqb_ref, rkb_ref, rvb_ref,
                      qkf_ref, vf_ref, gf_ref, qkb_ref, vb_ref, gb_ref, bias_ref,
                      rof_ref, rob_ref, mof_ref, mob_ref, s_ref, c_ref, m_ref):
    _init_at_first_step(s_ref, c_ref, m_ref)
    _ret_direction(0, lg_ref, rqf_ref, rkf_ref, rvf_ref, rof_ref, s_ref)
    _mlstm_direction(0, qkf_ref, vf_ref, gf_ref, bias_ref, mof_ref, c_ref, m_ref)
    _ret_direction(1, lg_ref, rqb_ref, rkb_ref, rvb_ref, rob_ref, s_ref)
    _mlstm_direction(1, qkb_ref, vb_ref, gb_ref, bias_ref, mob_ref, c_ref, m_ref)


def _scans(p, gates, qk, lg, lb, gate_b, layer, *, n_batch, seq, ctx_len):
    n = p.shape[0]
    state = pltpu.VMEM((2, NH, HD, HD), F32)
    vec = pltpu.VMEM((2, NH, 1, HD), F32)

    def call(kernel, t, inputs, make_specs, scratch, name, n_mixers=1):
        assert ctx_len % t == 0 and seq % t == 0, (ctx_len, seq, t)
        n_ctx, n_lat = ctx_len // t, seq // t

        def rows(d, width, cblock=0):
            return pl.BlockSpec((t, width), lambda b, i: (_chunk_block(b, d, i, n_ctx=n_ctx, n_lat=n_lat,
                                                                        n_batch=n_batch), cblock))

        out = jax.ShapeDtypeStruct((n, BR), F32)
        return pl.pallas_call(
            kernel,
            out_shape=(out, out) * n_mixers,
            grid=(n_batch, n_ctx + n_lat),
            in_specs=make_specs(rows),
            out_specs=(rows(0, BR), rows(1, BR)) * n_mixers,
            scratch_shapes=scratch,
            compiler_params=_cparams(("parallel", "arbitrary")),
            name=name,
        )(*inputs)

    lb_spec = lambda d: pl.BlockSpec((None, None, 1, BR), lambda b, i: (layer, d, 0, 0))
    raw_a = call(_hgrn_kernel, T_HGRN, (lb, p, p, p, lb, p, p, p),
                 lambda rows: [spec for d in (0, 1)
                               for spec in (lb_spec(d), rows(d, BR, COL_HQ), rows(d, BR, COL_HF + d),
                                            rows(d, BR, COL_HV))],
                 [state], "hgrn2_scan")
    raw_bd = call(_ret_mlstm_kernel, T_ATTN, (lg, p, p, p, p, p, p, qk, p, gates, qk, p, gates, gate_b),
                  lambda rows: [pl.BlockSpec(memory_space=pltpu.SMEM)]
                  + [rows(d, BR, c) for d in (0, 1) for c in (COL_RQ, COL_RK, COL_RV)]
                  + [spec for d in (0, 1)
                     for spec in (rows(d, 2 * BR), rows(d, BR, COL_MV), rows(d, GATE_PAD))]
                  + [pl.BlockSpec((None, 1, GATE_PAD), lambda b, i: (layer, 0, 0))],
                  [state, pltpu.VMEM((2, NH, HD, 2 * HD), F32), vec], "retention_mlstm_scan", n_mixers=2)
    return raw_a, raw_bd[0:2], raw_bd[2:4]


def _cmul(ar, ai, br, bi):
    return ar * br - ai * bi, ar * bi + ai * br


def _pow_table(lam_r, lam_i, expo):
    pr = jnp.ones(expo.shape, F32)
    pi = jnp.zeros(expo.shape, F32)
    ar = jnp.broadcast_to(lam_r, expo.shape)
    ai = jnp.broadcast_to(lam_i, expo.shape)
    for k in range(S5_T.bit_length() - 1):
        nr, ni = _cmul(pr, pi, ar, ai)
        bit = (expo & (1 << k)) != 0
        pr, pi = jnp.where(bit, nr, pr), jnp.where(bit, ni, pi)
        ar, ai = _cmul(ar, ai, ar, ai)
    return pr, pi


def _s5_operators(d, u, col_ref, row_ref, m_ref, w_ref, ys_ref, z_ref, wr_ref, wi_ref):
    t = S5_T
    fwd = d == 0
    col = col_ref[d]
    rowp = row_ref[d]
    lam_rc, lam_ic = col[:, 0:1], col[:, 1:2]
    bbr, bbi = rowp[0:S5_HG], rowp[S5_HG:2 * S5_HG]
    lam_rr, lam_ir = rowp[2 * S5_HG:2 * S5_HG + 1], rowp[2 * S5_HG + 1:2 * S5_HG + 2]
    lane = lax.broadcasted_iota(jnp.int32, (t, t), 1)
    sub = lax.broadcasted_iota(jnp.int32, (t, t), 0)

    pr, pi = _pow_table(lam_rc, lam_ic, lane if fwd else t - 1 - lane)
    qr, qi = _cmul(pr, pi, lam_rc, lam_ic)
    for h in range(S5_HG):
        cr, ci = col[:, 2 + h:3 + h], col[:, 2 + S5_HG + h:3 + S5_HG + h]
        hs = slice(h * t, (h + 1) * t)
        z_ref[d, 0:S5_PP, hs] = cr * pr - ci * pi
        z_ref[d, S5_PP:2 * S5_PP, hs] = -(cr * pi + ci * pr)
        ys_ref[d, 0:S5_PP, hs] = (cr * qr - ci * qi).astype(BF16)
        ys_ref[d, S5_PP:2 * S5_PP, hs] = (-(cr * qi + ci * qr)).astype(BF16)
    b_hi, b_mid, _ = _split3(jnp.concatenate([bbr, bbi], axis=1))
    z_hi, z_mid, _ = _split3(z_ref[d])
    krow = _dot(b_hi, z_hi) + (_dot(b_hi, z_mid) + _dot(b_mid, z_hi))
    shift, keep = (0, lane >= sub) if fwd else (1, lane <= sub)
    for hp in range(S5_HG):
        for h in range(S5_HG):
            tile = jnp.broadcast_to(krow[hp:hp + 1, h * t:(h + 1) * t], (t, t))
            tile = pltpu.roll(tile, shift, 1, stride=1, stride_axis=0)
            m_ref[d, hp * t:(hp + 1) * t, h * t:(h + 1) * t] = jnp.where(keep, tile, 0.0).astype(BF16)
    tr, ti = _pow_table(lam_rr, lam_ir, t - 1 - sub if fwd else sub)
    for hp in range(S5_HG):
        br, bi = bbr[hp:hp + 1], bbi[hp:hp + 1]
        w_ref[d, hp * t:(hp + 1) * t, 0:S5_PP] = (tr * br - ti * bi).astype(BF16)
        w_ref[d, hp * t:(hp + 1) * t, S5_PP:2 * S5_PP] = (tr * bi + ti * br).astype(BF16)
    wv = _dot(u, w_ref[d])
    wr_ref[d] = wv[:, 0:S5_PP]
    wi_ref[d] = wv[:, S5_PP:2 * S5_PP]
    lr, li = lam_rr, lam_ir
    for _ in range(t.bit_length() - 1):
        lr, li = _cmul(lr, li, lr, li)
    return lr, li


def _s5_kernel(u_ref, col_ref, row_ref, o_ref, m_ref, w_ref, ys_ref, z_ref, wr_ref, wi_ref, xr_ref, xi_ref, *,
               n_batch, n_ctx, n_lat):
    t = S5_T
    u = jnp.concatenate([u_ref[hp] for hp in range(S5_HG)], axis=1)
    lam_t = [_s5_operators(d, u, col_ref, row_ref, m_ref, w_ref, ys_ref, z_ref, wr_ref, wi_ref) for d in (0, 1)]

    def make_step(base, stride, count):
        def step(i, carry):
            out = []
            for d in (0, 1):
                xr, xi = carry[2 * d], carry[2 * d + 1]
                j = i if d == 0 else count - 1 - i
                idx = pl.ds(base + j, n_batch, stride=stride)
                xr_ref[d, idx, :] = xr
                xi_ref[d, idx, :] = xi
                nr, ni = _cmul(xr, xi, *lam_t[d])
                out += [nr + wr_ref[d, idx, :], ni + wi_ref[d, idx, :]]
            return tuple(out)
        return step

    zero = jnp.zeros((n_batch, S5_PP), F32)
    carry = lax.fori_loop(0, n_ctx, make_step(n_batch * n_lat, n_ctx, n_ctx), (zero,) * 4)
    lax.fori_loop(0, n_lat, make_step(0, n_lat, n_lat), carry)
    y = None
    for d in (0, 1):
        x_prev = jnp.concatenate([xr_ref[d], xi_ref[d]], axis=1).astype(BF16)
        y_d = _dot(u, m_ref[d]) + _dot(x_prev, ys_ref[d])
        y = y_d if y is None else y + y_d
    for h in range(S5_HG):
        o_ref[h] = y[:, h * t:(h + 1) * t]


def _s5_params(a_re, a_im, log_dt, b_re, b_im, c_re, c_im):
    a_re, a_im = a_re.astype(F32), a_im.astype(F32)
    dt = jnp.exp(log_dt.astype(F32))[..., None]
    mag = jnp.exp(a_re * dt)
    lam_re, lam_im = mag * jnp.cos(a_im * dt), mag * jnp.sin(a_im * dt)
    den = a_re * a_re + a_im * a_im
    num_re, num_im = lam_re - 1.0, lam_im
    fr = (num_re * a_re + num_im * a_im) / den
    fi = (num_im * a_re - num_re * a_im) / den
    b_re, b_im = b_re.astype(F32), b_im.astype(F32)
    bb_re = fr[..., None] * b_re - fi[..., None] * b_im
    bb_im = fr[..., None] * b_im + fi[..., None] * b_re
    c_re_t = jnp.swapaxes(c_re.astype(F32), -1, -2)
    c_im_t = jnp.swapaxes(c_im.astype(F32), -1, -2)
    col = jnp.concatenate([lam_re[..., None], lam_im[..., None], c_re_t, c_im_t], axis=-1)
    col = jnp.pad(col, ((0, 0),) * 3 + ((0, S5_PP - S5_P), (0, 128 - col.shape[-1])))
    row = jnp.concatenate([jnp.swapaxes(bb_re, -1, -2), jnp.swapaxes(bb_im, -1, -2),
                           lam_re[..., None, :], lam_im[..., None, :]], axis=-2)
    row = jnp.pad(row, ((0, 0),) * 3 + ((0, S5_ROWS - row.shape[-2]), (0, S5_PP - S5_P)))
    return col, row


def _s5_scan(ut, col, row, layer, *, n_batch, seq, ctx_len):
    t = S5_T
    n = ut.shape[1]
    nc = n // t
    assert ctx_len % t == 0 and seq % t == 0, (ctx_len, seq, t)
    n_ctx, n_lat = ctx_len // t, seq // t
    ut = ut.reshape(S5_G, S5_HG, nc, t)
    kern = functools.partial(_s5_kernel, n_batch=n_batch, n_ctx=n_ctx, n_lat=n_lat)
    yt = pl.pallas_call(
        kern,
        out_shape=jax.ShapeDtypeStruct((S5_G, S5_HG, nc, t), F32),
        grid=(S5_G,),
        in_specs=[pl.BlockSpec((None, S5_HG, nc, t), lambda g: (g, 0, 0, 0)),
                  pl.BlockSpec((None, 2, None, S5_PP, 128), lambda g: (layer, 0, g, 0, 0)),
                  pl.BlockSpec((None, 2, None, S5_ROWS, S5_PP), lambda g: (layer, 0, g, 0, 0))],
        out_specs=pl.BlockSpec((None, S5_HG, nc, t), lambda g: (g, 0, 0, 0)),
        scratch_shapes=[pltpu.VMEM((2, S5_HG * t, S5_HG * t), BF16),
                        pltpu.VMEM((2, S5_HG * t, 2 * S5_PP), BF16),
                        pltpu.VMEM((2, 2 * S5_PP, S5_HG * t), BF16),
                        pltpu.VMEM((2, 2 * S5_PP, S5_HG * t), F32),
                        pltpu.VMEM((2, nc, S5_PP), F32), pltpu.VMEM((2, nc, S5_PP), F32),
                        pltpu.VMEM((2, nc, S5_PP), F32), pltpu.VMEM((2, nc, S5_PP), F32)],
        compiler_params=_cparams(("parallel",)),
        name="s5_scan",
    )(ut, col, row)
    return yt.reshape(BR, n)


def _head_rms(x, g):
    parts = []
    for h in range(NH):
        xh = x[:, h * HD:(h + 1) * HD]
        parts.append(xh * lax.rsqrt(jnp.mean(jnp.square(xh), axis=-1, keepdims=True) + NORM_EPS))
    return jnp.concatenate(parts, axis=-1) * g


def _finish_kernel(raf_ref, rab_ref, rbf_ref, rbb_ref, rc_ref, rdf_ref, rdb_ref, ga_ref, gb_ref, u_ref, gd_ref,
                   mg_ref, x_ref, mod_ref, vec_ref, glu_w_ref, wb_ref, wo_ref, o_ref):
    vec = vec_ref[...]
    oa = _head_rms(raf_ref[...] + rab_ref[...], vec[0:1]) * _silu(ga_ref[...].astype(F32))
    ob = _head_rms(rbf_ref[...] + rbb_ref[...], vec[1:2]) * _silu(gb_ref[...].astype(F32))
    od = _head_rms(rdf_ref[...] + rdb_ref[...], vec[2:3]) * _silu(gd_ref[...].astype(F32))
    yc = rc_ref[...].T + vec[3:4] * u_ref[...].astype(F32)
    yc = 0.5 * yc * (1.0 + lax.erf(yc * (2.0 ** -0.5)))
    oc = yc * _sigmoid(_dot(yc.astype(BF16), glu_w_ref[...]) + vec[4:5])
    y2 = None
    for j, o in enumerate((oa, ob, oc, od)):
        gate2 = jnp.tanh(0.5 * mg_ref[:, j * D_MODEL:(j + 1) * D_MODEL].astype(F32)) + 1.0
        term = gate2 * _dot(o.astype(BF16), wb_ref[j])
        y2 = term if y2 is None else y2 + term
    mix2 = _dot(y2.astype(BF16), wo_ref[...])
    o_ref[...] = x_ref[...] + (0.5 * mod_ref[2:3, :]) * mix2


def _finish(raw_a, raw_b, raw_c, raw_d, p, x_all, mod, vec, glu_w, w_branch, w_out, layer, *, n_rows, n_batch,
            seq, tm=256):
    mrow = _mod_row_map(n_rows, tm, n_batch, seq)
    one = pl.Buffered(1)

    def pcol(cblock):
        return pl.BlockSpec((tm, BR), lambda i: (i, cblock))

    raw = pl.BlockSpec((tm, BR), lambda i: (i, 0))
    return pl.pallas_call(
        _finish_kernel,
        out_shape=jax.ShapeDtypeStruct((n_rows, D_MODEL), F32),
        grid=(n_rows // tm,),
        in_specs=[raw, raw, raw, raw, pl.BlockSpec((BR, tm), lambda i: (0, i)), raw, raw,
                  pcol(COL_HG), pcol(COL_RG), pcol(COL_SU), pcol(COL_MZ),
                  pl.BlockSpec((tm, N_BRANCH * D_MODEL), lambda i: (i, COL_MERGE)),
                  pl.BlockSpec((tm, D_MODEL), lambda i: (i, 0)),
                  pl.BlockSpec((None, 8, D_MODEL), lambda i: (mrow(i), 0, 0)),
                  pl.BlockSpec((None, 8, BR), lambda i: (layer, 0, 0), pipeline_mode=one),
                  pl.BlockSpec((None, BR, BR), lambda i: (layer, 0, 0), pipeline_mode=one),
                  pl.BlockSpec((None, N_BRANCH, BR, D_MODEL), lambda i: (layer, 0, 0, 0), pipeline_mode=one),
                  pl.BlockSpec((None, D_MODEL, D_MODEL), lambda i: (layer, 0, 0), pipeline_mode=one)],
        out_specs=pl.BlockSpec((tm, D_MODEL), lambda i: (i, 0)),
        compiler_params=_cparams(("parallel",)),
        name="finish",
    )(*raw_a, *raw_b, raw_c, *raw_d, p, p, p, p, p, x_all, mod, vec, glu_w, w_branch, w_out)


def _mlp_kernel(x_ref, mod_ref, g_ref, w1_ref, w2_ref, fin_ref, o_ref, xn_ref, acc_ref, *, final_norm):
    j = pl.program_id(1)

    @pl.when(j == 0)
    def _():
        h = _norm_mod(x_ref[...], g_ref[...], mod_ref[3:4, :], mod_ref[4:5, :])
        xn_ref[...] = h.astype(BF16)
        acc_ref[...] = jnp.zeros_like(acc_ref)

    a = jnp.square(jnp.maximum(_dot(xn_ref[...], w1_ref[...]), 0.0))
    acc_ref[...] += _dot(a.astype(BF16), w2_ref[...])

    @pl.when(j == pl.num_programs(1) - 1)
    def _():
        y = x_ref[...] + mod_ref[5:6, :] * acc_ref[...]
        if final_norm:
            y = (y * lax.rsqrt(jnp.mean(jnp.square(y), axis=-1, keepdims=True) + NORM_EPS)) * fin_ref[...]
        o_ref[...] = y


def _mlp(x_all, mod, norm_w, w1, w2, fin_w, layer, *, n_rows, n_batch, seq, final_norm, tm=512, tf=1024):
    mrow = _mod_row_map(n_rows, tm, n_batch, seq)
    assert D_FF % tf == 0, tf
    kern = functools.partial(_mlp_kernel, final_norm=final_norm)
    return pl.pallas_call(
        kern,
        out_shape=jax.ShapeDtypeStruct((n_rows, D_MODEL), F32),
        grid=(n_rows // tm, D_FF // tf),
        in_specs=[pl.BlockSpec((tm, D_MODEL), lambda i, j: (i, 0)),
                  pl.BlockSpec((None, 8, D_MODEL), lambda i, j: (mrow(i), 0, 0)),
                  pl.BlockSpec((None, 1, D_MODEL), lambda i, j: (layer, 0, 0)),
                  pl.BlockSpec((None, D_MODEL, tf), lambda i, j: (layer, 0, j)),
                  pl.BlockSpec((None, tf, D_MODEL), lambda i, j: (layer, j, 0)),
                  pl.BlockSpec((1, D_MODEL), lambda i, j: (0, 0))],
        out_specs=pl.BlockSpec((tm, D_MODEL), lambda i, j: (i, 0)),
        scratch_shapes=[pltpu.VMEM((tm, D_MODEL), BF16), pltpu.VMEM((tm, D_MODEL), F32)],
        compiler_params=_cparams(("parallel", "arbitrary")),
        name="mlp",
    )(x_all, mod, norm_w, w1, w2, fin_w.reshape(1, D_MODEL))


W_PREP_ROWS = 1024
N_MERGE_BLK = N_BRANCH * D_MODEL // W_PREP_ROWS
MERGE_OFF = GATE_OFF + 4 * NH


def _w_prep_kernel(a_ref, g_ref, main_ref, gate_ref):
    @pl.when(pl.program_id(1) == 0)
    def _():
        gate_ref[...] = jnp.zeros_like(gate_ref)
        gate_ref[0:4 * NH, :] = g_ref[0].astype(BF16)

    main_ref[...] = a_ref[0].astype(BF16)


def _split_w_in(w):
    depth = w.shape[0]
    wt = jnp.swapaxes(w, 1, 2)
    tr = W_PREP_ROWS

    def a_map(l, j):
        row = jnp.where(j < N_MERGE_BLK, MERGE_OFF + j * tr, (j - N_MERGE_BLK) * tr)
        return l, pl.multiple_of(row, 16), 0

    return pl.pallas_call(
        _w_prep_kernel,
        out_shape=(jax.ShapeDtypeStruct((depth, P_WIDTH, D_MODEL), BF16),
                   jax.ShapeDtypeStruct((depth, GATE_PAD, D_MODEL), BF16)),
        grid=(depth, P_WIDTH // tr),
        in_specs=[pl.BlockSpec((pl.Element(1), pl.Element(tr), pl.Element(D_MODEL)), a_map),
                  pl.BlockSpec((pl.Element(1), pl.Element(4 * NH), pl.Element(D_MODEL)),
                               lambda l, j: (l, GATE_OFF, 0))],
        out_specs=(pl.BlockSpec((None, tr, D_MODEL), lambda l, j: (l, j, 0)),
                   pl.BlockSpec((None, GATE_PAD, D_MODEL), lambda l, j: (l, 0, 0))),
        compiler_params=_cparams(("parallel", "arbitrary")),
        name="w_in_prep",
    )(wt, wt)


def kernel(x, c, ctx, c_ctx, w_mod, b_mod, norm_mix, norm_mlp, w_in, hgrn_lb_logits, hgrn_norm, ret_decay, ret_norm, s5_a_re, s5_a_im, s5_log_dt, s5_b_re, s5_b_im, s5_c_re, s5_c_im, s5_d, s5_glu_w, s5_glu_b, mlstm_conv_w, mlstm_conv_b, mlstm_gate_b, mlstm_norm, w_branch, w_out, w_ff1, w_ff2, final_norm):
    n_batch, seq, _ = x.shape
    ctx_len = ctx.shape[1]
    depth = w_in.shape[0]
    nl = n_batch * seq
    dims = dict(n_batch=n_batch, seq=seq)

    p_lb = jax.nn.softmax(hgrn_lb_logits.astype(F32), axis=0)
    lower_bounds = (jnp.cumsum(p_lb, axis=0) - p_lb[0]).reshape(depth, 2, 1, BR)
    lg = jnp.log1p(-jnp.exp(ret_decay.astype(F32)))
    gate_b = jnp.pad(mlstm_gate_b.reshape(depth, 1, 4 * NH).astype(F32), ((0, 0), (0, 0), (0, GATE_PAD - 4 * NH)))
    vec = jnp.stack([hgrn_norm, ret_norm, mlstm_norm, s5_d, s5_glu_b], axis=1).astype(F32)
    vec = jnp.pad(vec, ((0, 0), (0, 8 - vec.shape[1]), (0, 0)))
    w_main, w_gate = _split_w_in(w_in)
    glu_w, wb, wo = s5_glu_w.astype(BF16), w_branch.astype(BF16), w_out.astype(BF16)
    w1, w2 = w_ff1.astype(BF16), w_ff2.astype(BF16)
    norm_mix3 = norm_mix.reshape(depth, 1, D_MODEL)
    norm_mlp3 = norm_mlp.reshape(depth, 1, D_MODEL)
    conv_w = mlstm_conv_w.reshape(depth, 9, 2 * BR)
    conv_b = mlstm_conv_b.reshape(depth, 1, 2 * BR)
    s5_col, s5_row = _s5_params(s5_a_re, s5_a_im, s5_log_dt, s5_b_re, s5_b_im, s5_c_re, s5_c_im)

    c_pad = jnp.zeros((8, D_MODEL), F32).at[:n_batch].set(c).at[n_batch].set(c_ctx)
    x_all = jnp.concatenate([x.reshape(nl, D_MODEL), ctx.reshape(n_batch * ctx_len, D_MODEL)], axis=0)

    mod_all = _modulation(c_pad, w_mod, b_mod)[:, :n_batch + 1].reshape(depth, n_batch + 1, N_MOD, D_MODEL)
    mod_all = jnp.pad(mod_all, ((0, 0), (0, 0), (0, 8 - N_MOD), (0, 0)))
    for l in range(depth):
        last = l == depth - 1
        mod = mod_all[l]
        p, gates, ut = _in_proj(x_all, mod, norm_mix3, w_main, w_gate, l, **dims)
        qk = _mlstm_conv(p, conv_w, conv_b, l, ctx_len=ctx_len, **dims)
        raw_a, raw_b, raw_d = _scans(p, gates, qk, lg[l], lower_bounds, gate_b, l, ctx_len=ctx_len, **dims)
        raw_c = _s5_scan(ut, s5_col, s5_row, l, ctx_len=ctx_len, **dims)
        n_rows = nl if last else x_all.shape[0]
        x_mid = _finish(raw_a, raw_b, raw_c, raw_d, p, x_all, mod, vec, glu_w, wb, wo, l, n_rows=n_rows, **dims)
        x_all = _mlp(x_mid, mod, norm_mlp3, w1, w2, final_norm, l, n_rows=n_rows, final_norm=last, **dims)
    return x_all.reshape(n_batch, seq, D_MODEL)
```

```python
import functools
import math

import jax
import jax.numpy as jnp
from jax import lax
from jax.experimental import pallas as pl
from jax.experimental.pallas import tpu as pltpu

F32 = jnp.float32
BF16 = jnp.bfloat16

D_MODEL = 2048
N_BRANCH = 4
BR = D_MODEL // N_BRANCH
HD = 128
NH = BR // HD
S5_HG = 16
S5_G = BR // S5_HG
S5_P = 64
D_FF = 4 * D_MODEL
N_MOD = 6
GRID_W = 64
NORM_EPS = 1e-6
NEG_BIG = -1e30
F_TINY = 1e-30
S5_DT_MIN = 1e-3

COL_MERGE = 0
COL_HQ, COL_HF, COL_HV, COL_HG = 16, 17, 19, 20
COL_RQ, COL_RK, COL_RV, COL_RG = 21, 22, 23, 24
COL_SU = 25
COL_MQK, COL_MV, COL_MZ = 26, 28, 29
P_WIDTH = 30 * BR
GATE_OFF = 14 * BR
GATE_PAD = 128

T_HGRN = 128
T_ATTN = 256
S5_T = 128
S5_PP = 128
S5_ROWS = 40
VMEM_LIMIT = 56 * 1024 * 1024

_HI = lax.Precision.HIGHEST


def _cparams(sem):
    return pltpu.CompilerParams(dimension_semantics=sem, vmem_limit_bytes=VMEM_LIMIT)


def _dot(a, b):
    return jnp.dot(a, b, preferred_element_type=F32)


def _dot_nt(a, b):
    return lax.dot_general(a, b, (((1,), (1,)), ((), ())), preferred_element_type=F32)


def _dot_tn(a, b):
    return lax.dot_general(a, b, (((0,), (0,)), ((), ())), preferred_element_type=F32)


def _split3(x):
    hi = x.astype(BF16)
    r1 = x - hi.astype(F32)
    mid = r1.astype(BF16)
    return hi, mid, (r1 - mid.astype(F32)).astype(BF16)


def _split2(x):
    hi = x.astype(BF16)
    return hi, (x - hi.astype(F32)).astype(BF16)


def _cumsum_rows(tri, x):
    hi, mid = _split2(x)
    return _dot(tri, hi) + _dot(tri, mid)


def _sigmoid(x):
    return 1.0 / (1.0 + jnp.exp(-x))


def _silu(x):
    return x * _sigmoid(x)


def _mod_kernel(c_ref, w_ref, b_ref, o_ref):
    @pl.when(pl.program_id(1) == 0)
    def _():
        o_ref[...] = jnp.broadcast_to(b_ref[...], o_ref.shape)

    a = _silu(c_ref[...]).astype(BF16)
    o_ref[...] += _dot(a, w_ref[...].astype(BF16))


def _modulation(c_pad, w_mod, b_mod, tk=256):
    depth, _, n = w_mod.shape
    return pl.pallas_call(
        _mod_kernel,
        out_shape=jax.ShapeDtypeStruct((depth, 8, n), F32),
        grid=(depth, D_MODEL // tk),
        in_specs=[pl.BlockSpec((8, tk), lambda l, k: (0, k)),
                  pl.BlockSpec((None, tk, n), lambda l, k: (l, k, 0)),
                  pl.BlockSpec((None, 1, n), lambda l, k: (l, 0, 0))],
        out_specs=pl.BlockSpec((None, 8, n), lambda l, k: (l, 0, 0)),
        compiler_params=_cparams(("parallel", "arbitrary")),
        name="modulation",
    )(c_pad, w_mod, b_mod.reshape(depth, 1, n))


def _norm_mod(x, g, shift, scale):
    y = x * lax.rsqrt(jnp.mean(jnp.square(x), axis=-1, keepdims=True) + NORM_EPS)
    return (y * g) * (1.0 + scale) + shift


def _inproj_kernel(x_ref, mod_ref, g_ref, w_ref, wg_ref, p_ref, gate_ref, ut_ref, xn_ref, *, su_block, su_off):
    @pl.when(pl.program_id(1) == 0)
    def _():
        h = _norm_mod(x_ref[...], g_ref[...], mod_ref[0:1, :], mod_ref[1:2, :])
        xn_ref[...] = h.astype(BF16)
        gate_ref[...] = _dot_nt(xn_ref[...], wg_ref[...])

    acc = _dot_nt(xn_ref[...], w_ref[...])
    p_ref[...] = acc.astype(p_ref.dtype)

    @pl.when(pl.program_id(1) == su_block)
    def _():
        ut_ref[...] = acc[:, su_off:su_off + BR].T.astype(ut_ref.dtype)


def _mod_row_map(n_rows, tm, n_batch, seq):
    assert n_rows % tm == 0 and seq % tm == 0, (n_rows, seq, tm)
    n_lat_tiles, tiles_per_batch = n_batch * seq // tm, seq // tm

    def row(i):
        return jnp.where(i < n_lat_tiles, i // tiles_per_batch, n_batch)
    return row


def _in_proj(x_all, mod, norm_w, w_main, w_gate, layer, *, n_batch, seq, tm=1024, tn=1536):
    n = x_all.shape[0]
    assert P_WIDTH % tn == 0, tn
    su_block, su_off = divmod(COL_SU * BR, tn)
    assert su_off + BR <= tn, (su_off, tn)
    mrow = _mod_row_map(n, tm, n_batch, seq)
    return pl.pallas_call(
        functools.partial(_inproj_kernel, su_block=su_block, su_off=su_off),
        out_shape=(jax.ShapeDtypeStruct((n, P_WIDTH), BF16),
                   jax.ShapeDtypeStruct((n, GATE_PAD), F32),
                   jax.ShapeDtypeStruct((BR, n), BF16)),
        grid=(n // tm, P_WIDTH // tn),
        in_specs=[pl.BlockSpec((tm, D_MODEL), lambda i, j: (i, 0)),
                  pl.BlockSpec((None, 8, D_MODEL), lambda i, j: (mrow(i), 0, 0)),
                  pl.BlockSpec((None, 1, D_MODEL), lambda i, j: (layer, 0, 0)),
                  pl.BlockSpec((None, tn, D_MODEL), lambda i, j: (layer, j, 0)),
                  pl.BlockSpec((None, GATE_PAD, D_MODEL), lambda i, j: (layer, 0, 0))],
        out_specs=(pl.BlockSpec((tm, tn), lambda i, j: (i, j)),
                   pl.BlockSpec((tm, GATE_PAD), lambda i, j: (i, 0)),
                   pl.BlockSpec((BR, tm), lambda i, j: (0, i))),
        scratch_shapes=[pltpu.VMEM((tm, D_MODEL), BF16)],
        compiler_params=_cparams(("parallel", "arbitrary")),
        name="in_proj",
    )(x_all, mod, norm_w, w_main, w_gate)


CONV_ROWS = 1024


def _conv_kernel(up_ref, x_ref, dn_ref, w_ref, b_ref, o_ref, *, n_lat_blocks, blocks_per_img, ctx_len):
    rb = pl.program_id(0)
    x = x_ref[...].astype(F32)
    w = w_ref[...]
    n = CONV_ROWS

    @pl.when(rb < n_lat_blocks)
    def _():
        rr = rb % blocks_per_img
        up = jnp.where(rr > 0, up_ref[...].astype(F32), 0.0)
        dn = jnp.where(rr < blocks_per_img - 1, dn_ref[...].astype(F32), 0.0)
        ext = jnp.concatenate([up, x, dn], axis=0)
        ne = n + 2 * GRID_W
        col = lax.broadcasted_iota(jnp.int32, (ne, 1), 0) % GRID_W
        xl = jnp.where(col >= 1, pltpu.roll(ext, 1, 0), 0.0)
        xr = jnp.where(col <= GRID_W - 2, pltpu.roll(ext, ne - 1, 0), 0.0)
        acc = None
        for i in range(3):
            y = w[3 * i:3 * i + 1] * xl + w[3 * i + 1:3 * i + 2] * ext + w[3 * i + 2:3 * i + 3] * xr
            part = y[i * GRID_W:i * GRID_W + n]
            acc = part if acc is None else acc + part
        o_ref[...] = _silu(acc + b_ref[...]).astype(o_ref.dtype)

    @pl.when(rb >= n_lat_blocks)
    def _():
        t = lax.broadcasted_iota(jnp.int32, (n, 1), 0) % ctx_len
        xl = jnp.where(t >= 1, pltpu.roll(x, 1, 0), 0.0)
        xr = jnp.where(t <= ctx_len - 2, pltpu.roll(x, n - 1, 0), 0.0)
        acc = w[3:4] * xl + w[4:5] * x + w[5:6] * xr
        o_ref[...] = _silu(acc + b_ref[...]).astype(o_ref.dtype)


def _mlstm_conv(p, conv_w, conv_b, layer, *, n_batch, seq, ctx_len, cb=2 * BR):
    assert CONV_ROWS % ctx_len == 0 and seq % CONV_ROWS == 0 and (n_batch * ctx_len) % CONV_ROWS == 0
    n = p.shape[0]
    n_blocks = n // CONV_ROWS
    hb = CONV_ROWS // GRID_W
    n_halo = n // GRID_W
    c0 = COL_MQK * BR // cb
    kern = functools.partial(_conv_kernel, n_lat_blocks=n_batch * seq // CONV_ROWS,
                             blocks_per_img=seq // CONV_ROWS, ctx_len=ctx_len)
    return pl.pallas_call(
        kern,
        out_shape=jax.ShapeDtypeStruct((n, 2 * BR), BF16),
        grid=(n_blocks, 2 * BR // cb),
        in_specs=[pl.BlockSpec((GRID_W, cb), lambda r, c: (jnp.maximum(r * hb - 1, 0), c0 + c)),
                  pl.BlockSpec((CONV_ROWS, cb), lambda r, c: (r, c0 + c)),
                  pl.BlockSpec((GRID_W, cb), lambda r, c: (jnp.minimum((r + 1) * hb, n_halo - 1), c0 + c)),
                  pl.BlockSpec((None, 9, cb), lambda r, c: (layer, 0, c)),
                  pl.BlockSpec((None, 1, cb), lambda r, c: (layer, 0, c))],
        out_specs=pl.BlockSpec((CONV_ROWS, cb), lambda r, c: (r, c)),
        compiler_params=_cparams(("parallel", "parallel")),
        name="mlstm_conv",
    )(p, p, p, conv_w, conv_b)


def _chunk_block(b, d, i, *, n_ctx, n_lat, n_batch):
    ctx_j = i if d == 0 else n_ctx - 1 - i
    lat_j = i - n_ctx if d == 0 else n_ctx + n_lat - 1 - i
    return jnp.where(i < n_ctx, n_batch * n_lat + b * n_ctx + ctx_j, b * n_lat + lat_j)


def _order_mask(d, t):
    r = lax.broadcasted_iota(jnp.int32, (t, t), 0)
    c = lax.broadcasted_iota(jnp.int32, (t, t), 1)
    return (c <= r) if d == 0 else (c >= r)


def _init_at_first_step(*refs):
    @pl.when(pl.program_id(1) == 0)
    def _():
        for ref in refs:
            ref[...] = jnp.zeros_like(ref)


def _ret_direction(d, lg_ref, q_ref, k_ref, v_ref, o_ref, s_ref):
    t = T_ATTN
    r = lax.broadcasted_iota(jnp.int32, (t, t), 0)
    c = lax.broadcasted_iota(jnp.int32, (t, t), 1)
    rel = (r - c) if d == 0 else (c - r)
    relf = jnp.maximum(rel, 0).astype(F32)
    tt = lax.broadcasted_iota(jnp.int32, (t, 1), 0)
    pos = (tt if d == 0 else t - 1 - tt).astype(F32)
    scale = HD ** -0.5
    for h in range(NH):
        sl = slice(h * HD, (h + 1) * HD)
        lg = lg_ref[d, h]
        q = q_ref[:, sl]
        k = k_ref[:, sl]
        v = v_ref[:, sl]
        intra = jnp.where(rel >= 0, jnp.exp(lg * relf), 0.0)
        scores = (_dot_nt(q, k) * scale) * intra
        s_old = s_ref[d, h]
        out = _dot(scores.astype(BF16), v) + jnp.exp(lg * (pos + 1.0)) * _dot(q, s_old.astype(BF16))
        kd = (k.astype(F32) * (scale * jnp.exp(lg * (t - 1.0 - pos)))).astype(BF16)
        s_ref[d, h] = jnp.exp(lg * t) * s_old + _dot_tn(kd, v)
        o_ref[:, sl] = out


def _hgrn_direction(d, lb_ref, q_ref, z_ref, v_ref, o_ref, st_ref):
    t = T_HGRN
    fwd = d == 0
    lb = lb_ref[...]
    z = z_ref[...].astype(F32)
    sig = _sigmoid(z)
    f = lb + (1.0 - lb) * sig
    logf = jnp.log(jnp.maximum(f, F_TINY))
    kk = (1.0 - lb) * (1.0 - sig)
    qq = _silu(q_ref[...].astype(F32))
    r2 = lax.broadcasted_iota(jnp.int32, (t, t), 0)
    c2 = lax.broadcasted_iota(jnp.int32, (t, t), 1)
    tri = jnp.where((c2 <= r2) if fwd else (c2 >= r2), 1.0, 0.0).astype(BF16)
    cum = _cumsum_rows(tri, logf)
    xdiff = jnp.where((c2 < r2) if fwd else (c2 > r2), r2 ^ c2, 0)
    row = lax.broadcasted_iota(jnp.int32, (t, 1), 0)
    if fwd:
        e_ref = jnp.where(row == 0, 0.0, pltpu.roll(cum, 1, 0))
    else:
        e_ref = jnp.where(row == t - 1, 0.0, pltpu.roll(cum, t - 1, 0))
    f_ref = cum
    levels = []
    w = 1
    while w < t:
        if w == 1:
            q_l, k_l = (qq * jnp.maximum(f, F_TINY)).astype(BF16), kk.astype(BF16)
        else:
            q_l = (qq * jnp.exp(cum - e_ref)).astype(BF16)
            k_l = (kk * jnp.exp(f_ref - cum)).astype(BF16)
        levels.append((q_l, k_l, (xdiff >> (w.bit_length() - 1)) == 1))
        upper = (row & w) != 0
        if fwd:
            e_ref = jnp.where(upper, pltpu.roll(e_ref, w, 0), e_ref)
            f_ref = jnp.where(upper, f_ref, pltpu.roll(f_ref, t - w, 0))
        else:
            e_ref = jnp.where(upper, e_ref, pltpu.roll(e_ref, t - w, 0))
            f_ref = jnp.where(upper, pltpu.roll(f_ref, w, 0), f_ref)
        w *= 2
    q_b = qq.astype(BF16)
    k_b = kk.astype(BF16)
    tail = cum[t - 1:t] if fwd else cum[0:1]
    q_state = (qq * jnp.exp(cum)).astype(BF16)
    k_state = (kk * jnp.exp(tail - cum)).astype(BF16)
    decay = jnp.exp(tail)
    for h in range(NH):
        sl = slice(h * HD, (h + 1) * HD)
        v_h = v_ref[:, sl]
        scores = jnp.where(r2 == c2, _dot_nt(q_b[:, sl], k_b[:, sl]), 0.0)
        for q_l, k_l, m in levels:
            scores = jnp.where(m, _dot_nt(q_l[:, sl], k_l[:, sl]), scores)
        st = st_ref[d, h]
        o_ref[:, sl] = _dot(scores.astype(BF16), v_h) + _dot_nt(q_state[:, sl], st.astype(BF16))
        st_ref[d, h] = st * decay[:, sl] + _dot_tn(v_h, k_state[:, sl])


def _hgrn_kernel(lbf_ref, qf_ref, zf_ref, vf_ref, lbb_ref, qb_ref, zb_ref, vb_ref, of_ref, ob_ref, st_ref):
    _init_at_first_step(st_ref)
    _hgrn_direction(0, lbf_ref, qf_ref, zf_ref, vf_ref, of_ref, st_ref)
    _hgrn_direction(1, lbb_ref, qb_ref, zb_ref, vb_ref, ob_ref, st_ref)


def _log_sigmoid(x):
    return jnp.minimum(x, 0.0) - jnp.log(1.0 + jnp.exp(-jnp.abs(x)))


def _rep_lane(parts, lane, width):
    sel = jnp.where(lax.broadcasted_iota(jnp.int32, (GATE_PAD, width), 0) == lane, 1.0, 0.0).astype(BF16)
    hi, mid = parts
    return _dot(hi, sel) + _dot(mid, sel)


def _mlstm_direction(d, qk_ref, v_ref, g_ref, gb_ref, o_ref, c_ref, m_ref):
    t = T_ATTN
    assert t == 2 * HD
    fwd = d == 0
    g = g_ref[...] + gb_ref[...]
    r2 = lax.broadcasted_iota(jnp.int32, (t, t), 0)
    c2 = lax.broadcasted_iota(jnp.int32, (t, t), 1)
    mask = (c2 <= r2) if fwd else (c2 >= r2)
    tri = jnp.where(mask, 1.0, 0.0).astype(BF16)
    eye = jnp.where(r2 == c2, 1.0, 0.0).astype(BF16)
    lf = _log_sigmoid(g)
    cum_c = _cumsum_rows(tri, lf)
    a_col = g - pltpu.roll(cum_c, GATE_PAD - NH, 1)
    row = lax.broadcasted_iota(jnp.int32, (t, 1), 0)
    pm = a_col
    k = 1
    while k < t:
        if fwd:
            pm = jnp.maximum(pm, jnp.where(row >= k, pltpu.roll(pm, k, 0), NEG_BIG))
        else:
            pm = jnp.maximum(pm, jnp.where(row < t - k, pltpu.roll(pm, t - k, 0), NEG_BIG))
        k *= 2
    a_parts, pm_parts, cum_parts = _split2(a_col), _split2(pm), _split2(cum_c)
    a_rows = _dot_tn(a_parts[0], eye) + _dot_tn(a_parts[1], eye)
    scale = HD ** -0.5
    last = t - 1 if fwd else 0
    ones = jnp.ones((t, HD), BF16)
    for h in range(NH):
        sl = slice(h * HD, (h + 1) * HD)
        li, lfw = 2 * NH * d + h, 2 * NH * d + NH + h
        m_old = m_ref[d, h]
        mx2 = jnp.maximum(_rep_lane(pm_parts, li, 2 * HD), jnp.concatenate([m_old, m_old], axis=1))
        mx = mx2[:, 0:HD]
        cc = _rep_lane(cum_parts, lfw, HD)
        a_rep = _rep_lane(a_parts, li, HD)
        a_row = a_rows[li:li + 1]
        q = qk_ref[:, sl]
        k_h = qk_ref[:, BR + h * HD:BR + (h + 1) * HD]
        v_aug = jnp.concatenate([v_ref[:, sl], ones], axis=1)
        w = jnp.where(mask, jnp.exp(a_row - mx2), 0.0)
        scores = (_dot_nt(q, k_h) * scale) * w
        w_state = jnp.exp(m_old - mx)
        c_old = c_ref[d, h]
        s_hi = scores.astype(BF16)
        s_mid = (scores - s_hi.astype(F32)).astype(BF16)
        intra = _dot(s_hi, v_aug)
        inter = _dot(q, c_old.astype(BF16))
        num = intra[:, 0:HD] + w_state * inter[:, 0:HD]
        den = (intra[:, HD:2 * HD] + _dot(s_mid, ones)) + w_state * inter[:, HD:2 * HD]
        o_ref[:, sl] = num / jnp.maximum(jnp.abs(den), jnp.exp(-(cc + mx)))
        total = cc[last:last + 1]
        m_new_rel = mx[last:last + 1]
        keep = jnp.exp(m_old - m_new_rel)
        w_end = jnp.exp(a_rep - m_new_rel)
        kw = (k_h.astype(F32) * (scale * w_end)).astype(BF16)
        c_ref[d, h] = jnp.concatenate([keep, keep], axis=1) * c_old + _dot_tn(kw, v_aug)
        m_ref[d, h] = total + m_new_rel


def _ret_mlstm_kernel(lg_ref, rqf_ref, rkf_ref, rvf_ref, rqb_ref, rkb_ref, rvb_ref,
                      qkf_ref, vf_ref, gf_ref, qkb_ref, vb_ref, gb_ref, bias_ref,
                      rof_ref, rob_ref, mof_ref, mob_ref, s_ref, c_ref, m_ref):
    _init_at_first_step(s_ref, c_ref, m_ref)
    _ret_direction(0, lg_ref, rqf_ref, rkf_ref, rvf_ref, rof_ref, s_ref)
    _mlstm_direction(0, qkf_ref, vf_ref, gf_ref, bias_ref, mof_ref, c_ref, m_ref)
    _ret_direction(1, lg_ref, rqb_ref, rkb_ref, rvb_ref, rob_ref, s_ref)
    _mlstm_direction(1, qkb_ref, vb_ref, gb_ref, bias_ref, mob_ref, c_ref, m_ref)


def _scans(p, gates, qk, lg, lb, gate_b, layer, *, n_batch, seq, ctx_len):
    n = p.shape[0]
    state = pltpu.VMEM((2, NH, HD, HD), F32)
    vec = pltpu.VMEM((2, NH, 1, HD), F32)

    def call(kernel, t, inputs, make_specs, scratch, name, n_mixers=1):
        assert ctx_len % t == 0 and seq % t == 0, (ctx_len, seq, t)
        n_ctx, n_lat = ctx_len // t, seq // t

        def rows(d, width, cblock=0):
            return pl.BlockSpec((t, width), lambda b, i: (_chunk_block(b, d, i, n_ctx=n_ctx, n_lat=n_lat,
                                                                        n_batch=n_batch), cblock))

        out = jax.ShapeDtypeStruct((n, BR), F32)
        return pl.pallas_call(
            kernel,
            out_shape=(out, out) * n_mixers,
            grid=(n_batch, n_ctx + n_lat),
            in_specs=make_specs(rows),
            out_specs=(rows(0, BR), rows(1, BR)) * n_mixers,
            scratch_shapes=scratch,
            compiler_params=_cparams(("parallel", "arbitrary")),
            name=name,
        )(*inputs)

    lb_spec = lambda d: pl.BlockSpec((None, None, 1, BR), lambda b, i: (layer, d, 0, 0))
    raw_a = call(_hgrn_kernel, T_HGRN, (lb, p, p, p, lb, p, p, p),
                 lambda rows: [spec for d in (0, 1)
                               for spec in (lb_spec(d), rows(d, BR, COL_HQ), rows(d, BR, COL_HF + d),
                                            rows(d, BR, COL_HV))],
                 [state], "hgrn2_scan")
    raw_bd = call(_ret_mlstm_kernel, T_ATTN, (lg, p, p, p, p, p, p, qk, p, gates, qk, p, gates, gate_b),
                  lambda rows: [pl.BlockSpec(memory_space=pltpu.SMEM)]
                  + [rows(d, BR, c) for d in (0, 1) for c in (COL_RQ, COL_RK, COL_RV)]
                  + [spec for d in (0, 1)
                     for spec in (rows(d, 2 * BR), rows(d, BR, COL_MV), rows(d, GATE_PAD))]
                  + [pl.BlockSpec((None, 1, GATE_PAD), lambda b, i: (layer, 0, 0))],
                  [state, pltpu.VMEM((2, NH, HD, 2 * HD), F32), vec], "retention_mlstm_scan", n_mixers=2)
    return raw_a, raw_bd[0:2], raw_bd[2:4]


def _cmul(ar, ai, br, bi):
    return ar * br - ai * bi, ar * bi + ai * br


def _pow_table(lam_r, lam_i, expo):
    pr = jnp.ones(expo.shape, F32)
    pi = jnp.zeros(expo.shape, F32)
    ar = jnp.broadcast_to(lam_r, expo.shape)
    ai = jnp.broadcast_to(lam_i, expo.shape)
    for k in range(S5_T.bit_length() - 1):
        nr, ni = _cmul(pr, pi, ar, ai)
        bit = (expo & (1 << k)) != 0
        pr, pi = jnp.where(bit, nr, pr), jnp.where(bit, ni, pi)
        ar, ai = _cmul(ar, ai, ar, ai)
    return pr, pi


def _s5_operators(d, u, col_ref, row_ref, m_ref, w_ref, ys_ref, z_ref, wr_ref, wi_ref):
    t = S5_T
    fwd = d == 0
    col = col_ref[d]
    rowp = row_ref[d]
    lam_rc, lam_ic = col[:, 0:1], col[:, 1:2]
    bbr, bbi = rowp[0:S5_HG], rowp[S5_HG:2 * S5_HG]
    lam_rr, lam_ir = rowp[2 * S5_HG:2 * S5_HG + 1], rowp[2 * S5_HG + 1:2 * S5_HG + 2]
    lane = lax.broadcasted_iota(jnp.int32, (t, t), 1)
    sub = lax.broadcasted_iota(jnp.int32, (t, t), 0)

    pr, pi = _pow_table(lam_rc, lam_ic, lane if fwd else t - 1 - lane)
    qr, qi = _cmul(pr, pi, lam_rc, lam_ic)
    for h in range(S5_HG):
        cr, ci = col[:, 2 + h:3 + h], col[:, 2 + S5_HG + h:3 + S5_HG + h]
        hs = slice(h * t, (h + 1) * t)
        z_ref[d, 0:S5_PP, hs] = cr * pr - ci * pi
        z_ref[d, S5_PP:2 * S5_PP, hs] = -(cr * pi + ci * pr)
        ys_ref[d, 0:S5_PP, hs] = (cr * qr - ci * qi).astype(BF16)
        ys_ref[d, S5_PP:2 * S5_PP, hs] = (-(cr * qi + ci * qr)).astype(BF16)
    b_hi, b_mid, _ = _split3(jnp.concatenate([bbr, bbi], axis=1))
    z_hi, z_mid, _ = _split3(z_ref[d])
    krow = _dot(b_hi, z_hi) + (_dot(b_hi, z_mid) + _dot(b_mid, z_hi))
    shift, keep = (0, lane >= sub) if fwd else (1, lane <= sub)
    for hp in range(S5_HG):
        for h in range(S5_HG):
            tile = jnp.broadcast_to(krow[hp:hp + 1, h * t:(h + 1) * t], (t, t))
            tile = pltpu.roll(tile, shift, 1, stride=1, stride_axis=0)
            m_ref[d, hp * t:(hp + 1) * t, h * t:(h + 1) * t] = jnp.where(keep, tile, 0.0).astype(BF16)
    tr, ti = _pow_table(lam_rr, lam_ir, t - 1 - sub if fwd else sub)
    for hp in range(S5_HG):
        br, bi = bbr[hp:hp + 1], bbi[hp:hp + 1]
        w_ref[d, hp * t:(hp + 1) * t, 0:S5_PP] = (tr * br - ti * bi).astype(BF16)
        w_ref[d, hp * t:(hp + 1) * t, S5_PP:2 * S5_PP] = (tr * bi + ti * br).astype(BF16)
    wv = _dot(u, w_ref[d])
    wr_ref[d] = wv[:, 0:S5_PP]
    wi_ref[d] = wv[:, S5_PP:2 * S5_PP]
    lr, li = lam_rr, lam_ir
    for _ in range(t.bit_length() - 1):
        lr, li = _cmul(lr, li, lr, li)
    return lr, li


def _s5_kernel(u_ref, col_ref, row_ref, o_ref, m_ref, w_ref, ys_ref, z_ref, wr_ref, wi_ref, xr_ref, xi_ref, *,
               n_batch, n_ctx, n_lat):
    t = S5_T
    u = jnp.concatenate([u_ref[hp] for hp in range(S5_HG)], axis=1)
    lam_t = [_s5_operators(d, u, col_ref, row_ref, m_ref, w_ref, ys_ref, z_ref, wr_ref, wi_ref) for d in (0, 1)]

    def make_step(base, stride, count):
        def step(i, carry):
            out = []
            for d in (0, 1):
                xr, xi = carry[2 * d], carry[2 * d + 1]
                j = i if d == 0 else count - 1 - i
                idx = pl.ds(base + j, n_batch, stride=stride)
                xr_ref[d, idx, :] = xr
                xi_ref[d, idx, :] = xi
                nr, ni = _cmul(xr, xi, *lam_t[d])
                out += [nr + wr_ref[d, idx, :], ni + wi_ref[d, idx, :]]
            return tuple(out)
        return step

    zero = jnp.zeros((n_batch, S5_PP), F32)
    carry = lax.fori_loop(0, n_ctx, make_step(n_batch * n_lat, n_ctx, n_ctx), (zero,) * 4)
    lax.fori_loop(0, n_lat, make_step(0, n_lat, n_lat), carry)
    y = None
    for d in (0, 1):
        x_prev = jnp.concatenate([xr_ref[d], xi_ref[d]], axis=1).astype(BF16)
        y_d = _dot(u, m_ref[d]) + _dot(x_prev, ys_ref[d])
        y = y_d if y is None else y + y_d
    for h in range(S5_HG):
        o_ref[h] = y[:, h * t:(h + 1) * t]


def _s5_params(a_re, a_im, log_dt, b_re, b_im, c_re, c_im):
    a_re, a_im = a_re.astype(F32), a_im.astype(F32)
    dt = jnp.exp(log_dt.astype(F32))[..., None]
    mag = jnp.exp(a_re * dt)
    lam_re, lam_im = mag * jnp.cos(a_im * dt), mag * jnp.sin(a_im * dt)
    den = a_re * a_re + a_im * a_im
    num_re, num_im = lam_re - 1.0, lam_im
    fr = (num_re * a_re + num_im * a_im) / den
    fi = (num_im * a_re - num_re * a_im) / den
    b_re, b_im = b_re.astype(F32), b_im.astype(F32)
    bb_re = fr[..., None] * b_re - fi[..., None] * b_im
    bb_im = fr[..., None] * b_im + fi[..., None] * b_re
    c_re_t = jnp.swapaxes(c_re.astype(F32), -1, -2)
    c_im_t = jnp.swapaxes(c_im.astype(F32), -1, -2)
    col = jnp.concatenate([lam_re[..., None], lam_im[..., None], c_re_t, c_im_t], axis=-1)
    col = jnp.pad(col, ((0, 0),) * 3 + ((0, S5_PP - S5_P), (0, 128 - col.shape[-1])))
    row = jnp.concatenate([jnp.swapaxes(bb_re, -1, -2), jnp.swapaxes(bb_im, -1, -2),
                           lam_re[..., None, :], lam_im[..., None, :]], axis=-2)
    row = jnp.pad(row, ((0, 0),) * 3 + ((0, S5_ROWS - row.shape[-2]), (0, S5_PP - S5_P)))
    return col, row


def _s5_scan(ut, col, row, layer, *, n_batch, seq, ctx_len):
    t = S5_T
    n = ut.shape[1]
    nc = n // t
    assert ctx_len % t == 0 and seq % t == 0, (ctx_len, seq, t)
    n_ctx, n_lat = ctx_len // t, seq // t
    ut = ut.reshape(S5_G, S5_HG, nc, t)
    kern = functools.partial(_s5_kernel, n_batch=n_batch, n_ctx=n_ctx, n_lat=n_lat)
    yt = pl.pallas_call(
        kern,
        out_shape=jax.ShapeDtypeStruct((S5_G, S5_HG, nc, t), F32),
        grid=(S5_G,),
        in_specs=[pl.BlockSpec((None, S5_HG, nc, t), lambda g: (g, 0, 0, 0)),
                  pl.BlockSpec((None, 2, None, S5_PP, 128), lambda g: (layer, 0, g, 0, 0)),
                  pl.BlockSpec((None, 2, None, S5_ROWS, S5_PP), lambda g: (layer, 0, g, 0, 0))],
        out_specs=pl.BlockSpec((None, S5_HG, nc, t), lambda g: (g, 0, 0, 0)),
        scratch_shapes=[pltpu.VMEM((2, S5_HG * t, S5_HG * t), BF16),
                        pltpu.VMEM((2, S5_HG * t, 2 * S5_PP), BF16),
                        pltpu.VMEM((2, 2 * S5_PP, S5_HG * t), BF16),
                        pltpu.VMEM((2, 2 * S5_PP, S5_HG * t), F32),
                        pltpu.VMEM((2, nc, S5_PP), F32), pltpu.VMEM((2, nc, S5_PP), F32),
                        pltpu.VMEM((2, nc, S5_PP), F32), pltpu.VMEM((2, nc, S5_PP), F32)],
        compiler_params=_cparams(("parallel",)),
        name="s5_scan",
    )(ut, col, row)
    return yt.reshape(BR, n)


def _head_rms(x, g):
    parts = []
    for h in range(NH):
        xh = x[:, h * HD:(h + 1) * HD]
        parts.append(xh * lax.rsqrt(jnp.mean(jnp.square(xh), axis=-1, keepdims=True) + NORM_EPS))
    return jnp.concatenate(parts, axis=-1) * g


def _finish_kernel(raf_ref, rab_ref, rbf_ref, rbb_ref, rc_ref, rdf_ref, rdb_ref, ga_ref, gb_ref, u_ref, gd_ref,
                   mg_ref, x_ref, mod_ref, vec_ref, glu_w_ref, wb_ref, wo_ref, o_ref):
    vec = vec_ref[...]
    oa = _head_rms(raf_ref[...] + rab_ref[...], vec[0:1]) * _silu(ga_ref[...].astype(F32))
    ob = _head_rms(rbf_ref[...] + rbb_ref[...], vec[1:2]) * _silu(gb_ref[...].astype(F32))
    od = _head_rms(rdf_ref[...] + rdb_ref[...], vec[2:3]) * _silu(gd_ref[...].astype(F32))
    yc = rc_ref[...].T + vec[3:4] * u_ref[...].astype(F32)
    yc = 0.5 * yc * (1.0 + lax.erf(yc * (2.0 ** -0.5)))
    oc = yc * _sigmoid(_dot(yc.astype(BF16), glu_w_ref[...]) + vec[4:5])
    y2 = None
    for j, o in enumerate((oa, ob, oc, od)):
        gate2 = jnp.tanh(0.5 * mg_ref[:, j * D_MODEL:(j + 1) * D_MODEL].astype(F32)) + 1.0
        term = gate2 * _dot(o.astype(BF16), wb_ref[j])
        y2 = term if y2 is None else y2 + term
    mix2 = _dot(y2.astype(BF16), wo_ref[...])
    o_ref[...] = x_ref[...] + (0.5 * mod_ref[2:3, :]) * mix2


def _finish(raw_a, raw_b, raw_c, raw_d, p, x_all, mod, vec, glu_w, w_branch, w_out, layer, *, n_rows, n_batch,
            seq, tm=256):
    mrow = _mod_row_map(n_rows, tm, n_batch, seq)
    one = pl.Buffered(1)

    def pcol(cblock):
        return pl.BlockSpec((tm, BR), lambda i: (i, cblock))

    raw = pl.BlockSpec((tm, BR), lambda i: (i, 0))
    return pl.pallas_call(
        _finish_kernel,
        out_shape=jax.ShapeDtypeStruct((n_rows, D_MODEL), F32),
        grid=(n_rows // tm,),
        in_specs=[raw, raw, raw, raw, pl.BlockSpec((BR, tm), lambda i: (0, i)), raw, raw,
                  pcol(COL_HG), pcol(COL_RG), pcol(COL_SU), pcol(COL_MZ),
                  pl.BlockSpec((tm, N_BRANCH * D_MODEL), lambda i: (i, COL_MERGE)),
                  pl.BlockSpec((tm, D_MODEL), lambda i: (i, 0)),
                  pl.BlockSpec((None, 8, D_MODEL), lambda i: (mrow(i), 0, 0)),
                  pl.BlockSpec((None, 8, BR), lambda i: (layer, 0, 0), pipeline_mode=one),
                  pl.BlockSpec((None, BR, BR), lambda i: (layer, 0, 0), pipeline_mode=one),
                  pl.BlockSpec((None, N_BRANCH, BR, D_MODEL), lambda i: (layer, 0, 0, 0), pipeline_mode=one),
                  pl.BlockSpec((None, D_MODEL, D_MODEL), lambda i: (layer, 0, 0), pipeline_mode=one)],
        out_specs=pl.BlockSpec((tm, D_MODEL), lambda i: (i, 0)),
        compiler_params=_cparams(("parallel",)),
        name="finish",
    )(*raw_a, *raw_b, raw_c, *raw_d, p, p, p, p, p, x_all, mod, vec, glu_w, w_branch, w_out)


def _mlp_kernel(x_ref, mod_ref, g_ref, w1_ref, w2_ref, fin_ref, o_ref, xn_ref, acc_ref, *, final_norm):
    j = pl.program_id(1)

    @pl.when(j == 0)
    def _():
        h = _norm_mod(x_ref[...], g_ref[...], mod_ref[3:4, :], mod_ref[4:5, :])
        xn_ref[...] = h.astype(BF16)
        acc_ref[...] = jnp.zeros_like(acc_ref)

    a = jnp.square(jnp.maximum(_dot(xn_ref[...], w1_ref[...]), 0.0))
    acc_ref[...] += _dot(a.astype(BF16), w2_ref[...])

    @pl.when(j == pl.num_programs(1) - 1)
    def _():
        y = x_ref[...] + mod_ref[5:6, :] * acc_ref[...]
        if final_norm:
            y = (y * lax.rsqrt(jnp.mean(jnp.square(y), axis=-1, keepdims=True) + NORM_EPS)) * fin_ref[...]
        o_ref[...] = y


def _mlp(x_all, mod, norm_w, w1, w2, fin_w, layer, *, n_rows, n_batch, seq, final_norm, tm=512, tf=1024):
    mrow = _mod_row_map(n_rows, tm, n_batch, seq)
    assert D_FF % tf == 0, tf
    kern = functools.partial(_mlp_kernel, final_norm=final_norm)
    return pl.pallas_call(
        kern,
        out_shape=jax.ShapeDtypeStruct((n_rows, D_MODEL), F32),
        grid=(n_rows // tm, D_FF // tf),
        in_specs=[pl.BlockSpec((tm, D_MODEL), lambda i, j: (i, 0)),
                  pl.BlockSpec((None, 8, D_MODEL), lambda i, j: (mrow(i), 0, 0)),
                  pl.BlockSpec((None, 1, D_MODEL), lambda i, j: (layer, 0, 0)),
                  pl.BlockSpec((None, D_MODEL, tf), lambda i, j: (layer, 0, j)),
                  pl.BlockSpec((None, tf, D_MODEL), lambda i, j: (layer, j, 0)),
                  pl.BlockSpec((1, D_MODEL), lambda i, j: (0, 0))],
        out_specs=pl.BlockSpec((tm, D_MODEL), lambda i, j: (i, 0)),
        scratch_shapes=[pltpu.VMEM((tm, D_MODEL), BF16), pltpu.VMEM((tm, D_MODEL), F32)],
        compiler_params=_cparams(("parallel", "arbitrary")),
        name="mlp",
    )(x_all, mod, norm_w, w1, w2, fin_w.reshape(1, D_MODEL))


W_PREP_ROWS = 1024
N_MERGE_BLK = N_BRANCH * D_MODEL // W_PREP_ROWS
MERGE_OFF = GATE_OFF + 4 * NH


def _w_prep_kernel(a_ref, g_ref, main_ref, gate_ref):
    @pl.when(pl.program_id(1) == 0)
    def _():
        gate_ref[...] = jnp.zeros_like(gate_ref)
        gate_ref[0:4 * NH, :] = g_ref[0].astype(BF16)

    main_ref[...] = a_ref[0].astype(BF16)


def _split_w_in(w):
    depth = w.shape[0]
    wt = jnp.swapaxes(w, 1, 2)
    tr = W_PREP_ROWS

    def a_map(l, j):
        row = jnp.where(j < N_MERGE_BLK, MERGE_OFF + j * tr, (j - N_MERGE_BLK) * tr)
        return l, pl.multiple_of(row, 16), 0

    return pl.pallas_call(
        _w_prep_kernel,
        out_shape=(jax.ShapeDtypeStruct((depth, P_WIDTH, D_MODEL), BF16),
                   jax.ShapeDtypeStruct((depth, GATE_PAD, D_MODEL), BF16)),
        grid=(depth, P_WIDTH // tr),
        in_specs=[pl.BlockSpec((pl.Element(1), pl.Element(tr), pl.Element(D_MODEL)), a_map),
                  pl.BlockSpec((pl.Element(1), pl.Element(4 * NH), pl.Element(D_MODEL)),
                               lambda l, j: (l, GATE_OFF, 0))],
        out_specs=(pl.BlockSpec((None, tr, D_MODEL), lambda l, j: (l, j, 0)),
                   pl.BlockSpec((None, GATE_PAD, D_MODEL), lambda l, j: (l, 0, 0))),
        compiler_params=_cparams(("parallel", "arbitrary")),
        name="w_in_prep",
    )(wt, wt)


def kernel(x, c, ctx, c_ctx, w_mod, b_mod, norm_mix, norm_mlp, w_in, hgrn_lb_logits, hgrn_norm, ret_decay, ret_norm, s5_a_re, s5_a_im, s5_log_dt, s5_b_re, s5_b_im, s5_c_re, s5_c_im, s5_d, s5_glu_w, s5_glu_b, mlstm_conv_w, mlstm_conv_b, mlstm_gate_b, mlstm_norm, w_branch, w_out, w_ff1, w_ff2, final_norm):
    n_batch, seq, _ = x.shape
    ctx_len = ctx.shape[1]
    depth = w_in.shape[0]
    nl = n_batch * seq
    dims = dict(n_batch=n_batch, seq=seq)

    p_lb = jax.nn.softmax(hgrn_lb_logits.astype(F32), axis=0)
    lower_bounds = (jnp.cumsum(p_lb, axis=0) - p_lb[0]).reshape(depth, 2, 1, BR)
    lg = jnp.log1p(-jnp.exp(ret_decay.astype(F32)))
    gate_b = jnp.pad(mlstm_gate_b.reshape(depth, 1, 4 * NH).astype(F32), ((0, 0), (0, 0), (0, GATE_PAD - 4 * NH)))
    vec = jnp.stack([hgrn_norm, ret_norm, mlstm_norm, s5_d, s5_glu_b], axis=1).astype(F32)
    vec = jnp.pad(vec, ((0, 0), (0, 8 - vec.shape[1]), (0, 0)))
    w_main, w_gate = _split_w_in(w_in)
    glu_w, wb, wo = s5_glu_w.astype(BF16), w_branch.astype(BF16), w_out.astype(BF16)
    w1, w2 = w_ff1.astype(BF16), w_ff2.astype(BF16)
    norm_mix3 = norm_mix.reshape(depth, 1, D_MODEL)
    norm_mlp3 = norm_mlp.reshape(depth, 1, D_MODEL)
    conv_w = mlstm_conv_w.reshape(depth, 9, 2 * BR)
    conv_b = mlstm_conv_b.reshape(depth, 1, 2 * BR)
    s5_col, s5_row = _s5_params(s5_a_re, s5_a_im, s5_log_dt, s5_b_re, s5_b_im, s5_c_re, s5_c_im)

    c_pad = jnp.zeros((8, D_MODEL), F32).at[:n_batch].set(c).at[n_batch].set(c_ctx)
    x_all = jnp.concatenate([x.reshape(nl, D_MODEL), ctx.reshape(n_batch * ctx_len, D_MODEL)], axis=0)

    mod_all = _modulation(c_pad, w_mod, b_mod)[:, :n_batch + 1].reshape(depth, n_batch + 1, N_MOD, D_MODEL)
    mod_all = jnp.pad(mod_all, ((0, 0), (0, 0), (0, 8 - N_MOD), (0, 0)))
    for l in range(depth):
        last = l == depth - 1
        mod = mod_all[l]
        p, gates, ut = _in_proj(x_all, mod, norm_mix3, w_main, w_gate, l, **dims)
        qk = _mlstm_conv(p, conv_w, conv_b, l, ctx_len=ctx_len, **dims)
        raw_a, raw_b, raw_d = _scans(p, gates, qk, lg[l], lower_bounds, gate_b, l, ctx_len=ctx_len, **dims)
        raw_c = _s5_scan(ut, s5_col, s5_row, l, ctx_len=ctx_len, **dims)
        n_rows = nl if last else x_all.shape[0]
        x_mid = _finish(raw_a, raw_b, raw_c, raw_d, p, x_all, mod, vec, glu_w, wb, wo, l, n_rows=n_rows, **dims)
        x_all = _mlp(x_mid, mod, norm_mlp3, w1, w2, final_norm, l, n_rows=n_rows, final_norm=last, **dims)
    return x_all.reshape(n_batch, seq, D_MODEL)
```

```python
import functools

import jax
import jax.numpy as jnp
from jax import lax
from jax.experimental import pallas as pl
from jax.experimental.pallas import tpu as pltpu

F32 = jnp.float32
BF16 = jnp.bfloat16

D_MODEL = 2048
N_BRANCH = 4
BR = D_MODEL // N_BRANCH
HD = 128
NH = BR // HD
S5_HG = 16
S5_G = BR // S5_HG
S5_P = 64
D_FF = 4 * D_MODEL
N_MOD = 6
GRID_W = 64
NORM_EPS = 1e-6
NEG_BIG = -1e30
F_TINY = 1e-30

COL_MERGE = 0
COL_HQ, COL_HF, COL_HV, COL_HG = 16, 17, 19, 20
COL_RQ, COL_RK, COL_RV, COL_RG = 21, 22, 23, 24
COL_SU = 25
COL_MQK, COL_MV, COL_MZ = 26, 28, 29
P_WIDTH = 30 * BR
GATE_OFF = 14 * BR
GATE_PAD = 128
SUBLANES = 8

T_HGRN = 128
T_ATTN = 256
S5_T = 128
S5_PP = 128
S5_ROWS = 40
VMEM_LIMIT = 56 * 1024 * 1024

def _cparams(sem):
    return pltpu.CompilerParams(dimension_semantics=sem, vmem_limit_bytes=VMEM_LIMIT)


def _dot(a, b):
    return jnp.dot(a, b, preferred_element_type=F32)


def _dot_nt(a, b):
    return lax.dot_general(a, b, (((1,), (1,)), ((), ())), preferred_element_type=F32)


def _dot_tn(a, b):
    return lax.dot_general(a, b, (((0,), (0,)), ((), ())), preferred_element_type=F32)


def _split3(x):
    hi = x.astype(BF16)
    r1 = x - hi.astype(F32)
    mid = r1.astype(BF16)
    return hi, mid, (r1 - mid.astype(F32)).astype(BF16)


def _split2(x):
    hi = x.astype(BF16)
    return hi, (x - hi.astype(F32)).astype(BF16)


def _cumsum_rows(tri, x):
    hi, mid = _split2(x)
    return _dot(tri, hi) + _dot(tri, mid)


def _sigmoid(x):
    return 1.0 / (1.0 + jnp.exp(-x))


def _silu(x):
    return x * _sigmoid(x)


def _mod_kernel(c_ref, w_ref, b_ref, o_ref):
    @pl.when(pl.program_id(1) == 0)
    def _():
        o_ref[...] = jnp.broadcast_to(b_ref[...], o_ref.shape)

    a = _silu(c_ref[...]).astype(BF16)
    o_ref[...] += _dot(a, w_ref[...].astype(BF16))


def _modulation(c_pad, w_mod, b_mod, tk=256):
    depth, _, n = w_mod.shape
    return pl.pallas_call(
        _mod_kernel,
        out_shape=jax.ShapeDtypeStruct((depth, SUBLANES, n), F32),
        grid=(depth, D_MODEL // tk),
        in_specs=[pl.BlockSpec((SUBLANES, tk), lambda l, k: (0, k)),
                  pl.BlockSpec((None, tk, n), lambda l, k: (l, k, 0)),
                  pl.BlockSpec((None, 1, n), lambda l, k: (l, 0, 0))],
        out_specs=pl.BlockSpec((None, SUBLANES, n), lambda l, k: (l, 0, 0)),
        compiler_params=_cparams(("parallel", "arbitrary")),
        name="modulation",
    )(c_pad, w_mod, b_mod.reshape(depth, 1, n))


def _norm_mod(x, g, shift, scale):
    y = x * lax.rsqrt(jnp.mean(jnp.square(x), axis=-1, keepdims=True) + NORM_EPS)
    return (y * g) * (1.0 + scale) + shift


def _inproj_kernel(x_ref, mod_ref, g_ref, w_ref, wg_ref, p_ref, gate_ref, ut_ref, xn_ref, *, su_block, su_off):
    @pl.when(pl.program_id(1) == 0)
    def _():
        h = _norm_mod(x_ref[...], g_ref[...], mod_ref[0:1, :], mod_ref[1:2, :])
        xn_ref[...] = h.astype(BF16)
        gate_ref[...] = _dot_nt(xn_ref[...], wg_ref[...])

    acc = _dot_nt(xn_ref[...], w_ref[...])
    p_ref[...] = acc.astype(p_ref.dtype)

    @pl.when(pl.program_id(1) == su_block)
    def _():
        ut_ref[...] = acc[:, su_off:su_off + BR].T.astype(ut_ref.dtype)


def _mod_row_map(n_rows, tm, n_batch, seq):
    assert n_rows % tm == 0 and seq % tm == 0, (n_rows, seq, tm)
    n_lat_tiles, tiles_per_batch = n_batch * seq // tm, seq // tm

    def row(i):
        return jnp.where(i < n_lat_tiles, i // tiles_per_batch, n_batch)
    return row


def _in_proj(x_all, mod, norm_w, w_main, w_gate, layer, *, n_batch, seq, tm=1024, tn=1536):
    n = x_all.shape[0]
    assert P_WIDTH % tn == 0, tn
    su_block, su_off = divmod(COL_SU * BR, tn)
    assert su_off + BR <= tn, (su_off, tn)
    mrow = _mod_row_map(n, tm, n_batch, seq)
    return pl.pallas_call(
        functools.partial(_inproj_kernel, su_block=su_block, su_off=su_off),
        out_shape=(jax.ShapeDtypeStruct((n, P_WIDTH), BF16),
                   jax.ShapeDtypeStruct((n, GATE_PAD), F32),
                   jax.ShapeDtypeStruct((BR, n), BF16)),
        grid=(n // tm, P_WIDTH // tn),
        in_specs=[pl.BlockSpec((tm, D_MODEL), lambda i, j: (i, 0)),
                  pl.BlockSpec((None, SUBLANES, D_MODEL), lambda i, j: (mrow(i), 0, 0)),
                  pl.BlockSpec((None, 1, D_MODEL), lambda i, j: (layer, 0, 0)),
                  pl.BlockSpec((None, tn, D_MODEL), lambda i, j: (layer, j, 0)),
                  pl.BlockSpec((None, GATE_PAD, D_MODEL), lambda i, j: (layer, 0, 0))],
        out_specs=(pl.BlockSpec((tm, tn), lambda i, j: (i, j)),
                   pl.BlockSpec((tm, GATE_PAD), lambda i, j: (i, 0)),
                   pl.BlockSpec((BR, tm), lambda i, j: (0, i))),
        scratch_shapes=[pltpu.VMEM((tm, D_MODEL), BF16)],
        compiler_params=_cparams(("parallel", "arbitrary")),
        name="in_proj",
    )(x_all, mod, norm_w, w_main, w_gate)


CONV_ROWS = 1024


def _conv_kernel(up_ref, x_ref, dn_ref, w_ref, b_ref, o_ref, *, n_lat_blocks, blocks_per_img, ctx_len):
    rb = pl.program_id(0)
    x = x_ref[...].astype(F32)
    w = w_ref[...]
    n = CONV_ROWS

    @pl.when(rb < n_lat_blocks)
    def _():
        rr = rb % blocks_per_img
        up = jnp.where(rr > 0, up_ref[...].astype(F32), 0.0)
        dn = jnp.where(rr < blocks_per_img - 1, dn_ref[...].astype(F32), 0.0)
        ext = jnp.concatenate([up, x, dn], axis=0)
        ne = n + 2 * GRID_W
        col = lax.broadcasted_iota(jnp.int32, (ne, 1), 0) % GRID_W
        xl = jnp.where(col >= 1, pltpu.roll(ext, 1, 0), 0.0)
        xr = jnp.where(col <= GRID_W - 2, pltpu.roll(ext, ne - 1, 0), 0.0)
        acc = None
        for i in range(3):
            y = w[3 * i:3 * i + 1] * xl + w[3 * i + 1:3 * i + 2] * ext + w[3 * i + 2:3 * i + 3] * xr
            part = y[i * GRID_W:i * GRID_W + n]
            acc = part if acc is None else acc + part
        o_ref[...] = _silu(acc + b_ref[...]).astype(o_ref.dtype)

    @pl.when(rb >= n_lat_blocks)
    def _():
        t = lax.broadcasted_iota(jnp.int32, (n, 1), 0) % ctx_len
        xl = jnp.where(t >= 1, pltpu.roll(x, 1, 0), 0.0)
        xr = jnp.where(t <= ctx_len - 2, pltpu.roll(x, n - 1, 0), 0.0)
        acc = w[3:4] * xl + w[4:5] * x + w[5:6] * xr
        o_ref[...] = _silu(acc + b_ref[...]).astype(o_ref.dtype)


def _mlstm_conv(p, conv_w, conv_b, layer, *, n_batch, seq, ctx_len, cb=2 * BR):
    assert CONV_ROWS % ctx_len == 0 and seq % CONV_ROWS == 0 and (n_batch * ctx_len) % CONV_ROWS == 0
    n = p.shape[0]
    n_blocks = n // CONV_ROWS
    hb = CONV_ROWS // GRID_W
    n_halo = n // GRID_W
    c0 = COL_MQK * BR // cb
    kern = functools.partial(_conv_kernel, n_lat_blocks=n_batch * seq // CONV_ROWS,
                             blocks_per_img=seq // CONV_ROWS, ctx_len=ctx_len)
    return pl.pallas_call(
        kern,
        out_shape=jax.ShapeDtypeStruct((n, 2 * BR), BF16),
        grid=(n_blocks, 2 * BR // cb),
        in_specs=[pl.BlockSpec((GRID_W, cb), lambda r, c: (jnp.maximum(r * hb - 1, 0), c0 + c)),
                  pl.BlockSpec((CONV_ROWS, cb), lambda r, c: (r, c0 + c)),
                  pl.BlockSpec((GRID_W, cb), lambda r, c: (jnp.minimum((r + 1) * hb, n_halo - 1), c0 + c)),
                  pl.BlockSpec((None, 9, cb), lambda r, c: (layer, 0, c)),
                  pl.BlockSpec((None, 1, cb), lambda r, c: (layer, 0, c))],
        out_specs=pl.BlockSpec((CONV_ROWS, cb), lambda r, c: (r, c)),
        compiler_params=_cparams(("parallel", "parallel")),
        name="mlstm_conv",
    )(p, p, p, conv_w, conv_b)


def _chunk_block(b, d, i, *, n_ctx, n_lat, n_batch):
    ctx_j = i if d == 0 else n_ctx - 1 - i
    lat_j = i - n_ctx if d == 0 else n_ctx + n_lat - 1 - i
    return jnp.where(i < n_ctx, n_batch * n_lat + b * n_ctx + ctx_j, b * n_lat + lat_j)


def _init_at_first_step(*refs):
    @pl.when(pl.program_id(1) == 0)
    def _():
        for ref in refs:
            ref[...] = jnp.zeros_like(ref)


def _ret_direction(d, lg_ref, q_ref, k_ref, v_ref, o_ref, s_ref):
    t = T_ATTN
    r = lax.broadcasted_iota(jnp.int32, (t, t), 0)
    c = lax.broadcasted_iota(jnp.int32, (t, t), 1)
    rel = (r - c) if d == 0 else (c - r)
    relf = jnp.maximum(rel, 0).astype(F32)
    tt = lax.broadcasted_iota(jnp.int32, (t, 1), 0)
    pos = (tt if d == 0 else t - 1 - tt).astype(F32)
    scale = HD ** -0.5
    for h in range(NH):
        sl = slice(h * HD, (h + 1) * HD)
        lg = lg_ref[d, h]
        q = q_ref[:, sl]
        k = k_ref[:, sl]
        v = v_ref[:, sl]
        intra = jnp.where(rel >= 0, jnp.exp(lg * relf), 0.0)
        scores = (_dot_nt(q, k) * scale) * intra
        s_old = s_ref[d, h]
        out = _dot(scores.astype(BF16), v) + jnp.exp(lg * (pos + 1.0)) * _dot(q, s_old.astype(BF16))
        kd = (k.astype(F32) * (scale * jnp.exp(lg * (t - 1.0 - pos)))).astype(BF16)
        s_ref[d, h] = jnp.exp(lg * t) * s_old + _dot_tn(kd, v)
        o_ref[:, sl] = out


def _hgrn_direction(d, lb_ref, q_ref, z_ref, v_ref, o_ref, st_ref):
    t = T_HGRN
    fwd = d == 0
    lb = lb_ref[...]
    z = z_ref[...].astype(F32)
    sig = _sigmoid(z)
    f = lb + (1.0 - lb) * sig
    logf = jnp.log(jnp.maximum(f, F_TINY))
    kk = (1.0 - lb) * (1.0 - sig)
    qq = _silu(q_ref[...].astype(F32))
    r2 = lax.broadcasted_iota(jnp.int32, (t, t), 0)
    c2 = lax.broadcasted_iota(jnp.int32, (t, t), 1)
    tri = jnp.where((c2 <= r2) if fwd else (c2 >= r2), 1.0, 0.0).astype(BF16)
    cum = _cumsum_rows(tri, logf)
    xdiff = jnp.where((c2 < r2) if fwd else (c2 > r2), r2 ^ c2, 0)
    row = lax.broadcasted_iota(jnp.int32, (t, 1), 0)
    if fwd:
        e_ref = jnp.where(row == 0, 0.0, pltpu.roll(cum, 1, 0))
    else:
        e_ref = jnp.where(row == t - 1, 0.0, pltpu.roll(cum, t - 1, 0))
    f_ref = cum
    levels = []
    w = 1
    while w < t:
        if w == 1:
            q_l, k_l = (qq * jnp.maximum(f, F_TINY)).astype(BF16), kk.astype(BF16)
        else:
            q_l = (qq * jnp.exp(cum - e_ref)).astype(BF16)
            k_l = (kk * jnp.exp(f_ref - cum)).astype(BF16)
        levels.append((q_l, k_l, (xdiff >> (w.bit_length() - 1)) == 1))
        upper = (row & w) != 0
        if fwd:
            e_ref = jnp.where(upper, pltpu.roll(e_ref, w, 0), e_ref)
            f_ref = jnp.where(upper, f_ref, pltpu.roll(f_ref, t - w, 0))
        else:
            e_ref = jnp.where(upper, e_ref, pltpu.roll(e_ref, t - w, 0))
            f_ref = jnp.where(upper, pltpu.roll(f_ref, w, 0), f_ref)
        w *= 2
    q_b = qq.astype(BF16)
    k_b = kk.astype(BF16)
    tail = cum[t - 1:t] if fwd else cum[0:1]
    q_state = (qq * jnp.exp(cum)).astype(BF16)
    k_state = (kk * jnp.exp(tail - cum)).astype(BF16)
    decay = jnp.exp(tail)
    for h in range(NH):
        sl = slice(h * HD, (h + 1) * HD)
        v_h = v_ref[:, sl]
        scores = jnp.where(r2 == c2, _dot_nt(q_b[:, sl], k_b[:, sl]), 0.0)
        for q_l, k_l, m in levels:
            scores = jnp.where(m, _dot_nt(q_l[:, sl], k_l[:, sl]), scores)
        st = st_ref[d, h]
        o_ref[:, sl] = _dot(scores.astype(BF16), v_h) + _dot_nt(q_state[:, sl], st.astype(BF16))
        st_ref[d, h] = st * decay[:, sl] + _dot_tn(v_h, k_state[:, sl])


def _hgrn_kernel(lbf_ref, qf_ref, zf_ref, vf_ref, lbb_ref, qb_ref, zb_ref, vb_ref, of_ref, ob_ref, st_ref):
    _init_at_first_step(st_ref)
    _hgrn_direction(0, lbf_ref, qf_ref, zf_ref, vf_ref, of_ref, st_ref)
    _hgrn_direction(1, lbb_ref, qb_ref, zb_ref, vb_ref, ob_ref, st_ref)


def _log_sigmoid(x):
    return jnp.minimum(x, 0.0) - jnp.log(1.0 + jnp.exp(-jnp.abs(x)))


def _rep_lane(parts, lane, width):
    sel = jnp.where(lax.broadcasted_iota(jnp.int32, (GATE_PAD, width), 0) == lane, 1.0, 0.0).astype(BF16)
    hi, mid = parts
    return _dot(hi, sel) + _dot(mid, sel)


def _mlstm_direction(d, qk_ref, v_ref, g_ref, gb_ref, o_ref, c_ref, m_ref):
    t = T_ATTN
    assert t == 2 * HD
    fwd = d == 0
    g = g_ref[...] + gb_ref[...]
    r2 = lax.broadcasted_iota(jnp.int32, (t, t), 0)
    c2 = lax.broadcasted_iota(jnp.int32, (t, t), 1)
    mask = (c2 <= r2) if fwd else (c2 >= r2)
    tri = jnp.where(mask, 1.0, 0.0).astype(BF16)
    eye = jnp.where(r2 == c2, 1.0, 0.0).astype(BF16)
    lf = _log_sigmoid(g)
    cum_c = _cumsum_rows(tri, lf)
    a_col = g - pltpu.roll(cum_c, GATE_PAD - NH, 1)
    row = lax.broadcasted_iota(jnp.int32, (t, 1), 0)
    pm = a_col
    k = 1
    while k < t:
        if fwd:
            pm = jnp.maximum(pm, jnp.where(row >= k, pltpu.roll(pm, k, 0), NEG_BIG))
        else:
            pm = jnp.maximum(pm, jnp.where(row < t - k, pltpu.roll(pm, t - k, 0), NEG_BIG))
        k *= 2
    a_parts, pm_parts, cum_parts = _split2(a_col), _split2(pm), _split2(cum_c)
    a_rows = _dot_tn(a_parts[0], eye) + _dot_tn(a_parts[1], eye)
    scale = HD ** -0.5
    last = t - 1 if fwd else 0
    ones = jnp.ones((t, HD), BF16)
    for h in range(NH):
        sl = slice(h * HD, (h + 1) * HD)
        li, lfw = 2 * NH * d + h, 2 * NH * d + NH + h
        m_old = m_ref[d, h]
        mx2 = jnp.maximum(_rep_lane(pm_parts, li, 2 * HD), jnp.concatenate([m_old, m_old], axis=1))
        mx = mx2[:, 0:HD]
        cc = _rep_lane(cum_parts, lfw, HD)
        a_rep = _rep_lane(a_parts, li, HD)
        a_row = a_rows[li:li + 1]
        q = qk_ref[:, sl]
        k_h = qk_ref[:, BR + h * HD:BR + (h + 1) * HD]
        v_aug = jnp.concatenate([v_ref[:, sl], ones], axis=1)
        w = jnp.where(mask, jnp.exp(a_row - mx2), 0.0)
        scores = (_dot_nt(q, k_h) * scale) * w
        w_state = jnp.exp(m_old - mx)
        c_old = c_ref[d, h]
        s_hi = scores.astype(BF16)
        s_mid = (scores - s_hi.astype(F32)).astype(BF16)
        intra = _dot(s_hi, v_aug)
        inter = _dot(q, c_old.astype(BF16))
        num = intra[:, 0:HD] + w_state * inter[:, 0:HD]
        den = (intra[:, HD:2 * HD] + _dot(s_mid, ones)) + w_state * inter[:, HD:2 * HD]
        o_ref[:, sl] = num / jnp.maximum(jnp.abs(den), jnp.exp(-(cc + mx)))
        total = cc[last:last + 1]
        m_new_rel = mx[last:last + 1]
        keep = jnp.exp(m_old - m_new_rel)
        w_end = jnp.exp(a_rep - m_new_rel)
        kw = (k_h.astype(F32) * (scale * w_end)).astype(BF16)
        c_ref[d, h] = jnp.concatenate([keep, keep], axis=1) * c_old + _dot_tn(kw, v_aug)
        m_ref[d, h] = total + m_new_rel


def _ret_mlstm_kernel(lg_ref, rqf_ref, rkf_ref, rvf_ref, rqb_ref, rkb_ref, rvb_ref,
                      qkf_ref, vf_ref, gf_ref, qkb_ref, vb_ref, gb_ref, bias_ref,
                      rof_ref, rob_ref, mof_ref, mob_ref, s_ref, c_ref, m_ref):
    _init_at_first_step(s_ref, c_ref, m_ref)
    _ret_direction(0, lg_ref, rqf_ref, rkf_ref, rvf_ref, rof_ref, s_ref)
    _mlstm_direction(0, qkf_ref, vf_ref, gf_ref, bias_ref, mof_ref, c_ref, m_ref)
    _ret_direction(1, lg_ref, rqb_ref, rkb_ref, rvb_ref, rob_ref, s_ref)
    _mlstm_direction(1, qkb_ref, vb_ref, gb_ref, bias_ref, mob_ref, c_ref, m_ref)


def _scans(p, gates, qk, lg, lb, gate_b, layer, *, n_batch, seq, ctx_len):
    n = p.shape[0]
    state = pltpu.VMEM((2, NH, HD, HD), F32)
    vec = pltpu.VMEM((2, NH, 1, HD), F32)

    def call(kernel, t, inputs, make_specs, scratch, name, n_mixers=1):
        assert ctx_len % t == 0 and seq % t == 0, (ctx_len, seq, t)
        n_ctx, n_lat = ctx_len // t, seq // t

        def rows(d, width, cblock=0):
            return pl.BlockSpec((t, width), lambda b, i: (_chunk_block(b, d, i, n_ctx=n_ctx, n_lat=n_lat,
                                                                        n_batch=n_batch), cblock))

        out = jax.ShapeDtypeStruct((n, BR), F32)
        return pl.pallas_call(
            kernel,
            out_shape=(out, out) * n_mixers,
            grid=(n_batch, n_ctx + n_lat),
            in_specs=make_specs(rows),
            out_specs=(rows(0, BR), rows(1, BR)) * n_mixers,
            scratch_shapes=scratch,
            compiler_params=_cparams(("parallel", "arbitrary")),
            name=name,
        )(*inputs)

    lb_spec = lambda d: pl.BlockSpec((None, None, 1, BR), lambda b, i: (layer, d, 0, 0))
    raw_a = call(_hgrn_kernel, T_HGRN, (lb, p, p, p, lb, p, p, p),
                 lambda rows: [spec for d in (0, 1)
                               for spec in (lb_spec(d), rows(d, BR, COL_HQ), rows(d, BR, COL_HF + d),
                                            rows(d, BR, COL_HV))],
                 [state], "hgrn2_scan")
    raw_bd = call(_ret_mlstm_kernel, T_ATTN, (lg, p, p, p, p, p, p, qk, p, gates, qk, p, gates, gate_b),
                  lambda rows: [pl.BlockSpec(memory_space=pltpu.SMEM)]
                  + [rows(d, BR, c) for d in (0, 1) for c in (COL_RQ, COL_RK, COL_RV)]
                  + [spec for d in (0, 1)
                     for spec in (rows(d, 2 * BR), rows(d, BR, COL_MV), rows(d, GATE_PAD))]
                  + [pl.BlockSpec((None, 1, GATE_PAD), lambda b, i: (layer, 0, 0))],
                  [state, pltpu.VMEM((2, NH, HD, 2 * HD), F32), vec], "retention_mlstm_scan", n_mixers=2)
    return raw_a, raw_bd[0:2], raw_bd[2:4]


def _cmul(ar, ai, br, bi):
    return ar * br - ai * bi, ar * bi + ai * br


def _pow_table(lam_r, lam_i, expo):
    pr = jnp.ones(expo.shape, F32)
    pi = jnp.zeros(expo.shape, F32)
    ar = jnp.broadcast_to(lam_r, expo.shape)
    ai = jnp.broadcast_to(lam_i, expo.shape)
    for k in range(S5_T.bit_length() - 1):
        nr, ni = _cmul(pr, pi, ar, ai)
        bit = (expo & (1 << k)) != 0
        pr, pi = jnp.where(bit, nr, pr), jnp.where(bit, ni, pi)
        ar, ai = _cmul(ar, ai, ar, ai)
    return pr, pi


def _s5_operators(d, u, col_ref, row_ref, m_ref, w_ref, ys_ref, z_ref, wr_ref, wi_ref):
    t = S5_T
    fwd = d == 0
    col = col_ref[d]
    rowp = row_ref[d]
    lam_rc, lam_ic = col[:, 0:1], col[:, 1:2]
    bbr, bbi = rowp[0:S5_HG], rowp[S5_HG:2 * S5_HG]
    lam_rr, lam_ir = rowp[2 * S5_HG:2 * S5_HG + 1], rowp[2 * S5_HG + 1:2 * S5_HG + 2]
    lane = lax.broadcasted_iota(jnp.int32, (t, t), 1)
    sub = lax.broadcasted_iota(jnp.int32, (t, t), 0)

    pr, pi = _pow_table(lam_rc, lam_ic, lane if fwd else t - 1 - lane)
    qr, qi = _cmul(pr, pi, lam_rc, lam_ic)
    for h in range(S5_HG):
        cr, ci = col[:, 2 + h:3 + h], col[:, 2 + S5_HG + h:3 + S5_HG + h]
        hs = slice(h * t, (h + 1) * t)
        z_ref[d, 0:S5_PP, hs] = cr * pr - ci * pi
        z_ref[d, S5_PP:2 * S5_PP, hs] = -(cr * pi + ci * pr)
        ys_ref[d, 0:S5_PP, hs] = (cr * qr - ci * qi).astype(BF16)
        ys_ref[d, S5_PP:2 * S5_PP, hs] = (-(cr * qi + ci * qr)).astype(BF16)
    b_hi, b_mid, _ = _split3(jnp.concatenate([bbr, bbi], axis=1))
    z_hi, z_mid, _ = _split3(z_ref[d])
    krow = _dot(b_hi, z_hi) + (_dot(b_hi, z_mid) + _dot(b_mid, z_hi))
    shift, keep = (0, lane >= sub) if fwd else (1, lane <= sub)
    for hp in range(S5_HG):
        for h in range(S5_HG):
            tile = jnp.broadcast_to(krow[hp:hp + 1, h * t:(h + 1) * t], (t, t))
            tile = pltpu.roll(tile, shift, 1, stride=1, stride_axis=0)
            m_ref[d, hp * t:(hp + 1) * t, h * t:(h + 1) * t] = jnp.where(keep, tile, 0.0).astype(BF16)
    tr, ti = _pow_table(lam_rr, lam_ir, t - 1 - sub if fwd else sub)
    for hp in range(S5_HG):
        br, bi = bbr[hp:hp + 1], bbi[hp:hp + 1]
        w_ref[d, hp * t:(hp + 1) * t, 0:S5_PP] = (tr * br - ti * bi).astype(BF16)
        w_ref[d, hp * t:(hp + 1) * t, S5_PP:2 * S5_PP] = (tr * bi + ti * br).astype(BF16)
    wv = _dot(u, w_ref[d])
    wr_ref[d] = wv[:, 0:S5_PP]
    wi_ref[d] = wv[:, S5_PP:2 * S5_PP]
    lr, li = lam_rr, lam_ir
    for _ in range(t.bit_length() - 1):
        lr, li = _cmul(lr, li, lr, li)
    return lr, li


def _s5_kernel(u_ref, col_ref, row_ref, o_ref, m_ref, w_ref, ys_ref, z_ref, wr_ref, wi_ref, xr_ref, xi_ref, *,
               n_batch, n_ctx, n_lat):
    t = S5_T
    u = jnp.concatenate([u_ref[hp] for hp in range(S5_HG)], axis=1)
    lam_t = [_s5_operators(d, u, col_ref, row_ref, m_ref, w_ref, ys_ref, z_ref, wr_ref, wi_ref) for d in (0, 1)]

    def make_step(base, stride, count):
        def step(i, carry):
            out = []
            for d in (0, 1):
                xr, xi = carry[2 * d], carry[2 * d + 1]
                j = i if d == 0 else count - 1 - i
                idx = pl.ds(base + j, n_batch, stride=stride)
                xr_ref[d, idx, :] = xr
                xi_ref[d, idx, :] = xi
                nr, ni = _cmul(xr, xi, *lam_t[d])
                out += [nr + wr_ref[d, idx, :], ni + wi_ref[d, idx, :]]
            return tuple(out)
        return step

    zero = jnp.zeros((n_batch, S5_PP), F32)
    carry = lax.fori_loop(0, n_ctx, make_step(n_batch * n_lat, n_ctx, n_ctx), (zero,) * 4)
    lax.fori_loop(0, n_lat, make_step(0, n_lat, n_lat), carry)
    y = None
    for d in (0, 1):
        x_prev = jnp.concatenate([xr_ref[d], xi_ref[d]], axis=1).astype(BF16)
        y_d = _dot(u, m_ref[d]) + _dot(x_prev, ys_ref[d])
        y = y_d if y is None else y + y_d
    for h in range(S5_HG):
        o_ref[h] = y[:, h * t:(h + 1) * t]


def _s5_params(a_re, a_im, log_dt, b_re, b_im, c_re, c_im):
    a_re, a_im = a_re.astype(F32), a_im.astype(F32)
    dt = jnp.exp(log_dt.astype(F32))[..., None]
    mag = jnp.exp(a_re * dt)
    lam_re, lam_im = mag * jnp.cos(a_im * dt), mag * jnp.sin(a_im * dt)
    den = a_re * a_re + a_im * a_im
    num_re, num_im = lam_re - 1.0, lam_im
    fr = (num_re * a_re + num_im * a_im) / den
    fi = (num_im * a_re - num_re * a_im) / den
    b_re, b_im = b_re.astype(F32), b_im.astype(F32)
    bb_re = fr[..., None] * b_re - fi[..., None] * b_im
    bb_im = fr[..., None] * b_im + fi[..., None] * b_re
    c_re_t = jnp.swapaxes(c_re.astype(F32), -1, -2)
    c_im_t = jnp.swapaxes(c_im.astype(F32), -1, -2)
    col = jnp.concatenate([lam_re[..., None], lam_im[..., None], c_re_t, c_im_t], axis=-1)
    col = jnp.pad(col, ((0, 0),) * 3 + ((0, S5_PP - S5_P), (0, 128 - col.shape[-1])))
    row = jnp.concatenate([jnp.swapaxes(bb_re, -1, -2), jnp.swapaxes(bb_im, -1, -2),
                           lam_re[..., None, :], lam_im[..., None, :]], axis=-2)
    row = jnp.pad(row, ((0, 0),) * 3 + ((0, S5_ROWS - row.shape[-2]), (0, S5_PP - S5_P)))
    return col, row


def _s5_scan(ut, col, row, layer, *, n_batch, seq, ctx_len):
    t = S5_T
    n = ut.shape[1]
    nc = n // t
    assert ctx_len % t == 0 and seq % t == 0, (ctx_len, seq, t)
    n_ctx, n_lat = ctx_len // t, seq // t
    ut = ut.reshape(S5_G, S5_HG, nc, t)
    kern = functools.partial(_s5_kernel, n_batch=n_batch, n_ctx=n_ctx, n_lat=n_lat)
    yt = pl.pallas_call(
        kern,
        out_shape=jax.ShapeDtypeStruct((S5_G, S5_HG, nc, t), F32),
        grid=(S5_G,),
        in_specs=[pl.BlockSpec((None, S5_HG, nc, t), lambda g: (g, 0, 0, 0)),
                  pl.BlockSpec((None, 2, None, S5_PP, 128), lambda g: (layer, 0, g, 0, 0)),
                  pl.BlockSpec((None, 2, None, S5_ROWS, S5_PP), lambda g: (layer, 0, g, 0, 0))],
        out_specs=pl.BlockSpec((None, S5_HG, nc, t), lambda g: (g, 0, 0, 0)),
        scratch_shapes=[pltpu.VMEM((2, S5_HG * t, S5_HG * t), BF16),
                        pltpu.VMEM((2, S5_HG * t, 2 * S5_PP), BF16),
                        pltpu.VMEM((2, 2 * S5_PP, S5_HG * t), BF16),
                        pltpu.VMEM((2, 2 * S5_PP, S5_HG * t), F32),
                        pltpu.VMEM((2, nc, S5_PP), F32), pltpu.VMEM((2, nc, S5_PP), F32),
                        pltpu.VMEM((2, nc, S5_PP), F32), pltpu.VMEM((2, nc, S5_PP), F32)],
        compiler_params=_cparams(("parallel",)),
        name="s5_scan",
    )(ut, col, row)
    return yt.reshape(BR, n)


def _head_rms(x, g):
    parts = []
    for h in range(NH):
        xh = x[:, h * HD:(h + 1) * HD]
        parts.append(xh * lax.rsqrt(jnp.mean(jnp.square(xh), axis=-1, keepdims=True) + NORM_EPS))
    return jnp.concatenate(parts, axis=-1) * g


def _finish_kernel(raf_ref, rab_ref, rbf_ref, rbb_ref, rc_ref, rdf_ref, rdb_ref, ga_ref, gb_ref, u_ref, gd_ref,
                   mg_ref, x_ref, mod_ref, vec_ref, glu_w_ref, wb_ref, wo_ref, o_ref):
    vec = vec_ref[...]
    oa = _head_rms(raf_ref[...] + rab_ref[...], vec[0:1]) * _silu(ga_ref[...].astype(F32))
    ob = _head_rms(rbf_ref[...] + rbb_ref[...], vec[1:2]) * _silu(gb_ref[...].astype(F32))
    od = _head_rms(rdf_ref[...] + rdb_ref[...], vec[2:3]) * _silu(gd_ref[...].astype(F32))
    yc = rc_ref[...].T + vec[3:4] * u_ref[...].astype(F32)
    yc = 0.5 * yc * (1.0 + lax.erf(yc * (2.0 ** -0.5)))
    oc = yc * _sigmoid(_dot(yc.astype(BF16), glu_w_ref[...]) + vec[4:5])
    y2 = None
    for j, o in enumerate((oa, ob, oc, od)):
        gate2 = jnp.tanh(0.5 * mg_ref[:, j * D_MODEL:(j + 1) * D_MODEL].astype(F32)) + 1.0
        term = gate2 * _dot(o.astype(BF16), wb_ref[j])
        y2 = term if y2 is None else y2 + term
    mix2 = _dot(y2.astype(BF16), wo_ref[...])
    o_ref[...] = x_ref[...] + (0.5 * mod_ref[2:3, :]) * mix2


def _finish(raw_a, raw_b, raw_c, raw_d, p, x_all, mod, vec, glu_w, w_branch, w_out, layer, *, n_rows, n_batch,
            seq, tm=256):
    mrow = _mod_row_map(n_rows, tm, n_batch, seq)
    one = pl.Buffered(1)

    def pcol(cblock):
        return pl.BlockSpec((tm, BR), lambda i: (i, cblock))

    raw = pl.BlockSpec((tm, BR), lambda i: (i, 0))
    return pl.pallas_call(
        _finish_kernel,
        out_shape=jax.ShapeDtypeStruct((n_rows, D_MODEL), F32),
        grid=(n_rows // tm,),
        in_specs=[raw, raw, raw, raw, pl.BlockSpec((BR, tm), lambda i: (0, i)), raw, raw,
                  pcol(COL_HG), pcol(COL_RG), pcol(COL_SU), pcol(COL_MZ),
                  pl.BlockSpec((tm, N_BRANCH * D_MODEL), lambda i: (i, COL_MERGE)),
                  pl.BlockSpec((tm, D_MODEL), lambda i: (i, 0)),
                  pl.BlockSpec((None, SUBLANES, D_MODEL), lambda i: (mrow(i), 0, 0)),
                  pl.BlockSpec((None, SUBLANES, BR), lambda i: (layer, 0, 0), pipeline_mode=one),
                  pl.BlockSpec((None, BR, BR), lambda i: (layer, 0, 0), pipeline_mode=one),
                  pl.BlockSpec((None, N_BRANCH, BR, D_MODEL), lambda i: (layer, 0, 0, 0), pipeline_mode=one),
                  pl.BlockSpec((None, D_MODEL, D_MODEL), lambda i: (layer, 0, 0), pipeline_mode=one)],
        out_specs=pl.BlockSpec((tm, D_MODEL), lambda i: (i, 0)),
        compiler_params=_cparams(("parallel",)),
        name="finish",
    )(*raw_a, *raw_b, raw_c, *raw_d, p, p, p, p, p, x_all, mod, vec, glu_w, w_branch, w_out)


def _mlp_kernel(x_ref, mod_ref, g_ref, w1_ref, w2_ref, fin_ref, o_ref, xn_ref, acc_ref, *, final_norm):
    j = pl.program_id(1)

    @pl.when(j == 0)
    def _():
        h = _norm_mod(x_ref[...], g_ref[...], mod_ref[3:4, :], mod_ref[4:5, :])
        xn_ref[...] = h.astype(BF16)
        acc_ref[...] = jnp.zeros_like(acc_ref)

    a = jnp.square(jnp.maximum(_dot(xn_ref[...], w1_ref[...]), 0.0))
    acc_ref[...] += _dot(a.astype(BF16), w2_ref[...])

    @pl.when(j == pl.num_programs(1) - 1)
    def _():
        y = x_ref[...] + mod_ref[5:6, :] * acc_ref[...]
        if final_norm:
            y = (y * lax.rsqrt(jnp.mean(jnp.square(y), axis=-1, keepdims=True) + NORM_EPS)) * fin_ref[...]
        o_ref[...] = y


def _mlp(x_all, mod, norm_w, w1, w2, fin_w, layer, *, n_rows, n_batch, seq, final_norm, tm=512, tf=1024):
    mrow = _mod_row_map(n_rows, tm, n_batch, seq)
    assert D_FF % tf == 0, tf
    kern = functools.partial(_mlp_kernel, final_norm=final_norm)
    return pl.pallas_call(
        kern,
        out_shape=jax.ShapeDtypeStruct((n_rows, D_MODEL), F32),
        grid=(n_rows // tm, D_FF // tf),
        in_specs=[pl.BlockSpec((tm, D_MODEL), lambda i, j: (i, 0)),
                  pl.BlockSpec((None, SUBLANES, D_MODEL), lambda i, j: (mrow(i), 0, 0)),
                  pl.BlockSpec((None, 1, D_MODEL), lambda i, j: (layer, 0, 0)),
                  pl.BlockSpec((None, D_MODEL, tf), lambda i, j: (layer, 0, j)),
                  pl.BlockSpec((None, tf, D_MODEL), lambda i, j: (layer, j, 0)),
                  pl.BlockSpec((1, D_MODEL), lambda i, j: (0, 0))],
        out_specs=pl.BlockSpec((tm, D_MODEL), lambda i, j: (i, 0)),
        scratch_shapes=[pltpu.VMEM((tm, D_MODEL), BF16), pltpu.VMEM((tm, D_MODEL), F32)],
        compiler_params=_cparams(("parallel", "arbitrary")),
        name="mlp",
    )(x_all, mod, norm_w, w1, w2, fin_w.reshape(1, D_MODEL))


W_PREP_ROWS = 1024
N_MERGE_BLK = N_BRANCH * D_MODEL // W_PREP_ROWS
MERGE_OFF = GATE_OFF + 4 * NH


def _w_prep_kernel(a_ref, g_ref, main_ref, gate_ref):
    @pl.when(pl.program_id(1) == 0)
    def _():
        gate_ref[...] = jnp.zeros_like(gate_ref)
        gate_ref[0:4 * NH, :] = g_ref[0].astype(BF16)

    main_ref[...] = a_ref[0].astype(BF16)


def _split_w_in(w):
    depth = w.shape[0]
    wt = jnp.swapaxes(w, 1, 2)
    tr = W_PREP_ROWS

    def a_map(l, j):
        row = jnp.where(j < N_MERGE_BLK, MERGE_OFF + j * tr, (j - N_MERGE_BLK) * tr)
        return l, pl.multiple_of(row, 16), 0

    return pl.pallas_call(
        _w_prep_kernel,
        out_shape=(jax.ShapeDtypeStruct((depth, P_WIDTH, D_MODEL), BF16),
                   jax.ShapeDtypeStruct((depth, GATE_PAD, D_MODEL), BF16)),
        grid=(depth, P_WIDTH // tr),
        in_specs=[pl.BlockSpec((pl.Element(1), pl.Element(tr), pl.Element(D_MODEL)), a_map),
                  pl.BlockSpec((pl.Element(1), pl.Element(4 * NH), pl.Element(D_MODEL)),
                               lambda l, j: (l, GATE_OFF, 0))],
        out_specs=(pl.BlockSpec((None, tr, D_MODEL), lambda l, j: (l, j, 0)),
                   pl.BlockSpec((None, GATE_PAD, D_MODEL), lambda l, j: (l, 0, 0))),
        compiler_params=_cparams(("parallel", "arbitrary")),
        name="w_in_prep",
    )(wt, wt)


def kernel(x, c, ctx, c_ctx, w_mod, b_mod, norm_mix, norm_mlp, w_in, hgrn_lb_logits, hgrn_norm, ret_decay, ret_norm, s5_a_re, s5_a_im, s5_log_dt, s5_b_re, s5_b_im, s5_c_re, s5_c_im, s5_d, s5_glu_w, s5_glu_b, mlstm_conv_w, mlstm_conv_b, mlstm_gate_b, mlstm_norm, w_branch, w_out, w_ff1, w_ff2, final_norm):
    n_batch, seq, _ = x.shape
    ctx_len = ctx.shape[1]
    depth = w_in.shape[0]
    nl = n_batch * seq
    dims = dict(n_batch=n_batch, seq=seq)

    p_lb = jax.nn.softmax(hgrn_lb_logits.astype(F32), axis=0)
    lower_bounds = (jnp.cumsum(p_lb, axis=0) - p_lb[0]).reshape(depth, 2, 1, BR)
    lg = jnp.log1p(-jnp.exp(ret_decay.astype(F32)))
    gate_b = jnp.pad(mlstm_gate_b.reshape(depth, 1, 4 * NH).astype(F32), ((0, 0), (0, 0), (0, GATE_PAD - 4 * NH)))
    vec = jnp.stack([hgrn_norm, ret_norm, mlstm_norm, s5_d, s5_glu_b], axis=1).astype(F32)
    vec = jnp.pad(vec, ((0, 0), (0, SUBLANES - vec.shape[1]), (0, 0)))
    w_main, w_gate = _split_w_in(w_in)
    glu_w, wb, wo = s5_glu_w.astype(BF16), w_branch.astype(BF16), w_out.astype(BF16)
    w1, w2 = w_ff1.astype(BF16), w_ff2.astype(BF16)
    norm_mix3 = norm_mix.reshape(depth, 1, D_MODEL)
    norm_mlp3 = norm_mlp.reshape(depth, 1, D_MODEL)
    conv_w = mlstm_conv_w.reshape(depth, 9, 2 * BR)
    conv_b = mlstm_conv_b.reshape(depth, 1, 2 * BR)
    s5_col, s5_row = _s5_params(s5_a_re, s5_a_im, s5_log_dt, s5_b_re, s5_b_im, s5_c_re, s5_c_im)

    assert n_batch + 1 <= SUBLANES
    c_pad = jnp.zeros((SUBLANES, D_MODEL), F32).at[:n_batch].set(c).at[n_batch].set(c_ctx)
    x_all = jnp.concatenate([x.reshape(nl, D_MODEL), ctx.reshape(n_batch * ctx_len, D_MODEL)], axis=0)

    mod_all = _modulation(c_pad, w_mod, b_mod)[:, :n_batch + 1].reshape(depth, n_batch + 1, N_MOD, D_MODEL)
    mod_all = jnp.pad(mod_all, ((0, 0), (0, 0), (0, SUBLANES - N_MOD), (0, 0)))
    for l in range(depth):
        last = l == depth - 1
        mod = mod_all[l]
        p, gates, ut = _in_proj(x_all, mod, norm_mix3, w_main, w_gate, l, **dims)
        qk = _mlstm_conv(p, conv_w, conv_b, l, ctx_len=ctx_len, **dims)
        raw_a, raw_b, raw_d = _scans(p, gates, qk, lg[l], lower_bounds, gate_b, l, ctx_len=ctx_len, **dims)
        raw_c = _s5_scan(ut, s5_col, s5_row, l, ctx_len=ctx_len, **dims)
        n_rows = nl if last else x_all.shape[0]
        x_mid = _finish(raw_a, raw_b, raw_c, raw_d, p, x_all, mod, vec, glu_w, wb, wo, l, n_rows=n_rows, **dims)
        x_all = _mlp(x_mid, mod, norm_mlp3, w1, w2, final_norm, l, n_rows=n_rows, final_norm=last, **dims)
    return x_all.reshape(n_batch, seq, D_MODEL)
```

```python
import functools

import jax
import jax.numpy as jnp
from jax import lax
from jax.experimental import pallas as pl
from jax.experimental.pallas import tpu as pltpu

F32 = jnp.float32
BF16 = jnp.bfloat16

D_MODEL = 2048
N_BRANCH = 4
BR = D_MODEL // N_BRANCH
HD = 128
NH = BR // HD
S5_HG = 16
S5_G = BR // S5_HG
S5_P = 64
D_FF = 4 * D_MODEL
N_MOD = 6
GRID_W = 64
NORM_EPS = 1e-6
NEG_BIG = -1e30
F_TINY = 1e-30

COL_MERGE = 0
COL_HQ, COL_HF, COL_HV, COL_HG = 16, 17, 19, 20
COL_RQ, COL_RK, COL_RV, COL_RG = 21, 22, 23, 24
COL_SU = 25
COL_MQK, COL_MV, COL_MZ = 26, 28, 29
P_WIDTH = 30 * BR
GATE_OFF = 14 * BR
GATE_PAD = 128
SUBLANES = 8

T_HGRN = 128
T_ATTN = 256
S5_T = 128
S5_PP = 128
S5_ROWS = 40
VMEM_LIMIT = 56 * 1024 * 1024

def _cparams(sem):
    return pltpu.CompilerParams(dimension_semantics=sem, vmem_limit_bytes=VMEM_LIMIT)


def _dot(a, b):
    return jnp.dot(a, b, preferred_element_type=F32)


def _dot_nt(a, b):
    return lax.dot_general(a, b, (((1,), (1,)), ((), ())), preferred_element_type=F32)


def _dot_tn(a, b):
    return lax.dot_general(a, b, (((0,), (0,)), ((), ())), preferred_element_type=F32)


def _split3(x):
    hi = x.astype(BF16)
    r1 = x - hi.astype(F32)
    mid = r1.astype(BF16)
    return hi, mid, (r1 - mid.astype(F32)).astype(BF16)


def _split2(x):
    hi = x.astype(BF16)
    return hi, (x - hi.astype(F32)).astype(BF16)


def _cumsum_rows(tri, x):
    hi, mid = _split2(x)
    return _dot(tri, hi) + _dot(tri, mid)


def _sigmoid(x):
    return 1.0 / (1.0 + jnp.exp(-x))


def _silu(x):
    return x * _sigmoid(x)


def _mod_kernel(c_ref, w_ref, b_ref, o_ref):
    @pl.when(pl.program_id(1) == 0)
    def _():
        o_ref[...] = jnp.broadcast_to(b_ref[...], o_ref.shape)

    a = _silu(c_ref[...]).astype(BF16)
    o_ref[...] += _dot(a, w_ref[...].astype(BF16))


def _modulation(c_pad, w_mod, b_mod, tk=256):
    depth, _, n = w_mod.shape
    return pl.pallas_call(
        _mod_kernel,
        out_shape=jax.ShapeDtypeStruct((depth, SUBLANES, n), F32),
        grid=(depth, D_MODEL // tk),
        in_specs=[pl.BlockSpec((SUBLANES, tk), lambda l, k: (0, k)),
                  pl.BlockSpec((None, tk, n), lambda l, k: (l, k, 0)),
                  pl.BlockSpec((None, 1, n), lambda l, k: (l, 0, 0))],
        out_specs=pl.BlockSpec((None, SUBLANES, n), lambda l, k: (l, 0, 0)),
        compiler_params=_cparams(("parallel", "arbitrary")),
        name="modulation",
    )(c_pad, w_mod, b_mod.reshape(depth, 1, n))


def _norm_mod(x, g, shift, scale):
    y = x * lax.rsqrt(jnp.mean(jnp.square(x), axis=-1, keepdims=True) + NORM_EPS)
    return (y * g) * (1.0 + scale) + shift


def _inproj_kernel(x_ref, mod_ref, g_ref, w_ref, wg_ref, p_ref, gate_ref, ut_ref, xn_ref, *, su_block, su_off):
    @pl.when(pl.program_id(1) == 0)
    def _():
        h = _norm_mod(x_ref[...], g_ref[...], mod_ref[0:1, :], mod_ref[1:2, :])
        xn_ref[...] = h.astype(BF16)
        gate_ref[...] = _dot_nt(xn_ref[...], wg_ref[...])

    acc = _dot_nt(xn_ref[...], w_ref[...])
    p_ref[...] = acc.astype(p_ref.dtype)

    @pl.when(pl.program_id(1) == su_block)
    def _():
        ut_ref[...] = acc[:, su_off:su_off + BR].T.astype(ut_ref.dtype)


def _mod_row_map(n_rows, tm, n_batch, seq):
    assert n_rows % tm == 0 and seq % tm == 0, (n_rows, seq, tm)
    n_lat_tiles, tiles_per_batch = n_batch * seq // tm, seq // tm

    def row(i):
        return jnp.where(i < n_lat_tiles, i // tiles_per_batch, n_batch)
    return row


def _in_proj(x_all, mod, norm_w, w_main, w_gate, layer, *, n_batch, seq, tm=1024, tn=1536):
    n = x_all.shape[0]
    assert P_WIDTH % tn == 0, tn
    su_block, su_off = divmod(COL_SU * BR, tn)
    assert su_off + BR <= tn, (su_off, tn)
    mrow = _mod_row_map(n, tm, n_batch, seq)
    return pl.pallas_call(
        functools.partial(_inproj_kernel, su_block=su_block, su_off=su_off),
        out_shape=(jax.ShapeDtypeStruct((n, P_WIDTH), BF16),
                   jax.ShapeDtypeStruct((n, GATE_PAD), F32),
                   jax.ShapeDtypeStruct((BR, n), BF16)),
        grid=(n // tm, P_WIDTH // tn),
        in_specs=[pl.BlockSpec((tm, D_MODEL), lambda i, j: (i, 0)),
                  pl.BlockSpec((None, SUBLANES, D_MODEL), lambda i, j: (mrow(i), 0, 0)),
                  pl.BlockSpec((None, 1, D_MODEL), lambda i, j: (layer, 0, 0)),
                  pl.BlockSpec((None, tn, D_MODEL), lambda i, j: (layer, j, 0)),
                  pl.BlockSpec((None, GATE_PAD, D_MODEL), lambda i, j: (layer, 0, 0))],
        out_specs=(pl.BlockSpec((tm, tn), lambda i, j: (i, j)),
                   pl.BlockSpec((tm, GATE_PAD), lambda i, j: (i, 0)),
                   pl.BlockSpec((BR, tm), lambda i, j: (0, i))),
        scratch_shapes=[pltpu.VMEM((tm, D_MODEL), BF16)],
        compiler_params=_cparams(("parallel", "arbitrary")),
        name="in_proj",
    )(x_all, mod, norm_w, w_main, w_gate)


CONV_ROWS = 1024


def _conv_kernel(up_ref, x_ref, dn_ref, w_ref, b_ref, o_ref, *, n_lat_blocks, blocks_per_img, ctx_len):
    rb = pl.program_id(0)
    x = x_ref[...].astype(F32)
    w = w_ref[...]
    n = CONV_ROWS

    @pl.when(rb < n_lat_blocks)
    def _():
        rr = rb % blocks_per_img
        up = jnp.where(rr > 0, up_ref[...].astype(F32), 0.0)
        dn = jnp.where(rr < blocks_per_img - 1, dn_ref[...].astype(F32), 0.0)
        ext = jnp.concatenate([up, x, dn], axis=0)
        ne = n + 2 * GRID_W
        col = lax.broadcasted_iota(jnp.int32, (ne, 1), 0) % GRID_W
        xl = jnp.where(col >= 1, pltpu.roll(ext, 1, 0), 0.0)
        xr = jnp.where(col <= GRID_W - 2, pltpu.roll(ext, ne - 1, 0), 0.0)
        acc = None
        for i in range(3):
            y = w[3 * i:3 * i + 1] * xl + w[3 * i + 1:3 * i + 2] * ext + w[3 * i + 2:3 * i + 3] * xr
            part = y[i * GRID_W:i * GRID_W + n]
            acc = part if acc is None else acc + part
        o_ref[...] = _silu(acc + b_ref[...]).astype(o_ref.dtype)

    @pl.when(rb >= n_lat_blocks)
    def _():
        t = lax.broadcasted_iota(jnp.int32, (n, 1), 0) % ctx_len
        xl = jnp.where(t >= 1, pltpu.roll(x, 1, 0), 0.0)
        xr = jnp.where(t <= ctx_len - 2, pltpu.roll(x, n - 1, 0), 0.0)
        acc = w[3:4] * xl + w[4:5] * x + w[5:6] * xr
        o_ref[...] = _silu(acc + b_ref[...]).astype(o_ref.dtype)


def _mlstm_conv(p, conv_w, conv_b, layer, *, n_batch, seq, ctx_len, cb=2 * BR):
    assert CONV_ROWS % ctx_len == 0 and seq % CONV_ROWS == 0 and (n_batch * ctx_len) % CONV_ROWS == 0
    n = p.shape[0]
    n_blocks = n // CONV_ROWS
    hb = CONV_ROWS // GRID_W
    n_halo = n // GRID_W
    c0 = COL_MQK * BR // cb
    kern = functools.partial(_conv_kernel, n_lat_blocks=n_batch * seq // CONV_ROWS,
                             blocks_per_img=seq // CONV_ROWS, ctx_len=ctx_len)
    return pl.pallas_call(
        kern,
        out_shape=jax.ShapeDtypeStruct((n, 2 * BR), BF16),
        grid=(n_blocks, 2 * BR // cb),
        in_specs=[pl.BlockSpec((GRID_W, cb), lambda r, c: (jnp.maximum(r * hb - 1, 0), c0 + c)),
                  pl.BlockSpec((CONV_ROWS, cb), lambda r, c: (r, c0 + c)),
                  pl.BlockSpec((GRID_W, cb), lambda r, c: (jnp.minimum((r + 1) * hb, n_halo - 1), c0 + c)),
                  pl.BlockSpec((None, 9, cb), lambda r, c: (layer, 0, c)),
                  pl.BlockSpec((None, 1, cb), lambda r, c: (layer, 0, c))],
        out_specs=pl.BlockSpec((CONV_ROWS, cb), lambda r, c: (r, c)),
        compiler_params=_cparams(("parallel", "parallel")),
        name="mlstm_conv",
    )(p, p, p, conv_w, conv_b)


def _chunk_block(b, d, i, *, n_ctx, n_lat, n_batch):
    ctx_j = i if d == 0 else n_ctx - 1 - i
    lat_j = i - n_ctx if d == 0 else n_ctx + n_lat - 1 - i
    return jnp.where(i < n_ctx, n_batch * n_lat + b * n_ctx + ctx_j, b * n_lat + lat_j)


def _init_at_first_step(*refs):
    @pl.when(pl.program_id(1) == 0)
    def _():
        for ref in refs:
            ref[...] = jnp.zeros_like(ref)


def _ret_direction(d, lg_ref, q_ref, k_ref, v_ref, o_ref, s_ref):
    t = T_ATTN
    r = lax.broadcasted_iota(jnp.int32, (t, t), 0)
    c = lax.broadcasted_iota(jnp.int32, (t, t), 1)
    rel = (r - c) if d == 0 else (c - r)
    relf = jnp.maximum(rel, 0).astype(F32)
    tt = lax.broadcasted_iota(jnp.int32, (t, 1), 0)
    pos = (tt if d == 0 else t - 1 - tt).astype(F32)
    scale = HD ** -0.5
    for h in range(NH):
        sl = slice(h * HD, (h + 1) * HD)
        lg = lg_ref[d, h]
        q = q_ref[:, sl]
        k = k_ref[:, sl]
        v = v_ref[:, sl]
        intra = jnp.where(rel >= 0, jnp.exp(lg * relf), 0.0)
        scores = (_dot_nt(q, k) * scale) * intra
        s_old = s_ref[d, h]
        out = _dot(scores.astype(BF16), v) + jnp.exp(lg * (pos + 1.0)) * _dot(q, s_old.astype(BF16))
        kd = (k.astype(F32) * (scale * jnp.exp(lg * (t - 1.0 - pos)))).astype(BF16)
        s_ref[d, h] = jnp.exp(lg * t) * s_old + _dot_tn(kd, v)
        o_ref[:, sl] = out


def _hgrn_direction(d, lb_ref, q_ref, z_ref, v_ref, o_ref, st_ref):
    t = T_HGRN
    fwd = d == 0
    lb = lb_ref[...]
    z = z_ref[...].astype(F32)
    sig = _sigmoid(z)
    f = lb + (1.0 - lb) * sig
    logf = jnp.log(jnp.maximum(f, F_TINY))
    kk = (1.0 - lb) * (1.0 - sig)
    qq = _silu(q_ref[...].astype(F32))
    r2 = lax.broadcasted_iota(jnp.int32, (t, t), 0)
    c2 = lax.broadcasted_iota(jnp.int32, (t, t), 1)
    tri = jnp.where((c2 <= r2) if fwd else (c2 >= r2), 1.0, 0.0).astype(BF16)
    cum = _cumsum_rows(tri, logf)
    xdiff = jnp.where((c2 < r2) if fwd else (c2 > r2), r2 ^ c2, 0)
    row = lax.broadcasted_iota(jnp.int32, (t, 1), 0)
    if fwd:
        e_ref = jnp.where(row == 0, 0.0, pltpu.roll(cum, 1, 0))
    else:
        e_ref = jnp.where(row == t - 1, 0.0, pltpu.roll(cum, t - 1, 0))
    f_ref = cum
    levels = []
    w = 1
    while w < t:
        if w == 1:
            q_l, k_l = (qq * jnp.maximum(f, F_TINY)).astype(BF16), kk.astype(BF16)
        else:
            q_l = (qq * jnp.exp(cum - e_ref)).astype(BF16)
            k_l = (kk * jnp.exp(f_ref - cum)).astype(BF16)
        levels.append((q_l, k_l, (xdiff >> (w.bit_length() - 1)) == 1))
        upper = (row & w) != 0
        if fwd:
            e_ref = jnp.where(upper, pltpu.roll(e_ref, w, 0), e_ref)
            f_ref = jnp.where(upper, f_ref, pltpu.roll(f_ref, t - w, 0))
        else:
            e_ref = jnp.where(upper, e_ref, pltpu.roll(e_ref, t - w, 0))
            f_ref = jnp.where(upper, pltpu.roll(f_ref, w, 0), f_ref)
        w *= 2
    q_b = qq.astype(BF16)
    k_b = kk.astype(BF16)
    tail = cum[t - 1:t] if fwd else cum[0:1]
    q_state = (qq * jnp.exp(cum)).astype(BF16)
    k_state = (kk * jnp.exp(tail - cum)).astype(BF16)
    decay = jnp.exp(tail)
    for h in range(NH):
        sl = slice(h * HD, (h + 1) * HD)
        v_h = v_ref[:, sl]
        scores = jnp.where(r2 == c2, _dot_nt(q_b[:, sl], k_b[:, sl]), 0.0)
        for q_l, k_l, m in levels:
            scores = jnp.where(m, _dot_nt(q_l[:, sl], k_l[:, sl]), scores)
        st = st_ref[d, h]
        o_ref[:, sl] = _dot(scores.astype(BF16), v_h) + _dot_nt(q_state[:, sl], st.astype(BF16))
        st_ref[d, h] = st * decay[:, sl] + _dot_tn(v_h, k_state[:, sl])


def _hgrn_kernel(lbf_ref, qf_ref, zf_ref, vf_ref, lbb_ref, qb_ref, zb_ref, vb_ref, of_ref, ob_ref, st_ref):
    _init_at_first_step(st_ref)
    _hgrn_direction(0, lbf_ref, qf_ref, zf_ref, vf_ref, of_ref, st_ref)
    _hgrn_direction(1, lbb_ref, qb_ref, zb_ref, vb_ref, ob_ref, st_ref)


def _log_sigmoid(x):
    return jnp.minimum(x, 0.0) - jnp.log(1.0 + jnp.exp(-jnp.abs(x)))


def _rep_lane(parts, lane, width):
    sel = jnp.where(lax.broadcasted_iota(jnp.int32, (GATE_PAD, width), 0) == lane, 1.0, 0.0).astype(BF16)
    hi, mid = parts
    return _dot(hi, sel) + _dot(mid, sel)


def _mlstm_direction(d, qk_ref, v_ref, g_ref, gb_ref, o_ref, c_ref, m_ref):
    t = T_ATTN
    assert t == 2 * HD
    fwd = d == 0
    g = g_ref[...] + gb_ref[...]
    r2 = lax.broadcasted_iota(jnp.int32, (t, t), 0)
    c2 = lax.broadcasted_iota(jnp.int32, (t, t), 1)
    mask = (c2 <= r2) if fwd else (c2 >= r2)
    tri = jnp.where(mask, 1.0, 0.0).astype(BF16)
    eye = jnp.where(r2 == c2, 1.0, 0.0).astype(BF16)
    lf = _log_sigmoid(g)
    cum_c = _cumsum_rows(tri, lf)
    a_col = g - pltpu.roll(cum_c, GATE_PAD - NH, 1)
    row = lax.broadcasted_iota(jnp.int32, (t, 1), 0)
    pm = a_col
    k = 1
    while k < t:
        if fwd:
            pm = jnp.maximum(pm, jnp.where(row >= k, pltpu.roll(pm, k, 0), NEG_BIG))
        else:
            pm = jnp.maximum(pm, jnp.where(row < t - k, pltpu.roll(pm, t - k, 0), NEG_BIG))
        k *= 2
    a_parts, pm_parts, cum_parts = _split2(a_col), _split2(pm), _split2(cum_c)
    a_rows = _dot_tn(a_parts[0], eye) + _dot_tn(a_parts[1], eye)
    scale = HD ** -0.5
    last = t - 1 if fwd else 0
    ones = jnp.ones((t, HD), BF16)
    for h in range(NH):
        sl = slice(h * HD, (h + 1) * HD)
        li, lfw = 2 * NH * d + h, 2 * NH * d + NH + h
        m_old = m_ref[d, h]
        mx = jnp.maximum(_rep_lane(pm_parts, li, HD), m_old)
        mx2 = jnp.concatenate([mx, mx], axis=1)
        cc = _rep_lane(cum_parts, lfw, HD)
        a_rep = _rep_lane(a_parts, li, HD)
        a_row = a_rows[li:li + 1]
        q = qk_ref[:, sl]
        k_h = qk_ref[:, BR + h * HD:BR + (h + 1) * HD]
        v_aug = jnp.concatenate([v_ref[:, sl], ones], axis=1)
        w = jnp.where(mask, jnp.exp(a_row - mx2), 0.0)
        scores = (_dot_nt(q, k_h) * scale) * w
        w_state = jnp.exp(m_old - mx)
        c_old = c_ref[d, h]
        s_hi = scores.astype(BF16)
        s_mid = (scores - s_hi.astype(F32)).astype(BF16)
        intra = _dot(s_hi, v_aug)
        inter = _dot(q, c_old.astype(BF16))
        num = intra[:, 0:HD] + w_state * inter[:, 0:HD]
        den = (intra[:, HD:2 * HD] + _dot(s_mid, ones)) + w_state * inter[:, HD:2 * HD]
        o_ref[:, sl] = num / jnp.maximum(jnp.abs(den), jnp.exp(-(cc + mx)))
        total = cc[last:last + 1]
        m_new_rel = mx[last:last + 1]
        keep = jnp.exp(m_old - m_new_rel)
        w_end = jnp.exp(a_rep - m_new_rel)
        kw = (k_h.astype(F32) * (scale * w_end)).astype(BF16)
        c_ref[d, h] = jnp.concatenate([keep, keep], axis=1) * c_old + _dot_tn(kw, v_aug)
        m_ref[d, h] = total + m_new_rel


def _ret_mlstm_kernel(lg_ref, rqf_ref, rkf_ref, rvf_ref, rqb_ref, rkb_ref, rvb_ref,
                      qkf_ref, vf_ref, gf_ref, qkb_ref, vb_ref, gb_ref, bias_ref,
                      rof_ref, rob_ref, mof_ref, mob_ref, s_ref, c_ref, m_ref):
    _init_at_first_step(s_ref, c_ref, m_ref)
    _ret_direction(0, lg_ref, rqf_ref, rkf_ref, rvf_ref, rof_ref, s_ref)
    _mlstm_direction(0, qkf_ref, vf_ref, gf_ref, bias_ref, mof_ref, c_ref, m_ref)
    _ret_direction(1, lg_ref, rqb_ref, rkb_ref, rvb_ref, rob_ref, s_ref)
    _mlstm_direction(1, qkb_ref, vb_ref, gb_ref, bias_ref, mob_ref, c_ref, m_ref)


def _scans(p, gates, qk, lg, lb, gate_b, layer, *, n_batch, seq, ctx_len):
    n = p.shape[0]
    state = pltpu.VMEM((2, NH, HD, HD), F32)
    vec = pltpu.VMEM((2, NH, 1, HD), F32)

    def call(kernel, t, inputs, make_specs, scratch, name, n_mixers=1):
        assert ctx_len % t == 0 and seq % t == 0, (ctx_len, seq, t)
        n_ctx, n_lat = ctx_len // t, seq // t

        def rows(d, width, cblock=0):
            return pl.BlockSpec((t, width), lambda b, i: (_chunk_block(b, d, i, n_ctx=n_ctx, n_lat=n_lat,
                                                                        n_batch=n_batch), cblock))

        out = jax.ShapeDtypeStruct((n, BR), F32)
        return pl.pallas_call(
            kernel,
            out_shape=(out, out) * n_mixers,
            grid=(n_batch, n_ctx + n_lat),
            in_specs=make_specs(rows),
            out_specs=(rows(0, BR), rows(1, BR)) * n_mixers,
            scratch_shapes=scratch,
            compiler_params=_cparams(("parallel", "arbitrary")),
            name=name,
        )(*inputs)

    lb_spec = lambda d: pl.BlockSpec((None, None, 1, BR), lambda b, i: (layer, d, 0, 0))
    raw_a = call(_hgrn_kernel, T_HGRN, (lb, p, p, p, lb, p, p, p),
                 lambda rows: [spec for d in (0, 1)
                               for spec in (lb_spec(d), rows(d, BR, COL_HQ), rows(d, BR, COL_HF + d),
                                            rows(d, BR, COL_HV))],
                 [state], "hgrn2_scan")
    raw_bd = call(_ret_mlstm_kernel, T_ATTN, (lg, p, p, p, p, p, p, qk, p, gates, qk, p, gates, gate_b),
                  lambda rows: [pl.BlockSpec(memory_space=pltpu.SMEM)]
                  + [rows(d, BR, c) for d in (0, 1) for c in (COL_RQ, COL_RK, COL_RV)]
                  + [spec for d in (0, 1)
                     for spec in (rows(d, 2 * BR), rows(d, BR, COL_MV), rows(d, GATE_PAD))]
                  + [pl.BlockSpec((None, 1, GATE_PAD), lambda b, i: (layer, 0, 0))],
                  [state, pltpu.VMEM((2, NH, HD, 2 * HD), F32), vec], "retention_mlstm_scan", n_mixers=2)
    return raw_a, raw_bd[0:2], raw_bd[2:4]


def _cmul(ar, ai, br, bi):
    return ar * br - ai * bi, ar * bi + ai * br


def _pow_table(lam_r, lam_i, expo):
    pr = jnp.ones(expo.shape, F32)
    pi = jnp.zeros(expo.shape, F32)
    ar = jnp.broadcast_to(lam_r, expo.shape)
    ai = jnp.broadcast_to(lam_i, expo.shape)
    for k in range(S5_T.bit_length() - 1):
        nr, ni = _cmul(pr, pi, ar, ai)
        bit = (expo & (1 << k)) != 0
        pr, pi = jnp.where(bit, nr, pr), jnp.where(bit, ni, pi)
        ar, ai = _cmul(ar, ai, ar, ai)
    return pr, pi


def _s5_operators(d, u, col_ref, row_ref, m_ref, w_ref, ys_ref, z_ref, wr_ref, wi_ref):
    t = S5_T
    fwd = d == 0
    col = col_ref[d]
    rowp = row_ref[d]
    lam_rc, lam_ic = col[:, 0:1], col[:, 1:2]
    bbr, bbi = rowp[0:S5_HG], rowp[S5_HG:2 * S5_HG]
    lam_rr, lam_ir = rowp[2 * S5_HG:2 * S5_HG + 1], rowp[2 * S5_HG + 1:2 * S5_HG + 2]
    lane = lax.broadcasted_iota(jnp.int32, (t, t), 1)
    sub = lax.broadcasted_iota(jnp.int32, (t, t), 0)

    pr, pi = _pow_table(lam_rc, lam_ic, lane if fwd else t - 1 - lane)
    qr, qi = _cmul(pr, pi, lam_rc, lam_ic)
    for h in range(S5_HG):
        cr, ci = col[:, 2 + h:3 + h], col[:, 2 + S5_HG + h:3 + S5_HG + h]
        hs = slice(h * t, (h + 1) * t)
        z_ref[d, 0:S5_PP, hs] = cr * pr - ci * pi
        z_ref[d, S5_PP:2 * S5_PP, hs] = -(cr * pi + ci * pr)
        ys_ref[d, 0:S5_PP, hs] = (cr * qr - ci * qi).astype(BF16)
        ys_ref[d, S5_PP:2 * S5_PP, hs] = (-(cr * qi + ci * qr)).astype(BF16)
    b_hi, b_mid, _ = _split3(jnp.concatenate([bbr, bbi], axis=1))
    z_hi, z_mid, _ = _split3(z_ref[d])
    krow = _dot(b_hi, z_hi) + (_dot(b_hi, z_mid) + _dot(b_mid, z_hi))
    shift, keep = (0, lane >= sub) if fwd else (1, lane <= sub)
    for hp in range(S5_HG):
        for h in range(S5_HG):
            tile = jnp.broadcast_to(krow[hp:hp + 1, h * t:(h + 1) * t], (t, t))
            tile = pltpu.roll(tile, shift, 1, stride=1, stride_axis=0)
            m_ref[d, hp * t:(hp + 1) * t, h * t:(h + 1) * t] = jnp.where(keep, tile, 0.0).astype(BF16)
    tr, ti = _pow_table(lam_rr, lam_ir, t - 1 - sub if fwd else sub)
    for hp in range(S5_HG):
        br, bi = bbr[hp:hp + 1], bbi[hp:hp + 1]
        w_ref[d, hp * t:(hp + 1) * t, 0:S5_PP] = (tr * br - ti * bi).astype(BF16)
        w_ref[d, hp * t:(hp + 1) * t, S5_PP:2 * S5_PP] = (tr * bi + ti * br).astype(BF16)
    wv = _dot(u, w_ref[d])
    wr_ref[d] = wv[:, 0:S5_PP]
    wi_ref[d] = wv[:, S5_PP:2 * S5_PP]
    lr, li = lam_rr, lam_ir
    for _ in range(t.bit_length() - 1):
        lr, li = _cmul(lr, li, lr, li)
    return lr, li


def _s5_kernel(u_ref, col_ref, row_ref, o_ref, m_ref, w_ref, ys_ref, z_ref, wr_ref, wi_ref, xr_ref, xi_ref, *,
               n_batch, n_ctx, n_lat):
    t = S5_T
    u = jnp.concatenate([u_ref[hp] for hp in range(S5_HG)], axis=1)
    lam_t = [_s5_operators(d, u, col_ref, row_ref, m_ref, w_ref, ys_ref, z_ref, wr_ref, wi_ref) for d in (0, 1)]

    def make_step(base, stride, count):
        def step(i, carry):
            out = []
            for d in (0, 1):
                xr, xi = carry[2 * d], carry[2 * d + 1]
                j = i if d == 0 else count - 1 - i
                idx = pl.ds(base + j, n_batch, stride=stride)
                xr_ref[d, idx, :] = xr
                xi_ref[d, idx, :] = xi
                nr, ni = _cmul(xr, xi, *lam_t[d])
                out += [nr + wr_ref[d, idx, :], ni + wi_ref[d, idx, :]]
            return tuple(out)
        return step

    zero = jnp.zeros((n_batch, S5_PP), F32)
    carry = lax.fori_loop(0, n_ctx, make_step(n_batch * n_lat, n_ctx, n_ctx), (zero,) * 4)
    lax.fori_loop(0, n_lat, make_step(0, n_lat, n_lat), carry)
    y = None
    for d in (0, 1):
        x_prev = jnp.concatenate([xr_ref[d], xi_ref[d]], axis=1).astype(BF16)
        y_d = _dot(u, m_ref[d]) + _dot(x_prev, ys_ref[d])
        y = y_d if y is None else y + y_d
    for h in range(S5_HG):
        o_ref[h] = y[:, h * t:(h + 1) * t]


def _s5_params(a_re, a_im, log_dt, b_re, b_im, c_re, c_im):
    a_re, a_im = a_re.astype(F32), a_im.astype(F32)
    dt = jnp.exp(log_dt.astype(F32))[..., None]
    mag = jnp.exp(a_re * dt)
    lam_re, lam_im = mag * jnp.cos(a_im * dt), mag * jnp.sin(a_im * dt)
    den = a_re * a_re + a_im * a_im
    num_re, num_im = lam_re - 1.0, lam_im
    fr = (num_re * a_re + num_im * a_im) / den
    fi = (num_im * a_re - num_re * a_im) / den
    b_re, b_im = b_re.astype(F32), b_im.astype(F32)
    bb_re = fr[..., None] * b_re - fi[..., None] * b_im
    bb_im = fr[..., None] * b_im + fi[..., None] * b_re
    c_re_t = jnp.swapaxes(c_re.astype(F32), -1, -2)
    c_im_t = jnp.swapaxes(c_im.astype(F32), -1, -2)
    col = jnp.concatenate([lam_re[..., None], lam_im[..., None], c_re_t, c_im_t], axis=-1)
    col = jnp.pad(col, ((0, 0),) * 3 + ((0, S5_PP - S5_P), (0, 128 - col.shape[-1])))
    row = jnp.concatenate([jnp.swapaxes(bb_re, -1, -2), jnp.swapaxes(bb_im, -1, -2),
                           lam_re[..., None, :], lam_im[..., None, :]], axis=-2)
    row = jnp.pad(row, ((0, 0),) * 3 + ((0, S5_ROWS - row.shape[-2]), (0, S5_PP - S5_P)))
    return col, row


def _s5_scan(ut, col, row, layer, *, n_batch, seq, ctx_len):
    t = S5_T
    n = ut.shape[1]
    nc = n // t
    assert ctx_len % t == 0 and seq % t == 0, (ctx_len, seq, t)
    n_ctx, n_lat = ctx_len // t, seq // t
    ut = ut.reshape(S5_G, S5_HG, nc, t)
    kern = functools.partial(_s5_kernel, n_batch=n_batch, n_ctx=n_ctx, n_lat=n_lat)
    yt = pl.pallas_call(
        kern,
        out_shape=jax.ShapeDtypeStruct((S5_G, S5_HG, nc, t), F32),
        grid=(S5_G,),
        in_specs=[pl.BlockSpec((None, S5_HG, nc, t), lambda g: (g, 0, 0, 0)),
                  pl.BlockSpec((None, 2, None, S5_PP, 128), lambda g: (layer, 0, g, 0, 0)),
                  pl.BlockSpec((None, 2, None, S5_ROWS, S5_PP), lambda g: (layer, 0, g, 0, 0))],
        out_specs=pl.BlockSpec((None, S5_HG, nc, t), lambda g: (g, 0, 0, 0)),
        scratch_shapes=[pltpu.VMEM((2, S5_HG * t, S5_HG * t), BF16),
                        pltpu.VMEM((2, S5_HG * t, 2 * S5_PP), BF16),
                        pltpu.VMEM((2, 2 * S5_PP, S5_HG * t), BF16),
                        pltpu.VMEM((2, 2 * S5_PP, S5_HG * t), F32),
                        pltpu.VMEM((2, nc, S5_PP), F32), pltpu.VMEM((2, nc, S5_PP), F32),
                        pltpu.VMEM((2, nc, S5_PP), F32), pltpu.VMEM((2, nc, S5_PP), F32)],
        compiler_params=_cparams(("parallel",)),
        name="s5_scan",
    )(ut, col, row)
    return yt.reshape(BR, n)


def _head_rms(x, g):
    parts = []
    for h in range(NH):
        xh = x[:, h * HD:(h + 1) * HD]
        parts.append(xh * lax.rsqrt(jnp.mean(jnp.square(xh), axis=-1, keepdims=True) + NORM_EPS))
    return jnp.concatenate(parts, axis=-1) * g


def _finish_kernel(raf_ref, rab_ref, rbf_ref, rbb_ref, rc_ref, rdf_ref, rdb_ref, ga_ref, gb_ref, u_ref, gd_ref,
                   mg_ref, x_ref, mod_ref, vec_ref, glu_w_ref, wb_ref, wo_ref, o_ref):
    vec = vec_ref[...]
    oa = _head_rms(raf_ref[...] + rab_ref[...], vec[0:1]) * _silu(ga_ref[...].astype(F32))
    ob = _head_rms(rbf_ref[...] + rbb_ref[...], vec[1:2]) * _silu(gb_ref[...].astype(F32))
    od = _head_rms(rdf_ref[...] + rdb_ref[...], vec[2:3]) * _silu(gd_ref[...].astype(F32))
    yc = rc_ref[...].T + vec[3:4] * u_ref[...].astype(F32)
    yc = 0.5 * yc * (1.0 + lax.erf(yc * (2.0 ** -0.5)))
    oc = yc * _sigmoid(_dot(yc.astype(BF16), glu_w_ref[...]) + vec[4:5])
    y2 = None
    for j, o in enumerate((oa, ob, oc, od)):
        gate2 = jnp.tanh(0.5 * mg_ref[:, j * D_MODEL:(j + 1) * D_MODEL].astype(F32)) + 1.0
        term = gate2 * _dot(o.astype(BF16), wb_ref[j])
        y2 = term if y2 is None else y2 + term
    mix2 = _dot(y2.astype(BF16), wo_ref[...])
    o_ref[...] = x_ref[...] + (0.5 * mod_ref[2:3, :]) * mix2


def _finish(raw_a, raw_b, raw_c, raw_d, p, x_all, mod, vec, glu_w, w_branch, w_out, layer, *, n_rows, n_batch,
            seq, tm=256):
    mrow = _mod_row_map(n_rows, tm, n_batch, seq)
    one = pl.Buffered(1)

    def pcol(cblock):
        return pl.BlockSpec((tm, BR), lambda i: (i, cblock))

    raw = pl.BlockSpec((tm, BR), lambda i: (i, 0))
    return pl.pallas_call(
        _finish_kernel,
        out_shape=jax.ShapeDtypeStruct((n_rows, D_MODEL), F32),
        grid=(n_rows // tm,),
        in_specs=[raw, raw, raw, raw, pl.BlockSpec((BR, tm), lambda i: (0, i)), raw, raw,
                  pcol(COL_HG), pcol(COL_RG), pcol(COL_SU), pcol(COL_MZ),
                  pl.BlockSpec((tm, N_BRANCH * D_MODEL), lambda i: (i, COL_MERGE)),
                  pl.BlockSpec((tm, D_MODEL), lambda i: (i, 0)),
                  pl.BlockSpec((None, SUBLANES, D_MODEL), lambda i: (mrow(i), 0, 0)),
                  pl.BlockSpec((None, SUBLANES, BR), lambda i: (layer, 0, 0), pipeline_mode=one),
                  pl.BlockSpec((None, BR, BR), lambda i: (layer, 0, 0), pipeline_mode=one),
                  pl.BlockSpec((None, N_BRANCH, BR, D_MODEL), lambda i: (layer, 0, 0, 0), pipeline_mode=one),
                  pl.BlockSpec((None, D_MODEL, D_MODEL), lambda i: (layer, 0, 0), pipeline_mode=one)],
        out_specs=pl.BlockSpec((tm, D_MODEL), lambda i: (i, 0)),
        compiler_params=_cparams(("parallel",)),
        name="finish",
    )(*raw_a, *raw_b, raw_c, *raw_d, p, p, p, p, p, x_all, mod, vec, glu_w, w_branch, w_out)


def _mlp_kernel(x_ref, mod_ref, g_ref, w1_ref, w2_ref, fin_ref, o_ref, xn_ref, acc_ref, *, final_norm):
    j = pl.program_id(1)

    @pl.when(j == 0)
    def _():
        h = _norm_mod(x_ref[...], g_ref[...], mod_ref[3:4, :], mod_ref[4:5, :])
        xn_ref[...] = h.astype(BF16)
        acc_ref[...] = jnp.zeros_like(acc_ref)

    a = jnp.square(jnp.maximum(_dot(xn_ref[...], w1_ref[...]), 0.0))
    acc_ref[...] += _dot(a.astype(BF16), w2_ref[...])

    @pl.when(j == pl.num_programs(1) - 1)
    def _():
        y = x_ref[...] + mod_ref[5:6, :] * acc_ref[...]
        if final_norm:
            y = (y * lax.rsqrt(jnp.mean(jnp.square(y), axis=-1, keepdims=True) + NORM_EPS)) * fin_ref[...]
        o_ref[...] = y


def _mlp(x_all, mod, norm_w, w1, w2, fin_w, layer, *, n_rows, n_batch, seq, final_norm, tm=512, tf=1024):
    mrow = _mod_row_map(n_rows, tm, n_batch, seq)
    assert D_FF % tf == 0, tf
    kern = functools.partial(_mlp_kernel, final_norm=final_norm)
    return pl.pallas_call(
        kern,
        out_shape=jax.ShapeDtypeStruct((n_rows, D_MODEL), F32),
        grid=(n_rows // tm, D_FF // tf),
        in_specs=[pl.BlockSpec((tm, D_MODEL), lambda i, j: (i, 0)),
                  pl.BlockSpec((None, SUBLANES, D_MODEL), lambda i, j: (mrow(i), 0, 0)),
                  pl.BlockSpec((None, 1, D_MODEL), lambda i, j: (layer, 0, 0)),
                  pl.BlockSpec((None, D_MODEL, tf), lambda i, j: (layer, 0, j)),
                  pl.BlockSpec((None, tf, D_MODEL), lambda i, j: (layer, j, 0)),
                  pl.BlockSpec((1, D_MODEL), lambda i, j: (0, 0))],
        out_specs=pl.BlockSpec((tm, D_MODEL), lambda i, j: (i, 0)),
        scratch_shapes=[pltpu.VMEM((tm, D_MODEL), BF16), pltpu.VMEM((tm, D_MODEL), F32)],
        compiler_params=_cparams(("parallel", "arbitrary")),
        name="mlp",
    )(x_all, mod, norm_w, w1, w2, fin_w.reshape(1, D_MODEL))


W_PREP_ROWS = 1024
N_MERGE_BLK = N_BRANCH * D_MODEL // W_PREP_ROWS
MERGE_OFF = GATE_OFF + 4 * NH


def _w_prep_kernel(a_ref, g_ref, main_ref, gate_ref):
    @pl.when(pl.program_id(1) == 0)
    def _():
        gate_ref[...] = jnp.zeros_like(gate_ref)
        gate_ref[0:4 * NH, :] = g_ref[0].astype(BF16)

    main_ref[...] = a_ref[0].astype(BF16)


def _split_w_in(w):
    depth = w.shape[0]
    wt = jnp.swapaxes(w, 1, 2)
    tr = W_PREP_ROWS

    def a_map(l, j):
        row = jnp.where(j < N_MERGE_BLK, MERGE_OFF + j * tr, (j - N_MERGE_BLK) * tr)
        return l, pl.multiple_of(row, 16), 0

    return pl.pallas_call(
        _w_prep_kernel,
        out_shape=(jax.ShapeDtypeStruct((depth, P_WIDTH, D_MODEL), BF16),
                   jax.ShapeDtypeStruct((depth, GATE_PAD, D_MODEL), BF16)),
        grid=(depth, P_WIDTH // tr),
        in_specs=[pl.BlockSpec((pl.Element(1), pl.Element(tr), pl.Element(D_MODEL)), a_map),
                  pl.BlockSpec((pl.Element(1), pl.Element(4 * NH), pl.Element(D_MODEL)),
                               lambda l, j: (l, GATE_OFF, 0))],
        out_specs=(pl.BlockSpec((None, tr, D_MODEL), lambda l, j: (l, j, 0)),
                   pl.BlockSpec((None, GATE_PAD, D_MODEL), lambda l, j: (l, 0, 0))),
        compiler_params=_cparams(("parallel", "arbitrary")),
        name="w_in_prep",
    )(wt, wt)


def kernel(x, c, ctx, c_ctx, w_mod, b_mod, norm_mix, norm_mlp, w_in, hgrn_lb_logits, hgrn_norm, ret_decay, ret_norm, s5_a_re, s5_a_im, s5_log_dt, s5_b_re, s5_b_im, s5_c_re, s5_c_im, s5_d, s5_glu_w, s5_glu_b, mlstm_conv_w, mlstm_conv_b, mlstm_gate_b, mlstm_norm, w_branch, w_out, w_ff1, w_ff2, final_norm):
    n_batch, seq, _ = x.shape
    ctx_len = ctx.shape[1]
    depth = w_in.shape[0]
    nl = n_batch * seq
    dims = dict(n_batch=n_batch, seq=seq)

    p_lb = jax.nn.softmax(hgrn_lb_logits.astype(F32), axis=0)
    lower_bounds = (jnp.cumsum(p_lb, axis=0) - p_lb[0]).reshape(depth, 2, 1, BR)
    lg = jnp.log1p(-jnp.exp(ret_decay.astype(F32)))
    gate_b = jnp.pad(mlstm_gate_b.reshape(depth, 1, 4 * NH).astype(F32), ((0, 0), (0, 0), (0, GATE_PAD - 4 * NH)))
    vec = jnp.stack([hgrn_norm, ret_norm, mlstm_norm, s5_d, s5_glu_b], axis=1).astype(F32)
    vec = jnp.pad(vec, ((0, 0), (0, SUBLANES - vec.shape[1]), (0, 0)))
    w_main, w_gate = _split_w_in(w_in)
    glu_w, wb, wo = s5_glu_w.astype(BF16), w_branch.astype(BF16), w_out.astype(BF16)
    w1, w2 = w_ff1.astype(BF16), w_ff2.astype(BF16)
    norm_mix3 = norm_mix.reshape(depth, 1, D_MODEL)
    norm_mlp3 = norm_mlp.reshape(depth, 1, D_MODEL)
    conv_w = mlstm_conv_w.reshape(depth, 9, 2 * BR)
    conv_b = mlstm_conv_b.reshape(depth, 1, 2 * BR)
    s5_col, s5_row = _s5_params(s5_a_re, s5_a_im, s5_log_dt, s5_b_re, s5_b_im, s5_c_re, s5_c_im)

    assert n_batch + 1 <= SUBLANES
    c_pad = jnp.zeros((SUBLANES, D_MODEL), F32).at[:n_batch].set(c).at[n_batch].set(c_ctx)
    x_all = jnp.concatenate([x.reshape(nl, D_MODEL), ctx.reshape(n_batch * ctx_len, D_MODEL)], axis=0)

    mod_all = _modulation(c_pad, w_mod, b_mod)[:, :n_batch + 1].reshape(depth, n_batch + 1, N_MOD, D_MODEL)
    mod_all = jnp.pad(mod_all, ((0, 0), (0, 0), (0, SUBLANES - N_MOD), (0, 0)))
    for l in range(depth):
        last = l == depth - 1
        mod = mod_all[l]
        p, gates, ut = _in_proj(x_all, mod, norm_mix3, w_main, w_gate, l, **dims)
        qk = _mlstm_conv(p, conv_w, conv_b, l, ctx_len=ctx_len, **dims)
        raw_a, raw_b, raw_d = _scans(p, gates, qk, lg[l], lower_bounds, gate_b, l, ctx_len=ctx_len, **dims)
        raw_c = _s5_scan(ut, s5_col, s5_row, l, ctx_len=ctx_len, **dims)
        n_rows = nl if last else x_all.shape[0]
        x_mid = _finish(raw_a, raw_b, raw_c, raw_d, p, x_all, mod, vec, glu_w, wb, wo, l, n_rows=n_rows, **dims)
        x_all = _mlp(x_mid, mod, norm_mlp3, w1, w2, final_norm, l, n_rows=n_rows, final_norm=last, **dims)
    return x_all.reshape(n_batch, seq, D_MODEL)
```

```python
import functools

import jax
import jax.numpy as jnp
from jax import lax
from jax.experimental import pallas as pl
from jax.experimental.pallas import tpu as pltpu

F32 = jnp.float32
BF16 = jnp.bfloat16

D_MODEL = 2048
N_BRANCH = 4
BR = D_MODEL // N_BRANCH
HD = 128
NH = BR // HD
S5_HG = 16
S5_G = BR // S5_HG
S5_P = 64
D_FF = 4 * D_MODEL
N_MOD = 6
GRID_W = 64
NORM_EPS = 1e-6
NEG_BIG = -1e30
F_TINY = 1e-30

COL_MERGE = 0
COL_HQ, COL_HF, COL_HV, COL_HG = 16, 17, 19, 20
COL_RQ, COL_RK, COL_RV, COL_RG = 21, 22, 23, 24
COL_SU = 25
COL_MQK, COL_MV, COL_MZ = 26, 28, 29
P_WIDTH = 30 * BR
GATE_OFF = 14 * BR
GATE_PAD = 128
SUBLANES = 8

T_HGRN = 128
T_ATTN = 256
S5_T = 128
S5_PP = 128
S5_ROWS = 40
VMEM_LIMIT = 56 * 1024 * 1024

def _cparams(sem):
    return pltpu.CompilerParams(dimension_semantics=sem, vmem_limit_bytes=VMEM_LIMIT)


def _dot(a, b):
    return jnp.dot(a, b, preferred_element_type=F32)


def _dot_nt(a, b):
    return lax.dot_general(a, b, (((1,), (1,)), ((), ())), preferred_element_type=F32)


def _dot_tn(a, b):
    return lax.dot_general(a, b, (((0,), (0,)), ((), ())), preferred_element_type=F32)


def _split3(x):
    hi = x.astype(BF16)
    r1 = x - hi.astype(F32)
    mid = r1.astype(BF16)
    return hi, mid, (r1 - mid.astype(F32)).astype(BF16)


def _split2(x):
    hi = x.astype(BF16)
    return hi, (x - hi.astype(F32)).astype(BF16)


def _cumsum_rows(tri, x):
    hi, mid = _split2(x)
    return _dot(tri, hi) + _dot(tri, mid)


def _sigmoid(x):
    return 1.0 / (1.0 + jnp.exp(-x))


def _silu(x):
    return x * _sigmoid(x)


def _mod_kernel(c_ref, w_ref, b_ref, o_ref):
    @pl.when(pl.program_id(1) == 0)
    def _():
        o_ref[...] = jnp.broadcast_to(b_ref[...], o_ref.shape)

    a = _silu(c_ref[...]).astype(BF16)
    o_ref[...] += _dot(a, w_ref[...].astype(BF16))


def _modulation(c_pad, w_mod, b_mod, tk=256):
    depth, _, n = w_mod.shape
    return pl.pallas_call(
        _mod_kernel,
        out_shape=jax.ShapeDtypeStruct((depth, SUBLANES, n), F32),
        grid=(depth, D_MODEL // tk),
        in_specs=[pl.BlockSpec((SUBLANES, tk), lambda l, k: (0, k)),
                  pl.BlockSpec((None, tk, n), lambda l, k: (l, k, 0)),
                  pl.BlockSpec((None, 1, n), lambda l, k: (l, 0, 0))],
        out_specs=pl.BlockSpec((None, SUBLANES, n), lambda l, k: (l, 0, 0)),
        compiler_params=_cparams(("parallel", "arbitrary")),
        name="modulation",
    )(c_pad, w_mod, b_mod.reshape(depth, 1, n))


def _norm_mod(x, g, shift, scale):
    y = x * lax.rsqrt(jnp.mean(jnp.square(x), axis=-1, keepdims=True) + NORM_EPS)
    return y * (g * (1.0 + scale)) + shift


def _inproj_kernel(x_ref, mod_ref, g_ref, w_ref, wg_ref, p_ref, gate_ref, ut_ref, xn_ref, *, su_block, su_off):
    @pl.when(pl.program_id(1) == 0)
    def _():
        h = _norm_mod(x_ref[...], g_ref[...], mod_ref[0:1, :], mod_ref[1:2, :])
        xn_ref[...] = h.astype(BF16)
        gate_ref[...] = _dot_nt(xn_ref[...], wg_ref[...])

    acc = _dot_nt(xn_ref[...], w_ref[...])
    p_ref[...] = acc.astype(p_ref.dtype)

    @pl.when(pl.program_id(1) == su_block)
    def _():
        ut_ref[...] = acc[:, su_off:su_off + BR].T.astype(ut_ref.dtype)


def _mod_row_map(n_rows, tm, n_batch, seq):
    assert n_rows % tm == 0 and seq % tm == 0, (n_rows, seq, tm)
    n_lat_tiles, tiles_per_batch = n_batch * seq // tm, seq // tm

    def row(i):
        return jnp.where(i < n_lat_tiles, i // tiles_per_batch, n_batch)
    return row


def _in_proj(x_all, mod, norm_w, w_main, w_gate, layer, *, n_batch, seq, tm=1024, tn=1536):
    n = x_all.shape[0]
    assert P_WIDTH % tn == 0, tn
    su_block, su_off = divmod(COL_SU * BR, tn)
    assert su_off + BR <= tn, (su_off, tn)
    mrow = _mod_row_map(n, tm, n_batch, seq)
    return pl.pallas_call(
        functools.partial(_inproj_kernel, su_block=su_block, su_off=su_off),
        out_shape=(jax.ShapeDtypeStruct((n, P_WIDTH), BF16),
                   jax.ShapeDtypeStruct((n, GATE_PAD), F32),
                   jax.ShapeDtypeStruct((BR, n), BF16)),
        grid=(n // tm, P_WIDTH // tn),
        in_specs=[pl.BlockSpec((tm, D_MODEL), lambda i, j: (i, 0)),
                  pl.BlockSpec((None, SUBLANES, D_MODEL), lambda i, j: (mrow(i), 0, 0)),
                  pl.BlockSpec((None, 1, D_MODEL), lambda i, j: (layer, 0, 0)),
                  pl.BlockSpec((None, tn, D_MODEL), lambda i, j: (layer, j, 0)),
                  pl.BlockSpec((None, GATE_PAD, D_MODEL), lambda i, j: (layer, 0, 0))],
        out_specs=(pl.BlockSpec((tm, tn), lambda i, j: (i, j)),
                   pl.BlockSpec((tm, GATE_PAD), lambda i, j: (i, 0)),
                   pl.BlockSpec((BR, tm), lambda i, j: (0, i))),
        scratch_shapes=[pltpu.VMEM((tm, D_MODEL), BF16)],
        compiler_params=_cparams(("parallel", "arbitrary")),
        name="in_proj",
    )(x_all, mod, norm_w, w_main, w_gate)


CONV_ROWS = 1024


def _conv_kernel(up_ref, x_ref, dn_ref, w_ref, b_ref, o_ref, *, n_lat_blocks, blocks_per_img, ctx_len):
    rb = pl.program_id(0)
    x = x_ref[...].astype(F32)
    w = w_ref[...]
    n = CONV_ROWS

    @pl.when(rb < n_lat_blocks)
    def _():
        rr = rb % blocks_per_img
        up = jnp.where(rr > 0, up_ref[...].astype(F32), 0.0)
        dn = jnp.where(rr < blocks_per_img - 1, dn_ref[...].astype(F32), 0.0)
        ext = jnp.concatenate([up, x, dn], axis=0)
        ne = n + 2 * GRID_W
        col = lax.broadcasted_iota(jnp.int32, (ne, 1), 0) % GRID_W
        xl = jnp.where(col >= 1, pltpu.roll(ext, 1, 0), 0.0)
        xr = jnp.where(col <= GRID_W - 2, pltpu.roll(ext, ne - 1, 0), 0.0)
        acc = None
        for i in range(3):
            y = w[3 * i:3 * i + 1] * xl + w[3 * i + 1:3 * i + 2] * ext + w[3 * i + 2:3 * i + 3] * xr
            part = y[i * GRID_W:i * GRID_W + n]
            acc = part if acc is None else acc + part
        o_ref[...] = _silu(acc + b_ref[...]).astype(o_ref.dtype)

    @pl.when(rb >= n_lat_blocks)
    def _():
        t = lax.broadcasted_iota(jnp.int32, (n, 1), 0) % ctx_len
        xl = jnp.where(t >= 1, pltpu.roll(x, 1, 0), 0.0)
        xr = jnp.where(t <= ctx_len - 2, pltpu.roll(x, n - 1, 0), 0.0)
        acc = w[3:4] * xl + w[4:5] * x + w[5:6] * xr
        o_ref[...] = _silu(acc + b_ref[...]).astype(o_ref.dtype)


def _mlstm_conv(p, conv_w, conv_b, layer, *, n_batch, seq, ctx_len, cb=2 * BR):
    assert CONV_ROWS % ctx_len == 0 and seq % CONV_ROWS == 0 and (n_batch * ctx_len) % CONV_ROWS == 0
    n = p.shape[0]
    n_blocks = n // CONV_ROWS
    hb = CONV_ROWS // GRID_W
    n_halo = n // GRID_W
    c0 = COL_MQK * BR // cb
    kern = functools.partial(_conv_kernel, n_lat_blocks=n_batch * seq // CONV_ROWS,
                             blocks_per_img=seq // CONV_ROWS, ctx_len=ctx_len)
    return pl.pallas_call(
        kern,
        out_shape=jax.ShapeDtypeStruct((n, 2 * BR), BF16),
        grid=(n_blocks, 2 * BR // cb),
        in_specs=[pl.BlockSpec((GRID_W, cb), lambda r, c: (jnp.maximum(r * hb - 1, 0), c0 + c)),
                  pl.BlockSpec((CONV_ROWS, cb), lambda r, c: (r, c0 + c)),
                  pl.BlockSpec((GRID_W, cb), lambda r, c: (jnp.minimum((r + 1) * hb, n_halo - 1), c0 + c)),
                  pl.BlockSpec((None, 9, cb), lambda r, c: (layer, 0, c)),
                  pl.BlockSpec((None, 1, cb), lambda r, c: (layer, 0, c))],
        out_specs=pl.BlockSpec((CONV_ROWS, cb), lambda r, c: (r, c)),
        compiler_params=_cparams(("parallel", "parallel")),
        name="mlstm_conv",
    )(p, p, p, conv_w, conv_b)


def _chunk_block(b, d, i, *, n_ctx, n_lat, n_batch):
    ctx_j = i if d == 0 else n_ctx - 1 - i
    lat_j = i - n_ctx if d == 0 else n_ctx + n_lat - 1 - i
    return jnp.where(i < n_ctx, n_batch * n_lat + b * n_ctx + ctx_j, b * n_lat + lat_j)


def _init_at_first_step(*refs):
    @pl.when(pl.program_id(1) == 0)
    def _():
        for ref in refs:
            ref[...] = jnp.zeros_like(ref)


def _ret_direction(d, lg_ref, q_ref, k_ref, v_ref, o_ref, s_ref):
    t = T_ATTN
    r = lax.broadcasted_iota(jnp.int32, (t, t), 0)
    c = lax.broadcasted_iota(jnp.int32, (t, t), 1)
    rel = (r - c) if d == 0 else (c - r)
    relf = jnp.maximum(rel, 0).astype(F32)
    tt = lax.broadcasted_iota(jnp.int32, (t, 1), 0)
    pos = (tt if d == 0 else t - 1 - tt).astype(F32)
    scale = HD ** -0.5
    for h in range(NH):
        sl = slice(h * HD, (h + 1) * HD)
        lg = lg_ref[d, h]
        q = q_ref[:, sl]
        k = k_ref[:, sl]
        v = v_ref[:, sl]
        intra = jnp.where(rel >= 0, jnp.exp(lg * relf), 0.0)
        scores = (_dot_nt(q, k) * scale) * intra
        s_old = s_ref[d, h]
        out = _dot(scores.astype(BF16), v) + jnp.exp(lg * (pos + 1.0)) * _dot(q, s_old.astype(BF16))
        kd = (k.astype(F32) * (scale * jnp.exp(lg * (t - 1.0 - pos)))).astype(BF16)
        s_ref[d, h] = jnp.exp(lg * t) * s_old + _dot_tn(kd, v)
        o_ref[:, sl] = out


def _hgrn_direction(d, lb_ref, q_ref, z_ref, v_ref, o_ref, st_ref):
    t = T_HGRN
    fwd = d == 0
    lb = lb_ref[...]
    z = z_ref[...].astype(F32)
    sig = _sigmoid(z)
    f = lb + (1.0 - lb) * sig
    logf = jnp.log(jnp.maximum(f, F_TINY))
    kk = (1.0 - lb) * (1.0 - sig)
    qq = _silu(q_ref[...].astype(F32))
    r2 = lax.broadcasted_iota(jnp.int32, (t, t), 0)
    c2 = lax.broadcasted_iota(jnp.int32, (t, t), 1)
    tri = jnp.where((c2 <= r2) if fwd else (c2 >= r2), 1.0, 0.0).astype(BF16)
    cum = _cumsum_rows(tri, logf)
    xdiff = jnp.where((c2 < r2) if fwd else (c2 > r2), r2 ^ c2, 0)
    row = lax.broadcasted_iota(jnp.int32, (t, 1), 0)
    if fwd:
        e_ref = jnp.where(row == 0, 0.0, pltpu.roll(cum, 1, 0))
    else:
        e_ref = jnp.where(row == t - 1, 0.0, pltpu.roll(cum, t - 1, 0))
    f_ref = cum
    levels = []
    w = 1
    while w < t:
        if w == 1:
            q_l, k_l = (qq * jnp.maximum(f, F_TINY)).astype(BF16), kk.astype(BF16)
        else:
            q_l = (qq * jnp.exp(cum - e_ref)).astype(BF16)
            k_l = (kk * jnp.exp(f_ref - cum)).astype(BF16)
        levels.append((q_l, k_l, (xdiff >> (w.bit_length() - 1)) == 1))
        upper = (row & w) != 0
        if fwd:
            e_ref = jnp.where(upper, pltpu.roll(e_ref, w, 0), e_ref)
            f_ref = jnp.where(upper, f_ref, pltpu.roll(f_ref, t - w, 0))
        else:
            e_ref = jnp.where(upper, e_ref, pltpu.roll(e_ref, t - w, 0))
            f_ref = jnp.where(upper, pltpu.roll(f_ref, w, 0), f_ref)
        w *= 2
    q_b = qq.astype(BF16)
    k_b = kk.astype(BF16)
    tail = cum[t - 1:t] if fwd else cum[0:1]
    q_state = (qq * jnp.exp(cum)).astype(BF16)
    k_state = (kk * jnp.exp(tail - cum)).astype(BF16)
    decay = jnp.exp(tail)
    for h in range(NH):
        sl = slice(h * HD, (h + 1) * HD)
        v_h = v_ref[:, sl]
        scores = jnp.where(r2 == c2, _dot_nt(q_b[:, sl], k_b[:, sl]), 0.0)
        for q_l, k_l, m in levels:
            scores = jnp.where(m, _dot_nt(q_l[:, sl], k_l[:, sl]), scores)
        st = st_ref[d, h]
        o_ref[:, sl] = _dot(scores.astype(BF16), v_h) + _dot_nt(q_state[:, sl], st.astype(BF16))
        st_ref[d, h] = st * decay[:, sl] + _dot_tn(v_h, k_state[:, sl])


def _hgrn_kernel(lbf_ref, qf_ref, zf_ref, vf_ref, lbb_ref, qb_ref, zb_ref, vb_ref, of_ref, ob_ref, st_ref):
    _init_at_first_step(st_ref)
    _hgrn_direction(0, lbf_ref, qf_ref, zf_ref, vf_ref, of_ref, st_ref)
    _hgrn_direction(1, lbb_ref, qb_ref, zb_ref, vb_ref, ob_ref, st_ref)


def _log_sigmoid(x):
    return jnp.minimum(x, 0.0) - jnp.log(1.0 + jnp.exp(-jnp.abs(x)))


def _rep_lane(parts, lane, width):
    sel = jnp.where(lax.broadcasted_iota(jnp.int32, (GATE_PAD, width), 0) == lane, 1.0, 0.0).astype(BF16)
    hi, mid = parts
    return _dot(hi, sel) + _dot(mid, sel)


def _mlstm_direction(d, qk_ref, v_ref, g_ref, gb_ref, o_ref, c_ref, m_ref):
    t = T_ATTN
    assert t == 2 * HD
    fwd = d == 0
    g = g_ref[...] + gb_ref[...]
    r2 = lax.broadcasted_iota(jnp.int32, (t, t), 0)
    c2 = lax.broadcasted_iota(jnp.int32, (t, t), 1)
    mask = (c2 <= r2) if fwd else (c2 >= r2)
    tri = jnp.where(mask, 1.0, 0.0).astype(BF16)
    eye = jnp.where(r2 == c2, 1.0, 0.0).astype(BF16)
    lf = _log_sigmoid(g)
    cum_c = _cumsum_rows(tri, lf)
    a_col = g - pltpu.roll(cum_c, GATE_PAD - NH, 1)
    row = lax.broadcasted_iota(jnp.int32, (t, 1), 0)
    pm = a_col
    k = 1
    while k < t:
        if fwd:
            pm = jnp.maximum(pm, jnp.where(row >= k, pltpu.roll(pm, k, 0), NEG_BIG))
        else:
            pm = jnp.maximum(pm, jnp.where(row < t - k, pltpu.roll(pm, t - k, 0), NEG_BIG))
        k *= 2
    a_parts, pm_parts, cum_parts = _split2(a_col), _split2(pm), _split2(cum_c)
    a_rows = _dot_tn(a_parts[0], eye) + _dot_tn(a_parts[1], eye)
    scale = HD ** -0.5
    last = t - 1 if fwd else 0
    ones = jnp.ones((t, HD), BF16)
    for h in range(NH):
        sl = slice(h * HD, (h + 1) * HD)
        li, lfw = 2 * NH * d + h, 2 * NH * d + NH + h
        m_old = m_ref[d, h]
        mx = jnp.maximum(_rep_lane(pm_parts, li, HD), m_old)
        mx2 = jnp.concatenate([mx, mx], axis=1)
        cc = _rep_lane(cum_parts, lfw, HD)
        a_rep = _rep_lane(a_parts, li, HD)
        a_row = a_rows[li:li + 1]
        q = qk_ref[:, sl]
        k_h = qk_ref[:, BR + h * HD:BR + (h + 1) * HD]
        v_aug = jnp.concatenate([v_ref[:, sl], ones], axis=1)
        w = jnp.where(mask, jnp.exp(a_row - mx2), 0.0)
        scores = (_dot_nt(q, k_h) * scale) * w
        w_state = jnp.exp(m_old - mx)
        c_old = c_ref[d, h]
        s_hi = scores.astype(BF16)
        s_mid = (scores - s_hi.astype(F32)).astype(BF16)
        intra = _dot(s_hi, v_aug)
        inter = _dot(q, c_old.astype(BF16))
        num = intra[:, 0:HD] + w_state * inter[:, 0:HD]
        den = (intra[:, HD:2 * HD] + _dot(s_mid, ones)) + w_state * inter[:, HD:2 * HD]
        o_ref[:, sl] = num / jnp.maximum(jnp.abs(den), jnp.exp(-(cc + mx)))
        total = cc[last:last + 1]
        m_new_rel = mx[last:last + 1]
        keep = jnp.exp(m_old - m_new_rel)
        w_end = jnp.exp(a_rep - m_new_rel)
        kw = (k_h.astype(F32) * (scale * w_end)).astype(BF16)
        c_ref[d, h] = jnp.concatenate([keep, keep], axis=1) * c_old + _dot_tn(kw, v_aug)
        m_ref[d, h] = total + m_new_rel


def _ret_mlstm_kernel(lg_ref, rqf_ref, rkf_ref, rvf_ref, rqb_ref, rkb_ref, rvb_ref,
                      qkf_ref, vf_ref, gf_ref, qkb_ref, vb_ref, gb_ref, bias_ref,
                      rof_ref, rob_ref, mof_ref, mob_ref, s_ref, c_ref, m_ref):
    _init_at_first_step(s_ref, c_ref, m_ref)
    _ret_direction(0, lg_ref, rqf_ref, rkf_ref, rvf_ref, rof_ref, s_ref)
    _mlstm_direction(0, qkf_ref, vf_ref, gf_ref, bias_ref, mof_ref, c_ref, m_ref)
    _ret_direction(1, lg_ref, rqb_ref, rkb_ref, rvb_ref, rob_ref, s_ref)
    _mlstm_direction(1, qkb_ref, vb_ref, gb_ref, bias_ref, mob_ref, c_ref, m_ref)


def _scans(p, gates, qk, lg, lb, gate_b, layer, *, n_batch, seq, ctx_len):
    n = p.shape[0]
    state = pltpu.VMEM((2, NH, HD, HD), F32)
    vec = pltpu.VMEM((2, NH, 1, HD), F32)

    def call(kernel, t, inputs, make_specs, scratch, name, n_mixers=1):
        assert ctx_len % t == 0 and seq % t == 0, (ctx_len, seq, t)
        n_ctx, n_lat = ctx_len // t, seq // t

        def rows(d, width, cblock=0):
            return pl.BlockSpec((t, width), lambda b, i: (_chunk_block(b, d, i, n_ctx=n_ctx, n_lat=n_lat,
                                                                        n_batch=n_batch), cblock))

        out = jax.ShapeDtypeStruct((n, BR), F32)
        return pl.pallas_call(
            kernel,
            out_shape=(out, out) * n_mixers,
            grid=(n_batch, n_ctx + n_lat),
            in_specs=make_specs(rows),
            out_specs=(rows(0, BR), rows(1, BR)) * n_mixers,
            scratch_shapes=scratch,
            compiler_params=_cparams(("parallel", "arbitrary")),
            name=name,
        )(*inputs)

    lb_spec = lambda d: pl.BlockSpec((None, None, 1, BR), lambda b, i: (layer, d, 0, 0))
    raw_a = call(_hgrn_kernel, T_HGRN, (lb, p, p, p, lb, p, p, p),
                 lambda rows: [spec for d in (0, 1)
                               for spec in (lb_spec(d), rows(d, BR, COL_HQ), rows(d, BR, COL_HF + d),
                                            rows(d, BR, COL_HV))],
                 [state], "hgrn2_scan")
    raw_bd = call(_ret_mlstm_kernel, T_ATTN, (lg, p, p, p, p, p, p, qk, p, gates, qk, p, gates, gate_b),
                  lambda rows: [pl.BlockSpec(memory_space=pltpu.SMEM)]
                  + [rows(d, BR, c) for d in (0, 1) for c in (COL_RQ, COL_RK, COL_RV)]
                  + [spec for d in (0, 1)
                     for spec in (rows(d, 2 * BR), rows(d, BR, COL_MV), rows(d, GATE_PAD))]
                  + [pl.BlockSpec((None, 1, GATE_PAD), lambda b, i: (layer, 0, 0))],
                  [state, pltpu.VMEM((2, NH, HD, 2 * HD), F32), vec], "retention_mlstm_scan", n_mixers=2)
    return raw_a, raw_bd[0:2], raw_bd[2:4]


def _cmul(ar, ai, br, bi):
    return ar * br - ai * bi, ar * bi + ai * br


def _pow_table(lam_r, lam_i, expo):
    pr = jnp.ones(expo.shape, F32)
    pi = jnp.zeros(expo.shape, F32)
    ar = jnp.broadcast_to(lam_r, expo.shape)
    ai = jnp.broadcast_to(lam_i, expo.shape)
    for k in range(S5_T.bit_length() - 1):
        nr, ni = _cmul(pr, pi, ar, ai)
        bit = (expo & (1 << k)) != 0
        pr, pi = jnp.where(bit, nr, pr), jnp.where(bit, ni, pi)
        ar, ai = _cmul(ar, ai, ar, ai)
    return pr, pi


def _s5_operators(d, u, col_ref, row_ref, m_ref, w_ref, ys_ref, z_ref, wr_ref, wi_ref):
    t = S5_T
    fwd = d == 0
    col = col_ref[d]
    rowp = row_ref[d]
    lam_rc, lam_ic = col[:, 0:1], col[:, 1:2]
    bbr, bbi = rowp[0:S5_HG], rowp[S5_HG:2 * S5_HG]
    lam_rr, lam_ir = rowp[2 * S5_HG:2 * S5_HG + 1], rowp[2 * S5_HG + 1:2 * S5_HG + 2]
    lane = lax.broadcasted_iota(jnp.int32, (t, t), 1)
    sub = lax.broadcasted_iota(jnp.int32, (t, t), 0)

    pr, pi = _pow_table(lam_rc, lam_ic, lane if fwd else t - 1 - lane)
    qr, qi = _cmul(pr, pi, lam_rc, lam_ic)
    for h in range(S5_HG):
        cr, ci = col[:, 2 + h:3 + h], col[:, 2 + S5_HG + h:3 + S5_HG + h]
        hs = slice(h * t, (h + 1) * t)
        z_ref[d, 0:S5_PP, hs] = cr * pr - ci * pi
        z_ref[d, S5_PP:2 * S5_PP, hs] = -(cr * pi + ci * pr)
        ys_ref[d, 0:S5_PP, hs] = (cr * qr - ci * qi).astype(BF16)
        ys_ref[d, S5_PP:2 * S5_PP, hs] = (-(cr * qi + ci * qr)).astype(BF16)
    b_hi, b_mid, _ = _split3(jnp.concatenate([bbr, bbi], axis=1))
    z_hi, z_mid, _ = _split3(z_ref[d])
    krow = _dot(b_hi, z_hi) + (_dot(b_hi, z_mid) + _dot(b_mid, z_hi))
    shift, keep = (0, lane >= sub) if fwd else (1, lane <= sub)
    for hp in range(S5_HG):
        for h in range(S5_HG):
            tile = jnp.broadcast_to(krow[hp:hp + 1, h * t:(h + 1) * t], (t, t))
            tile = pltpu.roll(tile, shift, 1, stride=1, stride_axis=0)
            m_ref[d, hp * t:(hp + 1) * t, h * t:(h + 1) * t] = jnp.where(keep, tile, 0.0).astype(BF16)
    tr, ti = _pow_table(lam_rr, lam_ir, t - 1 - sub if fwd else sub)
    for hp in range(S5_HG):
        br, bi = bbr[hp:hp + 1], bbi[hp:hp + 1]
        w_ref[d, hp * t:(hp + 1) * t, 0:S5_PP] = (tr * br - ti * bi).astype(BF16)
        w_ref[d, hp * t:(hp + 1) * t, S5_PP:2 * S5_PP] = (tr * bi + ti * br).astype(BF16)
    wv = _dot(u, w_ref[d])
    wr_ref[d] = wv[:, 0:S5_PP]
    wi_ref[d] = wv[:, S5_PP:2 * S5_PP]
    lr, li = lam_rr, lam_ir
    for _ in range(t.bit_length() - 1):
        lr, li = _cmul(lr, li, lr, li)
    return lr, li


def _s5_kernel(u_ref, col_ref, row_ref, o_ref, m_ref, w_ref, ys_ref, z_ref, wr_ref, wi_ref, xr_ref, xi_ref, *,
               n_batch, n_ctx, n_lat):
    t = S5_T
    u = jnp.concatenate([u_ref[hp] for hp in range(S5_HG)], axis=1)
    lam_t = [_s5_operators(d, u, col_ref, row_ref, m_ref, w_ref, ys_ref, z_ref, wr_ref, wi_ref) for d in (0, 1)]

    def make_step(base, stride, count):
        def step(i, carry):
            out = []
            for d in (0, 1):
                xr, xi = carry[2 * d], carry[2 * d + 1]
                j = i if d == 0 else count - 1 - i
                idx = pl.ds(base + j, n_batch, stride=stride)
                xr_ref[d, idx, :] = xr
                xi_ref[d, idx, :] = xi
                nr, ni = _cmul(xr, xi, *lam_t[d])
                out += [nr + wr_ref[d, idx, :], ni + wi_ref[d, idx, :]]
            return tuple(out)
        return step

    zero = jnp.zeros((n_batch, S5_PP), F32)
    carry = lax.fori_loop(0, n_ctx, make_step(n_batch * n_lat, n_ctx, n_ctx), (zero,) * 4)
    lax.fori_loop(0, n_lat, make_step(0, n_lat, n_lat), carry)
    y = None
    for d in (0, 1):
        x_prev = jnp.concatenate([xr_ref[d], xi_ref[d]], axis=1).astype(BF16)
        y_d = _dot(u, m_ref[d]) + _dot(x_prev, ys_ref[d])
        y = y_d if y is None else y + y_d
    for h in range(S5_HG):
        o_ref[h] = y[:, h * t:(h + 1) * t]


def _s5_params(a_re, a_im, log_dt, b_re, b_im, c_re, c_im):
    a_re, a_im = a_re.astype(F32), a_im.astype(F32)
    dt = jnp.exp(log_dt.astype(F32))[..., None]
    mag = jnp.exp(a_re * dt)
    lam_re, lam_im = mag * jnp.cos(a_im * dt), mag * jnp.sin(a_im * dt)
    den = a_re * a_re + a_im * a_im
    num_re, num_im = lam_re - 1.0, lam_im
    fr = (num_re * a_re + num_im * a_im) / den
    fi = (num_im * a_re - num_re * a_im) / den
    b_re, b_im = b_re.astype(F32), b_im.astype(F32)
    bb_re = fr[..., None] * b_re - fi[..., None] * b_im
    bb_im = fr[..., None] * b_im + fi[..., None] * b_re
    c_re_t = jnp.swapaxes(c_re.astype(F32), -1, -2)
    c_im_t = jnp.swapaxes(c_im.astype(F32), -1, -2)
    col = jnp.concatenate([lam_re[..., None], lam_im[..., None], c_re_t, c_im_t], axis=-1)
    col = jnp.pad(col, ((0, 0),) * 3 + ((0, S5_PP - S5_P), (0, 128 - col.shape[-1])))
    row = jnp.concatenate([jnp.swapaxes(bb_re, -1, -2), jnp.swapaxes(bb_im, -1, -2),
                           lam_re[..., None, :], lam_im[..., None, :]], axis=-2)
    row = jnp.pad(row, ((0, 0),) * 3 + ((0, S5_ROWS - row.shape[-2]), (0, S5_PP - S5_P)))
    return col, row


def _s5_scan(ut, col, row, layer, *, n_batch, seq, ctx_len):
    t = S5_T
    n = ut.shape[1]
    nc = n // t
    assert ctx_len % t == 0 and seq % t == 0, (ctx_len, seq, t)
    n_ctx, n_lat = ctx_len // t, seq // t
    ut = ut.reshape(S5_G, S5_HG, nc, t)
    kern = functools.partial(_s5_kernel, n_batch=n_batch, n_ctx=n_ctx, n_lat=n_lat)
    yt = pl.pallas_call(
        kern,
        out_shape=jax.ShapeDtypeStruct((S5_G, S5_HG, nc, t), F32),
        grid=(S5_G,),
        in_specs=[pl.BlockSpec((None, S5_HG, nc, t), lambda g: (g, 0, 0, 0)),
                  pl.BlockSpec((None, 2, None, S5_PP, 128), lambda g: (layer, 0, g, 0, 0)),
                  pl.BlockSpec((None, 2, None, S5_ROWS, S5_PP), lambda g: (layer, 0, g, 0, 0))],
        out_specs=pl.BlockSpec((None, S5_HG, nc, t), lambda g: (g, 0, 0, 0)),
        scratch_shapes=[pltpu.VMEM((2, S5_HG * t, S5_HG * t), BF16),
                        pltpu.VMEM((2, S5_HG * t, 2 * S5_PP), BF16),
                        pltpu.VMEM((2, 2 * S5_PP, S5_HG * t), BF16),
                        pltpu.VMEM((2, 2 * S5_PP, S5_HG * t), F32),
                        pltpu.VMEM((2, nc, S5_PP), F32), pltpu.VMEM((2, nc, S5_PP), F32),
                        pltpu.VMEM((2, nc, S5_PP), F32), pltpu.VMEM((2, nc, S5_PP), F32)],
        compiler_params=_cparams(("parallel",)),
        name="s5_scan",
    )(ut, col, row)
    return yt.reshape(BR, n)


def _head_rms(x, g):
    parts = []
    for h in range(NH):
        xh = x[:, h * HD:(h + 1) * HD]
        parts.append(xh * lax.rsqrt(jnp.mean(jnp.square(xh), axis=-1, keepdims=True) + NORM_EPS))
    return jnp.concatenate(parts, axis=-1) * g


def _finish_kernel(raf_ref, rab_ref, rbf_ref, rbb_ref, rc_ref, rdf_ref, rdb_ref, ga_ref, gb_ref, u_ref, gd_ref,
                   mg_ref, x_ref, mod_ref, vec_ref, glu_w_ref, wb_ref, wo_ref, o_ref):
    vec = vec_ref[...]
    oa = _head_rms(raf_ref[...] + rab_ref[...], vec[0:1]) * _silu(ga_ref[...].astype(F32))
    ob = _head_rms(rbf_ref[...] + rbb_ref[...], vec[1:2]) * _silu(gb_ref[...].astype(F32))
    od = _head_rms(rdf_ref[...] + rdb_ref[...], vec[2:3]) * _silu(gd_ref[...].astype(F32))
    yc = rc_ref[...].T + vec[3:4] * u_ref[...].astype(F32)
    yc = 0.5 * yc * (1.0 + lax.erf(yc * (2.0 ** -0.5)))
    oc = yc * _sigmoid(_dot(yc.astype(BF16), glu_w_ref[...]) + vec[4:5])
    y2 = None
    for j, o in enumerate((oa, ob, oc, od)):
        gate2 = jnp.tanh(0.5 * mg_ref[:, j * D_MODEL:(j + 1) * D_MODEL].astype(F32)) + 1.0
        term = gate2 * _dot(o.astype(BF16), wb_ref[j])
        y2 = term if y2 is None else y2 + term
    mix2 = _dot(y2.astype(BF16), wo_ref[...])
    o_ref[...] = x_ref[...] + (0.5 * mod_ref[2:3, :]) * mix2


def _finish(raw_a, raw_b, raw_c, raw_d, p, x_all, mod, vec, glu_w, w_branch, w_out, layer, *, n_rows, n_batch,
            seq, tm=256):
    mrow = _mod_row_map(n_rows, tm, n_batch, seq)
    one = pl.Buffered(1)

    def pcol(cblock):
        return pl.BlockSpec((tm, BR), lambda i: (i, cblock))

    raw = pl.BlockSpec((tm, BR), lambda i: (i, 0))
    return pl.pallas_call(
        _finish_kernel,
        out_shape=jax.ShapeDtypeStruct((n_rows, D_MODEL), F32),
        grid=(n_rows // tm,),
        in_specs=[raw, raw, raw, raw, pl.BlockSpec((BR, tm), lambda i: (0, i)), raw, raw,
                  pcol(COL_HG), pcol(COL_RG), pcol(COL_SU), pcol(COL_MZ),
                  pl.BlockSpec((tm, N_BRANCH * D_MODEL), lambda i: (i, COL_MERGE)),
                  pl.BlockSpec((tm, D_MODEL), lambda i: (i, 0)),
                  pl.BlockSpec((None, SUBLANES, D_MODEL), lambda i: (mrow(i), 0, 0)),
                  pl.BlockSpec((None, SUBLANES, BR), lambda i: (layer, 0, 0), pipeline_mode=one),
                  pl.BlockSpec((None, BR, BR), lambda i: (layer, 0, 0), pipeline_mode=one),
                  pl.BlockSpec((None, N_BRANCH, BR, D_MODEL), lambda i: (layer, 0, 0, 0), pipeline_mode=one),
                  pl.BlockSpec((None, D_MODEL, D_MODEL), lambda i: (layer, 0, 0), pipeline_mode=one)],
        out_specs=pl.BlockSpec((tm, D_MODEL), lambda i: (i, 0)),
        compiler_params=_cparams(("parallel",)),
        name="finish",
    )(*raw_a, *raw_b, raw_c, *raw_d, p, p, p, p, p, x_all, mod, vec, glu_w, w_branch, w_out)


def _mlp_kernel(x_ref, mod_ref, g_ref, w1_ref, w2_ref, fin_ref, o_ref, xn_ref, acc_ref, *, final_norm):
    j = pl.program_id(1)

    @pl.when(j == 0)
    def _():
        h = _norm_mod(x_ref[...], g_ref[...], mod_ref[3:4, :], mod_ref[4:5, :])
        xn_ref[...] = h.astype(BF16)
        acc_ref[...] = jnp.zeros_like(acc_ref)

    a = jnp.square(jnp.maximum(_dot(xn_ref[...], w1_ref[...]), 0.0))
    acc_ref[...] += _dot(a.astype(BF16), w2_ref[...])

    @pl.when(j == pl.num_programs(1) - 1)
    def _():
        y = x_ref[...] + mod_ref[5:6, :] * acc_ref[...]
        if final_norm:
            y = (y * lax.rsqrt(jnp.mean(jnp.square(y), axis=-1, keepdims=True) + NORM_EPS)) * fin_ref[...]
        o_ref[...] = y


def _mlp(x_all, mod, norm_w, w1, w2, fin_w, layer, *, n_rows, n_batch, seq, final_norm, tm=512, tf=1024):
    mrow = _mod_row_map(n_rows, tm, n_batch, seq)
    assert D_FF % tf == 0, tf
    kern = functools.partial(_mlp_kernel, final_norm=final_norm)
    return pl.pallas_call(
        kern,
        out_shape=jax.ShapeDtypeStruct((n_rows, D_MODEL), F32),
        grid=(n_rows // tm, D_FF // tf),
        in_specs=[pl.BlockSpec((tm, D_MODEL), lambda i, j: (i, 0)),
                  pl.BlockSpec((None, SUBLANES, D_MODEL), lambda i, j: (mrow(i), 0, 0)),
                  pl.BlockSpec((None, 1, D_MODEL), lambda i, j: (layer, 0, 0)),
                  pl.BlockSpec((None, D_MODEL, tf), lambda i, j: (layer, 0, j)),
                  pl.BlockSpec((None, tf, D_MODEL), lambda i, j: (layer, j, 0)),
                  pl.BlockSpec((1, D_MODEL), lambda i, j: (0, 0))],
        out_specs=pl.BlockSpec((tm, D_MODEL), lambda i, j: (i, 0)),
        scratch_shapes=[pltpu.VMEM((tm, D_MODEL), BF16), pltpu.VMEM((tm, D_MODEL), F32)],
        compiler_params=_cparams(("parallel", "arbitrary")),
        name="mlp",
    )(x_all, mod, norm_w, w1, w2, fin_w.reshape(1, D_MODEL))


W_PREP_ROWS = 1024
N_MERGE_BLK = N_BRANCH * D_MODEL // W_PREP_ROWS
MERGE_OFF = GATE_OFF + 4 * NH


def _w_prep_kernel(a_ref, g_ref, main_ref, gate_ref):
    @pl.when(pl.program_id(1) == 0)
    def _():
        gate_ref[...] = jnp.zeros_like(gate_ref)
        gate_ref[0:4 * NH, :] = g_ref[0].astype(BF16)

    main_ref[...] = a_ref[0].astype(BF16)


def _split_w_in(w):
    depth = w.shape[0]
    wt = jnp.swapaxes(w, 1, 2)
    tr = W_PREP_ROWS

    def a_map(l, j):
        row = jnp.where(j < N_MERGE_BLK, MERGE_OFF + j * tr, (j - N_MERGE_BLK) * tr)
        return l, pl.multiple_of(row, 16), 0

    return pl.pallas_call(
        _w_prep_kernel,
        out_shape=(jax.ShapeDtypeStruct((depth, P_WIDTH, D_MODEL), BF16),
                   jax.ShapeDtypeStruct((depth, GATE_PAD, D_MODEL), BF16)),
        grid=(depth, P_WIDTH // tr),
        in_specs=[pl.BlockSpec((pl.Element(1), pl.Element(tr), pl.Element(D_MODEL)), a_map),
                  pl.BlockSpec((pl.Element(1), pl.Element(4 * NH), pl.Element(D_MODEL)),
                               lambda l, j: (l, GATE_OFF, 0))],
        out_specs=(pl.BlockSpec((None, tr, D_MODEL), lambda l, j: (l, j, 0)),
                   pl.BlockSpec((None, GATE_PAD, D_MODEL), lambda l, j: (l, 0, 0))),
        compiler_params=_cparams(("parallel", "arbitrary")),
        name="w_in_prep",
    )(wt, wt)


def kernel(x, c, ctx, c_ctx, w_mod, b_mod, norm_mix, norm_mlp, w_in, hgrn_lb_logits, hgrn_norm, ret_decay, ret_norm, s5_a_re, s5_a_im, s5_log_dt, s5_b_re, s5_b_im, s5_c_re, s5_c_im, s5_d, s5_glu_w, s5_glu_b, mlstm_conv_w, mlstm_conv_b, mlstm_gate_b, mlstm_norm, w_branch, w_out, w_ff1, w_ff2, final_norm):
    n_batch, seq, _ = x.shape
    ctx_len = ctx.shape[1]
    depth = w_in.shape[0]
    nl = n_batch * seq
    dims = dict(n_batch=n_batch, seq=seq)

    p_lb = jax.nn.softmax(hgrn_lb_logits.astype(F32), axis=0)
    lower_bounds = (jnp.cumsum(p_lb, axis=0) - p_lb[0]).reshape(depth, 2, 1, BR)
    lg = jnp.log1p(-jnp.exp(ret_decay.astype(F32)))
    gate_b = jnp.pad(mlstm_gate_b.reshape(depth, 1, 4 * NH).astype(F32), ((0, 0), (0, 0), (0, GATE_PAD - 4 * NH)))
    vec = jnp.stack([hgrn_norm, ret_norm, mlstm_norm, s5_d, s5_glu_b], axis=1).astype(F32)
    vec = jnp.pad(vec, ((0, 0), (0, SUBLANES - vec.shape[1]), (0, 0)))
    w_main, w_gate = _split_w_in(w_in)
    glu_w, wb, wo = s5_glu_w.astype(BF16), w_branch.astype(BF16), w_out.astype(BF16)
    w1, w2 = w_ff1.astype(BF16), w_ff2.astype(BF16)
    norm_mix3 = norm_mix.reshape(depth, 1, D_MODEL)
    norm_mlp3 = norm_mlp.reshape(depth, 1, D_MODEL)
    conv_w = mlstm_conv_w.reshape(depth, 9, 2 * BR)
    conv_b = mlstm_conv_b.reshape(depth, 1, 2 * BR)
    s5_col, s5_row = _s5_params(s5_a_re, s5_a_im, s5_log_dt, s5_b_re, s5_b_im, s5_c_re, s5_c_im)

    assert n_batch + 1 <= SUBLANES
    c_pad = jnp.zeros((SUBLANES, D_MODEL), F32).at[:n_batch].set(c).at[n_batch].set(c_ctx)
    x_all = jnp.concatenate([x.reshape(nl, D_MODEL), ctx.reshape(n_batch * ctx_len, D_MODEL)], axis=0)

    mod_all = _modulation(c_pad, w_mod, b_mod)[:, :n_batch + 1].reshape(depth, n_batch + 1, N_MOD, D_MODEL)
    mod_all = jnp.pad(mod_all, ((0, 0), (0, 0), (0, SUBLANES - N_MOD), (0, 0)))
    for l in range(depth):
        last = l == depth - 1
        mod = mod_all[l]
        p, gates, ut = _in_proj(x_all, mod, norm_mix3, w_main, w_gate, l, **dims)
        qk = _mlstm_conv(p, conv_w, conv_b, l, ctx_len=ctx_len, **dims)
        raw_a, raw_b, raw_d = _scans(p, gates, qk, lg[l], lower_bounds, gate_b, l, ctx_len=ctx_len, **dims)
        raw_c = _s5_scan(ut, s5_col, s5_row, l, ctx_len=ctx_len, **dims)
        n_rows = nl if last else x_all.shape[0]
        x_mid = _finish(raw_a, raw_b, raw_c, raw_d, p, x_all, mod, vec, glu_w, wb, wo, l, n_rows=n_rows, **dims)
        x_all = _mlp(x_mid, mod, norm_mlp3, w1, w2, final_norm, l, n_rows=n_rows, final_norm=last, **dims)
    return x_all.reshape(n_batch, seq, D_MODEL)
```

```python
import functools
import math

import jax
import jax.numpy as jnp
from jax import lax
from jax.experimental import pallas as pl
from jax.experimental.pallas import tpu as pltpu

F32 = jnp.float32
BF16 = jnp.bfloat16

D_MODEL = 2048
N_BRANCH = 4
BR = D_MODEL // N_BRANCH
HD = 128
NH = BR // HD
S5_HG = 16
S5_G = BR // S5_HG
S5_P = 64
D_FF = 4 * D_MODEL
N_MOD = 6
GRID_W = 64
NORM_EPS = 1e-6
NEG_BIG = -1e30
F_TINY = 1e-30

COL_MERGE = 0
COL_HQ, COL_HF, COL_HV, COL_HG = 16, 17, 19, 20
COL_RQ, COL_RK, COL_RV, COL_RG = 21, 22, 23, 24
COL_SU = 25
COL_MQK, COL_MV, COL_MZ = 26, 28, 29
P_WIDTH = 30 * BR
GATE_OFF = 14 * BR
GATE_PAD = 128
SUBLANES = 8

T_HGRN = 128
T_ATTN = 256
S5_T = 128
S5_PP = 128
S5_ROWS = 40
VMEM_LIMIT = 56 * 1024 * 1024

def _cparams(sem):
    return pltpu.CompilerParams(dimension_semantics=sem, vmem_limit_bytes=VMEM_LIMIT)


def _dot(a, b):
    return jnp.dot(a, b, preferred_element_type=F32)


def _dot_nt(a, b):
    return lax.dot_general(a, b, (((1,), (1,)), ((), ())), preferred_element_type=F32)


def _dot_tn(a, b):
    return lax.dot_general(a, b, (((0,), (0,)), ((), ())), preferred_element_type=F32)


def _split3(x):
    hi = x.astype(BF16)
    r1 = x - hi.astype(F32)
    mid = r1.astype(BF16)
    return hi, mid, (r1 - mid.astype(F32)).astype(BF16)


def _split2(x):
    hi = x.astype(BF16)
    return hi, (x - hi.astype(F32)).astype(BF16)


def _cumsum_rows(tri, x):
    hi, mid = _split2(x)
    return _dot(tri, hi) + _dot(tri, mid)


def _sigmoid(x):
    return 1.0 / (1.0 + jnp.exp(-x))


def _silu(x):
    return x * _sigmoid(x)


def _mod_kernel(c_ref, w_ref, b_ref, o_ref):
    @pl.when(pl.program_id(1) == 0)
    def _():
        o_ref[...] = jnp.broadcast_to(b_ref[...], o_ref.shape)

    a = _silu(c_ref[...]).astype(BF16)
    o_ref[...] += _dot(a, w_ref[...].astype(BF16))


def _modulation(c_pad, w_mod, b_mod, tk=256):
    depth, _, n = w_mod.shape
    return pl.pallas_call(
        _mod_kernel,
        out_shape=jax.ShapeDtypeStruct((depth, SUBLANES, n), F32),
        grid=(depth, D_MODEL // tk),
        in_specs=[pl.BlockSpec((SUBLANES, tk), lambda l, k: (0, k)),
                  pl.BlockSpec((None, tk, n), lambda l, k: (l, k, 0)),
                  pl.BlockSpec((None, 1, n), lambda l, k: (l, 0, 0))],
        out_specs=pl.BlockSpec((None, SUBLANES, n), lambda l, k: (l, 0, 0)),
        compiler_params=_cparams(("parallel", "arbitrary")),
        name="modulation",
    )(c_pad, w_mod, b_mod.reshape(depth, 1, n))


def _norm_mod(x, g, shift, scale):
    y = x * lax.rsqrt(jnp.mean(jnp.square(x), axis=-1, keepdims=True) + NORM_EPS)
    return y * (g * (1.0 + scale)) + shift


def _inproj_kernel(x_ref, mod_ref, g_ref, w_ref, wg_ref, p_ref, gate_ref, ut_ref, xn_ref, *, su_block, su_off):
    @pl.when(pl.program_id(1) == 0)
    def _():
        h = _norm_mod(x_ref[...], g_ref[...], mod_ref[0:1, :], mod_ref[1:2, :])
        xn_ref[...] = h.astype(BF16)
        gate_ref[...] = _dot_nt(xn_ref[...], wg_ref[...])

    acc = _dot_nt(xn_ref[...], w_ref[...])
    p_ref[...] = acc.astype(p_ref.dtype)

    @pl.when(pl.program_id(1) == su_block)
    def _():
        ut_ref[...] = acc[:, su_off:su_off + BR].T.astype(ut_ref.dtype)


def _mod_row_map(n_rows, tm, n_batch, seq):
    assert n_rows % tm == 0 and seq % tm == 0, (n_rows, seq, tm)
    n_lat_tiles, tiles_per_batch = n_batch * seq // tm, seq // tm

    def row(i):
        return jnp.where(i < n_lat_tiles, i // tiles_per_batch, n_batch)
    return row


def _in_proj(x_all, mod, norm_w, w_main, w_gate, layer, *, n_batch, seq, tm=1024, tn=1536):
    n = x_all.shape[0]
    assert P_WIDTH % tn == 0, tn
    su_block, su_off = divmod(COL_SU * BR, tn)
    assert su_off + BR <= tn, (su_off, tn)
    mrow = _mod_row_map(n, tm, n_batch, seq)
    return pl.pallas_call(
        functools.partial(_inproj_kernel, su_block=su_block, su_off=su_off),
        out_shape=(jax.ShapeDtypeStruct((n, P_WIDTH), BF16),
                   jax.ShapeDtypeStruct((n, GATE_PAD), F32),
                   jax.ShapeDtypeStruct((BR, n), BF16)),
        grid=(n // tm, P_WIDTH // tn),
        in_specs=[pl.BlockSpec((tm, D_MODEL), lambda i, j: (i, 0)),
                  pl.BlockSpec((None, SUBLANES, D_MODEL), lambda i, j: (mrow(i), 0, 0)),
                  pl.BlockSpec((None, 1, D_MODEL), lambda i, j: (layer, 0, 0)),
                  pl.BlockSpec((None, tn, D_MODEL), lambda i, j: (layer, j, 0)),
                  pl.BlockSpec((None, GATE_PAD, D_MODEL), lambda i, j: (layer, 0, 0))],
        out_specs=(pl.BlockSpec((tm, tn), lambda i, j: (i, j)),
                   pl.BlockSpec((tm, GATE_PAD), lambda i, j: (i, 0)),
                   pl.BlockSpec((BR, tm), lambda i, j: (0, i))),
        scratch_shapes=[pltpu.VMEM((tm, D_MODEL), BF16)],
        compiler_params=_cparams(("parallel", "arbitrary")),
        name="in_proj",
    )(x_all, mod, norm_w, w_main, w_gate)


CONV_ROWS = 1024


def _conv_kernel(up_ref, x_ref, dn_ref, w_ref, b_ref, o_ref, *, n_lat_blocks, blocks_per_img, ctx_len):
    rb = pl.program_id(0)
    x = x_ref[...].astype(F32)
    w = w_ref[...]
    n = CONV_ROWS

    @pl.when(rb < n_lat_blocks)
    def _():
        rr = rb % blocks_per_img
        up = jnp.where(rr > 0, up_ref[...].astype(F32), 0.0)
        dn = jnp.where(rr < blocks_per_img - 1, dn_ref[...].astype(F32), 0.0)
        ext = jnp.concatenate([up, x, dn], axis=0)
        ne = n + 2 * GRID_W
        col = lax.broadcasted_iota(jnp.int32, (ne, 1), 0) % GRID_W
        xl = jnp.where(col >= 1, pltpu.roll(ext, 1, 0), 0.0)
        xr = jnp.where(col <= GRID_W - 2, pltpu.roll(ext, ne - 1, 0), 0.0)
        acc = None
        for i in range(3):
            y = w[3 * i:3 * i + 1] * xl + w[3 * i + 1:3 * i + 2] * ext + w[3 * i + 2:3 * i + 3] * xr
            part = y[i * GRID_W:i * GRID_W + n]
            acc = part if acc is None else acc + part
        o_ref[...] = _silu(acc + b_ref[...]).astype(o_ref.dtype)

    @pl.when(rb >= n_lat_blocks)
    def _():
        t = lax.broadcasted_iota(jnp.int32, (n, 1), 0) % ctx_len
        xl = jnp.where(t >= 1, pltpu.roll(x, 1, 0), 0.0)
        xr = jnp.where(t <= ctx_len - 2, pltpu.roll(x, n - 1, 0), 0.0)
        acc = w[3:4] * xl + w[4:5] * x + w[5:6] * xr
        o_ref[...] = _silu(acc + b_ref[...]).astype(o_ref.dtype)


def _mlstm_conv(p, conv_w, conv_b, layer, *, n_batch, seq, ctx_len, cb=2 * BR):
    assert CONV_ROWS % ctx_len == 0 and seq % CONV_ROWS == 0 and (n_batch * ctx_len) % CONV_ROWS == 0
    n = p.shape[0]
    n_blocks = n // CONV_ROWS
    hb = CONV_ROWS // GRID_W
    n_halo = n // GRID_W
    c0 = COL_MQK * BR // cb
    kern = functools.partial(_conv_kernel, n_lat_blocks=n_batch * seq // CONV_ROWS,
                             blocks_per_img=seq // CONV_ROWS, ctx_len=ctx_len)
    return pl.pallas_call(
        kern,
        out_shape=jax.ShapeDtypeStruct((n, 2 * BR), BF16),
        grid=(n_blocks, 2 * BR // cb),
        in_specs=[pl.BlockSpec((GRID_W, cb), lambda r, c: (jnp.maximum(r * hb - 1, 0), c0 + c)),
                  pl.BlockSpec((CONV_ROWS, cb), lambda r, c: (r, c0 + c)),
                  pl.BlockSpec((GRID_W, cb), lambda r, c: (jnp.minimum((r + 1) * hb, n_halo - 1), c0 + c)),
                  pl.BlockSpec((None, 9, cb), lambda r, c: (layer, 0, c)),
                  pl.BlockSpec((None, 1, cb), lambda r, c: (layer, 0, c))],
        out_specs=pl.BlockSpec((CONV_ROWS, cb), lambda r, c: (r, c)),
        compiler_params=_cparams(("parallel", "parallel")),
        name="mlstm_conv",
    )(p, p, p, conv_w, conv_b)


def _chunk_block(b, d, i, *, n_ctx, n_lat, n_batch):
    ctx_j = i if d == 0 else n_ctx - 1 - i
    lat_j = i - n_ctx if d == 0 else n_ctx + n_lat - 1 - i
    return jnp.where(i < n_ctx, n_batch * n_lat + b * n_ctx + ctx_j, b * n_lat + lat_j)


def _init_at_first_step(*refs):
    @pl.when(pl.program_id(1) == 0)
    def _():
        for ref in refs:
            ref[...] = jnp.zeros_like(ref)


def _ret_direction(d, lg_ref, q_ref, k_ref, v_ref, o_ref, s_ref):
    t = T_ATTN
    r = lax.broadcasted_iota(jnp.int32, (t, t), 0)
    c = lax.broadcasted_iota(jnp.int32, (t, t), 1)
    rel = (r - c) if d == 0 else (c - r)
    relf = jnp.maximum(rel, 0).astype(F32)
    tt = lax.broadcasted_iota(jnp.int32, (t, 1), 0)
    pos = (tt if d == 0 else t - 1 - tt).astype(F32)
    scale = HD ** -0.5
    ln_scale = math.log(scale)
    for h in range(NH):
        sl = slice(h * HD, (h + 1) * HD)
        lg = lg_ref[d, h]
        q = q_ref[:, sl]
        k = k_ref[:, sl]
        v = v_ref[:, sl]
        intra = jnp.where(rel >= 0, jnp.exp(lg * relf + ln_scale), 0.0)
        scores = _dot_nt(q, k) * intra
        s_old = s_ref[d, h]
        out = _dot(scores.astype(BF16), v) + jnp.exp(lg * (pos + 1.0)) * _dot(q, s_old.astype(BF16))
        kd = (k.astype(F32) * (scale * jnp.exp(lg * (t - 1.0 - pos)))).astype(BF16)
        s_ref[d, h] = jnp.exp(lg * t) * s_old + _dot_tn(kd, v)
        o_ref[:, sl] = out


def _hgrn_direction(d, lb_ref, q_ref, z_ref, v_ref, o_ref, st_ref):
    t = T_HGRN
    fwd = d == 0
    lb = lb_ref[...]
    z = z_ref[...].astype(F32)
    sig = _sigmoid(z)
    f = lb + (1.0 - lb) * sig
    logf = jnp.log(jnp.maximum(f, F_TINY))
    kk = (1.0 - lb) * (1.0 - sig)
    qq = _silu(q_ref[...].astype(F32))
    r2 = lax.broadcasted_iota(jnp.int32, (t, t), 0)
    c2 = lax.broadcasted_iota(jnp.int32, (t, t), 1)
    tri = jnp.where((c2 <= r2) if fwd else (c2 >= r2), 1.0, 0.0).astype(BF16)
    cum = _cumsum_rows(tri, logf)
    xdiff = jnp.where((c2 < r2) if fwd else (c2 > r2), r2 ^ c2, 0)
    row = lax.broadcasted_iota(jnp.int32, (t, 1), 0)
    if fwd:
        e_ref = jnp.where(row == 0, 0.0, pltpu.roll(cum, 1, 0))
    else:
        e_ref = jnp.where(row == t - 1, 0.0, pltpu.roll(cum, t - 1, 0))
    f_ref = cum
    levels = []
    w = 1
    while w < t:
        if w == 1:
            q_l, k_l = (qq * jnp.maximum(f, F_TINY)).astype(BF16), kk.astype(BF16)
        else:
            q_l = (qq * jnp.exp(cum - e_ref)).astype(BF16)
            k_l = (kk * jnp.exp(f_ref - cum)).astype(BF16)
        levels.append((q_l, k_l, (xdiff >> (w.bit_length() - 1)) == 1))
        upper = (row & w) != 0
        if fwd:
            e_ref = jnp.where(upper, pltpu.roll(e_ref, w, 0), e_ref)
            f_ref = jnp.where(upper, f_ref, pltpu.roll(f_ref, t - w, 0))
        else:
            e_ref = jnp.where(upper, e_ref, pltpu.roll(e_ref, t - w, 0))
            f_ref = jnp.where(upper, pltpu.roll(f_ref, w, 0), f_ref)
        w *= 2
    q_b = qq.astype(BF16)
    k_b = kk.astype(BF16)
    tail = cum[t - 1:t] if fwd else cum[0:1]
    q_state = (qq * jnp.exp(cum)).astype(BF16)
    k_state = (kk * jnp.exp(tail - cum)).astype(BF16)
    decay = jnp.exp(tail)
    for h in range(NH):
        sl = slice(h * HD, (h + 1) * HD)
        v_h = v_ref[:, sl]
        scores = jnp.where(r2 == c2, _dot_nt(q_b[:, sl], k_b[:, sl]), 0.0)
        for q_l, k_l, m in levels:
            scores = jnp.where(m, _dot_nt(q_l[:, sl], k_l[:, sl]), scores)
        st = st_ref[d, h]
        o_ref[:, sl] = _dot(scores.astype(BF16), v_h) + _dot_nt(q_state[:, sl], st.astype(BF16))
        st_ref[d, h] = st * decay[:, sl] + _dot_tn(v_h, k_state[:, sl])


def _hgrn_kernel(lbf_ref, qf_ref, zf_ref, vf_ref, lbb_ref, qb_ref, zb_ref, vb_ref, of_ref, ob_ref, st_ref):
    _init_at_first_step(st_ref)
    _hgrn_direction(0, lbf_ref, qf_ref, zf_ref, vf_ref, of_ref, st_ref)
    _hgrn_direction(1, lbb_ref, qb_ref, zb_ref, vb_ref, ob_ref, st_ref)


def _log_sigmoid(x):
    return jnp.minimum(x, 0.0) - jnp.log(1.0 + jnp.exp(-jnp.abs(x)))


def _rep_lane(parts, lane, width):
    sel = jnp.where(lax.broadcasted_iota(jnp.int32, (GATE_PAD, width), 0) == lane, 1.0, 0.0).astype(BF16)
    hi, mid = parts
    return _dot(hi, sel) + _dot(mid, sel)


def _mlstm_direction(d, qk_ref, v_ref, g_ref, gb_ref, o_ref, c_ref, m_ref):
    t = T_ATTN
    assert t == 2 * HD
    fwd = d == 0
    g = g_ref[...] + gb_ref[...]
    r2 = lax.broadcasted_iota(jnp.int32, (t, t), 0)
    c2 = lax.broadcasted_iota(jnp.int32, (t, t), 1)
    mask = (c2 <= r2) if fwd else (c2 >= r2)
    tri = jnp.where(mask, 1.0, 0.0).astype(BF16)
    eye = jnp.where(r2 == c2, 1.0, 0.0).astype(BF16)
    lf = _log_sigmoid(g)
    cum_c = _cumsum_rows(tri, lf)
    a_col = g - pltpu.roll(cum_c, GATE_PAD - NH, 1)
    row = lax.broadcasted_iota(jnp.int32, (t, 1), 0)
    pm = a_col
    k = 1
    while k < t:
        if fwd:
            pm = jnp.maximum(pm, jnp.where(row >= k, pltpu.roll(pm, k, 0), NEG_BIG))
        else:
            pm = jnp.maximum(pm, jnp.where(row < t - k, pltpu.roll(pm, t - k, 0), NEG_BIG))
        k *= 2
    a_parts, pm_parts, cum_parts = _split2(a_col), _split2(pm), _split2(cum_c)
    a_rows = _dot_tn(a_parts[0], eye) + _dot_tn(a_parts[1], eye)
    scale = HD ** -0.5
    ln_scale = math.log(scale)
    last = t - 1 if fwd else 0
    ones = jnp.ones((t, HD), BF16)
    for h in range(NH):
        sl = slice(h * HD, (h + 1) * HD)
        li, lfw = 2 * NH * d + h, 2 * NH * d + NH + h
        m_old = m_ref[d, h]
        mx = jnp.maximum(_rep_lane(pm_parts, li, HD), m_old)
        mx2 = jnp.concatenate([mx, mx], axis=1)
        cc = _rep_lane(cum_parts, lfw, HD)
        a_rep = _rep_lane(a_parts, li, HD)
        a_row = a_rows[li:li + 1]
        q = qk_ref[:, sl]
        k_h = qk_ref[:, BR + h * HD:BR + (h + 1) * HD]
        v_aug = jnp.concatenate([v_ref[:, sl], ones], axis=1)
        w = jnp.where(mask, jnp.exp((a_row + ln_scale) - mx2), 0.0)
        scores = _dot_nt(q, k_h) * w
        w_state = jnp.exp(m_old - mx)
        c_old = c_ref[d, h]
        s_hi = scores.astype(BF16)
        s_mid = (scores - s_hi.astype(F32)).astype(BF16)
        intra = _dot(s_hi, v_aug)
        inter = _dot(q, c_old.astype(BF16))
        num = intra[:, 0:HD] + w_state * inter[:, 0:HD]
        den = (intra[:, HD:2 * HD] + _dot(s_mid, ones)) + w_state * inter[:, HD:2 * HD]
        o_ref[:, sl] = num / jnp.maximum(jnp.abs(den), jnp.exp(-(cc + mx)))
        total = cc[last:last + 1]
        m_new_rel = mx[last:last + 1]
        keep = jnp.exp(m_old - m_new_rel)
        w_end = jnp.exp(a_rep - m_new_rel)
        kw = (k_h.astype(F32) * (scale * w_end)).astype(BF16)
        c_ref[d, h] = jnp.concatenate([keep, keep], axis=1) * c_old + _dot_tn(kw, v_aug)
        m_ref[d, h] = total + m_new_rel


def _ret_mlstm_kernel(lg_ref, rqf_ref, rkf_ref, rvf_ref, rqb_ref, rkb_ref, rvb_ref,
                      qkf_ref, vf_ref, gf_ref, qkb_ref, vb_ref, gb_ref, bias_ref,
                      rof_ref, rob_ref, mof_ref, mob_ref, s_ref, c_ref, m_ref):
    _init_at_first_step(s_ref, c_ref, m_ref)
    _ret_direction(0, lg_ref, rqf_ref, rkf_ref, rvf_ref, rof_ref, s_ref)
    _mlstm_direction(0, qkf_ref, vf_ref, gf_ref, bias_ref, mof_ref, c_ref, m_ref)
    _ret_direction(1, lg_ref, rqb_ref, rkb_ref, rvb_ref, rob_ref, s_ref)
    _mlstm_direction(1, qkb_ref, vb_ref, gb_ref, bias_ref, mob_ref, c_ref, m_ref)


def _scans(p, gates, qk, lg, lb, gate_b, layer, *, n_batch, seq, ctx_len):
    n = p.shape[0]
    state = pltpu.VMEM((2, NH, HD, HD), F32)
    vec = pltpu.VMEM((2, NH, 1, HD), F32)

    def call(kernel, t, inputs, make_specs, scratch, name, n_mixers=1):
        assert ctx_len % t == 0 and seq % t == 0, (ctx_len, seq, t)
        n_ctx, n_lat = ctx_len // t, seq // t

        def rows(d, width, cblock=0):
            return pl.BlockSpec((t, width), lambda b, i: (_chunk_block(b, d, i, n_ctx=n_ctx, n_lat=n_lat,
                                                                        n_batch=n_batch), cblock))

        out = jax.ShapeDtypeStruct((n, BR), F32)
        return pl.pallas_call(
            kernel,
            out_shape=(out, out) * n_mixers,
            grid=(n_batch, n_ctx + n_lat),
            in_specs=make_specs(rows),
            out_specs=(rows(0, BR), rows(1, BR)) * n_mixers,
            scratch_shapes=scratch,
            compiler_params=_cparams(("parallel", "arbitrary")),
            name=name,
        )(*inputs)

    lb_spec = lambda d: pl.BlockSpec((None, None, 1, BR), lambda b, i: (layer, d, 0, 0))
    raw_a = call(_hgrn_kernel, T_HGRN, (lb, p, p, p, lb, p, p, p),
                 lambda rows: [spec for d in (0, 1)
                               for spec in (lb_spec(d), rows(d, BR, COL_HQ), rows(d, BR, COL_HF + d),
                                            rows(d, BR, COL_HV))],
                 [state], "hgrn2_scan")
    raw_bd = call(_ret_mlstm_kernel, T_ATTN, (lg, p, p, p, p, p, p, qk, p, gates, qk, p, gates, gate_b),
                  lambda rows: [pl.BlockSpec(memory_space=pltpu.SMEM)]
                  + [rows(d, BR, c) for d in (0, 1) for c in (COL_RQ, COL_RK, COL_RV)]
                  + [spec for d in (0, 1)
                     for spec in (rows(d, 2 * BR), rows(d, BR, COL_MV), rows(d, GATE_PAD))]
                  + [pl.BlockSpec((None, 1, GATE_PAD), lambda b, i: (layer, 0, 0))],
                  [state, pltpu.VMEM((2, NH, HD, 2 * HD), F32), vec], "retention_mlstm_scan", n_mixers=2)
    return raw_a, raw_bd[0:2], raw_bd[2:4]


def _cmul(ar, ai, br, bi):
    return ar * br - ai * bi, ar * bi + ai * br


def _pow_table(lam_r, lam_i, expo):
    pr = jnp.ones(expo.shape, F32)
    pi = jnp.zeros(expo.shape, F32)
    ar = jnp.broadcast_to(lam_r, expo.shape)
    ai = jnp.broadcast_to(lam_i, expo.shape)
    for k in range(S5_T.bit_length() - 1):
        nr, ni = _cmul(pr, pi, ar, ai)
        bit = (expo & (1 << k)) != 0
        pr, pi = jnp.where(bit, nr, pr), jnp.where(bit, ni, pi)
        ar, ai = _cmul(ar, ai, ar, ai)
    return pr, pi


def _s5_operators(d, u, col_ref, row_ref, m_ref, w_ref, ys_ref, z_ref, wr_ref, wi_ref):
    t = S5_T
    fwd = d == 0
    col = col_ref[d]
    rowp = row_ref[d]
    lam_rc, lam_ic = col[:, 0:1], col[:, 1:2]
    bbr, bbi = rowp[0:S5_HG], rowp[S5_HG:2 * S5_HG]
    lam_rr, lam_ir = rowp[2 * S5_HG:2 * S5_HG + 1], rowp[2 * S5_HG + 1:2 * S5_HG + 2]
    lane = lax.broadcasted_iota(jnp.int32, (t, t), 1)
    sub = lax.broadcasted_iota(jnp.int32, (t, t), 0)

    pr, pi = _pow_table(lam_rc, lam_ic, lane if fwd else t - 1 - lane)
    qr, qi = _cmul(pr, pi, lam_rc, lam_ic)
    for h in range(S5_HG):
        cr, ci = col[:, 2 + h:3 + h], col[:, 2 + S5_HG + h:3 + S5_HG + h]
        hs = slice(h * t, (h + 1) * t)
        z_ref[d, 0:S5_PP, hs] = cr * pr - ci * pi
        z_ref[d, S5_PP:2 * S5_PP, hs] = -(cr * pi + ci * pr)
        ys_ref[d, 0:S5_PP, hs] = (cr * qr - ci * qi).astype(BF16)
        ys_ref[d, S5_PP:2 * S5_PP, hs] = (-(cr * qi + ci * qr)).astype(BF16)
    b_hi, b_mid, _ = _split3(jnp.concatenate([bbr, bbi], axis=1))
    z_hi, z_mid, _ = _split3(z_ref[d])
    krow = _dot(b_hi, z_hi) + (_dot(b_hi, z_mid) + _dot(b_mid, z_hi))
    shift, keep = (0, lane >= sub) if fwd else (1, lane <= sub)
    for hp in range(S5_HG):
        for h in range(S5_HG):
            tile = jnp.broadcast_to(krow[hp:hp + 1, h * t:(h + 1) * t], (t, t))
            tile = pltpu.roll(tile, shift, 1, stride=1, stride_axis=0)
            m_ref[d, hp * t:(hp + 1) * t, h * t:(h + 1) * t] = jnp.where(keep, tile, 0.0).astype(BF16)
    tr, ti = _pow_table(lam_rr, lam_ir, t - 1 - sub if fwd else sub)
    for hp in range(S5_HG):
        br, bi = bbr[hp:hp + 1], bbi[hp:hp + 1]
        w_ref[d, hp * t:(hp + 1) * t, 0:S5_PP] = (tr * br - ti * bi).astype(BF16)
        w_ref[d, hp * t:(hp + 1) * t, S5_PP:2 * S5_PP] = (tr * bi + ti * br).astype(BF16)
    wv = _dot(u, w_ref[d])
    wr_ref[d] = wv[:, 0:S5_PP]
    wi_ref[d] = wv[:, S5_PP:2 * S5_PP]
    lr, li = lam_rr, lam_ir
    for _ in range(t.bit_length() - 1):
        lr, li = _cmul(lr, li, lr, li)
    return lr, li


def _s5_kernel(u_ref, col_ref, row_ref, o_ref, m_ref, w_ref, ys_ref, z_ref, wr_ref, wi_ref, xr_ref, xi_ref, *,
               n_batch, n_ctx, n_lat):
    t = S5_T
    u = jnp.concatenate([u_ref[hp] for hp in range(S5_HG)], axis=1)
    lam_t = [_s5_operators(d, u, col_ref, row_ref, m_ref, w_ref, ys_ref, z_ref, wr_ref, wi_ref) for d in (0, 1)]

    def make_step(base, stride, count):
        def step(i, carry):
            out = []
            for d in (0, 1):
                xr, xi = carry[2 * d], carry[2 * d + 1]
                j = i if d == 0 else count - 1 - i
                idx = pl.ds(base + j, n_batch, stride=stride)
                xr_ref[d, idx, :] = xr
                xi_ref[d, idx, :] = xi
                nr, ni = _cmul(xr, xi, *lam_t[d])
                out += [nr + wr_ref[d, idx, :], ni + wi_ref[d, idx, :]]
            return tuple(out)
        return step

    zero = jnp.zeros((n_batch, S5_PP), F32)
    carry = lax.fori_loop(0, n_ctx, make_step(n_batch * n_lat, n_ctx, n_ctx), (zero,) * 4)
    lax.fori_loop(0, n_lat, make_step(0, n_lat, n_lat), carry)
    y = None
    for d in (0, 1):
        x_prev = jnp.concatenate([xr_ref[d], xi_ref[d]], axis=1).astype(BF16)
        y_d = _dot(u, m_ref[d]) + _dot(x_prev, ys_ref[d])
        y = y_d if y is None else y + y_d
    for h in range(S5_HG):
        o_ref[h] = y[:, h * t:(h + 1) * t]


def _s5_params(a_re, a_im, log_dt, b_re, b_im, c_re, c_im):
    a_re, a_im = a_re.astype(F32), a_im.astype(F32)
    dt = jnp.exp(log_dt.astype(F32))[..., None]
    mag = jnp.exp(a_re * dt)
    lam_re, lam_im = mag * jnp.cos(a_im * dt), mag * jnp.sin(a_im * dt)
    den = a_re * a_re + a_im * a_im
    num_re, num_im = lam_re - 1.0, lam_im
    fr = (num_re * a_re + num_im * a_im) / den
    fi = (num_im * a_re - num_re * a_im) / den
    b_re, b_im = b_re.astype(F32), b_im.astype(F32)
    bb_re = fr[..., None] * b_re - fi[..., None] * b_im
    bb_im = fr[..., None] * b_im + fi[..., None] * b_re
    c_re_t = jnp.swapaxes(c_re.astype(F32), -1, -2)
    c_im_t = jnp.swapaxes(c_im.astype(F32), -1, -2)
    col = jnp.concatenate([lam_re[..., None], lam_im[..., None], c_re_t, c_im_t], axis=-1)
    col = jnp.pad(col, ((0, 0),) * 3 + ((0, S5_PP - S5_P), (0, 128 - col.shape[-1])))
    row = jnp.concatenate([jnp.swapaxes(bb_re, -1, -2), jnp.swapaxes(bb_im, -1, -2),
                           lam_re[..., None, :], lam_im[..., None, :]], axis=-2)
    row = jnp.pad(row, ((0, 0),) * 3 + ((0, S5_ROWS - row.shape[-2]), (0, S5_PP - S5_P)))
    return col, row


def _s5_scan(ut, col, row, layer, *, n_batch, seq, ctx_len):
    t = S5_T
    n = ut.shape[1]
    nc = n // t
    assert ctx_len % t == 0 and seq % t == 0, (ctx_len, seq, t)
    n_ctx, n_lat = ctx_len // t, seq // t
    ut = ut.reshape(S5_G, S5_HG, nc, t)
    kern = functools.partial(_s5_kernel, n_batch=n_batch, n_ctx=n_ctx, n_lat=n_lat)
    yt = pl.pallas_call(
        kern,
        out_shape=jax.ShapeDtypeStruct((S5_G, S5_HG, nc, t), F32),
        grid=(S5_G,),
        in_specs=[pl.BlockSpec((None, S5_HG, nc, t), lambda g: (g, 0, 0, 0)),
                  pl.BlockSpec((None, 2, None, S5_PP, 128), lambda g: (layer, 0, g, 0, 0)),
                  pl.BlockSpec((None, 2, None, S5_ROWS, S5_PP), lambda g: (layer, 0, g, 0, 0))],
        out_specs=pl.BlockSpec((None, S5_HG, nc, t), lambda g: (g, 0, 0, 0)),
        scratch_shapes=[pltpu.VMEM((2, S5_HG * t, S5_HG * t), BF16),
                        pltpu.VMEM((2, S5_HG * t, 2 * S5_PP), BF16),
                        pltpu.VMEM((2, 2 * S5_PP, S5_HG * t), BF16),
                        pltpu.VMEM((2, 2 * S5_PP, S5_HG * t), F32),
                        pltpu.VMEM((2, nc, S5_PP), F32), pltpu.VMEM((2, nc, S5_PP), F32),
                        pltpu.VMEM((2, nc, S5_PP), F32), pltpu.VMEM((2, nc, S5_PP), F32)],
        compiler_params=_cparams(("parallel",)),
        name="s5_scan",
    )(ut, col, row)
    return yt.reshape(BR, n)


def _head_rms(x, g):
    parts = []
    for h in range(NH):
        xh = x[:, h * HD:(h + 1) * HD]
        parts.append(xh * lax.rsqrt(jnp.mean(jnp.square(xh), axis=-1, keepdims=True) + NORM_EPS))
    return jnp.concatenate(parts, axis=-1) * g


def _finish_kernel(raf_ref, rab_ref, rbf_ref, rbb_ref, rc_ref, rdf_ref, rdb_ref, ga_ref, gb_ref, u_ref, gd_ref,
                   mg_ref, x_ref, mod_ref, vec_ref, glu_w_ref, wb_ref, wo_ref, o_ref):
    vec = vec_ref[...]
    oa = _head_rms(raf_ref[...] + rab_ref[...], vec[0:1]) * _silu(ga_ref[...].astype(F32))
    ob = _head_rms(rbf_ref[...] + rbb_ref[...], vec[1:2]) * _silu(gb_ref[...].astype(F32))
    od = _head_rms(rdf_ref[...] + rdb_ref[...], vec[2:3]) * _silu(gd_ref[...].astype(F32))
    yc = rc_ref[...].T + vec[3:4] * u_ref[...].astype(F32)
    yc = 0.5 * yc * (1.0 + lax.erf(yc * (2.0 ** -0.5)))
    oc = yc * _sigmoid(_dot(yc.astype(BF16), glu_w_ref[...]) + vec[4:5])
    y2 = None
    for j, o in enumerate((oa, ob, oc, od)):
        gate2 = jnp.tanh(0.5 * mg_ref[:, j * D_MODEL:(j + 1) * D_MODEL].astype(F32)) + 1.0
        term = gate2 * _dot(o.astype(BF16), wb_ref[j])
        y2 = term if y2 is None else y2 + term
    mix2 = _dot(y2.astype(BF16), wo_ref[...])
    o_ref[...] = x_ref[...] + (0.5 * mod_ref[2:3, :]) * mix2


def _finish(raw_a, raw_b, raw_c, raw_d, p, x_all, mod, vec, glu_w, w_branch, w_out, layer, *, n_rows, n_batch,
            seq, tm=256):
    mrow = _mod_row_map(n_rows, tm, n_batch, seq)
    one = pl.Buffered(1)

    def pcol(cblock):
        return pl.BlockSpec((tm, BR), lambda i: (i, cblock))

    raw = pl.BlockSpec((tm, BR), lambda i: (i, 0))
    return pl.pallas_call(
        _finish_kernel,
        out_shape=jax.ShapeDtypeStruct((n_rows, D_MODEL), F32),
        grid=(n_rows // tm,),
        in_specs=[raw, raw, raw, raw, pl.BlockSpec((BR, tm), lambda i: (0, i)), raw, raw,
                  pcol(COL_HG), pcol(COL_RG), pcol(COL_SU), pcol(COL_MZ),
                  pl.BlockSpec((tm, N_BRANCH * D_MODEL), lambda i: (i, COL_MERGE)),
                  pl.BlockSpec((tm, D_MODEL), lambda i: (i, 0)),
                  pl.BlockSpec((None, SUBLANES, D_MODEL), lambda i: (mrow(i), 0, 0)),
                  pl.BlockSpec((None, SUBLANES, BR), lambda i: (layer, 0, 0), pipeline_mode=one),
                  pl.BlockSpec((None, BR, BR), lambda i: (layer, 0, 0), pipeline_mode=one),
                  pl.BlockSpec((None, N_BRANCH, BR, D_MODEL), lambda i: (layer, 0, 0, 0), pipeline_mode=one),
                  pl.BlockSpec((None, D_MODEL, D_MODEL), lambda i: (layer, 0, 0), pipeline_mode=one)],
        out_specs=pl.BlockSpec((tm, D_MODEL), lambda i: (i, 0)),
        compiler_params=_cparams(("parallel",)),
        name="finish",
    )(*raw_a, *raw_b, raw_c, *raw_d, p, p, p, p, p, x_all, mod, vec, glu_w, w_branch, w_out)


def _mlp_kernel(x_ref, mod_ref, g_ref, w1_ref, w2_ref, fin_ref, o_ref, xn_ref, acc_ref, *, final_norm):
    j = pl.program_id(1)

    @pl.when(j == 0)
    def _():
        h = _norm_mod(x_ref[...], g_ref[...], mod_ref[3:4, :], mod_ref[4:5, :])
        xn_ref[...] = h.astype(BF16)
        acc_ref[...] = jnp.zeros_like(acc_ref)

    a = jnp.square(jnp.maximum(_dot(xn_ref[...], w1_ref[...]), 0.0))
    acc_ref[...] += _dot(a.astype(BF16), w2_ref[...])

    @pl.when(j == pl.num_programs(1) - 1)
    def _():
        y = x_ref[...] + mod_ref[5:6, :] * acc_ref[...]
        if final_norm:
            y = (y * lax.rsqrt(jnp.mean(jnp.square(y), axis=-1, keepdims=True) + NORM_EPS)) * fin_ref[...]
        o_ref[...] = y


def _mlp(x_all, mod, norm_w, w1, w2, fin_w, layer, *, n_rows, n_batch, seq, final_norm, tm=512, tf=1024):
    mrow = _mod_row_map(n_rows, tm, n_batch, seq)
    assert D_FF % tf == 0, tf
    kern = functools.partial(_mlp_kernel, final_norm=final_norm)
    return pl.pallas_call(
        kern,
        out_shape=jax.ShapeDtypeStruct((n_rows, D_MODEL), F32),
        grid=(n_rows // tm, D_FF // tf),
        in_specs=[pl.BlockSpec((tm, D_MODEL), lambda i, j: (i, 0)),
                  pl.BlockSpec((None, SUBLANES, D_MODEL), lambda i, j: (mrow(i), 0, 0)),
                  pl.BlockSpec((None, 1, D_MODEL), lambda i, j: (layer, 0, 0)),
                  pl.BlockSpec((None, D_MODEL, tf), lambda i, j: (layer, 0, j)),
                  pl.BlockSpec((None, tf, D_MODEL), lambda i, j: (layer, j, 0)),
                  pl.BlockSpec((1, D_MODEL), lambda i, j: (0, 0))],
        out_specs=pl.BlockSpec((tm, D_MODEL), lambda i, j: (i, 0)),
        scratch_shapes=[pltpu.VMEM((tm, D_MODEL), BF16), pltpu.VMEM((tm, D_MODEL), F32)],
        compiler_params=_cparams(("parallel", "arbitrary")),
        name="mlp",
    )(x_all, mod, norm_w, w1, w2, fin_w.reshape(1, D_MODEL))


W_PREP_ROWS = 1024
N_MERGE_BLK = N_BRANCH * D_MODEL // W_PREP_ROWS
MERGE_OFF = GATE_OFF + 4 * NH


def _w_prep_kernel(a_ref, g_ref, main_ref, gate_ref):
    @pl.when(pl.program_id(1) == 0)
    def _():
        gate_ref[...] = jnp.zeros_like(gate_ref)
        gate_ref[0:4 * NH, :] = g_ref[0].astype(BF16)

    main_ref[...] = a_ref[0].astype(BF16)


def _split_w_in(w):
    depth = w.shape[0]
    wt = jnp.swapaxes(w, 1, 2)
    tr = W_PREP_ROWS

    def a_map(l, j):
        row = jnp.where(j < N_MERGE_BLK, MERGE_OFF + j * tr, (j - N_MERGE_BLK) * tr)
        return l, pl.multiple_of(row, 16), 0

    return pl.pallas_call(
        _w_prep_kernel,
        out_shape=(jax.ShapeDtypeStruct((depth, P_WIDTH, D_MODEL), BF16),
                   jax.ShapeDtypeStruct((depth, GATE_PAD, D_MODEL), BF16)),
        grid=(depth, P_WIDTH // tr),
        in_specs=[pl.BlockSpec((pl.Element(1), pl.Element(tr), pl.Element(D_MODEL)), a_map),
                  pl.BlockSpec((pl.Element(1), pl.Element(4 * NH), pl.Element(D_MODEL)),
                               lambda l, j: (l, GATE_OFF, 0))],
        out_specs=(pl.BlockSpec((None, tr, D_MODEL), lambda l, j: (l, j, 0)),
                   pl.BlockSpec((None, GATE_PAD, D_MODEL), lambda l, j: (l, 0, 0))),
        compiler_params=_cparams(("parallel", "arbitrary")),
        name="w_in_prep",
    )(wt, wt)


def kernel(x, c, ctx, c_ctx, w_mod, b_mod, norm_mix, norm_mlp, w_in, hgrn_lb_logits, hgrn_norm, ret_decay, ret_norm, s5_a_re, s5_a_im, s5_log_dt, s5_b_re, s5_b_im, s5_c_re, s5_c_im, s5_d, s5_glu_w, s5_glu_b, mlstm_conv_w, mlstm_conv_b, mlstm_gate_b, mlstm_norm, w_branch, w_out, w_ff1, w_ff2, final_norm):
    n_batch, seq, _ = x.shape
    ctx_len = ctx.shape[1]
    depth = w_in.shape[0]
    nl = n_batch * seq
    dims = dict(n_batch=n_batch, seq=seq)

    p_lb = jax.nn.softmax(hgrn_lb_logits.astype(F32), axis=0)
    lower_bounds = (jnp.cumsum(p_lb, axis=0) - p_lb[0]).reshape(depth, 2, 1, BR)
    lg = jnp.log1p(-jnp.exp(ret_decay.astype(F32)))
    gate_b = jnp.pad(mlstm_gate_b.reshape(depth, 1, 4 * NH).astype(F32), ((0, 0), (0, 0), (0, GATE_PAD - 4 * NH)))
    vec = jnp.stack([hgrn_norm, ret_norm, mlstm_norm, s5_d, s5_glu_b], axis=1).astype(F32)
    vec = jnp.pad(vec, ((0, 0), (0, SUBLANES - vec.shape[1]), (0, 0)))
    w_main, w_gate = _split_w_in(w_in)
    glu_w, wb, wo = s5_glu_w.astype(BF16), w_branch.astype(BF16), w_out.astype(BF16)
    w1, w2 = w_ff1.astype(BF16), w_ff2.astype(BF16)
    norm_mix3 = norm_mix.reshape(depth, 1, D_MODEL)
    norm_mlp3 = norm_mlp.reshape(depth, 1, D_MODEL)
    conv_w = mlstm_conv_w.reshape(depth, 9, 2 * BR)
    conv_b = mlstm_conv_b.reshape(depth, 1, 2 * BR)
    s5_col, s5_row = _s5_params(s5_a_re, s5_a_im, s5_log_dt, s5_b_re, s5_b_im, s5_c_re, s5_c_im)

    assert n_batch + 1 <= SUBLANES
    c_pad = jnp.zeros((SUBLANES, D_MODEL), F32).at[:n_batch].set(c).at[n_batch].set(c_ctx)
    x_all = jnp.concatenate([x.reshape(nl, D_MODEL), ctx.reshape(n_batch * ctx_len, D_MODEL)], axis=0)

    mod_all = _modulation(c_pad, w_mod, b_mod)[:, :n_batch + 1].reshape(depth, n_batch + 1, N_MOD, D_MODEL)
    mod_all = jnp.pad(mod_all, ((0, 0), (0, 0), (0, SUBLANES - N_MOD), (0, 0)))
    for l in range(depth):
        last = l == depth - 1
        mod = mod_all[l]
        p, gates, ut = _in_proj(x_all, mod, norm_mix3, w_main, w_gate, l, **dims)
        qk = _mlstm_conv(p, conv_w, conv_b, l, ctx_len=ctx_len, **dims)
        raw_a, raw_b, raw_d = _scans(p, gates, qk, lg[l], lower_bounds, gate_b, l, ctx_len=ctx_len, **dims)
        raw_c = _s5_scan(ut, s5_col, s5_row, l, ctx_len=ctx_len, **dims)
        n_rows = nl if last else x_all.shape[0]
        x_mid = _finish(raw_a, raw_b, raw_c, raw_d, p, x_all, mod, vec, glu_w, wb, wo, l, n_rows=n_rows, **dims)
        x_all = _mlp(x_mid, mod, norm_mlp3, w1, w2, final_norm, l, n_rows=n_rows, final_norm=last, **dims)
    return x_all.reshape(n_batch, seq, D_MODEL)
```
